```python
import math
import jax
import jax.numpy as jnp
from jax import lax
import numpy as np

D_MODEL = 2048
BATCH = 1
SEQ = 8192
DEPTH = 2

GRID_W = 64
CTX_LEN = 256
N_MIXERS = 4
GROUP_W = D_MODEL // N_MIXERS
HY_ORDER = 2
HY_EMB = 33
HY_HIDDEN = 64
HY_FAST_DECAY = 0.3
HY_SLOW_DECAY = 1.5
HY_TARGET = 1e-2
RW_HEAD = 64
RW_HEADS = GROUP_W // RW_HEAD
RW_LORA = max(32, int(round(1.8 * math.sqrt(D_MODEL) / 32)) * 32)
RW_GN_EPS = 64e-5
RT_HEAD = 64
RT_HEADS = GROUP_W // RT_HEAD
RT_CHUNK = 128
ROPE_BASE = 10000.0
HG_HEAD = 128
HG_HEADS = GROUP_W // HG_HEAD
HG_CHUNK = 64
HG_MIN_GATE = 1e-30
FFN_HIDDEN = int(math.ceil(8 * D_MODEL / 3 / 256)) * 256
ALPHA = (2 * DEPTH) ** 0.25
BETA = (8 * DEPTH) ** -0.25
LN_EPS = 1e-6
HY_COLS = 3 * GROUP_W
RW_COLS = 4 * GROUP_W + 4 * RW_LORA
RT_COLS = 4 * GROUP_W
HG_COLS = 5 * GROUP_W
P_IN = HY_COLS + RW_COLS + RT_COLS + HG_COLS

kernel_name = "hybrid_dit_hyena_rwkv7_retnet_hgrn2"


def layer_norm(x, g=None, b=None):
    xf = x.astype(jnp.float32)
    mu = jnp.mean(xf, -1, keepdims=True)
    var = jnp.mean(jnp.square(xf - mu), -1, keepdims=True)
    y = (xf - mu) * lax.rsqrt(var + LN_EPS)
    if g is not None:
        y = y * g.astype(jnp.float32) + b.astype(jnp.float32)
    return y.astype(x.dtype)


def head_norm(x, n_heads, eps, rms=False):
    B, L, W = x.shape
    xf = x.astype(jnp.float32).reshape(B, L, n_heads, W // n_heads)
    if not rms:
        xf = xf - jnp.mean(xf, -1, keepdims=True)
    y = xf * lax.rsqrt(jnp.mean(jnp.square(xf), -1, keepdims=True) + eps)
    return y.reshape(B, L, W)


def modulate(x, shift, scale):
    return layer_norm(x) * (1.0 + scale) + shift


def deepnorm_update(x, y, gate, g, b):
    return layer_norm(ALPHA * x + gate * y, g, b)


def swiglu(h, w1, w3, w2):
    return (jax.nn.silu(h @ w1) * (h @ w3)) @ w2


def centred_conv3(u, w, b):
    up = jnp.pad(u, ((0, 0), (1, 1), (0, 0)))
    return up[:, :-2] * w[0] + up[:, 1:-1] * w[1] + up[:, 2:] * w[2] + b


def centred_shift(u):
    up = jnp.pad(u, ((0, 0), (1, 1), (0, 0)))
    return 0.5 * (up[:, :-2] + up[:, 2:])


def rope_2d(x, rows, cols):
    d_axis = x.shape[-1] // 2
    inv = ROPE_BASE ** (-jnp.arange(0, d_axis, 2, dtype=jnp.float32) / d_axis)

    def rot(xa, pos):
        ang = pos[:, None, None] * inv
        cos, sin = jnp.cos(ang), jnp.sin(ang)
        x1, x2 = jnp.split(xa, 2, axis=-1)
        return jnp.concatenate([x1 * cos - x2 * sin, x1 * sin + x2 * cos], -1)

    xr, xc = jnp.split(x.astype(jnp.float32), 2, axis=-1)
    return jnp.concatenate([rot(xr, rows), rot(xc, cols)], -1)


def flip_time(xs):
    return tuple(jnp.flip(a, axis=1) for a in xs)


def bidir_with_context(scan_fn, ctx_fwd, lat_fwd, ctx_bwd, lat_bwd, s0):
    oc_f, sc_f = scan_fn(ctx_fwd, s0)
    ol_f, _ = scan_fn(lat_fwd, sc_f)
    oc_b, sc_b = scan_fn(flip_time(ctx_bwd), s0)
    ol_b, _ = scan_fn(flip_time(lat_bwd), sc_b)
    return oc_f + jnp.flip(oc_b, axis=1), ol_f + jnp.flip(ol_b, axis=1)


def hyena_filters(L, w1, b1, w2, b2, w3, b3, freq):
    t = jnp.linspace(0.0, 1.0, L, dtype=jnp.float32)[:, None]
    n_bands = (HY_EMB - 1) // 2
    f = jnp.linspace(1e-4, n_bands - 1, n_bands, dtype=jnp.float32)[None, :]
    ang = (2.0 * math.pi / L) * jnp.arange(L, dtype=jnp.float32)[:, None] * f
    z = jnp.concatenate([t, jnp.cos(ang), -jnp.sin(ang)], -1)
    freq = freq.astype(jnp.float32)
    h = jnp.sin(freq * (z @ w1 + b1))
    h = jnp.sin(freq * (h @ w2 + b2))
    h = (h @ w3 + b3).astype(jnp.float32).reshape(L, HY_ORDER, 2, GROUP_W)
    max_decay = math.log(HY_TARGET) / HY_FAST_DECAY
    min_decay = math.log(HY_TARGET) / HY_SLOW_DECAY
    deltas = jnp.linspace(min_decay, max_decay, GROUP_W, dtype=jnp.float32)
    h = h * jnp.exp(-t * jnp.abs(deltas))[:, None, None, :]
    return h / jnp.sum(jnp.abs(h), axis=(0, 2), keepdims=True)


def bidir_fft_conv(u, h_pos, h_neg, bias):
    L = u.shape[1]
    h_full = jnp.concatenate([h_pos, jnp.zeros_like(h_pos[:1]), h_neg[:0:-1]], axis=0)
    uf = jnp.fft.rfft(u.astype(jnp.float32), n=2 * L, axis=1)
    hf = jnp.fft.rfft(h_full, n=2 * L, axis=0)
    y = jnp.fft.irfft(uf * hf[None], n=2 * L, axis=1)[:, :L]
    return y + u.astype(jnp.float32) * bias.astype(jnp.float32)


def hyena_mixer(zc, zl, conv_w, conv_b, w1, b1, w2, b2, w3, b3, freq, bias, with_ctx):
    def run(z):
        L = z.shape[1]
        u = centred_conv3(z, conv_w, conv_b)
        v, x1, x2 = jnp.split(u, 3, axis=-1)
        h = hyena_filters(L, w1, b1, w2, b2, w3, b3, freq)
        y = v
        for n, gate in enumerate((x1, x2)):
            y = gate * bidir_fft_conv(y, h[:, n, 0], h[:, n, 1], bias[n])
        return y.astype(z.dtype)
    return (run(zc) if with_ctx else None), run(zl)


def rwkv7_scan(inputs, s0):
    r, w, k, v, kk, a = (jnp.moveaxis(t.astype(jnp.float32), 1, 0) for t in inputs)

    def step(S, xs):
        r_t, w_t, k_t, v_t, kk_t, a_t = xs
        sa = jnp.einsum('bhvk,bhk->bhv', S, -kk_t)
        S = S * w_t[:, :, None, :] + sa[..., :, None] * (kk_t * a_t)[..., None, :] + v_t[..., :, None] * k_t[..., None, :]
        return S, jnp.einsum('bhvk,bhk->bhv', S, r_t)

    s_fin, o = lax.scan(step, s0, (r, w, k, v, kk, a))
    return jnp.moveaxis(o, 0, 1), s_fin


def rwkv7_mixer(zc, zl, mu, w0, w2, a0, a2, k_k, k_a, r_k, gn_g, gn_b, with_ctx):
    split_at = list(np.cumsum([GROUP_W] * 4 + [RW_LORA] * 3))

    def heads(t):
        return t.reshape(t.shape[0], t.shape[1], RW_HEADS, RW_HEAD)

    def prep(z):
        z = z + (centred_shift(z) - z) * mu
        r, k, v, g, wd_f, wd_b, ad_f, ad_b = jnp.split(z, split_at, axis=-1)
        kk = heads((k * k_k).astype(jnp.float32))
        kk = kk * lax.rsqrt(jnp.maximum(jnp.sum(jnp.square(kk), -1, keepdims=True), 1e-24))
        dirs = []
        for d, (wd, ad) in enumerate(((wd_f, ad_f), (wd_b, ad_b))):
            w_log = -jax.nn.softplus(-(w0[d] + jnp.tanh(wd) @ w2[d])) - 0.5
            decay = jnp.exp(-jnp.exp(w_log.astype(jnp.float32)))
            a = jax.nn.sigmoid(a0[d] + ad @ a2[d])
            k_d = k * (1.0 + (a - 1.0) * k_a)
            dirs.append((heads(r), heads(decay), heads(k_d), heads(v), kk, heads(a)))
        return dirs, (r, k, v, g)

    def readout(o, r, k, v, g):
        B, L = o.shape[:2]
        y = head_norm(o.reshape(B, L, GROUP_W), RW_HEADS, RW_GN_EPS) * gn_g + gn_b
        bonus = jnp.sum(heads(r * k) * r_k, -1, keepdims=True) * heads(v)
        return ((y + bonus.reshape(B, L, GROUP_W)) * jax.nn.sigmoid(g)).astype(g.dtype)

    (c_f, c_b), c_rkvg = prep(zc)
    (l_f, l_b), l_rkvg = prep(zl)
    s0 = jnp.zeros((zl.shape[0], RW_HEADS, RW_HEAD, RW_HEAD), jnp.float32)
    oc, ol = bidir_with_context(rwkv7_scan, c_f, l_f, c_b, l_b, s0)
    return (readout(oc, *c_rkvg) if with_ctx else None), readout(ol, *l_rkvg)


def retention_chunkwise(inputs, s0, log_gamma):
    q, k, v = (t.astype(jnp.float32) for t in inputs)
    B, L, H, D = q.shape
    N = L // RT_CHUNK
    q, k, v = (t.reshape(B, N, RT_CHUNK, H, D) for t in (q, k, v))
    idx = jnp.arange(RT_CHUNK, dtype=jnp.float32)
    rel = idx[:, None] - idx[None, :]
    decay = jnp.where(rel >= 0, jnp.exp(jnp.maximum(rel, 0.0)[None] * log_gamma[:, None, None]), 0.0)
    scores = jnp.einsum('bnihd,bnjhd->bnhij', q, k) * decay
    o = jnp.einsum('bnhij,bnjhd->bnihd', scores, v)
    k_w = jnp.exp((RT_CHUNK - 1 - idx)[:, None] * log_gamma)
    kv = jnp.einsum('bnjhd,bnjhe->nbhde', k * k_w[:, :, None], v)
    chunk_decay = jnp.exp(RT_CHUNK * log_gamma)[None, :, None, None]

    def step(S, kv_n):
        return chunk_decay * S + kv_n, S

    s_fin, s_prev = lax.scan(step, s0, kv)
    q_w = jnp.exp((idx + 1.0)[:, None] * log_gamma)
    o = o + jnp.einsum('bnihd,nbhde->bnihe', q * q_w[:, :, None], s_prev)
    return o.reshape(B, L, H, D), s_fin


def retention_mixer(zc, zl, rows, cols, with_ctx):
    log_gamma = jnp.log1p(-jnp.exp2(-5.0 - jnp.arange(RT_HEADS, dtype=jnp.float32)))

    def prep(z, rotate):
        B, L = z.shape[:2]
        q, k, v, g = jnp.split(z, 4, axis=-1)
        q, k, v = (t.reshape(B, L, RT_HEADS, RT_HEAD) for t in (q, k, v))
        if rotate:
            q, k = rope_2d(q, rows, cols), rope_2d(k, rows, cols)
        return (q, k * RT_HEAD ** -0.5, v), g

    def scan_fn(xs, s0):
        return retention_chunkwise(xs, s0, log_gamma)

    def readout(o, g):
        B, L = o.shape[:2]
        return (head_norm(o.reshape(B, L, GROUP_W), RT_HEADS, 1e-6) * jax.nn.silu(g)).astype(g.dtype)

    c_in, gc = prep(zc, False)
    l_in, gl = prep(zl, True)
    s0 = jnp.zeros((zl.shape[0], RT_HEADS, RT_HEAD, RT_HEAD), jnp.float32)
    oc, ol = bidir_with_context(scan_fn, c_in, l_in, c_in, l_in, s0)
    return (readout(oc, gc) if with_ctx else None), readout(ol, gl)


def gla_chunkwise(inputs, s0):
    q, log_f, k, v = (t.astype(jnp.float32) for t in inputs)
    B, L, H, _ = q.shape
    N = L // HG_CHUNK

    def chunks(t):
        return t.reshape(B, N, HG_CHUNK, H, t.shape[-1]).transpose(1, 0, 3, 2, 4)

    causal = jnp.tril(jnp.ones((HG_CHUNK, HG_CHUNK), bool))[:, :, None]

    def step(S, xs):
        q_c, lf_c, k_c, v_c = xs
        b = jnp.cumsum(lf_c, axis=2)
        diff = b[:, :, :, None, :] - b[:, :, None, :, :]
        dec = jnp.where(causal, jnp.exp(jnp.where(causal, diff, 0.0)), 0.0)
        A = jnp.einsum('bhtd,bhsd,bhtsd->bhts', q_c, k_c, dec)
        o = jnp.einsum('bhts,bhsv->bhtv', A, v_c) + jnp.einsum('bhtd,bhdv->bhtv', q_c * jnp.exp(b), S)
        b_last = b[:, :, -1:, :]
        S = jnp.exp(b_last[:, :, 0, :])[..., None] * S + jnp.einsum('bhsd,bhsv->bhdv', k_c * jnp.exp(b_last - b), v_c)
        return S, o

    s_fin, o = lax.scan(step, s0, tuple(chunks(t) for t in (q, log_f, k, v)))
    return o.transpose(1, 0, 3, 2, 4).reshape(B, L, H, v.shape[-1]), s_fin


def hgrn2_mixer(zc, zl, lb_f, lb_b, norm_g, with_ctx):
    def heads(t):
        return t.reshape(t.shape[0], t.shape[1], HG_HEADS, HG_HEAD)

    def prep(z):
        q, f_f, f_b, i, g = jnp.split(z, 5, axis=-1)
        q, i = heads(jax.nn.silu(q)), heads(i)
        dirs = []
        for f, lb in ((f_f, lb_f), (f_b, lb_b)):
            gate = lb + (1.0 - lb) * jax.nn.sigmoid(f.astype(jnp.float32))
            log_f = jnp.log(jnp.maximum(gate, HG_MIN_GATE))
            dirs.append((q, heads(log_f), heads(1.0 - gate), i))
        return dirs, g

    def readout(o, g):
        B, L = o.shape[:2]
        return (head_norm(o.reshape(B, L, GROUP_W), HG_HEADS, 1e-6, rms=True) * norm_g * jax.nn.silu(g)).astype(g.dtype)

    (c_f, c_b), gc = prep(zc)
    (l_f, l_b), gl = prep(zl)
    s0 = jnp.zeros((zl.shape[0], HG_HEADS, HG_HEAD, HG_HEAD), jnp.float32)
    oc, ol = bidir_with_context(gla_chunkwise, c_f, l_f, c_b, l_b, s0)
    return (readout(oc, gc) if with_ctx else None), readout(ol, gl)


def setup_inputs(seed: int = 0) -> dict:
    key = jax.random.key(seed)
    keys = jax.random.split(key, 40)
    counter = iter(range(40))

    def nrm(shape, scale):
        return jax.random.normal(keys[next(counter)], shape, jnp.float32) * scale

    D, F = D_MODEL, FFN_HIDDEN
    decay_speed = jnp.linspace(-6.5, -1.5, GROUP_W, dtype=jnp.float32)
    return {
        "x": nrm((BATCH, SEQ, D), 1.0),
        "c": nrm((BATCH, D), 1.0),
        "ctx": nrm((BATCH, CTX_LEN, D), 1.0),
        "c_ctx": nrm((D,), 1.0),
        "ada_w": nrm((DEPTH, D, 6 * D), D ** -0.5),
        "ada_b": nrm((DEPTH, 6 * D), 0.02),
        "w_in": nrm((DEPTH, D, P_IN), D ** -0.5),
        "w_out": nrm((DEPTH, D, D), BETA * D ** -0.5),
        "ln_g": 1.0 + nrm((DEPTH, 2, D), 0.02),
        "ln_b": nrm((DEPTH, 2, D), 0.02),
        "hy_conv_w": nrm((DEPTH, 3, HY_COLS), 3 ** -0.5),
        "hy_conv_b": nrm((DEPTH, HY_COLS), 0.02),
        "hy_w1": nrm((DEPTH, HY_EMB, HY_HIDDEN), HY_EMB ** -0.5),
        "hy_b1": nrm((DEPTH, HY_HIDDEN), 0.5),
        "hy_w2": nrm((DEPTH, HY_HIDDEN, HY_HIDDEN), HY_HIDDEN ** -0.5),
        "hy_b2": nrm((DEPTH, HY_HIDDEN), 0.5),
        "hy_w3": nrm((DEPTH, HY_HIDDEN, HY_ORDER * 2 * GROUP_W), HY_HIDDEN ** -0.5),
        "hy_b3": nrm((DEPTH, HY_ORDER * 2 * GROUP_W), 0.02),
        "hy_freq": 1.0 + nrm((DEPTH, HY_HIDDEN), 0.1),
        "hy_bias": nrm((DEPTH, HY_ORDER, GROUP_W), 1.0),
        "rw_mu": jax.random.uniform(keys[next(counter)], (DEPTH, RW_COLS), jnp.float32),
        "rw_w0": decay_speed + nrm((DEPTH, 2, GROUP_W), 0.1),
        "rw_w2": nrm((DEPTH, 2, RW_LORA, GROUP_W), 0.1 * RW_LORA ** -0.5),
        "rw_a0": nrm((DEPTH, 2, GROUP_W), 0.1),
        "rw_a2": nrm((DEPTH, 2, RW_LORA, GROUP_W), 0.1 * RW_LORA ** -0.5),
        "rw_k_k": 0.85 + nrm((DEPTH, GROUP_W), 0.02),
        "rw_k_a": 1.0 + nrm((DEPTH, GROUP_W), 0.02),
        "rw_r_k": nrm((DEPTH, RW_HEADS, RW_HEAD), 0.1),
        "rw_gn_g": 1.0 + nrm((DEPTH, GROUP_W), 0.02),
        "rw_gn_b": nrm((DEPTH, GROUP_W), 0.02),
        "hg_lb_raw": nrm((2, DEPTH, GROUP_W), 0.5),
        "hg_norm_g": 1.0 + nrm((DEPTH, GROUP_W), 0.02),
        "ffn_w1": nrm((DEPTH, D, F), D ** -0.5),
        "ffn_w3": nrm((DEPTH, D, F), D ** -0.5),
        "ffn_w2": nrm((DEPTH, F, D), BETA * F ** -0.5),
    }


def reference(x, c, ctx, c_ctx, ada_w, ada_b, w_in, w_out, ln_g, ln_b,
              hy_conv_w, hy_conv_b, hy_w1, hy_b1, hy_w2, hy_b2, hy_w3, hy_b3, hy_freq, hy_bias,
              rw_mu, rw_w0, rw_w2, rw_a0, rw_a2, rw_k_k, rw_k_a, rw_r_k, rw_gn_g, rw_gn_b,
              hg_lb_raw, hg_norm_g, ffn_w1, ffn_w3, ffn_w2):
    L = x.shape[1]
    ROWS = L // GRID_W
    rows = jnp.repeat(jnp.arange(ROWS, dtype=jnp.float32), GRID_W)
    cols = jnp.tile(jnp.arange(GRID_W, dtype=jnp.float32), ROWS)
    sm = jax.nn.softmax(hg_lb_raw.astype(jnp.float32), axis=1)
    lower_bounds = jnp.cumsum(sm, axis=1) - sm[:, :1]
    split_at = list(np.cumsum([HY_COLS, RW_COLS, RT_COLS]))

    xl, xc = x, ctx
    for l in range(DEPTH):
        with_ctx = l < DEPTH - 1
        mod_l = (jax.nn.silu(c) @ ada_w[l] + ada_b[l])[:, None, :]
        mod_c = (jax.nn.silu(c_ctx) @ ada_w[l] + ada_b[l])[None, None, :]
        sh1_l, sc1_l, g1_l, sh2_l, sc2_l, g2_l = jnp.split(mod_l, 6, axis=-1)
        sh1_c, sc1_c, g1_c, sh2_c, sc2_c, g2_c = jnp.split(mod_c, 6, axis=-1)

        zl = modulate(xl, sh1_l, sc1_l) @ w_in[l]
        zc = modulate(xc, sh1_c, sc1_c) @ w_in[l]
        zl_hy, zl_rw, zl_rt, zl_hg = jnp.split(zl, split_at, axis=-1)
        zc_hy, zc_rw, zc_rt, zc_hg = jnp.split(zc, split_at, axis=-1)

        oc_hy, ol_hy = hyena_mixer(zc_hy, zl_hy, hy_conv_w[l], hy_conv_b[l], hy_w1[l], hy_b1[l], hy_w2[l], hy_b2[l],
                                   hy_w3[l], hy_b3[l], hy_freq[l], hy_bias[l], with_ctx)
        oc_rw, ol_rw = rwkv7_mixer(zc_rw, zl_rw, rw_mu[l], rw_w0[l], rw_w2[l], rw_a0[l], rw_a2[l], rw_k_k[l],
                                   rw_k_a[l], rw_r_k[l], rw_gn_g[l], rw_gn_b[l], with_ctx)
        oc_rt, ol_rt = retention_mixer(zc_rt, zl_rt, rows, cols, with_ctx)
        oc_hg, ol_hg = hgrn2_mixer(zc_hg, zl_hg, lower_bounds[0, l], lower_bounds[1, l], hg_norm_g[l], with_ctx)

        yl = jnp.concatenate([ol_hy, ol_rw, ol_rt, ol_hg], axis=-1) @ w_out[l]
        xl = deepnorm_update(xl, yl, g1_l, ln_g[l, 0], ln_b[l, 0])
        xl = deepnorm_update(xl, swiglu(modulate(xl, sh2_l, sc2_l), ffn_w1[l], ffn_w3[l], ffn_w2[l]),
                             g2_l, ln_g[l, 1], ln_b[l, 1])
        if with_ctx:
            yc = jnp.concatenate([oc_hy, oc_rw, oc_rt, oc_hg], axis=-1) @ w_out[l]
            xc = deepnorm_update(xc, yc, g1_c, ln_g[l, 0], ln_b[l, 0])
            xc = deepnorm_update(xc, swiglu(modulate(xc, sh2_c, sc2_c), ffn_w1[l], ffn_w3[l], ffn_w2[l]),
                                 g2_c, ln_g[l, 1], ln_b[l, 1])
    return xl
```

```python
import functools
import math

import jax
import jax.numpy as jnp
import numpy as np
from jax import lax
from jax.experimental import pallas as pl
from jax.experimental.pallas import tpu as pltpu

D_MODEL = 2048
DEPTH = 2
GRID_W = 64
N_MIXERS = 4
GROUP_W = D_MODEL // N_MIXERS
HY_ORDER = 2
HY_EMB = 33
HY_FAST_DECAY = 0.3
HY_SLOW_DECAY = 1.5
HY_TARGET = 1e-2
RW_HEAD = 64
RW_HEADS = GROUP_W // RW_HEAD
RW_LORA = 96
RW_GN_EPS = 64e-5
RT_HEAD = 64
RT_HEADS = GROUP_W // RT_HEAD
RT_CHUNK = 128
ROPE_BASE = 10000.0
HG_HEAD = 128
HG_HEADS = GROUP_W // HG_HEAD
HG_CHUNK = 64
HG_MIN_GATE = 1e-30
FFN_HIDDEN = 5632
ALPHA = (2 * DEPTH) ** 0.25
LN_EPS = 1e-6
HY_COLS = 3 * GROUP_W
RW_COLS = 4 * GROUP_W + 4 * RW_LORA
RT_COLS = 4 * GROUP_W
HG_COLS = 5 * GROUP_W
P_IN = HY_COLS + RW_COLS + RT_COLS + HG_COLS

LANES = 128
P_IN_PAD = 8704
PROJ_TN = 512
FFN_TF = 512
VMEM_LIMIT = 56 * 1024 * 1024


def _row_tile(m):
    return 512 if m % 512 == 0 else m


def _ln_rows(x):
    mu = jnp.mean(x, axis=-1, keepdims=True)
    xc = x - mu
    var = jnp.mean(xc * xc, axis=-1, keepdims=True)
    return xc * lax.rsqrt(var + LN_EPS)


def _ada_kernel(c_ref, w_ref, b_ref, o_ref):
    c = c_ref[...]
    h = c * jax.nn.sigmoid(c)
    o_ref[...] = jnp.dot(h.astype(jnp.bfloat16), w_ref[...].astype(jnp.bfloat16),
                         preferred_element_type=jnp.float32) + b_ref[...]


def ada_modulation(c8, w, b):
    n = w.shape[1]
    tn = 1024
    return pl.pallas_call(
        _ada_kernel,
        grid=(n // tn,),
        in_specs=[pl.BlockSpec((8, D_MODEL), lambda j: (0, 0)),
                  pl.BlockSpec((D_MODEL, tn), lambda j: (0, j)),
                  pl.BlockSpec((1, tn), lambda j: (0, j))],
        out_specs=pl.BlockSpec((8, tn), lambda j: (0, j)),
        out_shape=jax.ShapeDtypeStruct((8, n), jnp.float32),
        compiler_params=pltpu.CompilerParams(dimension_semantics=("arbitrary",),
                                             vmem_limit_bytes=VMEM_LIMIT),
    )(c8, w, b)


def _proj_kernel(x_ref, sh_ref, sc_ref, w_ref, o_ref, h_ref):
    @pl.when(pl.program_id(1) == 0)
    def _():
        h = _ln_rows(x_ref[...]) * (1.0 + sc_ref[...]) + sh_ref[...]
        h_ref[...] = h.astype(jnp.bfloat16)

    o_ref[...] = jnp.dot(h_ref[...], w_ref[...], preferred_element_type=jnp.float32)


def modulated_projection(x, shift, scale, w_bf16):
    m = x.shape[0]
    n = w_bf16.shape[1]
    tm = _row_tile(m)
    return pl.pallas_call(
        _proj_kernel,
        grid=(m // tm, n // PROJ_TN),
        in_specs=[pl.BlockSpec((tm, D_MODEL), lambda i, j: (i, 0)),
                  pl.BlockSpec((1, D_MODEL), lambda i, j: (0, 0)),
                  pl.BlockSpec((1, D_MODEL), lambda i, j: (0, 0)),
                  pl.BlockSpec((D_MODEL, PROJ_TN), lambda i, j: (0, j))],
        out_specs=pl.BlockSpec((tm, PROJ_TN), lambda i, j: (i, j)),
        out_shape=jax.ShapeDtypeStruct((m, n), jnp.float32),
        scratch_shapes=[pltpu.VMEM((tm, D_MODEL), jnp.bfloat16)],
        compiler_params=pltpu.CompilerParams(dimension_semantics=("arbitrary", "arbitrary"),
                                             vmem_limit_bytes=VMEM_LIMIT),
    )(x, shift, scale, w_bf16)


def _outproj_kernel(y_ref, w_ref, x_ref, gate_ref, g_ref, b_ref, o_ref):
    y = jnp.dot(y_ref[...].astype(jnp.bfloat16), w_ref[...], preferred_element_type=jnp.float32)
    r = ALPHA * x_ref[...] + gate_ref[...] * y
    o_ref[...] = _ln_rows(r) * g_ref[...] + b_ref[...]


def outproj_deepnorm(y, w_bf16, x, gate, g, b):
    m = x.shape[0]
    tm = _row_tile(m)
    row = pl.BlockSpec((tm, D_MODEL), lambda i: (i, 0))
    vec = pl.BlockSpec((1, D_MODEL), lambda i: (0, 0))
    return pl.pallas_call(
        _outproj_kernel,
        grid=(m // tm,),
        in_specs=[row, pl.BlockSpec((D_MODEL, D_MODEL), lambda i: (0, 0)), row, vec, vec, vec],
        out_specs=row,
        out_shape=jax.ShapeDtypeStruct((m, D_MODEL), jnp.float32),
        compiler_params=pltpu.CompilerParams(dimension_semantics=("arbitrary",),
                                             vmem_limit_bytes=VMEM_LIMIT),
    )(y, w_bf16, x, gate, g, b)


def _ffn_kernel(x_ref, sh_ref, sc_ref, w1_ref, w3_ref, w2_ref, gate_ref, g_ref, b_ref, o_ref, h_ref, acc_ref):
    j = pl.program_id(1)

    @pl.when(j == 0)
    def _():
        h = _ln_rows(x_ref[...]) * (1.0 + sc_ref[...]) + sh_ref[...]
        h_ref[...] = h.astype(jnp.bfloat16)
        acc_ref[...] = jnp.zeros_like(acc_ref)

    h = h_ref[...]
    a = jnp.dot(h, w1_ref[...], preferred_element_type=jnp.float32)
    u = jnp.dot(h, w3_ref[...], preferred_element_type=jnp.float32)
    s = (a * jax.nn.sigmoid(a) * u).astype(jnp.bfloat16)
    acc_ref[...] += jnp.dot(s, w2_ref[...], preferred_element_type=jnp.float32)

    @pl.when(j == pl.num_programs(1) - 1)
    def _():
        r = ALPHA * x_ref[...] + gate_ref[...] * acc_ref[...]
        o_ref[...] = _ln_rows(r) * g_ref[...] + b_ref[...]


def ffn_deepnorm(x, shift, scale, w1, w3, w2, gate, g, b):
    m = x.shape[0]
    tm = _row_tile(m)
    row = pl.BlockSpec((tm, D_MODEL), lambda i, j: (i, 0))
    vec = pl.BlockSpec((1, D_MODEL), lambda i, j: (0, 0))
    return pl.pallas_call(
        _ffn_kernel,
        grid=(m // tm, FFN_HIDDEN // FFN_TF),
        in_specs=[row, vec, vec,
                  pl.BlockSpec((D_MODEL, FFN_TF), lambda i, j: (0, j)),
                  pl.BlockSpec((D_MODEL, FFN_TF), lambda i, j: (0, j)),
                  pl.BlockSpec((FFN_TF, D_MODEL), lambda i, j: (j, 0)),
                  vec, vec, vec],
        out_specs=row,
        out_shape=jax.ShapeDtypeStruct((m, D_MODEL), jnp.float32),
        scratch_shapes=[pltpu.VMEM((tm, D_MODEL), jnp.bfloat16),
                        pltpu.VMEM((tm, D_MODEL), jnp.float32)],
        compiler_params=pltpu.CompilerParams(dimension_semantics=("arbitrary", "arbitrary"),
                                             vmem_limit_bytes=VMEM_LIMIT),
    )(x, shift, scale, w1, w3, w2, gate, g, b)


def _head_norm(x, n_heads, eps, rms=False):
    B, L, W = x.shape
    xf = x.astype(jnp.float32).reshape(B, L, n_heads, W // n_heads)
    if not rms:
        xf = xf - jnp.mean(xf, -1, keepdims=True)
    y = xf * lax.rsqrt(jnp.mean(jnp.square(xf), -1, keepdims=True) + eps)
    return y.reshape(B, L, W)


def _centred_conv3(u, w, b):
    up = jnp.pad(u, ((0, 0), (1, 1), (0, 0)))
    return up[:, :-2] * w[0] + up[:, 1:-1] * w[1] + up[:, 2:] * w[2] + b


def _centred_shift(u):
    up = jnp.pad(u, ((0, 0), (1, 1), (0, 0)))
    return 0.5 * (up[:, :-2] + up[:, 2:])


def _rope_2d(x, rows, cols):
    d_axis = x.shape[-1] // 2
    inv = ROPE_BASE ** (-jnp.arange(0, d_axis, 2, dtype=jnp.float32) / d_axis)

    def rot(xa, pos):
        ang = pos[:, None, None] * inv
        cos, sin = jnp.cos(ang), jnp.sin(ang)
        x1, x2 = jnp.split(xa, 2, axis=-1)
        return jnp.concatenate([x1 * cos - x2 * sin, x1 * sin + x2 * cos], -1)

    xr, xc = jnp.split(x.astype(jnp.float32), 2, axis=-1)
    return jnp.concatenate([rot(xr, rows), rot(xc, cols)], -1)


def _flip_time(xs):
    return tuple(jnp.flip(a, axis=1) for a in xs)


def _bidir_with_context(scan_fn, ctx_fwd, lat_fwd, ctx_bwd, lat_bwd, s0):
    oc_f, sc_f = scan_fn(ctx_fwd, s0)
    ol_f, _ = scan_fn(lat_fwd, sc_f)
    oc_b, sc_b = scan_fn(_flip_time(ctx_bwd), s0)
    ol_b, _ = scan_fn(_flip_time(lat_bwd), sc_b)
    return oc_f + jnp.flip(oc_b, axis=1), ol_f + jnp.flip(ol_b, axis=1)


def _hyena_filters(L, w1, b1, w2, b2, w3, b3, freq):
    t = jnp.linspace(0.0, 1.0, L, dtype=jnp.float32)[:, None]
    n_bands = (HY_EMB - 1) // 2
    f = jnp.linspace(1e-4, n_bands - 1, n_bands, dtype=jnp.float32)[None, :]
    ang = (2.0 * math.pi / L) * jnp.arange(L, dtype=jnp.float32)[:, None] * f
    z = jnp.concatenate([t, jnp.cos(ang), -jnp.sin(ang)], -1)
    freq = freq.astype(jnp.float32)
    h = jnp.sin(freq * (z @ w1 + b1))
    h = jnp.sin(freq * (h @ w2 + b2))
    h = (h @ w3 + b3).astype(jnp.float32).reshape(L, HY_ORDER, 2, GROUP_W)
    max_decay = math.log(HY_TARGET) / HY_FAST_DECAY
    min_decay = math.log(HY_TARGET) / HY_SLOW_DECAY
    deltas = jnp.linspace(min_decay, max_decay, GROUP_W, dtype=jnp.float32)
    h = h * jnp.exp(-t * jnp.abs(deltas))[:, None, None, :]
    return h / jnp.sum(jnp.abs(h), axis=(0, 2), keepdims=True)


def _bidir_fft_conv(u, h_pos, h_neg, bias):
    L = u.shape[1]
    h_full = jnp.concatenate([h_pos, jnp.zeros_like(h_pos[:1]), h_neg[:0:-1]], axis=0)
    uf = jnp.fft.rfft(u.astype(jnp.float32), n=2 * L, axis=1)
    hf = jnp.fft.rfft(h_full, n=2 * L, axis=0)
    y = jnp.fft.irfft(uf * hf[None], n=2 * L, axis=1)[:, :L]
    return y + u.astype(jnp.float32) * bias.astype(jnp.float32)


def _hyena_mixer(zc, zl, conv_w, conv_b, w1, b1, w2, b2, w3, b3, freq, bias, with_ctx):
    def run(z):
        L = z.shape[1]
        u = _centred_conv3(z, conv_w, conv_b)
        v, x1, x2 = jnp.split(u, 3, axis=-1)
        h = _hyena_filters(L, w1, b1, w2, b2, w3, b3, freq)
        y = v
        for n, gate in enumerate((x1, x2)):
            y = gate * _bidir_fft_conv(y, h[:, n, 0], h[:, n, 1], bias[n])
        return y.astype(z.dtype)
    return (run(zc) if with_ctx else None), run(zl)


def _rwkv7_scan(inputs, s0):
    r, w, k, v, kk, a = (jnp.moveaxis(t.astype(jnp.float32), 1, 0) for t in inputs)

    def step(S, xs):
        r_t, w_t, k_t, v_t, kk_t, a_t = xs
        sa = jnp.einsum('bhvk,bhk->bhv', S, -kk_t)
        S = S * w_t[:, :, None, :] + sa[..., :, None] * (kk_t * a_t)[..., None, :] + v_t[..., :, None] * k_t[..., None, :]
        return S, jnp.einsum('bhvk,bhk->bhv', S, r_t)

    s_fin, o = lax.scan(step, s0, (r, w, k, v, kk, a))
    return jnp.moveaxis(o, 0, 1), s_fin


def _rwkv7_mixer(zc, zl, mu, w0, w2, a0, a2, k_k, k_a, r_k, gn_g, gn_b, with_ctx):
    split_at = list(np.cumsum([GROUP_W] * 4 + [RW_LORA] * 3))

    def heads(t):
        return t.reshape(t.shape[0], t.shape[1], RW_HEADS, RW_HEAD)

    def prep(z):
        z = z + (_centred_shift(z) - z) * mu
        r, k, v, g, wd_f, wd_b, ad_f, ad_b = jnp.split(z, split_at, axis=-1)
        kk = heads((k * k_k).astype(jnp.float32))
        kk = kk * lax.rsqrt(jnp.maximum(jnp.sum(jnp.square(kk), -1, keepdims=True), 1e-24))
        dirs = []
        for d, (wd, ad) in enumerate(((wd_f, ad_f), (wd_b, ad_b))):
            w_log = -jax.nn.softplus(-(w0[d] + jnp.tanh(wd) @ w2[d])) - 0.5
            decay = jnp.exp(-jnp.exp(w_log.astype(jnp.float32)))
            a = jax.nn.sigmoid(a0[d] + ad @ a2[d])
            k_d = k * (1.0 + (a - 1.0) * k_a)
            dirs.append((heads(r), heads(decay), heads(k_d), heads(v), kk, heads(a)))
        return dirs, (r, k, v, g)

    def readout(o, r, k, v, g):
        B, L = o.shape[:2]
        y = _head_norm(o.reshape(B, L, GROUP_W), RW_HEADS, RW_GN_EPS) * gn_g + gn_b
        bonus = jnp.sum(heads(r * k) * r_k, -1, keepdims=True) * heads(v)
        return ((y + bonus.reshape(B, L, GROUP_W)) * jax.nn.sigmoid(g)).astype(g.dtype)

    (c_f, c_b), c_rkvg = prep(zc)
    (l_f, l_b), l_rkvg = prep(zl)
    s0 = jnp.zeros((zl.shape[0], RW_HEADS, RW_HEAD, RW_HEAD), jnp.float32)
    oc, ol = _bidir_with_context(_rwkv7_scan, c_f, l_f, c_b, l_b, s0)
    return (readout(oc, *c_rkvg) if with_ctx else None), readout(ol, *l_rkvg)


def _retention_chunkwise(inputs, s0, log_gamma):
    q, k, v = (t.astype(jnp.float32) for t in inputs)
    B, L, H, D = q.shape
    N = L // RT_CHUNK
    q, k, v = (t.reshape(B, N, RT_CHUNK, H, D) for t in (q, k, v))
    idx = jnp.arange(RT_CHUNK, dtype=jnp.float32)
    rel = idx[:, None] - idx[None, :]
    decay = jnp.where(rel >= 0, jnp.exp(jnp.maximum(rel, 0.0)[None] * log_gamma[:, None, None]), 0.0)
    scores = jnp.einsum('bnihd,bnjhd->bnhij', q, k) * decay
    o = jnp.einsum('bnhij,bnjhd->bnihd', scores, v)
    k_w = jnp.exp((RT_CHUNK - 1 - idx)[:, None] * log_gamma)
    kv = jnp.einsum('bnjhd,bnjhe->nbhde', k * k_w[:, :, None], v)
    chunk_decay = jnp.exp(RT_CHUNK * log_gamma)[None, :, None, None]

    def step(S, kv_n):
        return chunk_decay * S + kv_n, S

    s_fin, s_prev = lax.scan(step, s0, kv)
    q_w = jnp.exp((idx + 1.0)[:, None] * log_gamma)
    o = o + jnp.einsum('bnihd,nbhde->bnihe', q * q_w[:, :, None], s_prev)
    return o.reshape(B, L, H, D), s_fin


def _retention_mixer(zc, zl, rows, cols, with_ctx):
    log_gamma = jnp.log1p(-jnp.exp2(-5.0 - jnp.arange(RT_HEADS, dtype=jnp.float32)))

    def prep(z, rotate):
        B, L = z.shape[:2]
        q, k, v, g = jnp.split(z, 4, axis=-1)
        q, k, v = (t.reshape(B, L, RT_HEADS, RT_HEAD) for t in (q, k, v))
        if rotate:
            q, k = _rope_2d(q, rows, cols), _rope_2d(k, rows, cols)
        return (q, k * RT_HEAD ** -0.5, v), g

    def scan_fn(xs, s0):
        return _retention_chunkwise(xs, s0, log_gamma)

    def readout(o, g):
        B, L = o.shape[:2]
        return (_head_norm(o.reshape(B, L, GROUP_W), RT_HEADS, 1e-6) * jax.nn.silu(g)).astype(g.dtype)

    c_in, gc = prep(zc, False)
    l_in, gl = prep(zl, True)
    s0 = jnp.zeros((zl.shape[0], RT_HEADS, RT_HEAD, RT_HEAD), jnp.float32)
    oc, ol = _bidir_with_context(scan_fn, c_in, l_in, c_in, l_in, s0)
    return (readout(oc, gc) if with_ctx else None), readout(ol, gl)


def _gla_chunkwise(inputs, s0):
    q, log_f, k, v = (t.astype(jnp.float32) for t in inputs)
    B, L, H, _ = q.shape
    N = L // HG_CHUNK

    def chunks(t):
        return t.reshape(B, N, HG_CHUNK, H, t.shape[-1]).transpose(1, 0, 3, 2, 4)

    causal = jnp.tril(jnp.ones((HG_CHUNK, HG_CHUNK), bool))[:, :, None]

    def step(S, xs):
        q_c, lf_c, k_c, v_c = xs
        b = jnp.cumsum(lf_c, axis=2)
        diff = b[:, :, :, None, :] - b[:, :, None, :, :]
        dec = jnp.where(causal, jnp.exp(jnp.where(causal, diff, 0.0)), 0.0)
        A = jnp.einsum('bhtd,bhsd,bhtsd->bhts', q_c, k_c, dec)
        o = jnp.einsum('bhts,bhsv->bhtv', A, v_c) + jnp.einsum('bhtd,bhdv->bhtv', q_c * jnp.exp(b), S)
        b_last = b[:, :, -1:, :]
        S = jnp.exp(b_last[:, :, 0, :])[..., None] * S + jnp.einsum('bhsd,bhsv->bhdv', k_c * jnp.exp(b_last - b), v_c)
        return S, o

    s_fin, o = lax.scan(step, s0, tuple(chunks(t) for t in (q, log_f, k, v)))
    return o.transpose(1, 0, 3, 2, 4).reshape(B, L, H, v.shape[-1]), s_fin


def _hgrn2_mixer(zc, zl, lb_f, lb_b, norm_g, with_ctx):
    def heads(t):
        return t.reshape(t.shape[0], t.shape[1], HG_HEADS, HG_HEAD)

    def prep(z):
        q, f_f, f_b, i, g = jnp.split(z, 5, axis=-1)
        q, i = heads(jax.nn.silu(q)), heads(i)
        dirs = []
        for f, lb in ((f_f, lb_f), (f_b, lb_b)):
            gate = lb + (1.0 - lb) * jax.nn.sigmoid(f.astype(jnp.float32))
            log_f = jnp.log(jnp.maximum(gate, HG_MIN_GATE))
            dirs.append((q, heads(log_f), heads(1.0 - gate), i))
        return dirs, g

    def readout(o, g):
        B, L = o.shape[:2]
        return (_head_norm(o.reshape(B, L, GROUP_W), HG_HEADS, 1e-6, rms=True) * norm_g * jax.nn.silu(g)).astype(g.dtype)

    (c_f, c_b), gc = prep(zc)
    (l_f, l_b), gl = prep(zl)
    s0 = jnp.zeros((zl.shape[0], HG_HEADS, HG_HEAD, HG_HEAD), jnp.float32)
    oc, ol = _bidir_with_context(_gla_chunkwise, c_f, l_f, c_b, l_b, s0)
    return (readout(oc, gc) if with_ctx else None), readout(ol, gl)


def kernel(x, c, ctx, c_ctx, ada_w, ada_b, w_in, w_out, ln_g, ln_b, hy_conv_w, hy_conv_b, hy_w1, hy_b1, hy_w2, hy_b2, hy_w3, hy_b3, hy_freq, hy_bias, rw_mu, rw_w0, rw_w2, rw_a0, rw_a2, rw_k_k, rw_k_a, rw_r_k, rw_gn_g, rw_gn_b, hg_lb_raw, hg_norm_g, ffn_w1, ffn_w3, ffn_w2):
    L = x.shape[1]
    n_rows = L // GRID_W
    rows = jnp.repeat(jnp.arange(n_rows, dtype=jnp.float32), GRID_W)
    cols = jnp.tile(jnp.arange(GRID_W, dtype=jnp.float32), n_rows)
    sm = jax.nn.softmax(hg_lb_raw.astype(jnp.float32), axis=1)
    lower_bounds = jnp.cumsum(sm, axis=1) - sm[:, :1]
    split_at = list(np.cumsum([HY_COLS, RW_COLS, RT_COLS]))

    c8 = jnp.zeros((8, D_MODEL), jnp.float32).at[0].set(c[0]).at[1].set(c_ctx)
    xl, xc = x[0], ctx[0]
    for l in range(DEPTH):
        with_ctx = l < DEPTH - 1
        mod = ada_modulation(c8, ada_w[l], ada_b[l][None, :])
        mod_l = [mod[0:1, i * D_MODEL:(i + 1) * D_MODEL] for i in range(6)]
        mod_c = [mod[1:2, i * D_MODEL:(i + 1) * D_MODEL] for i in range(6)]
        w_in_b = jnp.pad(w_in[l], ((0, 0), (0, P_IN_PAD - P_IN))).astype(jnp.bfloat16)
        w_out_b = w_out[l].astype(jnp.bfloat16)
        w1_b, w3_b, w2_b = (w[l].astype(jnp.bfloat16) for w in (ffn_w1, ffn_w3, ffn_w2))

        zl = modulated_projection(xl, mod_l[0], mod_l[1], w_in_b)[None, :, :P_IN]
        zc = modulated_projection(xc, mod_c[0], mod_c[1], w_in_b)[None, :, :P_IN]
        zl_hy, zl_rw, zl_rt, zl_hg = jnp.split(zl, split_at, axis=-1)
        zc_hy, zc_rw, zc_rt, zc_hg = jnp.split(zc, split_at, axis=-1)

        oc_hy, ol_hy = _hyena_mixer(zc_hy, zl_hy, hy_conv_w[l], hy_conv_b[l], hy_w1[l], hy_b1[l], hy_w2[l], hy_b2[l],
                                    hy_w3[l], hy_b3[l], hy_freq[l], hy_bias[l], with_ctx)
        oc_rw, ol_rw = _rwkv7_mixer(zc_rw, zl_rw, rw_mu[l], rw_w0[l], rw_w2[l], rw_a0[l], rw_a2[l], rw_k_k[l],
                                    rw_k_a[l], rw_r_k[l], rw_gn_g[l], rw_gn_b[l], with_ctx)
        oc_rt, ol_rt = _retention_mixer(zc_rt, zl_rt, rows, cols, with_ctx)
        oc_hg, ol_hg = _hgrn2_mixer(zc_hg, zl_hg, lower_bounds[0, l], lower_bounds[1, l], hg_norm_g[l], with_ctx)

        g0, b0 = ln_g[l, 0][None, :], ln_b[l, 0][None, :]
        g1, b1 = ln_g[l, 1][None, :], ln_b[l, 1][None, :]
        yl = jnp.concatenate([ol_hy, ol_rw, ol_rt, ol_hg], axis=-1)[0]
        xl = outproj_deepnorm(yl, w_out_b, xl, mod_l[2], g0, b0)
        xl = ffn_deepnorm(xl, mod_l[3], mod_l[4], w1_b, w3_b, w2_b, mod_l[5], g1, b1)
        if with_ctx:
            yc = jnp.concatenate([oc_hy, oc_rw, oc_rt, oc_hg], axis=-1)[0]
            xc = outproj_deepnorm(yc, w_out_b, xc, mod_c[2], g0, b0)
            xc = ffn_deepnorm(xc, mod_c[3], mod_c[4], w1_b, w3_b, w2_b, mod_c[5], g1, b1)
    return xl[None]
```

```python
import functools
import math

import jax
import jax.numpy as jnp
import numpy as np
from jax import lax
from jax.experimental import pallas as pl
from jax.experimental.pallas import tpu as pltpu

D_MODEL = 2048
DEPTH = 2
GRID_W = 64
N_MIXERS = 4
GROUP_W = D_MODEL // N_MIXERS
HY_ORDER = 2
HY_EMB = 33
HY_FAST_DECAY = 0.3
HY_SLOW_DECAY = 1.5
HY_TARGET = 1e-2
RW_HEAD = 64
RW_HEADS = GROUP_W // RW_HEAD
RW_LORA = 96
RW_GN_EPS = 64e-5
RT_HEAD = 64
RT_HEADS = GROUP_W // RT_HEAD
RT_CHUNK = 128
ROPE_BASE = 10000.0
HG_HEAD = 128
HG_HEADS = GROUP_W // HG_HEAD
HG_CHUNK = 64
HG_MIN_GATE = 1e-30
FFN_HIDDEN = 5632
ALPHA = (2 * DEPTH) ** 0.25
LN_EPS = 1e-6
HY_COLS = 3 * GROUP_W
RW_COLS = 4 * GROUP_W + 4 * RW_LORA
RT_COLS = 4 * GROUP_W
HG_COLS = 5 * GROUP_W
P_IN = HY_COLS + RW_COLS + RT_COLS + HG_COLS

LANES = 128
P_IN_PAD = 8704
PROJ_TN = 512
FFN_TF = 512
VMEM_LIMIT = 56 * 1024 * 1024


def _row_tile(m):
    return 512 if m % 512 == 0 else m


def _ln_rows(x):
    mu = jnp.mean(x, axis=-1, keepdims=True)
    xc = x - mu
    var = jnp.mean(xc * xc, axis=-1, keepdims=True)
    return xc * lax.rsqrt(var + LN_EPS)


def _ada_kernel(c_ref, w_ref, b_ref, o_ref):
    c = c_ref[...]
    h = c * jax.nn.sigmoid(c)
    o_ref[...] = jnp.dot(h.astype(jnp.bfloat16), w_ref[...].astype(jnp.bfloat16),
                         preferred_element_type=jnp.float32) + b_ref[...]


def ada_modulation(c8, w, b):
    n = w.shape[1]
    tn = 1024
    return pl.pallas_call(
        _ada_kernel,
        grid=(n // tn,),
        in_specs=[pl.BlockSpec((8, D_MODEL), lambda j: (0, 0)),
                  pl.BlockSpec((D_MODEL, tn), lambda j: (0, j)),
                  pl.BlockSpec((1, tn), lambda j: (0, j))],
        out_specs=pl.BlockSpec((8, tn), lambda j: (0, j)),
        out_shape=jax.ShapeDtypeStruct((8, n), jnp.float32),
        compiler_params=pltpu.CompilerParams(dimension_semantics=("arbitrary",),
                                             vmem_limit_bytes=VMEM_LIMIT),
    )(c8, w, b)


def _proj_kernel(x_ref, sh_ref, sc_ref, w_ref, o_ref, h_ref):
    @pl.when(pl.program_id(1) == 0)
    def _():
        h = _ln_rows(x_ref[...]) * (1.0 + sc_ref[...]) + sh_ref[...]
        h_ref[...] = h.astype(jnp.bfloat16)

    o_ref[...] = jnp.dot(h_ref[...], w_ref[...], preferred_element_type=jnp.float32)


def modulated_projection(x, shift, scale, w_bf16):
    m = x.shape[0]
    n = w_bf16.shape[1]
    tm = _row_tile(m)
    return pl.pallas_call(
        _proj_kernel,
        grid=(m // tm, n // PROJ_TN),
        in_specs=[pl.BlockSpec((tm, D_MODEL), lambda i, j: (i, 0)),
                  pl.BlockSpec((1, D_MODEL), lambda i, j: (0, 0)),
                  pl.BlockSpec((1, D_MODEL), lambda i, j: (0, 0)),
                  pl.BlockSpec((D_MODEL, PROJ_TN), lambda i, j: (0, j))],
        out_specs=pl.BlockSpec((tm, PROJ_TN), lambda i, j: (i, j)),
        out_shape=jax.ShapeDtypeStruct((m, n), jnp.float32),
        scratch_shapes=[pltpu.VMEM((tm, D_MODEL), jnp.bfloat16)],
        compiler_params=pltpu.CompilerParams(dimension_semantics=("arbitrary", "arbitrary"),
                                             vmem_limit_bytes=VMEM_LIMIT),
    )(x, shift, scale, w_bf16)


def _outproj_kernel(y_ref, w_ref, x_ref, gate_ref, g_ref, b_ref, o_ref):
    y = jnp.dot(y_ref[...].astype(jnp.bfloat16), w_ref[...], preferred_element_type=jnp.float32)
    r = ALPHA * x_ref[...] + gate_ref[...] * y
    o_ref[...] = _ln_rows(r) * g_ref[...] + b_ref[...]


def outproj_deepnorm(y, w_bf16, x, gate, g, b):
    m = x.shape[0]
    tm = _row_tile(m)
    row = pl.BlockSpec((tm, D_MODEL), lambda i: (i, 0))
    vec = pl.BlockSpec((1, D_MODEL), lambda i: (0, 0))
    return pl.pallas_call(
        _outproj_kernel,
        grid=(m // tm,),
        in_specs=[row, pl.BlockSpec((D_MODEL, D_MODEL), lambda i: (0, 0)), row, vec, vec, vec],
        out_specs=row,
        out_shape=jax.ShapeDtypeStruct((m, D_MODEL), jnp.float32),
        compiler_params=pltpu.CompilerParams(dimension_semantics=("arbitrary",),
                                             vmem_limit_bytes=VMEM_LIMIT),
    )(y, w_bf16, x, gate, g, b)


def _ffn_kernel(x_ref, sh_ref, sc_ref, w1_ref, w3_ref, w2_ref, gate_ref, g_ref, b_ref, o_ref, h_ref, acc_ref):
    j = pl.program_id(1)

    @pl.when(j == 0)
    def _():
        h = _ln_rows(x_ref[...]) * (1.0 + sc_ref[...]) + sh_ref[...]
        h_ref[...] = h.astype(jnp.bfloat16)
        acc_ref[...] = jnp.zeros_like(acc_ref)

    h = h_ref[...]
    a = jnp.dot(h, w1_ref[...], preferred_element_type=jnp.float32)
    u = jnp.dot(h, w3_ref[...], preferred_element_type=jnp.float32)
    s = (a * jax.nn.sigmoid(a) * u).astype(jnp.bfloat16)
    acc_ref[...] += jnp.dot(s, w2_ref[...], preferred_element_type=jnp.float32)

    @pl.when(j == pl.num_programs(1) - 1)
    def _():
        r = ALPHA * x_ref[...] + gate_ref[...] * acc_ref[...]
        o_ref[...] = _ln_rows(r) * g_ref[...] + b_ref[...]


def ffn_deepnorm(x, shift, scale, w1, w3, w2, gate, g, b):
    m = x.shape[0]
    tm = _row_tile(m)
    row = pl.BlockSpec((tm, D_MODEL), lambda i, j: (i, 0))
    vec = pl.BlockSpec((1, D_MODEL), lambda i, j: (0, 0))
    return pl.pallas_call(
        _ffn_kernel,
        grid=(m // tm, FFN_HIDDEN // FFN_TF),
        in_specs=[row, vec, vec,
                  pl.BlockSpec((D_MODEL, FFN_TF), lambda i, j: (0, j)),
                  pl.BlockSpec((D_MODEL, FFN_TF), lambda i, j: (0, j)),
                  pl.BlockSpec((FFN_TF, D_MODEL), lambda i, j: (j, 0)),
                  vec, vec, vec],
        out_specs=row,
        out_shape=jax.ShapeDtypeStruct((m, D_MODEL), jnp.float32),
        scratch_shapes=[pltpu.VMEM((tm, D_MODEL), jnp.bfloat16),
                        pltpu.VMEM((tm, D_MODEL), jnp.float32)],
        compiler_params=pltpu.CompilerParams(dimension_semantics=("arbitrary", "arbitrary"),
                                             vmem_limit_bytes=VMEM_LIMIT),
    )(x, shift, scale, w1, w3, w2, gate, g, b)


HI = lax.Precision.HIGHEST
SCAN_T = 64
GROUP_LANES = 256


def _dot(a, b):
    return jnp.dot(a, b, precision=HI, preferred_element_type=jnp.float32)


def _dot_nt(a, b):
    return lax.dot_general(a, b, (((1,), (1,)), ((), ())), precision=HI, preferred_element_type=jnp.float32)


def _dot_tn(a, b):
    return lax.dot_general(a, b, (((0,), (0,)), ((), ())), precision=HI, preferred_element_type=jnp.float32)


def _chunk_index(d, i, n_ctx, n_all):
    bwd = jnp.where(i < n_ctx, n_ctx - 1 - i, n_all + n_ctx - 1 - i)
    return jnp.where(d == 0, i, bwd)


def _rwkv_scan_kernel(r_ref, v_ref, kk_ref, lw_ref, k_ref, a_ref, o_ref, ht_ref, *, head):
    T = SCAN_T
    G = GROUP_LANES
    nh = G // head
    n = nh * T
    d = pl.program_id(0)

    @pl.when(pl.program_id(2) == 0)
    def _():
        ht_ref[...] = jnp.zeros_like(ht_ref)

    r, v, kk = r_ref[...], v_ref[...], kk_ref[...]
    lw, k, a = lw_ref[0], k_ref[0], a_ref[0]

    ti = lax.broadcasted_iota(jnp.int32, (T, T), 0)
    si = lax.broadcasted_iota(jnp.int32, (T, T), 1)
    sign = 1 - 2 * d
    before = (ti - si) * sign > 0
    tri_incl = jnp.where(before | (si == ti), 1.0, 0.0)
    c = _dot(tri_incl, lw)
    ctot = jnp.sum(lw, axis=0, keepdims=True)
    c_prev = c - lw
    beta = kk * a
    einv = jnp.exp(-c)
    efin = jnp.exp(ctot - c)
    a_bar = -kk * jnp.exp(c_prev)
    r_bar = r * jnp.exp(c)
    k_til, b_til = k * einv, beta * einv
    k_hat, b_hat = k * efin, beta * efin

    row_h = lax.broadcasted_iota(jnp.int32, (n, G), 0) // T
    col_h = lax.broadcasted_iota(jnp.int32, (n, G), 1) // head
    same = row_h == col_h

    def bd(x):
        return jnp.where(same, jnp.concatenate([x] * nh, axis=0), 0.0)

    rt = lax.broadcasted_iota(jnp.int32, (n, n), 0)
    cs = lax.broadcasted_iota(jnp.int32, (n, n), 1)
    blk = (rt // T) == (cs // T)
    before_bd = blk & ((rt - cs) * sign > 0)
    incl_bd = before_bd | (rt == cs)

    lhs = jnp.concatenate([bd(a_bar), bd(r_bar)], axis=0)
    rhs = jnp.concatenate([bd(k_til), bd(b_til)], axis=0)
    m = _dot_nt(lhs, rhs)
    ak, ab, rk, rb = m[:n, :n], m[:n, n:], m[n:, :n], m[n:, n:]
    ht = ht_ref[...]
    g = _dot_nt(lhs, ht)
    v_bd = bd(v)
    lb = jnp.where(before_bd, ab, 0.0)
    x = g[:n] + _dot(jnp.where(before_bd, ak, 0.0), v_bd)
    x = x + _dot(lb, x)
    lp = lb
    p = 2
    while p < T:
        lp = _dot(lp, lp)
        x = x + _dot(lp, x)
        p *= 2
    o = g[n:] + _dot(jnp.where(incl_bd, rk, 0.0), v_bd) + _dot(jnp.where(incl_bd, rb, 0.0), x)
    acc = o[0:T]
    for h in range(1, nh):
        acc = acc + o[h * T:(h + 1) * T]
    o_ref[0] = acc
    ht_ref[...] = ht * jnp.exp(ctot) + _dot_tn(v_bd, bd(k_hat)) + _dot_tn(x, bd(b_hat))


def rwkv_scan(r, v, kk, lw2, k2, a2, n_ctx_rows):
    N, W = r.shape
    T = SCAN_T
    n_all, n_ctx = N // T, n_ctx_rows // T
    shared = pl.BlockSpec((T, GROUP_LANES), lambda d, g, i: (_chunk_index(d, i, n_ctx, n_all), g))
    per_dir = pl.BlockSpec((1, T, GROUP_LANES), lambda d, g, i: (d, _chunk_index(d, i, n_ctx, n_all), g))
    return pl.pallas_call(
        functools.partial(_rwkv_scan_kernel, head=RW_HEAD),
        grid=(2, W // GROUP_LANES, n_all),
        in_specs=[shared, shared, shared, per_dir, per_dir, per_dir],
        out_specs=per_dir,
        out_shape=jax.ShapeDtypeStruct((2, N, W), jnp.float32),
        scratch_shapes=[pltpu.VMEM((GROUP_LANES, GROUP_LANES), jnp.float32)],
        compiler_params=pltpu.CompilerParams(dimension_semantics=("arbitrary", "arbitrary", "arbitrary"),
                                             vmem_limit_bytes=VMEM_LIMIT),
    )(r, v, kk, lw2, k2, a2)


def _head_norm(x, n_heads, eps, rms=False):
    B, L, W = x.shape
    xf = x.astype(jnp.float32).reshape(B, L, n_heads, W // n_heads)
    if not rms:
        xf = xf - jnp.mean(xf, -1, keepdims=True)
    y = xf * lax.rsqrt(jnp.mean(jnp.square(xf), -1, keepdims=True) + eps)
    return y.reshape(B, L, W)


def _centred_conv3(u, w, b):
    up = jnp.pad(u, ((0, 0), (1, 1), (0, 0)))
    return up[:, :-2] * w[0] + up[:, 1:-1] * w[1] + up[:, 2:] * w[2] + b


def _centred_shift(u):
    up = jnp.pad(u, ((0, 0), (1, 1), (0, 0)))
    return 0.5 * (up[:, :-2] + up[:, 2:])


def _rope_2d(x, rows, cols):
    d_axis = x.shape[-1] // 2
    inv = ROPE_BASE ** (-jnp.arange(0, d_axis, 2, dtype=jnp.float32) / d_axis)

    def rot(xa, pos):
        ang = pos[:, None, None] * inv
        cos, sin = jnp.cos(ang), jnp.sin(ang)
        x1, x2 = jnp.split(xa, 2, axis=-1)
        return jnp.concatenate([x1 * cos - x2 * sin, x1 * sin + x2 * cos], -1)

    xr, xc = jnp.split(x.astype(jnp.float32), 2, axis=-1)
    return jnp.concatenate([rot(xr, rows), rot(xc, cols)], -1)


def _flip_time(xs):
    return tuple(jnp.flip(a, axis=1) for a in xs)


def _bidir_with_context(scan_fn, ctx_fwd, lat_fwd, ctx_bwd, lat_bwd, s0):
    oc_f, sc_f = scan_fn(ctx_fwd, s0)
    ol_f, _ = scan_fn(lat_fwd, sc_f)
    oc_b, sc_b = scan_fn(_flip_time(ctx_bwd), s0)
    ol_b, _ = scan_fn(_flip_time(lat_bwd), sc_b)
    return oc_f + jnp.flip(oc_b, axis=1), ol_f + jnp.flip(ol_b, axis=1)


def _hyena_filters(L, w1, b1, w2, b2, w3, b3, freq):
    t = jnp.linspace(0.0, 1.0, L, dtype=jnp.float32)[:, None]
    n_bands = (HY_EMB - 1) // 2
    f = jnp.linspace(1e-4, n_bands - 1, n_bands, dtype=jnp.float32)[None, :]
    ang = (2.0 * math.pi / L) * jnp.arange(L, dtype=jnp.float32)[:, None] * f
    z = jnp.concatenate([t, jnp.cos(ang), -jnp.sin(ang)], -1)
    freq = freq.astype(jnp.float32)
    h = jnp.sin(freq * (z @ w1 + b1))
    h = jnp.sin(freq * (h @ w2 + b2))
    h = (h @ w3 + b3).astype(jnp.float32).reshape(L, HY_ORDER, 2, GROUP_W)
    max_decay = math.log(HY_TARGET) / HY_FAST_DECAY
    min_decay = math.log(HY_TARGET) / HY_SLOW_DECAY
    deltas = jnp.linspace(min_decay, max_decay, GROUP_W, dtype=jnp.float32)
    h = h * jnp.exp(-t * jnp.abs(deltas))[:, None, None, :]
    return h / jnp.sum(jnp.abs(h), axis=(0, 2), keepdims=True)


def _bidir_fft_conv(u, h_pos, h_neg, bias):
    L = u.shape[1]
    h_full = jnp.concatenate([h_pos, jnp.zeros_like(h_pos[:1]), h_neg[:0:-1]], axis=0)
    uf = jnp.fft.rfft(u.astype(jnp.float32), n=2 * L, axis=1)
    hf = jnp.fft.rfft(h_full, n=2 * L, axis=0)
    y = jnp.fft.irfft(uf * hf[None], n=2 * L, axis=1)[:, :L]
    return y + u.astype(jnp.float32) * bias.astype(jnp.float32)


def _hyena_mixer(zc, zl, conv_w, conv_b, w1, b1, w2, b2, w3, b3, freq, bias, with_ctx):
    def run(z):
        L = z.shape[1]
        u = _centred_conv3(z, conv_w, conv_b)
        v, x1, x2 = jnp.split(u, 3, axis=-1)
        h = _hyena_filters(L, w1, b1, w2, b2, w3, b3, freq)
        y = v
        for n, gate in enumerate((x1, x2)):
            y = gate * _bidir_fft_conv(y, h[:, n, 0], h[:, n, 1], bias[n])
        return y.astype(z.dtype)
    return (run(zc) if with_ctx else None), run(zl)


def _rwkv7_scan(inputs, s0):
    r, w, k, v, kk, a = (jnp.moveaxis(t.astype(jnp.float32), 1, 0) for t in inputs)

    def step(S, xs):
        r_t, w_t, k_t, v_t, kk_t, a_t = xs
        sa = jnp.einsum('bhvk,bhk->bhv', S, -kk_t)
        S = S * w_t[:, :, None, :] + sa[..., :, None] * (kk_t * a_t)[..., None, :] + v_t[..., :, None] * k_t[..., None, :]
        return S, jnp.einsum('bhvk,bhk->bhv', S, r_t)

    s_fin, o = lax.scan(step, s0, (r, w, k, v, kk, a))
    return jnp.moveaxis(o, 0, 1), s_fin


def _rwkv7_mixer(zc, zl, mu, w0, w2, a0, a2, k_k, k_a, r_k, gn_g, gn_b, with_ctx):
    split_at = list(np.cumsum([GROUP_W] * 4 + [RW_LORA] * 3))

    def heads(t):
        return t.reshape(t.shape[0], t.shape[1], RW_HEADS, RW_HEAD)

    def prep(z):
        z = z + (_centred_shift(z) - z) * mu
        r, k, v, g, wd_f, wd_b, ad_f, ad_b = jnp.split(z, split_at, axis=-1)
        kk = heads((k * k_k).astype(jnp.float32))
        kk = kk * lax.rsqrt(jnp.maximum(jnp.sum(jnp.square(kk), -1, keepdims=True), 1e-24))
        dirs = []
        for d, (wd, ad) in enumerate(((wd_f, ad_f), (wd_b, ad_b))):
            w_log = -jax.nn.softplus(-(w0[d] + jnp.tanh(wd) @ w2[d])) - 0.5
            log_decay = -jnp.exp(w_log.astype(jnp.float32))
            a = jax.nn.sigmoid(a0[d] + ad @ a2[d])
            k_d = k * (1.0 + (a - 1.0) * k_a)
            dirs.append((log_decay[0], k_d[0], a[0]))
        return dirs, (r, k, v, g), kk.reshape(kk.shape[1], GROUP_W)

    def readout(o, r, k, v, g):
        B, L = o.shape[:2]
        y = _head_norm(o.reshape(B, L, GROUP_W), RW_HEADS, RW_GN_EPS) * gn_g + gn_b
        bonus = jnp.sum(heads(r * k) * r_k, -1, keepdims=True) * heads(v)
        return ((y + bonus.reshape(B, L, GROUP_W)) * jax.nn.sigmoid(g)).astype(g.dtype)

    c_dirs, c_rkvg, c_kk = prep(zc)
    l_dirs, l_rkvg, l_kk = prep(zl)
    n_ctx = zc.shape[1]
    cat = lambda a, b: jnp.concatenate([a, b], axis=0)
    per_dir = [jnp.stack([cat(c_dirs[d][i], l_dirs[d][i]) for d in range(2)]) for i in range(3)]
    o2 = rwkv_scan(cat(c_rkvg[0][0], l_rkvg[0][0]), cat(c_rkvg[2][0], l_rkvg[2][0]), cat(c_kk, l_kk),
                   per_dir[0], per_dir[1], per_dir[2], n_ctx)
    o = (o2[0] + o2[1])[None]
    oc, ol = o[:, :n_ctx], o[:, n_ctx:]
    return (readout(oc, *c_rkvg) if with_ctx else None), readout(ol, *l_rkvg)


def _retention_chunkwise(inputs, s0, log_gamma):
    q, k, v = (t.astype(jnp.float32) for t in inputs)
    B, L, H, D = q.shape
    N = L // RT_CHUNK
    q, k, v = (t.reshape(B, N, RT_CHUNK, H, D) for t in (q, k, v))
    idx = jnp.arange(RT_CHUNK, dtype=jnp.float32)
    rel = idx[:, None] - idx[None, :]
    decay = jnp.where(rel >= 0, jnp.exp(jnp.maximum(rel, 0.0)[None] * log_gamma[:, None, None]), 0.0)
    scores = jnp.einsum('bnihd,bnjhd->bnhij', q, k) * decay
    o = jnp.einsum('bnhij,bnjhd->bnihd', scores, v)
    k_w = jnp.exp((RT_CHUNK - 1 - idx)[:, None] * log_gamma)
    kv = jnp.einsum('bnjhd,bnjhe->nbhde', k * k_w[:, :, None], v)
    chunk_decay = jnp.exp(RT_CHUNK * log_gamma)[None, :, None, None]

    def step(S, kv_n):
        return chunk_decay * S + kv_n, S

    s_fin, s_prev = lax.scan(step, s0, kv)
    q_w = jnp.exp((idx + 1.0)[:, None] * log_gamma)
    o = o + jnp.einsum('bnihd,nbhde->bnihe', q * q_w[:, :, None], s_prev)
    return o.reshape(B, L, H, D), s_fin


def _retention_mixer(zc, zl, rows, cols, with_ctx):
    log_gamma = jnp.log1p(-jnp.exp2(-5.0 - jnp.arange(RT_HEADS, dtype=jnp.float32)))

    def prep(z, rotate):
        B, L = z.shape[:2]
        q, k, v, g = jnp.split(z, 4, axis=-1)
        q, k, v = (t.reshape(B, L, RT_HEADS, RT_HEAD) for t in (q, k, v))
        if rotate:
            q, k = _rope_2d(q, rows, cols), _rope_2d(k, rows, cols)
        return (q, k * RT_HEAD ** -0.5, v), g

    def scan_fn(xs, s0):
        return _retention_chunkwise(xs, s0, log_gamma)

    def readout(o, g):
        B, L = o.shape[:2]
        return (_head_norm(o.reshape(B, L, GROUP_W), RT_HEADS, 1e-6) * jax.nn.silu(g)).astype(g.dtype)

    c_in, gc = prep(zc, False)
    l_in, gl = prep(zl, True)
    s0 = jnp.zeros((zl.shape[0], RT_HEADS, RT_HEAD, RT_HEAD), jnp.float32)
    oc, ol = _bidir_with_context(scan_fn, c_in, l_in, c_in, l_in, s0)
    return (readout(oc, gc) if with_ctx else None), readout(ol, gl)


def _gla_chunkwise(inputs, s0):
    q, log_f, k, v = (t.astype(jnp.float32) for t in inputs)
    B, L, H, _ = q.shape
    N = L // HG_CHUNK

    def chunks(t):
        return t.reshape(B, N, HG_CHUNK, H, t.shape[-1]).transpose(1, 0, 3, 2, 4)

    causal = jnp.tril(jnp.ones((HG_CHUNK, HG_CHUNK), bool))[:, :, None]

    def step(S, xs):
        q_c, lf_c, k_c, v_c = xs
        b = jnp.cumsum(lf_c, axis=2)
        diff = b[:, :, :, None, :] - b[:, :, None, :, :]
        dec = jnp.where(causal, jnp.exp(jnp.where(causal, diff, 0.0)), 0.0)
        A = jnp.einsum('bhtd,bhsd,bhtsd->bhts', q_c, k_c, dec)
        o = jnp.einsum('bhts,bhsv->bhtv', A, v_c) + jnp.einsum('bhtd,bhdv->bhtv', q_c * jnp.exp(b), S)
        b_last = b[:, :, -1:, :]
        S = jnp.exp(b_last[:, :, 0, :])[..., None] * S + jnp.einsum('bhsd,bhsv->bhdv', k_c * jnp.exp(b_last - b), v_c)
        return S, o

    s_fin, o = lax.scan(step, s0, tuple(chunks(t) for t in (q, log_f, k, v)))
    return o.transpose(1, 0, 3, 2, 4).reshape(B, L, H, v.shape[-1]), s_fin


def _hgrn2_mixer(zc, zl, lb_f, lb_b, norm_g, with_ctx):
    def heads(t):
        return t.reshape(t.shape[0], t.shape[1], HG_HEADS, HG_HEAD)

    def prep(z):
        q, f_f, f_b, i, g = jnp.split(z, 5, axis=-1)
        q, i = heads(jax.nn.silu(q)), heads(i)
        dirs = []
        for f, lb in ((f_f, lb_f), (f_b, lb_b)):
            gate = lb + (1.0 - lb) * jax.nn.sigmoid(f.astype(jnp.float32))
            log_f = jnp.log(jnp.maximum(gate, HG_MIN_GATE))
            dirs.append((q, heads(log_f), heads(1.0 - gate), i))
        return dirs, g

    def readout(o, g):
        B, L = o.shape[:2]
        return (_head_norm(o.reshape(B, L, GROUP_W), HG_HEADS, 1e-6, rms=True) * norm_g * jax.nn.silu(g)).astype(g.dtype)

    (c_f, c_b), gc = prep(zc)
    (l_f, l_b), gl = prep(zl)
    s0 = jnp.zeros((zl.shape[0], HG_HEADS, HG_HEAD, HG_HEAD), jnp.float32)
    oc, ol = _bidir_with_context(_gla_chunkwise, c_f, l_f, c_b, l_b, s0)
    return (readout(oc, gc) if with_ctx else None), readout(ol, gl)


def kernel(x, c, ctx, c_ctx, ada_w, ada_b, w_in, w_out, ln_g, ln_b, hy_conv_w, hy_conv_b, hy_w1, hy_b1, hy_w2, hy_b2, hy_w3, hy_b3, hy_freq, hy_bias, rw_mu, rw_w0, rw_w2, rw_a0, rw_a2, rw_k_k, rw_k_a, rw_r_k, rw_gn_g, rw_gn_b, hg_lb_raw, hg_norm_g, ffn_w1, ffn_w3, ffn_w2):
    L = x.shape[1]
    n_rows = L // GRID_W
    rows = jnp.repeat(jnp.arange(n_rows, dtype=jnp.float32), GRID_W)
    cols = jnp.tile(jnp.arange(GRID_W, dtype=jnp.float32), n_rows)
    sm = jax.nn.softmax(hg_lb_raw.astype(jnp.float32), axis=1)
    lower_bounds = jnp.cumsum(sm, axis=1) - sm[:, :1]
    split_at = list(np.cumsum([HY_COLS, RW_COLS, RT_COLS]))

    c8 = jnp.zeros((8, D_MODEL), jnp.float32).at[0].set(c[0]).at[1].set(c_ctx)
    xl, xc = x[0], ctx[0]
    for l in range(DEPTH):
        with_ctx = l < DEPTH - 1
        mod = ada_modulation(c8, ada_w[l], ada_b[l][None, :])
        mod_l = [mod[0:1, i * D_MODEL:(i + 1) * D_MODEL] for i in range(6)]
        mod_c = [mod[1:2, i * D_MODEL:(i + 1) * D_MODEL] for i in range(6)]
        w_in_b = jnp.pad(w_in[l], ((0, 0), (0, P_IN_PAD - P_IN))).astype(jnp.bfloat16)
        w_out_b = w_out[l].astype(jnp.bfloat16)
        w1_b, w3_b, w2_b = (w[l].astype(jnp.bfloat16) for w in (ffn_w1, ffn_w3, ffn_w2))

        zl = modulated_projection(xl, mod_l[0], mod_l[1], w_in_b)[None, :, :P_IN]
        zc = modulated_projection(xc, mod_c[0], mod_c[1], w_in_b)[None, :, :P_IN]
        zl_hy, zl_rw, zl_rt, zl_hg = jnp.split(zl, split_at, axis=-1)
        zc_hy, zc_rw, zc_rt, zc_hg = jnp.split(zc, split_at, axis=-1)

        oc_hy, ol_hy = _hyena_mixer(zc_hy, zl_hy, hy_conv_w[l], hy_conv_b[l], hy_w1[l], hy_b1[l], hy_w2[l], hy_b2[l],
                                    hy_w3[l], hy_b3[l], hy_freq[l], hy_bias[l], with_ctx)
        oc_rw, ol_rw = _rwkv7_mixer(zc_rw, zl_rw, rw_mu[l], rw_w0[l], rw_w2[l], rw_a0[l], rw_a2[l], rw_k_k[l],
                                    rw_k_a[l], rw_r_k[l], rw_gn_g[l], rw_gn_b[l], with_ctx)
        oc_rt, ol_rt = _retention_mixer(zc_rt, zl_rt, rows, cols, with_ctx)
        oc_hg, ol_hg = _hgrn2_mixer(zc_hg, zl_hg, lower_bounds[0, l], lower_bounds[1, l], hg_norm_g[l], with_ctx)

        g0, b0 = ln_g[l, 0][None, :], ln_b[l, 0][None, :]
        g1, b1 = ln_g[l, 1][None, :], ln_b[l, 1][None, :]
        yl = jnp.concatenate([ol_hy, ol_rw, ol_rt, ol_hg], axis=-1)[0]
        xl = outproj_deepnorm(yl, w_out_b, xl, mod_l[2], g0, b0)
        xl = ffn_deepnorm(xl, mod_l[3], mod_l[4], w1_b, w3_b, w2_b, mod_l[5], g1, b1)
        if with_ctx:
            yc = jnp.concatenate([oc_hy, oc_rw, oc_rt, oc_hg], axis=-1)[0]
            xc = outproj_deepnorm(yc, w_out_b, xc, mod_c[2], g0, b0)
            xc = ffn_deepnorm(xc, mod_c[3], mod_c[4], w1_b, w3_b, w2_b, mod_c[5], g1, b1)
    return xl[None]
```

```python
import functools
import math

import jax
import jax.numpy as jnp
import numpy as np
from jax import lax
from jax.experimental import pallas as pl
from jax.experimental.pallas import tpu as pltpu

D_MODEL = 2048
DEPTH = 2
GRID_W = 64
N_MIXERS = 4
GROUP_W = D_MODEL // N_MIXERS
HY_ORDER = 2
HY_EMB = 33
HY_FAST_DECAY = 0.3
HY_SLOW_DECAY = 1.5
HY_TARGET = 1e-2
RW_HEAD = 64
RW_HEADS = GROUP_W // RW_HEAD
RW_LORA = 96
RW_GN_EPS = 64e-5
RT_HEAD = 64
RT_HEADS = GROUP_W // RT_HEAD
RT_CHUNK = 128
ROPE_BASE = 10000.0
HG_HEAD = 128
HG_HEADS = GROUP_W // HG_HEAD
HG_CHUNK = 64
HG_MIN_GATE = 1e-30
FFN_HIDDEN = 5632
ALPHA = (2 * DEPTH) ** 0.25
LN_EPS = 1e-6
HY_COLS = 3 * GROUP_W
RW_COLS = 4 * GROUP_W + 4 * RW_LORA
RT_COLS = 4 * GROUP_W
HG_COLS = 5 * GROUP_W
P_IN = HY_COLS + RW_COLS + RT_COLS + HG_COLS

LANES = 128
P_IN_PAD = 8704
PROJ_TN = 512
FFN_TF = 512
VMEM_LIMIT = 56 * 1024 * 1024


def _row_tile(m):
    return 512 if m % 512 == 0 else m


def _ln_rows(x):
    mu = jnp.mean(x, axis=-1, keepdims=True)
    xc = x - mu
    var = jnp.mean(xc * xc, axis=-1, keepdims=True)
    return xc * lax.rsqrt(var + LN_EPS)


def _ada_kernel(c_ref, w_ref, b_ref, o_ref):
    c = c_ref[...]
    h = c * jax.nn.sigmoid(c)
    o_ref[...] = jnp.dot(h.astype(jnp.bfloat16), w_ref[...].astype(jnp.bfloat16),
                         preferred_element_type=jnp.float32) + b_ref[...]


def ada_modulation(c8, w, b):
    n = w.shape[1]
    tn = 1024
    return pl.pallas_call(
        _ada_kernel,
        grid=(n // tn,),
        in_specs=[pl.BlockSpec((8, D_MODEL), lambda j: (0, 0)),
                  pl.BlockSpec((D_MODEL, tn), lambda j: (0, j)),
                  pl.BlockSpec((1, tn), lambda j: (0, j))],
        out_specs=pl.BlockSpec((8, tn), lambda j: (0, j)),
        out_shape=jax.ShapeDtypeStruct((8, n), jnp.float32),
        compiler_params=pltpu.CompilerParams(dimension_semantics=("arbitrary",),
                                             vmem_limit_bytes=VMEM_LIMIT),
    )(c8, w, b)


def _proj_kernel(x_ref, sh_ref, sc_ref, w_ref, o_ref, h_ref):
    @pl.when(pl.program_id(1) == 0)
    def _():
        h = _ln_rows(x_ref[...]) * (1.0 + sc_ref[...]) + sh_ref[...]
        h_ref[...] = h.astype(jnp.bfloat16)

    o_ref[...] = jnp.dot(h_ref[...], w_ref[...], preferred_element_type=jnp.float32)


def modulated_projection(x, shift, scale, w_bf16):
    m = x.shape[0]
    n = w_bf16.shape[1]
    tm = _row_tile(m)
    return pl.pallas_call(
        _proj_kernel,
        grid=(m // tm, n // PROJ_TN),
        in_specs=[pl.BlockSpec((tm, D_MODEL), lambda i, j: (i, 0)),
                  pl.BlockSpec((1, D_MODEL), lambda i, j: (0, 0)),
                  pl.BlockSpec((1, D_MODEL), lambda i, j: (0, 0)),
                  pl.BlockSpec((D_MODEL, PROJ_TN), lambda i, j: (0, j))],
        out_specs=pl.BlockSpec((tm, PROJ_TN), lambda i, j: (i, j)),
        out_shape=jax.ShapeDtypeStruct((m, n), jnp.float32),
        scratch_shapes=[pltpu.VMEM((tm, D_MODEL), jnp.bfloat16)],
        compiler_params=pltpu.CompilerParams(dimension_semantics=("arbitrary", "arbitrary"),
                                             vmem_limit_bytes=VMEM_LIMIT),
    )(x, shift, scale, w_bf16)


def _outproj_kernel(y_ref, w_ref, x_ref, gate_ref, g_ref, b_ref, o_ref):
    y = jnp.dot(y_ref[...].astype(jnp.bfloat16), w_ref[...], preferred_element_type=jnp.float32)
    r = ALPHA * x_ref[...] + gate_ref[...] * y
    o_ref[...] = _ln_rows(r) * g_ref[...] + b_ref[...]


def outproj_deepnorm(y, w_bf16, x, gate, g, b):
    m = x.shape[0]
    tm = _row_tile(m)
    row = pl.BlockSpec((tm, D_MODEL), lambda i: (i, 0))
    vec = pl.BlockSpec((1, D_MODEL), lambda i: (0, 0))
    return pl.pallas_call(
        _outproj_kernel,
        grid=(m // tm,),
        in_specs=[row, pl.BlockSpec((D_MODEL, D_MODEL), lambda i: (0, 0)), row, vec, vec, vec],
        out_specs=row,
        out_shape=jax.ShapeDtypeStruct((m, D_MODEL), jnp.float32),
        compiler_params=pltpu.CompilerParams(dimension_semantics=("arbitrary",),
                                             vmem_limit_bytes=VMEM_LIMIT),
    )(y, w_bf16, x, gate, g, b)


def _ffn_kernel(x_ref, sh_ref, sc_ref, w1_ref, w3_ref, w2_ref, gate_ref, g_ref, b_ref, o_ref, h_ref, acc_ref):
    j = pl.program_id(1)

    @pl.when(j == 0)
    def _():
        h = _ln_rows(x_ref[...]) * (1.0 + sc_ref[...]) + sh_ref[...]
        h_ref[...] = h.astype(jnp.bfloat16)
        acc_ref[...] = jnp.zeros_like(acc_ref)

    h = h_ref[...]
    a = jnp.dot(h, w1_ref[...], preferred_element_type=jnp.float32)
    u = jnp.dot(h, w3_ref[...], preferred_element_type=jnp.float32)
    s = (a * jax.nn.sigmoid(a) * u).astype(jnp.bfloat16)
    acc_ref[...] += jnp.dot(s, w2_ref[...], preferred_element_type=jnp.float32)

    @pl.when(j == pl.num_programs(1) - 1)
    def _():
        r = ALPHA * x_ref[...] + gate_ref[...] * acc_ref[...]
        o_ref[...] = _ln_rows(r) * g_ref[...] + b_ref[...]


def ffn_deepnorm(x, shift, scale, w1, w3, w2, gate, g, b):
    m = x.shape[0]
    tm = _row_tile(m)
    row = pl.BlockSpec((tm, D_MODEL), lambda i, j: (i, 0))
    vec = pl.BlockSpec((1, D_MODEL), lambda i, j: (0, 0))
    return pl.pallas_call(
        _ffn_kernel,
        grid=(m // tm, FFN_HIDDEN // FFN_TF),
        in_specs=[row, vec, vec,
                  pl.BlockSpec((D_MODEL, FFN_TF), lambda i, j: (0, j)),
                  pl.BlockSpec((D_MODEL, FFN_TF), lambda i, j: (0, j)),
                  pl.BlockSpec((FFN_TF, D_MODEL), lambda i, j: (j, 0)),
                  vec, vec, vec],
        out_specs=row,
        out_shape=jax.ShapeDtypeStruct((m, D_MODEL), jnp.float32),
        scratch_shapes=[pltpu.VMEM((tm, D_MODEL), jnp.bfloat16),
                        pltpu.VMEM((tm, D_MODEL), jnp.float32)],
        compiler_params=pltpu.CompilerParams(dimension_semantics=("arbitrary", "arbitrary"),
                                             vmem_limit_bytes=VMEM_LIMIT),
    )(x, shift, scale, w1, w3, w2, gate, g, b)


HI = lax.Precision.HIGHEST
SCAN_T = 64
GROUP_LANES = 256


def _dot(a, b):
    return jnp.dot(a, b, precision=HI, preferred_element_type=jnp.float32)


def _dot_nt(a, b):
    return lax.dot_general(a, b, (((1,), (1,)), ((), ())), precision=HI, preferred_element_type=jnp.float32)


def _dot_tn(a, b):
    return lax.dot_general(a, b, (((0,), (0,)), ((), ())), precision=HI, preferred_element_type=jnp.float32)


def _chunk_index(d, i, n_ctx, n_all):
    bwd = jnp.where(i < n_ctx, n_ctx - 1 - i, n_all + n_ctx - 1 - i)
    return jnp.where(d == 0, i, bwd)


def _rwkv_scan_kernel(r_ref, v_ref, kk_ref, lw_ref, k_ref, a_ref, o_ref, ht_ref, *, head):
    T = SCAN_T
    G = GROUP_LANES
    nh = G // head
    n = nh * T
    d = pl.program_id(0)

    @pl.when(pl.program_id(2) == 0)
    def _():
        ht_ref[...] = jnp.zeros_like(ht_ref)

    r, v, kk = r_ref[...], v_ref[...], kk_ref[...]
    lw, k, a = lw_ref[0], k_ref[0], a_ref[0]

    ti = lax.broadcasted_iota(jnp.int32, (T, T), 0)
    si = lax.broadcasted_iota(jnp.int32, (T, T), 1)
    sign = 1 - 2 * d
    before = (ti - si) * sign > 0
    tri_incl = jnp.where(before | (si == ti), 1.0, 0.0)
    c = _dot(tri_incl, lw)
    ctot = jnp.sum(lw, axis=0, keepdims=True)
    c_prev = c - lw
    beta = kk * a
    einv = jnp.exp(-c)
    efin = jnp.exp(ctot - c)
    a_bar = -kk * jnp.exp(c_prev)
    r_bar = r * jnp.exp(c)
    k_til, b_til = k * einv, beta * einv
    k_hat, b_hat = k * efin, beta * efin

    row_h = lax.broadcasted_iota(jnp.int32, (n, G), 0) // T
    col_h = lax.broadcasted_iota(jnp.int32, (n, G), 1) // head
    same = row_h == col_h

    def bd(x):
        return jnp.where(same, jnp.concatenate([x] * nh, axis=0), 0.0)

    rt = lax.broadcasted_iota(jnp.int32, (n, n), 0)
    cs = lax.broadcasted_iota(jnp.int32, (n, n), 1)
    blk = (rt // T) == (cs // T)
    before_bd = blk & ((rt - cs) * sign > 0)
    incl_bd = before_bd | (rt == cs)

    lhs = jnp.concatenate([bd(a_bar), bd(r_bar)], axis=0)
    rhs = jnp.concatenate([bd(k_til), bd(b_til)], axis=0)
    m = _dot_nt(lhs, rhs)
    ak, ab, rk, rb = m[:n, :n], m[:n, n:], m[n:, :n], m[n:, n:]
    ht = ht_ref[...]
    g = _dot_nt(lhs, ht)
    v_bd = bd(v)
    lb = jnp.where(before_bd, ab, 0.0)
    x = g[:n] + _dot(jnp.where(before_bd, ak, 0.0), v_bd)
    x = x + _dot(lb, x)
    lp = lb
    p = 2
    while p < T:
        lp = _dot(lp, lp)
        x = x + _dot(lp, x)
        p *= 2
    o = g[n:] + _dot(jnp.where(incl_bd, rk, 0.0), v_bd) + _dot(jnp.where(incl_bd, rb, 0.0), x)
    acc = o[0:T]
    for h in range(1, nh):
        acc = acc + o[h * T:(h + 1) * T]
    o_ref[0] = acc
    ht_ref[...] = ht * jnp.exp(ctot) + _dot_tn(v_bd, bd(k_hat)) + _dot_tn(x, bd(b_hat))


def rwkv_scan(r, v, kk, lw2, k2, a2, n_ctx_rows):
    N, W = r.shape
    T = SCAN_T
    n_all, n_ctx = N // T, n_ctx_rows // T
    shared = pl.BlockSpec((T, GROUP_LANES), lambda d, g, i: (_chunk_index(d, i, n_ctx, n_all), g))
    per_dir = pl.BlockSpec((1, T, GROUP_LANES), lambda d, g, i: (d, _chunk_index(d, i, n_ctx, n_all), g))
    return pl.pallas_call(
        functools.partial(_rwkv_scan_kernel, head=RW_HEAD),
        grid=(2, W // GROUP_LANES, n_all),
        in_specs=[shared, shared, shared, per_dir, per_dir, per_dir],
        out_specs=per_dir,
        out_shape=jax.ShapeDtypeStruct((2, N, W), jnp.float32),
        scratch_shapes=[pltpu.VMEM((GROUP_LANES, GROUP_LANES), jnp.float32)],
        compiler_params=pltpu.CompilerParams(dimension_semantics=("arbitrary", "arbitrary", "arbitrary"),
                                             vmem_limit_bytes=VMEM_LIMIT),
    )(r, v, kk, lw2, k2, a2)


def _retention_scan_kernel(q_ref, k_ref, v_ref, lg_ref, o_ref, st_ref, *, head):
    T, G = SCAN_T, GROUP_LANES
    nh = G // head
    n = nh * T
    d = pl.program_id(0)
    sign = 1 - 2 * d

    @pl.when(pl.program_id(2) == 0)
    def _():
        st_ref[...] = jnp.zeros_like(st_ref)

    q, k, v, lg = q_ref[...], k_ref[...], v_ref[...], lg_ref[...]
    t = lax.broadcasted_iota(jnp.int32, (T, 1), 0)
    pos = (t + d * (T - 1 - 2 * t) + 1).astype(jnp.float32)
    c = pos * lg
    ctot = float(T) * lg
    q_bar = q * jnp.exp(c)
    k_til = k * jnp.exp(-c)
    k_hat = k * jnp.exp(ctot - c)

    row_h = lax.broadcasted_iota(jnp.int32, (n, G), 0) // T
    col_h = lax.broadcasted_iota(jnp.int32, (n, G), 1) // head
    same = row_h == col_h

    def bd(x):
        return jnp.where(same, jnp.concatenate([x] * nh, axis=0), 0.0)

    rt = lax.broadcasted_iota(jnp.int32, (n, n), 0)
    cs = lax.broadcasted_iota(jnp.int32, (n, n), 1)
    incl_bd = ((rt // T) == (cs // T)) & ((rt - cs) * sign >= 0)

    q_bd, v_bd = bd(q_bar), bd(v)
    st = st_ref[...]
    scores = jnp.where(incl_bd, _dot_nt(q_bd, bd(k_til)), 0.0)
    o = _dot_nt(q_bd, st) + _dot(scores, v_bd)
    acc = o[0:T]
    for h in range(1, nh):
        acc = acc + o[h * T:(h + 1) * T]
    o_ref[0] = acc
    st_ref[...] = st * jnp.exp(ctot) + _dot_tn(v_bd, bd(k_hat))


def retention_scan(q, k, v, lg, n_ctx_rows):
    N, W = q.shape
    T = SCAN_T
    n_all, n_ctx = N // T, n_ctx_rows // T
    shared = pl.BlockSpec((T, GROUP_LANES), lambda d, g, i: (_chunk_index(d, i, n_ctx, n_all), g))
    return pl.pallas_call(
        functools.partial(_retention_scan_kernel, head=RT_HEAD),
        grid=(2, W // GROUP_LANES, n_all),
        in_specs=[shared, shared, shared, pl.BlockSpec((1, GROUP_LANES), lambda d, g, i: (0, g))],
        out_specs=pl.BlockSpec((1, T, GROUP_LANES), lambda d, g, i: (d, _chunk_index(d, i, n_ctx, n_all), g)),
        out_shape=jax.ShapeDtypeStruct((2, N, W), jnp.float32),
        scratch_shapes=[pltpu.VMEM((GROUP_LANES, GROUP_LANES), jnp.float32)],
        compiler_params=pltpu.CompilerParams(dimension_semantics=("arbitrary", "arbitrary", "arbitrary"),
                                             vmem_limit_bytes=VMEM_LIMIT),
    )(q, k, v, lg)


HG_SUB = 16


def _gla_scan_kernel(q_ref, v_ref, lf_ref, k_ref, o_ref, st_ref, *, head):
    T, S = SCAN_T, HG_SUB
    W = q_ref.shape[-1]
    nh = W // head
    d = pl.program_id(0)
    sign = 1 - 2 * d

    @pl.when(pl.program_id(1) == 0)
    def _():
        st_ref[...] = jnp.zeros_like(st_ref)

    ti = lax.broadcasted_iota(jnp.int32, (S, S), 0)
    si = lax.broadcasted_iota(jnp.int32, (S, S), 1)
    tri_incl = jnp.where((ti - si) * sign >= 0, 1.0, 0.0)
    row = lax.broadcasted_iota(jnp.int32, (S, 1), 0)

    for j in range(T // S):
        jb = j + d * (T // S - 1 - 2 * j)
        rows = pl.ds(pl.multiple_of(jb * S, S), S)
        q, v = q_ref[rows, :], v_ref[rows, :]
        lf, k = lf_ref[0, rows, :], k_ref[0, rows, :]
        b = _dot(tri_incl, lf)
        btot = jnp.sum(lf, axis=0, keepdims=True)
        qe = q * jnp.exp(b)
        ke = k * jnp.exp(btot - b)
        for h in range(nh):
            ls = slice(h * head, (h + 1) * head)
            qh, kh, vh, bh = q[:, ls], k[:, ls], v[:, ls], b[:, ls]
            st = st_ref[h]
            o = _dot_nt(qe[:, ls], st)
            for s in range(S):
                e = jnp.exp(jnp.minimum(bh - bh[s:s + 1, :], 0.0))
                a_s = jnp.sum(qh * kh[s:s + 1, :] * e, axis=-1, keepdims=True)
                a_s = jnp.where((row - s) * sign >= 0, a_s, 0.0)
                o = o + a_s * vh[s:s + 1, :]
            o_ref[0, rows, ls] = o
            st_ref[h] = st * jnp.exp(btot[:, ls]) + _dot_tn(vh, ke[:, ls])


def gla_scan(q, v, lf2, k2, n_ctx_rows):
    N, W = q.shape
    T = SCAN_T
    n_all, n_ctx = N // T, n_ctx_rows // T
    shared = pl.BlockSpec((T, W), lambda d, i: (_chunk_index(d, i, n_ctx, n_all), 0))
    per_dir = pl.BlockSpec((1, T, W), lambda d, i: (d, _chunk_index(d, i, n_ctx, n_all), 0))
    return pl.pallas_call(
        functools.partial(_gla_scan_kernel, head=HG_HEAD),
        grid=(2, n_all),
        in_specs=[shared, shared, per_dir, per_dir],
        out_specs=per_dir,
        out_shape=jax.ShapeDtypeStruct((2, N, W), jnp.float32),
        scratch_shapes=[pltpu.VMEM((W // HG_HEAD, HG_HEAD, HG_HEAD), jnp.float32)],
        compiler_params=pltpu.CompilerParams(dimension_semantics=("arbitrary", "arbitrary"),
                                             vmem_limit_bytes=VMEM_LIMIT),
    )(q, v, lf2, k2)


def _head_norm(x, n_heads, eps, rms=False):
    B, L, W = x.shape
    xf = x.astype(jnp.float32).reshape(B, L, n_heads, W // n_heads)
    if not rms:
        xf = xf - jnp.mean(xf, -1, keepdims=True)
    y = xf * lax.rsqrt(jnp.mean(jnp.square(xf), -1, keepdims=True) + eps)
    return y.reshape(B, L, W)


def _centred_conv3(u, w, b):
    up = jnp.pad(u, ((0, 0), (1, 1), (0, 0)))
    return up[:, :-2] * w[0] + up[:, 1:-1] * w[1] + up[:, 2:] * w[2] + b


def _centred_shift(u):
    up = jnp.pad(u, ((0, 0), (1, 1), (0, 0)))
    return 0.5 * (up[:, :-2] + up[:, 2:])


def _rope_2d(x, rows, cols):
    d_axis = x.shape[-1] // 2
    inv = ROPE_BASE ** (-jnp.arange(0, d_axis, 2, dtype=jnp.float32) / d_axis)

    def rot(xa, pos):
        ang = pos[:, None, None] * inv
        cos, sin = jnp.cos(ang), jnp.sin(ang)
        x1, x2 = jnp.split(xa, 2, axis=-1)
        return jnp.concatenate([x1 * cos - x2 * sin, x1 * sin + x2 * cos], -1)

    xr, xc = jnp.split(x.astype(jnp.float32), 2, axis=-1)
    return jnp.concatenate([rot(xr, rows), rot(xc, cols)], -1)


def _flip_time(xs):
    return tuple(jnp.flip(a, axis=1) for a in xs)


def _bidir_with_context(scan_fn, ctx_fwd, lat_fwd, ctx_bwd, lat_bwd, s0):
    oc_f, sc_f = scan_fn(ctx_fwd, s0)
    ol_f, _ = scan_fn(lat_fwd, sc_f)
    oc_b, sc_b = scan_fn(_flip_time(ctx_bwd), s0)
    ol_b, _ = scan_fn(_flip_time(lat_bwd), sc_b)
    return oc_f + jnp.flip(oc_b, axis=1), ol_f + jnp.flip(ol_b, axis=1)


def _hyena_filters(L, w1, b1, w2, b2, w3, b3, freq):
    t = jnp.linspace(0.0, 1.0, L, dtype=jnp.float32)[:, None]
    n_bands = (HY_EMB - 1) // 2
    f = jnp.linspace(1e-4, n_bands - 1, n_bands, dtype=jnp.float32)[None, :]
    ang = (2.0 * math.pi / L) * jnp.arange(L, dtype=jnp.float32)[:, None] * f
    z = jnp.concatenate([t, jnp.cos(ang), -jnp.sin(ang)], -1)
    freq = freq.astype(jnp.float32)
    h = jnp.sin(freq * (z @ w1 + b1))
    h = jnp.sin(freq * (h @ w2 + b2))
    h = (h @ w3 + b3).astype(jnp.float32).reshape(L, HY_ORDER, 2, GROUP_W)
    max_decay = math.log(HY_TARGET) / HY_FAST_DECAY
    min_decay = math.log(HY_TARGET) / HY_SLOW_DECAY
    deltas = jnp.linspace(min_decay, max_decay, GROUP_W, dtype=jnp.float32)
    h = h * jnp.exp(-t * jnp.abs(deltas))[:, None, None, :]
    return h / jnp.sum(jnp.abs(h), axis=(0, 2), keepdims=True)


def _bidir_fft_conv(u, h_pos, h_neg, bias):
    L = u.shape[1]
    h_full = jnp.concatenate([h_pos, jnp.zeros_like(h_pos[:1]), h_neg[:0:-1]], axis=0)
    uf = jnp.fft.rfft(u.astype(jnp.float32), n=2 * L, axis=1)
    hf = jnp.fft.rfft(h_full, n=2 * L, axis=0)
    y = jnp.fft.irfft(uf * hf[None], n=2 * L, axis=1)[:, :L]
    return y + u.astype(jnp.float32) * bias.astype(jnp.float32)


def _hyena_mixer(zc, zl, conv_w, conv_b, w1, b1, w2, b2, w3, b3, freq, bias, with_ctx):
    def run(z):
        L = z.shape[1]
        u = _centred_conv3(z, conv_w, conv_b)
        v, x1, x2 = jnp.split(u, 3, axis=-1)
        h = _hyena_filters(L, w1, b1, w2, b2, w3, b3, freq)
        y = v
        for n, gate in enumerate((x1, x2)):
            y = gate * _bidir_fft_conv(y, h[:, n, 0], h[:, n, 1], bias[n])
        return y.astype(z.dtype)
    return (run(zc) if with_ctx else None), run(zl)


def _rwkv7_scan(inputs, s0):
    r, w, k, v, kk, a = (jnp.moveaxis(t.astype(jnp.float32), 1, 0) for t in inputs)

    def step(S, xs):
        r_t, w_t, k_t, v_t, kk_t, a_t = xs
        sa = jnp.einsum('bhvk,bhk->bhv', S, -kk_t)
        S = S * w_t[:, :, None, :] + sa[..., :, None] * (kk_t * a_t)[..., None, :] + v_t[..., :, None] * k_t[..., None, :]
        return S, jnp.einsum('bhvk,bhk->bhv', S, r_t)

    s_fin, o = lax.scan(step, s0, (r, w, k, v, kk, a))
    return jnp.moveaxis(o, 0, 1), s_fin


def _rwkv7_mixer(zc, zl, mu, w0, w2, a0, a2, k_k, k_a, r_k, gn_g, gn_b, with_ctx):
    split_at = list(np.cumsum([GROUP_W] * 4 + [RW_LORA] * 3))

    def heads(t):
        return t.reshape(t.shape[0], t.shape[1], RW_HEADS, RW_HEAD)

    def prep(z):
        z = z + (_centred_shift(z) - z) * mu
        r, k, v, g, wd_f, wd_b, ad_f, ad_b = jnp.split(z, split_at, axis=-1)
        kk = heads((k * k_k).astype(jnp.float32))
        kk = kk * lax.rsqrt(jnp.maximum(jnp.sum(jnp.square(kk), -1, keepdims=True), 1e-24))
        dirs = []
        for d, (wd, ad) in enumerate(((wd_f, ad_f), (wd_b, ad_b))):
            w_log = -jax.nn.softplus(-(w0[d] + jnp.tanh(wd) @ w2[d])) - 0.5
            log_decay = -jnp.exp(w_log.astype(jnp.float32))
            a = jax.nn.sigmoid(a0[d] + ad @ a2[d])
            k_d = k * (1.0 + (a - 1.0) * k_a)
            dirs.append((log_decay[0], k_d[0], a[0]))
        return dirs, (r, k, v, g), kk.reshape(kk.shape[1], GROUP_W)

    def readout(o, r, k, v, g):
        B, L = o.shape[:2]
        y = _head_norm(o.reshape(B, L, GROUP_W), RW_HEADS, RW_GN_EPS) * gn_g + gn_b
        bonus = jnp.sum(heads(r * k) * r_k, -1, keepdims=True) * heads(v)
        return ((y + bonus.reshape(B, L, GROUP_W)) * jax.nn.sigmoid(g)).astype(g.dtype)

    c_dirs, c_rkvg, c_kk = prep(zc)
    l_dirs, l_rkvg, l_kk = prep(zl)
    n_ctx = zc.shape[1]
    cat = lambda a, b: jnp.concatenate([a, b], axis=0)
    per_dir = [jnp.stack([cat(c_dirs[d][i], l_dirs[d][i]) for d in range(2)]) for i in range(3)]
    o2 = rwkv_scan(cat(c_rkvg[0][0], l_rkvg[0][0]), cat(c_rkvg[2][0], l_rkvg[2][0]), cat(c_kk, l_kk),
                   per_dir[0], per_dir[1], per_dir[2], n_ctx)
    o = (o2[0] + o2[1])[None]
    oc, ol = o[:, :n_ctx], o[:, n_ctx:]
    return (readout(oc, *c_rkvg) if with_ctx else None), readout(ol, *l_rkvg)


def _retention_chunkwise(inputs, s0, log_gamma):
    q, k, v = (t.astype(jnp.float32) for t in inputs)
    B, L, H, D = q.shape
    N = L // RT_CHUNK
    q, k, v = (t.reshape(B, N, RT_CHUNK, H, D) for t in (q, k, v))
    idx = jnp.arange(RT_CHUNK, dtype=jnp.float32)
    rel = idx[:, None] - idx[None, :]
    decay = jnp.where(rel >= 0, jnp.exp(jnp.maximum(rel, 0.0)[None] * log_gamma[:, None, None]), 0.0)
    scores = jnp.einsum('bnihd,bnjhd->bnhij', q, k) * decay
    o = jnp.einsum('bnhij,bnjhd->bnihd', scores, v)
    k_w = jnp.exp((RT_CHUNK - 1 - idx)[:, None] * log_gamma)
    kv = jnp.einsum('bnjhd,bnjhe->nbhde', k * k_w[:, :, None], v)
    chunk_decay = jnp.exp(RT_CHUNK * log_gamma)[None, :, None, None]

    def step(S, kv_n):
        return chunk_decay * S + kv_n, S

    s_fin, s_prev = lax.scan(step, s0, kv)
    q_w = jnp.exp((idx + 1.0)[:, None] * log_gamma)
    o = o + jnp.einsum('bnihd,nbhde->bnihe', q * q_w[:, :, None], s_prev)
    return o.reshape(B, L, H, D), s_fin


def _retention_mixer(zc, zl, rows, cols, with_ctx):
    log_gamma = jnp.log1p(-jnp.exp2(-5.0 - jnp.arange(RT_HEADS, dtype=jnp.float32)))

    def prep(z, rotate):
        B, L = z.shape[:2]
        q, k, v, g = jnp.split(z, 4, axis=-1)
        q, k, v = (t.reshape(B, L, RT_HEADS, RT_HEAD) for t in (q, k, v))
        if rotate:
            q, k = _rope_2d(q, rows, cols), _rope_2d(k, rows, cols)
        return (q, k * RT_HEAD ** -0.5, v), g

    def scan_fn(xs, s0):
        return _retention_chunkwise(xs, s0, log_gamma)

    def readout(o, g):
        B, L = o.shape[:2]
        return (_head_norm(o.reshape(B, L, GROUP_W), RT_HEADS, 1e-6) * jax.nn.silu(g)).astype(g.dtype)

    c_in, gc = prep(zc, False)
    l_in, gl = prep(zl, True)
    n_ctx = zc.shape[1]
    q, k, v = (jnp.concatenate([a.reshape(n_ctx, GROUP_W), b.reshape(-1, GROUP_W)], axis=0) for a, b in zip(c_in, l_in))
    o2 = retention_scan(q, k, v, jnp.repeat(log_gamma, RT_HEAD)[None, :], n_ctx)
    o = (o2[0] + o2[1])[None]
    oc, ol = o[:, :n_ctx], o[:, n_ctx:]
    return (readout(oc, gc) if with_ctx else None), readout(ol, gl)


def _gla_chunkwise(inputs, s0):
    q, log_f, k, v = (t.astype(jnp.float32) for t in inputs)
    B, L, H, _ = q.shape
    N = L // HG_CHUNK

    def chunks(t):
        return t.reshape(B, N, HG_CHUNK, H, t.shape[-1]).transpose(1, 0, 3, 2, 4)

    causal = jnp.tril(jnp.ones((HG_CHUNK, HG_CHUNK), bool))[:, :, None]

    def step(S, xs):
        q_c, lf_c, k_c, v_c = xs
        b = jnp.cumsum(lf_c, axis=2)
        diff = b[:, :, :, None, :] - b[:, :, None, :, :]
        dec = jnp.where(causal, jnp.exp(jnp.where(causal, diff, 0.0)), 0.0)
        A = jnp.einsum('bhtd,bhsd,bhtsd->bhts', q_c, k_c, dec)
        o = jnp.einsum('bhts,bhsv->bhtv', A, v_c) + jnp.einsum('bhtd,bhdv->bhtv', q_c * jnp.exp(b), S)
        b_last = b[:, :, -1:, :]
        S = jnp.exp(b_last[:, :, 0, :])[..., None] * S + jnp.einsum('bhsd,bhsv->bhdv', k_c * jnp.exp(b_last - b), v_c)
        return S, o

    s_fin, o = lax.scan(step, s0, tuple(chunks(t) for t in (q, log_f, k, v)))
    return o.transpose(1, 0, 3, 2, 4).reshape(B, L, H, v.shape[-1]), s_fin


def _hgrn2_mixer(zc, zl, lb_f, lb_b, norm_g, with_ctx):
    def heads(t):
        return t.reshape(t.shape[0], t.shape[1], HG_HEADS, HG_HEAD)

    def prep(z):
        q, f_f, f_b, i, g = jnp.split(z, 5, axis=-1)
        q = jax.nn.silu(q)
        dirs = []
        for f, lb in ((f_f, lb_f), (f_b, lb_b)):
            gate = lb + (1.0 - lb) * jax.nn.sigmoid(f.astype(jnp.float32))
            log_f = jnp.log(jnp.maximum(gate, HG_MIN_GATE))
            dirs.append((log_f[0], (1.0 - gate)[0]))
        return dirs, g, q[0], i[0]

    def readout(o, g):
        B, L = o.shape[:2]
        return (_head_norm(o.reshape(B, L, GROUP_W), HG_HEADS, 1e-6, rms=True) * norm_g * jax.nn.silu(g)).astype(g.dtype)

    c_dirs, gc, c_q, c_i = prep(zc)
    l_dirs, gl, l_q, l_i = prep(zl)
    n_ctx = zc.shape[1]
    cat = lambda a, b: jnp.concatenate([a, b], axis=0)
    lf2, k2 = (jnp.stack([cat(c_dirs[d][i], l_dirs[d][i]) for d in range(2)]) for i in range(2))
    o2 = gla_scan(cat(c_q, l_q), cat(c_i, l_i), lf2, k2, n_ctx)
    o = (o2[0] + o2[1])[None]
    oc, ol = o[:, :n_ctx], o[:, n_ctx:]
    return (readout(oc, gc) if with_ctx else None), readout(ol, gl)


def kernel(x, c, ctx, c_ctx, ada_w, ada_b, w_in, w_out, ln_g, ln_b, hy_conv_w, hy_conv_b, hy_w1, hy_b1, hy_w2, hy_b2, hy_w3, hy_b3, hy_freq, hy_bias, rw_mu, rw_w0, rw_w2, rw_a0, rw_a2, rw_k_k, rw_k_a, rw_r_k, rw_gn_g, rw_gn_b, hg_lb_raw, hg_norm_g, ffn_w1, ffn_w3, ffn_w2):
    L = x.shape[1]
    n_rows = L // GRID_W
    rows = jnp.repeat(jnp.arange(n_rows, dtype=jnp.float32), GRID_W)
    cols = jnp.tile(jnp.arange(GRID_W, dtype=jnp.float32), n_rows)
    sm = jax.nn.softmax(hg_lb_raw.astype(jnp.float32), axis=1)
    lower_bounds = jnp.cumsum(sm, axis=1) - sm[:, :1]
    split_at = list(np.cumsum([HY_COLS, RW_COLS, RT_COLS]))

    c8 = jnp.zeros((8, D_MODEL), jnp.float32).at[0].set(c[0]).at[1].set(c_ctx)
    xl, xc = x[0], ctx[0]
    for l in range(DEPTH):
        with_ctx = l < DEPTH - 1
        mod = ada_modulation(c8, ada_w[l], ada_b[l][None, :])
        mod_l = [mod[0:1, i * D_MODEL:(i + 1) * D_MODEL] for i in range(6)]
        mod_c = [mod[1:2, i * D_MODEL:(i + 1) * D_MODEL] for i in range(6)]
        w_in_b = jnp.pad(w_in[l], ((0, 0), (0, P_IN_PAD - P_IN))).astype(jnp.bfloat16)
        w_out_b = w_out[l].astype(jnp.bfloat16)
        w1_b, w3_b, w2_b = (w[l].astype(jnp.bfloat16) for w in (ffn_w1, ffn_w3, ffn_w2))

        zl = modulated_projection(xl, mod_l[0], mod_l[1], w_in_b)[None, :, :P_IN]
        zc = modulated_projection(xc, mod_c[0], mod_c[1], w_in_b)[None, :, :P_IN]
        zl_hy, zl_rw, zl_rt, zl_hg = jnp.split(zl, split_at, axis=-1)
        zc_hy, zc_rw, zc_rt, zc_hg = jnp.split(zc, split_at, axis=-1)

        oc_hy, ol_hy = _hyena_mixer(zc_hy, zl_hy, hy_conv_w[l], hy_conv_b[l], hy_w1[l], hy_b1[l], hy_w2[l], hy_b2[l],
                                    hy_w3[l], hy_b3[l], hy_freq[l], hy_bias[l], with_ctx)
        oc_rw, ol_rw = _rwkv7_mixer(zc_rw, zl_rw, rw_mu[l], rw_w0[l], rw_w2[l], rw_a0[l], rw_a2[l], rw_k_k[l],
                                    rw_k_a[l], rw_r_k[l], rw_gn_g[l], rw_gn_b[l], with_ctx)
        oc_rt, ol_rt = _retention_mixer(zc_rt, zl_rt, rows, cols, with_ctx)
        oc_hg, ol_hg = _hgrn2_mixer(zc_hg, zl_hg, lower_bounds[0, l], lower_bounds[1, l], hg_norm_g[l], with_ctx)

        g0, b0 = ln_g[l, 0][None, :], ln_b[l, 0][None, :]
        g1, b1 = ln_g[l, 1][None, :], ln_b[l, 1][None, :]
        yl = jnp.concatenate([ol_hy, ol_rw, ol_rt, ol_hg], axis=-1)[0]
        xl = outproj_deepnorm(yl, w_out_b, xl, mod_l[2], g0, b0)
        xl = ffn_deepnorm(xl, mod_l[3], mod_l[4], w1_b, w3_b, w2_b, mod_l[5], g1, b1)
        if with_ctx:
            yc = jnp.concatenate([oc_hy, oc_rw, oc_rt, oc_hg], axis=-1)[0]
            xc = outproj_deepnorm(yc, w_out_b, xc, mod_c[2], g0, b0)
            xc = ffn_deepnorm(xc, mod_c[3], mod_c[4], w1_b, w3_b, w2_b, mod_c[5], g1, b1)
    return xl[None]
```

```python
import functools
import math

import jax
import jax.numpy as jnp
import numpy as np
from jax import lax
from jax.experimental import pallas as pl
from jax.experimental.pallas import tpu as pltpu

D_MODEL = 2048
DEPTH = 2
GRID_W = 64
N_MIXERS = 4
GROUP_W = D_MODEL // N_MIXERS
HY_ORDER = 2
HY_EMB = 33
HY_FAST_DECAY = 0.3
HY_SLOW_DECAY = 1.5
HY_TARGET = 1e-2
RW_HEAD = 64
RW_HEADS = GROUP_W // RW_HEAD
RW_LORA = 96
RW_GN_EPS = 64e-5
RT_HEAD = 64
RT_HEADS = GROUP_W // RT_HEAD
RT_CHUNK = 128
ROPE_BASE = 10000.0
HG_HEAD = 128
HG_HEADS = GROUP_W // HG_HEAD
HG_CHUNK = 64
HG_MIN_GATE = 1e-30
FFN_HIDDEN = 5632
ALPHA = (2 * DEPTH) ** 0.25
LN_EPS = 1e-6
HY_COLS = 3 * GROUP_W
RW_COLS = 4 * GROUP_W + 4 * RW_LORA
RT_COLS = 4 * GROUP_W
HG_COLS = 5 * GROUP_W
P_IN = HY_COLS + RW_COLS + RT_COLS + HG_COLS

LANES = 128
P_IN_PAD = 8704
PROJ_TN = 512
FFN_TF = 512
VMEM_LIMIT = 56 * 1024 * 1024


def _row_tile(m):
    return 512 if m % 512 == 0 else m


def _ln_rows(x):
    mu = jnp.mean(x, axis=-1, keepdims=True)
    xc = x - mu
    var = jnp.mean(xc * xc, axis=-1, keepdims=True)
    return xc * lax.rsqrt(var + LN_EPS)


def _ada_kernel(c_ref, w_ref, b_ref, o_ref):
    c = c_ref[...]
    h = c * jax.nn.sigmoid(c)
    o_ref[...] = jnp.dot(h.astype(jnp.bfloat16), w_ref[...].astype(jnp.bfloat16),
                         preferred_element_type=jnp.float32) + b_ref[...]


def ada_modulation(c8, w, b):
    n = w.shape[1]
    tn = 1024
    return pl.pallas_call(
        _ada_kernel,
        grid=(n // tn,),
        in_specs=[pl.BlockSpec((8, D_MODEL), lambda j: (0, 0)),
                  pl.BlockSpec((D_MODEL, tn), lambda j: (0, j)),
                  pl.BlockSpec((1, tn), lambda j: (0, j))],
        out_specs=pl.BlockSpec((8, tn), lambda j: (0, j)),
        out_shape=jax.ShapeDtypeStruct((8, n), jnp.float32),
        compiler_params=pltpu.CompilerParams(dimension_semantics=("arbitrary",),
                                             vmem_limit_bytes=VMEM_LIMIT),
    )(c8, w, b)


def _proj_kernel(x_ref, sh_ref, sc_ref, w_ref, o_ref, h_ref):
    @pl.when(pl.program_id(1) == 0)
    def _():
        h = _ln_rows(x_ref[...]) * (1.0 + sc_ref[...]) + sh_ref[...]
        h_ref[...] = h.astype(jnp.bfloat16)

    o_ref[...] = jnp.dot(h_ref[...], w_ref[...], preferred_element_type=jnp.float32)


def modulated_projection(x, shift, scale, w_bf16):
    m = x.shape[0]
    n = w_bf16.shape[1]
    tm = _row_tile(m)
    return pl.pallas_call(
        _proj_kernel,
        grid=(m // tm, n // PROJ_TN),
        in_specs=[pl.BlockSpec((tm, D_MODEL), lambda i, j: (i, 0)),
                  pl.BlockSpec((1, D_MODEL), lambda i, j: (0, 0)),
                  pl.BlockSpec((1, D_MODEL), lambda i, j: (0, 0)),
                  pl.BlockSpec((D_MODEL, PROJ_TN), lambda i, j: (0, j))],
        out_specs=pl.BlockSpec((tm, PROJ_TN), lambda i, j: (i, j)),
        out_shape=jax.ShapeDtypeStruct((m, n), jnp.float32),
        scratch_shapes=[pltpu.VMEM((tm, D_MODEL), jnp.bfloat16)],
        compiler_params=pltpu.CompilerParams(dimension_semantics=("arbitrary", "arbitrary"),
                                             vmem_limit_bytes=VMEM_LIMIT),
    )(x, shift, scale, w_bf16)


def _outproj_kernel(y_ref, w_ref, x_ref, gate_ref, g_ref, b_ref, o_ref):
    y = jnp.dot(y_ref[...].astype(jnp.bfloat16), w_ref[...], preferred_element_type=jnp.float32)
    r = ALPHA * x_ref[...] + gate_ref[...] * y
    o_ref[...] = _ln_rows(r) * g_ref[...] + b_ref[...]


def outproj_deepnorm(y, w_bf16, x, gate, g, b):
    m = x.shape[0]
    tm = _row_tile(m)
    row = pl.BlockSpec((tm, D_MODEL), lambda i: (i, 0))
    vec = pl.BlockSpec((1, D_MODEL), lambda i: (0, 0))
    return pl.pallas_call(
        _outproj_kernel,
        grid=(m // tm,),
        in_specs=[row, pl.BlockSpec((D_MODEL, D_MODEL), lambda i: (0, 0)), row, vec, vec, vec],
        out_specs=row,
        out_shape=jax.ShapeDtypeStruct((m, D_MODEL), jnp.float32),
        compiler_params=pltpu.CompilerParams(dimension_semantics=("arbitrary",),
                                             vmem_limit_bytes=VMEM_LIMIT),
    )(y, w_bf16, x, gate, g, b)


def _ffn_kernel(x_ref, sh_ref, sc_ref, w1_ref, w3_ref, w2_ref, gate_ref, g_ref, b_ref, o_ref, h_ref, acc_ref):
    j = pl.program_id(1)

    @pl.when(j == 0)
    def _():
        h = _ln_rows(x_ref[...]) * (1.0 + sc_ref[...]) + sh_ref[...]
        h_ref[...] = h.astype(jnp.bfloat16)
        acc_ref[...] = jnp.zeros_like(acc_ref)

    h = h_ref[...]
    a = jnp.dot(h, w1_ref[...], preferred_element_type=jnp.float32)
    u = jnp.dot(h, w3_ref[...], preferred_element_type=jnp.float32)
    s = (a * jax.nn.sigmoid(a) * u).astype(jnp.bfloat16)
    acc_ref[...] += jnp.dot(s, w2_ref[...], preferred_element_type=jnp.float32)

    @pl.when(j == pl.num_programs(1) - 1)
    def _():
        r = ALPHA * x_ref[...] + gate_ref[...] * acc_ref[...]
        o_ref[...] = _ln_rows(r) * g_ref[...] + b_ref[...]


def ffn_deepnorm(x, shift, scale, w1, w3, w2, gate, g, b):
    m = x.shape[0]
    tm = _row_tile(m)
    row = pl.BlockSpec((tm, D_MODEL), lambda i, j: (i, 0))
    vec = pl.BlockSpec((1, D_MODEL), lambda i, j: (0, 0))
    return pl.pallas_call(
        _ffn_kernel,
        grid=(m // tm, FFN_HIDDEN // FFN_TF),
        in_specs=[row, vec, vec,
                  pl.BlockSpec((D_MODEL, FFN_TF), lambda i, j: (0, j)),
                  pl.BlockSpec((D_MODEL, FFN_TF), lambda i, j: (0, j)),
                  pl.BlockSpec((FFN_TF, D_MODEL), lambda i, j: (j, 0)),
                  vec, vec, vec],
        out_specs=row,
        out_shape=jax.ShapeDtypeStruct((m, D_MODEL), jnp.float32),
        scratch_shapes=[pltpu.VMEM((tm, D_MODEL), jnp.bfloat16),
                        pltpu.VMEM((tm, D_MODEL), jnp.float32)],
        compiler_params=pltpu.CompilerParams(dimension_semantics=("arbitrary", "arbitrary"),
                                             vmem_limit_bytes=VMEM_LIMIT),
    )(x, shift, scale, w1, w3, w2, gate, g, b)


HI = lax.Precision.HIGHEST
SCAN_T = 64
GROUP_LANES = 256


def _dot(a, b):
    return jnp.dot(a, b, precision=HI, preferred_element_type=jnp.float32)


def _dot_nt(a, b):
    return lax.dot_general(a, b, (((1,), (1,)), ((), ())), precision=HI, preferred_element_type=jnp.float32)


def _dot_tn(a, b):
    return lax.dot_general(a, b, (((0,), (0,)), ((), ())), precision=HI, preferred_element_type=jnp.float32)


def _chunk_index(d, i, n_ctx, n_all):
    bwd = jnp.where(i < n_ctx, n_ctx - 1 - i, n_all + n_ctx - 1 - i)
    return jnp.where(d == 0, i, bwd)


def _rwkv_scan_kernel(r_ref, v_ref, kk_ref, lw_ref, k_ref, a_ref, o_ref, ht_ref, *, head):
    T = SCAN_T
    G = GROUP_LANES
    nh = G // head
    n = nh * T
    d = pl.program_id(0)

    @pl.when(pl.program_id(2) == 0)
    def _():
        ht_ref[...] = jnp.zeros_like(ht_ref)

    r, v, kk = r_ref[...], v_ref[...], kk_ref[...]
    lw, k, a = lw_ref[0], k_ref[0], a_ref[0]

    ti = lax.broadcasted_iota(jnp.int32, (T, T), 0)
    si = lax.broadcasted_iota(jnp.int32, (T, T), 1)
    sign = 1 - 2 * d
    before = (ti - si) * sign > 0
    tri_incl = jnp.where(before | (si == ti), 1.0, 0.0)
    c = _dot(tri_incl, lw)
    ctot = jnp.sum(lw, axis=0, keepdims=True)
    c_prev = c - lw
    beta = kk * a
    einv = jnp.exp(-c)
    efin = jnp.exp(ctot - c)
    a_bar = -kk * jnp.exp(c_prev)
    r_bar = r * jnp.exp(c)
    k_til, b_til = k * einv, beta * einv
    k_hat, b_hat = k * efin, beta * efin

    row_h = lax.broadcasted_iota(jnp.int32, (n, G), 0) // T
    col_h = lax.broadcasted_iota(jnp.int32, (n, G), 1) // head
    same = row_h == col_h

    def bd(x):
        return jnp.where(same, jnp.concatenate([x] * nh, axis=0), 0.0)

    rt = lax.broadcasted_iota(jnp.int32, (n, n), 0)
    cs = lax.broadcasted_iota(jnp.int32, (n, n), 1)
    blk = (rt // T) == (cs // T)
    before_bd = blk & ((rt - cs) * sign > 0)
    incl_bd = before_bd | (rt == cs)

    lhs = jnp.concatenate([bd(a_bar), bd(r_bar)], axis=0)
    rhs = jnp.concatenate([bd(k_til), bd(b_til)], axis=0)
    m = _dot_nt(lhs, rhs)
    ak, ab, rk, rb = m[:n, :n], m[:n, n:], m[n:, :n], m[n:, n:]
    ht = ht_ref[...]
    g = _dot_nt(lhs, ht)
    v_bd = bd(v)
    lb = jnp.where(before_bd, ab, 0.0)
    x = g[:n] + _dot(jnp.where(before_bd, ak, 0.0), v_bd)
    x = x + _dot(lb, x)
    lp = lb
    p = 2
    while p < T:
        lp = _dot(lp, lp)
        x = x + _dot(lp, x)
        p *= 2
    o = g[n:] + _dot(jnp.where(incl_bd, rk, 0.0), v_bd) + _dot(jnp.where(incl_bd, rb, 0.0), x)
    acc = o[0:T]
    for h in range(1, nh):
        acc = acc + o[h * T:(h + 1) * T]
    o_ref[0] = acc
    ht_ref[...] = ht * jnp.exp(ctot) + _dot_tn(v_bd, bd(k_hat)) + _dot_tn(x, bd(b_hat))


def rwkv_scan(r, v, kk, lw2, k2, a2, n_ctx_rows):
    N, W = r.shape
    T = SCAN_T
    n_all, n_ctx = N // T, n_ctx_rows // T
    shared = pl.BlockSpec((T, GROUP_LANES), lambda d, g, i: (_chunk_index(d, i, n_ctx, n_all), g))
    per_dir = pl.BlockSpec((1, T, GROUP_LANES), lambda d, g, i: (d, _chunk_index(d, i, n_ctx, n_all), g))
    return pl.pallas_call(
        functools.partial(_rwkv_scan_kernel, head=RW_HEAD),
        grid=(2, W // GROUP_LANES, n_all),
        in_specs=[shared, shared, shared, per_dir, per_dir, per_dir],
        out_specs=per_dir,
        out_shape=jax.ShapeDtypeStruct((2, N, W), jnp.float32),
        scratch_shapes=[pltpu.VMEM((GROUP_LANES, GROUP_LANES), jnp.float32)],
        compiler_params=pltpu.CompilerParams(dimension_semantics=("arbitrary", "arbitrary", "arbitrary"),
                                             vmem_limit_bytes=VMEM_LIMIT),
    )(r, v, kk, lw2, k2, a2)


def _retention_scan_kernel(q_ref, k_ref, v_ref, lg_ref, o_ref, st_ref, *, head):
    T, G = SCAN_T, GROUP_LANES
    nh = G // head
    n = nh * T
    d = pl.program_id(0)
    sign = 1 - 2 * d

    @pl.when(pl.program_id(2) == 0)
    def _():
        st_ref[...] = jnp.zeros_like(st_ref)

    q, k, v, lg = q_ref[...], k_ref[...], v_ref[...], lg_ref[...]
    t = lax.broadcasted_iota(jnp.int32, (T, 1), 0)
    pos = (t + d * (T - 1 - 2 * t) + 1).astype(jnp.float32)
    c = pos * lg
    ctot = float(T) * lg
    q_bar = q * jnp.exp(c)
    k_til = k * jnp.exp(-c)
    k_hat = k * jnp.exp(ctot - c)

    row_h = lax.broadcasted_iota(jnp.int32, (n, G), 0) // T
    col_h = lax.broadcasted_iota(jnp.int32, (n, G), 1) // head
    same = row_h == col_h

    def bd(x):
        return jnp.where(same, jnp.concatenate([x] * nh, axis=0), 0.0)

    rt = lax.broadcasted_iota(jnp.int32, (n, n), 0)
    cs = lax.broadcasted_iota(jnp.int32, (n, n), 1)
    incl_bd = ((rt // T) == (cs // T)) & ((rt - cs) * sign >= 0)

    q_bd, v_bd = bd(q_bar), bd(v)
    st = st_ref[...]
    scores = jnp.where(incl_bd, _dot_nt(q_bd, bd(k_til)), 0.0)
    o = _dot_nt(q_bd, st) + _dot(scores, v_bd)
    acc = o[0:T]
    for h in range(1, nh):
        acc = acc + o[h * T:(h + 1) * T]
    o_ref[0] = acc
    st_ref[...] = st * jnp.exp(ctot) + _dot_tn(v_bd, bd(k_hat))


def retention_scan(q, k, v, lg, n_ctx_rows):
    N, W = q.shape
    T = SCAN_T
    n_all, n_ctx = N // T, n_ctx_rows // T
    shared = pl.BlockSpec((T, GROUP_LANES), lambda d, g, i: (_chunk_index(d, i, n_ctx, n_all), g))
    return pl.pallas_call(
        functools.partial(_retention_scan_kernel, head=RT_HEAD),
        grid=(2, W // GROUP_LANES, n_all),
        in_specs=[shared, shared, shared, pl.BlockSpec((1, GROUP_LANES), lambda d, g, i: (0, g))],
        out_specs=pl.BlockSpec((1, T, GROUP_LANES), lambda d, g, i: (d, _chunk_index(d, i, n_ctx, n_all), g)),
        out_shape=jax.ShapeDtypeStruct((2, N, W), jnp.float32),
        scratch_shapes=[pltpu.VMEM((GROUP_LANES, GROUP_LANES), jnp.float32)],
        compiler_params=pltpu.CompilerParams(dimension_semantics=("arbitrary", "arbitrary", "arbitrary"),
                                             vmem_limit_bytes=VMEM_LIMIT),
    )(q, k, v, lg)


HG_SUB = 16


def _gla_scan_kernel(q_ref, v_ref, lf_ref, k_ref, o_ref, st_ref, *, head):
    T, S = SCAN_T, HG_SUB
    W = q_ref.shape[-1]
    nh = W // head
    d = pl.program_id(0)
    sign = 1 - 2 * d

    @pl.when(pl.program_id(1) == 0)
    def _():
        st_ref[...] = jnp.zeros_like(st_ref)

    ti = lax.broadcasted_iota(jnp.int32, (S, S), 0)
    si = lax.broadcasted_iota(jnp.int32, (S, S), 1)
    tri_incl = jnp.where((ti - si) * sign >= 0, 1.0, 0.0)
    row = lax.broadcasted_iota(jnp.int32, (S, 1), 0)

    for j in range(T // S):
        jb = j + d * (T // S - 1 - 2 * j)
        rows = pl.ds(pl.multiple_of(jb * S, S), S)
        q, v = q_ref[rows, :], v_ref[rows, :]
        lf, k = lf_ref[0, rows, :], k_ref[0, rows, :]
        b = _dot(tri_incl, lf)
        btot = jnp.sum(lf, axis=0, keepdims=True)
        qe = q * jnp.exp(b)
        ke = k * jnp.exp(btot - b)
        for h in range(nh):
            ls = slice(h * head, (h + 1) * head)
            qh, kh, vh, bh = q[:, ls], k[:, ls], v[:, ls], b[:, ls]
            st = st_ref[h]
            o = _dot_nt(qe[:, ls], st)
            for s in range(S):
                e = jnp.exp(jnp.minimum(bh - bh[s:s + 1, :], 0.0))
                a_s = jnp.sum(qh * kh[s:s + 1, :] * e, axis=-1, keepdims=True)
                a_s = jnp.where((row - s) * sign >= 0, a_s, 0.0)
                o = o + a_s * vh[s:s + 1, :]
            o_ref[0, rows, ls] = o
            st_ref[h] = st * jnp.exp(btot[:, ls]) + _dot_tn(vh, ke[:, ls])


def gla_scan(q, v, lf2, k2, n_ctx_rows):
    N, W = q.shape
    T = SCAN_T
    n_all, n_ctx = N // T, n_ctx_rows // T
    shared = pl.BlockSpec((T, W), lambda d, i: (_chunk_index(d, i, n_ctx, n_all), 0))
    per_dir = pl.BlockSpec((1, T, W), lambda d, i: (d, _chunk_index(d, i, n_ctx, n_all), 0))
    return pl.pallas_call(
        functools.partial(_gla_scan_kernel, head=HG_HEAD),
        grid=(2, n_all),
        in_specs=[shared, shared, per_dir, per_dir],
        out_specs=per_dir,
        out_shape=jax.ShapeDtypeStruct((2, N, W), jnp.float32),
        scratch_shapes=[pltpu.VMEM((W // HG_HEAD, HG_HEAD, HG_HEAD), jnp.float32)],
        compiler_params=pltpu.CompilerParams(dimension_semantics=("arbitrary", "arbitrary"),
                                             vmem_limit_bytes=VMEM_LIMIT),
    )(q, v, lf2, k2)


HY_EMB_PAD = 40
HY_FILTER_ROWS = 512


def hyena_features(L):
    t = jnp.linspace(0.0, 1.0, L, dtype=jnp.float32)[:, None]
    n_bands = (HY_EMB - 1) // 2
    f = jnp.linspace(1e-4, n_bands - 1, n_bands, dtype=jnp.float32)[None, :]
    ang = (2.0 * math.pi / L) * jnp.arange(L, dtype=jnp.float32)[:, None] * f
    z = jnp.concatenate([t, jnp.cos(ang), -jnp.sin(ang)], -1)
    return jnp.pad(z, ((0, 0), (0, HY_EMB_PAD - HY_EMB)))


def _filter_kernel(z_ref, w1_ref, b1_ref, w2_ref, b2_ref, w3_ref, b3_ref, fr_ref, dl_ref, h_ref, s_ref):
    i = pl.program_id(0)
    z = z_ref[...]
    fr = fr_ref[...]
    h = jnp.sin(fr * (_dot(z, w1_ref[...]) + b1_ref[...]))
    h = jnp.sin(fr * (_dot(h, w2_ref[...]) + b2_ref[...]))
    h = _dot(h, w3_ref[...]) + b3_ref[...]
    win = jnp.exp(-z[:, 0:1] * dl_ref[...])
    h = h * jnp.concatenate([win] * (h.shape[1] // win.shape[1]), axis=1)

    @pl.when(i == 0)
    def _():
        s_ref[...] = jnp.zeros_like(s_ref)

    s_ref[...] += jnp.sum(jnp.abs(h), axis=0, keepdims=True)
    row = lax.broadcasted_iota(jnp.int32, h.shape, 0) + i * h.shape[0]
    col = lax.broadcasted_iota(jnp.int32, h.shape, 1)
    neg = (col // GROUP_W) % 2 == 1
    h_ref[...] = jnp.where(neg & (row == 0), 0.0, h)


def hyena_filter_bank(L, w1, b1, w2, b2, w3, b3, freq):
    z = hyena_features(L)
    w1p = jnp.pad(w1, ((0, HY_EMB_PAD - HY_EMB), (0, 0)))
    max_decay = math.log(HY_TARGET) / HY_FAST_DECAY
    min_decay = math.log(HY_TARGET) / HY_SLOW_DECAY
    deltas = jnp.abs(jnp.linspace(min_decay, max_decay, GROUP_W, dtype=jnp.float32))[None, :]
    n = w3.shape[1]
    tr = min(L, HY_FILTER_ROWS)
    full = lambda a: pl.BlockSpec(a.shape, lambda i: (0,) * a.ndim)
    args = (z, w1p, b1[None, :], w2, b2[None, :], w3, b3[None, :], freq[None, :], deltas)
    return pl.pallas_call(
        _filter_kernel,
        grid=(L // tr,),
        in_specs=[pl.BlockSpec((tr, HY_EMB_PAD), lambda i: (i, 0))] + [full(a) for a in args[1:]],
        out_specs=[pl.BlockSpec((tr, n), lambda i: (i, 0)), pl.BlockSpec((1, n), lambda i: (0, 0))],
        out_shape=[jax.ShapeDtypeStruct((L, n), jnp.float32), jax.ShapeDtypeStruct((1, n), jnp.float32)],
        compiler_params=pltpu.CompilerParams(dimension_semantics=("arbitrary",), vmem_limit_bytes=VMEM_LIMIT),
    )(*args)


FFT_NB = 8
FFT_CB_WIDE = 512
FFT_CB = 256


def _split(x):
    hi = x.astype(jnp.bfloat16)
    lo = (x - hi.astype(jnp.float32)).astype(jnp.bfloat16)
    return hi, lo


def _dot3(a_hi, a_lo, b_hi, b_lo):
    d = functools.partial(jnp.dot, preferred_element_type=jnp.float32)
    return d(a_hi, b_hi) + (d(a_hi, b_lo) + d(a_lo, b_hi))


def fft_tables(n1, n2):
    N = n1 * n2
    a = jnp.arange(n1, dtype=jnp.float32)[:, None, None]
    k2 = jnp.arange(n2, dtype=jnp.float32)[None, :, None]
    b = jnp.arange(n2 // 2, dtype=jnp.float32)[None, None, :]
    ph = (jnp.mod(a * k2, float(N)) / N + jnp.mod(b * k2, float(n2)) / n2) * (-2.0 * math.pi)
    g = jnp.concatenate([jnp.cos(ph), jnp.sin(ph)], axis=1)
    ginv = jnp.transpose(g, (0, 2, 1)) / N
    k1 = jnp.arange(n1, dtype=jnp.float32)[:, None]
    aa = jnp.arange(n1, dtype=jnp.float32)[None, :]
    f = jnp.mod(k1 * aa, float(n1)) * (-2.0 * math.pi / n1)
    fr, fi = jnp.cos(f), jnp.sin(f)
    ff = jnp.concatenate([jnp.concatenate([fr, -fi], axis=1), jnp.concatenate([fi, fr], axis=1)], axis=0)
    return {"g": _split(g), "ginv": _split(ginv), "ff": _split(ff), "fft": _split(ff.T)}


def _stage_a_kernel(u_ref, gh_ref, gl_ref, o_ref):
    for j in range(FFT_NB):
        xh, xl = _split(u_ref[:, j, :])
        o_ref[j] = _dot3(gh_ref[j], gl_ref[j], xh, xl)


def fft_stage_a(u3, g):
    nb, n1, C = u3.shape
    cb = min(C, FFT_CB_WIDE)
    gspec = pl.BlockSpec((FFT_NB, 4 * nb, nb), lambda c, i: (i, 0, 0))
    return pl.pallas_call(
        _stage_a_kernel,
        grid=(C // cb, n1 // FFT_NB),
        in_specs=[pl.BlockSpec((nb, FFT_NB, cb), lambda c, i: (0, i, c)), gspec, gspec],
        out_specs=pl.BlockSpec((FFT_NB, 4 * nb, cb), lambda c, i: (i, 0, c)),
        out_shape=jax.ShapeDtypeStruct((n1, 4 * nb, C), jnp.float32),
        compiler_params=pltpu.CompilerParams(dimension_semantics=("arbitrary", "arbitrary"),
                                             vmem_limit_bytes=VMEM_LIMIT),
    )(u3, g[0], g[1])


def _stage_b_filter_kernel(re_ref, im_ref, fh_ref, fl_ref, o_ref):
    for j in range(FFT_NB):
        th, tl = _split(jnp.concatenate([re_ref[:, j, :], im_ref[:, j, :]], axis=0))
        o_ref[j] = _dot3(fh_ref[...], fl_ref[...], th, tl)


def fft_stage_b_filter(t1, ff):
    n1, n2x2, C = t1.shape
    n2 = n2x2 // 2
    cb = min(C, FFT_CB)
    blk = lambda off: pl.BlockSpec((n1, FFT_NB, cb), lambda c, i: (0, i + off, c))
    mat = pl.BlockSpec((2 * n1, 2 * n1), lambda c, i: (0, 0))
    return pl.pallas_call(
        _stage_b_filter_kernel,
        grid=(C // cb, n2 // FFT_NB),
        in_specs=[blk(0), blk(n2 // FFT_NB), mat, mat],
        out_specs=pl.BlockSpec((FFT_NB, 2 * n1, cb), lambda c, i: (i, 0, c)),
        out_shape=jax.ShapeDtypeStruct((n2, 2 * n1, C), jnp.float32),
        compiler_params=pltpu.CompilerParams(dimension_semantics=("arbitrary", "arbitrary"),
                                             vmem_limit_bytes=VMEM_LIMIT),
    )(t1, t1, ff[0], ff[1])


def _stage_b_conv_kernel(re_ref, im_ref, p_ref, q_ref, s_ref, fh_ref, fl_ref, fth_ref, ftl_ref, ore_ref, oim_ref):
    n1 = re_ref.shape[0]
    s = s_ref[...]
    for j in range(FFT_NB):
        th, tl = _split(jnp.concatenate([re_ref[:, j, :], im_ref[:, j, :]], axis=0))
        x = _dot3(fh_ref[...], fl_ref[...], th, tl)
        p, q = p_ref[j], q_ref[j]
        hr = (p[:n1] + q[:n1]) * s
        hi = (p[n1:] - q[n1:]) * s
        xr, xi = x[:n1], x[n1:]
        yh, yl = _split(jnp.concatenate([xr * hr - xi * hi, xr * hi + xi * hr], axis=0))
        z = _dot3(fth_ref[...], ftl_ref[...], yh, yl)
        ore_ref[:, j, :] = z[:n1]
        oim_ref[:, j, :] = z[n1:]


def fft_stage_b_conv(t1, spec, col_p, col_q, inv_norm, ff, fft_):
    n1, n2x2, C = t1.shape
    n2 = n2x2 // 2
    cb = min(C, FFT_CB)
    ncb = C // cb
    blk = lambda off: pl.BlockSpec((n1, FFT_NB, cb), lambda c, i: (0, i + off, c))
    mat = pl.BlockSpec((2 * n1, 2 * n1), lambda c, i: (0, 0))
    sp = lambda col: pl.BlockSpec((FFT_NB, 2 * n1, cb), lambda c, i: (i, 0, col * ncb + c))
    return pl.pallas_call(
        _stage_b_conv_kernel,
        grid=(ncb, n2 // FFT_NB),
        in_specs=[blk(0), blk(n2 // FFT_NB), sp(col_p), sp(col_q), pl.BlockSpec((1, cb), lambda c, i: (0, c)),
                  mat, mat, mat, mat],
        out_specs=[blk(0), blk(0)],
        out_shape=[jax.ShapeDtypeStruct((n1, n2, C), jnp.float32)] * 2,
        compiler_params=pltpu.CompilerParams(dimension_semantics=("arbitrary", "arbitrary"),
                                             vmem_limit_bytes=VMEM_LIMIT),
    )(t1, t1, spec, spec, inv_norm, ff[0], ff[1], fft_[0], fft_[1])


def _stage_a_inv_kernel(re_ref, im_ref, gh_ref, gl_ref, u_ref, gate_ref, bias_ref, o_ref):
    for j in range(FFT_NB):
        th, tl = _split(jnp.concatenate([re_ref[j], im_ref[j]], axis=0))
        y = _dot3(gh_ref[j], gl_ref[j], th, tl)
        o_ref[:, j, :] = gate_ref[:, j, :] * (y + u_ref[:, j, :] * bias_ref[...])


def fft_stage_a_inv(t2re, t2im, ginv, u3, gate3, bias):
    n1, n2, C = t2re.shape
    nb = n2 // 2
    cb = min(C, FFT_CB_WIDE)
    tb = pl.BlockSpec((FFT_NB, n2, cb), lambda c, i: (i, 0, c))
    gb = pl.BlockSpec((FFT_NB, nb, 2 * n2), lambda c, i: (i, 0, 0))
    ub = pl.BlockSpec((nb, FFT_NB, cb), lambda c, i: (0, i, c))
    return pl.pallas_call(
        _stage_a_inv_kernel,
        grid=(C // cb, n1 // FFT_NB),
        in_specs=[tb, tb, gb, gb, ub, ub, pl.BlockSpec((1, cb), lambda c, i: (0, c))],
        out_specs=ub,
        out_shape=jax.ShapeDtypeStruct((nb, n1, C), jnp.float32),
        compiler_params=pltpu.CompilerParams(dimension_semantics=("arbitrary", "arbitrary"),
                                             vmem_limit_bytes=VMEM_LIMIT),
    )(t2re, t2im, ginv[0], ginv[1], u3, gate3, bias)


FFT_N1 = 128


def hyena_long_conv_chain(v, gates, biases, filt, colsum):
    L, C = v.shape
    n1 = FFT_N1
    n2 = 2 * L // n1
    tabs = fft_tables(n1, n2)
    spec = fft_stage_b_filter(fft_stage_a(filt.reshape(n2 // 2, n1, filt.shape[1]), tabs["g"]), tabs["ff"])
    s4 = colsum.reshape(HY_ORDER, 2, C)
    inv_norm = 1.0 / (s4[:, 0] + s4[:, 1])
    y3 = v.reshape(n2 // 2, n1, C)
    for n in range(HY_ORDER):
        t1 = fft_stage_a(y3, tabs["g"])
        t2re, t2im = fft_stage_b_conv(t1, spec, 2 * n, 2 * n + 1, inv_norm[n][None, :], tabs["ff"], tabs["fft"])
        y3 = fft_stage_a_inv(t2re, t2im, tabs["ginv"], y3, gates[n].reshape(n2 // 2, n1, C), biases[n][None, :])
    return y3.reshape(L, C)


def _head_norm(x, n_heads, eps, rms=False):
    B, L, W = x.shape
    xf = x.astype(jnp.float32).reshape(B, L, n_heads, W // n_heads)
    if not rms:
        xf = xf - jnp.mean(xf, -1, keepdims=True)
    y = xf * lax.rsqrt(jnp.mean(jnp.square(xf), -1, keepdims=True) + eps)
    return y.reshape(B, L, W)


def _centred_conv3(u, w, b):
    up = jnp.pad(u, ((0, 0), (1, 1), (0, 0)))
    return up[:, :-2] * w[0] + up[:, 1:-1] * w[1] + up[:, 2:] * w[2] + b


def _centred_shift(u):
    up = jnp.pad(u, ((0, 0), (1, 1), (0, 0)))
    return 0.5 * (up[:, :-2] + up[:, 2:])


def _rope_2d(x, rows, cols):
    d_axis = x.shape[-1] // 2
    inv = ROPE_BASE ** (-jnp.arange(0, d_axis, 2, dtype=jnp.float32) / d_axis)

    def rot(xa, pos):
        ang = pos[:, None, None] * inv
        cos, sin = jnp.cos(ang), jnp.sin(ang)
        x1, x2 = jnp.split(xa, 2, axis=-1)
        return jnp.concatenate([x1 * cos - x2 * sin, x1 * sin + x2 * cos], -1)

    xr, xc = jnp.split(x.astype(jnp.float32), 2, axis=-1)
    return jnp.concatenate([rot(xr, rows), rot(xc, cols)], -1)


def _flip_time(xs):
    return tuple(jnp.flip(a, axis=1) for a in xs)


def _bidir_with_context(scan_fn, ctx_fwd, lat_fwd, ctx_bwd, lat_bwd, s0):
    oc_f, sc_f = scan_fn(ctx_fwd, s0)
    ol_f, _ = scan_fn(lat_fwd, sc_f)
    oc_b, sc_b = scan_fn(_flip_time(ctx_bwd), s0)
    ol_b, _ = scan_fn(_flip_time(lat_bwd), sc_b)
    return oc_f + jnp.flip(oc_b, axis=1), ol_f + jnp.flip(ol_b, axis=1)


def _hyena_filters(L, w1, b1, w2, b2, w3, b3, freq):
    t = jnp.linspace(0.0, 1.0, L, dtype=jnp.float32)[:, None]
    n_bands = (HY_EMB - 1) // 2
    f = jnp.linspace(1e-4, n_bands - 1, n_bands, dtype=jnp.float32)[None, :]
    ang = (2.0 * math.pi / L) * jnp.arange(L, dtype=jnp.float32)[:, None] * f
    z = jnp.concatenate([t, jnp.cos(ang), -jnp.sin(ang)], -1)
    freq = freq.astype(jnp.float32)
    h = jnp.sin(freq * (z @ w1 + b1))
    h = jnp.sin(freq * (h @ w2 + b2))
    h = (h @ w3 + b3).astype(jnp.float32).reshape(L, HY_ORDER, 2, GROUP_W)
    max_decay = math.log(HY_TARGET) / HY_FAST_DECAY
    min_decay = math.log(HY_TARGET) / HY_SLOW_DECAY
    deltas = jnp.linspace(min_decay, max_decay, GROUP_W, dtype=jnp.float32)
    h = h * jnp.exp(-t * jnp.abs(deltas))[:, None, None, :]
    return h / jnp.sum(jnp.abs(h), axis=(0, 2), keepdims=True)


def _bidir_fft_conv(u, h_pos, h_neg, bias):
    L = u.shape[1]
    h_full = jnp.concatenate([h_pos, jnp.zeros_like(h_pos[:1]), h_neg[:0:-1]], axis=0)
    uf = jnp.fft.rfft(u.astype(jnp.float32), n=2 * L, axis=1)
    hf = jnp.fft.rfft(h_full, n=2 * L, axis=0)
    y = jnp.fft.irfft(uf * hf[None], n=2 * L, axis=1)[:, :L]
    return y + u.astype(jnp.float32) * bias.astype(jnp.float32)


def _hyena_mixer(zc, zl, conv_w, conv_b, w1, b1, w2, b2, w3, b3, freq, bias, with_ctx):
    def run(z):
        L = z.shape[1]
        u = _centred_conv3(z, conv_w, conv_b)
        v, x1, x2 = jnp.split(u, 3, axis=-1)
        h = _hyena_filters(L, w1, b1, w2, b2, w3, b3, freq)
        y = v
        for n, gate in enumerate((x1, x2)):
            y = gate * _bidir_fft_conv(y, h[:, n, 0], h[:, n, 1], bias[n])
        return y.astype(z.dtype)

    def run_latent(z):
        L = z.shape[1]
        u = _centred_conv3(z, conv_w, conv_b)[0]
        v, x1, x2 = jnp.split(u, 3, axis=-1)
        filt, colsum = hyena_filter_bank(L, w1, b1, w2, b2, w3, b3, freq)
        return hyena_long_conv_chain(v, (x1, x2), bias, filt, colsum)[None]

    return (run(zc) if with_ctx else None), run_latent(zl)


def _rwkv7_scan(inputs, s0):
    r, w, k, v, kk, a = (jnp.moveaxis(t.astype(jnp.float32), 1, 0) for t in inputs)

    def step(S, xs):
        r_t, w_t, k_t, v_t, kk_t, a_t = xs
        sa = jnp.einsum('bhvk,bhk->bhv', S, -kk_t)
        S = S * w_t[:, :, None, :] + sa[..., :, None] * (kk_t * a_t)[..., None, :] + v_t[..., :, None] * k_t[..., None, :]
        return S, jnp.einsum('bhvk,bhk->bhv', S, r_t)

    s_fin, o = lax.scan(step, s0, (r, w, k, v, kk, a))
    return jnp.moveaxis(o, 0, 1), s_fin


def _rwkv7_mixer(zc, zl, mu, w0, w2, a0, a2, k_k, k_a, r_k, gn_g, gn_b, with_ctx):
    split_at = list(np.cumsum([GROUP_W] * 4 + [RW_LORA] * 3))

    def heads(t):
        return t.reshape(t.shape[0], t.shape[1], RW_HEADS, RW_HEAD)

    def prep(z):
        z = z + (_centred_shift(z) - z) * mu
        r, k, v, g, wd_f, wd_b, ad_f, ad_b = jnp.split(z, split_at, axis=-1)
        kk = heads((k * k_k).astype(jnp.float32))
        kk = kk * lax.rsqrt(jnp.maximum(jnp.sum(jnp.square(kk), -1, keepdims=True), 1e-24))
        dirs = []
        for d, (wd, ad) in enumerate(((wd_f, ad_f), (wd_b, ad_b))):
            w_log = -jax.nn.softplus(-(w0[d] + jnp.tanh(wd) @ w2[d])) - 0.5
            log_decay = -jnp.exp(w_log.astype(jnp.float32))
            a = jax.nn.sigmoid(a0[d] + ad @ a2[d])
            k_d = k * (1.0 + (a - 1.0) * k_a)
            dirs.append((log_decay[0], k_d[0], a[0]))
        return dirs, (r, k, v, g), kk.reshape(kk.shape[1], GROUP_W)

    def readout(o, r, k, v, g):
        B, L = o.shape[:2]
        y = _head_norm(o.reshape(B, L, GROUP_W), RW_HEADS, RW_GN_EPS) * gn_g + gn_b
        bonus = jnp.sum(heads(r * k) * r_k, -1, keepdims=True) * heads(v)
        return ((y + bonus.reshape(B, L, GROUP_W)) * jax.nn.sigmoid(g)).astype(g.dtype)

    c_dirs, c_rkvg, c_kk = prep(zc)
    l_dirs, l_rkvg, l_kk = prep(zl)
    n_ctx = zc.shape[1]
    cat = lambda a, b: jnp.concatenate([a, b], axis=0)
    per_dir = [jnp.stack([cat(c_dirs[d][i], l_dirs[d][i]) for d in range(2)]) for i in range(3)]
    o2 = rwkv_scan(cat(c_rkvg[0][0], l_rkvg[0][0]), cat(c_rkvg[2][0], l_rkvg[2][0]), cat(c_kk, l_kk),
                   per_dir[0], per_dir[1], per_dir[2], n_ctx)
    o = (o2[0] + o2[1])[None]
    oc, ol = o[:, :n_ctx], o[:, n_ctx:]
    return (readout(oc, *c_rkvg) if with_ctx else None), readout(ol, *l_rkvg)


def _retention_chunkwise(inputs, s0, log_gamma):
    q, k, v = (t.astype(jnp.float32) for t in inputs)
    B, L, H, D = q.shape
    N = L // RT_CHUNK
    q, k, v = (t.reshape(B, N, RT_CHUNK, H, D) for t in (q, k, v))
    idx = jnp.arange(RT_CHUNK, dtype=jnp.float32)
    rel = idx[:, None] - idx[None, :]
    decay = jnp.where(rel >= 0, jnp.exp(jnp.maximum(rel, 0.0)[None] * log_gamma[:, None, None]), 0.0)
    scores = jnp.einsum('bnihd,bnjhd->bnhij', q, k) * decay
    o = jnp.einsum('bnhij,bnjhd->bnihd', scores, v)
    k_w = jnp.exp((RT_CHUNK - 1 - idx)[:, None] * log_gamma)
    kv = jnp.einsum('bnjhd,bnjhe->nbhde', k * k_w[:, :, None], v)
    chunk_decay = jnp.exp(RT_CHUNK * log_gamma)[None, :, None, None]

    def step(S, kv_n):
        return chunk_decay * S + kv_n, S

    s_fin, s_prev = lax.scan(step, s0, kv)
    q_w = jnp.exp((idx + 1.0)[:, None] * log_gamma)
    o = o + jnp.einsum('bnihd,nbhde->bnihe', q * q_w[:, :, None], s_prev)
    return o.reshape(B, L, H, D), s_fin


def _retention_mixer(zc, zl, rows, cols, with_ctx):
    log_gamma = jnp.log1p(-jnp.exp2(-5.0 - jnp.arange(RT_HEADS, dtype=jnp.float32)))

    def prep(z, rotate):
        B, L = z.shape[:2]
        q, k, v, g = jnp.split(z, 4, axis=-1)
        q, k, v = (t.reshape(B, L, RT_HEADS, RT_HEAD) for t in (q, k, v))
        if rotate:
            q, k = _rope_2d(q, rows, cols), _rope_2d(k, rows, cols)
        return (q, k * RT_HEAD ** -0.5, v), g

    def scan_fn(xs, s0):
        return _retention_chunkwise(xs, s0, log_gamma)

    def readout(o, g):
        B, L = o.shape[:2]
        return (_head_norm(o.reshape(B, L, GROUP_W), RT_HEADS, 1e-6) * jax.nn.silu(g)).astype(g.dtype)

    c_in, gc = prep(zc, False)
    l_in, gl = prep(zl, True)
    n_ctx = zc.shape[1]
    q, k, v = (jnp.concatenate([a.reshape(n_ctx, GROUP_W), b.reshape(-1, GROUP_W)], axis=0) for a, b in zip(c_in, l_in))
    o2 = retention_scan(q, k, v, jnp.repeat(log_gamma, RT_HEAD)[None, :], n_ctx)
    o = (o2[0] + o2[1])[None]
    oc, ol = o[:, :n_ctx], o[:, n_ctx:]
    return (readout(oc, gc) if with_ctx else None), readout(ol, gl)


def _gla_chunkwise(inputs, s0):
    q, log_f, k, v = (t.astype(jnp.float32) for t in inputs)
    B, L, H, _ = q.shape
    N = L // HG_CHUNK

    def chunks(t):
        return t.reshape(B, N, HG_CHUNK, H, t.shape[-1]).transpose(1, 0, 3, 2, 4)

    causal = jnp.tril(jnp.ones((HG_CHUNK, HG_CHUNK), bool))[:, :, None]

    def step(S, xs):
        q_c, lf_c, k_c, v_c = xs
        b = jnp.cumsum(lf_c, axis=2)
        diff = b[:, :, :, None, :] - b[:, :, None, :, :]
        dec = jnp.where(causal, jnp.exp(jnp.where(causal, diff, 0.0)), 0.0)
        A = jnp.einsum('bhtd,bhsd,bhtsd->bhts', q_c, k_c, dec)
        o = jnp.einsum('bhts,bhsv->bhtv', A, v_c) + jnp.einsum('bhtd,bhdv->bhtv', q_c * jnp.exp(b), S)
        b_last = b[:, :, -1:, :]
        S = jnp.exp(b_last[:, :, 0, :])[..., None] * S + jnp.einsum('bhsd,bhsv->bhdv', k_c * jnp.exp(b_last - b), v_c)
        return S, o

    s_fin, o = lax.scan(step, s0, tuple(chunks(t) for t in (q, log_f, k, v)))
    return o.transpose(1, 0, 3, 2, 4).reshape(B, L, H, v.shape[-1]), s_fin


def _hgrn2_mixer(zc, zl, lb_f, lb_b, norm_g, with_ctx):
    def heads(t):
        return t.reshape(t.shape[0], t.shape[1], HG_HEADS, HG_HEAD)

    def prep(z):
        q, f_f, f_b, i, g = jnp.split(z, 5, axis=-1)
        q = jax.nn.silu(q)
        dirs = []
        for f, lb in ((f_f, lb_f), (f_b, lb_b)):
            gate = lb + (1.0 - lb) * jax.nn.sigmoid(f.astype(jnp.float32))
            log_f = jnp.log(jnp.maximum(gate, HG_MIN_GATE))
            dirs.append((log_f[0], (1.0 - gate)[0]))
        return dirs, g, q[0], i[0]

    def readout(o, g):
        B, L = o.shape[:2]
        return (_head_norm(o.reshape(B, L, GROUP_W), HG_HEADS, 1e-6, rms=True) * norm_g * jax.nn.silu(g)).astype(g.dtype)

    c_dirs, gc, c_q, c_i = prep(zc)
    l_dirs, gl, l_q, l_i = prep(zl)
    n_ctx = zc.shape[1]
    cat = lambda a, b: jnp.concatenate([a, b], axis=0)
    lf2, k2 = (jnp.stack([cat(c_dirs[d][i], l_dirs[d][i]) for d in range(2)]) for i in range(2))
    o2 = gla_scan(cat(c_q, l_q), cat(c_i, l_i), lf2, k2, n_ctx)
    o = (o2[0] + o2[1])[None]
    oc, ol = o[:, :n_ctx], o[:, n_ctx:]
    return (readout(oc, gc) if with_ctx else None), readout(ol, gl)


def kernel(x, c, ctx, c_ctx, ada_w, ada_b, w_in, w_out, ln_g, ln_b, hy_conv_w, hy_conv_b, hy_w1, hy_b1, hy_w2, hy_b2, hy_w3, hy_b3, hy_freq, hy_bias, rw_mu, rw_w0, rw_w2, rw_a0, rw_a2, rw_k_k, rw_k_a, rw_r_k, rw_gn_g, rw_gn_b, hg_lb_raw, hg_norm_g, ffn_w1, ffn_w3, ffn_w2):
    L = x.shape[1]
    n_rows = L // GRID_W
    rows = jnp.repeat(jnp.arange(n_rows, dtype=jnp.float32), GRID_W)
    cols = jnp.tile(jnp.arange(GRID_W, dtype=jnp.float32), n_rows)
    sm = jax.nn.softmax(hg_lb_raw.astype(jnp.float32), axis=1)
    lower_bounds = jnp.cumsum(sm, axis=1) - sm[:, :1]
    split_at = list(np.cumsum([HY_COLS, RW_COLS, RT_COLS]))

    c8 = jnp.zeros((8, D_MODEL), jnp.float32).at[0].set(c[0]).at[1].set(c_ctx)
    xl, xc = x[0], ctx[0]
    for l in range(DEPTH):
        with_ctx = l < DEPTH - 1
        mod = ada_modulation(c8, ada_w[l], ada_b[l][None, :])
        mod_l = [mod[0:1, i * D_MODEL:(i + 1) * D_MODEL] for i in range(6)]
        mod_c = [mod[1:2, i * D_MODEL:(i + 1) * D_MODEL] for i in range(6)]
        w_in_b = jnp.pad(w_in[l], ((0, 0), (0, P_IN_PAD - P_IN))).astype(jnp.bfloat16)
        w_out_b = w_out[l].astype(jnp.bfloat16)
        w1_b, w3_b, w2_b = (w[l].astype(jnp.bfloat16) for w in (ffn_w1, ffn_w3, ffn_w2))

        zl = modulated_projection(xl, mod_l[0], mod_l[1], w_in_b)[None, :, :P_IN]
        zc = modulated_projection(xc, mod_c[0], mod_c[1], w_in_b)[None, :, :P_IN]
        zl_hy, zl_rw, zl_rt, zl_hg = jnp.split(zl, split_at, axis=-1)
        zc_hy, zc_rw, zc_rt, zc_hg = jnp.split(zc, split_at, axis=-1)

        oc_hy, ol_hy = _hyena_mixer(zc_hy, zl_hy, hy_conv_w[l], hy_conv_b[l], hy_w1[l], hy_b1[l], hy_w2[l], hy_b2[l],
                                    hy_w3[l], hy_b3[l], hy_freq[l], hy_bias[l], with_ctx)
        oc_rw, ol_rw = _rwkv7_mixer(zc_rw, zl_rw, rw_mu[l], rw_w0[l], rw_w2[l], rw_a0[l], rw_a2[l], rw_k_k[l],
                                    rw_k_a[l], rw_r_k[l], rw_gn_g[l], rw_gn_b[l], with_ctx)
        oc_rt, ol_rt = _retention_mixer(zc_rt, zl_rt, rows, cols, with_ctx)
        oc_hg, ol_hg = _hgrn2_mixer(zc_hg, zl_hg, lower_bounds[0, l], lower_bounds[1, l], hg_norm_g[l], with_ctx)

        g0, b0 = ln_g[l, 0][None, :], ln_b[l, 0][None, :]
        g1, b1 = ln_g[l, 1][None, :], ln_b[l, 1][None, :]
        yl = jnp.concatenate([ol_hy, ol_rw, ol_rt, ol_hg], axis=-1)[0]
        xl = outproj_deepnorm(yl, w_out_b, xl, mod_l[2], g0, b0)
        xl = ffn_deepnorm(xl, mod_l[3], mod_l[4], w1_b, w3_b, w2_b, mod_l[5], g1, b1)
        if with_ctx:
            yc = jnp.concatenate([oc_hy, oc_rw, oc_rt, oc_hg], axis=-1)[0]
            xc = outproj_deepnorm(yc, w_out_b, xc, mod_c[2], g0, b0)
            xc = ffn_deepnorm(xc, mod_c[3], mod_c[4], w1_b, w3_b, w2_b, mod_c[5], g1, b1)
    return xl[None]
```

```python
import functools
import math

import jax
import jax.numpy as jnp
import numpy as np
from jax import lax
from jax.experimental import pallas as pl
from jax.experimental.pallas import tpu as pltpu

D_MODEL = 2048
DEPTH = 2
GRID_W = 64
N_MIXERS = 4
GROUP_W = D_MODEL // N_MIXERS
HY_ORDER = 2
HY_EMB = 33
HY_FAST_DECAY = 0.3
HY_SLOW_DECAY = 1.5
HY_TARGET = 1e-2
RW_HEAD = 64
RW_HEADS = GROUP_W // RW_HEAD
RW_LORA = 96
RW_GN_EPS = 64e-5
RT_HEAD = 64
RT_HEADS = GROUP_W // RT_HEAD
RT_CHUNK = 128
ROPE_BASE = 10000.0
HG_HEAD = 128
HG_HEADS = GROUP_W // HG_HEAD
HG_CHUNK = 64
HG_MIN_GATE = 1e-30
FFN_HIDDEN = 5632
ALPHA = (2 * DEPTH) ** 0.25
LN_EPS = 1e-6
HY_COLS = 3 * GROUP_W
RW_COLS = 4 * GROUP_W + 4 * RW_LORA
RT_COLS = 4 * GROUP_W
HG_COLS = 5 * GROUP_W
P_IN = HY_COLS + RW_COLS + RT_COLS + HG_COLS

LANES = 128
P_IN_PAD = 8704
PROJ_TN = 512
FFN_TF = 512
VMEM_LIMIT = 56 * 1024 * 1024


def _row_tile(m):
    return 512 if m % 512 == 0 else m


def _ln_rows(x):
    mu = jnp.mean(x, axis=-1, keepdims=True)
    xc = x - mu
    var = jnp.mean(xc * xc, axis=-1, keepdims=True)
    return xc * lax.rsqrt(var + LN_EPS)


def _ada_kernel(c_ref, w_ref, b_ref, o_ref):
    c = c_ref[...]
    h = c * jax.nn.sigmoid(c)
    o_ref[...] = jnp.dot(h.astype(jnp.bfloat16), w_ref[...].astype(jnp.bfloat16),
                         preferred_element_type=jnp.float32) + b_ref[...]


def ada_modulation(c8, w, b):
    n = w.shape[1]
    tn = 1024
    return pl.pallas_call(
        _ada_kernel,
        grid=(n // tn,),
        in_specs=[pl.BlockSpec((8, D_MODEL), lambda j: (0, 0)),
                  pl.BlockSpec((D_MODEL, tn), lambda j: (0, j)),
                  pl.BlockSpec((1, tn), lambda j: (0, j))],
        out_specs=pl.BlockSpec((8, tn), lambda j: (0, j)),
        out_shape=jax.ShapeDtypeStruct((8, n), jnp.float32),
        compiler_params=pltpu.CompilerParams(dimension_semantics=("arbitrary",),
                                             vmem_limit_bytes=VMEM_LIMIT),
    )(c8, w, b)


def _proj_kernel(x_ref, sh_ref, sc_ref, w_ref, o_ref, h_ref):
    @pl.when(pl.program_id(1) == 0)
    def _():
        h = _ln_rows(x_ref[...]) * (1.0 + sc_ref[...]) + sh_ref[...]
        h_ref[...] = h.astype(jnp.bfloat16)

    o_ref[...] = jnp.dot(h_ref[...], w_ref[...], preferred_element_type=jnp.float32)


def modulated_projection(x, shift, scale, w_bf16):
    m = x.shape[0]
    n = w_bf16.shape[1]
    tm = _row_tile(m)
    return pl.pallas_call(
        _proj_kernel,
        grid=(m // tm, n // PROJ_TN),
        in_specs=[pl.BlockSpec((tm, D_MODEL), lambda i, j: (i, 0)),
                  pl.BlockSpec((1, D_MODEL), lambda i, j: (0, 0)),
                  pl.BlockSpec((1, D_MODEL), lambda i, j: (0, 0)),
                  pl.BlockSpec((D_MODEL, PROJ_TN), lambda i, j: (0, j))],
        out_specs=pl.BlockSpec((tm, PROJ_TN), lambda i, j: (i, j)),
        out_shape=jax.ShapeDtypeStruct((m, n), jnp.float32),
        scratch_shapes=[pltpu.VMEM((tm, D_MODEL), jnp.bfloat16)],
        compiler_params=pltpu.CompilerParams(dimension_semantics=("arbitrary", "arbitrary"),
                                             vmem_limit_bytes=VMEM_LIMIT),
    )(x, shift, scale, w_bf16)


def _outproj_kernel(y_ref, w_ref, x_ref, gate_ref, g_ref, b_ref, o_ref):
    y = jnp.dot(y_ref[...].astype(jnp.bfloat16), w_ref[...], preferred_element_type=jnp.float32)
    r = ALPHA * x_ref[...] + gate_ref[...] * y
    o_ref[...] = _ln_rows(r) * g_ref[...] + b_ref[...]


def outproj_deepnorm(y, w_bf16, x, gate, g, b):
    m = x.shape[0]
    tm = _row_tile(m)
    row = pl.BlockSpec((tm, D_MODEL), lambda i: (i, 0))
    vec = pl.BlockSpec((1, D_MODEL), lambda i: (0, 0))
    return pl.pallas_call(
        _outproj_kernel,
        grid=(m // tm,),
        in_specs=[row, pl.BlockSpec((D_MODEL, D_MODEL), lambda i: (0, 0)), row, vec, vec, vec],
        out_specs=row,
        out_shape=jax.ShapeDtypeStruct((m, D_MODEL), jnp.float32),
        compiler_params=pltpu.CompilerParams(dimension_semantics=("arbitrary",),
                                             vmem_limit_bytes=VMEM_LIMIT),
    )(y, w_bf16, x, gate, g, b)


def _ffn_kernel(x_ref, sh_ref, sc_ref, w1_ref, w3_ref, w2_ref, gate_ref, g_ref, b_ref, o_ref, h_ref, acc_ref):
    j = pl.program_id(1)

    @pl.when(j == 0)
    def _():
        h = _ln_rows(x_ref[...]) * (1.0 + sc_ref[...]) + sh_ref[...]
        h_ref[...] = h.astype(jnp.bfloat16)
        acc_ref[...] = jnp.zeros_like(acc_ref)

    h = h_ref[...]
    a = jnp.dot(h, w1_ref[...], preferred_element_type=jnp.float32)
    u = jnp.dot(h, w3_ref[...], preferred_element_type=jnp.float32)
    s = (a * jax.nn.sigmoid(a) * u).astype(jnp.bfloat16)
    acc_ref[...] += jnp.dot(s, w2_ref[...], preferred_element_type=jnp.float32)

    @pl.when(j == pl.num_programs(1) - 1)
    def _():
        r = ALPHA * x_ref[...] + gate_ref[...] * acc_ref[...]
        o_ref[...] = _ln_rows(r) * g_ref[...] + b_ref[...]


def ffn_deepnorm(x, shift, scale, w1, w3, w2, gate, g, b):
    m = x.shape[0]
    tm = _row_tile(m)
    row = pl.BlockSpec((tm, D_MODEL), lambda i, j: (i, 0))
    vec = pl.BlockSpec((1, D_MODEL), lambda i, j: (0, 0))
    return pl.pallas_call(
        _ffn_kernel,
        grid=(m // tm, FFN_HIDDEN // FFN_TF),
        in_specs=[row, vec, vec,
                  pl.BlockSpec((D_MODEL, FFN_TF), lambda i, j: (0, j)),
                  pl.BlockSpec((D_MODEL, FFN_TF), lambda i, j: (0, j)),
                  pl.BlockSpec((FFN_TF, D_MODEL), lambda i, j: (j, 0)),
                  vec, vec, vec],
        out_specs=row,
        out_shape=jax.ShapeDtypeStruct((m, D_MODEL), jnp.float32),
        scratch_shapes=[pltpu.VMEM((tm, D_MODEL), jnp.bfloat16),
                        pltpu.VMEM((tm, D_MODEL), jnp.float32)],
        compiler_params=pltpu.CompilerParams(dimension_semantics=("arbitrary", "arbitrary"),
                                             vmem_limit_bytes=VMEM_LIMIT),
    )(x, shift, scale, w1, w3, w2, gate, g, b)


HI = lax.Precision.HIGHEST
SCAN_T = 64
GROUP_LANES = 256
RW_SUB = 16
NN = ((1,), (0,))
NT = ((1,), (1,))
TN = ((0,), (0,))


def _dot(a, b):
    return jnp.dot(a, b, precision=HI, preferred_element_type=jnp.float32)


def _dot_nt(a, b):
    return lax.dot_general(a, b, (NT, ((), ())), precision=HI, preferred_element_type=jnp.float32)


def _dot_tn(a, b):
    return lax.dot_general(a, b, (TN, ((), ())), precision=HI, preferred_element_type=jnp.float32)


def _split(x):
    hi = x.astype(jnp.bfloat16)
    lo = (x - hi.astype(jnp.float32)).astype(jnp.bfloat16)
    return hi, lo


def _mm3(a, b, dims=NN):
    d = lambda p, q: lax.dot_general(p, q, (dims, ((), ())), preferred_element_type=jnp.float32)
    return d(a[0], b[0]) + (d(a[0], b[1]) + d(a[1], b[0]))


def _chunk_index(d, i, n_ctx, n_all):
    bwd = jnp.where(i < n_ctx, n_ctx - 1 - i, n_all + n_ctx - 1 - i)
    return jnp.where(d == 0, i, bwd)


def _rwkv_scan_kernel(r_ref, v_ref, kk_ref, lw_ref, k_ref, a_ref, o_ref, ht_ref, *, head):
    T, G, S = SCAN_T, GROUP_LANES, RW_SUB
    nh = G // head
    nb = T // S
    n = nh * T
    d = pl.program_id(0)
    sign = 1 - 2 * d

    @pl.when(pl.program_id(2) == 0)
    def _():
        ht_ref[...] = jnp.zeros_like(ht_ref)

    r, v, kk = r_ref[...], v_ref[...], kk_ref[...]
    lw, k, a = lw_ref[0], k_ref[0], a_ref[0]

    ti = lax.broadcasted_iota(jnp.int32, (T, T), 0)
    si = lax.broadcasted_iota(jnp.int32, (T, T), 1)
    tri_incl = jnp.where((ti - si) * sign >= 0, 1.0, 0.0)
    c = _dot(tri_incl, lw)
    ctot = jnp.sum(lw, axis=0, keepdims=True)
    beta = kk * a
    einv = jnp.exp(-c)
    efin = jnp.exp(ctot - c)
    a_bar = -kk * jnp.exp(c - lw)
    r_bar = r * jnp.exp(c)

    rr = lax.broadcasted_iota(jnp.int32, (n, G), 0)
    same = ((rr // S) % nh) == (lax.broadcasted_iota(jnp.int32, (n, G), 1) // head)

    def bd(x):
        pieces = []
        for i in range(nb):
            pieces += [x[i * S:(i + 1) * S]] * nh
        return jnp.where(same, jnp.concatenate(pieces, axis=0), 0.0)

    rt = lax.broadcasted_iota(jnp.int32, (n, n), 0)
    cs = lax.broadcasted_iota(jnp.int32, (n, n), 1)
    t_r = (rt // (nh * S)) * S + rt % S
    t_c = (cs // (nh * S)) * S + cs % S
    same_h = ((rt // S) % nh) == ((cs // S) % nh)
    before = same_h & ((t_r - t_c) * sign > 0)
    incl = before | (rt == cs)
    diag_blk = (rt // (nh * S)) == (cs // (nh * S))
    eye = jnp.where(rt == cs, 1.0, 0.0)

    lhs = _split(jnp.concatenate([bd(a_bar), bd(r_bar)], axis=0))
    rhs = _split(jnp.concatenate([bd(k * einv), bd(beta * einv)], axis=0))
    m = _mm3(lhs, rhs, NT)
    ak, ab, rk, rb = m[:n, :n], m[:n, n:], m[n:, :n], m[n:, n:]
    ht = ht_ref[...]
    g = _mm3(lhs, _split(ht), NT)
    v_s = _split(bd(v))
    x = g[:n] + _mm3(_split(jnp.where(before, ak, 0.0)), v_s)
    lb = jnp.where(before, ab, 0.0)
    ld = jnp.where(diag_blk, lb, 0.0)
    lo = lb - ld
    xd = eye + ld
    lp = _split(ld)
    p = 2
    while p < S:
        lp = _split(_mm3(lp, lp))
        xd = xd + _mm3(lp, _split(xd))
        p *= 2
    xd_s = _split(xd)
    n_s = _split(_mm3(xd_s, _split(lo)))
    y = _mm3(xd_s, _split(x))
    y = y + _mm3(n_s, _split(y))
    p = 2
    while p < nb:
        n_s = _split(_mm3(n_s, n_s))
        y = y + _mm3(n_s, _split(y))
        p *= 2
    u_s = _split(y)
    o = (g[n:] + _mm3(_split(jnp.where(incl, rk, 0.0)), v_s)) + _mm3(_split(jnp.where(incl, rb, 0.0)), u_s)
    outs = []
    for i in range(nb):
        acc = o[i * nh * S:i * nh * S + S]
        for h in range(1, nh):
            acc = acc + o[i * nh * S + h * S:i * nh * S + (h + 1) * S]
        outs.append(acc)
    o_ref[0] = jnp.concatenate(outs, axis=0)
    ht_ref[...] = (ht * jnp.exp(ctot) + _mm3(v_s, _split(bd(k * efin)), TN)) + _mm3(u_s, _split(bd(beta * efin)), TN)


def rwkv_scan(r, v, kk, lw2, k2, a2, n_ctx_rows):
    N, W = r.shape
    T = SCAN_T
    n_all, n_ctx = N // T, n_ctx_rows // T
    shared = pl.BlockSpec((T, GROUP_LANES), lambda d, g, i: (_chunk_index(d, i, n_ctx, n_all), g))
    per_dir = pl.BlockSpec((1, T, GROUP_LANES), lambda d, g, i: (d, _chunk_index(d, i, n_ctx, n_all), g))
    return pl.pallas_call(
        functools.partial(_rwkv_scan_kernel, head=RW_HEAD),
        grid=(2, W // GROUP_LANES, n_all),
        in_specs=[shared, shared, shared, per_dir, per_dir, per_dir],
        out_specs=per_dir,
        out_shape=jax.ShapeDtypeStruct((2, N, W), jnp.float32),
        scratch_shapes=[pltpu.VMEM((GROUP_LANES, GROUP_LANES), jnp.float32)],
        compiler_params=pltpu.CompilerParams(dimension_semantics=("arbitrary", "arbitrary", "arbitrary"),
                                             vmem_limit_bytes=VMEM_LIMIT),
    )(r, v, kk, lw2, k2, a2)


def _retention_scan_kernel(q_ref, k_ref, v_ref, lg_ref, o_ref, st_ref, *, head):
    T, G = SCAN_T, GROUP_LANES
    nh = G // head
    n = nh * T
    d = pl.program_id(0)
    sign = 1 - 2 * d

    @pl.when(pl.program_id(2) == 0)
    def _():
        st_ref[...] = jnp.zeros_like(st_ref)

    q, k, v, lg = q_ref[...], k_ref[...], v_ref[...], lg_ref[...]
    t = lax.broadcasted_iota(jnp.int32, (T, 1), 0)
    pos = (t + d * (T - 1 - 2 * t) + 1).astype(jnp.float32)
    c = pos * lg
    ctot = float(T) * lg
    q_bar = q * jnp.exp(c)
    k_til = k * jnp.exp(-c)
    k_hat = k * jnp.exp(ctot - c)

    row_h = lax.broadcasted_iota(jnp.int32, (n, G), 0) // T
    col_h = lax.broadcasted_iota(jnp.int32, (n, G), 1) // head
    same = row_h == col_h

    def bd(x):
        return jnp.where(same, jnp.concatenate([x] * nh, axis=0), 0.0)

    rt = lax.broadcasted_iota(jnp.int32, (n, n), 0)
    cs = lax.broadcasted_iota(jnp.int32, (n, n), 1)
    incl_bd = ((rt // T) == (cs // T)) & ((rt - cs) * sign >= 0)

    q_bd, v_bd = bd(q_bar), bd(v)
    st = st_ref[...]
    scores = jnp.where(incl_bd, _dot_nt(q_bd, bd(k_til)), 0.0)
    o = _dot_nt(q_bd, st) + _dot(scores, v_bd)
    acc = o[0:T]
    for h in range(1, nh):
        acc = acc + o[h * T:(h + 1) * T]
    o_ref[0] = acc
    st_ref[...] = st * jnp.exp(ctot) + _dot_tn(v_bd, bd(k_hat))


def retention_scan(q, k, v, lg, n_ctx_rows):
    N, W = q.shape
    T = SCAN_T
    n_all, n_ctx = N // T, n_ctx_rows // T
    shared = pl.BlockSpec((T, GROUP_LANES), lambda d, g, i: (_chunk_index(d, i, n_ctx, n_all), g))
    return pl.pallas_call(
        functools.partial(_retention_scan_kernel, head=RT_HEAD),
        grid=(2, W // GROUP_LANES, n_all),
        in_specs=[shared, shared, shared, pl.BlockSpec((1, GROUP_LANES), lambda d, g, i: (0, g))],
        out_specs=pl.BlockSpec((1, T, GROUP_LANES), lambda d, g, i: (d, _chunk_index(d, i, n_ctx, n_all), g)),
        out_shape=jax.ShapeDtypeStruct((2, N, W), jnp.float32),
        scratch_shapes=[pltpu.VMEM((GROUP_LANES, GROUP_LANES), jnp.float32)],
        compiler_params=pltpu.CompilerParams(dimension_semantics=("arbitrary", "arbitrary", "arbitrary"),
                                             vmem_limit_bytes=VMEM_LIMIT),
    )(q, k, v, lg)


HG_SUB = 16


def _gla_scan_kernel(q_ref, v_ref, lf_ref, k_ref, o_ref, st_ref, *, head):
    T, S = SCAN_T, HG_SUB
    W = q_ref.shape[-1]
    nh = W // head
    d = pl.program_id(0)
    sign = 1 - 2 * d

    @pl.when(pl.program_id(1) == 0)
    def _():
        st_ref[...] = jnp.zeros_like(st_ref)

    ti = lax.broadcasted_iota(jnp.int32, (S, S), 0)
    si = lax.broadcasted_iota(jnp.int32, (S, S), 1)
    tri_incl = jnp.where((ti - si) * sign >= 0, 1.0, 0.0)
    row = lax.broadcasted_iota(jnp.int32, (S, 1), 0)

    for j in range(T // S):
        jb = j + d * (T // S - 1 - 2 * j)
        rows = pl.ds(pl.multiple_of(jb * S, S), S)
        q, v = q_ref[rows, :], v_ref[rows, :]
        lf, k = lf_ref[0, rows, :], k_ref[0, rows, :]
        b = _dot(tri_incl, lf)
        btot = jnp.sum(lf, axis=0, keepdims=True)
        qe = q * jnp.exp(b)
        ke = k * jnp.exp(btot - b)
        for h in range(nh):
            ls = slice(h * head, (h + 1) * head)
            qh, kh, vh, bh = q[:, ls], k[:, ls], v[:, ls], b[:, ls]
            st = st_ref[h]
            o = _dot_nt(qe[:, ls], st)
            for s in range(S):
                e = jnp.exp(jnp.minimum(bh - bh[s:s + 1, :], 0.0))
                a_s = jnp.sum(qh * kh[s:s + 1, :] * e, axis=-1, keepdims=True)
                a_s = jnp.where((row - s) * sign >= 0, a_s, 0.0)
                o = o + a_s * vh[s:s + 1, :]
            o_ref[0, rows, ls] = o
            st_ref[h] = st * jnp.exp(btot[:, ls]) + _dot_tn(vh, ke[:, ls])


def gla_scan(q, v, lf2, k2, n_ctx_rows):
    N, W = q.shape
    T = SCAN_T
    n_all, n_ctx = N // T, n_ctx_rows // T
    shared = pl.BlockSpec((T, W), lambda d, i: (_chunk_index(d, i, n_ctx, n_all), 0))
    per_dir = pl.BlockSpec((1, T, W), lambda d, i: (d, _chunk_index(d, i, n_ctx, n_all), 0))
    return pl.pallas_call(
        functools.partial(_gla_scan_kernel, head=HG_HEAD),
        grid=(2, n_all),
        in_specs=[shared, shared, per_dir, per_dir],
        out_specs=per_dir,
        out_shape=jax.ShapeDtypeStruct((2, N, W), jnp.float32),
        scratch_shapes=[pltpu.VMEM((W // HG_HEAD, HG_HEAD, HG_HEAD), jnp.float32)],
        compiler_params=pltpu.CompilerParams(dimension_semantics=("arbitrary", "arbitrary"),
                                             vmem_limit_bytes=VMEM_LIMIT),
    )(q, v, lf2, k2)


HY_EMB_PAD = 40
HY_FILTER_ROWS = 512


def hyena_features(L):
    t = jnp.linspace(0.0, 1.0, L, dtype=jnp.float32)[:, None]
    n_bands = (HY_EMB - 1) // 2
    f = jnp.linspace(1e-4, n_bands - 1, n_bands, dtype=jnp.float32)[None, :]
    ang = (2.0 * math.pi / L) * jnp.arange(L, dtype=jnp.float32)[:, None] * f
    z = jnp.concatenate([t, jnp.cos(ang), -jnp.sin(ang)], -1)
    return jnp.pad(z, ((0, 0), (0, HY_EMB_PAD - HY_EMB)))


def _filter_kernel(z_ref, w1_ref, b1_ref, w2_ref, b2_ref, w3_ref, b3_ref, fr_ref, dl_ref, h_ref, s_ref):
    i = pl.program_id(0)
    z = z_ref[...]
    fr = fr_ref[...]
    h = jnp.sin(fr * (_dot(z, w1_ref[...]) + b1_ref[...]))
    h = jnp.sin(fr * (_dot(h, w2_ref[...]) + b2_ref[...]))
    h = _dot(h, w3_ref[...]) + b3_ref[...]
    win = jnp.exp(-z[:, 0:1] * dl_ref[...])
    h = h * jnp.concatenate([win] * (h.shape[1] // win.shape[1]), axis=1)

    @pl.when(i == 0)
    def _():
        s_ref[...] = jnp.zeros_like(s_ref)

    s_ref[...] += jnp.sum(jnp.abs(h), axis=0, keepdims=True)
    row = lax.broadcasted_iota(jnp.int32, h.shape, 0) + i * h.shape[0]
    col = lax.broadcasted_iota(jnp.int32, h.shape, 1)
    neg = (col // GROUP_W) % 2 == 1
    h_ref[...] = jnp.where(neg & (row == 0), 0.0, h)


def hyena_filter_bank(L, w1, b1, w2, b2, w3, b3, freq):
    z = hyena_features(L)
    w1p = jnp.pad(w1, ((0, HY_EMB_PAD - HY_EMB), (0, 0)))
    max_decay = math.log(HY_TARGET) / HY_FAST_DECAY
    min_decay = math.log(HY_TARGET) / HY_SLOW_DECAY
    deltas = jnp.abs(jnp.linspace(min_decay, max_decay, GROUP_W, dtype=jnp.float32))[None, :]
    n = w3.shape[1]
    tr = min(L, HY_FILTER_ROWS)
    full = lambda a: pl.BlockSpec(a.shape, lambda i: (0,) * a.ndim)
    args = (z, w1p, b1[None, :], w2, b2[None, :], w3, b3[None, :], freq[None, :], deltas)
    return pl.pallas_call(
        _filter_kernel,
        grid=(L // tr,),
        in_specs=[pl.BlockSpec((tr, HY_EMB_PAD), lambda i: (i, 0))] + [full(a) for a in args[1:]],
        out_specs=[pl.BlockSpec((tr, n), lambda i: (i, 0)), pl.BlockSpec((1, n), lambda i: (0, 0))],
        out_shape=[jax.ShapeDtypeStruct((L, n), jnp.float32), jax.ShapeDtypeStruct((1, n), jnp.float32)],
        compiler_params=pltpu.CompilerParams(dimension_semantics=("arbitrary",), vmem_limit_bytes=VMEM_LIMIT),
    )(*args)


FFT_NB = 8
FFT_CB_WIDE = 512
FFT_CB = 256


def _dot3(a_hi, a_lo, b_hi, b_lo):
    d = functools.partial(jnp.dot, preferred_element_type=jnp.float32)
    return d(a_hi, b_hi) + (d(a_hi, b_lo) + d(a_lo, b_hi))


def fft_tables(n1, n2):
    N = n1 * n2
    a = jnp.arange(n1, dtype=jnp.float32)[:, None, None]
    k2 = jnp.arange(n2, dtype=jnp.float32)[None, :, None]
    b = jnp.arange(n2 // 2, dtype=jnp.float32)[None, None, :]
    ph = (jnp.mod(a * k2, float(N)) / N + jnp.mod(b * k2, float(n2)) / n2) * (-2.0 * math.pi)
    g = jnp.concatenate([jnp.cos(ph), jnp.sin(ph)], axis=1)
    ginv = jnp.transpose(g, (0, 2, 1)) / N
    k1 = jnp.arange(n1, dtype=jnp.float32)[:, None]
    aa = jnp.arange(n1, dtype=jnp.float32)[None, :]
    f = jnp.mod(k1 * aa, float(n1)) * (-2.0 * math.pi / n1)
    fr, fi = jnp.cos(f), jnp.sin(f)
    ff = jnp.concatenate([jnp.concatenate([fr, -fi], axis=1), jnp.concatenate([fi, fr], axis=1)], axis=0)
    return {"g": _split(g), "ginv": _split(ginv), "ff": _split(ff), "fft": _split(ff.T)}


def _stage_a_kernel(u_ref, gh_ref, gl_ref, o_ref):
    for j in range(FFT_NB):
        xh, xl = _split(u_ref[:, j, :])
        o_ref[j] = _dot3(gh_ref[j], gl_ref[j], xh, xl)


def fft_stage_a(u3, g):
    nb, n1, C = u3.shape
    cb = min(C, FFT_CB_WIDE)
    gspec = pl.BlockSpec((FFT_NB, 4 * nb, nb), lambda c, i: (i, 0, 0))
    return pl.pallas_call(
        _stage_a_kernel,
        grid=(C // cb, n1 // FFT_NB),
        in_specs=[pl.BlockSpec((nb, FFT_NB, cb), lambda c, i: (0, i, c)), gspec, gspec],
        out_specs=pl.BlockSpec((FFT_NB, 4 * nb, cb), lambda c, i: (i, 0, c)),
        out_shape=jax.ShapeDtypeStruct((n1, 4 * nb, C), jnp.float32),
        compiler_params=pltpu.CompilerParams(dimension_semantics=("arbitrary", "arbitrary"),
                                             vmem_limit_bytes=VMEM_LIMIT),
    )(u3, g[0], g[1])


def _stage_b_filter_kernel(re_ref, im_ref, fh_ref, fl_ref, o_ref):
    for j in range(FFT_NB):
        th, tl = _split(jnp.concatenate([re_ref[:, j, :], im_ref[:, j, :]], axis=0))
        o_ref[j] = _dot3(fh_ref[...], fl_ref[...], th, tl)


def fft_stage_b_filter(t1, ff):
    n1, n2x2, C = t1.shape
    n2 = n2x2 // 2
    cb = min(C, FFT_CB)
    blk = lambda off: pl.BlockSpec((n1, FFT_NB, cb), lambda c, i: (0, i + off, c))
    mat = pl.BlockSpec((2 * n1, 2 * n1), lambda c, i: (0, 0))
    return pl.pallas_call(
        _stage_b_filter_kernel,
        grid=(C // cb, n2 // FFT_NB),
        in_specs=[blk(0), blk(n2 // FFT_NB), mat, mat],
        out_specs=pl.BlockSpec((FFT_NB, 2 * n1, cb), lambda c, i: (i, 0, c)),
        out_shape=jax.ShapeDtypeStruct((n2, 2 * n1, C), jnp.float32),
        compiler_params=pltpu.CompilerParams(dimension_semantics=("arbitrary", "arbitrary"),
                                             vmem_limit_bytes=VMEM_LIMIT),
    )(t1, t1, ff[0], ff[1])


def _stage_b_conv_kernel(re_ref, im_ref, p_ref, q_ref, s_ref, fh_ref, fl_ref, fth_ref, ftl_ref, ore_ref, oim_ref):
    n1 = re_ref.shape[0]
    s = s_ref[...]
    for j in range(FFT_NB):
        th, tl = _split(jnp.concatenate([re_ref[:, j, :], im_ref[:, j, :]], axis=0))
        x = _dot3(fh_ref[...], fl_ref[...], th, tl)
        p, q = p_ref[j], q_ref[j]
        hr = (p[:n1] + q[:n1]) * s
        hi = (p[n1:] - q[n1:]) * s
        xr, xi = x[:n1], x[n1:]
        yh, yl = _split(jnp.concatenate([xr * hr - xi * hi, xr * hi + xi * hr], axis=0))
        z = _dot3(fth_ref[...], ftl_ref[...], yh, yl)
        ore_ref[:, j, :] = z[:n1]
        oim_ref[:, j, :] = z[n1:]


def fft_stage_b_conv(t1, spec, col_p, col_q, inv_norm, ff, fft_):
    n1, n2x2, C = t1.shape
    n2 = n2x2 // 2
    cb = min(C, FFT_CB)
    ncb = C // cb
    blk = lambda off: pl.BlockSpec((n1, FFT_NB, cb), lambda c, i: (0, i + off, c))
    mat = pl.BlockSpec((2 * n1, 2 * n1), lambda c, i: (0, 0))
    sp = lambda col: pl.BlockSpec((FFT_NB, 2 * n1, cb), lambda c, i: (i, 0, col * ncb + c))
    return pl.pallas_call(
        _stage_b_conv_kernel,
        grid=(ncb, n2 // FFT_NB),
        in_specs=[blk(0), blk(n2 // FFT_NB), sp(col_p), sp(col_q), pl.BlockSpec((1, cb), lambda c, i: (0, c)),
                  mat, mat, mat, mat],
        out_specs=[blk(0), blk(0)],
        out_shape=[jax.ShapeDtypeStruct((n1, n2, C), jnp.float32)] * 2,
        compiler_params=pltpu.CompilerParams(dimension_semantics=("arbitrary", "arbitrary"),
                                             vmem_limit_bytes=VMEM_LIMIT),
    )(t1, t1, spec, spec, inv_norm, ff[0], ff[1], fft_[0], fft_[1])


def _stage_a_inv_kernel(re_ref, im_ref, gh_ref, gl_ref, u_ref, gate_ref, bias_ref, o_ref):
    for j in range(FFT_NB):
        th, tl = _split(jnp.concatenate([re_ref[j], im_ref[j]], axis=0))
        y = _dot3(gh_ref[j], gl_ref[j], th, tl)
        o_ref[:, j, :] = gate_ref[:, j, :] * (y + u_ref[:, j, :] * bias_ref[...])


def fft_stage_a_inv(t2re, t2im, ginv, u3, gate3, bias):
    n1, n2, C = t2re.shape
    nb = n2 // 2
    cb = min(C, FFT_CB_WIDE)
    tb = pl.BlockSpec((FFT_NB, n2, cb), lambda c, i: (i, 0, c))
    gb = pl.BlockSpec((FFT_NB, nb, 2 * n2), lambda c, i: (i, 0, 0))
    ub = pl.BlockSpec((nb, FFT_NB, cb), lambda c, i: (0, i, c))
    return pl.pallas_call(
        _stage_a_inv_kernel,
        grid=(C // cb, n1 // FFT_NB),
        in_specs=[tb, tb, gb, gb, ub, ub, pl.BlockSpec((1, cb), lambda c, i: (0, c))],
        out_specs=ub,
        out_shape=jax.ShapeDtypeStruct((nb, n1, C), jnp.float32),
        compiler_params=pltpu.CompilerParams(dimension_semantics=("arbitrary", "arbitrary"),
                                             vmem_limit_bytes=VMEM_LIMIT),
    )(t2re, t2im, ginv[0], ginv[1], u3, gate3, bias)


FFT_N1 = 128


def hyena_long_conv_chain(v, gates, biases, filt, colsum):
    L, C = v.shape
    n1 = FFT_N1
    n2 = 2 * L // n1
    tabs = fft_tables(n1, n2)
    spec = fft_stage_b_filter(fft_stage_a(filt.reshape(n2 // 2, n1, filt.shape[1]), tabs["g"]), tabs["ff"])
    s4 = colsum.reshape(HY_ORDER, 2, C)
    inv_norm = 1.0 / (s4[:, 0] + s4[:, 1])
    y3 = v.reshape(n2 // 2, n1, C)
    for n in range(HY_ORDER):
        t1 = fft_stage_a(y3, tabs["g"])
        t2re, t2im = fft_stage_b_conv(t1, spec, 2 * n, 2 * n + 1, inv_norm[n][None, :], tabs["ff"], tabs["fft"])
        y3 = fft_stage_a_inv(t2re, t2im, tabs["ginv"], y3, gates[n].reshape(n2 // 2, n1, C), biases[n][None, :])
    return y3.reshape(L, C)


def _head_norm(x, n_heads, eps, rms=False):
    B, L, W = x.shape
    xf = x.astype(jnp.float32).reshape(B, L, n_heads, W // n_heads)
    if not rms:
        xf = xf - jnp.mean(xf, -1, keepdims=True)
    y = xf * lax.rsqrt(jnp.mean(jnp.square(xf), -1, keepdims=True) + eps)
    return y.reshape(B, L, W)


def _centred_conv3(u, w, b):
    up = jnp.pad(u, ((0, 0), (1, 1), (0, 0)))
    return up[:, :-2] * w[0] + up[:, 1:-1] * w[1] + up[:, 2:] * w[2] + b


def _centred_shift(u):
    up = jnp.pad(u, ((0, 0), (1, 1), (0, 0)))
    return 0.5 * (up[:, :-2] + up[:, 2:])


def _rope_2d(x, rows, cols):
    d_axis = x.shape[-1] // 2
    inv = ROPE_BASE ** (-jnp.arange(0, d_axis, 2, dtype=jnp.float32) / d_axis)

    def rot(xa, pos):
        ang = pos[:, None, None] * inv
        cos, sin = jnp.cos(ang), jnp.sin(ang)
        x1, x2 = jnp.split(xa, 2, axis=-1)
        return jnp.concatenate([x1 * cos - x2 * sin, x1 * sin + x2 * cos], -1)

    xr, xc = jnp.split(x.astype(jnp.float32), 2, axis=-1)
    return jnp.concatenate([rot(xr, rows), rot(xc, cols)], -1)


def _flip_time(xs):
    return tuple(jnp.flip(a, axis=1) for a in xs)


def _bidir_with_context(scan_fn, ctx_fwd, lat_fwd, ctx_bwd, lat_bwd, s0):
    oc_f, sc_f = scan_fn(ctx_fwd, s0)
    ol_f, _ = scan_fn(lat_fwd, sc_f)
    oc_b, sc_b = scan_fn(_flip_time(ctx_bwd), s0)
    ol_b, _ = scan_fn(_flip_time(lat_bwd), sc_b)
    return oc_f + jnp.flip(oc_b, axis=1), ol_f + jnp.flip(ol_b, axis=1)


def _hyena_filters(L, w1, b1, w2, b2, w3, b3, freq):
    t = jnp.linspace(0.0, 1.0, L, dtype=jnp.float32)[:, None]
    n_bands = (HY_EMB - 1) // 2
    f = jnp.linspace(1e-4, n_bands - 1, n_bands, dtype=jnp.float32)[None, :]
    ang = (2.0 * math.pi / L) * jnp.arange(L, dtype=jnp.float32)[:, None] * f
    z = jnp.concatenate([t, jnp.cos(ang), -jnp.sin(ang)], -1)
    freq = freq.astype(jnp.float32)
    h = jnp.sin(freq * (z @ w1 + b1))
    h = jnp.sin(freq * (h @ w2 + b2))
    h = (h @ w3 + b3).astype(jnp.float32).reshape(L, HY_ORDER, 2, GROUP_W)
    max_decay = math.log(HY_TARGET) / HY_FAST_DECAY
    min_decay = math.log(HY_TARGET) / HY_SLOW_DECAY
    deltas = jnp.linspace(min_decay, max_decay, GROUP_W, dtype=jnp.float32)
    h = h * jnp.exp(-t * jnp.abs(deltas))[:, None, None, :]
    return h / jnp.sum(jnp.abs(h), axis=(0, 2), keepdims=True)


def _bidir_fft_conv(u, h_pos, h_neg, bias):
    L = u.shape[1]
    h_full = jnp.concatenate([h_pos, jnp.zeros_like(h_pos[:1]), h_neg[:0:-1]], axis=0)
    uf = jnp.fft.rfft(u.astype(jnp.float32), n=2 * L, axis=1)
    hf = jnp.fft.rfft(h_full, n=2 * L, axis=0)
    y = jnp.fft.irfft(uf * hf[None], n=2 * L, axis=1)[:, :L]
    return y + u.astype(jnp.float32) * bias.astype(jnp.float32)


def _hyena_mixer(zc, zl, conv_w, conv_b, w1, b1, w2, b2, w3, b3, freq, bias, with_ctx):
    def run(z):
        L = z.shape[1]
        u = _centred_conv3(z, conv_w, conv_b)
        v, x1, x2 = jnp.split(u, 3, axis=-1)
        h = _hyena_filters(L, w1, b1, w2, b2, w3, b3, freq)
        y = v
        for n, gate in enumerate((x1, x2)):
            y = gate * _bidir_fft_conv(y, h[:, n, 0], h[:, n, 1], bias[n])
        return y.astype(z.dtype)

    def run_latent(z):
        L = z.shape[1]
        u = _centred_conv3(z, conv_w, conv_b)[0]
        v, x1, x2 = jnp.split(u, 3, axis=-1)
        filt, colsum = hyena_filter_bank(L, w1, b1, w2, b2, w3, b3, freq)
        return hyena_long_conv_chain(v, (x1, x2), bias, filt, colsum)[None]

    return (run(zc) if with_ctx else None), run_latent(zl)


def _rwkv7_scan(inputs, s0):
    r, w, k, v, kk, a = (jnp.moveaxis(t.astype(jnp.float32), 1, 0) for t in inputs)

    def step(S, xs):
        r_t, w_t, k_t, v_t, kk_t, a_t = xs
        sa = jnp.einsum('bhvk,bhk->bhv', S, -kk_t)
        S = S * w_t[:, :, None, :] + sa[..., :, None] * (kk_t * a_t)[..., None, :] + v_t[..., :, None] * k_t[..., None, :]
        return S, jnp.einsum('bhvk,bhk->bhv', S, r_t)

    s_fin, o = lax.scan(step, s0, (r, w, k, v, kk, a))
    return jnp.moveaxis(o, 0, 1), s_fin


def _rwkv7_mixer(zc, zl, mu, w0, w2, a0, a2, k_k, k_a, r_k, gn_g, gn_b, with_ctx):
    split_at = list(np.cumsum([GROUP_W] * 4 + [RW_LORA] * 3))

    def heads(t):
        return t.reshape(t.shape[0], t.shape[1], RW_HEADS, RW_HEAD)

    def prep(z):
        z = z + (_centred_shift(z) - z) * mu
        r, k, v, g, wd_f, wd_b, ad_f, ad_b = jnp.split(z, split_at, axis=-1)
        kk = heads((k * k_k).astype(jnp.float32))
        kk = kk * lax.rsqrt(jnp.maximum(jnp.sum(jnp.square(kk), -1, keepdims=True), 1e-24))
        dirs = []
        for d, (wd, ad) in enumerate(((wd_f, ad_f), (wd_b, ad_b))):
            w_log = -jax.nn.softplus(-(w0[d] + jnp.tanh(wd) @ w2[d])) - 0.5
            log_decay = -jnp.exp(w_log.astype(jnp.float32))
            a = jax.nn.sigmoid(a0[d] + ad @ a2[d])
            k_d = k * (1.0 + (a - 1.0) * k_a)
            dirs.append((log_decay[0], k_d[0], a[0]))
        return dirs, (r, k, v, g), kk.reshape(kk.shape[1], GROUP_W)

    def readout(o, r, k, v, g):
        B, L = o.shape[:2]
        y = _head_norm(o.reshape(B, L, GROUP_W), RW_HEADS, RW_GN_EPS) * gn_g + gn_b
        bonus = jnp.sum(heads(r * k) * r_k, -1, keepdims=True) * heads(v)
        return ((y + bonus.reshape(B, L, GROUP_W)) * jax.nn.sigmoid(g)).astype(g.dtype)

    c_dirs, c_rkvg, c_kk = prep(zc)
    l_dirs, l_rkvg, l_kk = prep(zl)
    n_ctx = zc.shape[1]
    cat = lambda a, b: jnp.concatenate([a, b], axis=0)
    per_dir = [jnp.stack([cat(c_dirs[d][i], l_dirs[d][i]) for d in range(2)]) for i in range(3)]
    o2 = rwkv_scan(cat(c_rkvg[0][0], l_rkvg[0][0]), cat(c_rkvg[2][0], l_rkvg[2][0]), cat(c_kk, l_kk),
                   per_dir[0], per_dir[1], per_dir[2], n_ctx)
    o = (o2[0] + o2[1])[None]
    oc, ol = o[:, :n_ctx], o[:, n_ctx:]
    return (readout(oc, *c_rkvg) if with_ctx else None), readout(ol, *l_rkvg)


def _retention_chunkwise(inputs, s0, log_gamma):
    q, k, v = (t.astype(jnp.float32) for t in inputs)
    B, L, H, D = q.shape
    N = L // RT_CHUNK
    q, k, v = (t.reshape(B, N, RT_CHUNK, H, D) for t in (q, k, v))
    idx = jnp.arange(RT_CHUNK, dtype=jnp.float32)
    rel = idx[:, None] - idx[None, :]
    decay = jnp.where(rel >= 0, jnp.exp(jnp.maximum(rel, 0.0)[None] * log_gamma[:, None, None]), 0.0)
    scores = jnp.einsum('bnihd,bnjhd->bnhij', q, k) * decay
    o = jnp.einsum('bnhij,bnjhd->bnihd', scores, v)
    k_w = jnp.exp((RT_CHUNK - 1 - idx)[:, None] * log_gamma)
    kv = jnp.einsum('bnjhd,bnjhe->nbhde', k * k_w[:, :, None], v)
    chunk_decay = jnp.exp(RT_CHUNK * log_gamma)[None, :, None, None]

    def step(S, kv_n):
        return chunk_decay * S + kv_n, S

    s_fin, s_prev = lax.scan(step, s0, kv)
    q_w = jnp.exp((idx + 1.0)[:, None] * log_gamma)
    o = o + jnp.einsum('bnihd,nbhde->bnihe', q * q_w[:, :, None], s_prev)
    return o.reshape(B, L, H, D), s_fin


def _retention_mixer(zc, zl, rows, cols, with_ctx):
    log_gamma = jnp.log1p(-jnp.exp2(-5.0 - jnp.arange(RT_HEADS, dtype=jnp.float32)))

    def prep(z, rotate):
        B, L = z.shape[:2]
        q, k, v, g = jnp.split(z, 4, axis=-1)
        q, k, v = (t.reshape(B, L, RT_HEADS, RT_HEAD) for t in (q, k, v))
        if rotate:
            q, k = _rope_2d(q, rows, cols), _rope_2d(k, rows, cols)
        return (q, k * RT_HEAD ** -0.5, v), g

    def scan_fn(xs, s0):
        return _retention_chunkwise(xs, s0, log_gamma)

    def readout(o, g):
        B, L = o.shape[:2]
        return (_head_norm(o.reshape(B, L, GROUP_W), RT_HEADS, 1e-6) * jax.nn.silu(g)).astype(g.dtype)

    c_in, gc = prep(zc, False)
    l_in, gl = prep(zl, True)
    n_ctx = zc.shape[1]
    q, k, v = (jnp.concatenate([a.reshape(n_ctx, GROUP_W), b.reshape(-1, GROUP_W)], axis=0) for a, b in zip(c_in, l_in))
    o2 = retention_scan(q, k, v, jnp.repeat(log_gamma, RT_HEAD)[None, :], n_ctx)
    o = (o2[0] + o2[1])[None]
    oc, ol = o[:, :n_ctx], o[:, n_ctx:]
    return (readout(oc, gc) if with_ctx else None), readout(ol, gl)


def _gla_chunkwise(inputs, s0):
    q, log_f, k, v = (t.astype(jnp.float32) for t in inputs)
    B, L, H, _ = q.shape
    N = L // HG_CHUNK

    def chunks(t):
        return t.reshape(B, N, HG_CHUNK, H, t.shape[-1]).transpose(1, 0, 3, 2, 4)

    causal = jnp.tril(jnp.ones((HG_CHUNK, HG_CHUNK), bool))[:, :, None]

    def step(S, xs):
        q_c, lf_c, k_c, v_c = xs
        b = jnp.cumsum(lf_c, axis=2)
        diff = b[:, :, :, None, :] - b[:, :, None, :, :]
        dec = jnp.where(causal, jnp.exp(jnp.where(causal, diff, 0.0)), 0.0)
        A = jnp.einsum('bhtd,bhsd,bhtsd->bhts', q_c, k_c, dec)
        o = jnp.einsum('bhts,bhsv->bhtv', A, v_c) + jnp.einsum('bhtd,bhdv->bhtv', q_c * jnp.exp(b), S)
        b_last = b[:, :, -1:, :]
        S = jnp.exp(b_last[:, :, 0, :])[..., None] * S + jnp.einsum('bhsd,bhsv->bhdv', k_c * jnp.exp(b_last - b), v_c)
        return S, o

    s_fin, o = lax.scan(step, s0, tuple(chunks(t) for t in (q, log_f, k, v)))
    return o.transpose(1, 0, 3, 2, 4).reshape(B, L, H, v.shape[-1]), s_fin


def _hgrn2_mixer(zc, zl, lb_f, lb_b, norm_g, with_ctx):
    def heads(t):
        return t.reshape(t.shape[0], t.shape[1], HG_HEADS, HG_HEAD)

    def prep(z):
        q, f_f, f_b, i, g = jnp.split(z, 5, axis=-1)
        q = jax.nn.silu(q)
        dirs = []
        for f, lb in ((f_f, lb_f), (f_b, lb_b)):
            gate = lb + (1.0 - lb) * jax.nn.sigmoid(f.astype(jnp.float32))
            log_f = jnp.log(jnp.maximum(gate, HG_MIN_GATE))
            dirs.append((log_f[0], (1.0 - gate)[0]))
        return dirs, g, q[0], i[0]

    def readout(o, g):
        B, L = o.shape[:2]
        return (_head_norm(o.reshape(B, L, GROUP_W), HG_HEADS, 1e-6, rms=True) * norm_g * jax.nn.silu(g)).astype(g.dtype)

    c_dirs, gc, c_q, c_i = prep(zc)
    l_dirs, gl, l_q, l_i = prep(zl)
    n_ctx = zc.shape[1]
    cat = lambda a, b: jnp.concatenate([a, b], axis=0)
    lf2, k2 = (jnp.stack([cat(c_dirs[d][i], l_dirs[d][i]) for d in range(2)]) for i in range(2))
    o2 = gla_scan(cat(c_q, l_q), cat(c_i, l_i), lf2, k2, n_ctx)
    o = (o2[0] + o2[1])[None]
    oc, ol = o[:, :n_ctx], o[:, n_ctx:]
    return (readout(oc, gc) if with_ctx else None), readout(ol, gl)


def kernel(x, c, ctx, c_ctx, ada_w, ada_b, w_in, w_out, ln_g, ln_b, hy_conv_w, hy_conv_b, hy_w1, hy_b1, hy_w2, hy_b2, hy_w3, hy_b3, hy_freq, hy_bias, rw_mu, rw_w0, rw_w2, rw_a0, rw_a2, rw_k_k, rw_k_a, rw_r_k, rw_gn_g, rw_gn_b, hg_lb_raw, hg_norm_g, ffn_w1, ffn_w3, ffn_w2):
    L = x.shape[1]
    n_rows = L // GRID_W
    rows = jnp.repeat(jnp.arange(n_rows, dtype=jnp.float32), GRID_W)
    cols = jnp.tile(jnp.arange(GRID_W, dtype=jnp.float32), n_rows)
    sm = jax.nn.softmax(hg_lb_raw.astype(jnp.float32), axis=1)
    lower_bounds = jnp.cumsum(sm, axis=1) - sm[:, :1]
    split_at = list(np.cumsum([HY_COLS, RW_COLS, RT_COLS]))

    c8 = jnp.zeros((8, D_MODEL), jnp.float32).at[0].set(c[0]).at[1].set(c_ctx)
    xl, xc = x[0], ctx[0]
    for l in range(DEPTH):
        with_ctx = l < DEPTH - 1
        mod = ada_modulation(c8, ada_w[l], ada_b[l][None, :])
        mod_l = [mod[0:1, i * D_MODEL:(i + 1) * D_MODEL] for i in range(6)]
        mod_c = [mod[1:2, i * D_MODEL:(i + 1) * D_MODEL] for i in range(6)]
        w_in_b = jnp.pad(w_in[l], ((0, 0), (0, P_IN_PAD - P_IN))).astype(jnp.bfloat16)
        w_out_b = w_out[l].astype(jnp.bfloat16)
        w1_b, w3_b, w2_b = (w[l].astype(jnp.bfloat16) for w in (ffn_w1, ffn_w3, ffn_w2))

        zl = modulated_projection(xl, mod_l[0], mod_l[1], w_in_b)[None, :, :P_IN]
        zc = modulated_projection(xc, mod_c[0], mod_c[1], w_in_b)[None, :, :P_IN]
        zl_hy, zl_rw, zl_rt, zl_hg = jnp.split(zl, split_at, axis=-1)
        zc_hy, zc_rw, zc_rt, zc_hg = jnp.split(zc, split_at, axis=-1)

        oc_hy, ol_hy = _hyena_mixer(zc_hy, zl_hy, hy_conv_w[l], hy_conv_b[l], hy_w1[l], hy_b1[l], hy_w2[l], hy_b2[l],
                                    hy_w3[l], hy_b3[l], hy_freq[l], hy_bias[l], with_ctx)
        oc_rw, ol_rw = _rwkv7_mixer(zc_rw, zl_rw, rw_mu[l], rw_w0[l], rw_w2[l], rw_a0[l], rw_a2[l], rw_k_k[l],
                                    rw_k_a[l], rw_r_k[l], rw_gn_g[l], rw_gn_b[l], with_ctx)
        oc_rt, ol_rt = _retention_mixer(zc_rt, zl_rt, rows, cols, with_ctx)
        oc_hg, ol_hg = _hgrn2_mixer(zc_hg, zl_hg, lower_bounds[0, l], lower_bounds[1, l], hg_norm_g[l], with_ctx)

        g0, b0 = ln_g[l, 0][None, :], ln_b[l, 0][None, :]
        g1, b1 = ln_g[l, 1][None, :], ln_b[l, 1][None, :]
        yl = jnp.concatenate([ol_hy, ol_rw, ol_rt, ol_hg], axis=-1)[0]
        xl = outproj_deepnorm(yl, w_out_b, xl, mod_l[2], g0, b0)
        xl = ffn_deepnorm(xl, mod_l[3], mod_l[4], w1_b, w3_b, w2_b, mod_l[5], g1, b1)
        if with_ctx:
            yc = jnp.concatenate([oc_hy, oc_rw, oc_rt, oc_hg], axis=-1)[0]
            xc = outproj_deepnorm(yc, w_out_b, xc, mod_c[2], g0, b0)
            xc = ffn_deepnorm(xc, mod_c[3], mod_c[4], w1_b, w3_b, w2_b, mod_c[5], g1, b1)
    return xl[None]
```

```python
import functools
import math

import jax
import jax.numpy as jnp
from jax import lax
from jax.experimental import pallas as pl
from jax.experimental.pallas import tpu as pltpu

D_MODEL = 2048
DEPTH = 2
GRID_W = 64
N_MIXERS = 4
GROUP_W = D_MODEL // N_MIXERS
HY_ORDER = 2
HY_EMB = 33
HY_FAST_DECAY = 0.3
HY_SLOW_DECAY = 1.5
HY_TARGET = 1e-2
RW_HEAD = 64
RW_LORA = 96
RW_GN_EPS = 64e-5
RT_HEAD = 64
RT_HEADS = GROUP_W // RT_HEAD
ROPE_BASE = 10000.0
HG_HEAD = 128
HG_MIN_GATE = 1e-30
FFN_HIDDEN = 5632
ALPHA = (2 * DEPTH) ** 0.25
LN_EPS = 1e-6
HEAD_NORM_EPS = 1e-6

CB = 512
COL_HY = 0
COL_RW = 3
COL_RT = 8
COL_HG = 12
P_IN_PAD = 17 * CB
RW_REAL = 4 * GROUP_W + 4 * RW_LORA
PROJ_TM = 768
OUT_TM = 256
FFN_TF = 512
PREP_ROWS = 256
HALO = 8
VMEM_LIMIT = 56 * 1024 * 1024

HI = lax.Precision.HIGHEST
NN = ((1,), (0,))
NT = ((1,), (1,))
TN = ((0,), (0,))


def _params(n_axes):
    return pltpu.CompilerParams(dimension_semantics=("arbitrary",) * n_axes, vmem_limit_bytes=VMEM_LIMIT)


def _full(a):
    return pl.BlockSpec(a.shape, lambda *_: (0,) * a.ndim)


def _dot(a, b):
    return jnp.dot(a, b, precision=HI, preferred_element_type=jnp.float32)


def _split(x):
    hi = x.astype(jnp.bfloat16)
    lo = (x - hi.astype(jnp.float32)).astype(jnp.bfloat16)
    return hi, lo


def _mm3(a, b, dims=NN):
    d = lambda p, q: lax.dot_general(p, q, (dims, ((), ())), preferred_element_type=jnp.float32)
    return d(a[0], b[0]) + (d(a[0], b[1]) + d(a[1], b[0]))


def _mm2(x, b):
    hi, lo = _split(x)
    d = functools.partial(jnp.dot, preferred_element_type=jnp.float32)
    return d(hi, b) + d(lo, b)


def _ln_rows(x):
    mu = jnp.mean(x, axis=-1, keepdims=True)
    xc = x - mu
    var = jnp.mean(xc * xc, axis=-1, keepdims=True)
    return xc * lax.rsqrt(var + LN_EPS)


def _sigmoid(x):
    return 1.0 / (1.0 + jnp.exp(-x))


def _ctx_rows(tm, n_ctx):
    return (pl.program_id(0) * tm + lax.broadcasted_iota(jnp.int32, (tm, 1), 0)) < n_ctx


def _mod_row(ref, is_ctx):
    return jnp.where(is_ctx, ref[1:2, :], ref[0:1, :])


def _block_indicator(width, head, value):
    i = jnp.arange(width) // head
    return jnp.where(i[:, None] == i[None, :], value, 0.0).astype(jnp.bfloat16)


def _ada_kernel(c_ref, w_ref, b_ref, o_ref):
    c = c_ref[...]
    h = c * _sigmoid(c)
    o_ref[...] = jnp.dot(h.astype(jnp.bfloat16), w_ref[...].astype(jnp.bfloat16),
                         preferred_element_type=jnp.float32) + b_ref[...]


def ada_modulation(c8, w, b):
    n = w.shape[1]
    tn = 1024
    return pl.pallas_call(
        _ada_kernel,
        grid=(n // tn,),
        in_specs=[pl.BlockSpec((8, D_MODEL), lambda j: (0, 0)),
                  pl.BlockSpec((D_MODEL, tn), lambda j: (0, j)),
                  pl.BlockSpec((1, tn), lambda j: (0, j))],
        out_specs=pl.BlockSpec((8, tn), lambda j: (0, j)),
        out_shape=jax.ShapeDtypeStruct((8, n), jnp.float32),
        compiler_params=_params(1),
    )(c8, w, b)


def _proj_kernel(x_ref, sh_ref, sc_ref, w_ref, o_ref, h_ref, *, n_ctx):
    @pl.when(pl.program_id(1) == 0)
    def _():
        is_ctx = _ctx_rows(x_ref.shape[0], n_ctx)
        h = _ln_rows(x_ref[...]) * (1.0 + _mod_row(sc_ref, is_ctx)) + _mod_row(sh_ref, is_ctx)
        h_ref[...] = h.astype(jnp.bfloat16)

    o_ref[...] = jnp.dot(h_ref[...], w_ref[...], preferred_element_type=jnp.float32)


def modulated_projection(x, mod, w_bf16, n_ctx):
    m = x.shape[0]
    n = w_bf16.shape[1]
    tm = PROJ_TM
    return pl.pallas_call(
        functools.partial(_proj_kernel, n_ctx=n_ctx),
        grid=(m // tm, n // CB),
        in_specs=[pl.BlockSpec((tm, D_MODEL), lambda i, j: (i, 0)),
                  pl.BlockSpec((8, D_MODEL), lambda i, j: (0, 0)),
                  pl.BlockSpec((8, D_MODEL), lambda i, j: (0, 1)),
                  pl.BlockSpec((D_MODEL, CB), lambda i, j: (0, j))],
        out_specs=pl.BlockSpec((tm, CB), lambda i, j: (i, j)),
        out_shape=jax.ShapeDtypeStruct((m, n), jnp.float32),
        scratch_shapes=[pltpu.VMEM((tm, D_MODEL), jnp.bfloat16)],
        compiler_params=_params(2),
    )(x, mod, mod, w_bf16)


def _halo_specs(tp, n_rows, col):
    per = tp // HALO
    last = n_rows // HALO - 1
    main = pl.BlockSpec((tp, CB), lambda i, *_: (i, col(*_) if callable(col) else col))
    prev = pl.BlockSpec((HALO, CB), lambda i, *_: (jnp.maximum(i * per - 1, 0), col(*_) if callable(col) else col))
    nxt = pl.BlockSpec((HALO, CB), lambda i, *_: (jnp.minimum((i + 1) * per, last), col(*_) if callable(col) else col))
    return [main, prev, nxt]


def _neighbours(x, prev, nxt, n_ctx, n_rows):
    tp = x.shape[0]
    loc = lax.broadcasted_iota(jnp.int32, (tp, 1), 0)
    row = pl.program_id(0) * tp + loc
    xp = jnp.where(loc == 0, prev[HALO - 1:HALO, :], pltpu.roll(x, 1, axis=0))
    xp = jnp.where((row == 0) | (row == n_ctx), 0.0, xp)
    xn = jnp.where(loc == tp - 1, nxt[0:1, :], pltpu.roll(x, tp - 1, axis=0))
    xn = jnp.where((row == n_ctx - 1) | (row == n_rows - 1), 0.0, xn)
    return xp, xn


def _conv3_kernel(z_ref, zp_ref, zn_ref, w_ref, b_ref, o_ref, *, n_ctx, n_rows):
    x = z_ref[...]
    xp, xn = _neighbours(x, zp_ref[...], zn_ref[...], n_ctx, n_rows)
    w = w_ref[...]
    o_ref[...] = xp * w[0:1, :] + x * w[1:2, :] + xn * w[2:3, :] + b_ref[...]


def hyena_conv3(z, w, b, n_ctx):
    n_rows = z.shape[0]
    tp = PREP_ROWS
    ncol = w.shape[1] // CB
    return pl.pallas_call(
        functools.partial(_conv3_kernel, n_ctx=n_ctx, n_rows=n_rows),
        grid=(n_rows // tp, ncol),
        in_specs=_halo_specs(tp, n_rows, lambda c: COL_HY + c)
        + [pl.BlockSpec((3, CB), lambda i, c: (0, c)), pl.BlockSpec((1, CB), lambda i, c: (0, c))],
        out_specs=pl.BlockSpec((tp, CB), lambda i, c: (i, c)),
        out_shape=jax.ShapeDtypeStruct((n_rows, w.shape[1]), jnp.float32),
        compiler_params=_params(2),
    )(z, z, z, w, b)


HY_EMB_PAD = 40
HY_FILTER_ROWS = 512


def hyena_features(L):
    t = jnp.linspace(0.0, 1.0, L, dtype=jnp.float32)[:, None]
    n_bands = (HY_EMB - 1) // 2
    f = jnp.linspace(1e-4, n_bands - 1, n_bands, dtype=jnp.float32)[None, :]
    ang = (2.0 * math.pi / L) * jnp.arange(L, dtype=jnp.float32)[:, None] * f
    z = jnp.concatenate([t, jnp.cos(ang), -jnp.sin(ang)], -1)
    return jnp.pad(z, ((0, 0), (0, HY_EMB_PAD - HY_EMB)))


def _filter_kernel(z_ref, w1_ref, b1_ref, w2_ref, b2_ref, w3_ref, b3_ref, fr_ref, dl_ref, h_ref, s_ref):
    i = pl.program_id(0)
    z = z_ref[...]
    fr = fr_ref[...]
    h = jnp.sin(fr * (_dot(z, w1_ref[...]) + b1_ref[...]))
    h = jnp.sin(fr * (_dot(h, w2_ref[...]) + b2_ref[...]))
    h = _dot(h, w3_ref[...]) + b3_ref[...]
    win = jnp.exp(-z[:, 0:1] * dl_ref[...])
    h = h * jnp.concatenate([win] * (h.shape[1] // win.shape[1]), axis=1)

    @pl.when(i == 0)
    def _():
        s_ref[...] = jnp.zeros_like(s_ref)

    s_ref[...] += jnp.sum(jnp.abs(h), axis=0, keepdims=True)
    row = lax.broadcasted_iota(jnp.int32, h.shape, 0) + i * h.shape[0]
    col = lax.broadcasted_iota(jnp.int32, h.shape, 1)
    neg = (col // GROUP_W) % 2 == 1
    h_ref[...] = jnp.where(neg & (row == 0), 0.0, h)


def hyena_filter_bank(L, w1, b1, w2, b2, w3, b3, freq):
    z = hyena_features(L)
    w1p = jnp.pad(w1, ((0, HY_EMB_PAD - HY_EMB), (0, 0)))
    max_decay = math.log(HY_TARGET) / HY_FAST_DECAY
    min_decay = math.log(HY_TARGET) / HY_SLOW_DECAY
    deltas = jnp.abs(jnp.linspace(min_decay, max_decay, GROUP_W, dtype=jnp.float32))[None, :]
    n = w3.shape[1]
    tr = min(L, HY_FILTER_ROWS)
    args = (z, w1p, b1[None, :], w2, b2[None, :], w3, b3[None, :], freq[None, :], deltas)
    return pl.pallas_call(
        _filter_kernel,
        grid=(L // tr,),
        in_specs=[pl.BlockSpec((tr, HY_EMB_PAD), lambda i: (i, 0))] + [_full(a) for a in args[1:]],
        out_specs=[pl.BlockSpec((tr, n), lambda i: (i, 0)), pl.BlockSpec((1, n), lambda i: (0, 0))],
        out_shape=[jax.ShapeDtypeStruct((L, n), jnp.float32), jax.ShapeDtypeStruct((1, n), jnp.float32)],
        compiler_params=_params(1),
    )(*args)


FFT_NB = 8
FFT_CB = 256
FFT_N1 = 128


def fft_tables(n1, n2):
    N = n1 * n2
    a = jnp.arange(n1, dtype=jnp.float32)[:, None, None]
    k2 = jnp.arange(n2, dtype=jnp.float32)[None, :, None]
    b = jnp.arange(n2 // 2, dtype=jnp.float32)[None, None, :]
    ph = (jnp.mod(a * k2, float(N)) / N + jnp.mod(b * k2, float(n2)) / n2) * (-2.0 * math.pi)
    g = jnp.concatenate([jnp.cos(ph), jnp.sin(ph)], axis=1)
    ginv = jnp.transpose(g, (0, 2, 1)) / N
    k1 = jnp.arange(n1, dtype=jnp.float32)[:, None]
    aa = jnp.arange(n1, dtype=jnp.float32)[None, :]
    f = jnp.mod(k1 * aa, float(n1)) * (-2.0 * math.pi / n1)
    fr, fi = jnp.cos(f), jnp.sin(f)
    ff = jnp.concatenate([jnp.concatenate([fr, -fi], axis=1), jnp.concatenate([fi, fr], axis=1)], axis=0)
    return {"g": _split(g), "ginv": _split(ginv), "ff": _split(ff), "fft": _split(ff.T)}


def _stage_a_kernel(u_ref, gh_ref, gl_ref, o_ref):
    for j in range(FFT_NB):
        o_ref[j] = _mm3((gh_ref[j], gl_ref[j]), _split(u_ref[:, j, :]))


def fft_stage_a(u3, col, ncol, g):
    nb, n1, _ = u3.shape
    gspec = pl.BlockSpec((FFT_NB, 4 * nb, nb), lambda c, i: (i, 0, 0))
    return pl.pallas_call(
        _stage_a_kernel,
        grid=(ncol, n1 // FFT_NB),
        in_specs=[pl.BlockSpec((nb, FFT_NB, CB), lambda c, i: (0, i, col + c)), gspec, gspec],
        out_specs=pl.BlockSpec((FFT_NB, 4 * nb, CB), lambda c, i: (i, 0, c)),
        out_shape=jax.ShapeDtypeStruct((n1, 4 * nb, ncol * CB), jnp.float32),
        compiler_params=_params(2),
    )(u3, g[0], g[1])


def _stage_b_filter_kernel(re_ref, im_ref, fh_ref, fl_ref, o_ref):
    ff = (fh_ref[...], fl_ref[...])
    for j in range(FFT_NB):
        o_ref[j] = _mm3(ff, _split(jnp.concatenate([re_ref[:, j, :], im_ref[:, j, :]], axis=0)))


def fft_stage_b_filter(t1, ff):
    n1, n2x2, C = t1.shape
    n2 = n2x2 // 2
    blk = lambda off: pl.BlockSpec((n1, FFT_NB, FFT_CB), lambda c, i: (0, i + off, c))
    mat = pl.BlockSpec((2 * n1, 2 * n1), lambda c, i: (0, 0))
    return pl.pallas_call(
        _stage_b_filter_kernel,
        grid=(C // FFT_CB, n2 // FFT_NB),
        in_specs=[blk(0), blk(n2 // FFT_NB), mat, mat],
        out_specs=pl.BlockSpec((FFT_NB, 2 * n1, FFT_CB), lambda c, i: (i, 0, c)),
        out_shape=jax.ShapeDtypeStruct((n2, 2 * n1, C), jnp.float32),
        compiler_params=_params(2),
    )(t1, t1, ff[0], ff[1])


def _filter_spectrum(p, q, s, n1):
    return (p[:n1] + q[:n1]) * s, (p[n1:] - q[n1:]) * s


def _stage_b_conv_kernel(re_ref, im_ref, p_ref, q_ref, s_ref, fh_ref, fl_ref, fth_ref, ftl_ref, ore_ref, oim_ref):
    n1 = re_ref.shape[0]
    s = s_ref[...]
    ff, fft_ = (fh_ref[...], fl_ref[...]), (fth_ref[...], ftl_ref[...])
    for j in range(FFT_NB):
        x = _mm3(ff, _split(jnp.concatenate([re_ref[:, j, :], im_ref[:, j, :]], axis=0)))
        hr, hi = _filter_spectrum(p_ref[j], q_ref[j], s, n1)
        xr, xi = x[:n1], x[n1:]
        z = _mm3(fft_, _split(jnp.concatenate([xr * hr - xi * hi, xr * hi + xi * hr], axis=0)))
        ore_ref[:, j, :] = z[:n1]
        oim_ref[:, j, :] = z[n1:]


def fft_stage_b_conv(t1, spec, col_p, col_q, inv_norm, ff, fft_):
    n1, n2x2, C = t1.shape
    n2 = n2x2 // 2
    ncb = C // FFT_CB
    blk = lambda off: pl.BlockSpec((n1, FFT_NB, FFT_CB), lambda c, i: (0, i + off, c))
    mat = pl.BlockSpec((2 * n1, 2 * n1), lambda c, i: (0, 0))
    sp = lambda col: pl.BlockSpec((FFT_NB, 2 * n1, FFT_CB), lambda c, i: (i, 0, col * ncb + c))
    return pl.pallas_call(
        _stage_b_conv_kernel,
        grid=(ncb, n2 // FFT_NB),
        in_specs=[blk(0), blk(n2 // FFT_NB), sp(col_p), sp(col_q), pl.BlockSpec((1, FFT_CB), lambda c, i: (0, c)),
                  mat, mat, mat, mat],
        out_specs=[blk(0), blk(0)],
        out_shape=[jax.ShapeDtypeStruct((n1, n2, C), jnp.float32)] * 2,
        compiler_params=_params(2),
    )(t1, t1, spec, spec, inv_norm, ff[0], ff[1], fft_[0], fft_[1])


def _stage_a_inv_kernel(re_ref, im_ref, gh_ref, gl_ref, u_ref, gate_ref, bias_ref, o_ref):
    for j in range(FFT_NB):
        y = _mm3((gh_ref[j], gl_ref[j]), _split(jnp.concatenate([re_ref[j], im_ref[j]], axis=0)))
        o_ref[:, j, :] = gate_ref[:, j, :] * (y + u_ref[:, j, :] * bias_ref[...])


def fft_stage_a_inv(t2re, t2im, ginv, u3, u_col, gate3, gate_col, bias):
    n1, n2, C = t2re.shape
    nb = n2 // 2
    tb = pl.BlockSpec((FFT_NB, n2, CB), lambda c, i: (i, 0, c))
    gb = pl.BlockSpec((FFT_NB, nb, 2 * n2), lambda c, i: (i, 0, 0))
    ub = lambda col: pl.BlockSpec((nb, FFT_NB, CB), lambda c, i: (0, i, col + c))
    return pl.pallas_call(
        _stage_a_inv_kernel,
        grid=(C // CB, n1 // FFT_NB),
        in_specs=[tb, tb, gb, gb, ub(u_col), ub(gate_col), pl.BlockSpec((1, CB), lambda c, i: (0, c))],
        out_specs=ub(0),
        out_shape=jax.ShapeDtypeStruct((nb, n1, C), jnp.float32),
        compiler_params=_params(2),
    )(t2re, t2im, ginv[0], ginv[1], u3, gate3, bias)


def hyena_long_conv_chain(u, biases, filt, colsum, tabs):
    L = u.shape[0]
    C = GROUP_W
    n1 = FFT_N1
    n2 = 2 * L // n1
    spec = fft_stage_b_filter(fft_stage_a(filt.reshape(n2 // 2, n1, filt.shape[1]), 0, filt.shape[1] // CB, tabs["g"]),
                              tabs["ff"])
    s4 = colsum.reshape(HY_ORDER, 2, C)
    inv_norm = 1.0 / (s4[:, 0] + s4[:, 1])
    u3 = u.reshape(n2 // 2, n1, u.shape[1])
    y3, y_col = u3, 0
    for n in range(HY_ORDER):
        t1 = fft_stage_a(y3, y_col, 1, tabs["g"])
        t2re, t2im = fft_stage_b_conv(t1, spec, 2 * n, 2 * n + 1, inv_norm[n][None, :], tabs["ff"], tabs["fft"])
        y3 = fft_stage_a_inv(t2re, t2im, tabs["ginv"], y3, y_col, u3, n + 1, biases[n][None, :])
        y_col = 0
    return y3.reshape(L, C)


def dense_dft_tables(n):
    N = 2 * n
    k = jnp.arange(N, dtype=jnp.float32)[:, None]
    t = jnp.arange(n, dtype=jnp.float32)[None, :]
    ph = jnp.mod(k * t, float(N)) * (2.0 * math.pi / N)
    fd = jnp.concatenate([jnp.cos(ph), -jnp.sin(ph)], axis=0)
    return _split(fd), _split(fd.T / N)


def _hyena_ctx_kernel(u_ref, h_ref, s_ref, bias_ref, fdh_ref, fdl_ref, fth_ref, ftl_ref, o_ref):
    C = GROUP_W
    fd, ft = (fdh_ref[...], fdl_ref[...]), (fth_ref[...], ftl_ref[...])
    K = fd[0].shape[0] // 2
    hs = _mm3(fd, _split(h_ref[...]))
    s = s_ref[...]
    y = u_ref[:, 0:C]
    for n in range(HY_ORDER):
        cp, cq = 2 * n * C, (2 * n + 1) * C
        inv = 1.0 / (s[:, cp:cp + C] + s[:, cq:cq + C])
        hr, hi = _filter_spectrum(hs[:, cp:cp + C], hs[:, cq:cq + C], inv, K)
        x = _mm3(fd, _split(y))
        xr, xi = x[:K], x[K:]
        conv = _mm3(ft, _split(jnp.concatenate([xr * hr - xi * hi, xr * hi + xi * hr], axis=0)))
        y = u_ref[:, (n + 1) * C:(n + 2) * C] * (conv + y * bias_ref[n:n + 1, :])
    o_ref[...] = y


def hyena_ctx(u, filt, colsum, biases, tabs):
    args = (u, filt, colsum, biases, tabs[0][0], tabs[0][1], tabs[1][0], tabs[1][1])
    return pl.pallas_call(
        _hyena_ctx_kernel,
        in_specs=[_full(a) for a in args],
        out_specs=pl.BlockSpec((u.shape[0], GROUP_W), lambda: (0, 0)),
        out_shape=jax.ShapeDtypeStruct((u.shape[0], GROUP_W), jnp.float32),
        compiler_params=pltpu.CompilerParams(vmem_limit_bytes=VMEM_LIMIT),
    )(*args)


SCAN_T = 64
GROUP_LANES = 256
RW_SUB = 16
HG_SUB = 16


def _chunk_index(d, i, n_ctx, n_all):
    bwd = jnp.where(i < n_ctx, n_ctx - 1 - i, n_all + n_ctx - 1 - i)
    return jnp.where(d == 0, i, bwd)


def _stacking(T, S, nh, head, sign):
    G = nh * head
    nb = T // S
    n = nh * T
    rr = lax.broadcasted_iota(jnp.int32, (n, G), 0)
    same = ((rr // S) % nh) == (lax.broadcasted_iota(jnp.int32, (n, G), 1) // head)

    def bd(x):
        pieces = []
        for i in range(nb):
            pieces += [x[i * S:(i + 1) * S]] * nh
        return jnp.where(same, jnp.concatenate(pieces, axis=0), 0.0)

    def collapse(o):
        outs = []
        for i in range(nb):
            acc = o[i * nh * S:i * nh * S + S]
            for h in range(1, nh):
                acc = acc + o[i * nh * S + h * S:i * nh * S + (h + 1) * S]
            outs.append(acc)
        return jnp.concatenate(outs, axis=0)

    rt = lax.broadcasted_iota(jnp.int32, (n, n), 0)
    cs = lax.broadcasted_iota(jnp.int32, (n, n), 1)
    t_r = (rt // (nh * S)) * S + rt % S
    t_c = (cs // (nh * S)) * S + cs % S
    same_h = ((rt // S) % nh) == ((cs // S) % nh)
    before = same_h & ((t_r - t_c) * sign > 0)
    return bd, collapse, before, rt, cs


def _softplus(x):
    return jnp.maximum(x, 0.0) + jnp.log(1.0 + jnp.exp(-jnp.abs(x)))


def _rwkv_prep_kernel(*refs, n_ctx, n_rows):
    zrefs, rest = refs[:15], refs[15:]
    (mu_ref, kk_w_ref, ka_ref, w0_ref, a0_ref, w2h_ref, w2l_ref, a2h_ref, a2l_ref, ones_ref,
     r_ref, k_ref, v_ref, g_ref, kk_ref, lw_ref, kd_ref, a_ref) = rest
    slabs = []
    for c in range(5):
        x = zrefs[3 * c][...]
        xp, xn = _neighbours(x, zrefs[3 * c + 1][...], zrefs[3 * c + 2][...], n_ctx, n_rows)
        slabs.append(x + (0.5 * (xp + xn) - x) * mu_ref[:, c * CB:(c + 1) * CB])
    r, k, v, g, lora = slabs
    r_ref[...], k_ref[...], v_ref[...], g_ref[...] = r, k, v, g
    kk = k * kk_w_ref[...]
    ss = _mm2(kk * kk, ones_ref[...])
    kk_ref[...] = kk * lax.rsqrt(jnp.maximum(ss, 1e-24))
    lora_t = _split(jnp.tanh(lora))
    lora_s = _split(lora)
    for d in range(2):
        w_log = -_softplus(-(w0_ref[d:d + 1, :] + _mm3(lora_t, (w2h_ref[d], w2l_ref[d])))) - 0.5
        lw_ref[d] = -jnp.exp(w_log)
        a = _sigmoid(a0_ref[d:d + 1, :] + _mm3(lora_s, (a2h_ref[d], a2l_ref[d])))
        a_ref[d] = a
        kd_ref[d] = k * (1.0 + (a - 1.0) * ka_ref[...])


def rwkv_prep(z, mu, k_k, k_a, w0, a0, w2, a2, n_ctx):
    n_rows = z.shape[0]
    tp = PREP_ROWS
    mu_p = jnp.pad(mu, (0, 5 * CB - RW_REAL))[None, :]
    w2p = jnp.zeros((2, CB, GROUP_W), jnp.float32)
    a2p = jnp.zeros((2, CB, GROUP_W), jnp.float32)
    for d in range(2):
        w2p = w2p.at[d, d * RW_LORA:(d + 1) * RW_LORA].set(w2[d])
        a2p = a2p.at[d, (2 + d) * RW_LORA:(3 + d) * RW_LORA].set(a2[d])
    w2s, a2s = _split(w2p), _split(a2p)
    small = (mu_p, k_k[None, :], k_a[None, :], w0, a0, w2s[0], w2s[1], a2s[0], a2s[1],
             _block_indicator(GROUP_W, RW_HEAD, 1.0))
    zspecs = []
    for c in range(5):
        zspecs += _halo_specs(tp, n_rows, COL_RW + c)
    one = pl.BlockSpec((tp, CB), lambda i: (i, 0))
    two = pl.BlockSpec((2, tp, CB), lambda i: (0, i, 0))
    s1 = jax.ShapeDtypeStruct((n_rows, GROUP_W), jnp.float32)
    s2 = jax.ShapeDtypeStruct((2, n_rows, GROUP_W), jnp.float32)
    return pl.pallas_call(
        functools.partial(_rwkv_prep_kernel, n_ctx=n_ctx, n_rows=n_rows),
        grid=(n_rows // tp,),
        in_specs=zspecs + [_full(a) for a in small],
        out_specs=[one] * 5 + [two] * 3,
        out_shape=[s1] * 5 + [s2] * 3,
        compiler_params=_params(1),
    )(*([z] * 15), *small)


def _rwkv_scan_kernel(r_ref, v_ref, kk_ref, lw_ref, k_ref, a_ref, o_ref, ht_ref, *, head):
    T, G, S = SCAN_T, GROUP_LANES, RW_SUB
    nh = G // head
    nb = T // S
    n = nh * T
    d = pl.program_id(0)
    sign = 1 - 2 * d

    @pl.when(pl.program_id(2) == 0)
    def _():
        ht_ref[...] = jnp.zeros_like(ht_ref)

    r, v, kk = r_ref[...], v_ref[...], kk_ref[...]
    lw, k, a = lw_ref[0], k_ref[0], a_ref[0]

    ti = lax.broadcasted_iota(jnp.int32, (T, T), 0)
    si = lax.broadcasted_iota(jnp.int32, (T, T), 1)
    tri_incl = jnp.where((ti - si) * sign >= 0, 1.0, 0.0)
    c = _dot(tri_incl, lw)
    ctot = jnp.sum(lw, axis=0, keepdims=True)
    beta = kk * a
    einv = jnp.exp(-c)
    efin = jnp.exp(ctot - c)
    a_bar = -kk * jnp.exp(c - lw)
    r_bar = r * jnp.exp(c)

    bd, collapse, before, rt, cs = _stacking(T, S, nh, head, sign)
    incl = before | (rt == cs)
    diag_blk = (rt // (nh * S)) == (cs // (nh * S))
    eye = jnp.where(rt == cs, 1.0, 0.0)

    lhs = _split(jnp.concatenate([bd(a_bar), bd(r_bar)], axis=0))
    rhs = _split(jnp.concatenate([bd(k * einv), bd(beta * einv)], axis=0))
    m = _mm3(lhs, rhs, NT)
    ak, ab, rk, rb = m[:n, :n], m[:n, n:], m[n:, :n], m[n:, n:]
    ht = ht_ref[...]
    g = _mm3(lhs, _split(ht), NT)
    v_s = _split(bd(v))
    x = g[:n] + _mm3(_split(jnp.where(before, ak, 0.0)), v_s)
    lb = jnp.where(before, ab, 0.0)
    ld = jnp.where(diag_blk, lb, 0.0)
    lo = lb - ld
    xd = eye + ld
    lp = _split(ld)
    p = 2
    while p < S:
        lp = _split(_mm3(lp, lp))
        xd = xd + _mm3(lp, _split(xd))
        p *= 2
    xd_s = _split(xd)
    n_s = _split(_mm3(xd_s, _split(lo)))
    y = _mm3(xd_s, _split(x))
    y = y + _mm3(n_s, _split(y))
    p = 2
    while p < nb:
        n_s = _split(_mm3(n_s, n_s))
        y = y + _mm3(n_s, _split(y))
        p *= 2
    u_s = _split(y)
    o = (g[n:] + _mm3(_split(jnp.where(incl, rk, 0.0)), v_s)) + _mm3(_split(jnp.where(incl, rb, 0.0)), u_s)
    o_ref[0] = collapse(o)
    ht_ref[...] = (ht * jnp.exp(ctot) + _mm3(v_s, _split(bd(k * efin)), TN)) + _mm3(u_s, _split(bd(beta * efin)), TN)


def rwkv_scan(r, v, kk, lw2, k2, a2, n_ctx_rows):
    N, W = r.shape
    T = SCAN_T
    n_all, n_ctx = N // T, n_ctx_rows // T
    shared = pl.BlockSpec((T, GROUP_LANES), lambda d, g, i: (_chunk_index(d, i, n_ctx, n_all), g))
    per_dir = pl.BlockSpec((1, T, GROUP_LANES), lambda d, g, i: (d, _chunk_index(d, i, n_ctx, n_all), g))
    return pl.pallas_call(
        functools.partial(_rwkv_scan_kernel, head=RW_HEAD),
        grid=(2, W // GROUP_LANES, n_all),
        in_specs=[shared, shared, shared, per_dir, per_dir, per_dir],
        out_specs=per_dir,
        out_shape=jax.ShapeDtypeStruct((2, N, W), jnp.float32),
        scratch_shapes=[pltpu.VMEM((GROUP_LANES, GROUP_LANES), jnp.float32)],
        compiler_params=_params(3),
    )(r, v, kk, lw2, k2, a2)


def rope_tables(n_ctx, L):
    d_axis = RT_HEAD // 2
    half = d_axis // 2
    inv = ROPE_BASE ** (-jnp.arange(0, d_axis, 2, dtype=jnp.float32) / d_axis)
    t = jnp.arange(L)
    pos = jnp.stack([(t // GRID_W).astype(jnp.float32), (t % GRID_W).astype(jnp.float32)], axis=1)
    j = jnp.arange(RT_HEAD)
    ang = pos[:, j // d_axis] * inv[j % half][None, :]
    sgn = jnp.where((j % d_axis) < half, -1.0, 1.0)[None, :]
    cos = jnp.concatenate([jnp.ones((n_ctx, RT_HEAD), jnp.float32), jnp.cos(ang)], axis=0)
    sin = jnp.concatenate([jnp.zeros((n_ctx, RT_HEAD), jnp.float32), jnp.sin(ang) * sgn], axis=0)
    return jnp.tile(cos, (1, RT_HEADS)), jnp.tile(sin, (1, RT_HEADS))


def _rotate(x, cos, sin):
    G = x.shape[1]
    half = RT_HEAD // 4
    lane = lax.broadcasted_iota(jnp.int32, x.shape, 1)
    partner = jnp.where((lane % (2 * half)) < half, pltpu.roll(x, G - half, axis=1), pltpu.roll(x, half, axis=1))
    return x * cos + partner * sin


def _retention_scan_kernel(q_ref, k_ref, v_ref, cos_ref, sin_ref, lg_ref, o_ref, st_ref, *, head):
    T, G = SCAN_T, GROUP_LANES
    nh = G // head
    d = pl.program_id(0)
    sign = 1 - 2 * d

    @pl.when(pl.program_id(2) == 0)
    def _():
        st_ref[...] = jnp.zeros_like(st_ref)

    cos, sin, lg = cos_ref[...], sin_ref[...], lg_ref[...]
    q = _rotate(q_ref[...], cos, sin)
    k = _rotate(k_ref[...], cos, sin) * (head ** -0.5)
    v = v_ref[...]
    t = lax.broadcasted_iota(jnp.int32, (T, 1), 0)
    pos = (t + d * (T - 1 - 2 * t) + 1).astype(jnp.float32)
    c = pos * lg
    ctot = float(T) * lg
    q_bar = q * jnp.exp(c)
    k_til = k * jnp.exp(-c)
    k_hat = k * jnp.exp(ctot - c)

    bd, collapse, before, rt, cs = _stacking(T, T, nh, head, sign)
    incl = before | (rt == cs)
    q_s, v_s = _split(bd(q_bar)), _split(bd(v))
    st = st_ref[...]
    scores = jnp.where(incl, _mm3(q_s, _split(bd(k_til)), NT), 0.0)
    o_ref[0] = collapse(_mm3(q_s, _split(st), NT) + _mm3(_split(scores), v_s))
    st_ref[...] = st * jnp.exp(ctot) + _mm3(v_s, _split(bd(k_hat)), TN)


def retention_scan(z, cos, sin, lg, n_ctx_rows):
    N = z.shape[0]
    T = SCAN_T
    per = CB // GROUP_LANES
    n_all, n_ctx = N // T, n_ctx_rows // T
    zc = lambda col: pl.BlockSpec((T, GROUP_LANES),
                                  lambda d, g, i: (_chunk_index(d, i, n_ctx, n_all), (COL_RT + col) * per + g))
    tab = pl.BlockSpec((T, GROUP_LANES), lambda d, g, i: (_chunk_index(d, i, n_ctx, n_all), g))
    return pl.pallas_call(
        functools.partial(_retention_scan_kernel, head=RT_HEAD),
        grid=(2, GROUP_W // GROUP_LANES, n_all),
        in_specs=[zc(0), zc(1), zc(2), tab, tab, pl.BlockSpec((1, GROUP_LANES), lambda d, g, i: (0, g))],
        out_specs=pl.BlockSpec((1, T, GROUP_LANES), lambda d, g, i: (d, _chunk_index(d, i, n_ctx, n_all), g)),
        out_shape=jax.ShapeDtypeStruct((2, N, GROUP_W), jnp.float32),
        scratch_shapes=[pltpu.VMEM((GROUP_LANES, GROUP_LANES), jnp.float32)],
        compiler_params=_params(3),
    )(z, z, z, cos, sin, lg)


def _gla_scan_kernel(q_ref, f_ref, i_ref, lb_ref, o_ref, st_ref, *, head):
    T, S = SCAN_T, HG_SUB
    W = q_ref.shape[-1]
    nh = W // head
    d = pl.program_id(0)
    sign = 1 - 2 * d

    @pl.when(pl.program_id(1) == 0)
    def _():
        st_ref[...] = jnp.zeros_like(st_ref)

    ti = lax.broadcasted_iota(jnp.int32, (S, S), 0)
    si = lax.broadcasted_iota(jnp.int32, (S, S), 1)
    tri_incl = jnp.where((ti - si) * sign >= 0, 1.0, 0.0)
    row = lax.broadcasted_iota(jnp.int32, (S, 1), 0)
    lb = lb_ref[0]

    for j in range(T // S):
        jb = j + d * (T // S - 1 - 2 * j)
        rows = pl.ds(pl.multiple_of(jb * S, S), S)
        q = q_ref[rows, :]
        q = q * _sigmoid(q)
        v = i_ref[rows, :]
        gate = lb + (1.0 - lb) * _sigmoid(f_ref[rows, :])
        lf = jnp.log(jnp.maximum(gate, HG_MIN_GATE))
        k = 1.0 - gate
        b = _dot(tri_incl, lf)
        btot = jnp.sum(lf, axis=0, keepdims=True)
        qe = q * jnp.exp(b)
        ke = k * jnp.exp(btot - b)
        for h in range(nh):
            ls = slice(h * head, (h + 1) * head)
            qh, kh, vh, bh = q[:, ls], k[:, ls], v[:, ls], b[:, ls]
            st = st_ref[h]
            o = _mm3(_split(qe[:, ls]), _split(st), NT)
            for s in range(S):
                e = jnp.exp(jnp.minimum(bh - bh[s:s + 1, :], 0.0))
                a_s = jnp.sum(qh * kh[s:s + 1, :] * e, axis=-1, keepdims=True)
                a_s = jnp.where((row - s) * sign >= 0, a_s, 0.0)
                o = o + a_s * vh[s:s + 1, :]
            o_ref[0, rows, ls] = o
            st_ref[h] = st * jnp.exp(btot[:, ls]) + _mm3(_split(vh), _split(ke[:, ls]), TN)


def gla_scan(z, lb2, n_ctx_rows):
    N = z.shape[0]
    W = GROUP_W
    T = SCAN_T
    n_all, n_ctx = N // T, n_ctx_rows // T
    zc = lambda col: pl.BlockSpec((T, W), lambda d, i: (_chunk_index(d, i, n_ctx, n_all), COL_HG + col))
    zf = pl.BlockSpec((T, W), lambda d, i: (_chunk_index(d, i, n_ctx, n_all), COL_HG + 1 + d))
    return pl.pallas_call(
        functools.partial(_gla_scan_kernel, head=HG_HEAD),
        grid=(2, n_all),
        in_specs=[zc(0), zf, zc(3), pl.BlockSpec((1, 1, W), lambda d, i: (d, 0, 0))],
        out_specs=pl.BlockSpec((1, T, W), lambda d, i: (d, _chunk_index(d, i, n_ctx, n_all), 0)),
        out_shape=jax.ShapeDtypeStruct((2, N, W), jnp.float32),
        scratch_shapes=[pltpu.VMEM((W // HG_HEAD, HG_HEAD, HG_HEAD), jnp.float32)],
        compiler_params=_params(2),
    )(z, z, z, lb2)


def _outproj_kernel(hy_ref, rwo_ref, r_ref, k_ref, v_ref, rwg_ref, rto_ref, rtg_ref, hgo_ref, hgg_ref,
                    rk_ref, gng_ref, gnb_ref, hgn_ref, avg64_ref, avg128_ref,
                    w_ref, x_ref, gate_ref, g_ref, b_ref, o_ref, *, n_ctx):
    avg64, avg128 = avg64_ref[...], avg128_ref[...]

    def head_norm(o, avg, eps, centre):
        if centre:
            o = o - _mm2(o, avg)
        return o * lax.rsqrt(_mm2(o * o, avg) + eps)

    silu = lambda t: t * _sigmoid(t)
    y_rw = head_norm(rwo_ref[0] + rwo_ref[1], avg64, RW_GN_EPS, True) * gng_ref[...] + gnb_ref[...]
    bonus = (float(RW_HEAD) * _mm2(r_ref[...] * k_ref[...] * rk_ref[...], avg64)) * v_ref[...]
    y_rw = (y_rw + bonus) * _sigmoid(rwg_ref[...])
    y_rt = head_norm(rto_ref[0] + rto_ref[1], avg64, HEAD_NORM_EPS, True) * silu(rtg_ref[...])
    y_hg = head_norm(hgo_ref[0] + hgo_ref[1], avg128, HEAD_NORM_EPS, False) * hgn_ref[...] * silu(hgg_ref[...])
    y = None
    for m, ym in enumerate((hy_ref[...], y_rw, y_rt, y_hg)):
        part = jnp.dot(ym.astype(jnp.bfloat16), w_ref[m * GROUP_W:(m + 1) * GROUP_W, :],
                       preferred_element_type=jnp.float32)
        y = part if y is None else y + part
    is_ctx = _ctx_rows(x_ref.shape[0], n_ctx)
    r = ALPHA * x_ref[...] + _mod_row(gate_ref, is_ctx) * y
    o_ref[...] = _ln_rows(r) * g_ref[...] + b_ref[...]


def outproj_deepnorm(y_hy, rw_o2, rw_r, rw_k, rw_v, rw_g, rt_o2, hg_o2, z, r_k, gn_g, gn_b, hg_norm_g,
                     w_bf16, x, mod, g, b, n_ctx):
    m = x.shape[0]
    tm = OUT_TM
    one = pl.BlockSpec((tm, GROUP_W), lambda i: (i, 0))
    two = pl.BlockSpec((2, tm, GROUP_W), lambda i: (0, i, 0))
    zc = lambda col: pl.BlockSpec((tm, CB), lambda i: (i, col))
    row = pl.BlockSpec((tm, D_MODEL), lambda i: (i, 0))
    small = (r_k, gn_g, gn_b, hg_norm_g, _block_indicator(GROUP_W, RW_HEAD, 1.0 / RW_HEAD),
             _block_indicator(GROUP_W, HG_HEAD, 1.0 / HG_HEAD))
    return pl.pallas_call(
        functools.partial(_outproj_kernel, n_ctx=n_ctx),
        grid=(m // tm,),
        in_specs=[one, two, one, one, one, one, two, zc(COL_RT + 3), two, zc(COL_HG + 4)]
        + [_full(a) for a in small]
        + [_full(w_bf16), row, pl.BlockSpec((8, D_MODEL), lambda i: (0, 2)), _full(g), _full(b)],
        out_specs=row,
        out_shape=jax.ShapeDtypeStruct((m, D_MODEL), jnp.float32),
        compiler_params=_params(1),
    )(y_hy, rw_o2, rw_r, rw_k, rw_v, rw_g, rt_o2, z, hg_o2, z, *small, w_bf16, x, mod, g, b)


def _ffn_kernel(x_ref, sh_ref, sc_ref, w1_ref, w3_ref, w2_ref, gate_ref, g_ref, b_ref, o_ref, h_ref, acc_ref, *, n_ctx):
    j = pl.program_id(1)

    @pl.when(j == 0)
    def _():
        is_ctx = _ctx_rows(x_ref.shape[0], n_ctx)
        h = _ln_rows(x_ref[...]) * (1.0 + _mod_row(sc_ref, is_ctx)) + _mod_row(sh_ref, is_ctx)
        h_ref[...] = h.astype(jnp.bfloat16)
        acc_ref[...] = jnp.zeros_like(acc_ref)

    h = h_ref[...]
    a = jnp.dot(h, w1_ref[...], preferred_element_type=jnp.float32)
    u = jnp.dot(h, w3_ref[...], preferred_element_type=jnp.float32)
    s = (a * _sigmoid(a) * u).astype(jnp.bfloat16)
    acc_ref[...] += jnp.dot(s, w2_ref[...], preferred_element_type=jnp.float32)

    @pl.when(j == pl.num_programs(1) - 1)
    def _():
        is_ctx = _ctx_rows(x_ref.shape[0], n_ctx)
        r = ALPHA * x_ref[...] + _mod_row(gate_ref, is_ctx) * acc_ref[...]
        o_ref[...] = _ln_rows(r) * g_ref[...] + b_ref[...]


def ffn_deepnorm(x, mod, w1, w3, w2, g, b, n_ctx):
    m = x.shape[0]
    tm = PROJ_TM
    row = pl.BlockSpec((tm, D_MODEL), lambda i, j: (i, 0))
    modc = lambda c: pl.BlockSpec((8, D_MODEL), lambda i, j: (0, c))
    vec = pl.BlockSpec((1, D_MODEL), lambda i, j: (0, 0))
    return pl.pallas_call(
        functools.partial(_ffn_kernel, n_ctx=n_ctx),
        grid=(m // tm, FFN_HIDDEN // FFN_TF),
        in_specs=[row, modc(3), modc(4),
                  pl.BlockSpec((D_MODEL, FFN_TF), lambda i, j: (0, j)),
                  pl.BlockSpec((D_MODEL, FFN_TF), lambda i, j: (0, j)),
                  pl.BlockSpec((FFN_TF, D_MODEL), lambda i, j: (j, 0)),
                  modc(5), vec, vec],
        out_specs=row,
        out_shape=jax.ShapeDtypeStruct((m, D_MODEL), jnp.float32),
        scratch_shapes=[pltpu.VMEM((tm, D_MODEL), jnp.bfloat16),
                        pltpu.VMEM((tm, D_MODEL), jnp.float32)],
        compiler_params=_params(2),
    )(x, mod, mod, w1, w3, w2, mod, g, b)


def kernel(x, c, ctx, c_ctx, ada_w, ada_b, w_in, w_out, ln_g, ln_b, hy_conv_w, hy_conv_b, hy_w1, hy_b1, hy_w2, hy_b2, hy_w3, hy_b3, hy_freq, hy_bias, rw_mu, rw_w0, rw_w2, rw_a0, rw_a2, rw_k_k, rw_k_a, rw_r_k, rw_gn_g, rw_gn_b, hg_lb_raw, hg_norm_g, ffn_w1, ffn_w3, ffn_w2):
    L, n_ctx = x.shape[1], ctx.shape[1]
    sm = jax.nn.softmax(hg_lb_raw.astype(jnp.float32), axis=1)
    lower_bounds = jnp.cumsum(sm, axis=1) - sm[:, :1]
    log_gamma = jnp.log1p(-jnp.exp2(-5.0 - jnp.arange(RT_HEADS, dtype=jnp.float32)))
    lg = jnp.repeat(log_gamma, RT_HEAD)[None, :]
    cos, sin = rope_tables(n_ctx, L)
    fft_tabs = fft_tables(FFT_N1, 2 * L // FFT_N1)
    ctx_tabs = dense_dft_tables(n_ctx)

    c8 = jnp.zeros((8, D_MODEL), jnp.float32).at[0].set(c[0]).at[1].set(c_ctx)
    xs = jnp.concatenate([ctx[0], x[0]], axis=0)
    for l in range(DEPTH):
        with_ctx = l < DEPTH - 1
        mod = ada_modulation(c8, ada_w[l], ada_b[l][None, :])
        pad = jnp.zeros((D_MODEL, CB - 4 * RW_LORA), jnp.float32)
        split = (COL_RW * CB) + RW_REAL
        w_in_b = jnp.concatenate([w_in[l][:, :split], pad, w_in[l][:, split:]], axis=1).astype(jnp.bfloat16)
        w_out_b = w_out[l].astype(jnp.bfloat16)
        w1_b, w3_b, w2_b = (w[l].astype(jnp.bfloat16) for w in (ffn_w1, ffn_w3, ffn_w2))

        z = modulated_projection(xs, mod, w_in_b, n_ctx)

        u = hyena_conv3(z, hy_conv_w[l], hy_conv_b[l][None, :], n_ctx)
        hy_w = (hy_w1[l], hy_b1[l], hy_w2[l], hy_b2[l], hy_w3[l], hy_b3[l], hy_freq[l])
        filt, colsum = hyena_filter_bank(L, *hy_w)
        y_lat = hyena_long_conv_chain(u[n_ctx:], hy_bias[l], filt, colsum, fft_tabs)
        if with_ctx:
            filt_c, colsum_c = hyena_filter_bank(n_ctx, *hy_w)
            y_ctx = hyena_ctx(u[:n_ctx], filt_c, colsum_c, hy_bias[l], ctx_tabs)
        else:
            y_ctx = jnp.zeros((n_ctx, GROUP_W), jnp.float32)
        y_hy = jnp.concatenate([y_ctx, y_lat], axis=0)

        rw_r, rw_k, rw_v, rw_g, rw_kk, rw_lw, rw_kd, rw_a = rwkv_prep(
            z, rw_mu[l], rw_k_k[l], rw_k_a[l], rw_w0[l], rw_a0[l], rw_w2[l], rw_a2[l], n_ctx)
        rw_o2 = rwkv_scan(rw_r, rw_v, rw_kk, rw_lw, rw_kd, rw_a, n_ctx)
        rt_o2 = retention_scan(z, cos, sin, lg, n_ctx)
        hg_o2 = gla_scan(z, lower_bounds[:, l][:, None, :], n_ctx)

        xs = outproj_deepnorm(y_hy, rw_o2, rw_r, rw_k, rw_v, rw_g, rt_o2, hg_o2, z,
                              rw_r_k[l].reshape(1, GROUP_W), rw_gn_g[l][None, :], rw_gn_b[l][None, :],
                              hg_norm_g[l][None, :], w_out_b, xs, mod, ln_g[l, 0][None, :], ln_b[l, 0][None, :], n_ctx)
        xs = ffn_deepnorm(xs, mod, w1_b, w3_b, w2_b, ln_g[l, 1][None, :], ln_b[l, 1][None, :], n_ctx)
    return xs[n_ctx:][None]
```

```python
import functools
import math

import jax
import jax.numpy as jnp
from jax import lax
from jax.experimental import pallas as pl
from jax.experimental.pallas import tpu as pltpu

D_MODEL = 2048
DEPTH = 2
GRID_W = 64
N_MIXERS = 4
GROUP_W = D_MODEL // N_MIXERS
HY_ORDER = 2
HY_EMB = 33
HY_FAST_DECAY = 0.3
HY_SLOW_DECAY = 1.5
HY_TARGET = 1e-2
RW_HEAD = 64
RW_LORA = 96
RW_GN_EPS = 64e-5
RT_HEAD = 64
RT_HEADS = GROUP_W // RT_HEAD
ROPE_BASE = 10000.0
HG_HEAD = 128
HG_MIN_GATE = 1e-30
FFN_HIDDEN = 5632
ALPHA = (2 * DEPTH) ** 0.25
LN_EPS = 1e-6
HEAD_NORM_EPS = 1e-6

CB = 512
COL_HY = 0
COL_RW = 3
COL_RT = 8
COL_HG = 12
P_IN_PAD = 17 * CB
RW_REAL = 4 * GROUP_W + 4 * RW_LORA
PROJ_TM = 768
OUT_TM = 256
FFN_TF = 512
PREP_ROWS = 256
HALO = 8
VMEM_LIMIT = 56 * 1024 * 1024

HI = lax.Precision.HIGHEST
NN = ((1,), (0,))
NT = ((1,), (1,))
TN = ((0,), (0,))


def _params(n_axes):
    return pltpu.CompilerParams(dimension_semantics=("arbitrary",) * n_axes, vmem_limit_bytes=VMEM_LIMIT)


def _full(a):
    return pl.BlockSpec(a.shape, lambda *_: (0,) * a.ndim)


def _dot(a, b):
    return jnp.dot(a, b, precision=HI, preferred_element_type=jnp.float32)


def _split(x):
    hi = x.astype(jnp.bfloat16)
    lo = (x - hi.astype(jnp.float32)).astype(jnp.bfloat16)
    return hi, lo


def _mm3(a, b, dims=NN):
    d = lambda p, q: lax.dot_general(p, q, (dims, ((), ())), preferred_element_type=jnp.float32)
    return d(a[0], b[0]) + (d(a[0], b[1]) + d(a[1], b[0]))


def _mm1(a, b, dims=NN):
    return lax.dot_general(a, b, (dims, ((), ())), preferred_element_type=jnp.float32)


def _mm2(x, b):
    hi, lo = _split(x)
    d = functools.partial(jnp.dot, preferred_element_type=jnp.float32)
    return d(hi, b) + d(lo, b)


def _ln_rows(x):
    mu = jnp.mean(x, axis=-1, keepdims=True)
    xc = x - mu
    var = jnp.mean(xc * xc, axis=-1, keepdims=True)
    return xc * lax.rsqrt(var + LN_EPS)


def _sigmoid(x):
    return 1.0 / (1.0 + jnp.exp(-x))


def _ctx_rows(tm, n_ctx):
    return (pl.program_id(0) * tm + lax.broadcasted_iota(jnp.int32, (tm, 1), 0)) < n_ctx


def _mod_row(ref, is_ctx):
    return jnp.where(is_ctx, ref[1:2, :], ref[0:1, :])


def _block_indicator(width, head, value):
    i = jnp.arange(width) // head
    return jnp.where(i[:, None] == i[None, :], value, 0.0).astype(jnp.bfloat16)


def _ada_kernel(c_ref, w_ref, b_ref, o_ref):
    c = c_ref[...]
    h = c * _sigmoid(c)
    o_ref[...] = jnp.dot(h.astype(jnp.bfloat16), w_ref[...].astype(jnp.bfloat16),
                         preferred_element_type=jnp.float32) + b_ref[...]


def ada_modulation(c8, w, b):
    n = w.shape[1]
    tn = 1024
    return pl.pallas_call(
        _ada_kernel,
        grid=(n // tn,),
        in_specs=[pl.BlockSpec((8, D_MODEL), lambda j: (0, 0)),
                  pl.BlockSpec((D_MODEL, tn), lambda j: (0, j)),
                  pl.BlockSpec((1, tn), lambda j: (0, j))],
        out_specs=pl.BlockSpec((8, tn), lambda j: (0, j)),
        out_shape=jax.ShapeDtypeStruct((8, n), jnp.float32),
        compiler_params=_params(1),
    )(c8, w, b)


def _proj_kernel(x_ref, sh_ref, sc_ref, w_ref, o_ref, h_ref, *, n_ctx):
    @pl.when(pl.program_id(1) == 0)
    def _():
        is_ctx = _ctx_rows(x_ref.shape[0], n_ctx)
        h = _ln_rows(x_ref[...]) * (1.0 + _mod_row(sc_ref, is_ctx)) + _mod_row(sh_ref, is_ctx)
        h_ref[...] = h.astype(jnp.bfloat16)

    o_ref[...] = jnp.dot(h_ref[...], w_ref[...], preferred_element_type=jnp.float32)


def modulated_projection(x, mod, w_bf16, n_ctx):
    m = x.shape[0]
    n = w_bf16.shape[1]
    tm = PROJ_TM
    return pl.pallas_call(
        functools.partial(_proj_kernel, n_ctx=n_ctx),
        grid=(m // tm, n // CB),
        in_specs=[pl.BlockSpec((tm, D_MODEL), lambda i, j: (i, 0)),
                  pl.BlockSpec((8, D_MODEL), lambda i, j: (0, 0)),
                  pl.BlockSpec((8, D_MODEL), lambda i, j: (0, 1)),
                  pl.BlockSpec((D_MODEL, CB), lambda i, j: (0, j))],
        out_specs=pl.BlockSpec((tm, CB), lambda i, j: (i, j)),
        out_shape=jax.ShapeDtypeStruct((m, n), jnp.float32),
        scratch_shapes=[pltpu.VMEM((tm, D_MODEL), jnp.bfloat16)],
        compiler_params=_params(2),
    )(x, mod, mod, w_bf16)


def _halo_specs(tp, n_rows, col):
    per = tp // HALO
    last = n_rows // HALO - 1
    main = pl.BlockSpec((tp, CB), lambda i, *_: (i, col(*_) if callable(col) else col))
    prev = pl.BlockSpec((HALO, CB), lambda i, *_: (jnp.maximum(i * per - 1, 0), col(*_) if callable(col) else col))
    nxt = pl.BlockSpec((HALO, CB), lambda i, *_: (jnp.minimum((i + 1) * per, last), col(*_) if callable(col) else col))
    return [main, prev, nxt]


def _neighbours(x, prev, nxt, n_ctx, n_rows):
    tp = x.shape[0]
    loc = lax.broadcasted_iota(jnp.int32, (tp, 1), 0)
    row = pl.program_id(0) * tp + loc
    xp = jnp.where(loc == 0, prev[HALO - 1:HALO, :], pltpu.roll(x, 1, axis=0))
    xp = jnp.where((row == 0) | (row == n_ctx), 0.0, xp)
    xn = jnp.where(loc == tp - 1, nxt[0:1, :], pltpu.roll(x, tp - 1, axis=0))
    xn = jnp.where((row == n_ctx - 1) | (row == n_rows - 1), 0.0, xn)
    return xp, xn


def _conv3_kernel(z_ref, zp_ref, zn_ref, w_ref, b_ref, o_ref, *, n_ctx, n_rows):
    x = z_ref[...]
    xp, xn = _neighbours(x, zp_ref[...], zn_ref[...], n_ctx, n_rows)
    w = w_ref[...]
    o_ref[...] = xp * w[0:1, :] + x * w[1:2, :] + xn * w[2:3, :] + b_ref[...]


def hyena_conv3(z, w, b, n_ctx):
    n_rows = z.shape[0]
    tp = PREP_ROWS
    ncol = w.shape[1] // CB
    return pl.pallas_call(
        functools.partial(_conv3_kernel, n_ctx=n_ctx, n_rows=n_rows),
        grid=(n_rows // tp, ncol),
        in_specs=_halo_specs(tp, n_rows, lambda c: COL_HY + c)
        + [pl.BlockSpec((3, CB), lambda i, c: (0, c)), pl.BlockSpec((1, CB), lambda i, c: (0, c))],
        out_specs=pl.BlockSpec((tp, CB), lambda i, c: (i, c)),
        out_shape=jax.ShapeDtypeStruct((n_rows, w.shape[1]), jnp.float32),
        compiler_params=_params(2),
    )(z, z, z, w, b)


HY_EMB_PAD = 40
HY_FILTER_ROWS = 512


def hyena_features(L):
    t = jnp.linspace(0.0, 1.0, L, dtype=jnp.float32)[:, None]
    n_bands = (HY_EMB - 1) // 2
    f = jnp.linspace(1e-4, n_bands - 1, n_bands, dtype=jnp.float32)[None, :]
    ang = (2.0 * math.pi / L) * jnp.arange(L, dtype=jnp.float32)[:, None] * f
    z = jnp.concatenate([t, jnp.cos(ang), -jnp.sin(ang)], -1)
    return jnp.pad(z, ((0, 0), (0, HY_EMB_PAD - HY_EMB)))


def _filter_kernel(z_ref, w1_ref, b1_ref, w2_ref, b2_ref, w3_ref, b3_ref, fr_ref, dl_ref, h_ref, s_ref):
    i = pl.program_id(0)
    z = z_ref[...]
    fr = fr_ref[...]
    h = jnp.sin(fr * (_dot(z, w1_ref[...]) + b1_ref[...]))
    h = jnp.sin(fr * (_dot(h, w2_ref[...]) + b2_ref[...]))
    h = _dot(h, w3_ref[...]) + b3_ref[...]
    win = jnp.exp(-z[:, 0:1] * dl_ref[...])
    h = h * jnp.concatenate([win] * (h.shape[1] // win.shape[1]), axis=1)

    @pl.when(i == 0)
    def _():
        s_ref[...] = jnp.zeros_like(s_ref)

    s_ref[...] += jnp.sum(jnp.abs(h), axis=0, keepdims=True)
    row = lax.broadcasted_iota(jnp.int32, h.shape, 0) + i * h.shape[0]
    col = lax.broadcasted_iota(jnp.int32, h.shape, 1)
    neg = (col // GROUP_W) % 2 == 1
    h_ref[...] = jnp.where(neg & (row == 0), 0.0, h)


def hyena_filter_bank(L, w1, b1, w2, b2, w3, b3, freq):
    z = hyena_features(L)
    w1p = jnp.pad(w1, ((0, HY_EMB_PAD - HY_EMB), (0, 0)))
    max_decay = math.log(HY_TARGET) / HY_FAST_DECAY
    min_decay = math.log(HY_TARGET) / HY_SLOW_DECAY
    deltas = jnp.abs(jnp.linspace(min_decay, max_decay, GROUP_W, dtype=jnp.float32))[None, :]
    n = w3.shape[1]
    tr = min(L, HY_FILTER_ROWS)
    args = (z, w1p, b1[None, :], w2, b2[None, :], w3, b3[None, :], freq[None, :], deltas)
    return pl.pallas_call(
        _filter_kernel,
        grid=(L // tr,),
        in_specs=[pl.BlockSpec((tr, HY_EMB_PAD), lambda i: (i, 0))] + [_full(a) for a in args[1:]],
        out_specs=[pl.BlockSpec((tr, n), lambda i: (i, 0)), pl.BlockSpec((1, n), lambda i: (0, 0))],
        out_shape=[jax.ShapeDtypeStruct((L, n), jnp.float32), jax.ShapeDtypeStruct((1, n), jnp.float32)],
        compiler_params=_params(1),
    )(*args)


FFT_NB = 8
FFT_CB = 256
FFT_N1 = 128


def fft_tables(n1, n2):
    N = n1 * n2
    a = jnp.arange(n1, dtype=jnp.float32)[:, None, None]
    k2 = jnp.arange(n2, dtype=jnp.float32)[None, :, None]
    b = jnp.arange(n2 // 2, dtype=jnp.float32)[None, None, :]
    ph = (jnp.mod(a * k2, float(N)) / N + jnp.mod(b * k2, float(n2)) / n2) * (-2.0 * math.pi)
    g = jnp.concatenate([jnp.cos(ph), jnp.sin(ph)], axis=1)
    ginv = jnp.transpose(g, (0, 2, 1)) / N
    k1 = jnp.arange(n1, dtype=jnp.float32)[:, None]
    aa = jnp.arange(n1, dtype=jnp.float32)[None, :]
    f = jnp.mod(k1 * aa, float(n1)) * (-2.0 * math.pi / n1)
    fr, fi = jnp.cos(f), jnp.sin(f)
    ff = jnp.concatenate([jnp.concatenate([fr, -fi], axis=1), jnp.concatenate([fi, fr], axis=1)], axis=0)
    return {"g": _split(g), "ginv": _split(ginv), "ff": _split(ff), "fft": _split(ff.T)}


def _stage_a_kernel(u_ref, gh_ref, gl_ref, o_ref):
    for j in range(FFT_NB):
        o_ref[j] = _mm3((gh_ref[j], gl_ref[j]), _split(u_ref[:, j, :]))


def fft_stage_a(u3, col, ncol, g):
    nb, n1, _ = u3.shape
    gspec = pl.BlockSpec((FFT_NB, 4 * nb, nb), lambda c, i: (i, 0, 0))
    return pl.pallas_call(
        _stage_a_kernel,
        grid=(ncol, n1 // FFT_NB),
        in_specs=[pl.BlockSpec((nb, FFT_NB, CB), lambda c, i: (0, i, col + c)), gspec, gspec],
        out_specs=pl.BlockSpec((FFT_NB, 4 * nb, CB), lambda c, i: (i, 0, c)),
        out_shape=jax.ShapeDtypeStruct((n1, 4 * nb, ncol * CB), jnp.float32),
        compiler_params=_params(2),
    )(u3, g[0], g[1])


def _stage_b_filter_kernel(re_ref, im_ref, fh_ref, fl_ref, o_ref):
    ff = (fh_ref[...], fl_ref[...])
    for j in range(FFT_NB):
        o_ref[j] = _mm3(ff, _split(jnp.concatenate([re_ref[:, j, :], im_ref[:, j, :]], axis=0)))


def fft_stage_b_filter(t1, ff):
    n1, n2x2, C = t1.shape
    n2 = n2x2 // 2
    blk = lambda off: pl.BlockSpec((n1, FFT_NB, FFT_CB), lambda c, i: (0, i + off, c))
    mat = pl.BlockSpec((2 * n1, 2 * n1), lambda c, i: (0, 0))
    return pl.pallas_call(
        _stage_b_filter_kernel,
        grid=(C // FFT_CB, n2 // FFT_NB),
        in_specs=[blk(0), blk(n2 // FFT_NB), mat, mat],
        out_specs=pl.BlockSpec((FFT_NB, 2 * n1, FFT_CB), lambda c, i: (i, 0, c)),
        out_shape=jax.ShapeDtypeStruct((n2, 2 * n1, C), jnp.float32),
        compiler_params=_params(2),
    )(t1, t1, ff[0], ff[1])


def _filter_spectrum(p, q, s, n1):
    return (p[:n1] + q[:n1]) * s, (p[n1:] - q[n1:]) * s


def _stage_b_conv_kernel(re_ref, im_ref, p_ref, q_ref, s_ref, fh_ref, fl_ref, fth_ref, ftl_ref, ore_ref, oim_ref):
    n1 = re_ref.shape[0]
    s = s_ref[...]
    ff, fft_ = (fh_ref[...], fl_ref[...]), (fth_ref[...], ftl_ref[...])
    for j in range(FFT_NB):
        x = _mm3(ff, _split(jnp.concatenate([re_ref[:, j, :], im_ref[:, j, :]], axis=0)))
        hr, hi = _filter_spectrum(p_ref[j], q_ref[j], s, n1)
        xr, xi = x[:n1], x[n1:]
        z = _mm3(fft_, _split(jnp.concatenate([xr * hr - xi * hi, xr * hi + xi * hr], axis=0)))
        ore_ref[:, j, :] = z[:n1]
        oim_ref[:, j, :] = z[n1:]


def fft_stage_b_conv(t1, spec, col_p, col_q, inv_norm, ff, fft_):
    n1, n2x2, C = t1.shape
    n2 = n2x2 // 2
    ncb = C // FFT_CB
    blk = lambda off: pl.BlockSpec((n1, FFT_NB, FFT_CB), lambda c, i: (0, i + off, c))
    mat = pl.BlockSpec((2 * n1, 2 * n1), lambda c, i: (0, 0))
    sp = lambda col: pl.BlockSpec((FFT_NB, 2 * n1, FFT_CB), lambda c, i: (i, 0, col * ncb + c))
    return pl.pallas_call(
        _stage_b_conv_kernel,
        grid=(ncb, n2 // FFT_NB),
        in_specs=[blk(0), blk(n2 // FFT_NB), sp(col_p), sp(col_q), pl.BlockSpec((1, FFT_CB), lambda c, i: (0, c)),
                  mat, mat, mat, mat],
        out_specs=[blk(0), blk(0)],
        out_shape=[jax.ShapeDtypeStruct((n1, n2, C), jnp.float32)] * 2,
        compiler_params=_params(2),
    )(t1, t1, spec, spec, inv_norm, ff[0], ff[1], fft_[0], fft_[1])


def _stage_a_inv_kernel(re_ref, im_ref, gh_ref, gl_ref, u_ref, gate_ref, bias_ref, o_ref):
    for j in range(FFT_NB):
        y = _mm3((gh_ref[j], gl_ref[j]), _split(jnp.concatenate([re_ref[j], im_ref[j]], axis=0)))
        o_ref[:, j, :] = gate_ref[:, j, :] * (y + u_ref[:, j, :] * bias_ref[...])


def fft_stage_a_inv(t2re, t2im, ginv, u3, u_col, gate3, gate_col, bias):
    n1, n2, C = t2re.shape
    nb = n2 // 2
    tb = pl.BlockSpec((FFT_NB, n2, CB), lambda c, i: (i, 0, c))
    gb = pl.BlockSpec((FFT_NB, nb, 2 * n2), lambda c, i: (i, 0, 0))
    ub = lambda col: pl.BlockSpec((nb, FFT_NB, CB), lambda c, i: (0, i, col + c))
    return pl.pallas_call(
        _stage_a_inv_kernel,
        grid=(C // CB, n1 // FFT_NB),
        in_specs=[tb, tb, gb, gb, ub(u_col), ub(gate_col), pl.BlockSpec((1, CB), lambda c, i: (0, c))],
        out_specs=ub(0),
        out_shape=jax.ShapeDtypeStruct((nb, n1, C), jnp.float32),
        compiler_params=_params(2),
    )(t2re, t2im, ginv[0], ginv[1], u3, gate3, bias)


def hyena_long_conv_chain(u, biases, filt, colsum, tabs):
    L = u.shape[0]
    C = GROUP_W
    n1 = FFT_N1
    n2 = 2 * L // n1
    spec = fft_stage_b_filter(fft_stage_a(filt.reshape(n2 // 2, n1, filt.shape[1]), 0, filt.shape[1] // CB, tabs["g"]),
                              tabs["ff"])
    s4 = colsum.reshape(HY_ORDER, 2, C)
    inv_norm = 1.0 / (s4[:, 0] + s4[:, 1])
    u3 = u.reshape(n2 // 2, n1, u.shape[1])
    y3, y_col = u3, 0
    for n in range(HY_ORDER):
        t1 = fft_stage_a(y3, y_col, 1, tabs["g"])
        t2re, t2im = fft_stage_b_conv(t1, spec, 2 * n, 2 * n + 1, inv_norm[n][None, :], tabs["ff"], tabs["fft"])
        y3 = fft_stage_a_inv(t2re, t2im, tabs["ginv"], y3, y_col, u3, n + 1, biases[n][None, :])
        y_col = 0
    return y3.reshape(L, C)


def dense_dft_tables(n):
    N = 2 * n
    k = jnp.arange(N, dtype=jnp.float32)[:, None]
    t = jnp.arange(n, dtype=jnp.float32)[None, :]
    ph = jnp.mod(k * t, float(N)) * (2.0 * math.pi / N)
    fd = jnp.concatenate([jnp.cos(ph), -jnp.sin(ph)], axis=0)
    return _split(fd), _split(fd.T / N)


def _hyena_ctx_kernel(u_ref, h_ref, s_ref, bias_ref, fdh_ref, fdl_ref, fth_ref, ftl_ref, o_ref):
    C = GROUP_W
    fd, ft = (fdh_ref[...], fdl_ref[...]), (fth_ref[...], ftl_ref[...])
    K = fd[0].shape[0] // 2
    hs = _mm3(fd, _split(h_ref[...]))
    s = s_ref[...]
    y = u_ref[:, 0:C]
    for n in range(HY_ORDER):
        cp, cq = 2 * n * C, (2 * n + 1) * C
        inv = 1.0 / (s[:, cp:cp + C] + s[:, cq:cq + C])
        hr, hi = _filter_spectrum(hs[:, cp:cp + C], hs[:, cq:cq + C], inv, K)
        x = _mm3(fd, _split(y))
        xr, xi = x[:K], x[K:]
        conv = _mm3(ft, _split(jnp.concatenate([xr * hr - xi * hi, xr * hi + xi * hr], axis=0)))
        y = u_ref[:, (n + 1) * C:(n + 2) * C] * (conv + y * bias_ref[n:n + 1, :])
    o_ref[...] = y


def hyena_ctx(u, filt, colsum, biases, tabs):
    args = (u, filt, colsum, biases, tabs[0][0], tabs[0][1], tabs[1][0], tabs[1][1])
    return pl.pallas_call(
        _hyena_ctx_kernel,
        in_specs=[_full(a) for a in args],
        out_specs=pl.BlockSpec((u.shape[0], GROUP_W), lambda: (0, 0)),
        out_shape=jax.ShapeDtypeStruct((u.shape[0], GROUP_W), jnp.float32),
        compiler_params=pltpu.CompilerParams(vmem_limit_bytes=VMEM_LIMIT),
    )(*args)


SCAN_T = 64
GROUP_LANES = 256
RW_SUB = 16
HG_SUB = 16


def _chunk_index(d, i, n_ctx, n_all):
    bwd = jnp.where(i < n_ctx, n_ctx - 1 - i, n_all + n_ctx - 1 - i)
    return jnp.where(d == 0, i, bwd)


def _stacking(T, S, nh, head, sign):
    G = nh * head
    nb = T // S
    n = nh * T
    rr = lax.broadcasted_iota(jnp.int32, (n, G), 0)
    same = ((rr // S) % nh) == (lax.broadcasted_iota(jnp.int32, (n, G), 1) // head)

    def bd(x):
        pieces = []
        for i in range(nb):
            pieces += [x[i * S:(i + 1) * S]] * nh
        return jnp.where(same, jnp.concatenate(pieces, axis=0), 0.0)

    def collapse(o):
        outs = []
        for i in range(nb):
            acc = o[i * nh * S:i * nh * S + S]
            for h in range(1, nh):
                acc = acc + o[i * nh * S + h * S:i * nh * S + (h + 1) * S]
            outs.append(acc)
        return jnp.concatenate(outs, axis=0)

    rt = lax.broadcasted_iota(jnp.int32, (n, n), 0)
    cs = lax.broadcasted_iota(jnp.int32, (n, n), 1)
    t_r = (rt // (nh * S)) * S + rt % S
    t_c = (cs // (nh * S)) * S + cs % S
    same_h = ((rt // S) % nh) == ((cs // S) % nh)
    before = same_h & ((t_r - t_c) * sign > 0)
    return bd, collapse, before, rt, cs


def _softplus(x):
    return jnp.maximum(x, 0.0) + jnp.log(1.0 + jnp.exp(-jnp.abs(x)))


def _rwkv_prep_kernel(*refs, n_ctx, n_rows):
    zrefs, rest = refs[:15], refs[15:]
    (mu_ref, kk_w_ref, ka_ref, w0_ref, a0_ref, w2h_ref, w2l_ref, a2h_ref, a2l_ref, ones_ref,
     r_ref, k_ref, v_ref, g_ref, kk_ref, lw_ref, kd_ref, a_ref) = rest
    slabs = []
    for c in range(5):
        x = zrefs[3 * c][...]
        xp, xn = _neighbours(x, zrefs[3 * c + 1][...], zrefs[3 * c + 2][...], n_ctx, n_rows)
        slabs.append(x + (0.5 * (xp + xn) - x) * mu_ref[:, c * CB:(c + 1) * CB])
    r, k, v, g, lora = slabs
    r_ref[...], k_ref[...], v_ref[...], g_ref[...] = r, k, v, g
    kk = k * kk_w_ref[...]
    ss = _mm2(kk * kk, ones_ref[...])
    kk_ref[...] = kk * lax.rsqrt(jnp.maximum(ss, 1e-24))
    lora_t = _split(jnp.tanh(lora))
    lora_s = _split(lora)
    for d in range(2):
        w_log = -_softplus(-(w0_ref[d:d + 1, :] + _mm3(lora_t, (w2h_ref[d], w2l_ref[d])))) - 0.5
        lw_ref[d] = -jnp.exp(w_log)
        a = _sigmoid(a0_ref[d:d + 1, :] + _mm3(lora_s, (a2h_ref[d], a2l_ref[d])))
        a_ref[d] = a
        kd_ref[d] = k * (1.0 + (a - 1.0) * ka_ref[...])


def rwkv_prep(z, mu, k_k, k_a, w0, a0, w2, a2, n_ctx):
    n_rows = z.shape[0]
    tp = PREP_ROWS
    mu_p = jnp.pad(mu, (0, 5 * CB - RW_REAL))[None, :]
    w2p = jnp.zeros((2, CB, GROUP_W), jnp.float32)
    a2p = jnp.zeros((2, CB, GROUP_W), jnp.float32)
    for d in range(2):
        w2p = w2p.at[d, d * RW_LORA:(d + 1) * RW_LORA].set(w2[d])
        a2p = a2p.at[d, (2 + d) * RW_LORA:(3 + d) * RW_LORA].set(a2[d])
    w2s, a2s = _split(w2p), _split(a2p)
    small = (mu_p, k_k[None, :], k_a[None, :], w0, a0, w2s[0], w2s[1], a2s[0], a2s[1],
             _block_indicator(GROUP_W, RW_HEAD, 1.0))
    zspecs = []
    for c in range(5):
        zspecs += _halo_specs(tp, n_rows, COL_RW + c)
    one = pl.BlockSpec((tp, CB), lambda i: (i, 0))
    two = pl.BlockSpec((2, tp, CB), lambda i: (0, i, 0))
    s1 = jax.ShapeDtypeStruct((n_rows, GROUP_W), jnp.float32)
    s2 = jax.ShapeDtypeStruct((2, n_rows, GROUP_W), jnp.float32)
    return pl.pallas_call(
        functools.partial(_rwkv_prep_kernel, n_ctx=n_ctx, n_rows=n_rows),
        grid=(n_rows // tp,),
        in_specs=zspecs + [_full(a) for a in small],
        out_specs=[one] * 5 + [two] * 3,
        out_shape=[s1] * 5 + [s2] * 3,
        compiler_params=_params(1),
    )(*([z] * 15), *small)


def _rwkv_scan_kernel(r_ref, v_ref, kk_ref, lw_ref, k_ref, a_ref, o_ref, ht_ref, *, head):
    T, G, S = SCAN_T, GROUP_LANES, RW_SUB
    nh = G // head
    nb = T // S
    n = nh * T
    d = pl.program_id(0)
    sign = 1 - 2 * d

    @pl.when(pl.program_id(1) == 0)
    def _():
        ht_ref[...] = jnp.zeros_like(ht_ref)

    ti = lax.broadcasted_iota(jnp.int32, (T, T), 0)
    si = lax.broadcasted_iota(jnp.int32, (T, T), 1)
    tri_incl = jnp.where((ti - si) * sign >= 0, 1.0, 0.0)
    bd, collapse, before, rt, cs = _stacking(T, S, nh, head, sign)
    incl = before | (rt == cs)
    diag_blk = (rt // (nh * S)) == (cs // (nh * S))
    eye = jnp.where(rt == cs, 1.0, 0.0)
    bf = lambda t: t.astype(jnp.bfloat16)

    groups = range(r_ref.shape[1] // G)
    states = [ht_ref[grp] for grp in groups]
    results = []
    for grp in groups:
        ls = slice(grp * G, (grp + 1) * G)
        r, v, kk = r_ref[:, ls], v_ref[:, ls], kk_ref[:, ls]
        lw, k, a = lw_ref[0, :, ls], k_ref[0, :, ls], a_ref[0, :, ls]
        c = _dot(tri_incl, lw)
        ctot = jnp.sum(lw, axis=0, keepdims=True)
        beta = kk * a
        einv = jnp.exp(-c)
        efin = jnp.exp(ctot - c)
        a_bar = -kk * jnp.exp(c - lw)
        r_bar = r * jnp.exp(c)

        lhs = _split(jnp.concatenate([bd(a_bar), bd(r_bar)], axis=0))
        rhs = _split(jnp.concatenate([bd(k * einv), bd(beta * einv)], axis=0))
        m = _mm3(lhs, rhs, NT)
        ak, ab, rk, rb = m[:n, :n], m[:n, n:], m[n:, :n], m[n:, n:]
        ht = states[grp]
        g = _mm1(lhs[0], bf(ht), NT)
        v_b = bf(bd(v))
        x = g[:n] + _mm1(bf(jnp.where(before, ak, 0.0)), v_b)
        lb = jnp.where(before, ab, 0.0)
        ld = jnp.where(diag_blk, lb, 0.0)
        lo = lb - ld
        xd = eye + ld
        lp = _split(ld)
        p = 2
        while p < S:
            lp = _split(_mm3(lp, lp))
            xd = xd + _mm3(lp, _split(xd))
            p *= 2
        xd_s = _split(xd)
        n_s = _split(_mm3(xd_s, _split(lo)))
        y = _mm3(xd_s, _split(x))
        y = y + _mm3(n_s, _split(y))
        p = 2
        while p < nb:
            n_s = _split(_mm3(n_s, n_s))
            y = y + _mm3(n_s, _split(y))
            p *= 2
        u_b = bf(y)
        o = (g[n:] + _mm1(bf(jnp.where(incl, rk, 0.0)), v_b)) + _mm1(bf(jnp.where(incl, rb, 0.0)), u_b)
        ht_new = (ht * jnp.exp(ctot) + _mm1(v_b, bf(bd(k * efin)), TN)) + _mm1(u_b, bf(bd(beta * efin)), TN)
        results.append((ls, collapse(o), ht_new))
    for grp, (ls, o, ht_new) in enumerate(results):
        o_ref[0, :, ls] = o
        ht_ref[grp] = ht_new


def rwkv_scan(r, v, kk, lw2, k2, a2, n_ctx_rows):
    N, W = r.shape
    T = SCAN_T
    n_all, n_ctx = N // T, n_ctx_rows // T
    shared = pl.BlockSpec((T, W), lambda d, i: (_chunk_index(d, i, n_ctx, n_all), 0))
    per_dir = pl.BlockSpec((1, T, W), lambda d, i: (d, _chunk_index(d, i, n_ctx, n_all), 0))
    return pl.pallas_call(
        functools.partial(_rwkv_scan_kernel, head=RW_HEAD),
        grid=(2, n_all),
        in_specs=[shared, shared, shared, per_dir, per_dir, per_dir],
        out_specs=per_dir,
        out_shape=jax.ShapeDtypeStruct((2, N, W), jnp.float32),
        scratch_shapes=[pltpu.VMEM((W // GROUP_LANES, GROUP_LANES, GROUP_LANES), jnp.float32)],
        compiler_params=_params(2),
    )(r, v, kk, lw2, k2, a2)


def rope_tables(n_ctx, L):
    d_axis = RT_HEAD // 2
    half = d_axis // 2
    inv = ROPE_BASE ** (-jnp.arange(0, d_axis, 2, dtype=jnp.float32) / d_axis)
    t = jnp.arange(L)
    pos = jnp.stack([(t // GRID_W).astype(jnp.float32), (t % GRID_W).astype(jnp.float32)], axis=1)
    j = jnp.arange(RT_HEAD)
    ang = pos[:, j // d_axis] * inv[j % half][None, :]
    sgn = jnp.where((j % d_axis) < half, -1.0, 1.0)[None, :]
    cos = jnp.concatenate([jnp.ones((n_ctx, RT_HEAD), jnp.float32), jnp.cos(ang)], axis=0)
    sin = jnp.concatenate([jnp.zeros((n_ctx, RT_HEAD), jnp.float32), jnp.sin(ang) * sgn], axis=0)
    return jnp.tile(cos, (1, RT_HEADS)), jnp.tile(sin, (1, RT_HEADS))


def _rotate(x, cos, sin):
    G = x.shape[1]
    half = RT_HEAD // 4
    lane = lax.broadcasted_iota(jnp.int32, x.shape, 1)
    partner = jnp.where((lane % (2 * half)) < half, pltpu.roll(x, G - half, axis=1), pltpu.roll(x, half, axis=1))
    return x * cos + partner * sin


def _retention_scan_kernel(q_ref, k_ref, v_ref, cos_ref, sin_ref, lg_ref, o_ref, st_ref, *, head):
    T, G = SCAN_T, GROUP_LANES
    nh = G // head
    d = pl.program_id(0)
    sign = 1 - 2 * d

    @pl.when(pl.program_id(1) == 0)
    def _():
        st_ref[...] = jnp.zeros_like(st_ref)

    t = lax.broadcasted_iota(jnp.int32, (T, 1), 0)
    pos = (t + d * (T - 1 - 2 * t) + 1).astype(jnp.float32)
    bd, collapse, before, rt, cs = _stacking(T, T, nh, head, sign)
    incl = before | (rt == cs)
    bf = lambda x: x.astype(jnp.bfloat16)

    groups = range(q_ref.shape[1] // G)
    states = [st_ref[grp] for grp in groups]
    results = []
    for grp in groups:
        ls = slice(grp * G, (grp + 1) * G)
        cos, sin, lg = cos_ref[:, ls], sin_ref[:, ls], lg_ref[:, ls]
        q = _rotate(q_ref[:, ls], cos, sin)
        k = _rotate(k_ref[:, ls], cos, sin) * (head ** -0.5)
        c = pos * lg
        ctot = float(T) * lg
        q_b = bf(bd(q * jnp.exp(c)))
        k_til = k * jnp.exp(-c)
        k_hat = k * jnp.exp(ctot - c)
        v_b = bf(bd(v_ref[:, ls]))
        st = states[grp]
        scores = jnp.where(incl, _mm1(q_b, bf(bd(k_til)), NT), 0.0)
        results.append((ls, collapse(_mm1(q_b, bf(st), NT) + _mm1(bf(scores), v_b)),
                        st * jnp.exp(ctot) + _mm1(v_b, bf(bd(k_hat)), TN)))
    for grp, (ls, o, st_new) in enumerate(results):
        o_ref[0, :, ls] = o
        st_ref[grp] = st_new


def retention_scan(z, cos, sin, lg, n_ctx_rows):
    N = z.shape[0]
    T = SCAN_T
    n_all, n_ctx = N // T, n_ctx_rows // T
    zc = lambda col: pl.BlockSpec((T, CB), lambda d, i: (_chunk_index(d, i, n_ctx, n_all), COL_RT + col))
    tab = pl.BlockSpec((T, GROUP_W), lambda d, i: (_chunk_index(d, i, n_ctx, n_all), 0))
    return pl.pallas_call(
        functools.partial(_retention_scan_kernel, head=RT_HEAD),
        grid=(2, n_all),
        in_specs=[zc(0), zc(1), zc(2), tab, tab, pl.BlockSpec((1, GROUP_W), lambda d, i: (0, 0))],
        out_specs=pl.BlockSpec((1, T, GROUP_W), lambda d, i: (d, _chunk_index(d, i, n_ctx, n_all), 0)),
        out_shape=jax.ShapeDtypeStruct((2, N, GROUP_W), jnp.float32),
        scratch_shapes=[pltpu.VMEM((GROUP_W // GROUP_LANES, GROUP_LANES, GROUP_LANES), jnp.float32)],
        compiler_params=_params(2),
    )(z, z, z, cos, sin, lg)


def _gla_scan_kernel(q_ref, f_ref, i_ref, lb_ref, o_ref, st_ref, *, head):
    T, S = SCAN_T, HG_SUB
    W = q_ref.shape[-1]
    nh = W // head
    d = pl.program_id(0)
    sign = 1 - 2 * d

    @pl.when(pl.program_id(1) == 0)
    def _():
        st_ref[...] = jnp.zeros_like(st_ref)

    ti = lax.broadcasted_iota(jnp.int32, (S, S), 0)
    si = lax.broadcasted_iota(jnp.int32, (S, S), 1)
    tri_incl = jnp.where((ti - si) * sign >= 0, 1.0, 0.0)
    row = lax.broadcasted_iota(jnp.int32, (S, 1), 0)
    lb = lb_ref[0]

    states = [st_ref[h] for h in range(nh)]
    for j in range(T // S):
        jb = j + d * (T // S - 1 - 2 * j)
        rows = pl.ds(pl.multiple_of(jb * S, S), S)
        q = q_ref[rows, :]
        q = q * _sigmoid(q)
        v = i_ref[rows, :]
        gate = lb + (1.0 - lb) * _sigmoid(f_ref[rows, :])
        lf = jnp.log(jnp.maximum(gate, HG_MIN_GATE))
        k = 1.0 - gate
        b = _dot(tri_incl, lf)
        btot = jnp.sum(lf, axis=0, keepdims=True)
        qe = q * jnp.exp(b)
        ke = k * jnp.exp(btot - b)
        outs = []
        for h in range(nh):
            ls = slice(h * head, (h + 1) * head)
            qh, kh, vh, bh = q[:, ls], k[:, ls], v[:, ls], b[:, ls]
            st = states[h]
            o = _mm1(qe[:, ls].astype(jnp.bfloat16), st.astype(jnp.bfloat16), NT)
            for s in range(S):
                e = jnp.exp(jnp.minimum(bh - bh[s:s + 1, :], 0.0))
                a_s = jnp.sum(qh * kh[s:s + 1, :] * e, axis=-1, keepdims=True)
                a_s = jnp.where((row - s) * sign >= 0, a_s, 0.0)
                o = o + a_s * vh[s:s + 1, :]
            outs.append(o)
            states[h] = st * jnp.exp(btot[:, ls]) + _mm1(vh.astype(jnp.bfloat16), ke[:, ls].astype(jnp.bfloat16), TN)
        o_ref[0, rows, :] = jnp.concatenate(outs, axis=1)
    for h in range(nh):
        st_ref[h] = states[h]


def gla_scan(z, lb2, n_ctx_rows):
    N = z.shape[0]
    W = GROUP_W
    T = SCAN_T
    n_all, n_ctx = N // T, n_ctx_rows // T
    zc = lambda col: pl.BlockSpec((T, W), lambda d, i: (_chunk_index(d, i, n_ctx, n_all), COL_HG + col))
    zf = pl.BlockSpec((T, W), lambda d, i: (_chunk_index(d, i, n_ctx, n_all), COL_HG + 1 + d))
    return pl.pallas_call(
        functools.partial(_gla_scan_kernel, head=HG_HEAD),
        grid=(2, n_all),
        in_specs=[zc(0), zf, zc(3), pl.BlockSpec((1, 1, W), lambda d, i: (d, 0, 0))],
        out_specs=pl.BlockSpec((1, T, W), lambda d, i: (d, _chunk_index(d, i, n_ctx, n_all), 0)),
        out_shape=jax.ShapeDtypeStruct((2, N, W), jnp.float32),
        scratch_shapes=[pltpu.VMEM((W // HG_HEAD, HG_HEAD, HG_HEAD), jnp.float32)],
        compiler_params=_params(2),
    )(z, z, z, lb2)


def _outproj_kernel(hy_ref, rwo_ref, r_ref, k_ref, v_ref, rwg_ref, rto_ref, rtg_ref, hgo_ref, hgg_ref,
                    rk_ref, gng_ref, gnb_ref, hgn_ref, avg64_ref, avg128_ref,
                    w_ref, x_ref, gate_ref, g_ref, b_ref, o_ref, *, n_ctx):
    avg64, avg128 = avg64_ref[...], avg128_ref[...]

    def head_norm(o, avg, eps, centre):
        if centre:
            o = o - _mm2(o, avg)
        return o * lax.rsqrt(_mm2(o * o, avg) + eps)

    silu = lambda t: t * _sigmoid(t)
    y_rw = head_norm(rwo_ref[0] + rwo_ref[1], avg64, RW_GN_EPS, True) * gng_ref[...] + gnb_ref[...]
    bonus = (float(RW_HEAD) * _mm2(r_ref[...] * k_ref[...] * rk_ref[...], avg64)) * v_ref[...]
    y_rw = (y_rw + bonus) * _sigmoid(rwg_ref[...])
    y_rt = head_norm(rto_ref[0] + rto_ref[1], avg64, HEAD_NORM_EPS, True) * silu(rtg_ref[...])
    y_hg = head_norm(hgo_ref[0] + hgo_ref[1], avg128, HEAD_NORM_EPS, False) * hgn_ref[...] * silu(hgg_ref[...])
    y = None
    for m, ym in enumerate((hy_ref[...], y_rw, y_rt, y_hg)):
        part = jnp.dot(ym.astype(jnp.bfloat16), w_ref[m * GROUP_W:(m + 1) * GROUP_W, :],
                       preferred_element_type=jnp.float32)
        y = part if y is None else y + part
    is_ctx = _ctx_rows(x_ref.shape[0], n_ctx)
    r = ALPHA * x_ref[...] + _mod_row(gate_ref, is_ctx) * y
    o_ref[...] = _ln_rows(r) * g_ref[...] + b_ref[...]


def outproj_deepnorm(y_hy, rw_o2, rw_r, rw_k, rw_v, rw_g, rt_o2, hg_o2, z, r_k, gn_g, gn_b, hg_norm_g,
                     w_bf16, x, mod, g, b, n_ctx):
    m = x.shape[0]
    tm = OUT_TM
    one = pl.BlockSpec((tm, GROUP_W), lambda i: (i, 0))
    two = pl.BlockSpec((2, tm, GROUP_W), lambda i: (0, i, 0))
    zc = lambda col: pl.BlockSpec((tm, CB), lambda i: (i, col))
    row = pl.BlockSpec((tm, D_MODEL), lambda i: (i, 0))
    small = (r_k, gn_g, gn_b, hg_norm_g, _block_indicator(GROUP_W, RW_HEAD, 1.0 / RW_HEAD),
             _block_indicator(GROUP_W, HG_HEAD, 1.0 / HG_HEAD))
    return pl.pallas_call(
        functools.partial(_outproj_kernel, n_ctx=n_ctx),
        grid=(m // tm,),
        in_specs=[one, two, one, one, one, one, two, zc(COL_RT + 3), two, zc(COL_HG + 4)]
        + [_full(a) for a in small]
        + [_full(w_bf16), row, pl.BlockSpec((8, D_MODEL), lambda i: (0, 2)), _full(g), _full(b)],
        out_specs=row,
        out_shape=jax.ShapeDtypeStruct((m, D_MODEL), jnp.float32),
        compiler_params=_params(1),
    )(y_hy, rw_o2, rw_r, rw_k, rw_v, rw_g, rt_o2, z, hg_o2, z, *small, w_bf16, x, mod, g, b)


def _ffn_kernel(x_ref, sh_ref, sc_ref, w1_ref, w3_ref, w2_ref, gate_ref, g_ref, b_ref, o_ref, h_ref, acc_ref, *, n_ctx):
    j = pl.program_id(1)

    @pl.when(j == 0)
    def _():
        is_ctx = _ctx_rows(x_ref.shape[0], n_ctx)
        h = _ln_rows(x_ref[...]) * (1.0 + _mod_row(sc_ref, is_ctx)) + _mod_row(sh_ref, is_ctx)
        h_ref[...] = h.astype(jnp.bfloat16)
        acc_ref[...] = jnp.zeros_like(acc_ref)

    h = h_ref[...]
    a = jnp.dot(h, w1_ref[...], preferred_element_type=jnp.float32)
    u = jnp.dot(h, w3_ref[...], preferred_element_type=jnp.float32)
    s = (a * _sigmoid(a) * u).astype(jnp.bfloat16)
    acc_ref[...] += jnp.dot(s, w2_ref[...], preferred_element_type=jnp.float32)

    @pl.when(j == pl.num_programs(1) - 1)
    def _():
        is_ctx = _ctx_rows(x_ref.shape[0], n_ctx)
        r = ALPHA * x_ref[...] + _mod_row(gate_ref, is_ctx) * acc_ref[...]
        o_ref[...] = _ln_rows(r) * g_ref[...] + b_ref[...]


def ffn_deepnorm(x, mod, w1, w3, w2, g, b, n_ctx):
    m = x.shape[0]
    tm = PROJ_TM
    row = pl.BlockSpec((tm, D_MODEL), lambda i, j: (i, 0))
    modc = lambda c: pl.BlockSpec((8, D_MODEL), lambda i, j: (0, c))
    vec = pl.BlockSpec((1, D_MODEL), lambda i, j: (0, 0))
    return pl.pallas_call(
        functools.partial(_ffn_kernel, n_ctx=n_ctx),
        grid=(m // tm, FFN_HIDDEN // FFN_TF),
        in_specs=[row, modc(3), modc(4),
                  pl.BlockSpec((D_MODEL, FFN_TF), lambda i, j: (0, j)),
                  pl.BlockSpec((D_MODEL, FFN_TF), lambda i, j: (0, j)),
                  pl.BlockSpec((FFN_TF, D_MODEL), lambda i, j: (j, 0)),
                  modc(5), vec, vec],
        out_specs=row,
        out_shape=jax.ShapeDtypeStruct((m, D_MODEL), jnp.float32),
        scratch_shapes=[pltpu.VMEM((tm, D_MODEL), jnp.bfloat16),
                        pltpu.VMEM((tm, D_MODEL), jnp.float32)],
        compiler_params=_params(2),
    )(x, mod, mod, w1, w3, w2, mod, g, b)


def kernel(x, c, ctx, c_ctx, ada_w, ada_b, w_in, w_out, ln_g, ln_b, hy_conv_w, hy_conv_b, hy_w1, hy_b1, hy_w2, hy_b2, hy_w3, hy_b3, hy_freq, hy_bias, rw_mu, rw_w0, rw_w2, rw_a0, rw_a2, rw_k_k, rw_k_a, rw_r_k, rw_gn_g, rw_gn_b, hg_lb_raw, hg_norm_g, ffn_w1, ffn_w3, ffn_w2):
    L, n_ctx = x.shape[1], ctx.shape[1]
    sm = jax.nn.softmax(hg_lb_raw.astype(jnp.float32), axis=1)
    lower_bounds = jnp.cumsum(sm, axis=1) - sm[:, :1]
    log_gamma = jnp.log1p(-jnp.exp2(-5.0 - jnp.arange(RT_HEADS, dtype=jnp.float32)))
    lg = jnp.repeat(log_gamma, RT_HEAD)[None, :]
    cos, sin = rope_tables(n_ctx, L)
    fft_tabs = fft_tables(FFT_N1, 2 * L // FFT_N1)
    ctx_tabs = dense_dft_tables(n_ctx)

    c8 = jnp.zeros((8, D_MODEL), jnp.float32).at[0].set(c[0]).at[1].set(c_ctx)
    xs = jnp.concatenate([ctx[0], x[0]], axis=0)
    for l in range(DEPTH):
        with_ctx = l < DEPTH - 1
        mod = ada_modulation(c8, ada_w[l], ada_b[l][None, :])
        pad = jnp.zeros((D_MODEL, CB - 4 * RW_LORA), jnp.float32)
        split = (COL_RW * CB) + RW_REAL
        w_in_b = jnp.concatenate([w_in[l][:, :split], pad, w_in[l][:, split:]], axis=1).astype(jnp.bfloat16)
        w_out_b = w_out[l].astype(jnp.bfloat16)
        w1_b, w3_b, w2_b = (w[l].astype(jnp.bfloat16) for w in (ffn_w1, ffn_w3, ffn_w2))

        z = modulated_projection(xs, mod, w_in_b, n_ctx)

        u = hyena_conv3(z, hy_conv_w[l], hy_conv_b[l][None, :], n_ctx)
        hy_w = (hy_w1[l], hy_b1[l], hy_w2[l], hy_b2[l], hy_w3[l], hy_b3[l], hy_freq[l])
        filt, colsum = hyena_filter_bank(L, *hy_w)
        y_lat = hyena_long_conv_chain(u[n_ctx:], hy_bias[l], filt, colsum, fft_tabs)
        if with_ctx:
            filt_c, colsum_c = hyena_filter_bank(n_ctx, *hy_w)
            y_ctx = hyena_ctx(u[:n_ctx], filt_c, colsum_c, hy_bias[l], ctx_tabs)
        else:
            y_ctx = jnp.zeros((n_ctx, GROUP_W), jnp.float32)
        y_hy = jnp.concatenate([y_ctx, y_lat], axis=0)

        rw_r, rw_k, rw_v, rw_g, rw_kk, rw_lw, rw_kd, rw_a = rwkv_prep(
            z, rw_mu[l], rw_k_k[l], rw_k_a[l], rw_w0[l], rw_a0[l], rw_w2[l], rw_a2[l], n_ctx)
        rw_o2 = rwkv_scan(rw_r, rw_v, rw_kk, rw_lw, rw_kd, rw_a, n_ctx)
        rt_o2 = retention_scan(z, cos, sin, lg, n_ctx)
        hg_o2 = gla_scan(z, lower_bounds[:, l][:, None, :], n_ctx)

        xs = outproj_deepnorm(y_hy, rw_o2, rw_r, rw_k, rw_v, rw_g, rt_o2, hg_o2, z,
                              rw_r_k[l].reshape(1, GROUP_W), rw_gn_g[l][None, :], rw_gn_b[l][None, :],
                              hg_norm_g[l][None, :], w_out_b, xs, mod, ln_g[l, 0][None, :], ln_b[l, 0][None, :], n_ctx)
        xs = ffn_deepnorm(xs, mod, w1_b, w3_b, w2_b, ln_g[l, 1][None, :], ln_b[l, 1][None, :], n_ctx)
    return xs[n_ctx:][None]
```

```python
import functools
import math

import jax
import jax.numpy as jnp
from jax import lax
from jax.experimental import pallas as pl
from jax.experimental.pallas import tpu as pltpu

D_MODEL = 2048
DEPTH = 2
GRID_W = 64
N_MIXERS = 4
GROUP_W = D_MODEL // N_MIXERS
HY_ORDER = 2
HY_EMB = 33
HY_FAST_DECAY = 0.3
HY_SLOW_DECAY = 1.5
HY_TARGET = 1e-2
RW_HEAD = 64
RW_LORA = 96
RW_GN_EPS = 64e-5
RT_HEAD = 64
RT_HEADS = GROUP_W // RT_HEAD
ROPE_BASE = 10000.0
HG_HEAD = 128
HG_MIN_GATE = 1e-30
FFN_HIDDEN = 5632
ALPHA = (2 * DEPTH) ** 0.25
LN_EPS = 1e-6
HEAD_NORM_EPS = 1e-6

CB = 512
COL_HY = 0
COL_RW = 3
COL_RT = 8
COL_HG = 12
P_IN_PAD = 17 * CB
RW_REAL = 4 * GROUP_W + 4 * RW_LORA
PROJ_TM = 768
OUT_TM = 256
FFN_TF = 512
PREP_ROWS = 256
HALO = 8
VMEM_LIMIT = 56 * 1024 * 1024

HI = lax.Precision.HIGHEST
NN = ((1,), (0,))
NT = ((1,), (1,))
TN = ((0,), (0,))


def _params(n_axes):
    return pltpu.CompilerParams(dimension_semantics=("arbitrary",) * n_axes, vmem_limit_bytes=VMEM_LIMIT)


def _full(a):
    return pl.BlockSpec(a.shape, lambda *_: (0,) * a.ndim)


def _dot(a, b):
    return jnp.dot(a, b, precision=HI, preferred_element_type=jnp.float32)


def _split(x):
    hi = x.astype(jnp.bfloat16)
    lo = (x - hi.astype(jnp.float32)).astype(jnp.bfloat16)
    return hi, lo


def _mm3(a, b, dims=NN):
    d = lambda p, q: lax.dot_general(p, q, (dims, ((), ())), preferred_element_type=jnp.float32)
    return d(a[0], b[0]) + (d(a[0], b[1]) + d(a[1], b[0]))


def _mm1(a, b, dims=NN):
    return lax.dot_general(a, b, (dims, ((), ())), preferred_element_type=jnp.float32)


def _mm2(x, b):
    hi, lo = _split(x)
    d = functools.partial(jnp.dot, preferred_element_type=jnp.float32)
    return d(hi, b) + d(lo, b)


def _ln_rows(x):
    mu = jnp.mean(x, axis=-1, keepdims=True)
    xc = x - mu
    var = jnp.mean(xc * xc, axis=-1, keepdims=True)
    return xc * lax.rsqrt(var + LN_EPS)


def _sigmoid(x):
    return 1.0 / (1.0 + jnp.exp(-x))


def _ctx_rows(tm, n_ctx):
    return (pl.program_id(0) * tm + lax.broadcasted_iota(jnp.int32, (tm, 1), 0)) < n_ctx


def _mod_row(ref, is_ctx):
    return jnp.where(is_ctx, ref[1:2, :], ref[0:1, :])


def _block_indicator(width, head, value):
    i = jnp.arange(width) // head
    return jnp.where(i[:, None] == i[None, :], value, 0.0).astype(jnp.bfloat16)


def _ada_kernel(c_ref, w_ref, b_ref, o_ref):
    c = c_ref[...]
    h = c * _sigmoid(c)
    o_ref[...] = jnp.dot(h.astype(jnp.bfloat16), w_ref[...].astype(jnp.bfloat16),
                         preferred_element_type=jnp.float32) + b_ref[...]


def ada_modulation(c8, w, b):
    n = w.shape[1]
    tn = 1024
    return pl.pallas_call(
        _ada_kernel,
        grid=(n // tn,),
        in_specs=[pl.BlockSpec((8, D_MODEL), lambda j: (0, 0)),
                  pl.BlockSpec((D_MODEL, tn), lambda j: (0, j)),
                  pl.BlockSpec((1, tn), lambda j: (0, j))],
        out_specs=pl.BlockSpec((8, tn), lambda j: (0, j)),
        out_shape=jax.ShapeDtypeStruct((8, n), jnp.float32),
        compiler_params=_params(1),
    )(c8, w, b)


def _proj_kernel(x_ref, sh_ref, sc_ref, w_ref, o_ref, h_ref, *, n_ctx):
    @pl.when(pl.program_id(1) == 0)
    def _():
        is_ctx = _ctx_rows(x_ref.shape[0], n_ctx)
        h = _ln_rows(x_ref[...]) * (1.0 + _mod_row(sc_ref, is_ctx)) + _mod_row(sh_ref, is_ctx)
        h_ref[...] = h.astype(jnp.bfloat16)

    o_ref[...] = jnp.dot(h_ref[...], w_ref[...], preferred_element_type=jnp.float32)


def modulated_projection(x, mod, w_bf16, n_ctx):
    m = x.shape[0]
    n = w_bf16.shape[1]
    tm = PROJ_TM
    return pl.pallas_call(
        functools.partial(_proj_kernel, n_ctx=n_ctx),
        grid=(m // tm, n // CB),
        in_specs=[pl.BlockSpec((tm, D_MODEL), lambda i, j: (i, 0)),
                  pl.BlockSpec((8, D_MODEL), lambda i, j: (0, 0)),
                  pl.BlockSpec((8, D_MODEL), lambda i, j: (0, 1)),
                  pl.BlockSpec((D_MODEL, CB), lambda i, j: (0, j))],
        out_specs=pl.BlockSpec((tm, CB), lambda i, j: (i, j)),
        out_shape=jax.ShapeDtypeStruct((m, n), jnp.float32),
        scratch_shapes=[pltpu.VMEM((tm, D_MODEL), jnp.bfloat16)],
        compiler_params=_params(2),
    )(x, mod, mod, w_bf16)


def _halo_specs(tp, n_rows, col):
    per = tp // HALO
    last = n_rows // HALO - 1
    main = pl.BlockSpec((tp, CB), lambda i, *_: (i, col(*_) if callable(col) else col))
    prev = pl.BlockSpec((HALO, CB), lambda i, *_: (jnp.maximum(i * per - 1, 0), col(*_) if callable(col) else col))
    nxt = pl.BlockSpec((HALO, CB), lambda i, *_: (jnp.minimum((i + 1) * per, last), col(*_) if callable(col) else col))
    return [main, prev, nxt]


def _neighbours(x, prev, nxt, n_ctx, n_rows):
    tp = x.shape[0]
    loc = lax.broadcasted_iota(jnp.int32, (tp, 1), 0)
    row = pl.program_id(0) * tp + loc
    xp = jnp.where(loc == 0, prev[HALO - 1:HALO, :], pltpu.roll(x, 1, axis=0))
    xp = jnp.where((row == 0) | (row == n_ctx), 0.0, xp)
    xn = jnp.where(loc == tp - 1, nxt[0:1, :], pltpu.roll(x, tp - 1, axis=0))
    xn = jnp.where((row == n_ctx - 1) | (row == n_rows - 1), 0.0, xn)
    return xp, xn


def _conv3_kernel(z_ref, zp_ref, zn_ref, w_ref, b_ref, o_ref, *, n_ctx, n_rows):
    x = z_ref[...]
    xp, xn = _neighbours(x, zp_ref[...], zn_ref[...], n_ctx, n_rows)
    w = w_ref[...]
    o_ref[...] = xp * w[0:1, :] + x * w[1:2, :] + xn * w[2:3, :] + b_ref[...]


def hyena_conv3(z, w, b, n_ctx):
    n_rows = z.shape[0]
    tp = PREP_ROWS
    ncol = w.shape[1] // CB
    return pl.pallas_call(
        functools.partial(_conv3_kernel, n_ctx=n_ctx, n_rows=n_rows),
        grid=(n_rows // tp, ncol),
        in_specs=_halo_specs(tp, n_rows, lambda c: COL_HY + c)
        + [pl.BlockSpec((3, CB), lambda i, c: (0, c)), pl.BlockSpec((1, CB), lambda i, c: (0, c))],
        out_specs=pl.BlockSpec((tp, CB), lambda i, c: (i, c)),
        out_shape=jax.ShapeDtypeStruct((n_rows, w.shape[1]), jnp.float32),
        compiler_params=_params(2),
    )(z, z, z, w, b)


HY_EMB_PAD = 40
HY_FILTER_ROWS = 512


def hyena_features(L):
    t = jnp.linspace(0.0, 1.0, L, dtype=jnp.float32)[:, None]
    n_bands = (HY_EMB - 1) // 2
    f = jnp.linspace(1e-4, n_bands - 1, n_bands, dtype=jnp.float32)[None, :]
    ang = (2.0 * math.pi / L) * jnp.arange(L, dtype=jnp.float32)[:, None] * f
    z = jnp.concatenate([t, jnp.cos(ang), -jnp.sin(ang)], -1)
    return jnp.pad(z, ((0, 0), (0, HY_EMB_PAD - HY_EMB)))


def _filter_kernel(z_ref, w1_ref, b1_ref, w2_ref, b2_ref, w3_ref, b3_ref, fr_ref, dl_ref, h_ref, s_ref):
    i = pl.program_id(0)
    z = z_ref[...]
    fr = fr_ref[...]
    h = jnp.sin(fr * (_dot(z, w1_ref[...]) + b1_ref[...]))
    h = jnp.sin(fr * (_dot(h, w2_ref[...]) + b2_ref[...]))
    h = _dot(h, w3_ref[...]) + b3_ref[...]
    win = jnp.exp(-z[:, 0:1] * dl_ref[...])
    h = h * jnp.concatenate([win] * (h.shape[1] // win.shape[1]), axis=1)

    @pl.when(i == 0)
    def _():
        s_ref[...] = jnp.zeros_like(s_ref)

    s_ref[...] += jnp.sum(jnp.abs(h), axis=0, keepdims=True)
    row = lax.broadcasted_iota(jnp.int32, h.shape, 0) + i * h.shape[0]
    col = lax.broadcasted_iota(jnp.int32, h.shape, 1)
    neg = (col // GROUP_W) % 2 == 1
    h_ref[...] = jnp.where(neg & (row == 0), 0.0, h)


def hyena_filter_bank(L, w1, b1, w2, b2, w3, b3, freq):
    z = hyena_features(L)
    w1p = jnp.pad(w1, ((0, HY_EMB_PAD - HY_EMB), (0, 0)))
    max_decay = math.log(HY_TARGET) / HY_FAST_DECAY
    min_decay = math.log(HY_TARGET) / HY_SLOW_DECAY
    deltas = jnp.abs(jnp.linspace(min_decay, max_decay, GROUP_W, dtype=jnp.float32))[None, :]
    n = w3.shape[1]
    tr = min(L, HY_FILTER_ROWS)
    args = (z, w1p, b1[None, :], w2, b2[None, :], w3, b3[None, :], freq[None, :], deltas)
    return pl.pallas_call(
        _filter_kernel,
        grid=(L // tr,),
        in_specs=[pl.BlockSpec((tr, HY_EMB_PAD), lambda i: (i, 0))] + [_full(a) for a in args[1:]],
        out_specs=[pl.BlockSpec((tr, n), lambda i: (i, 0)), pl.BlockSpec((1, n), lambda i: (0, 0))],
        out_shape=[jax.ShapeDtypeStruct((L, n), jnp.float32), jax.ShapeDtypeStruct((1, n), jnp.float32)],
        compiler_params=_params(1),
    )(*args)


FFT_NB = 8
FFT_CB = 256
FFT_N1 = 128


def fft_tables(n1, n2):
    N = n1 * n2
    a = jnp.arange(n1, dtype=jnp.float32)[:, None, None]
    k2 = jnp.arange(n2, dtype=jnp.float32)[None, :, None]
    b = jnp.arange(n2 // 2, dtype=jnp.float32)[None, None, :]
    ph = (jnp.mod(a * k2, float(N)) / N + jnp.mod(b * k2, float(n2)) / n2) * (-2.0 * math.pi)
    g = jnp.concatenate([jnp.cos(ph), jnp.sin(ph)], axis=1)
    ginv = jnp.transpose(g, (0, 2, 1)) / N
    k1 = jnp.arange(n1, dtype=jnp.float32)[:, None]
    aa = jnp.arange(n1, dtype=jnp.float32)[None, :]
    f = jnp.mod(k1 * aa, float(n1)) * (-2.0 * math.pi / n1)
    fr, fi = jnp.cos(f), jnp.sin(f)
    ff = jnp.concatenate([jnp.concatenate([fr, -fi], axis=1), jnp.concatenate([fi, fr], axis=1)], axis=0)
    bf = lambda t: t.astype(jnp.bfloat16)
    return {"g": bf(g), "ginv": bf(ginv), "ff": bf(ff), "fft": bf(ff.T)}


def _stage_a_kernel(u_ref, g_ref, o_ref):
    for j in range(FFT_NB):
        o_ref[j] = _mm1(g_ref[j], u_ref[:, j, :].astype(jnp.bfloat16))


def fft_stage_a(u3, col, ncol, g):
    nb, n1, _ = u3.shape
    gspec = pl.BlockSpec((FFT_NB, 4 * nb, nb), lambda c, i: (i, 0, 0))
    return pl.pallas_call(
        _stage_a_kernel,
        grid=(ncol, n1 // FFT_NB),
        in_specs=[pl.BlockSpec((nb, FFT_NB, CB), lambda c, i: (0, i, col + c)), gspec],
        out_specs=pl.BlockSpec((FFT_NB, 4 * nb, CB), lambda c, i: (i, 0, c)),
        out_shape=jax.ShapeDtypeStruct((n1, 4 * nb, ncol * CB), jnp.float32),
        compiler_params=_params(2),
    )(u3, g)


def _stage_b_filter_kernel(re_ref, im_ref, f_ref, o_ref):
    ff = f_ref[...]
    for j in range(FFT_NB):
        o_ref[j] = _mm1(ff, jnp.concatenate([re_ref[:, j, :], im_ref[:, j, :]], axis=0).astype(jnp.bfloat16))


def fft_stage_b_filter(t1, ff):
    n1, n2x2, C = t1.shape
    n2 = n2x2 // 2
    blk = lambda off: pl.BlockSpec((n1, FFT_NB, FFT_CB), lambda c, i: (0, i + off, c))
    mat = pl.BlockSpec((2 * n1, 2 * n1), lambda c, i: (0, 0))
    return pl.pallas_call(
        _stage_b_filter_kernel,
        grid=(C // FFT_CB, n2 // FFT_NB),
        in_specs=[blk(0), blk(n2 // FFT_NB), mat],
        out_specs=pl.BlockSpec((FFT_NB, 2 * n1, FFT_CB), lambda c, i: (i, 0, c)),
        out_shape=jax.ShapeDtypeStruct((n2, 2 * n1, C), jnp.float32),
        compiler_params=_params(2),
    )(t1, t1, ff)


def _filter_spectrum(p, q, s, n1):
    return (p[:n1] + q[:n1]) * s, (p[n1:] - q[n1:]) * s


def _stage_b_conv_kernel(re_ref, im_ref, p_ref, q_ref, s_ref, f_ref, ft_ref, ore_ref, oim_ref):
    n1 = re_ref.shape[0]
    s = s_ref[...]
    ff, fft_ = f_ref[...], ft_ref[...]
    bf = lambda t: t.astype(jnp.bfloat16)
    for j in range(FFT_NB):
        x = _mm1(ff, bf(jnp.concatenate([re_ref[:, j, :], im_ref[:, j, :]], axis=0)))
        hr, hi = _filter_spectrum(p_ref[j], q_ref[j], s, n1)
        xr, xi = x[:n1], x[n1:]
        z = _mm1(fft_, bf(jnp.concatenate([xr * hr - xi * hi, xr * hi + xi * hr], axis=0)))
        ore_ref[:, j, :] = z[:n1]
        oim_ref[:, j, :] = z[n1:]


def fft_stage_b_conv(t1, spec, col_p, col_q, inv_norm, ff, fft_):
    n1, n2x2, C = t1.shape
    n2 = n2x2 // 2
    ncb = C // FFT_CB
    blk = lambda off: pl.BlockSpec((n1, FFT_NB, FFT_CB), lambda c, i: (0, i + off, c))
    mat = pl.BlockSpec((2 * n1, 2 * n1), lambda c, i: (0, 0))
    sp = lambda col: pl.BlockSpec((FFT_NB, 2 * n1, FFT_CB), lambda c, i: (i, 0, col * ncb + c))
    return pl.pallas_call(
        _stage_b_conv_kernel,
        grid=(ncb, n2 // FFT_NB),
        in_specs=[blk(0), blk(n2 // FFT_NB), sp(col_p), sp(col_q), pl.BlockSpec((1, FFT_CB), lambda c, i: (0, c)),
                  mat, mat],
        out_specs=[blk(0), blk(0)],
        out_shape=[jax.ShapeDtypeStruct((n1, n2, C), jnp.float32)] * 2,
        compiler_params=_params(2),
    )(t1, t1, spec, spec, inv_norm, ff, fft_)


def _stage_a_inv_kernel(re_ref, im_ref, g_ref, u_ref, gate_ref, bias_ref, o_ref):
    for j in range(FFT_NB):
        y = _mm1(g_ref[j], jnp.concatenate([re_ref[j], im_ref[j]], axis=0).astype(jnp.bfloat16))
        o_ref[:, j, :] = gate_ref[:, j, :] * (y + u_ref[:, j, :] * bias_ref[...])


def fft_stage_a_inv(t2re, t2im, ginv, u3, u_col, gate3, gate_col, bias):
    n1, n2, C = t2re.shape
    nb = n2 // 2
    tb = pl.BlockSpec((FFT_NB, n2, CB), lambda c, i: (i, 0, c))
    gb = pl.BlockSpec((FFT_NB, nb, 2 * n2), lambda c, i: (i, 0, 0))
    ub = lambda col: pl.BlockSpec((nb, FFT_NB, CB), lambda c, i: (0, i, col + c))
    return pl.pallas_call(
        _stage_a_inv_kernel,
        grid=(C // CB, n1 // FFT_NB),
        in_specs=[tb, tb, gb, ub(u_col), ub(gate_col), pl.BlockSpec((1, CB), lambda c, i: (0, c))],
        out_specs=ub(0),
        out_shape=jax.ShapeDtypeStruct((nb, n1, C), jnp.float32),
        compiler_params=_params(2),
    )(t2re, t2im, ginv, u3, gate3, bias)


def hyena_long_conv_chain(u, biases, filt, colsum, tabs):
    L = u.shape[0]
    C = GROUP_W
    n1 = FFT_N1
    n2 = 2 * L // n1
    spec = fft_stage_b_filter(fft_stage_a(filt.reshape(n2 // 2, n1, filt.shape[1]), 0, filt.shape[1] // CB, tabs["g"]),
                              tabs["ff"])
    s4 = colsum.reshape(HY_ORDER, 2, C)
    inv_norm = 1.0 / (s4[:, 0] + s4[:, 1])
    u3 = u.reshape(n2 // 2, n1, u.shape[1])
    y3, y_col = u3, 0
    for n in range(HY_ORDER):
        t1 = fft_stage_a(y3, y_col, 1, tabs["g"])
        t2re, t2im = fft_stage_b_conv(t1, spec, 2 * n, 2 * n + 1, inv_norm[n][None, :], tabs["ff"], tabs["fft"])
        y3 = fft_stage_a_inv(t2re, t2im, tabs["ginv"], y3, y_col, u3, n + 1, biases[n][None, :])
        y_col = 0
    return y3.reshape(L, C)


def dense_dft_tables(n):
    N = 2 * n
    k = jnp.arange(N, dtype=jnp.float32)[:, None]
    t = jnp.arange(n, dtype=jnp.float32)[None, :]
    ph = jnp.mod(k * t, float(N)) * (2.0 * math.pi / N)
    fd = jnp.concatenate([jnp.cos(ph), -jnp.sin(ph)], axis=0)
    return fd.astype(jnp.bfloat16), (fd.T / N).astype(jnp.bfloat16)


def _hyena_ctx_kernel(u_ref, h_ref, s_ref, bias_ref, fd_ref, ft_ref, o_ref):
    C = GROUP_W
    fd, ft = fd_ref[...], ft_ref[...]
    K = fd.shape[0] // 2
    bf = lambda t: t.astype(jnp.bfloat16)
    hs = _mm1(fd, bf(h_ref[...]))
    s = s_ref[...]
    y = u_ref[:, 0:C]
    for n in range(HY_ORDER):
        cp, cq = 2 * n * C, (2 * n + 1) * C
        inv = 1.0 / (s[:, cp:cp + C] + s[:, cq:cq + C])
        hr, hi = _filter_spectrum(hs[:, cp:cp + C], hs[:, cq:cq + C], inv, K)
        x = _mm1(fd, bf(y))
        xr, xi = x[:K], x[K:]
        conv = _mm1(ft, bf(jnp.concatenate([xr * hr - xi * hi, xr * hi + xi * hr], axis=0)))
        y = u_ref[:, (n + 1) * C:(n + 2) * C] * (conv + y * bias_ref[n:n + 1, :])
    o_ref[...] = y


def hyena_ctx(u, filt, colsum, biases, tabs):
    args = (u, filt, colsum, biases, tabs[0], tabs[1])
    return pl.pallas_call(
        _hyena_ctx_kernel,
        in_specs=[_full(a) for a in args],
        out_specs=pl.BlockSpec((u.shape[0], GROUP_W), lambda: (0, 0)),
        out_shape=jax.ShapeDtypeStruct((u.shape[0], GROUP_W), jnp.float32),
        compiler_params=pltpu.CompilerParams(vmem_limit_bytes=VMEM_LIMIT),
    )(*args)


SCAN_T = 64
GROUP_LANES = 256
RW_SUB = 16
HG_SUB = 16


def _chunk_index(d, i, n_ctx, n_all):
    bwd = jnp.where(i < n_ctx, n_ctx - 1 - i, n_all + n_ctx - 1 - i)
    return jnp.where(d == 0, i, bwd)


def _stacking(T, S, nh, head, sign):
    G = nh * head
    nb = T // S
    n = nh * T
    rr = lax.broadcasted_iota(jnp.int32, (n, G), 0)
    same = ((rr // S) % nh) == (lax.broadcasted_iota(jnp.int32, (n, G), 1) // head)

    def bd(x):
        pieces = []
        for i in range(nb):
            pieces += [x[i * S:(i + 1) * S]] * nh
        return jnp.where(same, jnp.concatenate(pieces, axis=0), 0.0)

    def collapse(o):
        outs = []
        for i in range(nb):
            acc = o[i * nh * S:i * nh * S + S]
            for h in range(1, nh):
                acc = acc + o[i * nh * S + h * S:i * nh * S + (h + 1) * S]
            outs.append(acc)
        return jnp.concatenate(outs, axis=0)

    rt = lax.broadcasted_iota(jnp.int32, (n, n), 0)
    cs = lax.broadcasted_iota(jnp.int32, (n, n), 1)
    t_r = (rt // (nh * S)) * S + rt % S
    t_c = (cs // (nh * S)) * S + cs % S
    same_h = ((rt // S) % nh) == ((cs // S) % nh)
    before = same_h & ((t_r - t_c) * sign > 0)
    return bd, collapse, before, rt, cs


def _softplus(x):
    return jnp.maximum(x, 0.0) + jnp.log(1.0 + jnp.exp(-jnp.abs(x)))


def _rwkv_prep_kernel(*refs, n_ctx, n_rows):
    zrefs, rest = refs[:15], refs[15:]
    (mu_ref, kk_w_ref, ka_ref, w0_ref, a0_ref, w2h_ref, w2l_ref, a2h_ref, a2l_ref, ones_ref,
     r_ref, k_ref, v_ref, g_ref, kk_ref, lw_ref, kd_ref, a_ref) = rest
    slabs = []
    for c in range(5):
        x = zrefs[3 * c][...]
        xp, xn = _neighbours(x, zrefs[3 * c + 1][...], zrefs[3 * c + 2][...], n_ctx, n_rows)
        slabs.append(x + (0.5 * (xp + xn) - x) * mu_ref[:, c * CB:(c + 1) * CB])
    r, k, v, g, lora = slabs
    r_ref[...], k_ref[...], v_ref[...], g_ref[...] = r, k, v, g
    kk = k * kk_w_ref[...]
    ss = _mm2(kk * kk, ones_ref[...])
    kk_ref[...] = kk * lax.rsqrt(jnp.maximum(ss, 1e-24))
    lora_t = _split(jnp.tanh(lora))
    lora_s = _split(lora)
    for d in range(2):
        w_log = -_softplus(-(w0_ref[d:d + 1, :] + _mm3(lora_t, (w2h_ref[d], w2l_ref[d])))) - 0.5
        lw_ref[d] = -jnp.exp(w_log)
        a = _sigmoid(a0_ref[d:d + 1, :] + _mm3(lora_s, (a2h_ref[d], a2l_ref[d])))
        a_ref[d] = a
        kd_ref[d] = k * (1.0 + (a - 1.0) * ka_ref[...])


def rwkv_prep(z, mu, k_k, k_a, w0, a0, w2, a2, n_ctx):
    n_rows = z.shape[0]
    tp = PREP_ROWS
    mu_p = jnp.pad(mu, (0, 5 * CB - RW_REAL))[None, :]
    w2p = jnp.zeros((2, CB, GROUP_W), jnp.float32)
    a2p = jnp.zeros((2, CB, GROUP_W), jnp.float32)
    for d in range(2):
        w2p = w2p.at[d, d * RW_LORA:(d + 1) * RW_LORA].set(w2[d])
        a2p = a2p.at[d, (2 + d) * RW_LORA:(3 + d) * RW_LORA].set(a2[d])
    w2s, a2s = _split(w2p), _split(a2p)
    small = (mu_p, k_k[None, :], k_a[None, :], w0, a0, w2s[0], w2s[1], a2s[0], a2s[1],
             _block_indicator(GROUP_W, RW_HEAD, 1.0))
    zspecs = []
    for c in range(5):
        zspecs += _halo_specs(tp, n_rows, COL_RW + c)
    one = pl.BlockSpec((tp, CB), lambda i: (i, 0))
    two = pl.BlockSpec((2, tp, CB), lambda i: (0, i, 0))
    s1 = jax.ShapeDtypeStruct((n_rows, GROUP_W), jnp.float32)
    s2 = jax.ShapeDtypeStruct((2, n_rows, GROUP_W), jnp.float32)
    return pl.pallas_call(
        functools.partial(_rwkv_prep_kernel, n_ctx=n_ctx, n_rows=n_rows),
        grid=(n_rows // tp,),
        in_specs=zspecs + [_full(a) for a in small],
        out_specs=[one] * 5 + [two] * 3,
        out_shape=[s1] * 5 + [s2] * 3,
        compiler_params=_params(1),
    )(*([z] * 15), *small)


def _rwkv_scan_kernel(r_ref, v_ref, kk_ref, lw_ref, k_ref, a_ref, o_ref, ht_ref, *, head):
    T, G, S = SCAN_T, GROUP_LANES, RW_SUB
    nh = G // head
    nb = T // S
    n = nh * T
    d = pl.program_id(0)
    sign = 1 - 2 * d

    @pl.when(pl.program_id(1) == 0)
    def _():
        ht_ref[...] = jnp.zeros_like(ht_ref)

    ti = lax.broadcasted_iota(jnp.int32, (T, T), 0)
    si = lax.broadcasted_iota(jnp.int32, (T, T), 1)
    tri_incl = jnp.where((ti - si) * sign >= 0, 1.0, 0.0)
    bd, collapse, before, rt, cs = _stacking(T, S, nh, head, sign)
    incl = before | (rt == cs)
    diag_blk = (rt // (nh * S)) == (cs // (nh * S))
    eye = jnp.where(rt == cs, 1.0, 0.0)
    bf = lambda t: t.astype(jnp.bfloat16)

    groups = range(r_ref.shape[1] // G)
    states = [ht_ref[grp] for grp in groups]
    results = []
    for grp in groups:
        ls = slice(grp * G, (grp + 1) * G)
        r, v, kk = r_ref[:, ls], v_ref[:, ls], kk_ref[:, ls]
        lw, k, a = lw_ref[0, :, ls], k_ref[0, :, ls], a_ref[0, :, ls]
        c = _dot(tri_incl, lw)
        ctot = jnp.sum(lw, axis=0, keepdims=True)
        beta = kk * a
        einv = jnp.exp(-c)
        efin = jnp.exp(ctot - c)
        a_bar = -kk * jnp.exp(c - lw)
        r_bar = r * jnp.exp(c)

        lhs = bf(jnp.concatenate([bd(a_bar), bd(r_bar)], axis=0))
        rhs = bf(jnp.concatenate([bd(k * einv), bd(beta * einv)], axis=0))
        m = _mm1(lhs, rhs, NT)
        ak, ab, rk, rb = m[:n, :n], m[:n, n:], m[n:, :n], m[n:, n:]
        ht = states[grp]
        g = _mm1(lhs, bf(ht), NT)
        v_b = bf(bd(v))
        x = g[:n] + _mm1(bf(jnp.where(before, ak, 0.0)), v_b)
        lb = jnp.where(before, ab, 0.0)
        ld = jnp.where(diag_blk, lb, 0.0)
        lo = lb - ld
        xd = eye + ld
        lp = bf(ld)
        p = 2
        while p < S:
            lp = bf(_mm1(lp, lp))
            xd = xd + _mm1(lp, bf(xd))
            p *= 2
        xd_b = bf(xd)
        n_b = bf(_mm1(xd_b, bf(lo)))
        y = _mm1(xd_b, bf(x))
        y = y + _mm1(n_b, bf(y))
        p = 2
        while p < nb:
            n_b = bf(_mm1(n_b, n_b))
            y = y + _mm1(n_b, bf(y))
            p *= 2
        u_b = bf(y)
        o = (g[n:] + _mm1(bf(jnp.where(incl, rk, 0.0)), v_b)) + _mm1(bf(jnp.where(incl, rb, 0.0)), u_b)
        ht_new = (ht * jnp.exp(ctot) + _mm1(v_b, bf(bd(k * efin)), TN)) + _mm1(u_b, bf(bd(beta * efin)), TN)
        results.append((ls, collapse(o), ht_new))
    for grp, (ls, o, ht_new) in enumerate(results):
        o_ref[0, :, ls] = o
        ht_ref[grp] = ht_new


def rwkv_scan(r, v, kk, lw2, k2, a2, n_ctx_rows):
    N, W = r.shape
    T = SCAN_T
    n_all, n_ctx = N // T, n_ctx_rows // T
    shared = pl.BlockSpec((T, W), lambda d, i: (_chunk_index(d, i, n_ctx, n_all), 0))
    per_dir = pl.BlockSpec((1, T, W), lambda d, i: (d, _chunk_index(d, i, n_ctx, n_all), 0))
    return pl.pallas_call(
        functools.partial(_rwkv_scan_kernel, head=RW_HEAD),
        grid=(2, n_all),
        in_specs=[shared, shared, shared, per_dir, per_dir, per_dir],
        out_specs=per_dir,
        out_shape=jax.ShapeDtypeStruct((2, N, W), jnp.float32),
        scratch_shapes=[pltpu.VMEM((W // GROUP_LANES, GROUP_LANES, GROUP_LANES), jnp.float32)],
        compiler_params=_params(2),
    )(r, v, kk, lw2, k2, a2)


def rope_tables(n_ctx, L):
    d_axis = RT_HEAD // 2
    half = d_axis // 2
    inv = ROPE_BASE ** (-jnp.arange(0, d_axis, 2, dtype=jnp.float32) / d_axis)
    t = jnp.arange(L)
    pos = jnp.stack([(t // GRID_W).astype(jnp.float32), (t % GRID_W).astype(jnp.float32)], axis=1)
    j = jnp.arange(RT_HEAD)
    ang = pos[:, j // d_axis] * inv[j % half][None, :]
    sgn = jnp.where((j % d_axis) < half, -1.0, 1.0)[None, :]
    cos = jnp.concatenate([jnp.ones((n_ctx, RT_HEAD), jnp.float32), jnp.cos(ang)], axis=0)
    sin = jnp.concatenate([jnp.zeros((n_ctx, RT_HEAD), jnp.float32), jnp.sin(ang) * sgn], axis=0)
    return jnp.tile(cos, (1, RT_HEADS)), jnp.tile(sin, (1, RT_HEADS))


def _rotate(x, cos, sin):
    G = x.shape[1]
    half = RT_HEAD // 4
    lane = lax.broadcasted_iota(jnp.int32, x.shape, 1)
    partner = jnp.where((lane % (2 * half)) < half, pltpu.roll(x, G - half, axis=1), pltpu.roll(x, half, axis=1))
    return x * cos + partner * sin


def _retention_scan_kernel(q_ref, k_ref, v_ref, cos_ref, sin_ref, lg_ref, o_ref, st_ref, *, head):
    T, G = SCAN_T, GROUP_LANES
    nh = G // head
    d = pl.program_id(0)
    sign = 1 - 2 * d

    @pl.when(pl.program_id(1) == 0)
    def _():
        st_ref[...] = jnp.zeros_like(st_ref)

    t = lax.broadcasted_iota(jnp.int32, (T, 1), 0)
    pos = (t + d * (T - 1 - 2 * t) + 1).astype(jnp.float32)
    bd, collapse, before, rt, cs = _stacking(T, T, nh, head, sign)
    incl = before | (rt == cs)
    bf = lambda x: x.astype(jnp.bfloat16)

    groups = range(q_ref.shape[1] // G)
    states = [st_ref[grp] for grp in groups]
    results = []
    for grp in groups:
        ls = slice(grp * G, (grp + 1) * G)
        cos, sin, lg = cos_ref[:, ls], sin_ref[:, ls], lg_ref[:, ls]
        q = _rotate(q_ref[:, ls], cos, sin)
        k = _rotate(k_ref[:, ls], cos, sin) * (head ** -0.5)
        c = pos * lg
        ctot = float(T) * lg
        q_b = bf(bd(q * jnp.exp(c)))
        k_til = k * jnp.exp(-c)
        k_hat = k * jnp.exp(ctot - c)
        v_b = bf(bd(v_ref[:, ls]))
        st = states[grp]
        scores = jnp.where(incl, _mm1(q_b, bf(bd(k_til)), NT), 0.0)
        results.append((ls, collapse(_mm1(q_b, bf(st), NT) + _mm1(bf(scores), v_b)),
                        st * jnp.exp(ctot) + _mm1(v_b, bf(bd(k_hat)), TN)))
    for grp, (ls, o, st_new) in enumerate(results):
        o_ref[0, :, ls] = o
        st_ref[grp] = st_new


def retention_scan(z, cos, sin, lg, n_ctx_rows):
    N = z.shape[0]
    T = SCAN_T
    n_all, n_ctx = N // T, n_ctx_rows // T
    zc = lambda col: pl.BlockSpec((T, CB), lambda d, i: (_chunk_index(d, i, n_ctx, n_all), COL_RT + col))
    tab = pl.BlockSpec((T, GROUP_W), lambda d, i: (_chunk_index(d, i, n_ctx, n_all), 0))
    return pl.pallas_call(
        functools.partial(_retention_scan_kernel, head=RT_HEAD),
        grid=(2, n_all),
        in_specs=[zc(0), zc(1), zc(2), tab, tab, pl.BlockSpec((1, GROUP_W), lambda d, i: (0, 0))],
        out_specs=pl.BlockSpec((1, T, GROUP_W), lambda d, i: (d, _chunk_index(d, i, n_ctx, n_all), 0)),
        out_shape=jax.ShapeDtypeStruct((2, N, GROUP_W), jnp.float32),
        scratch_shapes=[pltpu.VMEM((GROUP_W // GROUP_LANES, GROUP_LANES, GROUP_LANES), jnp.float32)],
        compiler_params=_params(2),
    )(z, z, z, cos, sin, lg)


def _gla_scan_kernel(q_ref, f_ref, i_ref, lb_ref, o_ref, st_ref, *, head):
    T, S = SCAN_T, HG_SUB
    W = q_ref.shape[-1]
    nh = W // head
    d = pl.program_id(0)
    sign = 1 - 2 * d

    @pl.when(pl.program_id(1) == 0)
    def _():
        st_ref[...] = jnp.zeros_like(st_ref)

    ti = lax.broadcasted_iota(jnp.int32, (S, S), 0)
    si = lax.broadcasted_iota(jnp.int32, (S, S), 1)
    tri_incl = jnp.where((ti - si) * sign >= 0, 1.0, 0.0)
    row = lax.broadcasted_iota(jnp.int32, (S, 1), 0)
    lb = lb_ref[0]

    states = [st_ref[h] for h in range(nh)]
    for j in range(T // S):
        jb = j + d * (T // S - 1 - 2 * j)
        rows = pl.ds(pl.multiple_of(jb * S, S), S)
        q = q_ref[rows, :]
        q = q * _sigmoid(q)
        v = i_ref[rows, :]
        gate = lb + (1.0 - lb) * _sigmoid(f_ref[rows, :])
        lf = jnp.log(jnp.maximum(gate, HG_MIN_GATE))
        k = 1.0 - gate
        b = _dot(tri_incl, lf)
        btot = jnp.sum(lf, axis=0, keepdims=True)
        qe = q * jnp.exp(b)
        ke = k * jnp.exp(btot - b)
        outs = []
        for h in range(nh):
            ls = slice(h * head, (h + 1) * head)
            qh, kh, vh, bh = q[:, ls], k[:, ls], v[:, ls], b[:, ls]
            st = states[h]
            o = _mm1(qe[:, ls].astype(jnp.bfloat16), st.astype(jnp.bfloat16), NT)
            for s in range(S):
                e = jnp.exp(jnp.minimum(bh - bh[s:s + 1, :], 0.0))
                a_s = jnp.sum(qh * kh[s:s + 1, :] * e, axis=-1, keepdims=True)
                a_s = jnp.where((row - s) * sign >= 0, a_s, 0.0)
                o = o + a_s * vh[s:s + 1, :]
            outs.append(o)
            states[h] = st * jnp.exp(btot[:, ls]) + _mm1(vh.astype(jnp.bfloat16), ke[:, ls].astype(jnp.bfloat16), TN)
        o_ref[0, rows, :] = jnp.concatenate(outs, axis=1)
    for h in range(nh):
        st_ref[h] = states[h]


def gla_scan(z, lb2, n_ctx_rows):
    N = z.shape[0]
    W = GROUP_W
    T = SCAN_T
    n_all, n_ctx = N // T, n_ctx_rows // T
    zc = lambda col: pl.BlockSpec((T, W), lambda d, i: (_chunk_index(d, i, n_ctx, n_all), COL_HG + col))
    zf = pl.BlockSpec((T, W), lambda d, i: (_chunk_index(d, i, n_ctx, n_all), COL_HG + 1 + d))
    return pl.pallas_call(
        functools.partial(_gla_scan_kernel, head=HG_HEAD),
        grid=(2, n_all),
        in_specs=[zc(0), zf, zc(3), pl.BlockSpec((1, 1, W), lambda d, i: (d, 0, 0))],
        out_specs=pl.BlockSpec((1, T, W), lambda d, i: (d, _chunk_index(d, i, n_ctx, n_all), 0)),
        out_shape=jax.ShapeDtypeStruct((2, N, W), jnp.float32),
        scratch_shapes=[pltpu.VMEM((W // HG_HEAD, HG_HEAD, HG_HEAD), jnp.float32)],
        compiler_params=_params(2),
    )(z, z, z, lb2)


def _outproj_kernel(hy_ref, rwo_ref, r_ref, k_ref, v_ref, rwg_ref, rto_ref, rtg_ref, hgo_ref, hgg_ref,
                    rk_ref, gng_ref, gnb_ref, hgn_ref, avg64_ref, avg128_ref,
                    w_ref, x_ref, gate_ref, g_ref, b_ref, o_ref, *, n_ctx):
    avg64, avg128 = avg64_ref[...], avg128_ref[...]

    def head_norm(o, avg, eps, centre):
        if centre:
            o = o - _mm2(o, avg)
        return o * lax.rsqrt(_mm2(o * o, avg) + eps)

    silu = lambda t: t * _sigmoid(t)
    y_rw = head_norm(rwo_ref[0] + rwo_ref[1], avg64, RW_GN_EPS, True) * gng_ref[...] + gnb_ref[...]
    bonus = (float(RW_HEAD) * _mm2(r_ref[...] * k_ref[...] * rk_ref[...], avg64)) * v_ref[...]
    y_rw = (y_rw + bonus) * _sigmoid(rwg_ref[...])
    y_rt = head_norm(rto_ref[0] + rto_ref[1], avg64, HEAD_NORM_EPS, True) * silu(rtg_ref[...])
    y_hg = head_norm(hgo_ref[0] + hgo_ref[1], avg128, HEAD_NORM_EPS, False) * hgn_ref[...] * silu(hgg_ref[...])
    y = None
    for m, ym in enumerate((hy_ref[...], y_rw, y_rt, y_hg)):
        part = jnp.dot(ym.astype(jnp.bfloat16), w_ref[m * GROUP_W:(m + 1) * GROUP_W, :],
                       preferred_element_type=jnp.float32)
        y = part if y is None else y + part
    is_ctx = _ctx_rows(x_ref.shape[0], n_ctx)
    r = ALPHA * x_ref[...] + _mod_row(gate_ref, is_ctx) * y
    o_ref[...] = _ln_rows(r) * g_ref[...] + b_ref[...]


def outproj_deepnorm(y_hy, rw_o2, rw_r, rw_k, rw_v, rw_g, rt_o2, hg_o2, z, r_k, gn_g, gn_b, hg_norm_g,
                     w_bf16, x, mod, g, b, n_ctx):
    m = x.shape[0]
    tm = OUT_TM
    one = pl.BlockSpec((tm, GROUP_W), lambda i: (i, 0))
    two = pl.BlockSpec((2, tm, GROUP_W), lambda i: (0, i, 0))
    zc = lambda col: pl.BlockSpec((tm, CB), lambda i: (i, col))
    row = pl.BlockSpec((tm, D_MODEL), lambda i: (i, 0))
    small = (r_k, gn_g, gn_b, hg_norm_g, _block_indicator(GROUP_W, RW_HEAD, 1.0 / RW_HEAD),
             _block_indicator(GROUP_W, HG_HEAD, 1.0 / HG_HEAD))
    return pl.pallas_call(
        functools.partial(_outproj_kernel, n_ctx=n_ctx),
        grid=(m // tm,),
        in_specs=[one, two, one, one, one, one, two, zc(COL_RT + 3), two, zc(COL_HG + 4)]
        + [_full(a) for a in small]
        + [_full(w_bf16), row, pl.BlockSpec((8, D_MODEL), lambda i: (0, 2)), _full(g), _full(b)],
        out_specs=row,
        out_shape=jax.ShapeDtypeStruct((m, D_MODEL), jnp.float32),
        compiler_params=_params(1),
    )(y_hy, rw_o2, rw_r, rw_k, rw_v, rw_g, rt_o2, z, hg_o2, z, *small, w_bf16, x, mod, g, b)


def _ffn_kernel(x_ref, sh_ref, sc_ref, w1_ref, w3_ref, w2_ref, gate_ref, g_ref, b_ref, o_ref, h_ref, acc_ref, *, n_ctx):
    j = pl.program_id(1)

    @pl.when(j == 0)
    def _():
        is_ctx = _ctx_rows(x_ref.shape[0], n_ctx)
        h = _ln_rows(x_ref[...]) * (1.0 + _mod_row(sc_ref, is_ctx)) + _mod_row(sh_ref, is_ctx)
        h_ref[...] = h.astype(jnp.bfloat16)
        acc_ref[...] = jnp.zeros_like(acc_ref)

    h = h_ref[...]
    a = jnp.dot(h, w1_ref[...], preferred_element_type=jnp.float32)
    u = jnp.dot(h, w3_ref[...], preferred_element_type=jnp.float32)
    s = (a * _sigmoid(a) * u).astype(jnp.bfloat16)
    acc_ref[...] += jnp.dot(s, w2_ref[...], preferred_element_type=jnp.float32)

    @pl.when(j == pl.num_programs(1) - 1)
    def _():
        is_ctx = _ctx_rows(x_ref.shape[0], n_ctx)
        r = ALPHA * x_ref[...] + _mod_row(gate_ref, is_ctx) * acc_ref[...]
        o_ref[...] = _ln_rows(r) * g_ref[...] + b_ref[...]


def ffn_deepnorm(x, mod, w1, w3, w2, g, b, n_ctx):
    m = x.shape[0]
    tm = PROJ_TM
    row = pl.BlockSpec((tm, D_MODEL), lambda i, j: (i, 0))
    modc = lambda c: pl.BlockSpec((8, D_MODEL), lambda i, j: (0, c))
    vec = pl.BlockSpec((1, D_MODEL), lambda i, j: (0, 0))
    return pl.pallas_call(
        functools.partial(_ffn_kernel, n_ctx=n_ctx),
        grid=(m // tm, FFN_HIDDEN // FFN_TF),
        in_specs=[row, modc(3), modc(4),
                  pl.BlockSpec((D_MODEL, FFN_TF), lambda i, j: (0, j)),
                  pl.BlockSpec((D_MODEL, FFN_TF), lambda i, j: (0, j)),
                  pl.BlockSpec((FFN_TF, D_MODEL), lambda i, j: (j, 0)),
                  modc(5), vec, vec],
        out_specs=row,
        out_shape=jax.ShapeDtypeStruct((m, D_MODEL), jnp.float32),
        scratch_shapes=[pltpu.VMEM((tm, D_MODEL), jnp.bfloat16),
                        pltpu.VMEM((tm, D_MODEL), jnp.float32)],
        compiler_params=_params(2),
    )(x, mod, mod, w1, w3, w2, mod, g, b)


def kernel(x, c, ctx, c_ctx, ada_w, ada_b, w_in, w_out, ln_g, ln_b, hy_conv_w, hy_conv_b, hy_w1, hy_b1, hy_w2, hy_b2, hy_w3, hy_b3, hy_freq, hy_bias, rw_mu, rw_w0, rw_w2, rw_a0, rw_a2, rw_k_k, rw_k_a, rw_r_k, rw_gn_g, rw_gn_b, hg_lb_raw, hg_norm_g, ffn_w1, ffn_w3, ffn_w2):
    L, n_ctx = x.shape[1], ctx.shape[1]
    sm = jax.nn.softmax(hg_lb_raw.astype(jnp.float32), axis=1)
    lower_bounds = jnp.cumsum(sm, axis=1) - sm[:, :1]
    log_gamma = jnp.log1p(-jnp.exp2(-5.0 - jnp.arange(RT_HEADS, dtype=jnp.float32)))
    lg = jnp.repeat(log_gamma, RT_HEAD)[None, :]
    cos, sin = rope_tables(n_ctx, L)
    fft_tabs = fft_tables(FFT_N1, 2 * L // FFT_N1)
    ctx_tabs = dense_dft_tables(n_ctx)

    c8 = jnp.zeros((8, D_MODEL), jnp.float32).at[0].set(c[0]).at[1].set(c_ctx)
    xs = jnp.concatenate([ctx[0], x[0]], axis=0)
    for l in range(DEPTH):
        with_ctx = l < DEPTH - 1
        mod = ada_modulation(c8, ada_w[l], ada_b[l][None, :])
        pad = jnp.zeros((D_MODEL, CB - 4 * RW_LORA), jnp.float32)
        split = (COL_RW * CB) + RW_REAL
        w_in_b = jnp.concatenate([w_in[l][:, :split], pad, w_in[l][:, split:]], axis=1).astype(jnp.bfloat16)
        w_out_b = w_out[l].astype(jnp.bfloat16)
        w1_b, w3_b, w2_b = (w[l].astype(jnp.bfloat16) for w in (ffn_w1, ffn_w3, ffn_w2))

        z = modulated_projection(xs, mod, w_in_b, n_ctx)

        u = hyena_conv3(z, hy_conv_w[l], hy_conv_b[l][None, :], n_ctx)
        hy_w = (hy_w1[l], hy_b1[l], hy_w2[l], hy_b2[l], hy_w3[l], hy_b3[l], hy_freq[l])
        filt, colsum = hyena_filter_bank(L, *hy_w)
        y_lat = hyena_long_conv_chain(u[n_ctx:], hy_bias[l], filt, colsum, fft_tabs)
        if with_ctx:
            filt_c, colsum_c = hyena_filter_bank(n_ctx, *hy_w)
            y_ctx = hyena_ctx(u[:n_ctx], filt_c, colsum_c, hy_bias[l], ctx_tabs)
        else:
            y_ctx = jnp.zeros((n_ctx, GROUP_W), jnp.float32)
        y_hy = jnp.concatenate([y_ctx, y_lat], axis=0)

        rw_r, rw_k, rw_v, rw_g, rw_kk, rw_lw, rw_kd, rw_a = rwkv_prep(
            z, rw_mu[l], rw_k_k[l], rw_k_a[l], rw_w0[l], rw_a0[l], rw_w2[l], rw_a2[l], n_ctx)
        rw_o2 = rwkv_scan(rw_r, rw_v, rw_kk, rw_lw, rw_kd, rw_a, n_ctx)
        rt_o2 = retention_scan(z, cos, sin, lg, n_ctx)
        hg_o2 = gla_scan(z, lower_bounds[:, l][:, None, :], n_ctx)

        xs = outproj_deepnorm(y_hy, rw_o2, rw_r, rw_k, rw_v, rw_g, rt_o2, hg_o2, z,
                              rw_r_k[l].reshape(1, GROUP_W), rw_gn_g[l][None, :], rw_gn_b[l][None, :],
                              hg_norm_g[l][None, :], w_out_b, xs, mod, ln_g[l, 0][None, :], ln_b[l, 0][None, :], n_ctx)
        xs = ffn_deepnorm(xs, mod, w1_b, w3_b, w2_b, ln_g[l, 1][None, :], ln_b[l, 1][None, :], n_ctx)
    return xs[n_ctx:][None]
```

```python
import functools
import math

import jax
import jax.numpy as jnp
from jax import lax
from jax.experimental import pallas as pl
from jax.experimental.pallas import tpu as pltpu

D_MODEL = 2048
DEPTH = 2
GRID_W = 64
N_MIXERS = 4
GROUP_W = D_MODEL // N_MIXERS
HY_ORDER = 2
HY_EMB = 33
HY_FAST_DECAY = 0.3
HY_SLOW_DECAY = 1.5
HY_TARGET = 1e-2
RW_HEAD = 64
RW_LORA = 96
RW_GN_EPS = 64e-5
RT_HEAD = 64
RT_HEADS = GROUP_W // RT_HEAD
ROPE_BASE = 10000.0
HG_HEAD = 128
HG_MIN_GATE = 1e-30
FFN_HIDDEN = 5632
ALPHA = (2 * DEPTH) ** 0.25
LN_EPS = 1e-6
HEAD_NORM_EPS = 1e-6

CB = 512
COL_HY = 0
COL_RW = 3
COL_RT = 8
COL_HG = 12
P_IN_PAD = 17 * CB
RW_REAL = 4 * GROUP_W + 4 * RW_LORA
PROJ_TM = 768
OUT_TM = 256
FFN_TF = 512
PREP_ROWS = 256
HALO = 8
VMEM_LIMIT = 56 * 1024 * 1024

HI = lax.Precision.HIGHEST
NN = ((1,), (0,))
NT = ((1,), (1,))
TN = ((0,), (0,))


def _params(n_axes):
    return pltpu.CompilerParams(dimension_semantics=("arbitrary",) * n_axes, vmem_limit_bytes=VMEM_LIMIT)


def _full(a):
    return pl.BlockSpec(a.shape, lambda *_: (0,) * a.ndim)


def _dot(a, b):
    return jnp.dot(a, b, precision=HI, preferred_element_type=jnp.float32)


def _split(x):
    hi = x.astype(jnp.bfloat16)
    lo = (x - hi.astype(jnp.float32)).astype(jnp.bfloat16)
    return hi, lo


def _mm3(a, b, dims=NN):
    d = lambda p, q: lax.dot_general(p, q, (dims, ((), ())), preferred_element_type=jnp.float32)
    return d(a[0], b[0]) + (d(a[0], b[1]) + d(a[1], b[0]))


def _mm1(a, b, dims=NN):
    return lax.dot_general(a, b, (dims, ((), ())), preferred_element_type=jnp.float32)


def _mm2(x, b):
    hi, lo = _split(x)
    d = functools.partial(jnp.dot, preferred_element_type=jnp.float32)
    return d(hi, b) + d(lo, b)


def _ln_rows(x):
    mu = jnp.mean(x, axis=-1, keepdims=True)
    xc = x - mu
    var = jnp.mean(xc * xc, axis=-1, keepdims=True)
    return xc * lax.rsqrt(var + LN_EPS)


def _sigmoid(x):
    return 1.0 / (1.0 + jnp.exp(-x))


def _ctx_rows(tm, n_ctx):
    return (pl.program_id(0) * tm + lax.broadcasted_iota(jnp.int32, (tm, 1), 0)) < n_ctx


def _mod_row(ref, is_ctx):
    return jnp.where(is_ctx, ref[1:2, :], ref[0:1, :])


def _block_indicator(width, head, value):
    i = jnp.arange(width) // head
    return jnp.where(i[:, None] == i[None, :], value, 0.0).astype(jnp.bfloat16)


def _ada_kernel(c_ref, w_ref, b_ref, o_ref):
    c = c_ref[...]
    h = c * _sigmoid(c)
    o_ref[...] = jnp.dot(h.astype(jnp.bfloat16), w_ref[...].astype(jnp.bfloat16),
                         preferred_element_type=jnp.float32) + b_ref[...]


def ada_modulation(c8, w, b):
    n = w.shape[1]
    tn = 1024
    return pl.pallas_call(
        _ada_kernel,
        grid=(n // tn,),
        in_specs=[pl.BlockSpec((8, D_MODEL), lambda j: (0, 0)),
                  pl.BlockSpec((D_MODEL, tn), lambda j: (0, j)),
                  pl.BlockSpec((1, tn), lambda j: (0, j))],
        out_specs=pl.BlockSpec((8, tn), lambda j: (0, j)),
        out_shape=jax.ShapeDtypeStruct((8, n), jnp.float32),
        compiler_params=_params(1),
    )(c8, w, b)


def _proj_kernel(x_ref, sh_ref, sc_ref, w_ref, o_ref, h_ref, *, n_ctx):
    @pl.when(pl.program_id(1) == 0)
    def _():
        is_ctx = _ctx_rows(x_ref.shape[0], n_ctx)
        h = _ln_rows(x_ref[...]) * (1.0 + _mod_row(sc_ref, is_ctx)) + _mod_row(sh_ref, is_ctx)
        h_ref[...] = h.astype(jnp.bfloat16)

    o_ref[...] = jnp.dot(h_ref[...], w_ref[...], preferred_element_type=jnp.float32)


def modulated_projection(x, mod, w_bf16, n_ctx):
    m = x.shape[0]
    n = w_bf16.shape[1]
    tm = PROJ_TM
    return pl.pallas_call(
        functools.partial(_proj_kernel, n_ctx=n_ctx),
        grid=(m // tm, n // CB),
        in_specs=[pl.BlockSpec((tm, D_MODEL), lambda i, j: (i, 0)),
                  pl.BlockSpec((8, D_MODEL), lambda i, j: (0, 0)),
                  pl.BlockSpec((8, D_MODEL), lambda i, j: (0, 1)),
                  pl.BlockSpec((D_MODEL, CB), lambda i, j: (0, j))],
        out_specs=pl.BlockSpec((tm, CB), lambda i, j: (i, j)),
        out_shape=jax.ShapeDtypeStruct((m, n), jnp.float32),
        scratch_shapes=[pltpu.VMEM((tm, D_MODEL), jnp.bfloat16)],
        compiler_params=_params(2),
    )(x, mod, mod, w_bf16)


def _halo_specs(tp, n_rows, col):
    per = tp // HALO
    last = n_rows // HALO - 1
    main = pl.BlockSpec((tp, CB), lambda i, *_: (i, col(*_) if callable(col) else col))
    prev = pl.BlockSpec((HALO, CB), lambda i, *_: (jnp.maximum(i * per - 1, 0), col(*_) if callable(col) else col))
    nxt = pl.BlockSpec((HALO, CB), lambda i, *_: (jnp.minimum((i + 1) * per, last), col(*_) if callable(col) else col))
    return [main, prev, nxt]


def _neighbours(x, prev, nxt, n_ctx, n_rows):
    tp = x.shape[0]
    loc = lax.broadcasted_iota(jnp.int32, (tp, 1), 0)
    row = pl.program_id(0) * tp + loc
    xp = jnp.where(loc == 0, prev[HALO - 1:HALO, :], pltpu.roll(x, 1, axis=0))
    xp = jnp.where((row == 0) | (row == n_ctx), 0.0, xp)
    xn = jnp.where(loc == tp - 1, nxt[0:1, :], pltpu.roll(x, tp - 1, axis=0))
    xn = jnp.where((row == n_ctx - 1) | (row == n_rows - 1), 0.0, xn)
    return xp, xn


def _conv3_kernel(z_ref, zp_ref, zn_ref, w_ref, b_ref, o_ref, *, n_ctx, n_rows):
    x = z_ref[...]
    xp, xn = _neighbours(x, zp_ref[...], zn_ref[...], n_ctx, n_rows)
    w = w_ref[...]
    o_ref[...] = xp * w[0:1, :] + x * w[1:2, :] + xn * w[2:3, :] + b_ref[...]


def hyena_conv3(z, w, b, n_ctx):
    n_rows = z.shape[0]
    tp = PREP_ROWS
    ncol = w.shape[1] // CB
    return pl.pallas_call(
        functools.partial(_conv3_kernel, n_ctx=n_ctx, n_rows=n_rows),
        grid=(n_rows // tp, ncol),
        in_specs=_halo_specs(tp, n_rows, lambda c: COL_HY + c)
        + [pl.BlockSpec((3, CB), lambda i, c: (0, c)), pl.BlockSpec((1, CB), lambda i, c: (0, c))],
        out_specs=pl.BlockSpec((tp, CB), lambda i, c: (i, c)),
        out_shape=jax.ShapeDtypeStruct((n_rows, w.shape[1]), jnp.float32),
        compiler_params=_params(2),
    )(z, z, z, w, b)


HY_EMB_PAD = 40
HY_FILTER_ROWS = 512


def hyena_features(L):
    t = jnp.linspace(0.0, 1.0, L, dtype=jnp.float32)[:, None]
    n_bands = (HY_EMB - 1) // 2
    f = jnp.linspace(1e-4, n_bands - 1, n_bands, dtype=jnp.float32)[None, :]
    ang = (2.0 * math.pi / L) * jnp.arange(L, dtype=jnp.float32)[:, None] * f
    z = jnp.concatenate([t, jnp.cos(ang), -jnp.sin(ang)], -1)
    return jnp.pad(z, ((0, 0), (0, HY_EMB_PAD - HY_EMB)))


def _filter_kernel(z_ref, w1_ref, b1_ref, w2_ref, b2_ref, w3_ref, b3_ref, fr_ref, dl_ref, h_ref, s_ref):
    i = pl.program_id(0)
    z = z_ref[...]
    fr = fr_ref[...]
    h = jnp.sin(fr * (_dot(z, w1_ref[...]) + b1_ref[...]))
    h = jnp.sin(fr * (_dot(h, w2_ref[...]) + b2_ref[...]))
    h = _dot(h, w3_ref[...]) + b3_ref[...]
    win = jnp.exp(-z[:, 0:1] * dl_ref[...])
    h = h * jnp.concatenate([win] * (h.shape[1] // win.shape[1]), axis=1)

    @pl.when(i == 0)
    def _():
        s_ref[...] = jnp.zeros_like(s_ref)

    s_ref[...] += jnp.sum(jnp.abs(h), axis=0, keepdims=True)
    row = lax.broadcasted_iota(jnp.int32, h.shape, 0) + i * h.shape[0]
    col = lax.broadcasted_iota(jnp.int32, h.shape, 1)
    neg = (col // GROUP_W) % 2 == 1
    h_ref[...] = jnp.where(neg & (row == 0), 0.0, h)


def hyena_filter_bank(L, w1, b1, w2, b2, w3, b3, freq):
    z = hyena_features(L)
    w1p = jnp.pad(w1, ((0, HY_EMB_PAD - HY_EMB), (0, 0)))
    max_decay = math.log(HY_TARGET) / HY_FAST_DECAY
    min_decay = math.log(HY_TARGET) / HY_SLOW_DECAY
    deltas = jnp.abs(jnp.linspace(min_decay, max_decay, GROUP_W, dtype=jnp.float32))[None, :]
    n = w3.shape[1]
    tr = min(L, HY_FILTER_ROWS)
    args = (z, w1p, b1[None, :], w2, b2[None, :], w3, b3[None, :], freq[None, :], deltas)
    return pl.pallas_call(
        _filter_kernel,
        grid=(L // tr,),
        in_specs=[pl.BlockSpec((tr, HY_EMB_PAD), lambda i: (i, 0))] + [_full(a) for a in args[1:]],
        out_specs=[pl.BlockSpec((tr, n), lambda i: (i, 0)), pl.BlockSpec((1, n), lambda i: (0, 0))],
        out_shape=[jax.ShapeDtypeStruct((L, n), jnp.float32), jax.ShapeDtypeStruct((1, n), jnp.float32)],
        compiler_params=_params(1),
    )(*args)


FFT_NB = 8
FFT_CB = 256
FFT_N1 = 128


def fft_tables(n1, n2):
    N = n1 * n2
    nk = -(-(n2 // 2 + 1) // FFT_NB) * FFT_NB
    a = jnp.arange(n1, dtype=jnp.float32)[:, None, None]
    k2 = jnp.arange(nk, dtype=jnp.float32)[None, :, None]
    b = jnp.arange(n2 // 2, dtype=jnp.float32)[None, None, :]
    ph = (jnp.mod(a * k2, float(N)) / N + jnp.mod(b * k2, float(n2)) / n2) * (-2.0 * math.pi)
    g = jnp.concatenate([jnp.cos(ph), jnp.sin(ph)], axis=1)
    kk = jnp.arange(nk)
    weight = jnp.where((kk == 0) | (kk == n2 // 2), 1.0, jnp.where(kk < n2 // 2, 2.0, 0.0))
    ginv = jnp.transpose(g, (0, 2, 1)) * (jnp.tile(weight, 2) / N)
    k1 = jnp.arange(n1, dtype=jnp.float32)[:, None]
    aa = jnp.arange(n1, dtype=jnp.float32)[None, :]
    f = jnp.mod(k1 * aa, float(n1)) * (-2.0 * math.pi / n1)
    fr, fi = jnp.cos(f), jnp.sin(f)
    ff = jnp.concatenate([jnp.concatenate([fr, -fi], axis=1), jnp.concatenate([fi, fr], axis=1)], axis=0)
    bf = lambda t: t.astype(jnp.bfloat16)
    return {"g": bf(g), "ginv": bf(ginv), "ff": bf(ff), "fft": bf(ff.T)}


def _stage_a_kernel(u_ref, g_ref, o_ref):
    for j in range(FFT_NB):
        o_ref[j] = _mm1(g_ref[j], u_ref[:, j, :].astype(jnp.bfloat16))


def fft_stage_a(u3, col, ncol, g):
    nb, n1, _ = u3.shape
    rows = g.shape[1]
    gspec = pl.BlockSpec((FFT_NB, rows, nb), lambda c, i: (i, 0, 0))
    return pl.pallas_call(
        _stage_a_kernel,
        grid=(ncol, n1 // FFT_NB),
        in_specs=[pl.BlockSpec((nb, FFT_NB, CB), lambda c, i: (0, i, col + c)), gspec],
        out_specs=pl.BlockSpec((FFT_NB, rows, CB), lambda c, i: (i, 0, c)),
        out_shape=jax.ShapeDtypeStruct((n1, rows, ncol * CB), jnp.float32),
        compiler_params=_params(2),
    )(u3, g)


def _stage_b_filter_kernel(re_ref, im_ref, f_ref, o_ref):
    ff = f_ref[...]
    for j in range(FFT_NB):
        o_ref[j] = _mm1(ff, jnp.concatenate([re_ref[:, j, :], im_ref[:, j, :]], axis=0).astype(jnp.bfloat16))


def fft_stage_b_filter(t1, ff):
    n1, n2x2, C = t1.shape
    n2 = n2x2 // 2
    blk = lambda off: pl.BlockSpec((n1, FFT_NB, FFT_CB), lambda c, i: (0, i + off, c))
    mat = pl.BlockSpec((2 * n1, 2 * n1), lambda c, i: (0, 0))
    return pl.pallas_call(
        _stage_b_filter_kernel,
        grid=(C // FFT_CB, n2 // FFT_NB),
        in_specs=[blk(0), blk(n2 // FFT_NB), mat],
        out_specs=pl.BlockSpec((FFT_NB, 2 * n1, FFT_CB), lambda c, i: (i, 0, c)),
        out_shape=jax.ShapeDtypeStruct((n2, 2 * n1, C), jnp.float32),
        compiler_params=_params(2),
    )(t1, t1, ff)


def _filter_spectrum(p, q, s, n1):
    return (p[:n1] + q[:n1]) * s, (p[n1:] - q[n1:]) * s


def _stage_b_conv_kernel(re_ref, im_ref, p_ref, q_ref, s_ref, f_ref, ft_ref, ore_ref, oim_ref):
    n1 = re_ref.shape[0]
    s = s_ref[...]
    ff, fft_ = f_ref[...], ft_ref[...]
    bf = lambda t: t.astype(jnp.bfloat16)
    for j in range(FFT_NB):
        x = _mm1(ff, bf(jnp.concatenate([re_ref[:, j, :], im_ref[:, j, :]], axis=0)))
        hr, hi = _filter_spectrum(p_ref[j], q_ref[j], s, n1)
        xr, xi = x[:n1], x[n1:]
        z = _mm1(fft_, bf(jnp.concatenate([xr * hr - xi * hi, xr * hi + xi * hr], axis=0)))
        ore_ref[:, j, :] = z[:n1]
        oim_ref[:, j, :] = z[n1:]


def fft_stage_b_conv(t1, spec, col_p, col_q, inv_norm, ff, fft_):
    n1, n2x2, C = t1.shape
    n2 = n2x2 // 2
    ncb = C // FFT_CB
    blk = lambda off: pl.BlockSpec((n1, FFT_NB, FFT_CB), lambda c, i: (0, i + off, c))
    mat = pl.BlockSpec((2 * n1, 2 * n1), lambda c, i: (0, 0))
    sp = lambda col: pl.BlockSpec((FFT_NB, 2 * n1, FFT_CB), lambda c, i: (i, 0, col * ncb + c))
    return pl.pallas_call(
        _stage_b_conv_kernel,
        grid=(ncb, n2 // FFT_NB),
        in_specs=[blk(0), blk(n2 // FFT_NB), sp(col_p), sp(col_q), pl.BlockSpec((1, FFT_CB), lambda c, i: (0, c)),
                  mat, mat],
        out_specs=[blk(0), blk(0)],
        out_shape=[jax.ShapeDtypeStruct((n1, n2, C), jnp.float32)] * 2,
        compiler_params=_params(2),
    )(t1, t1, spec, spec, inv_norm, ff, fft_)


def _stage_a_inv_kernel(re_ref, im_ref, g_ref, u_ref, gate_ref, bias_ref, o_ref):
    for j in range(FFT_NB):
        y = _mm1(g_ref[j], jnp.concatenate([re_ref[j], im_ref[j]], axis=0).astype(jnp.bfloat16))
        o_ref[:, j, :] = gate_ref[:, j, :] * (y + u_ref[:, j, :] * bias_ref[...])


def fft_stage_a_inv(t2re, t2im, ginv, u3, u_col, gate3, gate_col, bias):
    n1, nk, C = t2re.shape
    nb = u3.shape[0]
    tb = pl.BlockSpec((FFT_NB, nk, CB), lambda c, i: (i, 0, c))
    gb = pl.BlockSpec((FFT_NB, nb, 2 * nk), lambda c, i: (i, 0, 0))
    ub = lambda col: pl.BlockSpec((nb, FFT_NB, CB), lambda c, i: (0, i, col + c))
    return pl.pallas_call(
        _stage_a_inv_kernel,
        grid=(C // CB, n1 // FFT_NB),
        in_specs=[tb, tb, gb, ub(u_col), ub(gate_col), pl.BlockSpec((1, CB), lambda c, i: (0, c))],
        out_specs=ub(0),
        out_shape=jax.ShapeDtypeStruct((nb, n1, C), jnp.float32),
        compiler_params=_params(2),
    )(t2re, t2im, ginv, u3, gate3, bias)


def hyena_long_conv_chain(u, biases, filt, colsum, tabs):
    L = u.shape[0]
    C = GROUP_W
    n1 = FFT_N1
    n2 = 2 * L // n1
    spec = fft_stage_b_filter(fft_stage_a(filt.reshape(n2 // 2, n1, filt.shape[1]), 0, filt.shape[1] // CB, tabs["g"]),
                              tabs["ff"])
    s4 = colsum.reshape(HY_ORDER, 2, C)
    inv_norm = 1.0 / (s4[:, 0] + s4[:, 1])
    u3 = u.reshape(n2 // 2, n1, u.shape[1])
    y3, y_col = u3, 0
    for n in range(HY_ORDER):
        t1 = fft_stage_a(y3, y_col, 1, tabs["g"])
        t2re, t2im = fft_stage_b_conv(t1, spec, 2 * n, 2 * n + 1, inv_norm[n][None, :], tabs["ff"], tabs["fft"])
        y3 = fft_stage_a_inv(t2re, t2im, tabs["ginv"], y3, y_col, u3, n + 1, biases[n][None, :])
        y_col = 0
    return y3.reshape(L, C)


def dense_dft_tables(n):
    N = 2 * n
    k = jnp.arange(N, dtype=jnp.float32)[:, None]
    t = jnp.arange(n, dtype=jnp.float32)[None, :]
    ph = jnp.mod(k * t, float(N)) * (2.0 * math.pi / N)
    fd = jnp.concatenate([jnp.cos(ph), -jnp.sin(ph)], axis=0)
    return fd.astype(jnp.bfloat16), (fd.T / N).astype(jnp.bfloat16)


def _hyena_ctx_kernel(u_ref, h_ref, s_ref, bias_ref, fd_ref, ft_ref, o_ref):
    C = GROUP_W
    fd, ft = fd_ref[...], ft_ref[...]
    K = fd.shape[0] // 2
    bf = lambda t: t.astype(jnp.bfloat16)
    hs = _mm1(fd, bf(h_ref[...]))
    s = s_ref[...]
    y = u_ref[:, 0:C]
    for n in range(HY_ORDER):
        cp, cq = 2 * n * C, (2 * n + 1) * C
        inv = 1.0 / (s[:, cp:cp + C] + s[:, cq:cq + C])
        hr, hi = _filter_spectrum(hs[:, cp:cp + C], hs[:, cq:cq + C], inv, K)
        x = _mm1(fd, bf(y))
        xr, xi = x[:K], x[K:]
        conv = _mm1(ft, bf(jnp.concatenate([xr * hr - xi * hi, xr * hi + xi * hr], axis=0)))
        y = u_ref[:, (n + 1) * C:(n + 2) * C] * (conv + y * bias_ref[n:n + 1, :])
    o_ref[...] = y


def hyena_ctx(u, filt, colsum, biases, tabs):
    args = (u, filt, colsum, biases, tabs[0], tabs[1])
    return pl.pallas_call(
        _hyena_ctx_kernel,
        in_specs=[_full(a) for a in args],
        out_specs=pl.BlockSpec((u.shape[0], GROUP_W), lambda: (0, 0)),
        out_shape=jax.ShapeDtypeStruct((u.shape[0], GROUP_W), jnp.float32),
        compiler_params=pltpu.CompilerParams(vmem_limit_bytes=VMEM_LIMIT),
    )(*args)


SCAN_T = 64
GROUP_LANES = 256
RW_SUB = 16
HG_SUB = 16


def _chunk_index(d, i, n_ctx, n_all):
    bwd = jnp.where(i < n_ctx, n_ctx - 1 - i, n_all + n_ctx - 1 - i)
    return jnp.where(d == 0, i, bwd)


def _stacking(T, S, nh, head, sign):
    G = nh * head
    nb = T // S
    n = nh * T
    rr = lax.broadcasted_iota(jnp.int32, (n, G), 0)
    same = ((rr // S) % nh) == (lax.broadcasted_iota(jnp.int32, (n, G), 1) // head)

    def bd(x):
        pieces = []
        for i in range(nb):
            pieces += [x[i * S:(i + 1) * S]] * nh
        return jnp.where(same, jnp.concatenate(pieces, axis=0), 0.0)

    def collapse(o):
        outs = []
        for i in range(nb):
            acc = o[i * nh * S:i * nh * S + S]
            for h in range(1, nh):
                acc = acc + o[i * nh * S + h * S:i * nh * S + (h + 1) * S]
            outs.append(acc)
        return jnp.concatenate(outs, axis=0)

    rt = lax.broadcasted_iota(jnp.int32, (n, n), 0)
    cs = lax.broadcasted_iota(jnp.int32, (n, n), 1)
    t_r = (rt // (nh * S)) * S + rt % S
    t_c = (cs // (nh * S)) * S + cs % S
    same_h = ((rt // S) % nh) == ((cs // S) % nh)
    before = same_h & ((t_r - t_c) * sign > 0)
    return bd, collapse, before, rt, cs


def _softplus(x):
    return jnp.maximum(x, 0.0) + jnp.log(1.0 + jnp.exp(-jnp.abs(x)))


def _rwkv_prep_kernel(*refs, n_ctx, n_rows):
    zrefs, rest = refs[:15], refs[15:]
    (mu_ref, kk_w_ref, ka_ref, w0_ref, a0_ref, w2h_ref, w2l_ref, a2h_ref, a2l_ref, ones_ref,
     r_ref, k_ref, v_ref, g_ref, kk_ref, lw_ref, kd_ref, a_ref) = rest
    slabs = []
    for c in range(5):
        x = zrefs[3 * c][...]
        xp, xn = _neighbours(x, zrefs[3 * c + 1][...], zrefs[3 * c + 2][...], n_ctx, n_rows)
        slabs.append(x + (0.5 * (xp + xn) - x) * mu_ref[:, c * CB:(c + 1) * CB])
    r, k, v, g, lora = slabs
    r_ref[...], k_ref[...], v_ref[...], g_ref[...] = r, k, v, g
    kk = k * kk_w_ref[...]
    ss = _mm2(kk * kk, ones_ref[...])
    kk_ref[...] = kk * lax.rsqrt(jnp.maximum(ss, 1e-24))
    lora_t = _split(jnp.tanh(lora))
    lora_s = _split(lora)
    for d in range(2):
        w_log = -_softplus(-(w0_ref[d:d + 1, :] + _mm3(lora_t, (w2h_ref[d], w2l_ref[d])))) - 0.5
        lw_ref[d] = -jnp.exp(w_log)
        a = _sigmoid(a0_ref[d:d + 1, :] + _mm3(lora_s, (a2h_ref[d], a2l_ref[d])))
        a_ref[d] = a
        kd_ref[d] = k * (1.0 + (a - 1.0) * ka_ref[...])


def rwkv_prep(z, mu, k_k, k_a, w0, a0, w2, a2, n_ctx):
    n_rows = z.shape[0]
    tp = PREP_ROWS
    mu_p = jnp.pad(mu, (0, 5 * CB - RW_REAL))[None, :]
    w2p = jnp.zeros((2, CB, GROUP_W), jnp.float32)
    a2p = jnp.zeros((2, CB, GROUP_W), jnp.float32)
    for d in range(2):
        w2p = w2p.at[d, d * RW_LORA:(d + 1) * RW_LORA].set(w2[d])
        a2p = a2p.at[d, (2 + d) * RW_LORA:(3 + d) * RW_LORA].set(a2[d])
    w2s, a2s = _split(w2p), _split(a2p)
    small = (mu_p, k_k[None, :], k_a[None, :], w0, a0, w2s[0], w2s[1], a2s[0], a2s[1],
             _block_indicator(GROUP_W, RW_HEAD, 1.0))
    zspecs = []
    for c in range(5):
        zspecs += _halo_specs(tp, n_rows, COL_RW + c)
    one = pl.BlockSpec((tp, CB), lambda i: (i, 0))
    two = pl.BlockSpec((2, tp, CB), lambda i: (0, i, 0))
    s1 = jax.ShapeDtypeStruct((n_rows, GROUP_W), jnp.float32)
    s2 = jax.ShapeDtypeStruct((2, n_rows, GROUP_W), jnp.float32)
    return pl.pallas_call(
        functools.partial(_rwkv_prep_kernel, n_ctx=n_ctx, n_rows=n_rows),
        grid=(n_rows // tp,),
        in_specs=zspecs + [_full(a) for a in small],
        out_specs=[one] * 5 + [two] * 3,
        out_shape=[s1] * 5 + [s2] * 3,
        compiler_params=_params(1),
    )(*([z] * 15), *small)


def _rwkv_scan_kernel(r_ref, v_ref, kk_ref, lw_ref, k_ref, a_ref, o_ref, ht_ref, *, head):
    T, G, S = SCAN_T, GROUP_LANES, RW_SUB
    nh = G // head
    nb = T // S
    n = nh * T
    d = pl.program_id(0)
    sign = 1 - 2 * d

    @pl.when(pl.program_id(1) == 0)
    def _():
        ht_ref[...] = jnp.zeros_like(ht_ref)

    ti = lax.broadcasted_iota(jnp.int32, (T, T), 0)
    si = lax.broadcasted_iota(jnp.int32, (T, T), 1)
    tri_incl = jnp.where((ti - si) * sign >= 0, 1.0, 0.0)
    bd, collapse, before, rt, cs = _stacking(T, S, nh, head, sign)
    incl = before | (rt == cs)
    diag_blk = (rt // (nh * S)) == (cs // (nh * S))
    eye = jnp.where(rt == cs, 1.0, 0.0)
    bf = lambda t: t.astype(jnp.bfloat16)

    groups = range(r_ref.shape[1] // G)
    lanes = [slice(grp * G, (grp + 1) * G) for grp in groups]
    each = lambda f, *seqs: [f(*xs) for xs in zip(*seqs)]
    ht = [ht_ref[grp] for grp in groups]
    r, v, kk = ([ref[:, ls] for ls in lanes] for ref in (r_ref, v_ref, kk_ref))
    lw, k, a = ([ref[0, :, ls] for ls in lanes] for ref in (lw_ref, k_ref, a_ref))
    c = each(lambda t: _dot(tri_incl, t), lw)
    ctot = each(lambda t: jnp.sum(t, axis=0, keepdims=True), lw)
    beta = each(lambda p, q: p * q, kk, a)
    einv = each(lambda t: jnp.exp(-t), c)
    lhs = each(lambda kk_, r_, c_, lw_: bf(jnp.concatenate([bd(-kk_ * jnp.exp(c_ - lw_)), bd(r_ * jnp.exp(c_))], axis=0)),
               kk, r, c, lw)
    rhs = each(lambda k_, b_, e_: bf(jnp.concatenate([bd(k_ * e_), bd(b_ * e_)], axis=0)), k, beta, einv)
    m = each(lambda p, q: _mm1(p, q, NT), lhs, rhs)
    g = each(lambda p, h: _mm1(p, bf(h), NT), lhs, ht)
    v_b = each(lambda t: bf(bd(t)), v)
    x = each(lambda g_, m_, v_: g_[:n] + _mm1(bf(jnp.where(before, m_[:n, :n], 0.0)), v_), g, m, v_b)
    lb = each(lambda m_: jnp.where(before, m_[:n, n:], 0.0), m)
    ld = each(lambda t: jnp.where(diag_blk, t, 0.0), lb)
    lo = each(lambda p, q: p - q, lb, ld)
    xd = each(lambda t: eye + t, ld)
    lp = each(bf, ld)
    p = 2
    while p < S:
        lp = each(lambda t: bf(_mm1(t, t)), lp)
        xd = each(lambda x_, l_: x_ + _mm1(l_, bf(x_)), xd, lp)
        p *= 2
    xd_b = each(bf, xd)
    n_b = each(lambda x_, l_: bf(_mm1(x_, bf(l_))), xd_b, lo)
    y = each(lambda x_, t: _mm1(x_, bf(t)), xd_b, x)
    y = each(lambda y_, n_: y_ + _mm1(n_, bf(y_)), y, n_b)
    p = 2
    while p < nb:
        n_b = each(lambda t: bf(_mm1(t, t)), n_b)
        y = each(lambda y_, n_: y_ + _mm1(n_, bf(y_)), y, n_b)
        p *= 2
    u_b = each(bf, y)
    o = each(lambda g_, m_, v_, u_: (g_[n:] + _mm1(bf(jnp.where(incl, m_[n:, :n], 0.0)), v_))
             + _mm1(bf(jnp.where(incl, m_[n:, n:], 0.0)), u_), g, m, v_b, u_b)
    efin = each(lambda ct, c_: jnp.exp(ct - c_), ctot, c)
    ht_new = each(lambda h, ct, v_, u_, k_, b_, e_: (h * jnp.exp(ct) + _mm1(v_, bf(bd(k_ * e_)), TN))
                  + _mm1(u_, bf(bd(b_ * e_)), TN), ht, ctot, v_b, u_b, k, beta, efin)
    for grp in groups:
        o_ref[0, :, lanes[grp]] = collapse(o[grp])
        ht_ref[grp] = ht_new[grp]


def rwkv_scan(r, v, kk, lw2, k2, a2, n_ctx_rows):
    N, W = r.shape
    T = SCAN_T
    n_all, n_ctx = N // T, n_ctx_rows // T
    shared = pl.BlockSpec((T, W), lambda d, i: (_chunk_index(d, i, n_ctx, n_all), 0))
    per_dir = pl.BlockSpec((1, T, W), lambda d, i: (d, _chunk_index(d, i, n_ctx, n_all), 0))
    return pl.pallas_call(
        functools.partial(_rwkv_scan_kernel, head=RW_HEAD),
        grid=(2, n_all),
        in_specs=[shared, shared, shared, per_dir, per_dir, per_dir],
        out_specs=per_dir,
        out_shape=jax.ShapeDtypeStruct((2, N, W), jnp.float32),
        scratch_shapes=[pltpu.VMEM((W // GROUP_LANES, GROUP_LANES, GROUP_LANES), jnp.float32)],
        compiler_params=_params(2),
    )(r, v, kk, lw2, k2, a2)


def rope_tables(n_ctx, L):
    d_axis = RT_HEAD // 2
    half = d_axis // 2
    inv = ROPE_BASE ** (-jnp.arange(0, d_axis, 2, dtype=jnp.float32) / d_axis)
    t = jnp.arange(L)
    pos = jnp.stack([(t // GRID_W).astype(jnp.float32), (t % GRID_W).astype(jnp.float32)], axis=1)
    j = jnp.arange(RT_HEAD)
    ang = pos[:, j // d_axis] * inv[j % half][None, :]
    sgn = jnp.where((j % d_axis) < half, -1.0, 1.0)[None, :]
    cos = jnp.concatenate([jnp.ones((n_ctx, RT_HEAD), jnp.float32), jnp.cos(ang)], axis=0)
    sin = jnp.concatenate([jnp.zeros((n_ctx, RT_HEAD), jnp.float32), jnp.sin(ang) * sgn], axis=0)
    return jnp.tile(cos, (1, RT_HEADS)), jnp.tile(sin, (1, RT_HEADS))


def _rotate(x, cos, sin):
    G = x.shape[1]
    half = RT_HEAD // 4
    lane = lax.broadcasted_iota(jnp.int32, x.shape, 1)
    partner = jnp.where((lane % (2 * half)) < half, pltpu.roll(x, G - half, axis=1), pltpu.roll(x, half, axis=1))
    return x * cos + partner * sin


def _retention_scan_kernel(q_ref, k_ref, v_ref, cos_ref, sin_ref, lg_ref, o_ref, st_ref, *, head):
    T, G = SCAN_T, GROUP_LANES
    nh = G // head
    d = pl.program_id(0)
    sign = 1 - 2 * d

    @pl.when(pl.program_id(1) == 0)
    def _():
        st_ref[...] = jnp.zeros_like(st_ref)

    t = lax.broadcasted_iota(jnp.int32, (T, 1), 0)
    pos = (t + d * (T - 1 - 2 * t) + 1).astype(jnp.float32)
    bd, collapse, before, rt, cs = _stacking(T, T, nh, head, sign)
    incl = before | (rt == cs)
    bf = lambda x: x.astype(jnp.bfloat16)

    groups = range(q_ref.shape[1] // G)
    states = [st_ref[grp] for grp in groups]
    results = []
    for grp in groups:
        ls = slice(grp * G, (grp + 1) * G)
        cos, sin, lg = cos_ref[:, ls], sin_ref[:, ls], lg_ref[:, ls]
        q = _rotate(q_ref[:, ls], cos, sin)
        k = _rotate(k_ref[:, ls], cos, sin) * (head ** -0.5)
        c = pos * lg
        ctot = float(T) * lg
        q_b = bf(bd(q * jnp.exp(c)))
        k_til = k * jnp.exp(-c)
        k_hat = k * jnp.exp(ctot - c)
        v_b = bf(bd(v_ref[:, ls]))
        st = states[grp]
        scores = jnp.where(incl, _mm1(q_b, bf(bd(k_til)), NT), 0.0)
        results.append((ls, collapse(_mm1(q_b, bf(st), NT) + _mm1(bf(scores), v_b)),
                        st * jnp.exp(ctot) + _mm1(v_b, bf(bd(k_hat)), TN)))
    for grp, (ls, o, st_new) in enumerate(results):
        o_ref[0, :, ls] = o
        st_ref[grp] = st_new


def retention_scan(z, cos, sin, lg, n_ctx_rows):
    N = z.shape[0]
    T = SCAN_T
    n_all, n_ctx = N // T, n_ctx_rows // T
    zc = lambda col: pl.BlockSpec((T, CB), lambda d, i: (_chunk_index(d, i, n_ctx, n_all), COL_RT + col))
    tab = pl.BlockSpec((T, GROUP_W), lambda d, i: (_chunk_index(d, i, n_ctx, n_all), 0))
    return pl.pallas_call(
        functools.partial(_retention_scan_kernel, head=RT_HEAD),
        grid=(2, n_all),
        in_specs=[zc(0), zc(1), zc(2), tab, tab, pl.BlockSpec((1, GROUP_W), lambda d, i: (0, 0))],
        out_specs=pl.BlockSpec((1, T, GROUP_W), lambda d, i: (d, _chunk_index(d, i, n_ctx, n_all), 0)),
        out_shape=jax.ShapeDtypeStruct((2, N, GROUP_W), jnp.float32),
        scratch_shapes=[pltpu.VMEM((GROUP_W // GROUP_LANES, GROUP_LANES, GROUP_LANES), jnp.float32)],
        compiler_params=_params(2),
    )(z, z, z, cos, sin, lg)


def _gla_scan_kernel(q_ref, f_ref, i_ref, lb_ref, o_ref, st_ref, *, head):
    T, S = SCAN_T, HG_SUB
    W = q_ref.shape[-1]
    nh = W // head
    d = pl.program_id(0)
    sign = 1 - 2 * d

    @pl.when(pl.program_id(1) == 0)
    def _():
        st_ref[...] = jnp.zeros_like(st_ref)

    ti = lax.broadcasted_iota(jnp.int32, (S, S), 0)
    si = lax.broadcasted_iota(jnp.int32, (S, S), 1)
    tri_incl = jnp.where((ti - si) * sign >= 0, 1.0, 0.0)
    row = lax.broadcasted_iota(jnp.int32, (S, 1), 0)
    lb = lb_ref[0]

    states = [st_ref[h] for h in range(nh)]
    for j in range(T // S):
        jb = j + d * (T // S - 1 - 2 * j)
        rows = pl.ds(pl.multiple_of(jb * S, S), S)
        q = q_ref[rows, :]
        q = q * _sigmoid(q)
        v = i_ref[rows, :]
        gate = lb + (1.0 - lb) * _sigmoid(f_ref[rows, :])
        lf = jnp.log(jnp.maximum(gate, HG_MIN_GATE))
        k = 1.0 - gate
        b = _dot(tri_incl, lf)
        btot = jnp.sum(lf, axis=0, keepdims=True)
        qe = q * jnp.exp(b)
        ke = k * jnp.exp(btot - b)
        outs = []
        for h in range(nh):
            ls = slice(h * head, (h + 1) * head)
            qh, kh, vh, bh = q[:, ls], k[:, ls], v[:, ls], b[:, ls]
            st = states[h]
            o = _mm1(qe[:, ls].astype(jnp.bfloat16), st.astype(jnp.bfloat16), NT)
            for s in range(S):
                e = jnp.exp(jnp.minimum(bh - bh[s:s + 1, :], 0.0))
                a_s = jnp.sum(qh * kh[s:s + 1, :] * e, axis=-1, keepdims=True)
                a_s = jnp.where((row - s) * sign >= 0, a_s, 0.0)
                o = o + a_s * vh[s:s + 1, :]
            outs.append(o)
            states[h] = st * jnp.exp(btot[:, ls]) + _mm1(vh.astype(jnp.bfloat16), ke[:, ls].astype(jnp.bfloat16), TN)
        o_ref[0, rows, :] = jnp.concatenate(outs, axis=1)
    for h in range(nh):
        st_ref[h] = states[h]


def gla_scan(z, lb2, n_ctx_rows):
    N = z.shape[0]
    W = GROUP_W
    T = SCAN_T
    n_all, n_ctx = N // T, n_ctx_rows // T
    zc = lambda col: pl.BlockSpec((T, W), lambda d, i: (_chunk_index(d, i, n_ctx, n_all), COL_HG + col))
    zf = pl.BlockSpec((T, W), lambda d, i: (_chunk_index(d, i, n_ctx, n_all), COL_HG + 1 + d))
    return pl.pallas_call(
        functools.partial(_gla_scan_kernel, head=HG_HEAD),
        grid=(2, n_all),
        in_specs=[zc(0), zf, zc(3), pl.BlockSpec((1, 1, W), lambda d, i: (d, 0, 0))],
        out_specs=pl.BlockSpec((1, T, W), lambda d, i: (d, _chunk_index(d, i, n_ctx, n_all), 0)),
        out_shape=jax.ShapeDtypeStruct((2, N, W), jnp.float32),
        scratch_shapes=[pltpu.VMEM((W // HG_HEAD, HG_HEAD, HG_HEAD), jnp.float32)],
        compiler_params=_params(2),
    )(z, z, z, lb2)


def _outproj_kernel(hy_ref, rwo_ref, r_ref, k_ref, v_ref, rwg_ref, rto_ref, rtg_ref, hgo_ref, hgg_ref,
                    rk_ref, gng_ref, gnb_ref, hgn_ref, avg64_ref, avg128_ref,
                    w_ref, x_ref, gate_ref, g_ref, b_ref, o_ref, *, n_ctx):
    avg64, avg128 = avg64_ref[...], avg128_ref[...]

    def head_norm(o, avg, eps, centre):
        if centre:
            o = o - _mm2(o, avg)
        return o * lax.rsqrt(_mm2(o * o, avg) + eps)

    silu = lambda t: t * _sigmoid(t)
    y_rw = head_norm(rwo_ref[0] + rwo_ref[1], avg64, RW_GN_EPS, True) * gng_ref[...] + gnb_ref[...]
    bonus = (float(RW_HEAD) * _mm2(r_ref[...] * k_ref[...] * rk_ref[...], avg64)) * v_ref[...]
    y_rw = (y_rw + bonus) * _sigmoid(rwg_ref[...])
    y_rt = head_norm(rto_ref[0] + rto_ref[1], avg64, HEAD_NORM_EPS, True) * silu(rtg_ref[...])
    y_hg = head_norm(hgo_ref[0] + hgo_ref[1], avg128, HEAD_NORM_EPS, False) * hgn_ref[...] * silu(hgg_ref[...])
    y = None
    for m, ym in enumerate((hy_ref[...], y_rw, y_rt, y_hg)):
        part = jnp.dot(ym.astype(jnp.bfloat16), w_ref[m * GROUP_W:(m + 1) * GROUP_W, :],
                       preferred_element_type=jnp.float32)
        y = part if y is None else y + part
    is_ctx = _ctx_rows(x_ref.shape[0], n_ctx)
    r = ALPHA * x_ref[...] + _mod_row(gate_ref, is_ctx) * y
    o_ref[...] = _ln_rows(r) * g_ref[...] + b_ref[...]


def outproj_deepnorm(y_hy, rw_o2, rw_r, rw_k, rw_v, rw_g, rt_o2, hg_o2, z, r_k, gn_g, gn_b, hg_norm_g,
                     w_bf16, x, mod, g, b, n_ctx):
    m = x.shape[0]
    tm = OUT_TM
    one = pl.BlockSpec((tm, GROUP_W), lambda i: (i, 0))
    two = pl.BlockSpec((2, tm, GROUP_W), lambda i: (0, i, 0))
    zc = lambda col: pl.BlockSpec((tm, CB), lambda i: (i, col))
    row = pl.BlockSpec((tm, D_MODEL), lambda i: (i, 0))
    small = (r_k, gn_g, gn_b, hg_norm_g, _block_indicator(GROUP_W, RW_HEAD, 1.0 / RW_HEAD),
             _block_indicator(GROUP_W, HG_HEAD, 1.0 / HG_HEAD))
    return pl.pallas_call(
        functools.partial(_outproj_kernel, n_ctx=n_ctx),
        grid=(m // tm,),
        in_specs=[one, two, one, one, one, one, two, zc(COL_RT + 3), two, zc(COL_HG + 4)]
        + [_full(a) for a in small]
        + [_full(w_bf16), row, pl.BlockSpec((8, D_MODEL), lambda i: (0, 2)), _full(g), _full(b)],
        out_specs=row,
        out_shape=jax.ShapeDtypeStruct((m, D_MODEL), jnp.float32),
        compiler_params=_params(1),
    )(y_hy, rw_o2, rw_r, rw_k, rw_v, rw_g, rt_o2, z, hg_o2, z, *small, w_bf16, x, mod, g, b)


def _ffn_kernel(x_ref, sh_ref, sc_ref, w1_ref, w3_ref, w2_ref, gate_ref, g_ref, b_ref, o_ref, h_ref, acc_ref, *, n_ctx):
    j = pl.program_id(1)

    @pl.when(j == 0)
    def _():
        is_ctx = _ctx_rows(x_ref.shape[0], n_ctx)
        h = _ln_rows(x_ref[...]) * (1.0 + _mod_row(sc_ref, is_ctx)) + _mod_row(sh_ref, is_ctx)
        h_ref[...] = h.astype(jnp.bfloat16)
        acc_ref[...] = jnp.zeros_like(acc_ref)

    h = h_ref[...]
    a = jnp.dot(h, w1_ref[...], preferred_element_type=jnp.float32)
    u = jnp.dot(h, w3_ref[...], preferred_element_type=jnp.float32)
    s = (a * _sigmoid(a) * u).astype(jnp.bfloat16)
    acc_ref[...] += jnp.dot(s, w2_ref[...], preferred_element_type=jnp.float32)

    @pl.when(j == pl.num_programs(1) - 1)
    def _():
        is_ctx = _ctx_rows(x_ref.shape[0], n_ctx)
        r = ALPHA * x_ref[...] + _mod_row(gate_ref, is_ctx) * acc_ref[...]
        o_ref[...] = _ln_rows(r) * g_ref[...] + b_ref[...]


def ffn_deepnorm(x, mod, w1, w3, w2, g, b, n_ctx):
    m = x.shape[0]
    tm = PROJ_TM
    row = pl.BlockSpec((tm, D_MODEL), lambda i, j: (i, 0))
    modc = lambda c: pl.BlockSpec((8, D_MODEL), lambda i, j: (0, c))
    vec = pl.BlockSpec((1, D_MODEL), lambda i, j: (0, 0))
    return pl.pallas_call(
        functools.partial(_ffn_kernel, n_ctx=n_ctx),
        grid=(m // tm, FFN_HIDDEN // FFN_TF),
        in_specs=[row, modc(3), modc(4),
                  pl.BlockSpec((D_MODEL, FFN_TF), lambda i, j: (0, j)),
                  pl.BlockSpec((D_MODEL, FFN_TF), lambda i, j: (0, j)),
                  pl.BlockSpec((FFN_TF, D_MODEL), lambda i, j: (j, 0)),
                  modc(5), vec, vec],
        out_specs=row,
        out_shape=jax.ShapeDtypeStruct((m, D_MODEL), jnp.float32),
        scratch_shapes=[pltpu.VMEM((tm, D_MODEL), jnp.bfloat16),
                        pltpu.VMEM((tm, D_MODEL), jnp.float32)],
        compiler_params=_params(2),
    )(x, mod, mod, w1, w3, w2, mod, g, b)


def kernel(x, c, ctx, c_ctx, ada_w, ada_b, w_in, w_out, ln_g, ln_b, hy_conv_w, hy_conv_b, hy_w1, hy_b1, hy_w2, hy_b2, hy_w3, hy_b3, hy_freq, hy_bias, rw_mu, rw_w0, rw_w2, rw_a0, rw_a2, rw_k_k, rw_k_a, rw_r_k, rw_gn_g, rw_gn_b, hg_lb_raw, hg_norm_g, ffn_w1, ffn_w3, ffn_w2):
    L, n_ctx = x.shape[1], ctx.shape[1]
    sm = jax.nn.softmax(hg_lb_raw.astype(jnp.float32), axis=1)
    lower_bounds = jnp.cumsum(sm, axis=1) - sm[:, :1]
    log_gamma = jnp.log1p(-jnp.exp2(-5.0 - jnp.arange(RT_HEADS, dtype=jnp.float32)))
    lg = jnp.repeat(log_gamma, RT_HEAD)[None, :]
    cos, sin = rope_tables(n_ctx, L)
    fft_tabs = fft_tables(FFT_N1, 2 * L // FFT_N1)
    ctx_tabs = dense_dft_tables(n_ctx)

    c8 = jnp.zeros((8, D_MODEL), jnp.float32).at[0].set(c[0]).at[1].set(c_ctx)
    xs = jnp.concatenate([ctx[0], x[0]], axis=0)
    for l in range(DEPTH):
        with_ctx = l < DEPTH - 1
        mod = ada_modulation(c8, ada_w[l], ada_b[l][None, :])
        pad = jnp.zeros((D_MODEL, CB - 4 * RW_LORA), jnp.float32)
        split = (COL_RW * CB) + RW_REAL
        w_in_b = jnp.concatenate([w_in[l][:, :split], pad, w_in[l][:, split:]], axis=1).astype(jnp.bfloat16)
        w_out_b = w_out[l].astype(jnp.bfloat16)
        w1_b, w3_b, w2_b = (w[l].astype(jnp.bfloat16) for w in (ffn_w1, ffn_w3, ffn_w2))

        z = modulated_projection(xs, mod, w_in_b, n_ctx)

        u = hyena_conv3(z, hy_conv_w[l], hy_conv_b[l][None, :], n_ctx)
        hy_w = (hy_w1[l], hy_b1[l], hy_w2[l], hy_b2[l], hy_w3[l], hy_b3[l], hy_freq[l])
        filt, colsum = hyena_filter_bank(L, *hy_w)
        y_lat = hyena_long_conv_chain(u[n_ctx:], hy_bias[l], filt, colsum, fft_tabs)
        if with_ctx:
            filt_c, colsum_c = hyena_filter_bank(n_ctx, *hy_w)
            y_ctx = hyena_ctx(u[:n_ctx], filt_c, colsum_c, hy_bias[l], ctx_tabs)
        else:
            y_ctx = jnp.zeros((n_ctx, GROUP_W), jnp.float32)
        y_hy = jnp.concatenate([y_ctx, y_lat], axis=0)

        rw_r, rw_k, rw_v, rw_g, rw_kk, rw_lw, rw_kd, rw_a = rwkv_prep(
            z, rw_mu[l], rw_k_k[l], rw_k_a[l], rw_w0[l], rw_a0[l], rw_w2[l], rw_a2[l], n_ctx)
        rw_o2 = rwkv_scan(rw_r, rw_v, rw_kk, rw_lw, rw_kd, rw_a, n_ctx)
        rt_o2 = retention_scan(z, cos, sin, lg, n_ctx)
        hg_o2 = gla_scan(z, lower_bounds[:, l][:, None, :], n_ctx)

        xs = outproj_deepnorm(y_hy, rw_o2, rw_r, rw_k, rw_v, rw_g, rt_o2, hg_o2, z,
                              rw_r_k[l].reshape(1, GROUP_W), rw_gn_g[l][None, :], rw_gn_b[l][None, :],
                              hg_norm_g[l][None, :], w_out_b, xs, mod, ln_g[l, 0][None, :], ln_b[l, 0][None, :], n_ctx)
        xs = ffn_deepnorm(xs, mod, w1_b, w3_b, w2_b, ln_g[l, 1][None, :], ln_b[l, 1][None, :], n_ctx)
    return xs[n_ctx:][None]
```

```python
import functools
import math

import jax
import jax.numpy as jnp
from jax import lax
from jax.experimental import pallas as pl
from jax.experimental.pallas import tpu as pltpu

D_MODEL = 2048
DEPTH = 2
GRID_W = 64
N_MIXERS = 4
GROUP_W = D_MODEL // N_MIXERS
HY_ORDER = 2
HY_EMB = 33
HY_FAST_DECAY = 0.3
HY_SLOW_DECAY = 1.5
HY_TARGET = 1e-2
RW_HEAD = 64
RW_LORA = 96
RW_GN_EPS = 64e-5
RT_HEAD = 64
RT_HEADS = GROUP_W // RT_HEAD
ROPE_BASE = 10000.0
HG_HEAD = 128
HG_MIN_GATE = 1e-30
FFN_HIDDEN = 5632
ALPHA = (2 * DEPTH) ** 0.25
LN_EPS = 1e-6
HEAD_NORM_EPS = 1e-6

CB = 512
COL_HY = 0
COL_RW = 3
COL_RT = 8
COL_HG = 12
P_IN_PAD = 17 * CB
RW_REAL = 4 * GROUP_W + 4 * RW_LORA
PROJ_TM = 768
FFN_LAST_TM = 512
OUT_TM = 256
FFN_TF = 512
PREP_ROWS = 256
HALO = 8
VMEM_LIMIT = 56 * 1024 * 1024

HI = lax.Precision.HIGHEST
NN = ((1,), (0,))
NT = ((1,), (1,))
TN = ((0,), (0,))


def _params(n_axes):
    return pltpu.CompilerParams(dimension_semantics=("arbitrary",) * n_axes, vmem_limit_bytes=VMEM_LIMIT)


def _full(a):
    return pl.BlockSpec(a.shape, lambda *_: (0,) * a.ndim)


def _dot(a, b):
    return jnp.dot(a, b, precision=HI, preferred_element_type=jnp.float32)


def _split(x):
    hi = x.astype(jnp.bfloat16)
    lo = (x - hi.astype(jnp.float32)).astype(jnp.bfloat16)
    return hi, lo


def _mm3(a, b, dims=NN):
    d = lambda p, q: lax.dot_general(p, q, (dims, ((), ())), preferred_element_type=jnp.float32)
    return d(a[0], b[0]) + (d(a[0], b[1]) + d(a[1], b[0]))


def _mm1(a, b, dims=NN):
    return lax.dot_general(a, b, (dims, ((), ())), preferred_element_type=jnp.float32)


def _mm2(x, b):
    hi, lo = _split(x)
    d = functools.partial(jnp.dot, preferred_element_type=jnp.float32)
    return d(hi, b) + d(lo, b)


def _ln_rows(x):
    mu = jnp.mean(x, axis=-1, keepdims=True)
    xc = x - mu
    var = jnp.mean(xc * xc, axis=-1, keepdims=True)
    return xc * lax.rsqrt(var + LN_EPS)


def _sigmoid(x):
    return 1.0 / (1.0 + jnp.exp(-x))


def _ctx_rows(tm, n_lat):
    return (pl.program_id(0) * tm + lax.broadcasted_iota(jnp.int32, (tm, 1), 0)) >= n_lat


def _mod_row(ref, is_ctx):
    return jnp.where(is_ctx, ref[1:2, :], ref[0:1, :])


def _block_indicator(width, head, value):
    i = jnp.arange(width) // head
    return jnp.where(i[:, None] == i[None, :], value, 0.0).astype(jnp.bfloat16)


def _ada_kernel(c_ref, w_ref, b_ref, o_ref):
    c = c_ref[...]
    h = c * _sigmoid(c)
    o_ref[...] = jnp.dot(h.astype(jnp.bfloat16), w_ref[...].astype(jnp.bfloat16),
                         preferred_element_type=jnp.float32) + b_ref[...]


def ada_modulation(c8, w, b):
    n = w.shape[1]
    tn = 1024
    return pl.pallas_call(
        _ada_kernel,
        grid=(n // tn,),
        in_specs=[pl.BlockSpec((8, D_MODEL), lambda j: (0, 0)),
                  pl.BlockSpec((D_MODEL, tn), lambda j: (0, j)),
                  pl.BlockSpec((1, tn), lambda j: (0, j))],
        out_specs=pl.BlockSpec((8, tn), lambda j: (0, j)),
        out_shape=jax.ShapeDtypeStruct((8, n), jnp.float32),
        compiler_params=_params(1),
    )(c8, w, b)


def _proj_kernel(x_ref, sh_ref, sc_ref, w_ref, o_ref, h_ref, *, n_lat):
    @pl.when(pl.program_id(1) == 0)
    def _():
        is_ctx = _ctx_rows(x_ref.shape[0], n_lat)
        h = _ln_rows(x_ref[...]) * (1.0 + _mod_row(sc_ref, is_ctx)) + _mod_row(sh_ref, is_ctx)
        h_ref[...] = h.astype(jnp.bfloat16)

    o_ref[...] = jnp.dot(h_ref[...], w_ref[...], preferred_element_type=jnp.float32)


def modulated_projection(x, mod, w_bf16, n_lat):
    m = x.shape[0]
    n = w_bf16.shape[1]
    tm = PROJ_TM
    return pl.pallas_call(
        functools.partial(_proj_kernel, n_lat=n_lat),
        grid=(m // tm, n // CB),
        in_specs=[pl.BlockSpec((tm, D_MODEL), lambda i, j: (i, 0)),
                  pl.BlockSpec((8, D_MODEL), lambda i, j: (0, 0)),
                  pl.BlockSpec((8, D_MODEL), lambda i, j: (0, 1)),
                  pl.BlockSpec((D_MODEL, CB), lambda i, j: (0, j))],
        out_specs=pl.BlockSpec((tm, CB), lambda i, j: (i, j)),
        out_shape=jax.ShapeDtypeStruct((m, n), jnp.float32),
        scratch_shapes=[pltpu.VMEM((tm, D_MODEL), jnp.bfloat16)],
        compiler_params=_params(2),
    )(x, mod, mod, w_bf16)


def _halo_specs(tp, n_rows, col):
    per = tp // HALO
    last = n_rows // HALO - 1
    main = pl.BlockSpec((tp, CB), lambda i, *_: (i, col(*_) if callable(col) else col))
    prev = pl.BlockSpec((HALO, CB), lambda i, *_: (jnp.maximum(i * per - 1, 0), col(*_) if callable(col) else col))
    nxt = pl.BlockSpec((HALO, CB), lambda i, *_: (jnp.minimum((i + 1) * per, last), col(*_) if callable(col) else col))
    return [main, prev, nxt]


def _neighbours(x, prev, nxt, n_lat, n_rows):
    tp = x.shape[0]
    loc = lax.broadcasted_iota(jnp.int32, (tp, 1), 0)
    row = pl.program_id(0) * tp + loc
    xp = jnp.where(loc == 0, prev[HALO - 1:HALO, :], pltpu.roll(x, 1, axis=0))
    xp = jnp.where((row == 0) | (row == n_lat), 0.0, xp)
    xn = jnp.where(loc == tp - 1, nxt[0:1, :], pltpu.roll(x, tp - 1, axis=0))
    xn = jnp.where((row == n_lat - 1) | (row == n_rows - 1), 0.0, xn)
    return xp, xn


def _conv3_kernel(z_ref, zp_ref, zn_ref, w_ref, b_ref, o_ref, *, n_lat, n_rows):
    x = z_ref[...]
    xp, xn = _neighbours(x, zp_ref[...], zn_ref[...], n_lat, n_rows)
    w = w_ref[...]
    o_ref[...] = xp * w[0:1, :] + x * w[1:2, :] + xn * w[2:3, :] + b_ref[...]


def hyena_conv3(z, w, b, n_lat):
    n_rows = z.shape[0]
    tp = PREP_ROWS
    ncol = w.shape[1] // CB
    return pl.pallas_call(
        functools.partial(_conv3_kernel, n_lat=n_lat, n_rows=n_rows),
        grid=(n_rows // tp, ncol),
        in_specs=_halo_specs(tp, n_rows, lambda c: COL_HY + c)
        + [pl.BlockSpec((3, CB), lambda i, c: (0, c)), pl.BlockSpec((1, CB), lambda i, c: (0, c))],
        out_specs=pl.BlockSpec((tp, CB), lambda i, c: (i, c)),
        out_shape=jax.ShapeDtypeStruct((n_rows, w.shape[1]), jnp.float32),
        compiler_params=_params(2),
    )(z, z, z, w, b)


HY_EMB_PAD = 40
HY_FILTER_ROWS = 512


def hyena_features(L):
    t = jnp.linspace(0.0, 1.0, L, dtype=jnp.float32)[:, None]
    n_bands = (HY_EMB - 1) // 2
    f = jnp.linspace(1e-4, n_bands - 1, n_bands, dtype=jnp.float32)[None, :]
    ang = (2.0 * math.pi / L) * jnp.arange(L, dtype=jnp.float32)[:, None] * f
    z = jnp.concatenate([t, jnp.cos(ang), -jnp.sin(ang)], -1)
    return jnp.pad(z, ((0, 0), (0, HY_EMB_PAD - HY_EMB)))


def _filter_kernel(z_ref, w1_ref, b1_ref, w2_ref, b2_ref, w3_ref, b3_ref, fr_ref, dl_ref, h_ref, s_ref):
    i = pl.program_id(0)
    z = z_ref[...]
    fr = fr_ref[...]
    h = jnp.sin(fr * (_dot(z, w1_ref[...]) + b1_ref[...]))
    h = jnp.sin(fr * (_dot(h, w2_ref[...]) + b2_ref[...]))
    h = _mm1(h.astype(jnp.bfloat16), w3_ref[...].astype(jnp.bfloat16)) + b3_ref[...]
    win = jnp.exp(-z[:, 0:1] * dl_ref[...])
    h = h * jnp.concatenate([win] * (h.shape[1] // win.shape[1]), axis=1)

    @pl.when(i == 0)
    def _():
        s_ref[...] = jnp.zeros_like(s_ref)

    s_ref[...] += jnp.sum(jnp.abs(h), axis=0, keepdims=True)
    row = lax.broadcasted_iota(jnp.int32, h.shape, 0) + i * h.shape[0]
    col = lax.broadcasted_iota(jnp.int32, h.shape, 1)
    neg = (col // GROUP_W) % 2 == 1
    h_ref[...] = jnp.where(neg & (row == 0), 0.0, h)


def hyena_filter_bank(L, w1, b1, w2, b2, w3, b3, freq):
    z = hyena_features(L)
    w1p = jnp.pad(w1, ((0, HY_EMB_PAD - HY_EMB), (0, 0)))
    max_decay = math.log(HY_TARGET) / HY_FAST_DECAY
    min_decay = math.log(HY_TARGET) / HY_SLOW_DECAY
    deltas = jnp.abs(jnp.linspace(min_decay, max_decay, GROUP_W, dtype=jnp.float32))[None, :]
    n = w3.shape[1]
    tr = min(L, HY_FILTER_ROWS)
    args = (z, w1p, b1[None, :], w2, b2[None, :], w3, b3[None, :], freq[None, :], deltas)
    return pl.pallas_call(
        _filter_kernel,
        grid=(L // tr,),
        in_specs=[pl.BlockSpec((tr, HY_EMB_PAD), lambda i: (i, 0))] + [_full(a) for a in args[1:]],
        out_specs=[pl.BlockSpec((tr, n), lambda i: (i, 0)), pl.BlockSpec((1, n), lambda i: (0, 0))],
        out_shape=[jax.ShapeDtypeStruct((L, n), jnp.float32), jax.ShapeDtypeStruct((1, n), jnp.float32)],
        compiler_params=_params(1),
    )(*args)


FFT_NB = 8
FFT_CB = 256
FFT_N1 = 128


def fft_tables(n1, n2):
    N = n1 * n2
    nk = -(-(n2 // 2 + 1) // FFT_NB) * FFT_NB
    a = jnp.arange(n1, dtype=jnp.float32)[:, None, None]
    k2 = jnp.arange(nk, dtype=jnp.float32)[None, :, None]
    b = jnp.arange(n2 // 2, dtype=jnp.float32)[None, None, :]
    ph = (jnp.mod(a * k2, float(N)) / N + jnp.mod(b * k2, float(n2)) / n2) * (-2.0 * math.pi)
    g = jnp.concatenate([jnp.cos(ph), jnp.sin(ph)], axis=1)
    kk = jnp.arange(nk)
    weight = jnp.where((kk == 0) | (kk == n2 // 2), 1.0, jnp.where(kk < n2 // 2, 2.0, 0.0))
    ginv = jnp.transpose(g, (0, 2, 1)) * (jnp.tile(weight, 2) / N)
    k1 = jnp.arange(n1, dtype=jnp.float32)[:, None]
    aa = jnp.arange(n1, dtype=jnp.float32)[None, :]
    f = jnp.mod(k1 * aa, float(n1)) * (-2.0 * math.pi / n1)
    fr, fi = jnp.cos(f), jnp.sin(f)
    ff = jnp.concatenate([jnp.concatenate([fr, -fi], axis=1), jnp.concatenate([fi, fr], axis=1)], axis=0)
    bf = lambda t: t.astype(jnp.bfloat16)
    return {"g": bf(g), "ginv": bf(ginv), "ff": bf(ff), "fft": bf(ff.T)}


def _stage_a_kernel(u_ref, g_ref, o_ref):
    for j in range(FFT_NB):
        o_ref[j] = _mm1(g_ref[j], u_ref[:, j, :].astype(jnp.bfloat16))


def fft_stage_a(u3, nb, col, ncol, g):
    n1 = u3.shape[1]
    rows = g.shape[1]
    gspec = pl.BlockSpec((FFT_NB, rows, nb), lambda c, i: (i, 0, 0))
    return pl.pallas_call(
        _stage_a_kernel,
        grid=(ncol, n1 // FFT_NB),
        in_specs=[pl.BlockSpec((nb, FFT_NB, CB), lambda c, i: (0, i, col + c)), gspec],
        out_specs=pl.BlockSpec((FFT_NB, rows, CB), lambda c, i: (i, 0, c)),
        out_shape=jax.ShapeDtypeStruct((n1, rows, ncol * CB), jnp.float32),
        compiler_params=_params(2),
    )(u3, g)


def _stage_b_filter_kernel(re_ref, im_ref, f_ref, o_ref):
    ff = f_ref[...]
    for j in range(FFT_NB):
        o_ref[j] = _mm1(ff, jnp.concatenate([re_ref[:, j, :], im_ref[:, j, :]], axis=0).astype(jnp.bfloat16))


def fft_stage_b_filter(t1, ff):
    n1, n2x2, C = t1.shape
    n2 = n2x2 // 2
    blk = lambda off: pl.BlockSpec((n1, FFT_NB, FFT_CB), lambda c, i: (0, i + off, c))
    mat = pl.BlockSpec((2 * n1, 2 * n1), lambda c, i: (0, 0))
    return pl.pallas_call(
        _stage_b_filter_kernel,
        grid=(C // FFT_CB, n2 // FFT_NB),
        in_specs=[blk(0), blk(n2 // FFT_NB), mat],
        out_specs=pl.BlockSpec((FFT_NB, 2 * n1, FFT_CB), lambda c, i: (i, 0, c)),
        out_shape=jax.ShapeDtypeStruct((n2, 2 * n1, C), jnp.float32),
        compiler_params=_params(2),
    )(t1, t1, ff)


def _filter_spectrum(p, q, s, n1):
    return (p[:n1] + q[:n1]) * s, (p[n1:] - q[n1:]) * s


def _stage_b_conv_kernel(re_ref, im_ref, p_ref, q_ref, s_ref, f_ref, ft_ref, ore_ref, oim_ref):
    n1 = re_ref.shape[0]
    s = s_ref[...]
    ff, fft_ = f_ref[...], ft_ref[...]
    bf = lambda t: t.astype(jnp.bfloat16)
    for j in range(FFT_NB):
        x = _mm1(ff, bf(jnp.concatenate([re_ref[:, j, :], im_ref[:, j, :]], axis=0)))
        hr, hi = _filter_spectrum(p_ref[j], q_ref[j], s, n1)
        xr, xi = x[:n1], x[n1:]
        z = _mm1(fft_, bf(jnp.concatenate([xr * hr - xi * hi, xr * hi + xi * hr], axis=0)))
        ore_ref[:, j, :] = z[:n1]
        oim_ref[:, j, :] = z[n1:]


def fft_stage_b_conv(t1, spec, col_p, col_q, inv_norm, ff, fft_):
    n1, n2x2, C = t1.shape
    n2 = n2x2 // 2
    ncb = C // FFT_CB
    blk = lambda off: pl.BlockSpec((n1, FFT_NB, FFT_CB), lambda c, i: (0, i + off, c))
    mat = pl.BlockSpec((2 * n1, 2 * n1), lambda c, i: (0, 0))
    sp = lambda col: pl.BlockSpec((FFT_NB, 2 * n1, FFT_CB), lambda c, i: (i, 0, col * ncb + c))
    return pl.pallas_call(
        _stage_b_conv_kernel,
        grid=(ncb, n2 // FFT_NB),
        in_specs=[blk(0), blk(n2 // FFT_NB), sp(col_p), sp(col_q), pl.BlockSpec((1, FFT_CB), lambda c, i: (0, c)),
                  mat, mat],
        out_specs=[blk(0), blk(0)],
        out_shape=[jax.ShapeDtypeStruct((n1, n2, C), jnp.float32)] * 2,
        compiler_params=_params(2),
    )(t1, t1, spec, spec, inv_norm, ff, fft_)


def _stage_a_inv_kernel(re_ref, im_ref, g_ref, u_ref, gate_ref, bias_ref, o_ref):
    for j in range(FFT_NB):
        y = _mm1(g_ref[j], jnp.concatenate([re_ref[j], im_ref[j]], axis=0).astype(jnp.bfloat16))
        o_ref[:, j, :] = gate_ref[:, j, :] * (y + u_ref[:, j, :] * bias_ref[...])


def fft_stage_a_inv(t2re, t2im, ginv, nb, u3, u_col, gate3, gate_col, bias):
    n1, nk, C = t2re.shape
    tb = pl.BlockSpec((FFT_NB, nk, CB), lambda c, i: (i, 0, c))
    gb = pl.BlockSpec((FFT_NB, nb, 2 * nk), lambda c, i: (i, 0, 0))
    ub = lambda col: pl.BlockSpec((nb, FFT_NB, CB), lambda c, i: (0, i, col + c))
    return pl.pallas_call(
        _stage_a_inv_kernel,
        grid=(C // CB, n1 // FFT_NB),
        in_specs=[tb, tb, gb, ub(u_col), ub(gate_col), pl.BlockSpec((1, CB), lambda c, i: (0, c))],
        out_specs=ub(0),
        out_shape=jax.ShapeDtypeStruct((nb, n1, C), jnp.float32),
        compiler_params=_params(2),
    )(t2re, t2im, ginv, u3, gate3, bias)


def hyena_long_conv_chain(u, L, biases, filt, colsum, tabs):
    C = GROUP_W
    n1 = FFT_N1
    n2 = 2 * L // n1
    nb = n2 // 2
    spec = fft_stage_b_filter(fft_stage_a(filt.reshape(nb, n1, filt.shape[1]), nb, 0, filt.shape[1] // CB, tabs["g"]),
                              tabs["ff"])
    s4 = colsum.reshape(HY_ORDER, 2, C)
    inv_norm = 1.0 / (s4[:, 0] + s4[:, 1])
    u3 = u.reshape(u.shape[0] // n1, n1, u.shape[1])
    y3, y_col = u3, 0
    for n in range(HY_ORDER):
        t1 = fft_stage_a(y3, nb, y_col, 1, tabs["g"])
        t2re, t2im = fft_stage_b_conv(t1, spec, 2 * n, 2 * n + 1, inv_norm[n][None, :], tabs["ff"], tabs["fft"])
        y3 = fft_stage_a_inv(t2re, t2im, tabs["ginv"], nb, y3, y_col, u3, n + 1, biases[n][None, :])
        y_col = 0
    return y3.reshape(L, C)


def dense_dft_tables(n):
    N = 2 * n
    k = jnp.arange(N, dtype=jnp.float32)[:, None]
    t = jnp.arange(n, dtype=jnp.float32)[None, :]
    ph = jnp.mod(k * t, float(N)) * (2.0 * math.pi / N)
    fd = jnp.concatenate([jnp.cos(ph), -jnp.sin(ph)], axis=0)
    return fd.astype(jnp.bfloat16), (fd.T / N).astype(jnp.bfloat16)


def _hyena_ctx_kernel(u_ref, h_ref, s_ref, bias_ref, fd_ref, ft_ref, o_ref):
    C = GROUP_W
    fd, ft = fd_ref[...], ft_ref[...]
    K = fd.shape[0] // 2
    bf = lambda t: t.astype(jnp.bfloat16)
    hs = _mm1(fd, bf(h_ref[...]))
    s = s_ref[...]
    y = u_ref[:, 0:C]
    for n in range(HY_ORDER):
        cp, cq = 2 * n * C, (2 * n + 1) * C
        inv = 1.0 / (s[:, cp:cp + C] + s[:, cq:cq + C])
        hr, hi = _filter_spectrum(hs[:, cp:cp + C], hs[:, cq:cq + C], inv, K)
        x = _mm1(fd, bf(y))
        xr, xi = x[:K], x[K:]
        conv = _mm1(ft, bf(jnp.concatenate([xr * hr - xi * hi, xr * hi + xi * hr], axis=0)))
        y = u_ref[:, (n + 1) * C:(n + 2) * C] * (conv + y * bias_ref[n:n + 1, :])
    o_ref[...] = y


def hyena_ctx(u, filt, colsum, biases, tabs):
    args = (u, filt, colsum, biases, tabs[0], tabs[1])
    return pl.pallas_call(
        _hyena_ctx_kernel,
        in_specs=[_full(a) for a in args],
        out_specs=pl.BlockSpec((u.shape[0], GROUP_W), lambda: (0, 0)),
        out_shape=jax.ShapeDtypeStruct((u.shape[0], GROUP_W), jnp.float32),
        compiler_params=pltpu.CompilerParams(vmem_limit_bytes=VMEM_LIMIT),
    )(*args)


SCAN_T = 64
GROUP_LANES = 256
RW_SUB = 16
HG_SUB = 16
LOG2_E = 1.4426950408889634


def _chunk_index(d, i, n_ctx, n_all):
    fwd = jnp.where(i < n_ctx, n_all - n_ctx + i, i - n_ctx)
    return jnp.where(d == 0, fwd, n_all - 1 - i)


def _stacking(T, S, nh, head, sign):
    G = nh * head
    nb = T // S
    n = nh * T
    rr = lax.broadcasted_iota(jnp.int32, (n, G), 0)
    same = ((rr // S) % nh) == (lax.broadcasted_iota(jnp.int32, (n, G), 1) // head)

    def bd(x):
        pieces = []
        for i in range(nb):
            pieces += [x[i * S:(i + 1) * S]] * nh
        return jnp.where(same, jnp.concatenate(pieces, axis=0), 0.0)

    def collapse(o):
        outs = []
        for i in range(nb):
            acc = o[i * nh * S:i * nh * S + S]
            for h in range(1, nh):
                acc = acc + o[i * nh * S + h * S:i * nh * S + (h + 1) * S]
            outs.append(acc)
        return jnp.concatenate(outs, axis=0)

    rt = lax.broadcasted_iota(jnp.int32, (n, n), 0)
    cs = lax.broadcasted_iota(jnp.int32, (n, n), 1)
    t_r = (rt // (nh * S)) * S + rt % S
    t_c = (cs // (nh * S)) * S + cs % S
    same_h = ((rt // S) % nh) == ((cs // S) % nh)
    before = same_h & ((t_r - t_c) * sign > 0)
    return bd, collapse, before, rt, cs


def _softplus(x):
    return jnp.maximum(x, 0.0) + jnp.log(1.0 + jnp.exp(-jnp.abs(x)))


def _rwkv_prep_kernel(*refs, n_lat, n_rows):
    zrefs, rest = refs[:15], refs[15:]
    (mu_ref, kk_w_ref, ka_ref, w0_ref, a0_ref, w2h_ref, w2l_ref, a2h_ref, a2l_ref, ones_ref,
     r_ref, k_ref, v_ref, g_ref, kk_ref, lw_ref, kd_ref, a_ref) = rest
    slabs = []
    for c in range(5):
        x = zrefs[3 * c][...]
        xp, xn = _neighbours(x, zrefs[3 * c + 1][...], zrefs[3 * c + 2][...], n_lat, n_rows)
        slabs.append(x + (0.5 * (xp + xn) - x) * mu_ref[:, c * CB:(c + 1) * CB])
    r, k, v, g, lora = slabs
    r_ref[...], k_ref[...], v_ref[...], g_ref[...] = r, k, v, g
    kk = k * kk_w_ref[...]
    ss = _mm2(kk * kk, ones_ref[...])
    kk_ref[...] = kk * lax.rsqrt(jnp.maximum(ss, 1e-24))
    lora_t = _split(jnp.tanh(lora))
    lora_s = _split(lora)
    for d in range(2):
        w_log = -_softplus(-(w0_ref[d:d + 1, :] + _mm3(lora_t, (w2h_ref[d], w2l_ref[d])))) - 0.5
        lw_ref[d] = -jnp.exp(w_log)
        a = _sigmoid(a0_ref[d:d + 1, :] + _mm3(lora_s, (a2h_ref[d], a2l_ref[d])))
        a_ref[d] = a
        kd_ref[d] = k * (1.0 + (a - 1.0) * ka_ref[...])


def rwkv_prep(z, mu, k_k, k_a, w0, a0, w2, a2, n_lat):
    n_rows = z.shape[0]
    tp = PREP_ROWS
    mu_p = jnp.pad(mu, (0, 5 * CB - RW_REAL))[None, :]
    w2p = jnp.zeros((2, CB, GROUP_W), jnp.float32)
    a2p = jnp.zeros((2, CB, GROUP_W), jnp.float32)
    for d in range(2):
        w2p = w2p.at[d, d * RW_LORA:(d + 1) * RW_LORA].set(w2[d])
        a2p = a2p.at[d, (2 + d) * RW_LORA:(3 + d) * RW_LORA].set(a2[d])
    w2s, a2s = _split(w2p), _split(a2p)
    small = (mu_p, k_k[None, :], k_a[None, :], w0, a0, w2s[0], w2s[1], a2s[0], a2s[1],
             _block_indicator(GROUP_W, RW_HEAD, 1.0))
    zspecs = []
    for c in range(5):
        zspecs += _halo_specs(tp, n_rows, COL_RW + c)
    one = pl.BlockSpec((tp, CB), lambda i: (i, 0))
    two = pl.BlockSpec((2, tp, CB), lambda i: (0, i, 0))
    s1 = jax.ShapeDtypeStruct((n_rows, GROUP_W), jnp.float32)
    s2 = jax.ShapeDtypeStruct((2, n_rows, GROUP_W), jnp.float32)
    return pl.pallas_call(
        functools.partial(_rwkv_prep_kernel, n_lat=n_lat, n_rows=n_rows),
        grid=(n_rows // tp,),
        in_specs=zspecs + [_full(a) for a in small],
        out_specs=[one] * 5 + [two] * 3,
        out_shape=[s1] * 5 + [s2] * 3,
        compiler_params=_params(1),
    )(*([z] * 15), *small)


def _rwkv_scan_kernel(r_ref, v_ref, kk_ref, lw_ref, k_ref, a_ref, o_ref, ht_ref, *, head):
    T, G, S = SCAN_T, GROUP_LANES, RW_SUB
    nh = G // head
    nb = T // S
    n = nh * T
    d = pl.program_id(0)
    sign = 1 - 2 * d

    @pl.when(pl.program_id(1) == 0)
    def _():
        ht_ref[...] = jnp.zeros_like(ht_ref)

    ti = lax.broadcasted_iota(jnp.int32, (T, T), 0)
    si = lax.broadcasted_iota(jnp.int32, (T, T), 1)
    tri_incl = jnp.where((ti - si) * sign >= 0, 1.0, 0.0)
    bd, collapse, before, rt, cs = _stacking(T, S, nh, head, sign)
    incl = before | (rt == cs)
    diag_blk = (rt // (nh * S)) == (cs // (nh * S))
    eye = jnp.where(rt == cs, 1.0, 0.0)
    bf = lambda t: t.astype(jnp.bfloat16)

    groups = range(r_ref.shape[1] // G)
    lanes = [slice(grp * G, (grp + 1) * G) for grp in groups]
    each = lambda f, *seqs: [f(*xs) for xs in zip(*seqs)]
    ht = [ht_ref[grp] for grp in groups]
    r, v, kk = ([ref[:, ls] for ls in lanes] for ref in (r_ref, v_ref, kk_ref))
    lw, k, a = ([ref[0, :, ls] for ls in lanes] for ref in (lw_ref, k_ref, a_ref))
    c = each(lambda t: _dot(tri_incl, t), lw)
    ctot = each(lambda t: jnp.sum(t, axis=0, keepdims=True), lw)
    beta = each(lambda p, q: p * q, kk, a)
    einv = each(lambda t: jnp.exp(-t), c)
    lhs = each(lambda kk_, r_, c_, lw_: bf(jnp.concatenate([bd(-kk_ * jnp.exp(c_ - lw_)), bd(r_ * jnp.exp(c_))], axis=0)),
               kk, r, c, lw)
    rhs = each(lambda k_, b_, e_: bf(jnp.concatenate([bd(k_ * e_), bd(b_ * e_)], axis=0)), k, beta, einv)
    m = each(lambda p, q: _mm1(p, q, NT), lhs, rhs)
    g = each(lambda p, h: _mm1(p, bf(h), NT), lhs, ht)
    v_b = each(lambda t: bf(bd(t)), v)
    x = each(lambda g_, m_, v_: g_[:n] + _mm1(bf(jnp.where(before, m_[:n, :n], 0.0)), v_), g, m, v_b)
    lb = each(lambda m_: jnp.where(before, m_[:n, n:], 0.0), m)
    ld = each(lambda t: jnp.where(diag_blk, t, 0.0), lb)
    lo = each(lambda p, q: p - q, lb, ld)
    xd = each(lambda t: eye + t, ld)
    lp = each(bf, ld)
    p = 2
    while p < S:
        lp = each(lambda t: bf(_mm1(t, t)), lp)
        xd = each(lambda x_, l_: x_ + _mm1(l_, bf(x_)), xd, lp)
        p *= 2
    xd_b = each(bf, xd)
    n_b = each(lambda x_, l_: bf(_mm1(x_, bf(l_))), xd_b, lo)
    y = each(lambda x_, t: _mm1(x_, bf(t)), xd_b, x)
    y = each(lambda y_, n_: y_ + _mm1(n_, bf(y_)), y, n_b)
    p = 2
    while p < nb:
        n_b = each(lambda t: bf(_mm1(t, t)), n_b)
        y = each(lambda y_, n_: y_ + _mm1(n_, bf(y_)), y, n_b)
        p *= 2
    u_b = each(bf, y)
    o = each(lambda g_, m_, v_, u_: (g_[n:] + _mm1(bf(jnp.where(incl, m_[n:, :n], 0.0)), v_))
             + _mm1(bf(jnp.where(incl, m_[n:, n:], 0.0)), u_), g, m, v_b, u_b)
    efin = each(lambda ct, c_: jnp.exp(ct - c_), ctot, c)
    ht_new = each(lambda h, ct, v_, u_, k_, b_, e_: (h * jnp.exp(ct) + _mm1(v_, bf(bd(k_ * e_)), TN))
                  + _mm1(u_, bf(bd(b_ * e_)), TN), ht, ctot, v_b, u_b, k, beta, efin)
    for grp in groups:
        o_ref[0, :, lanes[grp]] = collapse(o[grp])
        ht_ref[grp] = ht_new[grp]


def rwkv_scan(r, v, kk, lw2, k2, a2, n_ctx_rows):
    N, W = r.shape
    T = SCAN_T
    n_all, n_ctx = N // T, n_ctx_rows // T
    shared = pl.BlockSpec((T, W), lambda d, i: (_chunk_index(d, i, n_ctx, n_all), 0))
    per_dir = pl.BlockSpec((1, T, W), lambda d, i: (d, _chunk_index(d, i, n_ctx, n_all), 0))
    return pl.pallas_call(
        functools.partial(_rwkv_scan_kernel, head=RW_HEAD),
        grid=(2, n_all),
        in_specs=[shared, shared, shared, per_dir, per_dir, per_dir],
        out_specs=per_dir,
        out_shape=jax.ShapeDtypeStruct((2, N, W), jnp.float32),
        scratch_shapes=[pltpu.VMEM((W // GROUP_LANES, GROUP_LANES, GROUP_LANES), jnp.float32)],
        compiler_params=_params(2),
    )(r, v, kk, lw2, k2, a2)


def rope_tables(n_ctx, L):
    d_axis = RT_HEAD // 2
    half = d_axis // 2
    inv = ROPE_BASE ** (-jnp.arange(0, d_axis, 2, dtype=jnp.float32) / d_axis)
    t = jnp.arange(L)
    pos = jnp.stack([(t // GRID_W).astype(jnp.float32), (t % GRID_W).astype(jnp.float32)], axis=1)
    j = jnp.arange(RT_HEAD)
    ang = pos[:, j // d_axis] * inv[j % half][None, :]
    sgn = jnp.where((j % d_axis) < half, -1.0, 1.0)[None, :]
    cos = jnp.concatenate([jnp.cos(ang), jnp.ones((n_ctx, RT_HEAD), jnp.float32)], axis=0)
    sin = jnp.concatenate([jnp.sin(ang) * sgn, jnp.zeros((n_ctx, RT_HEAD), jnp.float32)], axis=0)
    return jnp.tile(cos, (1, RT_HEADS)), jnp.tile(sin, (1, RT_HEADS))


def _rotate(x, cos, sin):
    G = x.shape[1]
    half = RT_HEAD // 4
    lane = lax.broadcasted_iota(jnp.int32, x.shape, 1)
    partner = jnp.where((lane % (2 * half)) < half, pltpu.roll(x, G - half, axis=1), pltpu.roll(x, half, axis=1))
    return x * cos + partner * sin


def _retention_scan_kernel(q_ref, k_ref, v_ref, cos_ref, sin_ref, lg_ref, o_ref, st_ref, *, head):
    T, G = SCAN_T, GROUP_LANES
    nh = G // head
    d = pl.program_id(0)
    sign = 1 - 2 * d

    @pl.when(pl.program_id(1) == 0)
    def _():
        st_ref[...] = jnp.zeros_like(st_ref)

    t = lax.broadcasted_iota(jnp.int32, (T, 1), 0)
    pos = (t + d * (T - 1 - 2 * t) + 1).astype(jnp.float32)
    bd, collapse, before, rt, cs = _stacking(T, T, nh, head, sign)
    incl = before | (rt == cs)
    bf = lambda x: x.astype(jnp.bfloat16)

    groups = range(q_ref.shape[1] // G)
    states = [st_ref[grp] for grp in groups]
    results = []
    for grp in groups:
        ls = slice(grp * G, (grp + 1) * G)
        cos, sin, lg = cos_ref[:, ls], sin_ref[:, ls], lg_ref[:, ls]
        q = _rotate(q_ref[:, ls], cos, sin)
        k = _rotate(k_ref[:, ls], cos, sin) * (head ** -0.5)
        c = pos * lg
        ctot = float(T) * lg
        q_b = bf(bd(q * jnp.exp(c)))
        k_til = k * jnp.exp(-c)
        k_hat = k * jnp.exp(ctot - c)
        v_b = bf(bd(v_ref[:, ls]))
        st = states[grp]
        scores = jnp.where(incl, _mm1(q_b, bf(bd(k_til)), NT), 0.0)
        results.append((ls, collapse(_mm1(q_b, bf(st), NT) + _mm1(bf(scores), v_b)),
                        st * jnp.exp(ctot) + _mm1(v_b, bf(bd(k_hat)), TN)))
    for grp, (ls, o, st_new) in enumerate(results):
        o_ref[0, :, ls] = o
        st_ref[grp] = st_new


def retention_scan(z, cos, sin, lg, n_ctx_rows):
    N = z.shape[0]
    T = SCAN_T
    n_all, n_ctx = N // T, n_ctx_rows // T
    zc = lambda col: pl.BlockSpec((T, CB), lambda d, i: (_chunk_index(d, i, n_ctx, n_all), COL_RT + col))
    tab = pl.BlockSpec((T, GROUP_W), lambda d, i: (_chunk_index(d, i, n_ctx, n_all), 0))
    return pl.pallas_call(
        functools.partial(_retention_scan_kernel, head=RT_HEAD),
        grid=(2, n_all),
        in_specs=[zc(0), zc(1), zc(2), tab, tab, pl.BlockSpec((1, GROUP_W), lambda d, i: (0, 0))],
        out_specs=pl.BlockSpec((1, T, GROUP_W), lambda d, i: (d, _chunk_index(d, i, n_ctx, n_all), 0)),
        out_shape=jax.ShapeDtypeStruct((2, N, GROUP_W), jnp.float32),
        scratch_shapes=[pltpu.VMEM((GROUP_W // GROUP_LANES, GROUP_LANES, GROUP_LANES), jnp.float32)],
        compiler_params=_params(2),
    )(z, z, z, cos, sin, lg)


def _gla_scan_kernel(q_ref, f_ref, i_ref, lb_ref, o_ref, st_ref, *, head):
    T, S = SCAN_T, HG_SUB
    W = q_ref.shape[-1]
    nh = W // head
    d = pl.program_id(0)
    sign = 1 - 2 * d

    @pl.when(pl.program_id(1) == 0)
    def _():
        st_ref[...] = jnp.zeros_like(st_ref)

    ti = lax.broadcasted_iota(jnp.int32, (S, S), 0)
    si = lax.broadcasted_iota(jnp.int32, (S, S), 1)
    tri_incl = jnp.where((ti - si) * sign >= 0, 1.0, 0.0)
    row = lax.broadcasted_iota(jnp.int32, (S, 1), 0)
    lb = lb_ref[0]

    states = [st_ref[h] for h in range(nh)]
    for j in range(T // S):
        jb = j + d * (T // S - 1 - 2 * j)
        rows = pl.ds(pl.multiple_of(jb * S, S), S)
        q = q_ref[rows, :]
        q = q * _sigmoid(q)
        v = i_ref[rows, :]
        gate = lb + (1.0 - lb) * _sigmoid(f_ref[rows, :])
        lf = jnp.log(jnp.maximum(gate, HG_MIN_GATE))
        k = 1.0 - gate
        b = _dot(tri_incl, lf)
        btot = jnp.sum(lf, axis=0, keepdims=True)
        qe = q * jnp.exp(b)
        ke = k * jnp.exp(btot - b)
        outs = []
        for h in range(nh):
            ls = slice(h * head, (h + 1) * head)
            qh, kh, vh = q[:, ls], k[:, ls], v[:, ls]
            bh = b[:, ls] * LOG2_E
            st = states[h]
            o = _mm1(qe[:, ls].astype(jnp.bfloat16), st.astype(jnp.bfloat16), NT)
            for s in range(S):
                e = jnp.exp2(bh - bh[s:s + 1, :])
                a_s = jnp.sum(qh * kh[s:s + 1, :] * e, axis=-1, keepdims=True)
                a_s = jnp.where((row - s) * sign >= 0, a_s, 0.0)
                o = o + a_s * vh[s:s + 1, :]
            outs.append(o)
            states[h] = st * jnp.exp(btot[:, ls]) + _mm1(vh.astype(jnp.bfloat16), ke[:, ls].astype(jnp.bfloat16), TN)
        o_ref[0, rows, :] = jnp.concatenate(outs, axis=1)
    for h in range(nh):
        st_ref[h] = states[h]


def gla_scan(z, lb2, n_ctx_rows):
    N = z.shape[0]
    W = GROUP_W
    T = SCAN_T
    n_all, n_ctx = N // T, n_ctx_rows // T
    zc = lambda col: pl.BlockSpec((T, W), lambda d, i: (_chunk_index(d, i, n_ctx, n_all), COL_HG + col))
    zf = pl.BlockSpec((T, W), lambda d, i: (_chunk_index(d, i, n_ctx, n_all), COL_HG + 1 + d))
    return pl.pallas_call(
        functools.partial(_gla_scan_kernel, head=HG_HEAD),
        grid=(2, n_all),
        in_specs=[zc(0), zf, zc(3), pl.BlockSpec((1, 1, W), lambda d, i: (d, 0, 0))],
        out_specs=pl.BlockSpec((1, T, W), lambda d, i: (d, _chunk_index(d, i, n_ctx, n_all), 0)),
        out_shape=jax.ShapeDtypeStruct((2, N, W), jnp.float32),
        scratch_shapes=[pltpu.VMEM((W // HG_HEAD, HG_HEAD, HG_HEAD), jnp.float32)],
        compiler_params=_params(2),
    )(z, z, z, lb2)


def _outproj_kernel(hy_ref, rwo_ref, r_ref, k_ref, v_ref, rwg_ref, rto_ref, rtg_ref, hgo_ref, hgg_ref,
                    rk_ref, gng_ref, gnb_ref, hgn_ref, avg64_ref, avg128_ref,
                    w_ref, x_ref, gate_ref, g_ref, b_ref, o_ref, *, n_lat):
    avg64, avg128 = avg64_ref[...], avg128_ref[...]

    def head_norm(o, avg, eps, centre):
        if centre:
            o = o - _mm2(o, avg)
        return o * lax.rsqrt(_mm2(o * o, avg) + eps)

    silu = lambda t: t * _sigmoid(t)
    y_rw = head_norm(rwo_ref[0] + rwo_ref[1], avg64, RW_GN_EPS, True) * gng_ref[...] + gnb_ref[...]
    bonus = (float(RW_HEAD) * _mm2(r_ref[...] * k_ref[...] * rk_ref[...], avg64)) * v_ref[...]
    y_rw = (y_rw + bonus) * _sigmoid(rwg_ref[...])
    y_rt = head_norm(rto_ref[0] + rto_ref[1], avg64, HEAD_NORM_EPS, True) * silu(rtg_ref[...])
    y_hg = head_norm(hgo_ref[0] + hgo_ref[1], avg128, HEAD_NORM_EPS, False) * hgn_ref[...] * silu(hgg_ref[...])
    y = None
    for m, ym in enumerate((hy_ref[...], y_rw, y_rt, y_hg)):
        part = jnp.dot(ym.astype(jnp.bfloat16), w_ref[m * GROUP_W:(m + 1) * GROUP_W, :],
                       preferred_element_type=jnp.float32)
        y = part if y is None else y + part
    is_ctx = _ctx_rows(x_ref.shape[0], n_lat)
    r = ALPHA * x_ref[...] + _mod_row(gate_ref, is_ctx) * y
    o_ref[...] = _ln_rows(r) * g_ref[...] + b_ref[...]


def outproj_deepnorm(y_hy, rw_o2, rw_r, rw_k, rw_v, rw_g, rt_o2, hg_o2, z, r_k, gn_g, gn_b, hg_norm_g,
                     w_bf16, x, mod, g, b, n_lat, m):
    tm = OUT_TM
    one = pl.BlockSpec((tm, GROUP_W), lambda i: (i, 0))
    two = pl.BlockSpec((2, tm, GROUP_W), lambda i: (0, i, 0))
    zc = lambda col: pl.BlockSpec((tm, CB), lambda i: (i, col))
    row = pl.BlockSpec((tm, D_MODEL), lambda i: (i, 0))
    small = (r_k, gn_g, gn_b, hg_norm_g, _block_indicator(GROUP_W, RW_HEAD, 1.0 / RW_HEAD),
             _block_indicator(GROUP_W, HG_HEAD, 1.0 / HG_HEAD))
    return pl.pallas_call(
        functools.partial(_outproj_kernel, n_lat=n_lat),
        grid=(m // tm,),
        in_specs=[one, two, one, one, one, one, two, zc(COL_RT + 3), two, zc(COL_HG + 4)]
        + [_full(a) for a in small]
        + [_full(w_bf16), row, pl.BlockSpec((8, D_MODEL), lambda i: (0, 2)), _full(g), _full(b)],
        out_specs=row,
        out_shape=jax.ShapeDtypeStruct((m, D_MODEL), jnp.float32),
        compiler_params=_params(1),
    )(y_hy, rw_o2, rw_r, rw_k, rw_v, rw_g, rt_o2, z, hg_o2, z, *small, w_bf16, x, mod, g, b)


def _ffn_kernel(x_ref, sh_ref, sc_ref, w1_ref, w3_ref, w2_ref, gate_ref, g_ref, b_ref, o_ref, h_ref, acc_ref, *, n_lat):
    j = pl.program_id(1)

    @pl.when(j == 0)
    def _():
        is_ctx = _ctx_rows(x_ref.shape[0], n_lat)
        h = _ln_rows(x_ref[...]) * (1.0 + _mod_row(sc_ref, is_ctx)) + _mod_row(sh_ref, is_ctx)
        h_ref[...] = h.astype(jnp.bfloat16)
        acc_ref[...] = jnp.zeros_like(acc_ref)

    h = h_ref[...]
    a = jnp.dot(h, w1_ref[...], preferred_element_type=jnp.float32)
    u = jnp.dot(h, w3_ref[...], preferred_element_type=jnp.float32)
    s = (a * _sigmoid(a) * u).astype(jnp.bfloat16)
    acc_ref[...] += jnp.dot(s, w2_ref[...], preferred_element_type=jnp.float32)

    @pl.when(j == pl.num_programs(1) - 1)
    def _():
        is_ctx = _ctx_rows(x_ref.shape[0], n_lat)
        r = ALPHA * x_ref[...] + _mod_row(gate_ref, is_ctx) * acc_ref[...]
        o_ref[...] = _ln_rows(r) * g_ref[...] + b_ref[...]


def ffn_deepnorm(x, mod, w1, w3, w2, g, b, n_lat, tm):
    m = x.shape[0]
    row = pl.BlockSpec((tm, D_MODEL), lambda i, j: (i, 0))
    modc = lambda c: pl.BlockSpec((8, D_MODEL), lambda i, j: (0, c))
    vec = pl.BlockSpec((1, D_MODEL), lambda i, j: (0, 0))
    return pl.pallas_call(
        functools.partial(_ffn_kernel, n_lat=n_lat),
        grid=(m // tm, FFN_HIDDEN // FFN_TF),
        in_specs=[row, modc(3), modc(4),
                  pl.BlockSpec((D_MODEL, FFN_TF), lambda i, j: (0, j)),
                  pl.BlockSpec((D_MODEL, FFN_TF), lambda i, j: (0, j)),
                  pl.BlockSpec((FFN_TF, D_MODEL), lambda i, j: (j, 0)),
                  modc(5), vec, vec],
        out_specs=row,
        out_shape=jax.ShapeDtypeStruct((m, D_MODEL), jnp.float32),
        scratch_shapes=[pltpu.VMEM((tm, D_MODEL), jnp.bfloat16),
                        pltpu.VMEM((tm, D_MODEL), jnp.float32)],
        compiler_params=_params(2),
    )(x, mod, mod, w1, w3, w2, mod, g, b)


def kernel(x, c, ctx, c_ctx, ada_w, ada_b, w_in, w_out, ln_g, ln_b, hy_conv_w, hy_conv_b, hy_w1, hy_b1, hy_w2, hy_b2, hy_w3, hy_b3, hy_freq, hy_bias, rw_mu, rw_w0, rw_w2, rw_a0, rw_a2, rw_k_k, rw_k_a, rw_r_k, rw_gn_g, rw_gn_b, hg_lb_raw, hg_norm_g, ffn_w1, ffn_w3, ffn_w2):
    L, n_ctx = x.shape[1], ctx.shape[1]
    sm = jax.nn.softmax(hg_lb_raw.astype(jnp.float32), axis=1)
    lower_bounds = jnp.cumsum(sm, axis=1) - sm[:, :1]
    log_gamma = jnp.log1p(-jnp.exp2(-5.0 - jnp.arange(RT_HEADS, dtype=jnp.float32)))
    lg = jnp.repeat(log_gamma, RT_HEAD)[None, :]
    cos, sin = rope_tables(n_ctx, L)
    fft_tabs = fft_tables(FFT_N1, 2 * L // FFT_N1)
    ctx_tabs = dense_dft_tables(n_ctx)

    c8 = jnp.zeros((8, D_MODEL), jnp.float32).at[0].set(c[0]).at[1].set(c_ctx)
    xs = jnp.concatenate([x[0], ctx[0]], axis=0)
    n_rows = L + n_ctx
    for l in range(DEPTH):
        with_ctx = l < DEPTH - 1
        mod = ada_modulation(c8, ada_w[l], ada_b[l][None, :])
        pad = jnp.zeros((D_MODEL, CB - 4 * RW_LORA), jnp.float32)
        split = (COL_RW * CB) + RW_REAL
        w_in_b = jnp.concatenate([w_in[l][:, :split], pad, w_in[l][:, split:]], axis=1).astype(jnp.bfloat16)
        w_out_b = w_out[l].astype(jnp.bfloat16)
        w1_b, w3_b, w2_b = (w[l].astype(jnp.bfloat16) for w in (ffn_w1, ffn_w3, ffn_w2))

        z = modulated_projection(xs, mod, w_in_b, L)

        u = hyena_conv3(z, hy_conv_w[l], hy_conv_b[l][None, :], L)
        hy_w = (hy_w1[l], hy_b1[l], hy_w2[l], hy_b2[l], hy_w3[l], hy_b3[l], hy_freq[l])
        filt, colsum = hyena_filter_bank(L, *hy_w)
        y_lat = hyena_long_conv_chain(u, L, hy_bias[l], filt, colsum, fft_tabs)
        if with_ctx:
            filt_c, colsum_c = hyena_filter_bank(n_ctx, *hy_w)
            y_ctx = hyena_ctx(u[L:], filt_c, colsum_c, hy_bias[l], ctx_tabs)
        else:
            y_ctx = jnp.zeros((n_ctx, GROUP_W), jnp.float32)
        y_hy = jnp.concatenate([y_lat, y_ctx], axis=0)

        rw_r, rw_k, rw_v, rw_g, rw_kk, rw_lw, rw_kd, rw_a = rwkv_prep(
            z, rw_mu[l], rw_k_k[l], rw_k_a[l], rw_w0[l], rw_a0[l], rw_w2[l], rw_a2[l], L)
        rw_o2 = rwkv_scan(rw_r, rw_v, rw_kk, rw_lw, rw_kd, rw_a, n_ctx)
        rt_o2 = retention_scan(z, cos, sin, lg, n_ctx)
        hg_o2 = gla_scan(z, lower_bounds[:, l][:, None, :], n_ctx)

        xs = outproj_deepnorm(y_hy, rw_o2, rw_r, rw_k, rw_v, rw_g, rt_o2, hg_o2, z,
                              rw_r_k[l].reshape(1, GROUP_W), rw_gn_g[l][None, :], rw_gn_b[l][None, :],
                              hg_norm_g[l][None, :], w_out_b, xs, mod, ln_g[l, 0][None, :], ln_b[l, 0][None, :],
                              L, n_rows if with_ctx else L)
        xs = ffn_deepnorm(xs, mod, w1_b, w3_b, w2_b, ln_g[l, 1][None, :], ln_b[l, 1][None, :], L,
                          PROJ_TM if with_ctx else FFN_LAST_TM)
    return xs[None]
```

```python
import functools
import math

import jax
import jax.numpy as jnp
from jax import lax
from jax.experimental import pallas as pl
from jax.experimental.pallas import tpu as pltpu

D_MODEL = 2048
DEPTH = 2
GRID_W = 64
N_MIXERS = 4
GROUP_W = D_MODEL // N_MIXERS
HY_ORDER = 2
HY_EMB = 33
HY_FAST_DECAY = 0.3
HY_SLOW_DECAY = 1.5
HY_TARGET = 1e-2
RW_HEAD = 64
RW_LORA = 96
RW_GN_EPS = 64e-5
RT_HEAD = 64
RT_HEADS = GROUP_W // RT_HEAD
ROPE_BASE = 10000.0
HG_HEAD = 128
HG_MIN_GATE = 1e-30
FFN_HIDDEN = 5632
ALPHA = (2 * DEPTH) ** 0.25
LN_EPS = 1e-6
HEAD_NORM_EPS = 1e-6

LANES = 128
CB = 512
COL_HY = 0
COL_RW = 3
COL_RT = 8
COL_HG = 12
P_IN_PAD = 17 * CB
RW_REAL = 4 * GROUP_W + 4 * RW_LORA
PROJ_TM = 768
FFN_LAST_TM = 512
OUT_TM = 256
FFN_TF = 512
PREP_ROWS = 256
HALO = 8
VMEM_LIMIT = 56 * 1024 * 1024

HI = lax.Precision.HIGHEST
NN = ((1,), (0,))
NT = ((1,), (1,))
TN = ((0,), (0,))


def _params(n_axes):
    return pltpu.CompilerParams(dimension_semantics=("arbitrary",) * n_axes, vmem_limit_bytes=VMEM_LIMIT)


def _full(a):
    return pl.BlockSpec(a.shape, lambda *_: (0,) * a.ndim)


def _dot(a, b):
    return jnp.dot(a, b, precision=HI, preferred_element_type=jnp.float32)


def _split(x):
    hi = x.astype(jnp.bfloat16)
    lo = (x - hi.astype(jnp.float32)).astype(jnp.bfloat16)
    return hi, lo


def _mm3(a, b, dims=NN):
    d = lambda p, q: lax.dot_general(p, q, (dims, ((), ())), preferred_element_type=jnp.float32)
    return d(a[0], b[0]) + (d(a[0], b[1]) + d(a[1], b[0]))


def _mm1(a, b, dims=NN):
    return lax.dot_general(a, b, (dims, ((), ())), preferred_element_type=jnp.float32)


def _mm2(x, b):
    hi, lo = _split(x)
    d = functools.partial(jnp.dot, preferred_element_type=jnp.float32)
    return d(hi, b) + d(lo, b)


def _ln_rows(x):
    mu = jnp.mean(x, axis=-1, keepdims=True)
    xc = x - mu
    var = jnp.mean(xc * xc, axis=-1, keepdims=True)
    return xc * lax.rsqrt(var + LN_EPS)


def _sigmoid(x):
    return 1.0 / (1.0 + jnp.exp(-x))


def _ctx_rows(tm, n_lat):
    return (pl.program_id(0) * tm + lax.broadcasted_iota(jnp.int32, (tm, 1), 0)) >= n_lat


def _mod_row(ref, is_ctx):
    return jnp.where(is_ctx, ref[1:2, :], ref[0:1, :])


def _block_indicator(width, head, value):
    i = jnp.arange(width) // head
    return jnp.where(i[:, None] == i[None, :], value, 0.0).astype(jnp.bfloat16)


def _ada_kernel(c_ref, w_ref, b_ref, o_ref):
    c = c_ref[...]
    h = c * _sigmoid(c)
    o_ref[...] = jnp.dot(h.astype(jnp.bfloat16), w_ref[...].astype(jnp.bfloat16),
                         preferred_element_type=jnp.float32) + b_ref[...]


def ada_modulation(c8, w, b):
    n = w.shape[1]
    tn = 1024
    return pl.pallas_call(
        _ada_kernel,
        grid=(n // tn,),
        in_specs=[pl.BlockSpec((8, D_MODEL), lambda j: (0, 0)),
                  pl.BlockSpec((D_MODEL, tn), lambda j: (0, j)),
                  pl.BlockSpec((1, tn), lambda j: (0, j))],
        out_specs=pl.BlockSpec((8, tn), lambda j: (0, j)),
        out_shape=jax.ShapeDtypeStruct((8, n), jnp.float32),
        compiler_params=_params(1),
    )(c8, w, b)


W_SUB = CB // LANES
W_PAD_PIECE = (COL_RW * CB + RW_REAL) // LANES


def _proj_kernel(x_ref, sh_ref, sc_ref, *refs, n_lat):
    w_refs, (o_ref, h_ref) = refs[:W_SUB], refs[W_SUB:]
    j = pl.program_id(1)

    @pl.when(j == 0)
    def _():
        is_ctx = _ctx_rows(x_ref.shape[0], n_lat)
        h = _ln_rows(x_ref[...]) * (1.0 + _mod_row(sc_ref, is_ctx)) + _mod_row(sh_ref, is_ctx)
        h_ref[...] = h.astype(jnp.bfloat16)

    pieces = [w_refs[k][...] for k in range(W_SUB)]
    for k in range(W_SUB):
        if (W_PAD_PIECE - k) % W_SUB == 0:
            pieces[k] = jnp.where(j * W_SUB + k == W_PAD_PIECE, 0.0, pieces[k])
    w = jnp.concatenate(pieces, axis=1).astype(jnp.bfloat16)
    o_ref[...] = jnp.dot(h_ref[...], w, preferred_element_type=jnp.float32)


def modulated_projection(x, mod, w, n_lat):
    m = x.shape[0]
    tm = PROJ_TM

    def piece(k):
        def index(i, j):
            p = j * W_SUB + k
            return 0, jnp.where(p < W_PAD_PIECE, p, p - 1)
        return pl.BlockSpec((D_MODEL, LANES), index)

    return pl.pallas_call(
        functools.partial(_proj_kernel, n_lat=n_lat),
        grid=(m // tm, P_IN_PAD // CB),
        in_specs=[pl.BlockSpec((tm, D_MODEL), lambda i, j: (i, 0)),
                  pl.BlockSpec((8, D_MODEL), lambda i, j: (0, 0)),
                  pl.BlockSpec((8, D_MODEL), lambda i, j: (0, 1))] + [piece(k) for k in range(W_SUB)],
        out_specs=pl.BlockSpec((tm, CB), lambda i, j: (i, j)),
        out_shape=jax.ShapeDtypeStruct((m, P_IN_PAD), jnp.float32),
        scratch_shapes=[pltpu.VMEM((tm, D_MODEL), jnp.bfloat16)],
        compiler_params=_params(2),
    )(x, mod, mod, *([w] * W_SUB))


def _halo_specs(tp, n_rows, col):
    per = tp // HALO
    last = n_rows // HALO - 1
    main = pl.BlockSpec((tp, CB), lambda i, *_: (i, col(*_) if callable(col) else col))
    prev = pl.BlockSpec((HALO, CB), lambda i, *_: (jnp.maximum(i * per - 1, 0), col(*_) if callable(col) else col))
    nxt = pl.BlockSpec((HALO, CB), lambda i, *_: (jnp.minimum((i + 1) * per, last), col(*_) if callable(col) else col))
    return [main, prev, nxt]


def _neighbours(x, prev, nxt, n_lat, n_rows):
    tp = x.shape[0]
    loc = lax.broadcasted_iota(jnp.int32, (tp, 1), 0)
    row = pl.program_id(0) * tp + loc
    xp = jnp.where(loc == 0, prev[HALO - 1:HALO, :], pltpu.roll(x, 1, axis=0))
    xp = jnp.where((row == 0) | (row == n_lat), 0.0, xp)
    xn = jnp.where(loc == tp - 1, nxt[0:1, :], pltpu.roll(x, tp - 1, axis=0))
    xn = jnp.where((row == n_lat - 1) | (row == n_rows - 1), 0.0, xn)
    return xp, xn


def _conv3_kernel(z_ref, zp_ref, zn_ref, w_ref, b_ref, o_ref, *, n_lat, n_rows):
    x = z_ref[...]
    xp, xn = _neighbours(x, zp_ref[...], zn_ref[...], n_lat, n_rows)
    w = w_ref[...]
    o_ref[...] = xp * w[0:1, :] + x * w[1:2, :] + xn * w[2:3, :] + b_ref[...]


def hyena_conv3(z, w, b, n_lat):
    n_rows = z.shape[0]
    tp = PREP_ROWS
    ncol = w.shape[1] // CB
    return pl.pallas_call(
        functools.partial(_conv3_kernel, n_lat=n_lat, n_rows=n_rows),
        grid=(n_rows // tp, ncol),
        in_specs=_halo_specs(tp, n_rows, lambda c: COL_HY + c)
        + [pl.BlockSpec((3, CB), lambda i, c: (0, c)), pl.BlockSpec((1, CB), lambda i, c: (0, c))],
        out_specs=pl.BlockSpec((tp, CB), lambda i, c: (i, c)),
        out_shape=jax.ShapeDtypeStruct((n_rows, w.shape[1]), jnp.float32),
        compiler_params=_params(2),
    )(z, z, z, w, b)


HY_EMB_PAD = 40
HY_FILTER_ROWS = 512


def hyena_features(L):
    t = jnp.linspace(0.0, 1.0, L, dtype=jnp.float32)[:, None]
    n_bands = (HY_EMB - 1) // 2
    f = jnp.linspace(1e-4, n_bands - 1, n_bands, dtype=jnp.float32)[None, :]
    ang = (2.0 * math.pi / L) * jnp.arange(L, dtype=jnp.float32)[:, None] * f
    z = jnp.concatenate([t, jnp.cos(ang), -jnp.sin(ang)], -1)
    return jnp.pad(z, ((0, 0), (0, HY_EMB_PAD - HY_EMB)))


def _filter_kernel(z_ref, w1_ref, b1_ref, w2_ref, b2_ref, w3_ref, b3_ref, fr_ref, dl_ref, h_ref, s_ref):
    i = pl.program_id(0)
    z = z_ref[...]
    fr = fr_ref[...]
    h = jnp.sin(fr * (_dot(z, w1_ref[...]) + b1_ref[...]))
    h = jnp.sin(fr * (_dot(h, w2_ref[...]) + b2_ref[...]))
    h = _mm1(h.astype(jnp.bfloat16), w3_ref[...].astype(jnp.bfloat16)) + b3_ref[...]
    win = jnp.exp(-z[:, 0:1] * dl_ref[...])
    h = h * jnp.concatenate([win] * (h.shape[1] // win.shape[1]), axis=1)

    @pl.when(i == 0)
    def _():
        s_ref[...] = jnp.zeros_like(s_ref)

    s_ref[...] += jnp.sum(jnp.abs(h), axis=0, keepdims=True)
    row = lax.broadcasted_iota(jnp.int32, h.shape, 0) + i * h.shape[0]
    col = lax.broadcasted_iota(jnp.int32, h.shape, 1)
    neg = (col // GROUP_W) % 2 == 1
    h_ref[...] = jnp.where(neg & (row == 0), 0.0, h)


def hyena_filter_bank(L, w1, b1, w2, b2, w3, b3, freq):
    z = hyena_features(L)
    w1p = jnp.pad(w1, ((0, HY_EMB_PAD - HY_EMB), (0, 0)))
    max_decay = math.log(HY_TARGET) / HY_FAST_DECAY
    min_decay = math.log(HY_TARGET) / HY_SLOW_DECAY
    deltas = jnp.abs(jnp.linspace(min_decay, max_decay, GROUP_W, dtype=jnp.float32))[None, :]
    n = w3.shape[1]
    tr = min(L, HY_FILTER_ROWS)
    args = (z, w1p, b1[None, :], w2, b2[None, :], w3, b3[None, :], freq[None, :], deltas)
    return pl.pallas_call(
        _filter_kernel,
        grid=(L // tr,),
        in_specs=[pl.BlockSpec((tr, HY_EMB_PAD), lambda i: (i, 0))] + [_full(a) for a in args[1:]],
        out_specs=[pl.BlockSpec((tr, n), lambda i: (i, 0)), pl.BlockSpec((1, n), lambda i: (0, 0))],
        out_shape=[jax.ShapeDtypeStruct((L, n), jnp.float32), jax.ShapeDtypeStruct((1, n), jnp.float32)],
        compiler_params=_params(1),
    )(*args)


FFT_NB = 8
FFT_CB = 256
FFT_N1 = 128


def fft_tables(n1, n2):
    N = n1 * n2
    nk = -(-(n2 // 2 + 1) // FFT_NB) * FFT_NB
    a = jnp.arange(n1, dtype=jnp.float32)[:, None, None]
    k2 = jnp.arange(nk, dtype=jnp.float32)[None, :, None]
    b = jnp.arange(n2 // 2, dtype=jnp.float32)[None, None, :]
    ph = (jnp.mod(a * k2, float(N)) / N + jnp.mod(b * k2, float(n2)) / n2) * (-2.0 * math.pi)
    g = jnp.concatenate([jnp.cos(ph), jnp.sin(ph)], axis=1)
    kk = jnp.arange(nk)
    weight = jnp.where((kk == 0) | (kk == n2 // 2), 1.0, jnp.where(kk < n2 // 2, 2.0, 0.0))
    ginv = jnp.transpose(g, (0, 2, 1)) * (jnp.tile(weight, 2) / N)
    k1 = jnp.arange(n1, dtype=jnp.float32)[:, None]
    aa = jnp.arange(n1, dtype=jnp.float32)[None, :]
    f = jnp.mod(k1 * aa, float(n1)) * (-2.0 * math.pi / n1)
    fr, fi = jnp.cos(f), jnp.sin(f)
    ff = jnp.concatenate([jnp.concatenate([fr, -fi], axis=1), jnp.concatenate([fi, fr], axis=1)], axis=0)
    bf = lambda t: t.astype(jnp.bfloat16)
    return {"g": bf(g), "ginv": bf(ginv), "ff": bf(ff), "fft": bf(ff.T)}


def _stage_a_kernel(u_ref, g_ref, o_ref):
    for j in range(FFT_NB):
        o_ref[j] = _mm1(g_ref[j], u_ref[:, j, :].astype(jnp.bfloat16))


def fft_stage_a(u3, nb, col, ncol, g):
    n1 = u3.shape[1]
    rows = g.shape[1]
    gspec = pl.BlockSpec((FFT_NB, rows, nb), lambda c, i: (i, 0, 0))
    return pl.pallas_call(
        _stage_a_kernel,
        grid=(ncol, n1 // FFT_NB),
        in_specs=[pl.BlockSpec((nb, FFT_NB, CB), lambda c, i: (0, i, col + c)), gspec],
        out_specs=pl.BlockSpec((FFT_NB, rows, CB), lambda c, i: (i, 0, c)),
        out_shape=jax.ShapeDtypeStruct((n1, rows, ncol * CB), jnp.float32),
        compiler_params=_params(2),
    )(u3, g)


def _stage_b_filter_kernel(re_ref, im_ref, f_ref, o_ref):
    ff = f_ref[...]
    for j in range(FFT_NB):
        o_ref[j] = _mm1(ff, jnp.concatenate([re_ref[:, j, :], im_ref[:, j, :]], axis=0).astype(jnp.bfloat16))


def fft_stage_b_filter(t1, ff):
    n1, n2x2, C = t1.shape
    n2 = n2x2 // 2
    blk = lambda off: pl.BlockSpec((n1, FFT_NB, FFT_CB), lambda c, i: (0, i + off, c))
    mat = pl.BlockSpec((2 * n1, 2 * n1), lambda c, i: (0, 0))
    return pl.pallas_call(
        _stage_b_filter_kernel,
        grid=(C // FFT_CB, n2 // FFT_NB),
        in_specs=[blk(0), blk(n2 // FFT_NB), mat],
        out_specs=pl.BlockSpec((FFT_NB, 2 * n1, FFT_CB), lambda c, i: (i, 0, c)),
        out_shape=jax.ShapeDtypeStruct((n2, 2 * n1, C), jnp.float32),
        compiler_params=_params(2),
    )(t1, t1, ff)


def _filter_spectrum(p, q, s, n1):
    return (p[:n1] + q[:n1]) * s, (p[n1:] - q[n1:]) * s


def _stage_b_conv_kernel(re_ref, im_ref, p_ref, q_ref, s_ref, f_ref, ft_ref, ore_ref, oim_ref):
    n1 = re_ref.shape[0]
    s = s_ref[...]
    ff, fft_ = f_ref[...], ft_ref[...]
    bf = lambda t: t.astype(jnp.bfloat16)
    for j in range(FFT_NB):
        x = _mm1(ff, bf(jnp.concatenate([re_ref[:, j, :], im_ref[:, j, :]], axis=0)))
        hr, hi = _filter_spectrum(p_ref[j], q_ref[j], s, n1)
        xr, xi = x[:n1], x[n1:]
        z = _mm1(fft_, bf(jnp.concatenate([xr * hr - xi * hi, xr * hi + xi * hr], axis=0)))
        ore_ref[:, j, :] = z[:n1]
        oim_ref[:, j, :] = z[n1:]


def fft_stage_b_conv(t1, spec, col_p, col_q, inv_norm, ff, fft_):
    n1, n2x2, C = t1.shape
    n2 = n2x2 // 2
    ncb = C // FFT_CB
    blk = lambda off: pl.BlockSpec((n1, FFT_NB, FFT_CB), lambda c, i: (0, i + off, c))
    mat = pl.BlockSpec((2 * n1, 2 * n1), lambda c, i: (0, 0))
    sp = lambda col: pl.BlockSpec((FFT_NB, 2 * n1, FFT_CB), lambda c, i: (i, 0, col * ncb + c))
    return pl.pallas_call(
        _stage_b_conv_kernel,
        grid=(ncb, n2 // FFT_NB),
        in_specs=[blk(0), blk(n2 // FFT_NB), sp(col_p), sp(col_q), pl.BlockSpec((1, FFT_CB), lambda c, i: (0, c)),
                  mat, mat],
        out_specs=[blk(0), blk(0)],
        out_shape=[jax.ShapeDtypeStruct((n1, n2, C), jnp.float32)] * 2,
        compiler_params=_params(2),
    )(t1, t1, spec, spec, inv_norm, ff, fft_)


def _stage_a_inv_kernel(re_ref, im_ref, g_ref, u_ref, gate_ref, bias_ref, o_ref):
    for j in range(FFT_NB):
        y = _mm1(g_ref[j], jnp.concatenate([re_ref[j], im_ref[j]], axis=0).astype(jnp.bfloat16))
        o_ref[:, j, :] = gate_ref[:, j, :] * (y + u_ref[:, j, :] * bias_ref[...])


def fft_stage_a_inv(t2re, t2im, ginv, nb, u3, u_col, gate3, gate_col, bias):
    n1, nk, C = t2re.shape
    tb = pl.BlockSpec((FFT_NB, nk, CB), lambda c, i: (i, 0, c))
    gb = pl.BlockSpec((FFT_NB, nb, 2 * nk), lambda c, i: (i, 0, 0))
    ub = lambda col: pl.BlockSpec((nb, FFT_NB, CB), lambda c, i: (0, i, col + c))
    return pl.pallas_call(
        _stage_a_inv_kernel,
        grid=(C // CB, n1 // FFT_NB),
        in_specs=[tb, tb, gb, ub(u_col), ub(gate_col), pl.BlockSpec((1, CB), lambda c, i: (0, c))],
        out_specs=ub(0),
        out_shape=jax.ShapeDtypeStruct((nb, n1, C), jnp.float32),
        compiler_params=_params(2),
    )(t2re, t2im, ginv, u3, gate3, bias)


def hyena_long_conv_chain(u, L, biases, filt, colsum, tabs):
    C = GROUP_W
    n1 = FFT_N1
    n2 = 2 * L // n1
    nb = n2 // 2
    spec = fft_stage_b_filter(fft_stage_a(filt.reshape(nb, n1, filt.shape[1]), nb, 0, filt.shape[1] // CB, tabs["g"]),
                              tabs["ff"])
    s4 = colsum.reshape(HY_ORDER, 2, C)
    inv_norm = 1.0 / (s4[:, 0] + s4[:, 1])
    u3 = u.reshape(u.shape[0] // n1, n1, u.shape[1])
    y3, y_col = u3, 0
    for n in range(HY_ORDER):
        t1 = fft_stage_a(y3, nb, y_col, 1, tabs["g"])
        t2re, t2im = fft_stage_b_conv(t1, spec, 2 * n, 2 * n + 1, inv_norm[n][None, :], tabs["ff"], tabs["fft"])
        y3 = fft_stage_a_inv(t2re, t2im, tabs["ginv"], nb, y3, y_col, u3, n + 1, biases[n][None, :])
        y_col = 0
    return y3.reshape(L, C)


def dense_dft_tables(n):
    N = 2 * n
    k = jnp.arange(N, dtype=jnp.float32)[:, None]
    t = jnp.arange(n, dtype=jnp.float32)[None, :]
    ph = jnp.mod(k * t, float(N)) * (2.0 * math.pi / N)
    fd = jnp.concatenate([jnp.cos(ph), -jnp.sin(ph)], axis=0)
    return fd.astype(jnp.bfloat16), (fd.T / N).astype(jnp.bfloat16)


def _hyena_ctx_kernel(u_ref, h_ref, s_ref, bias_ref, fd_ref, ft_ref, o_ref):
    C = GROUP_W
    fd, ft = fd_ref[...], ft_ref[...]
    K = fd.shape[0] // 2
    bf = lambda t: t.astype(jnp.bfloat16)
    hs = _mm1(fd, bf(h_ref[...]))
    s = s_ref[...]
    y = u_ref[:, 0:C]
    for n in range(HY_ORDER):
        cp, cq = 2 * n * C, (2 * n + 1) * C
        inv = 1.0 / (s[:, cp:cp + C] + s[:, cq:cq + C])
        hr, hi = _filter_spectrum(hs[:, cp:cp + C], hs[:, cq:cq + C], inv, K)
        x = _mm1(fd, bf(y))
        xr, xi = x[:K], x[K:]
        conv = _mm1(ft, bf(jnp.concatenate([xr * hr - xi * hi, xr * hi + xi * hr], axis=0)))
        y = u_ref[:, (n + 1) * C:(n + 2) * C] * (conv + y * bias_ref[n:n + 1, :])
    o_ref[...] = y


def hyena_ctx(u, filt, colsum, biases, tabs):
    args = (u, filt, colsum, biases, tabs[0], tabs[1])
    return pl.pallas_call(
        _hyena_ctx_kernel,
        in_specs=[_full(a) for a in args],
        out_specs=pl.BlockSpec((u.shape[0], GROUP_W), lambda: (0, 0)),
        out_shape=jax.ShapeDtypeStruct((u.shape[0], GROUP_W), jnp.float32),
        compiler_params=pltpu.CompilerParams(vmem_limit_bytes=VMEM_LIMIT),
    )(*args)


SCAN_T = 64
GROUP_LANES = 256
RW_SUB = 16
HG_SUB = 16
LOG2_E = 1.4426950408889634


def _chunk_index(d, i, n_ctx, n_all):
    fwd = jnp.where(i < n_ctx, n_all - n_ctx + i, i - n_ctx)
    return jnp.where(d == 0, fwd, n_all - 1 - i)


def _stacking(T, S, nh, head, sign):
    G = nh * head
    nb = T // S
    n = nh * T
    rr = lax.broadcasted_iota(jnp.int32, (n, G), 0)
    same = ((rr // S) % nh) == (lax.broadcasted_iota(jnp.int32, (n, G), 1) // head)

    def bd(x):
        pieces = []
        for i in range(nb):
            pieces += [x[i * S:(i + 1) * S]] * nh
        return jnp.where(same, jnp.concatenate(pieces, axis=0), 0.0)

    def collapse(o):
        outs = []
        for i in range(nb):
            acc = o[i * nh * S:i * nh * S + S]
            for h in range(1, nh):
                acc = acc + o[i * nh * S + h * S:i * nh * S + (h + 1) * S]
            outs.append(acc)
        return jnp.concatenate(outs, axis=0)

    rt = lax.broadcasted_iota(jnp.int32, (n, n), 0)
    cs = lax.broadcasted_iota(jnp.int32, (n, n), 1)
    t_r = (rt // (nh * S)) * S + rt % S
    t_c = (cs // (nh * S)) * S + cs % S
    same_h = ((rt // S) % nh) == ((cs // S) % nh)
    before = same_h & ((t_r - t_c) * sign > 0)
    return bd, collapse, before, rt, cs


def _softplus(x):
    return jnp.maximum(x, 0.0) + jnp.log(1.0 + jnp.exp(-jnp.abs(x)))


def _rwkv_prep_kernel(*refs, n_lat, n_rows):
    zrefs, rest = refs[:15], refs[15:]
    (mu_ref, kk_w_ref, ka_ref, w0_ref, a0_ref, w2h_ref, w2l_ref, a2h_ref, a2l_ref, ones_ref,
     r_ref, k_ref, v_ref, g_ref, kk_ref, lw_ref, kd_ref, a_ref) = rest
    slabs = []
    for c in range(5):
        x = zrefs[3 * c][...]
        xp, xn = _neighbours(x, zrefs[3 * c + 1][...], zrefs[3 * c + 2][...], n_lat, n_rows)
        slabs.append(x + (0.5 * (xp + xn) - x) * mu_ref[:, c * CB:(c + 1) * CB])
    r, k, v, g, lora = slabs
    r_ref[...], k_ref[...], v_ref[...], g_ref[...] = r, k, v, g
    kk = k * kk_w_ref[...]
    ss = _mm2(kk * kk, ones_ref[...])
    kk_ref[...] = kk * lax.rsqrt(jnp.maximum(ss, 1e-24))
    lora_t = _split(jnp.tanh(lora))
    lora_s = _split(lora)
    for d in range(2):
        w_log = -_softplus(-(w0_ref[d:d + 1, :] + _mm3(lora_t, (w2h_ref[d], w2l_ref[d])))) - 0.5
        lw_ref[d] = -jnp.exp(w_log)
        a = _sigmoid(a0_ref[d:d + 1, :] + _mm3(lora_s, (a2h_ref[d], a2l_ref[d])))
        a_ref[d] = a
        kd_ref[d] = k * (1.0 + (a - 1.0) * ka_ref[...])


def rwkv_prep(z, mu, k_k, k_a, w0, a0, w2, a2, n_lat):
    n_rows = z.shape[0]
    tp = PREP_ROWS
    mu_p = jnp.pad(mu, (0, 5 * CB - RW_REAL))[None, :]
    w2p = jnp.zeros((2, CB, GROUP_W), jnp.float32)
    a2p = jnp.zeros((2, CB, GROUP_W), jnp.float32)
    for d in range(2):
        w2p = w2p.at[d, d * RW_LORA:(d + 1) * RW_LORA].set(w2[d])
        a2p = a2p.at[d, (2 + d) * RW_LORA:(3 + d) * RW_LORA].set(a2[d])
    w2s, a2s = _split(w2p), _split(a2p)
    small = (mu_p, k_k[None, :], k_a[None, :], w0, a0, w2s[0], w2s[1], a2s[0], a2s[1],
             _block_indicator(GROUP_W, RW_HEAD, 1.0))
    zspecs = []
    for c in range(5):
        zspecs += _halo_specs(tp, n_rows, COL_RW + c)
    one = pl.BlockSpec((tp, CB), lambda i: (i, 0))
    two = pl.BlockSpec((2, tp, CB), lambda i: (0, i, 0))
    s1 = jax.ShapeDtypeStruct((n_rows, GROUP_W), jnp.float32)
    s2 = jax.ShapeDtypeStruct((2, n_rows, GROUP_W), jnp.float32)
    return pl.pallas_call(
        functools.partial(_rwkv_prep_kernel, n_lat=n_lat, n_rows=n_rows),
        grid=(n_rows // tp,),
        in_specs=zspecs + [_full(a) for a in small],
        out_specs=[one] * 5 + [two] * 3,
        out_shape=[s1] * 5 + [s2] * 3,
        compiler_params=_params(1),
    )(*([z] * 15), *small)


def _rwkv_scan_kernel(*refs, head):
    in_f, in_b, (of_ref, ob_ref, ht_ref) = refs[:6], refs[6:12], refs[12:]
    T, G, S = SCAN_T, GROUP_LANES, RW_SUB
    nh = G // head
    nb = T // S
    n = nh * T

    @pl.when(pl.program_id(0) == 0)
    def _():
        ht_ref[...] = jnp.zeros_like(ht_ref)

    ti = lax.broadcasted_iota(jnp.int32, (T, T), 0)
    si = lax.broadcasted_iota(jnp.int32, (T, T), 1)
    bf = lambda t: t.astype(jnp.bfloat16)
    each = lambda f, *seqs: [f(*xs) for xs in zip(*seqs)]
    n_grp = in_f[0].shape[1] // G
    lanes = [slice(grp * G, (grp + 1) * G) for grp in range(n_grp)]

    r, v, kk, lw, k, a, tri, before, incl, dest = ([] for _ in range(10))
    for (r_ref, v_ref, kk_ref, lw_ref, k_ref, a_ref), o_ref, sign in ((in_f, of_ref, 1), (in_b, ob_ref, -1)):
        bd, collapse, before_d, rt, cs = _stacking(T, S, nh, head, sign)
        for ls in lanes:
            r.append(r_ref[:, ls]), v.append(v_ref[:, ls]), kk.append(kk_ref[:, ls])
            lw.append(lw_ref[0, :, ls]), k.append(k_ref[0, :, ls]), a.append(a_ref[0, :, ls])
            tri.append(jnp.where((ti - si) * sign >= 0, 1.0, 0.0))
            before.append(before_d), incl.append(before_d | (rt == cs)), dest.append((o_ref, ls))
    diag_blk = (rt // (nh * S)) == (cs // (nh * S))
    eye = jnp.where(rt == cs, 1.0, 0.0)
    ht = [ht_ref[ch] for ch in range(len(r))]

    c = each(lambda tr, t: _dot(tr, t), tri, lw)
    ctot = each(lambda t: jnp.sum(t, axis=0, keepdims=True), lw)
    beta = each(lambda p, q: p * q, kk, a)
    einv = each(lambda t: jnp.exp(-t), c)
    lhs = each(lambda kk_, r_, c_, lw_: bf(jnp.concatenate([bd(-kk_ * jnp.exp(c_ - lw_)), bd(r_ * jnp.exp(c_))], axis=0)),
               kk, r, c, lw)
    rhs = each(lambda k_, b_, e_: bf(jnp.concatenate([bd(k_ * e_), bd(b_ * e_)], axis=0)), k, beta, einv)
    m = each(lambda p, q: _mm1(p, q, NT), lhs, rhs)
    g = each(lambda p, h: _mm1(p, bf(h), NT), lhs, ht)
    v_b = each(lambda t: bf(bd(t)), v)
    x = each(lambda g_, m_, v_, bm: g_[:n] + _mm1(bf(jnp.where(bm, m_[:n, :n], 0.0)), v_), g, m, v_b, before)
    lb = each(lambda m_, bm: jnp.where(bm, m_[:n, n:], 0.0), m, before)
    ld = each(lambda t: jnp.where(diag_blk, t, 0.0), lb)
    lo = each(lambda p, q: p - q, lb, ld)
    xd = each(lambda t: eye + t, ld)
    lp = each(bf, ld)
    p = 2
    while p < S:
        lp = each(lambda t: bf(_mm1(t, t)), lp)
        xd = each(lambda x_, l_: x_ + _mm1(l_, bf(x_)), xd, lp)
        p *= 2
    xd_b = each(bf, xd)
    n_b = each(lambda x_, l_: bf(_mm1(x_, bf(l_))), xd_b, lo)
    y = each(lambda x_, t: _mm1(x_, bf(t)), xd_b, x)
    y = each(lambda y_, n_: y_ + _mm1(n_, bf(y_)), y, n_b)
    p = 2
    while p < nb:
        n_b = each(lambda t: bf(_mm1(t, t)), n_b)
        y = each(lambda y_, n_: y_ + _mm1(n_, bf(y_)), y, n_b)
        p *= 2
    u_b = each(bf, y)
    o = each(lambda g_, m_, v_, u_, im: (g_[n:] + _mm1(bf(jnp.where(im, m_[n:, :n], 0.0)), v_))
             + _mm1(bf(jnp.where(im, m_[n:, n:], 0.0)), u_), g, m, v_b, u_b, incl)
    efin = each(lambda ct, c_: jnp.exp(ct - c_), ctot, c)
    ht_new = each(lambda h, ct, v_, u_, k_, b_, e_: (h * jnp.exp(ct) + _mm1(v_, bf(bd(k_ * e_)), TN))
                  + _mm1(u_, bf(bd(b_ * e_)), TN), ht, ctot, v_b, u_b, k, beta, efin)
    for ch, (o_ref, ls) in enumerate(dest):
        o_ref[:, ls] = collapse(o[ch])
        ht_ref[ch] = ht_new[ch]


def rwkv_scan(r, v, kk, lw2, k2, a2, n_ctx_rows):
    N, W = r.shape
    T = SCAN_T
    n_all, n_ctx = N // T, n_ctx_rows // T
    shared = lambda d: pl.BlockSpec((T, W), lambda i: (_chunk_index(d, i, n_ctx, n_all), 0))
    per_dir = lambda d: pl.BlockSpec((1, T, W), lambda i: (d, _chunk_index(d, i, n_ctx, n_all), 0))
    ins = lambda d: [shared(d)] * 3 + [per_dir(d)] * 3
    return pl.pallas_call(
        functools.partial(_rwkv_scan_kernel, head=RW_HEAD),
        grid=(n_all,),
        in_specs=ins(0) + ins(1),
        out_specs=[shared(0), shared(1)],
        out_shape=[jax.ShapeDtypeStruct((N, W), jnp.float32)] * 2,
        scratch_shapes=[pltpu.VMEM((2 * (W // GROUP_LANES), GROUP_LANES, GROUP_LANES), jnp.float32)],
        compiler_params=_params(1),
    )(*([r, v, kk, lw2, k2, a2] * 2))


def rope_tables(n_ctx, L):
    d_axis = RT_HEAD // 2
    half = d_axis // 2
    inv = ROPE_BASE ** (-jnp.arange(0, d_axis, 2, dtype=jnp.float32) / d_axis)
    t = jnp.arange(L)
    pos = jnp.stack([(t // GRID_W).astype(jnp.float32), (t % GRID_W).astype(jnp.float32)], axis=1)
    j = jnp.arange(RT_HEAD)
    ang = pos[:, j // d_axis] * inv[j % half][None, :]
    sgn = jnp.where((j % d_axis) < half, -1.0, 1.0)[None, :]
    cos = jnp.concatenate([jnp.cos(ang), jnp.ones((n_ctx, RT_HEAD), jnp.float32)], axis=0)
    sin = jnp.concatenate([jnp.sin(ang) * sgn, jnp.zeros((n_ctx, RT_HEAD), jnp.float32)], axis=0)
    return jnp.tile(cos, (1, RT_HEADS)), jnp.tile(sin, (1, RT_HEADS))


def _rotate(x, cos, sin):
    G = x.shape[1]
    half = RT_HEAD // 4
    lane = lax.broadcasted_iota(jnp.int32, x.shape, 1)
    partner = jnp.where((lane % (2 * half)) < half, pltpu.roll(x, G - half, axis=1), pltpu.roll(x, half, axis=1))
    return x * cos + partner * sin


def _retention_scan_kernel(q_ref, k_ref, v_ref, cos_ref, sin_ref, lg_ref, o_ref, st_ref, *, head):
    T, G = SCAN_T, GROUP_LANES
    nh = G // head
    d = pl.program_id(0)
    sign = 1 - 2 * d

    @pl.when(pl.program_id(1) == 0)
    def _():
        st_ref[...] = jnp.zeros_like(st_ref)

    t = lax.broadcasted_iota(jnp.int32, (T, 1), 0)
    pos = (t + d * (T - 1 - 2 * t) + 1).astype(jnp.float32)
    bd, collapse, before, rt, cs = _stacking(T, T, nh, head, sign)
    incl = before | (rt == cs)
    bf = lambda x: x.astype(jnp.bfloat16)

    groups = range(q_ref.shape[1] // G)
    states = [st_ref[grp] for grp in groups]
    results = []
    for grp in groups:
        ls = slice(grp * G, (grp + 1) * G)
        cos, sin, lg = cos_ref[:, ls], sin_ref[:, ls], lg_ref[:, ls]
        q = _rotate(q_ref[:, ls], cos, sin)
        k = _rotate(k_ref[:, ls], cos, sin) * (head ** -0.5)
        c = pos * lg
        ctot = float(T) * lg
        q_b = bf(bd(q * jnp.exp(c)))
        k_til = k * jnp.exp(-c)
        k_hat = k * jnp.exp(ctot - c)
        v_b = bf(bd(v_ref[:, ls]))
        st = states[grp]
        scores = jnp.where(incl, _mm1(q_b, bf(bd(k_til)), NT), 0.0)
        results.append((ls, collapse(_mm1(q_b, bf(st), NT) + _mm1(bf(scores), v_b)),
                        st * jnp.exp(ctot) + _mm1(v_b, bf(bd(k_hat)), TN)))
    for grp, (ls, o, st_new) in enumerate(results):
        o_ref[0, :, ls] = o
        st_ref[grp] = st_new


def retention_scan(z, cos, sin, lg, n_ctx_rows):
    N = z.shape[0]
    T = SCAN_T
    n_all, n_ctx = N // T, n_ctx_rows // T
    zc = lambda col: pl.BlockSpec((T, CB), lambda d, i: (_chunk_index(d, i, n_ctx, n_all), COL_RT + col))
    tab = pl.BlockSpec((T, GROUP_W), lambda d, i: (_chunk_index(d, i, n_ctx, n_all), 0))
    return pl.pallas_call(
        functools.partial(_retention_scan_kernel, head=RT_HEAD),
        grid=(2, n_all),
        in_specs=[zc(0), zc(1), zc(2), tab, tab, pl.BlockSpec((1, GROUP_W), lambda d, i: (0, 0))],
        out_specs=pl.BlockSpec((1, T, GROUP_W), lambda d, i: (d, _chunk_index(d, i, n_ctx, n_all), 0)),
        out_shape=jax.ShapeDtypeStruct((2, N, GROUP_W), jnp.float32),
        scratch_shapes=[pltpu.VMEM((GROUP_W // GROUP_LANES, GROUP_LANES, GROUP_LANES), jnp.float32)],
        compiler_params=_params(2),
    )(z, z, z, cos, sin, lg)


def _gla_scan_kernel(q_ref, f_ref, i_ref, lb_ref, o_ref, st_ref, *, head):
    T, S = SCAN_T, HG_SUB
    W = q_ref.shape[-1]
    nh = W // head
    d = pl.program_id(0)
    sign = 1 - 2 * d

    @pl.when(pl.program_id(1) == 0)
    def _():
        st_ref[...] = jnp.zeros_like(st_ref)

    ti = lax.broadcasted_iota(jnp.int32, (S, S), 0)
    si = lax.broadcasted_iota(jnp.int32, (S, S), 1)
    tri_incl = jnp.where((ti - si) * sign >= 0, 1.0, 0.0)
    row = lax.broadcasted_iota(jnp.int32, (S, 1), 0)
    lb = lb_ref[0]

    states = [st_ref[h] for h in range(nh)]
    for j in range(T // S):
        jb = j + d * (T // S - 1 - 2 * j)
        rows = pl.ds(pl.multiple_of(jb * S, S), S)
        q = q_ref[rows, :]
        q = q * _sigmoid(q)
        v = i_ref[rows, :]
        gate = lb + (1.0 - lb) * _sigmoid(f_ref[rows, :])
        lf = jnp.log(jnp.maximum(gate, HG_MIN_GATE))
        k = 1.0 - gate
        b = _dot(tri_incl, lf)
        btot = jnp.sum(lf, axis=0, keepdims=True)
        qe = q * jnp.exp(b)
        ke = k * jnp.exp(btot - b)
        outs = []
        for h in range(nh):
            ls = slice(h * head, (h + 1) * head)
            qh, kh, vh = q[:, ls], k[:, ls], v[:, ls]
            bh = b[:, ls] * LOG2_E
            st = states[h]
            o = _mm1(qe[:, ls].astype(jnp.bfloat16), st.astype(jnp.bfloat16), NT)
            for s in range(S):
                e = jnp.exp2(bh - bh[s:s + 1, :])
                a_s = jnp.sum(qh * kh[s:s + 1, :] * e, axis=-1, keepdims=True)
                a_s = jnp.where((row - s) * sign >= 0, a_s, 0.0)
                o = o + a_s * vh[s:s + 1, :]
            outs.append(o)
            states[h] = st * jnp.exp(btot[:, ls]) + _mm1(vh.astype(jnp.bfloat16), ke[:, ls].astype(jnp.bfloat16), TN)
        o_ref[0, rows, :] = jnp.concatenate(outs, axis=1)
    for h in range(nh):
        st_ref[h] = states[h]


def gla_scan(z, lb2, n_ctx_rows):
    N = z.shape[0]
    W = GROUP_W
    T = SCAN_T
    n_all, n_ctx = N // T, n_ctx_rows // T
    zc = lambda col: pl.BlockSpec((T, W), lambda d, i: (_chunk_index(d, i, n_ctx, n_all), COL_HG + col))
    zf = pl.BlockSpec((T, W), lambda d, i: (_chunk_index(d, i, n_ctx, n_all), COL_HG + 1 + d))
    return pl.pallas_call(
        functools.partial(_gla_scan_kernel, head=HG_HEAD),
        grid=(2, n_all),
        in_specs=[zc(0), zf, zc(3), pl.BlockSpec((1, 1, W), lambda d, i: (d, 0, 0))],
        out_specs=pl.BlockSpec((1, T, W), lambda d, i: (d, _chunk_index(d, i, n_ctx, n_all), 0)),
        out_shape=jax.ShapeDtypeStruct((2, N, W), jnp.float32),
        scratch_shapes=[pltpu.VMEM((W // HG_HEAD, HG_HEAD, HG_HEAD), jnp.float32)],
        compiler_params=_params(2),
    )(z, z, z, lb2)


def _outproj_kernel(hy_ref, rwof_ref, rwob_ref, r_ref, k_ref, v_ref, rwg_ref, rto_ref, rtg_ref, hgo_ref, hgg_ref,
                    rk_ref, gng_ref, gnb_ref, hgn_ref, avg64_ref, avg128_ref,
                    w_ref, x_ref, gate_ref, g_ref, b_ref, o_ref, *, n_lat):
    avg64, avg128 = avg64_ref[...], avg128_ref[...]

    def head_norm(o, avg, eps, centre):
        if centre:
            o = o - _mm2(o, avg)
        return o * lax.rsqrt(_mm2(o * o, avg) + eps)

    silu = lambda t: t * _sigmoid(t)
    y_rw = head_norm(rwof_ref[...] + rwob_ref[...], avg64, RW_GN_EPS, True) * gng_ref[...] + gnb_ref[...]
    bonus = (float(RW_HEAD) * _mm2(r_ref[...] * k_ref[...] * rk_ref[...], avg64)) * v_ref[...]
    y_rw = (y_rw + bonus) * _sigmoid(rwg_ref[...])
    y_rt = head_norm(rto_ref[0] + rto_ref[1], avg64, HEAD_NORM_EPS, True) * silu(rtg_ref[...])
    y_hg = head_norm(hgo_ref[0] + hgo_ref[1], avg128, HEAD_NORM_EPS, False) * hgn_ref[...] * silu(hgg_ref[...])
    y = None
    for m, ym in enumerate((hy_ref[...], y_rw, y_rt, y_hg)):
        part = jnp.dot(ym.astype(jnp.bfloat16), w_ref[m * GROUP_W:(m + 1) * GROUP_W, :],
                       preferred_element_type=jnp.float32)
        y = part if y is None else y + part
    is_ctx = _ctx_rows(x_ref.shape[0], n_lat)
    r = ALPHA * x_ref[...] + _mod_row(gate_ref, is_ctx) * y
    o_ref[...] = _ln_rows(r) * g_ref[...] + b_ref[...]


def outproj_deepnorm(y_hy, rw_o2, rw_r, rw_k, rw_v, rw_g, rt_o2, hg_o2, z, r_k, gn_g, gn_b, hg_norm_g,
                     w_bf16, x, mod, g, b, n_lat, m):
    tm = OUT_TM
    one = pl.BlockSpec((tm, GROUP_W), lambda i: (i, 0))
    two = pl.BlockSpec((2, tm, GROUP_W), lambda i: (0, i, 0))
    zc = lambda col: pl.BlockSpec((tm, CB), lambda i: (i, col))
    row = pl.BlockSpec((tm, D_MODEL), lambda i: (i, 0))
    small = (r_k, gn_g, gn_b, hg_norm_g, _block_indicator(GROUP_W, RW_HEAD, 1.0 / RW_HEAD),
             _block_indicator(GROUP_W, HG_HEAD, 1.0 / HG_HEAD))
    return pl.pallas_call(
        functools.partial(_outproj_kernel, n_lat=n_lat),
        grid=(m // tm,),
        in_specs=[one, one, one, one, one, one, one, two, zc(COL_RT + 3), two, zc(COL_HG + 4)]
        + [_full(a) for a in small]
        + [_full(w_bf16), row, pl.BlockSpec((8, D_MODEL), lambda i: (0, 2)), _full(g), _full(b)],
        out_specs=row,
        out_shape=jax.ShapeDtypeStruct((m, D_MODEL), jnp.float32),
        compiler_params=_params(1),
    )(y_hy, rw_o2[0], rw_o2[1], rw_r, rw_k, rw_v, rw_g, rt_o2, z, hg_o2, z, *small, w_bf16, x, mod, g, b)


def _ffn_kernel(x_ref, sh_ref, sc_ref, w1_ref, w3_ref, w2_ref, gate_ref, g_ref, b_ref, o_ref, h_ref, acc_ref, *, n_lat):
    j = pl.program_id(1)

    @pl.when(j == 0)
    def _():
        is_ctx = _ctx_rows(x_ref.shape[0], n_lat)
        h = _ln_rows(x_ref[...]) * (1.0 + _mod_row(sc_ref, is_ctx)) + _mod_row(sh_ref, is_ctx)
        h_ref[...] = h.astype(jnp.bfloat16)
        acc_ref[...] = jnp.zeros_like(acc_ref)

    h = h_ref[...]
    a = jnp.dot(h, w1_ref[...], preferred_element_type=jnp.float32)
    u = jnp.dot(h, w3_ref[...], preferred_element_type=jnp.float32)
    s = (a * _sigmoid(a) * u).astype(jnp.bfloat16)
    acc_ref[...] += jnp.dot(s, w2_ref[...], preferred_element_type=jnp.float32)

    @pl.when(j == pl.num_programs(1) - 1)
    def _():
        is_ctx = _ctx_rows(x_ref.shape[0], n_lat)
        r = ALPHA * x_ref[...] + _mod_row(gate_ref, is_ctx) * acc_ref[...]
        o_ref[...] = _ln_rows(r) * g_ref[...] + b_ref[...]


def ffn_deepnorm(x, mod, w1, w3, w2, g, b, n_lat, tm):
    m = x.shape[0]
    row = pl.BlockSpec((tm, D_MODEL), lambda i, j: (i, 0))
    modc = lambda c: pl.BlockSpec((8, D_MODEL), lambda i, j: (0, c))
    vec = pl.BlockSpec((1, D_MODEL), lambda i, j: (0, 0))
    return pl.pallas_call(
        functools.partial(_ffn_kernel, n_lat=n_lat),
        grid=(m // tm, FFN_HIDDEN // FFN_TF),
        in_specs=[row, modc(3), modc(4),
                  pl.BlockSpec((D_MODEL, FFN_TF), lambda i, j: (0, j)),
                  pl.BlockSpec((D_MODEL, FFN_TF), lambda i, j: (0, j)),
                  pl.BlockSpec((FFN_TF, D_MODEL), lambda i, j: (j, 0)),
                  modc(5), vec, vec],
        out_specs=row,
        out_shape=jax.ShapeDtypeStruct((m, D_MODEL), jnp.float32),
        scratch_shapes=[pltpu.VMEM((tm, D_MODEL), jnp.bfloat16),
                        pltpu.VMEM((tm, D_MODEL), jnp.float32)],
        compiler_params=_params(2),
    )(x, mod, mod, w1, w3, w2, mod, g, b)


def kernel(x, c, ctx, c_ctx, ada_w, ada_b, w_in, w_out, ln_g, ln_b, hy_conv_w, hy_conv_b, hy_w1, hy_b1, hy_w2, hy_b2, hy_w3, hy_b3, hy_freq, hy_bias, rw_mu, rw_w0, rw_w2, rw_a0, rw_a2, rw_k_k, rw_k_a, rw_r_k, rw_gn_g, rw_gn_b, hg_lb_raw, hg_norm_g, ffn_w1, ffn_w3, ffn_w2):
    L, n_ctx = x.shape[1], ctx.shape[1]
    sm = jax.nn.softmax(hg_lb_raw.astype(jnp.float32), axis=1)
    lower_bounds = jnp.cumsum(sm, axis=1) - sm[:, :1]
    log_gamma = jnp.log1p(-jnp.exp2(-5.0 - jnp.arange(RT_HEADS, dtype=jnp.float32)))
    lg = jnp.repeat(log_gamma, RT_HEAD)[None, :]
    cos, sin = rope_tables(n_ctx, L)
    fft_tabs = fft_tables(FFT_N1, 2 * L // FFT_N1)
    ctx_tabs = dense_dft_tables(n_ctx)

    c8 = jnp.zeros((8, D_MODEL), jnp.float32).at[0].set(c[0]).at[1].set(c_ctx)
    xs = jnp.concatenate([x[0], ctx[0]], axis=0)
    n_rows = L + n_ctx
    for l in range(DEPTH):
        with_ctx = l < DEPTH - 1
        mod = ada_modulation(c8, ada_w[l], ada_b[l][None, :])
        w_out_b = w_out[l].astype(jnp.bfloat16)
        w1_b, w3_b, w2_b = (w[l].astype(jnp.bfloat16) for w in (ffn_w1, ffn_w3, ffn_w2))

        z = modulated_projection(xs, mod, w_in[l], L)

        u = hyena_conv3(z, hy_conv_w[l], hy_conv_b[l][None, :], L)
        hy_w = (hy_w1[l], hy_b1[l], hy_w2[l], hy_b2[l], hy_w3[l], hy_b3[l], hy_freq[l])
        filt, colsum = hyena_filter_bank(L, *hy_w)
        y_lat = hyena_long_conv_chain(u, L, hy_bias[l], filt, colsum, fft_tabs)
        if with_ctx:
            filt_c, colsum_c = hyena_filter_bank(n_ctx, *hy_w)
            y_ctx = hyena_ctx(u[L:], filt_c, colsum_c, hy_bias[l], ctx_tabs)
        else:
            y_ctx = jnp.zeros((n_ctx, GROUP_W), jnp.float32)
        y_hy = jnp.concatenate([y_lat, y_ctx], axis=0)

        rw_r, rw_k, rw_v, rw_g, rw_kk, rw_lw, rw_kd, rw_a = rwkv_prep(
            z, rw_mu[l], rw_k_k[l], rw_k_a[l], rw_w0[l], rw_a0[l], rw_w2[l], rw_a2[l], L)
        rw_o2 = rwkv_scan(rw_r, rw_v, rw_kk, rw_lw, rw_kd, rw_a, n_ctx)
        rt_o2 = retention_scan(z, cos, sin, lg, n_ctx)
        hg_o2 = gla_scan(z, lower_bounds[:, l][:, None, :], n_ctx)

        xs = outproj_deepnorm(y_hy, rw_o2, rw_r, rw_k, rw_v, rw_g, rt_o2, hg_o2, z,
                              rw_r_k[l].reshape(1, GROUP_W), rw_gn_g[l][None, :], rw_gn_b[l][None, :],
                              hg_norm_g[l][None, :], w_out_b, xs, mod, ln_g[l, 0][None, :], ln_b[l, 0][None, :],
                              L, n_rows if with_ctx else L)
        xs = ffn_deepnorm(xs, mod, w1_b, w3_b, w2_b, ln_g[l, 1][None, :], ln_b[l, 1][None, :], L,
                          PROJ_TM if with_ctx else FFN_LAST_TM)
    return xs[None]
```

```python
import functools
import math

import jax
import jax.numpy as jnp
from jax import lax
from jax.experimental import pallas as pl
from jax.experimental.pallas import tpu as pltpu

D_MODEL = 2048
DEPTH = 2
GRID_W = 64
N_MIXERS = 4
GROUP_W = D_MODEL // N_MIXERS
HY_ORDER = 2
HY_EMB = 33
HY_FAST_DECAY = 0.3
HY_SLOW_DECAY = 1.5
HY_TARGET = 1e-2
RW_HEAD = 64
RW_LORA = 96
RW_GN_EPS = 64e-5
RT_HEAD = 64
RT_HEADS = GROUP_W // RT_HEAD
ROPE_BASE = 10000.0
HG_HEAD = 128
HG_MIN_GATE = 1e-30
FFN_HIDDEN = 5632
ALPHA = (2 * DEPTH) ** 0.25
LN_EPS = 1e-6
HEAD_NORM_EPS = 1e-6

LANES = 128
CB = 512
COL_HY = 0
COL_RW = 3
COL_RT = 8
COL_HG = 12
P_IN_PAD = 17 * CB
RW_REAL = 4 * GROUP_W + 4 * RW_LORA
PROJ_TM = 1408
FFN_TM = 768
FFN_LAST_TM = 512
OUT_TM = 256
FFN_TF = 512
PREP_ROWS = 256
HALO = 8
VMEM_LIMIT = 56 * 1024 * 1024

HI = lax.Precision.HIGHEST
NN = ((1,), (0,))
NT = ((1,), (1,))
TN = ((0,), (0,))


def _params(n_axes):
    return pltpu.CompilerParams(dimension_semantics=("arbitrary",) * n_axes, vmem_limit_bytes=VMEM_LIMIT)


def _full(a):
    return pl.BlockSpec(a.shape, lambda *_: (0,) * a.ndim)


def _dot(a, b):
    return jnp.dot(a, b, precision=HI, preferred_element_type=jnp.float32)


def _split(x):
    hi = x.astype(jnp.bfloat16)
    lo = (x - hi.astype(jnp.float32)).astype(jnp.bfloat16)
    return hi, lo


def _mm3(a, b, dims=NN):
    d = lambda p, q: lax.dot_general(p, q, (dims, ((), ())), preferred_element_type=jnp.float32)
    return d(a[0], b[0]) + (d(a[0], b[1]) + d(a[1], b[0]))


def _mm1(a, b, dims=NN):
    return lax.dot_general(a, b, (dims, ((), ())), preferred_element_type=jnp.float32)


def _mm2(x, b):
    hi, lo = _split(x)
    d = functools.partial(jnp.dot, preferred_element_type=jnp.float32)
    return d(hi, b) + d(lo, b)


def _ln_rows(x):
    mu = jnp.mean(x, axis=-1, keepdims=True)
    xc = x - mu
    var = jnp.mean(xc * xc, axis=-1, keepdims=True)
    return xc * lax.rsqrt(var + LN_EPS)


def _sigmoid(x):
    return 1.0 / (1.0 + jnp.exp(-x))


def _ctx_rows(tm, n_lat):
    return (pl.program_id(0) * tm + lax.broadcasted_iota(jnp.int32, (tm, 1), 0)) >= n_lat


def _mod_row(ref, is_ctx):
    return jnp.where(is_ctx, ref[1:2, :], ref[0:1, :])


def _block_indicator(width, head, value):
    i = jnp.arange(width) // head
    return jnp.where(i[:, None] == i[None, :], value, 0.0).astype(jnp.bfloat16)


def _ada_kernel(c_ref, w_ref, b_ref, o_ref):
    c = c_ref[...]
    h = c * _sigmoid(c)
    o_ref[...] = jnp.dot(h.astype(jnp.bfloat16), w_ref[...].astype(jnp.bfloat16),
                         preferred_element_type=jnp.float32) + b_ref[...]


def ada_modulation(c8, w, layer, b):
    n = w.shape[2]
    tn = 1024
    return pl.pallas_call(
        _ada_kernel,
        grid=(n // tn,),
        in_specs=[pl.BlockSpec((8, D_MODEL), lambda j: (0, 0)),
                  pl.BlockSpec((None, D_MODEL, tn), lambda j: (layer, 0, j)),
                  pl.BlockSpec((1, tn), lambda j: (0, j))],
        out_specs=pl.BlockSpec((8, tn), lambda j: (0, j)),
        out_shape=jax.ShapeDtypeStruct((8, n), jnp.float32),
        compiler_params=_params(1),
    )(c8, w, b)


W_SUB = CB // LANES
W_PAD_PIECE = (COL_RW * CB + RW_REAL) // LANES


def _proj_kernel(x_ref, sh_ref, sc_ref, *refs, n_lat):
    w_refs, (o_ref, h_ref) = refs[:W_SUB], refs[W_SUB:]
    j = pl.program_id(1)

    @pl.when(j == 0)
    def _():
        is_ctx = _ctx_rows(x_ref.shape[0], n_lat)
        h = _ln_rows(x_ref[...]) * (1.0 + _mod_row(sc_ref, is_ctx)) + _mod_row(sh_ref, is_ctx)
        h_ref[...] = h.astype(jnp.bfloat16)

    pieces = [w_refs[k][...] for k in range(W_SUB)]
    for k in range(W_SUB):
        if (W_PAD_PIECE - k) % W_SUB == 0:
            pieces[k] = jnp.where(j * W_SUB + k == W_PAD_PIECE, 0.0, pieces[k])
    w = jnp.concatenate(pieces, axis=1).astype(jnp.bfloat16)
    o_ref[...] = jnp.dot(h_ref[...], w, preferred_element_type=jnp.float32)


def modulated_projection(x, mod, w, layer, n_lat):
    m = x.shape[0]
    tm = PROJ_TM

    def piece(k):
        def index(i, j):
            p = j * W_SUB + k
            return layer, 0, jnp.where(p < W_PAD_PIECE, p, p - 1)
        return pl.BlockSpec((None, D_MODEL, LANES), index)

    return pl.pallas_call(
        functools.partial(_proj_kernel, n_lat=n_lat),
        grid=(m // tm, P_IN_PAD // CB),
        in_specs=[pl.BlockSpec((tm, D_MODEL), lambda i, j: (i, 0)),
                  pl.BlockSpec((8, D_MODEL), lambda i, j: (0, 0)),
                  pl.BlockSpec((8, D_MODEL), lambda i, j: (0, 1))] + [piece(k) for k in range(W_SUB)],
        out_specs=pl.BlockSpec((tm, CB), lambda i, j: (i, j)),
        out_shape=jax.ShapeDtypeStruct((m, P_IN_PAD), jnp.float32),
        scratch_shapes=[pltpu.VMEM((tm, D_MODEL), jnp.bfloat16)],
        compiler_params=_params(2),
    )(x, mod, mod, *([w] * W_SUB))


def _halo_specs(tp, n_rows, col):
    per = tp // HALO
    last = n_rows // HALO - 1
    main = pl.BlockSpec((tp, CB), lambda i, *_: (i, col(*_) if callable(col) else col))
    prev = pl.BlockSpec((HALO, CB), lambda i, *_: (jnp.maximum(i * per - 1, 0), col(*_) if callable(col) else col))
    nxt = pl.BlockSpec((HALO, CB), lambda i, *_: (jnp.minimum((i + 1) * per, last), col(*_) if callable(col) else col))
    return [main, prev, nxt]


def _neighbours(x, prev, nxt, n_lat, n_rows):
    tp = x.shape[0]
    loc = lax.broadcasted_iota(jnp.int32, (tp, 1), 0)
    row = pl.program_id(0) * tp + loc
    xp = jnp.where(loc == 0, prev[HALO - 1:HALO, :], pltpu.roll(x, 1, axis=0))
    xp = jnp.where((row == 0) | (row == n_lat), 0.0, xp)
    xn = jnp.where(loc == tp - 1, nxt[0:1, :], pltpu.roll(x, tp - 1, axis=0))
    xn = jnp.where((row == n_lat - 1) | (row == n_rows - 1), 0.0, xn)
    return xp, xn


def _conv3_kernel(z_ref, zp_ref, zn_ref, w_ref, b_ref, o_ref, *, n_lat, n_rows):
    x = z_ref[...]
    xp, xn = _neighbours(x, zp_ref[...], zn_ref[...], n_lat, n_rows)
    w = w_ref[...]
    o_ref[...] = xp * w[0:1, :] + x * w[1:2, :] + xn * w[2:3, :] + b_ref[...]


def hyena_conv3(z, w, b, n_lat):
    n_rows = z.shape[0]
    tp = PREP_ROWS
    ncol = w.shape[1] // CB
    return pl.pallas_call(
        functools.partial(_conv3_kernel, n_lat=n_lat, n_rows=n_rows),
        grid=(n_rows // tp, ncol),
        in_specs=_halo_specs(tp, n_rows, lambda c: COL_HY + c)
        + [pl.BlockSpec((3, CB), lambda i, c: (0, c)), pl.BlockSpec((1, CB), lambda i, c: (0, c))],
        out_specs=pl.BlockSpec((tp, CB), lambda i, c: (i, c)),
        out_shape=jax.ShapeDtypeStruct((n_rows, w.shape[1]), jnp.float32),
        compiler_params=_params(2),
    )(z, z, z, w, b)


HY_EMB_PAD = 40
HY_FILTER_ROWS = 512


def hyena_features(L):
    t = jnp.linspace(0.0, 1.0, L, dtype=jnp.float32)[:, None]
    n_bands = (HY_EMB - 1) // 2
    f = jnp.linspace(1e-4, n_bands - 1, n_bands, dtype=jnp.float32)[None, :]
    ang = (2.0 * math.pi / L) * jnp.arange(L, dtype=jnp.float32)[:, None] * f
    z = jnp.concatenate([t, jnp.cos(ang), -jnp.sin(ang)], -1)
    return jnp.pad(z, ((0, 0), (0, HY_EMB_PAD - HY_EMB)))


def _filter_kernel(z_ref, w1_ref, b1_ref, w2_ref, b2_ref, w3_ref, b3_ref, fr_ref, dl_ref, h_ref, s_ref):
    i = pl.program_id(0)
    z = z_ref[...]
    fr = fr_ref[...]
    h = jnp.sin(fr * (_dot(z, w1_ref[...]) + b1_ref[...]))
    h = jnp.sin(fr * (_dot(h, w2_ref[...]) + b2_ref[...]))
    h = _mm1(h.astype(jnp.bfloat16), w3_ref[...].astype(jnp.bfloat16)) + b3_ref[...]
    win = jnp.exp(-z[:, 0:1] * dl_ref[...])
    h = h * jnp.concatenate([win] * (h.shape[1] // win.shape[1]), axis=1)

    @pl.when(i == 0)
    def _():
        s_ref[...] = jnp.zeros_like(s_ref)

    s_ref[...] += jnp.sum(jnp.abs(h), axis=0, keepdims=True)
    row = lax.broadcasted_iota(jnp.int32, h.shape, 0) + i * h.shape[0]
    col = lax.broadcasted_iota(jnp.int32, h.shape, 1)
    neg = (col // GROUP_W) % 2 == 1
    h_ref[...] = jnp.where(neg & (row == 0), 0.0, h)


def hyena_filter_bank(L, w1, b1, w2, b2, w3, b3, freq):
    z = hyena_features(L)
    w1p = jnp.pad(w1, ((0, HY_EMB_PAD - HY_EMB), (0, 0)))
    max_decay = math.log(HY_TARGET) / HY_FAST_DECAY
    min_decay = math.log(HY_TARGET) / HY_SLOW_DECAY
    deltas = jnp.abs(jnp.linspace(min_decay, max_decay, GROUP_W, dtype=jnp.float32))[None, :]
    n = w3.shape[1]
    tr = min(L, HY_FILTER_ROWS)
    args = (z, w1p, b1[None, :], w2, b2[None, :], w3, b3[None, :], freq[None, :], deltas)
    return pl.pallas_call(
        _filter_kernel,
        grid=(L // tr,),
        in_specs=[pl.BlockSpec((tr, HY_EMB_PAD), lambda i: (i, 0))] + [_full(a) for a in args[1:]],
        out_specs=[pl.BlockSpec((tr, n), lambda i: (i, 0)), pl.BlockSpec((1, n), lambda i: (0, 0))],
        out_shape=[jax.ShapeDtypeStruct((L, n), jnp.float32), jax.ShapeDtypeStruct((1, n), jnp.float32)],
        compiler_params=_params(1),
    )(*args)


FFT_NB = 8
FFT_CB = 256
FFT_N1 = 128


def fft_tables(n1, n2):
    N = n1 * n2
    nk = -(-(n2 // 2 + 1) // FFT_NB) * FFT_NB
    a = jnp.arange(n1, dtype=jnp.float32)[:, None, None]
    k2 = jnp.arange(nk, dtype=jnp.float32)[None, :, None]
    b = jnp.arange(n2 // 2, dtype=jnp.float32)[None, None, :]
    ph = (jnp.mod(a * k2, float(N)) / N + jnp.mod(b * k2, float(n2)) / n2) * (-2.0 * math.pi)
    g = jnp.concatenate([jnp.cos(ph), jnp.sin(ph)], axis=1)
    kk = jnp.arange(nk)
    weight = jnp.where((kk == 0) | (kk == n2 // 2), 1.0, jnp.where(kk < n2 // 2, 2.0, 0.0))
    ginv = jnp.transpose(g, (0, 2, 1)) * (jnp.tile(weight, 2) / N)
    k1 = jnp.arange(n1, dtype=jnp.float32)[:, None]
    aa = jnp.arange(n1, dtype=jnp.float32)[None, :]
    f = jnp.mod(k1 * aa, float(n1)) * (-2.0 * math.pi / n1)
    fr, fi = jnp.cos(f), jnp.sin(f)
    ff = jnp.concatenate([jnp.concatenate([fr, -fi], axis=1), jnp.concatenate([fi, fr], axis=1)], axis=0)
    bf = lambda t: t.astype(jnp.bfloat16)
    return {"g": bf(g), "ginv": bf(ginv), "ff": bf(ff), "fft": bf(ff.T)}


def _stage_a_kernel(u_ref, g_ref, o_ref):
    for j in range(FFT_NB):
        o_ref[j] = _mm1(g_ref[j], u_ref[:, j, :].astype(jnp.bfloat16))


def fft_stage_a(u3, nb, col, ncol, g):
    n1 = u3.shape[1]
    rows = g.shape[1]
    gspec = pl.BlockSpec((FFT_NB, rows, nb), lambda c, i: (i, 0, 0))
    return pl.pallas_call(
        _stage_a_kernel,
        grid=(ncol, n1 // FFT_NB),
        in_specs=[pl.BlockSpec((nb, FFT_NB, CB), lambda c, i: (0, i, col + c)), gspec],
        out_specs=pl.BlockSpec((FFT_NB, rows, CB), lambda c, i: (i, 0, c)),
        out_shape=jax.ShapeDtypeStruct((n1, rows, ncol * CB), jnp.float32),
        compiler_params=_params(2),
    )(u3, g)


def _stage_b_filter_kernel(re_ref, im_ref, f_ref, o_ref):
    ff = f_ref[...]
    for j in range(FFT_NB):
        o_ref[j] = _mm1(ff, jnp.concatenate([re_ref[:, j, :], im_ref[:, j, :]], axis=0).astype(jnp.bfloat16))


def fft_stage_b_filter(t1, ff):
    n1, n2x2, C = t1.shape
    n2 = n2x2 // 2
    blk = lambda off: pl.BlockSpec((n1, FFT_NB, FFT_CB), lambda c, i: (0, i + off, c))
    mat = pl.BlockSpec((2 * n1, 2 * n1), lambda c, i: (0, 0))
    return pl.pallas_call(
        _stage_b_filter_kernel,
        grid=(C // FFT_CB, n2 // FFT_NB),
        in_specs=[blk(0), blk(n2 // FFT_NB), mat],
        out_specs=pl.BlockSpec((FFT_NB, 2 * n1, FFT_CB), lambda c, i: (i, 0, c)),
        out_shape=jax.ShapeDtypeStruct((n2, 2 * n1, C), jnp.float32),
        compiler_params=_params(2),
    )(t1, t1, ff)


def _filter_spectrum(p, q, s, n1):
    return (p[:n1] + q[:n1]) * s, (p[n1:] - q[n1:]) * s


def _stage_b_conv_kernel(re_ref, im_ref, p_ref, q_ref, s_ref, f_ref, ft_ref, ore_ref, oim_ref):
    n1 = re_ref.shape[0]
    s = s_ref[...]
    ff, fft_ = f_ref[...], ft_ref[...]
    bf = lambda t: t.astype(jnp.bfloat16)
    for j in range(FFT_NB):
        x = _mm1(ff, bf(jnp.concatenate([re_ref[:, j, :], im_ref[:, j, :]], axis=0)))
        hr, hi = _filter_spectrum(p_ref[j], q_ref[j], s, n1)
        xr, xi = x[:n1], x[n1:]
        z = _mm1(fft_, bf(jnp.concatenate([xr * hr - xi * hi, xr * hi + xi * hr], axis=0)))
        ore_ref[:, j, :] = z[:n1]
        oim_ref[:, j, :] = z[n1:]


def fft_stage_b_conv(t1, spec, col_p, col_q, inv_norm, ff, fft_):
    n1, n2x2, C = t1.shape
    n2 = n2x2 // 2
    ncb = C // FFT_CB
    blk = lambda off: pl.BlockSpec((n1, FFT_NB, FFT_CB), lambda c, i: (0, i + off, c))
    mat = pl.BlockSpec((2 * n1, 2 * n1), lambda c, i: (0, 0))
    sp = lambda col: pl.BlockSpec((FFT_NB, 2 * n1, FFT_CB), lambda c, i: (i, 0, col * ncb + c))
    return pl.pallas_call(
        _stage_b_conv_kernel,
        grid=(ncb, n2 // FFT_NB),
        in_specs=[blk(0), blk(n2 // FFT_NB), sp(col_p), sp(col_q), pl.BlockSpec((1, FFT_CB), lambda c, i: (0, c)),
                  mat, mat],
        out_specs=[blk(0), blk(0)],
        out_shape=[jax.ShapeDtypeStruct((n1, n2, C), jnp.float32)] * 2,
        compiler_params=_params(2),
    )(t1, t1, spec, spec, inv_norm, ff, fft_)


def _stage_a_inv_kernel(re_ref, im_ref, g_ref, u_ref, gate_ref, bias_ref, o_ref):
    for j in range(FFT_NB):
        y = _mm1(g_ref[j], jnp.concatenate([re_ref[j], im_ref[j]], axis=0).astype(jnp.bfloat16))
        o_ref[:, j, :] = gate_ref[:, j, :] * (y + u_ref[:, j, :] * bias_ref[...])


def fft_stage_a_inv(t2re, t2im, ginv, nb, u3, u_col, gate3, gate_col, bias):
    n1, nk, C = t2re.shape
    tb = pl.BlockSpec((FFT_NB, nk, CB), lambda c, i: (i, 0, c))
    gb = pl.BlockSpec((FFT_NB, nb, 2 * nk), lambda c, i: (i, 0, 0))
    ub = lambda col: pl.BlockSpec((nb, FFT_NB, CB), lambda c, i: (0, i, col + c))
    return pl.pallas_call(
        _stage_a_inv_kernel,
        grid=(C // CB, n1 // FFT_NB),
        in_specs=[tb, tb, gb, ub(u_col), ub(gate_col), pl.BlockSpec((1, CB), lambda c, i: (0, c))],
        out_specs=ub(0),
        out_shape=jax.ShapeDtypeStruct((nb, n1, C), jnp.float32),
        compiler_params=_params(2),
    )(t2re, t2im, ginv, u3, gate3, bias)


def hyena_long_conv_chain(u, L, biases, filt, colsum, tabs):
    C = GROUP_W
    n1 = FFT_N1
    n2 = 2 * L // n1
    nb = n2 // 2
    spec = fft_stage_b_filter(fft_stage_a(filt.reshape(nb, n1, filt.shape[1]), nb, 0, filt.shape[1] // CB, tabs["g"]),
                              tabs["ff"])
    s4 = colsum.reshape(HY_ORDER, 2, C)
    inv_norm = 1.0 / (s4[:, 0] + s4[:, 1])
    u3 = u.reshape(u.shape[0] // n1, n1, u.shape[1])
    y3, y_col = u3, 0
    for n in range(HY_ORDER):
        t1 = fft_stage_a(y3, nb, y_col, 1, tabs["g"])
        t2re, t2im = fft_stage_b_conv(t1, spec, 2 * n, 2 * n + 1, inv_norm[n][None, :], tabs["ff"], tabs["fft"])
        y3 = fft_stage_a_inv(t2re, t2im, tabs["ginv"], nb, y3, y_col, u3, n + 1, biases[n][None, :])
        y_col = 0
    return y3.reshape(L, C)


def dense_dft_tables(n):
    N = 2 * n
    k = jnp.arange(N, dtype=jnp.float32)[:, None]
    t = jnp.arange(n, dtype=jnp.float32)[None, :]
    ph = jnp.mod(k * t, float(N)) * (2.0 * math.pi / N)
    fd = jnp.concatenate([jnp.cos(ph), -jnp.sin(ph)], axis=0)
    return fd.astype(jnp.bfloat16), (fd.T / N).astype(jnp.bfloat16)


def _hyena_ctx_kernel(u_ref, h_ref, s_ref, bias_ref, fd_ref, ft_ref, o_ref):
    C = GROUP_W
    fd, ft = fd_ref[...], ft_ref[...]
    K = fd.shape[0] // 2
    bf = lambda t: t.astype(jnp.bfloat16)
    hs = _mm1(fd, bf(h_ref[...]))
    s = s_ref[...]
    y = u_ref[:, 0:C]
    for n in range(HY_ORDER):
        cp, cq = 2 * n * C, (2 * n + 1) * C
        inv = 1.0 / (s[:, cp:cp + C] + s[:, cq:cq + C])
        hr, hi = _filter_spectrum(hs[:, cp:cp + C], hs[:, cq:cq + C], inv, K)
        x = _mm1(fd, bf(y))
        xr, xi = x[:K], x[K:]
        conv = _mm1(ft, bf(jnp.concatenate([xr * hr - xi * hi, xr * hi + xi * hr], axis=0)))
        y = u_ref[:, (n + 1) * C:(n + 2) * C] * (conv + y * bias_ref[n:n + 1, :])
    o_ref[...] = y


def hyena_ctx(u, filt, colsum, biases, tabs):
    args = (u, filt, colsum, biases, tabs[0], tabs[1])
    return pl.pallas_call(
        _hyena_ctx_kernel,
        in_specs=[_full(a) for a in args],
        out_specs=pl.BlockSpec((u.shape[0], GROUP_W), lambda: (0, 0)),
        out_shape=jax.ShapeDtypeStruct((u.shape[0], GROUP_W), jnp.float32),
        compiler_params=pltpu.CompilerParams(vmem_limit_bytes=VMEM_LIMIT),
    )(*args)


SCAN_T = 64
GROUP_LANES = 256
RW_SUB = 16
HG_SUB = 16
LOG2_E = 1.4426950408889634


def _chunk_index(d, i, n_ctx, n_all):
    fwd = jnp.where(i < n_ctx, n_all - n_ctx + i, i - n_ctx)
    return jnp.where(d == 0, fwd, n_all - 1 - i)


def _stacking(T, S, nh, head, sign):
    G = nh * head
    nb = T // S
    n = nh * T
    rr = lax.broadcasted_iota(jnp.int32, (n, G), 0)
    same = ((rr // S) % nh) == (lax.broadcasted_iota(jnp.int32, (n, G), 1) // head)

    def bd(x):
        pieces = []
        for i in range(nb):
            pieces += [x[i * S:(i + 1) * S]] * nh
        return jnp.where(same, jnp.concatenate(pieces, axis=0), 0.0)

    def collapse(o):
        outs = []
        for i in range(nb):
            acc = o[i * nh * S:i * nh * S + S]
            for h in range(1, nh):
                acc = acc + o[i * nh * S + h * S:i * nh * S + (h + 1) * S]
            outs.append(acc)
        return jnp.concatenate(outs, axis=0)

    rt = lax.broadcasted_iota(jnp.int32, (n, n), 0)
    cs = lax.broadcasted_iota(jnp.int32, (n, n), 1)
    t_r = (rt // (nh * S)) * S + rt % S
    t_c = (cs // (nh * S)) * S + cs % S
    same_h = ((rt // S) % nh) == ((cs // S) % nh)
    before = same_h & ((t_r - t_c) * sign > 0)
    return bd, collapse, before, rt, cs


def _softplus(x):
    return jnp.maximum(x, 0.0) + jnp.log(1.0 + jnp.exp(-jnp.abs(x)))


def _rwkv_prep_kernel(*refs, n_lat, n_rows):
    zrefs, rest = refs[:15], refs[15:]
    (mu_ref, kk_w_ref, ka_ref, w0_ref, a0_ref, w2h_ref, w2l_ref, a2h_ref, a2l_ref, ones_ref,
     r_ref, k_ref, v_ref, g_ref, kk_ref, lw_ref, kd_ref, a_ref) = rest
    slabs = []
    for c in range(5):
        x = zrefs[3 * c][...]
        xp, xn = _neighbours(x, zrefs[3 * c + 1][...], zrefs[3 * c + 2][...], n_lat, n_rows)
        slabs.append(x + (0.5 * (xp + xn) - x) * mu_ref[:, c * CB:(c + 1) * CB])
    r, k, v, g, lora = slabs
    r_ref[...], k_ref[...], v_ref[...], g_ref[...] = r, k, v, g
    kk = k * kk_w_ref[...]
    ss = _mm2(kk * kk, ones_ref[...])
    kk_ref[...] = kk * lax.rsqrt(jnp.maximum(ss, 1e-24))
    lora_t = _split(jnp.tanh(lora))
    lora_s = _split(lora)
    for d in range(2):
        w_log = -_softplus(-(w0_ref[d:d + 1, :] + _mm3(lora_t, (w2h_ref[d], w2l_ref[d])))) - 0.5
        lw_ref[d] = -jnp.exp(w_log)
        a = _sigmoid(a0_ref[d:d + 1, :] + _mm3(lora_s, (a2h_ref[d], a2l_ref[d])))
        a_ref[d] = a
        kd_ref[d] = k * (1.0 + (a - 1.0) * ka_ref[...])


def rwkv_prep(z, mu, k_k, k_a, w0, a0, w2, a2, n_lat):
    n_rows = z.shape[0]
    tp = PREP_ROWS
    mu_p = jnp.pad(mu, (0, 5 * CB - RW_REAL))[None, :]
    w2p = jnp.zeros((2, CB, GROUP_W), jnp.float32)
    a2p = jnp.zeros((2, CB, GROUP_W), jnp.float32)
    for d in range(2):
        w2p = w2p.at[d, d * RW_LORA:(d + 1) * RW_LORA].set(w2[d])
        a2p = a2p.at[d, (2 + d) * RW_LORA:(3 + d) * RW_LORA].set(a2[d])
    w2s, a2s = _split(w2p), _split(a2p)
    small = (mu_p, k_k[None, :], k_a[None, :], w0, a0, w2s[0], w2s[1], a2s[0], a2s[1],
             _block_indicator(GROUP_W, RW_HEAD, 1.0))
    zspecs = []
    for c in range(5):
        zspecs += _halo_specs(tp, n_rows, COL_RW + c)
    one = pl.BlockSpec((tp, CB), lambda i: (i, 0))
    two = pl.BlockSpec((2, tp, CB), lambda i: (0, i, 0))
    s1 = jax.ShapeDtypeStruct((n_rows, GROUP_W), jnp.float32)
    s2 = jax.ShapeDtypeStruct((2, n_rows, GROUP_W), jnp.float32)
    return pl.pallas_call(
        functools.partial(_rwkv_prep_kernel, n_lat=n_lat, n_rows=n_rows),
        grid=(n_rows // tp,),
        in_specs=zspecs + [_full(a) for a in small],
        out_specs=[one] * 5 + [two] * 3,
        out_shape=[s1] * 5 + [s2] * 3,
        compiler_params=_params(1),
    )(*([z] * 15), *small)


def _rwkv_scan_kernel(*refs, head):
    in_f, in_b, (of_ref, ob_ref, ht_ref) = refs[:6], refs[6:12], refs[12:]
    T, G, S = SCAN_T, GROUP_LANES, RW_SUB
    nh = G // head
    nb = T // S
    n = nh * T

    @pl.when(pl.program_id(0) == 0)
    def _():
        ht_ref[...] = jnp.zeros_like(ht_ref)

    ti = lax.broadcasted_iota(jnp.int32, (T, T), 0)
    si = lax.broadcasted_iota(jnp.int32, (T, T), 1)
    bf = lambda t: t.astype(jnp.bfloat16)
    each = lambda f, *seqs: [f(*xs) for xs in zip(*seqs)]
    n_grp = in_f[0].shape[1] // G
    lanes = [slice(grp * G, (grp + 1) * G) for grp in range(n_grp)]

    r, v, kk, lw, k, a, tri, before, incl, dest = ([] for _ in range(10))
    for (r_ref, v_ref, kk_ref, lw_ref, k_ref, a_ref), o_ref, sign in ((in_f, of_ref, 1), (in_b, ob_ref, -1)):
        bd, collapse, before_d, rt, cs = _stacking(T, S, nh, head, sign)
        for ls in lanes:
            r.append(r_ref[:, ls]), v.append(v_ref[:, ls]), kk.append(kk_ref[:, ls])
            lw.append(lw_ref[0, :, ls]), k.append(k_ref[0, :, ls]), a.append(a_ref[0, :, ls])
            tri.append(jnp.where((ti - si) * sign >= 0, 1.0, 0.0))
            before.append(before_d), incl.append(before_d | (rt == cs)), dest.append((o_ref, ls))
    diag_blk = (rt // (nh * S)) == (cs // (nh * S))
    eye = jnp.where(rt == cs, 1.0, 0.0)
    ht = [ht_ref[ch] for ch in range(len(r))]

    c = each(lambda tr, t: _dot(tr, t), tri, lw)
    ctot = each(lambda t: jnp.sum(t, axis=0, keepdims=True), lw)
    beta = each(lambda p, q: p * q, kk, a)
    einv = each(lambda t: jnp.exp(-t), c)
    lhs = each(lambda kk_, r_, c_, lw_: bf(jnp.concatenate([bd(-kk_ * jnp.exp(c_ - lw_)), bd(r_ * jnp.exp(c_))], axis=0)),
               kk, r, c, lw)
    rhs = each(lambda k_, b_, e_: bf(jnp.concatenate([bd(k_ * e_), bd(b_ * e_)], axis=0)), k, beta, einv)
    m = each(lambda p, q: _mm1(p, q, NT), lhs, rhs)
    g = each(lambda p, h: _mm1(p, bf(h), NT), lhs, ht)
    v_b = each(lambda t: bf(bd(t)), v)
    x = each(lambda g_, m_, v_, bm: g_[:n] + _mm1(bf(jnp.where(bm, m_[:n, :n], 0.0)), v_), g, m, v_b, before)
    lb = each(lambda m_, bm: jnp.where(bm, m_[:n, n:], 0.0), m, before)
    ld = each(lambda t: jnp.where(diag_blk, t, 0.0), lb)
    lo = each(lambda p, q: p - q, lb, ld)
    xd = each(lambda t: eye + t, ld)
    lp = each(bf, ld)
    p = 2
    while p < S:
        lp = each(lambda t: bf(_mm1(t, t)), lp)
        xd = each(lambda x_, l_: x_ + _mm1(l_, bf(x_)), xd, lp)
        p *= 2
    xd_b = each(bf, xd)
    n_b = each(lambda x_, l_: bf(_mm1(x_, bf(l_))), xd_b, lo)
    y = each(lambda x_, t: _mm1(x_, bf(t)), xd_b, x)
    y = each(lambda y_, n_: y_ + _mm1(n_, bf(y_)), y, n_b)
    p = 2
    while p < nb:
        n_b = each(lambda t: bf(_mm1(t, t)), n_b)
        y = each(lambda y_, n_: y_ + _mm1(n_, bf(y_)), y, n_b)
        p *= 2
    u_b = each(bf, y)
    o = each(lambda g_, m_, v_, u_, im: (g_[n:] + _mm1(bf(jnp.where(im, m_[n:, :n], 0.0)), v_))
             + _mm1(bf(jnp.where(im, m_[n:, n:], 0.0)), u_), g, m, v_b, u_b, incl)
    efin = each(lambda ct, c_: jnp.exp(ct - c_), ctot, c)
    ht_new = each(lambda h, ct, v_, u_, k_, b_, e_: (h * jnp.exp(ct) + _mm1(v_, bf(bd(k_ * e_)), TN))
                  + _mm1(u_, bf(bd(b_ * e_)), TN), ht, ctot, v_b, u_b, k, beta, efin)
    for ch, (o_ref, ls) in enumerate(dest):
        o_ref[:, ls] = collapse(o[ch])
        ht_ref[ch] = ht_new[ch]


def rwkv_scan(r, v, kk, lw2, k2, a2, n_ctx_rows):
    N, W = r.shape
    T = SCAN_T
    n_all, n_ctx = N // T, n_ctx_rows // T
    shared = lambda d: pl.BlockSpec((T, W), lambda i: (_chunk_index(d, i, n_ctx, n_all), 0))
    per_dir = lambda d: pl.BlockSpec((1, T, W), lambda i: (d, _chunk_index(d, i, n_ctx, n_all), 0))
    ins = lambda d: [shared(d)] * 3 + [per_dir(d)] * 3
    return pl.pallas_call(
        functools.partial(_rwkv_scan_kernel, head=RW_HEAD),
        grid=(n_all,),
        in_specs=ins(0) + ins(1),
        out_specs=[shared(0), shared(1)],
        out_shape=[jax.ShapeDtypeStruct((N, W), jnp.float32)] * 2,
        scratch_shapes=[pltpu.VMEM((2 * (W // GROUP_LANES), GROUP_LANES, GROUP_LANES), jnp.float32)],
        compiler_params=_params(1),
    )(*([r, v, kk, lw2, k2, a2] * 2))


def rope_tables(n_ctx, L):
    d_axis = RT_HEAD // 2
    half = d_axis // 2
    inv = ROPE_BASE ** (-jnp.arange(0, d_axis, 2, dtype=jnp.float32) / d_axis)
    t = jnp.arange(L)
    pos = jnp.stack([(t // GRID_W).astype(jnp.float32), (t % GRID_W).astype(jnp.float32)], axis=1)
    j = jnp.arange(RT_HEAD)
    ang = pos[:, j // d_axis] * inv[j % half][None, :]
    sgn = jnp.where((j % d_axis) < half, -1.0, 1.0)[None, :]
    cos = jnp.concatenate([jnp.cos(ang), jnp.ones((n_ctx, RT_HEAD), jnp.float32)], axis=0)
    sin = jnp.concatenate([jnp.sin(ang) * sgn, jnp.zeros((n_ctx, RT_HEAD), jnp.float32)], axis=0)
    return jnp.tile(cos, (1, RT_HEADS)), jnp.tile(sin, (1, RT_HEADS))


def _rotate(x, cos, sin):
    G = x.shape[1]
    half = RT_HEAD // 4
    lane = lax.broadcasted_iota(jnp.int32, x.shape, 1)
    partner = jnp.where((lane % (2 * half)) < half, pltpu.roll(x, G - half, axis=1), pltpu.roll(x, half, axis=1))
    return x * cos + partner * sin


def _retention_scan_kernel(q_ref, k_ref, v_ref, cos_ref, sin_ref, lg_ref, o_ref, st_ref, *, head):
    T, G = SCAN_T, GROUP_LANES
    nh = G // head
    d = pl.program_id(0)
    sign = 1 - 2 * d

    @pl.when(pl.program_id(1) == 0)
    def _():
        st_ref[...] = jnp.zeros_like(st_ref)

    t = lax.broadcasted_iota(jnp.int32, (T, 1), 0)
    pos = (t + d * (T - 1 - 2 * t) + 1).astype(jnp.float32)
    bd, collapse, before, rt, cs = _stacking(T, T, nh, head, sign)
    incl = before | (rt == cs)
    bf = lambda x: x.astype(jnp.bfloat16)

    groups = range(q_ref.shape[1] // G)
    lanes = [slice(grp * G, (grp + 1) * G) for grp in groups]
    each = lambda f, *seqs: [f(*xs) for xs in zip(*seqs)]
    st = [st_ref[grp] for grp in groups]
    cos, sin, lg = ([ref[:, ls] for ls in lanes] for ref in (cos_ref, sin_ref, lg_ref))
    q = each(_rotate, [q_ref[:, ls] for ls in lanes], cos, sin)
    k = each(lambda x, c_, s_: _rotate(x, c_, s_) * (head ** -0.5), [k_ref[:, ls] for ls in lanes], cos, sin)
    c = each(lambda lg_: pos * lg_, lg)
    q_b = each(lambda q_, c_: bf(bd(q_ * jnp.exp(c_))), q, c)
    kt_b = each(lambda k_, c_: bf(bd(k_ * jnp.exp(-c_))), k, c)
    v_b = [bf(bd(v_ref[:, ls])) for ls in lanes]
    scores = each(lambda q_, k_: bf(jnp.where(incl, _mm1(q_, k_, NT), 0.0)), q_b, kt_b)
    inter = each(lambda q_, s_: _mm1(q_, bf(s_), NT), q_b, st)
    o = each(lambda g_, s_, v_: g_ + _mm1(s_, v_), inter, scores, v_b)
    st_new = each(lambda s_, lg_, v_, k_, c_: s_ * jnp.exp(float(T) * lg_)
                  + _mm1(v_, bf(bd(k_ * jnp.exp(float(T) * lg_ - c_))), TN), st, lg, v_b, k, c)
    for grp in groups:
        o_ref[0, :, lanes[grp]] = collapse(o[grp])
        st_ref[grp] = st_new[grp]


def retention_scan(z, cos, sin, lg, n_ctx_rows):
    N = z.shape[0]
    T = SCAN_T
    n_all, n_ctx = N // T, n_ctx_rows // T
    zc = lambda col: pl.BlockSpec((T, CB), lambda d, i: (_chunk_index(d, i, n_ctx, n_all), COL_RT + col))
    tab = pl.BlockSpec((T, GROUP_W), lambda d, i: (_chunk_index(d, i, n_ctx, n_all), 0))
    return pl.pallas_call(
        functools.partial(_retention_scan_kernel, head=RT_HEAD),
        grid=(2, n_all),
        in_specs=[zc(0), zc(1), zc(2), tab, tab, pl.BlockSpec((1, GROUP_W), lambda d, i: (0, 0))],
        out_specs=pl.BlockSpec((1, T, GROUP_W), lambda d, i: (d, _chunk_index(d, i, n_ctx, n_all), 0)),
        out_shape=jax.ShapeDtypeStruct((2, N, GROUP_W), jnp.float32),
        scratch_shapes=[pltpu.VMEM((GROUP_W // GROUP_LANES, GROUP_LANES, GROUP_LANES), jnp.float32)],
        compiler_params=_params(2),
    )(z, z, z, cos, sin, lg)


def _gla_scan_kernel(q_ref, f_ref, i_ref, lb_ref, o_ref, st_ref, *, head):
    T, S = SCAN_T, HG_SUB
    W = q_ref.shape[-1]
    nh = W // head
    d = pl.program_id(0)
    sign = 1 - 2 * d

    @pl.when(pl.program_id(1) == 0)
    def _():
        st_ref[...] = jnp.zeros_like(st_ref)

    ti = lax.broadcasted_iota(jnp.int32, (S, S), 0)
    si = lax.broadcasted_iota(jnp.int32, (S, S), 1)
    tri_incl = jnp.where((ti - si) * sign >= 0, 1.0, 0.0)
    row = lax.broadcasted_iota(jnp.int32, (S, 1), 0)
    lb = lb_ref[0]

    states = [st_ref[h] for h in range(nh)]
    for j in range(T // S):
        jb = j + d * (T // S - 1 - 2 * j)
        rows = pl.ds(pl.multiple_of(jb * S, S), S)
        q = q_ref[rows, :]
        q = q * _sigmoid(q)
        v = i_ref[rows, :]
        gate = lb + (1.0 - lb) * _sigmoid(f_ref[rows, :])
        lf = jnp.log(jnp.maximum(gate, HG_MIN_GATE))
        k = 1.0 - gate
        b = _dot(tri_incl, lf)
        btot = jnp.sum(lf, axis=0, keepdims=True)
        qe = q * jnp.exp(b)
        ke = k * jnp.exp(btot - b)
        outs = []
        for h in range(nh):
            ls = slice(h * head, (h + 1) * head)
            qh, kh, vh = q[:, ls], k[:, ls], v[:, ls]
            bh = b[:, ls] * LOG2_E
            st = states[h]
            o = _mm1(qe[:, ls].astype(jnp.bfloat16), st.astype(jnp.bfloat16), NT)
            for s in range(S):
                e = jnp.exp2(bh - bh[s:s + 1, :])
                a_s = jnp.sum(qh * kh[s:s + 1, :] * e, axis=-1, keepdims=True)
                a_s = jnp.where((row - s) * sign >= 0, a_s, 0.0)
                o = o + a_s * vh[s:s + 1, :]
            outs.append(o)
            states[h] = st * jnp.exp(btot[:, ls]) + _mm1(vh.astype(jnp.bfloat16), ke[:, ls].astype(jnp.bfloat16), TN)
        o_ref[0, rows, :] = jnp.concatenate(outs, axis=1)
    for h in range(nh):
        st_ref[h] = states[h]


def gla_scan(z, lb2, n_ctx_rows):
    N = z.shape[0]
    W = GROUP_W
    T = SCAN_T
    n_all, n_ctx = N // T, n_ctx_rows // T
    zc = lambda col: pl.BlockSpec((T, W), lambda d, i: (_chunk_index(d, i, n_ctx, n_all), COL_HG + col))
    zf = pl.BlockSpec((T, W), lambda d, i: (_chunk_index(d, i, n_ctx, n_all), COL_HG + 1 + d))
    return pl.pallas_call(
        functools.partial(_gla_scan_kernel, head=HG_HEAD),
        grid=(2, n_all),
        in_specs=[zc(0), zf, zc(3), pl.BlockSpec((1, 1, W), lambda d, i: (d, 0, 0))],
        out_specs=pl.BlockSpec((1, T, W), lambda d, i: (d, _chunk_index(d, i, n_ctx, n_all), 0)),
        out_shape=jax.ShapeDtypeStruct((2, N, W), jnp.float32),
        scratch_shapes=[pltpu.VMEM((W // HG_HEAD, HG_HEAD, HG_HEAD), jnp.float32)],
        compiler_params=_params(2),
    )(z, z, z, lb2)


def _outproj_kernel(hy_ref, rwof_ref, rwob_ref, r_ref, k_ref, v_ref, rwg_ref, rto_ref, rtg_ref, hgo_ref, hgg_ref,
                    rk_ref, gng_ref, gnb_ref, hgn_ref, avg64_ref, avg128_ref,
                    w_ref, x_ref, gate_ref, g_ref, b_ref, o_ref, *, n_lat):
    avg64, avg128 = avg64_ref[...], avg128_ref[...]

    def head_norm(o, avg, eps, centre):
        if centre:
            o = o - _mm2(o, avg)
        return o * lax.rsqrt(_mm2(o * o, avg) + eps)

    silu = lambda t: t * _sigmoid(t)
    y_rw = head_norm(rwof_ref[...] + rwob_ref[...], avg64, RW_GN_EPS, True) * gng_ref[...] + gnb_ref[...]
    bonus = (float(RW_HEAD) * _mm2(r_ref[...] * k_ref[...] * rk_ref[...], avg64)) * v_ref[...]
    y_rw = (y_rw + bonus) * _sigmoid(rwg_ref[...])
    y_rt = head_norm(rto_ref[0] + rto_ref[1], avg64, HEAD_NORM_EPS, True) * silu(rtg_ref[...])
    y_hg = head_norm(hgo_ref[0] + hgo_ref[1], avg128, HEAD_NORM_EPS, False) * hgn_ref[...] * silu(hgg_ref[...])
    y = None
    for m, ym in enumerate((hy_ref[...], y_rw, y_rt, y_hg)):
        part = jnp.dot(ym.astype(jnp.bfloat16), w_ref[m * GROUP_W:(m + 1) * GROUP_W, :],
                       preferred_element_type=jnp.float32)
        y = part if y is None else y + part
    is_ctx = _ctx_rows(x_ref.shape[0], n_lat)
    r = ALPHA * x_ref[...] + _mod_row(gate_ref, is_ctx) * y
    o_ref[...] = _ln_rows(r) * g_ref[...] + b_ref[...]


def outproj_deepnorm(y_hy, rw_o2, rw_r, rw_k, rw_v, rw_g, rt_o2, hg_o2, z, r_k, gn_g, gn_b, hg_norm_g,
                     w_bf16, x, mod, g, b, n_lat, m):
    tm = OUT_TM
    one = pl.BlockSpec((tm, GROUP_W), lambda i: (i, 0))
    two = pl.BlockSpec((2, tm, GROUP_W), lambda i: (0, i, 0))
    zc = lambda col: pl.BlockSpec((tm, CB), lambda i: (i, col))
    row = pl.BlockSpec((tm, D_MODEL), lambda i: (i, 0))
    small = (r_k, gn_g, gn_b, hg_norm_g, _block_indicator(GROUP_W, RW_HEAD, 1.0 / RW_HEAD),
             _block_indicator(GROUP_W, HG_HEAD, 1.0 / HG_HEAD))
    return pl.pallas_call(
        functools.partial(_outproj_kernel, n_lat=n_lat),
        grid=(m // tm,),
        in_specs=[one, one, one, one, one, one, one, two, zc(COL_RT + 3), two, zc(COL_HG + 4)]
        + [_full(a) for a in small]
        + [_full(w_bf16), row, pl.BlockSpec((8, D_MODEL), lambda i: (0, 2)), _full(g), _full(b)],
        out_specs=row,
        out_shape=jax.ShapeDtypeStruct((m, D_MODEL), jnp.float32),
        compiler_params=_params(1),
    )(y_hy, rw_o2[0], rw_o2[1], rw_r, rw_k, rw_v, rw_g, rt_o2, z, hg_o2, z, *small, w_bf16, x, mod, g, b)


def _ffn_kernel(x_ref, sh_ref, sc_ref, w1_ref, w3_ref, w2_ref, gate_ref, g_ref, b_ref, o_ref, h_ref, acc_ref, *, n_lat):
    j = pl.program_id(1)

    @pl.when(j == 0)
    def _():
        is_ctx = _ctx_rows(x_ref.shape[0], n_lat)
        h = _ln_rows(x_ref[...]) * (1.0 + _mod_row(sc_ref, is_ctx)) + _mod_row(sh_ref, is_ctx)
        h_ref[...] = h.astype(jnp.bfloat16)
        acc_ref[...] = jnp.zeros_like(acc_ref)

    h = h_ref[...]
    a = jnp.dot(h, w1_ref[...], preferred_element_type=jnp.float32)
    u = jnp.dot(h, w3_ref[...], preferred_element_type=jnp.float32)
    s = (a * _sigmoid(a) * u).astype(jnp.bfloat16)
    acc_ref[...] += jnp.dot(s, w2_ref[...], preferred_element_type=jnp.float32)

    @pl.when(j == pl.num_programs(1) - 1)
    def _():
        is_ctx = _ctx_rows(x_ref.shape[0], n_lat)
        r = ALPHA * x_ref[...] + _mod_row(gate_ref, is_ctx) * acc_ref[...]
        o_ref[...] = _ln_rows(r) * g_ref[...] + b_ref[...]


def ffn_deepnorm(x, mod, w1, w3, w2, g, b, n_lat, tm):
    m = x.shape[0]
    row = pl.BlockSpec((tm, D_MODEL), lambda i, j: (i, 0))
    modc = lambda c: pl.BlockSpec((8, D_MODEL), lambda i, j: (0, c))
    vec = pl.BlockSpec((1, D_MODEL), lambda i, j: (0, 0))
    return pl.pallas_call(
        functools.partial(_ffn_kernel, n_lat=n_lat),
        grid=(m // tm, FFN_HIDDEN // FFN_TF),
        in_specs=[row, modc(3), modc(4),
                  pl.BlockSpec((D_MODEL, FFN_TF), lambda i, j: (0, j)),
                  pl.BlockSpec((D_MODEL, FFN_TF), lambda i, j: (0, j)),
                  pl.BlockSpec((FFN_TF, D_MODEL), lambda i, j: (j, 0)),
                  modc(5), vec, vec],
        out_specs=row,
        out_shape=jax.ShapeDtypeStruct((m, D_MODEL), jnp.float32),
        scratch_shapes=[pltpu.VMEM((tm, D_MODEL), jnp.bfloat16),
                        pltpu.VMEM((tm, D_MODEL), jnp.float32)],
        compiler_params=_params(2),
    )(x, mod, mod, w1, w3, w2, mod, g, b)


def kernel(x, c, ctx, c_ctx, ada_w, ada_b, w_in, w_out, ln_g, ln_b, hy_conv_w, hy_conv_b, hy_w1, hy_b1, hy_w2, hy_b2, hy_w3, hy_b3, hy_freq, hy_bias, rw_mu, rw_w0, rw_w2, rw_a0, rw_a2, rw_k_k, rw_k_a, rw_r_k, rw_gn_g, rw_gn_b, hg_lb_raw, hg_norm_g, ffn_w1, ffn_w3, ffn_w2):
    L, n_ctx = x.shape[1], ctx.shape[1]
    sm = jax.nn.softmax(hg_lb_raw.astype(jnp.float32), axis=1)
    lower_bounds = jnp.cumsum(sm, axis=1) - sm[:, :1]
    log_gamma = jnp.log1p(-jnp.exp2(-5.0 - jnp.arange(RT_HEADS, dtype=jnp.float32)))
    lg = jnp.repeat(log_gamma, RT_HEAD)[None, :]
    cos, sin = rope_tables(n_ctx, L)
    fft_tabs = fft_tables(FFT_N1, 2 * L // FFT_N1)
    ctx_tabs = dense_dft_tables(n_ctx)

    c8 = jnp.zeros((8, D_MODEL), jnp.float32).at[0].set(c[0]).at[1].set(c_ctx)
    xs = jnp.concatenate([x[0], ctx[0]], axis=0)
    n_rows = L + n_ctx
    for l in range(DEPTH):
        with_ctx = l < DEPTH - 1
        mod = ada_modulation(c8, ada_w, l, ada_b[l][None, :])
        w_out_b = w_out[l].astype(jnp.bfloat16)
        w1_b, w3_b, w2_b = (w[l].astype(jnp.bfloat16) for w in (ffn_w1, ffn_w3, ffn_w2))

        z = modulated_projection(xs, mod, w_in, l, L)

        u = hyena_conv3(z, hy_conv_w[l], hy_conv_b[l][None, :], L)
        hy_w = (hy_w1[l], hy_b1[l], hy_w2[l], hy_b2[l], hy_w3[l], hy_b3[l], hy_freq[l])
        filt, colsum = hyena_filter_bank(L, *hy_w)
        y_lat = hyena_long_conv_chain(u, L, hy_bias[l], filt, colsum, fft_tabs)
        if with_ctx:
            filt_c, colsum_c = hyena_filter_bank(n_ctx, *hy_w)
            y_ctx = hyena_ctx(u[L:], filt_c, colsum_c, hy_bias[l], ctx_tabs)
        else:
            y_ctx = jnp.zeros((n_ctx, GROUP_W), jnp.float32)
        y_hy = jnp.concatenate([y_lat, y_ctx], axis=0)

        rw_r, rw_k, rw_v, rw_g, rw_kk, rw_lw, rw_kd, rw_a = rwkv_prep(
            z, rw_mu[l], rw_k_k[l], rw_k_a[l], rw_w0[l], rw_a0[l], rw_w2[l], rw_a2[l], L)
        rw_o2 = rwkv_scan(rw_r, rw_v, rw_kk, rw_lw, rw_kd, rw_a, n_ctx)
        rt_o2 = retention_scan(z, cos, sin, lg, n_ctx)
        hg_o2 = gla_scan(z, lower_bounds[:, l][:, None, :], n_ctx)

        xs = outproj_deepnorm(y_hy, rw_o2, rw_r, rw_k, rw_v, rw_g, rt_o2, hg_o2, z,
                              rw_r_k[l].reshape(1, GROUP_W), rw_gn_g[l][None, :], rw_gn_b[l][None, :],
                              hg_norm_g[l][None, :], w_out_b, xs, mod, ln_g[l, 0][None, :], ln_b[l, 0][None, :],
                              L, n_rows if with_ctx else L)
        xs = ffn_deepnorm(xs, mod, w1_b, w3_b, w2_b, ln_g[l, 1][None, :], ln_b[l, 1][None, :], L,
                          FFN_TM if with_ctx else FFN_LAST_TM)
    return xs[None]
```

```python
import functools
import math

import jax
import jax.numpy as jnp
from jax import lax
from jax.experimental import pallas as pl
from jax.experimental.pallas import tpu as pltpu

D_MODEL = 2048
DEPTH = 2
GRID_W = 64
N_MIXERS = 4
GROUP_W = D_MODEL // N_MIXERS
HY_ORDER = 2
HY_EMB = 33
HY_FAST_DECAY = 0.3
HY_SLOW_DECAY = 1.5
HY_TARGET = 1e-2
RW_HEAD = 64
RW_LORA = 96
RW_GN_EPS = 64e-5
RT_HEAD = 64
RT_HEADS = GROUP_W // RT_HEAD
ROPE_BASE = 10000.0
HG_HEAD = 128
HG_MIN_GATE = 1e-30
FFN_HIDDEN = 5632
ALPHA = (2 * DEPTH) ** 0.25
LN_EPS = 1e-6
HEAD_NORM_EPS = 1e-6

LANES = 128
CB = 512
COL_HY = 0
COL_RW = 3
COL_RT = 8
COL_HG = 12
P_IN_PAD = 17 * CB
RW_REAL = 4 * GROUP_W + 4 * RW_LORA
PROJ_TM = 1408
FFN_TM = 768
FFN_LAST_TM = 512
OUT_TM = 256
FFN_TF = 512
PREP_ROWS = 256
HALO = 8
VMEM_LIMIT = 56 * 1024 * 1024

HI = lax.Precision.HIGHEST
NN = ((1,), (0,))
NT = ((1,), (1,))
TN = ((0,), (0,))


def _params(n_axes):
    return pltpu.CompilerParams(dimension_semantics=("arbitrary",) * n_axes, vmem_limit_bytes=VMEM_LIMIT)


def _full(a):
    return pl.BlockSpec(a.shape, lambda *_: (0,) * a.ndim)


def _dot(a, b):
    return jnp.dot(a, b, precision=HI, preferred_element_type=jnp.float32)


def _split(x):
    hi = x.astype(jnp.bfloat16)
    lo = (x - hi.astype(jnp.float32)).astype(jnp.bfloat16)
    return hi, lo


def _mm3(a, b, dims=NN):
    d = lambda p, q: lax.dot_general(p, q, (dims, ((), ())), preferred_element_type=jnp.float32)
    return d(a[0], b[0]) + (d(a[0], b[1]) + d(a[1], b[0]))


def _mm1(a, b, dims=NN):
    return lax.dot_general(a, b, (dims, ((), ())), preferred_element_type=jnp.float32)


def _mm2(x, b):
    hi, lo = _split(x)
    d = functools.partial(jnp.dot, preferred_element_type=jnp.float32)
    return d(hi, b) + d(lo, b)


def _ln_rows(x):
    mu = jnp.mean(x, axis=-1, keepdims=True)
    xc = x - mu
    var = jnp.mean(xc * xc, axis=-1, keepdims=True)
    return xc * lax.rsqrt(var + LN_EPS)


def _sigmoid(x):
    return 1.0 / (1.0 + jnp.exp(-x))


def _ctx_rows(tm, n_lat):
    return (pl.program_id(0) * tm + lax.broadcasted_iota(jnp.int32, (tm, 1), 0)) >= n_lat


def _mod_row(ref, is_ctx):
    return jnp.where(is_ctx, ref[1:2, :], ref[0:1, :])


def _block_indicator(width, head, value):
    i = jnp.arange(width) // head
    return jnp.where(i[:, None] == i[None, :], value, 0.0).astype(jnp.bfloat16)


def _ada_kernel(c_ref, w_ref, b_ref, o_ref):
    c = c_ref[...]
    h = c * _sigmoid(c)
    o_ref[...] = jnp.dot(h.astype(jnp.bfloat16), w_ref[...].astype(jnp.bfloat16),
                         preferred_element_type=jnp.float32) + b_ref[...]


def ada_modulation(c8, w, layer, b):
    n = w.shape[2]
    tn = 1024
    return pl.pallas_call(
        _ada_kernel,
        grid=(n // tn,),
        in_specs=[pl.BlockSpec((8, D_MODEL), lambda j: (0, 0)),
                  pl.BlockSpec((None, D_MODEL, tn), lambda j: (layer, 0, j)),
                  pl.BlockSpec((1, tn), lambda j: (0, j))],
        out_specs=pl.BlockSpec((8, tn), lambda j: (0, j)),
        out_shape=jax.ShapeDtypeStruct((8, n), jnp.float32),
        compiler_params=_params(1),
    )(c8, w, b)


W_SUB = CB // LANES
W_PAD_PIECE = (COL_RW * CB + RW_REAL) // LANES


def _proj_kernel(x_ref, sh_ref, sc_ref, *refs, n_lat):
    w_refs, (o_ref, h_ref) = refs[:W_SUB], refs[W_SUB:]
    j = pl.program_id(1)

    @pl.when(j == 0)
    def _():
        is_ctx = _ctx_rows(x_ref.shape[0], n_lat)
        h = _ln_rows(x_ref[...]) * (1.0 + _mod_row(sc_ref, is_ctx)) + _mod_row(sh_ref, is_ctx)
        h_ref[...] = h.astype(jnp.bfloat16)

    pieces = [w_refs[k][...] for k in range(W_SUB)]
    for k in range(W_SUB):
        if (W_PAD_PIECE - k) % W_SUB == 0:
            pieces[k] = jnp.where(j * W_SUB + k == W_PAD_PIECE, 0.0, pieces[k])
    w = jnp.concatenate(pieces, axis=1).astype(jnp.bfloat16)
    o_ref[...] = jnp.dot(h_ref[...], w, preferred_element_type=jnp.float32)


def modulated_projection(x, mod, w, layer, n_lat):
    m = x.shape[0]
    tm = PROJ_TM

    def piece(k):
        def index(i, j):
            p = j * W_SUB + k
            return layer, 0, jnp.where(p < W_PAD_PIECE, p, p - 1)
        return pl.BlockSpec((None, D_MODEL, LANES), index)

    return pl.pallas_call(
        functools.partial(_proj_kernel, n_lat=n_lat),
        grid=(m // tm, P_IN_PAD // CB),
        in_specs=[pl.BlockSpec((tm, D_MODEL), lambda i, j: (i, 0)),
                  pl.BlockSpec((8, D_MODEL), lambda i, j: (0, 0)),
                  pl.BlockSpec((8, D_MODEL), lambda i, j: (0, 1))] + [piece(k) for k in range(W_SUB)],
        out_specs=pl.BlockSpec((tm, CB), lambda i, j: (i, j)),
        out_shape=jax.ShapeDtypeStruct((m, P_IN_PAD), jnp.float32),
        scratch_shapes=[pltpu.VMEM((tm, D_MODEL), jnp.bfloat16)],
        compiler_params=_params(2),
    )(x, mod, mod, *([w] * W_SUB))


def _halo_specs(tp, n_rows, col):
    per = tp // HALO
    last = n_rows // HALO - 1
    main = pl.BlockSpec((tp, CB), lambda i, *_: (i, col(*_) if callable(col) else col))
    prev = pl.BlockSpec((HALO, CB), lambda i, *_: (jnp.maximum(i * per - 1, 0), col(*_) if callable(col) else col))
    nxt = pl.BlockSpec((HALO, CB), lambda i, *_: (jnp.minimum((i + 1) * per, last), col(*_) if callable(col) else col))
    return [main, prev, nxt]


def _neighbours(x, prev, nxt, n_lat, n_rows):
    tp = x.shape[0]
    loc = lax.broadcasted_iota(jnp.int32, (tp, 1), 0)
    row = pl.program_id(0) * tp + loc
    xp = jnp.where(loc == 0, prev[HALO - 1:HALO, :], pltpu.roll(x, 1, axis=0))
    xp = jnp.where((row == 0) | (row == n_lat), 0.0, xp)
    xn = jnp.where(loc == tp - 1, nxt[0:1, :], pltpu.roll(x, tp - 1, axis=0))
    xn = jnp.where((row == n_lat - 1) | (row == n_rows - 1), 0.0, xn)
    return xp, xn


def _conv3_kernel(z_ref, zp_ref, zn_ref, w_ref, b_ref, o_ref, *, n_lat, n_rows):
    x = z_ref[...]
    xp, xn = _neighbours(x, zp_ref[...], zn_ref[...], n_lat, n_rows)
    w = w_ref[...]
    o_ref[...] = xp * w[0:1, :] + x * w[1:2, :] + xn * w[2:3, :] + b_ref[...]


def hyena_conv3(z, w, b, n_lat):
    n_rows = z.shape[0]
    tp = PREP_ROWS
    ncol = w.shape[1] // CB
    return pl.pallas_call(
        functools.partial(_conv3_kernel, n_lat=n_lat, n_rows=n_rows),
        grid=(n_rows // tp, ncol),
        in_specs=_halo_specs(tp, n_rows, lambda c: COL_HY + c)
        + [pl.BlockSpec((3, CB), lambda i, c: (0, c)), pl.BlockSpec((1, CB), lambda i, c: (0, c))],
        out_specs=pl.BlockSpec((tp, CB), lambda i, c: (i, c)),
        out_shape=jax.ShapeDtypeStruct((n_rows, w.shape[1]), jnp.float32),
        compiler_params=_params(2),
    )(z, z, z, w, b)


HY_EMB_PAD = 40
HY_FILTER_ROWS = 512


def hyena_features(L):
    t = jnp.linspace(0.0, 1.0, L, dtype=jnp.float32)[:, None]
    n_bands = (HY_EMB - 1) // 2
    f = jnp.linspace(1e-4, n_bands - 1, n_bands, dtype=jnp.float32)[None, :]
    ang = (2.0 * math.pi / L) * jnp.arange(L, dtype=jnp.float32)[:, None] * f
    z = jnp.concatenate([t, jnp.cos(ang), -jnp.sin(ang)], -1)
    return jnp.pad(z, ((0, 0), (0, HY_EMB_PAD - HY_EMB)))


def _filter_kernel(z_ref, w1_ref, b1_ref, w2_ref, b2_ref, w3_ref, b3_ref, fr_ref, dl_ref, h_ref, s_ref):
    i = pl.program_id(0)
    z = z_ref[...]
    fr = fr_ref[...]
    h = jnp.sin(fr * (_dot(z, w1_ref[...]) + b1_ref[...]))
    h = jnp.sin(fr * (_dot(h, w2_ref[...]) + b2_ref[...]))
    h = _mm1(h.astype(jnp.bfloat16), w3_ref[...].astype(jnp.bfloat16)) + b3_ref[...]
    win = jnp.exp(-z[:, 0:1] * dl_ref[...])
    h = h * jnp.concatenate([win] * (h.shape[1] // win.shape[1]), axis=1)

    @pl.when(i == 0)
    def _():
        s_ref[...] = jnp.zeros_like(s_ref)

    s_ref[...] += jnp.sum(jnp.abs(h), axis=0, keepdims=True)
    row = lax.broadcasted_iota(jnp.int32, h.shape, 0) + i * h.shape[0]
    col = lax.broadcasted_iota(jnp.int32, h.shape, 1)
    neg = (col // GROUP_W) % 2 == 1
    h_ref[...] = jnp.where(neg & (row == 0), 0.0, h)


def hyena_filter_bank(L, w1, b1, w2, b2, w3, b3, freq):
    z = hyena_features(L)
    w1p = jnp.pad(w1, ((0, HY_EMB_PAD - HY_EMB), (0, 0)))
    max_decay = math.log(HY_TARGET) / HY_FAST_DECAY
    min_decay = math.log(HY_TARGET) / HY_SLOW_DECAY
    deltas = jnp.abs(jnp.linspace(min_decay, max_decay, GROUP_W, dtype=jnp.float32))[None, :]
    n = w3.shape[1]
    tr = min(L, HY_FILTER_ROWS)
    args = (z, w1p, b1[None, :], w2, b2[None, :], w3, b3[None, :], freq[None, :], deltas)
    return pl.pallas_call(
        _filter_kernel,
        grid=(L // tr,),
        in_specs=[pl.BlockSpec((tr, HY_EMB_PAD), lambda i: (i, 0))] + [_full(a) for a in args[1:]],
        out_specs=[pl.BlockSpec((tr, n), lambda i: (i, 0)), pl.BlockSpec((1, n), lambda i: (0, 0))],
        out_shape=[jax.ShapeDtypeStruct((L, n), jnp.float32), jax.ShapeDtypeStruct((1, n), jnp.float32)],
        compiler_params=_params(1),
    )(*args)


FFT_NB = 8
FFT_CB = 256
FFT_N1 = 128


def fft_tables(n1, n2):
    N = n1 * n2
    nk = -(-(n2 // 2 + 1) // FFT_NB) * FFT_NB
    a = jnp.arange(n1, dtype=jnp.float32)[:, None, None]
    k2 = jnp.arange(nk, dtype=jnp.float32)[None, :, None]
    b = jnp.arange(n2 // 2, dtype=jnp.float32)[None, None, :]
    ph = (jnp.mod(a * k2, float(N)) / N + jnp.mod(b * k2, float(n2)) / n2) * (-2.0 * math.pi)
    g = jnp.concatenate([jnp.cos(ph), jnp.sin(ph)], axis=1)
    kk = jnp.arange(nk)
    weight = jnp.where((kk == 0) | (kk == n2 // 2), 1.0, jnp.where(kk < n2 // 2, 2.0, 0.0))
    ginv = jnp.transpose(g, (0, 2, 1)) * (jnp.tile(weight, 2) / N)
    k1 = jnp.arange(n1, dtype=jnp.float32)[:, None]
    aa = jnp.arange(n1, dtype=jnp.float32)[None, :]
    f = jnp.mod(k1 * aa, float(n1)) * (-2.0 * math.pi / n1)
    fr, fi = jnp.cos(f), jnp.sin(f)
    ff = jnp.concatenate([jnp.concatenate([fr, -fi], axis=1), jnp.concatenate([fi, fr], axis=1)], axis=0)
    bf = lambda t: t.astype(jnp.bfloat16)
    return {"g": bf(g), "ginv": bf(ginv), "ff": bf(ff), "fft": bf(ff.T)}


def _stage_a_kernel(u_ref, g_ref, o_ref):
    for j in range(FFT_NB):
        o_ref[j] = _mm1(g_ref[j], u_ref[:, j, :].astype(jnp.bfloat16))


def fft_stage_a(u3, nb, col, ncol, g):
    n1 = u3.shape[1]
    rows = g.shape[1]
    gspec = pl.BlockSpec((FFT_NB, rows, nb), lambda c, i: (i, 0, 0))
    return pl.pallas_call(
        _stage_a_kernel,
        grid=(ncol, n1 // FFT_NB),
        in_specs=[pl.BlockSpec((nb, FFT_NB, CB), lambda c, i: (0, i, col + c)), gspec],
        out_specs=pl.BlockSpec((FFT_NB, rows, CB), lambda c, i: (i, 0, c)),
        out_shape=jax.ShapeDtypeStruct((n1, rows, ncol * CB), jnp.float32),
        compiler_params=_params(2),
    )(u3, g)


def _stage_b_filter_kernel(re_ref, im_ref, f_ref, o_ref):
    ff = f_ref[...]
    for j in range(FFT_NB):
        o_ref[j] = _mm1(ff, jnp.concatenate([re_ref[:, j, :], im_ref[:, j, :]], axis=0).astype(jnp.bfloat16))


def fft_stage_b_filter(t1, ff):
    n1, n2x2, C = t1.shape
    n2 = n2x2 // 2
    blk = lambda off: pl.BlockSpec((n1, FFT_NB, FFT_CB), lambda c, i: (0, i + off, c))
    mat = pl.BlockSpec((2 * n1, 2 * n1), lambda c, i: (0, 0))
    return pl.pallas_call(
        _stage_b_filter_kernel,
        grid=(C // FFT_CB, n2 // FFT_NB),
        in_specs=[blk(0), blk(n2 // FFT_NB), mat],
        out_specs=pl.BlockSpec((FFT_NB, 2 * n1, FFT_CB), lambda c, i: (i, 0, c)),
        out_shape=jax.ShapeDtypeStruct((n2, 2 * n1, C), jnp.float32),
        compiler_params=_params(2),
    )(t1, t1, ff)


def _filter_spectrum(p, q, s, n1):
    return (p[:n1] + q[:n1]) * s, (p[n1:] - q[n1:]) * s


def _stage_b_conv_kernel(re_ref, im_ref, p_ref, q_ref, s_ref, f_ref, ft_ref, ore_ref, oim_ref):
    n1 = re_ref.shape[0]
    s = s_ref[...]
    ff, fft_ = f_ref[...], ft_ref[...]
    bf = lambda t: t.astype(jnp.bfloat16)
    for j in range(FFT_NB):
        x = _mm1(ff, bf(jnp.concatenate([re_ref[:, j, :], im_ref[:, j, :]], axis=0)))
        hr, hi = _filter_spectrum(p_ref[j], q_ref[j], s, n1)
        xr, xi = x[:n1], x[n1:]
        z = _mm1(fft_, bf(jnp.concatenate([xr * hr - xi * hi, xr * hi + xi * hr], axis=0)))
        ore_ref[:, j, :] = z[:n1]
        oim_ref[:, j, :] = z[n1:]


def fft_stage_b_conv(t1, spec, col_p, col_q, inv_norm, ff, fft_):
    n1, n2x2, C = t1.shape
    n2 = n2x2 // 2
    ncb = C // FFT_CB
    blk = lambda off: pl.BlockSpec((n1, FFT_NB, FFT_CB), lambda c, i: (0, i + off, c))
    mat = pl.BlockSpec((2 * n1, 2 * n1), lambda c, i: (0, 0))
    sp = lambda col: pl.BlockSpec((FFT_NB, 2 * n1, FFT_CB), lambda c, i: (i, 0, col * ncb + c))
    return pl.pallas_call(
        _stage_b_conv_kernel,
        grid=(ncb, n2 // FFT_NB),
        in_specs=[blk(0), blk(n2 // FFT_NB), sp(col_p), sp(col_q), pl.BlockSpec((1, FFT_CB), lambda c, i: (0, c)),
                  mat, mat],
        out_specs=[blk(0), blk(0)],
        out_shape=[jax.ShapeDtypeStruct((n1, n2, C), jnp.float32)] * 2,
        compiler_params=_params(2),
    )(t1, t1, spec, spec, inv_norm, ff, fft_)


def _stage_a_inv_kernel(re_ref, im_ref, g_ref, u_ref, gate_ref, bias_ref, o_ref):
    for j in range(FFT_NB):
        y = _mm1(g_ref[j], jnp.concatenate([re_ref[j], im_ref[j]], axis=0).astype(jnp.bfloat16))
        o_ref[:, j, :] = gate_ref[:, j, :] * (y + u_ref[:, j, :] * bias_ref[...])


def fft_stage_a_inv(t2re, t2im, ginv, nb, u3, u_col, gate3, gate_col, bias):
    n1, nk, C = t2re.shape
    tb = pl.BlockSpec((FFT_NB, nk, CB), lambda c, i: (i, 0, c))
    gb = pl.BlockSpec((FFT_NB, nb, 2 * nk), lambda c, i: (i, 0, 0))
    ub = lambda col: pl.BlockSpec((nb, FFT_NB, CB), lambda c, i: (0, i, col + c))
    return pl.pallas_call(
        _stage_a_inv_kernel,
        grid=(C // CB, n1 // FFT_NB),
        in_specs=[tb, tb, gb, ub(u_col), ub(gate_col), pl.BlockSpec((1, CB), lambda c, i: (0, c))],
        out_specs=ub(0),
        out_shape=jax.ShapeDtypeStruct((nb, n1, C), jnp.float32),
        compiler_params=_params(2),
    )(t2re, t2im, ginv, u3, gate3, bias)


def hyena_long_conv_chain(u, L, biases, filt, colsum, tabs):
    C = GROUP_W
    n1 = FFT_N1
    n2 = 2 * L // n1
    nb = n2 // 2
    spec = fft_stage_b_filter(fft_stage_a(filt.reshape(nb, n1, filt.shape[1]), nb, 0, filt.shape[1] // CB, tabs["g"]),
                              tabs["ff"])
    s4 = colsum.reshape(HY_ORDER, 2, C)
    inv_norm = 1.0 / (s4[:, 0] + s4[:, 1])
    u3 = u.reshape(u.shape[0] // n1, n1, u.shape[1])
    y3, y_col = u3, 0
    for n in range(HY_ORDER):
        t1 = fft_stage_a(y3, nb, y_col, 1, tabs["g"])
        t2re, t2im = fft_stage_b_conv(t1, spec, 2 * n, 2 * n + 1, inv_norm[n][None, :], tabs["ff"], tabs["fft"])
        y3 = fft_stage_a_inv(t2re, t2im, tabs["ginv"], nb, y3, y_col, u3, n + 1, biases[n][None, :])
        y_col = 0
    return y3.reshape(L, C)


def dense_dft_tables(n):
    N = 2 * n
    k = jnp.arange(N, dtype=jnp.float32)[:, None]
    t = jnp.arange(n, dtype=jnp.float32)[None, :]
    ph = jnp.mod(k * t, float(N)) * (2.0 * math.pi / N)
    fd = jnp.concatenate([jnp.cos(ph), -jnp.sin(ph)], axis=0)
    return fd.astype(jnp.bfloat16), (fd.T / N).astype(jnp.bfloat16)


def _hyena_ctx_kernel(u_ref, h_ref, s_ref, bias_ref, fd_ref, ft_ref, o_ref):
    C = GROUP_W
    fd, ft = fd_ref[...], ft_ref[...]
    K = fd.shape[0] // 2
    bf = lambda t: t.astype(jnp.bfloat16)
    hs = _mm1(fd, bf(h_ref[...]))
    s = s_ref[...]
    y = u_ref[:, 0:C]
    for n in range(HY_ORDER):
        cp, cq = 2 * n * C, (2 * n + 1) * C
        inv = 1.0 / (s[:, cp:cp + C] + s[:, cq:cq + C])
        hr, hi = _filter_spectrum(hs[:, cp:cp + C], hs[:, cq:cq + C], inv, K)
        x = _mm1(fd, bf(y))
        xr, xi = x[:K], x[K:]
        conv = _mm1(ft, bf(jnp.concatenate([xr * hr - xi * hi, xr * hi + xi * hr], axis=0)))
        y = u_ref[:, (n + 1) * C:(n + 2) * C] * (conv + y * bias_ref[n:n + 1, :])
    o_ref[...] = y


def hyena_ctx(u, filt, colsum, biases, tabs):
    args = (u, filt, colsum, biases, tabs[0], tabs[1])
    return pl.pallas_call(
        _hyena_ctx_kernel,
        in_specs=[_full(a) for a in args],
        out_specs=pl.BlockSpec((u.shape[0], GROUP_W), lambda: (0, 0)),
        out_shape=jax.ShapeDtypeStruct((u.shape[0], GROUP_W), jnp.float32),
        compiler_params=pltpu.CompilerParams(vmem_limit_bytes=VMEM_LIMIT),
    )(*args)


SCAN_T = 64
GROUP_LANES = 256
RW_SUB = 16
HG_SUB = 16
LOG2_E = 1.4426950408889634


def _chunk_index(d, i, n_ctx, n_all):
    fwd = jnp.where(i < n_ctx, n_all - n_ctx + i, i - n_ctx)
    return jnp.where(d == 0, fwd, n_all - 1 - i)


def _stacking(T, S, nh, head, sign):
    G = nh * head
    nb = T // S
    n = nh * T
    rr = lax.broadcasted_iota(jnp.int32, (n, G), 0)
    same = ((rr // S) % nh) == (lax.broadcasted_iota(jnp.int32, (n, G), 1) // head)

    def bd(x):
        pieces = []
        for i in range(nb):
            pieces += [x[i * S:(i + 1) * S]] * nh
        return jnp.where(same, jnp.concatenate(pieces, axis=0), 0.0)

    def collapse(o):
        outs = []
        for i in range(nb):
            acc = o[i * nh * S:i * nh * S + S]
            for h in range(1, nh):
                acc = acc + o[i * nh * S + h * S:i * nh * S + (h + 1) * S]
            outs.append(acc)
        return jnp.concatenate(outs, axis=0)

    rt = lax.broadcasted_iota(jnp.int32, (n, n), 0)
    cs = lax.broadcasted_iota(jnp.int32, (n, n), 1)
    t_r = (rt // (nh * S)) * S + rt % S
    t_c = (cs // (nh * S)) * S + cs % S
    same_h = ((rt // S) % nh) == ((cs // S) % nh)
    before = same_h & ((t_r - t_c) * sign > 0)
    return bd, collapse, before, rt, cs


def _softplus(x):
    return jnp.maximum(x, 0.0) + jnp.log(1.0 + jnp.exp(-jnp.abs(x)))


def _rwkv_prep_kernel(*refs, n_lat, n_rows):
    zrefs, rest = refs[:15], refs[15:]
    (mu_ref, kk_w_ref, ka_ref, w0_ref, a0_ref, w2h_ref, w2l_ref, a2h_ref, a2l_ref, ones_ref,
     r_ref, k_ref, v_ref, g_ref, kk_ref, lw_ref, kd_ref, a_ref) = rest
    slabs = []
    for c in range(5):
        x = zrefs[3 * c][...]
        xp, xn = _neighbours(x, zrefs[3 * c + 1][...], zrefs[3 * c + 2][...], n_lat, n_rows)
        slabs.append(x + (0.5 * (xp + xn) - x) * mu_ref[:, c * CB:(c + 1) * CB])
    r, k, v, g, lora = slabs
    r_ref[...], k_ref[...], v_ref[...], g_ref[...] = r, k, v, g
    kk = k * kk_w_ref[...]
    ss = _mm2(kk * kk, ones_ref[...])
    kk_ref[...] = kk * lax.rsqrt(jnp.maximum(ss, 1e-24))
    lora_t = _split(jnp.tanh(lora))
    lora_s = _split(lora)
    for d in range(2):
        w_log = -_softplus(-(w0_ref[d:d + 1, :] + _mm3(lora_t, (w2h_ref[d], w2l_ref[d])))) - 0.5
        lw_ref[d] = -jnp.exp(w_log)
        a = _sigmoid(a0_ref[d:d + 1, :] + _mm3(lora_s, (a2h_ref[d], a2l_ref[d])))
        a_ref[d] = a
        kd_ref[d] = k * (1.0 + (a - 1.0) * ka_ref[...])


def rwkv_prep(z, mu, k_k, k_a, w0, a0, w2, a2, n_lat):
    n_rows = z.shape[0]
    tp = PREP_ROWS
    mu_p = jnp.pad(mu, (0, 5 * CB - RW_REAL))[None, :]
    w2p = jnp.zeros((2, CB, GROUP_W), jnp.float32)
    a2p = jnp.zeros((2, CB, GROUP_W), jnp.float32)
    for d in range(2):
        w2p = w2p.at[d, d * RW_LORA:(d + 1) * RW_LORA].set(w2[d])
        a2p = a2p.at[d, (2 + d) * RW_LORA:(3 + d) * RW_LORA].set(a2[d])
    w2s, a2s = _split(w2p), _split(a2p)
    small = (mu_p, k_k[None, :], k_a[None, :], w0, a0, w2s[0], w2s[1], a2s[0], a2s[1],
             _block_indicator(GROUP_W, RW_HEAD, 1.0))
    zspecs = []
    for c in range(5):
        zspecs += _halo_specs(tp, n_rows, COL_RW + c)
    one = pl.BlockSpec((tp, CB), lambda i: (i, 0))
    two = pl.BlockSpec((2, tp, CB), lambda i: (0, i, 0))
    s1 = jax.ShapeDtypeStruct((n_rows, GROUP_W), jnp.float32)
    s2 = jax.ShapeDtypeStruct((2, n_rows, GROUP_W), jnp.float32)
    return pl.pallas_call(
        functools.partial(_rwkv_prep_kernel, n_lat=n_lat, n_rows=n_rows),
        grid=(n_rows // tp,),
        in_specs=zspecs + [_full(a) for a in small],
        out_specs=[one] * 5 + [two] * 3,
        out_shape=[s1] * 5 + [s2] * 3,
        compiler_params=_params(1),
    )(*([z] * 15), *small)


def _rwkv_scan_kernel(*refs, head):
    in_f, in_b, (of_ref, ob_ref, ht_ref) = refs[:6], refs[6:12], refs[12:]
    T, G, S = SCAN_T, GROUP_LANES, RW_SUB
    nh = G // head
    nb = T // S
    n = nh * T

    @pl.when(pl.program_id(0) == 0)
    def _():
        ht_ref[...] = jnp.zeros_like(ht_ref)

    ti = lax.broadcasted_iota(jnp.int32, (T, T), 0)
    si = lax.broadcasted_iota(jnp.int32, (T, T), 1)
    bf = lambda t: t.astype(jnp.bfloat16)
    each = lambda f, *seqs: [f(*xs) for xs in zip(*seqs)]
    n_grp = in_f[0].shape[1] // G
    lanes = [slice(grp * G, (grp + 1) * G) for grp in range(n_grp)]

    r, v, kk, lw, k, a, tri, before, incl, dest = ([] for _ in range(10))
    for (r_ref, v_ref, kk_ref, lw_ref, k_ref, a_ref), o_ref, sign in ((in_f, of_ref, 1), (in_b, ob_ref, -1)):
        bd, collapse, before_d, rt, cs = _stacking(T, S, nh, head, sign)
        for ls in lanes:
            r.append(r_ref[:, ls]), v.append(v_ref[:, ls]), kk.append(kk_ref[:, ls])
            lw.append(lw_ref[0, :, ls]), k.append(k_ref[0, :, ls]), a.append(a_ref[0, :, ls])
            tri.append(jnp.where((ti - si) * sign >= 0, 1.0, 0.0))
            before.append(before_d), incl.append(before_d | (rt == cs)), dest.append((o_ref, ls))
    diag_blk = (rt // (nh * S)) == (cs // (nh * S))
    eye = jnp.where(rt == cs, 1.0, 0.0)
    ht = [ht_ref[ch] for ch in range(len(r))]

    c = each(lambda tr, t: _dot(tr, t), tri, lw)
    ctot = each(lambda t: jnp.sum(t, axis=0, keepdims=True), lw)
    beta = each(lambda p, q: p * q, kk, a)
    einv = each(lambda t: jnp.exp(-t), c)
    lhs = each(lambda kk_, r_, c_, lw_: bf(jnp.concatenate([bd(-kk_ * jnp.exp(c_ - lw_)), bd(r_ * jnp.exp(c_))], axis=0)),
               kk, r, c, lw)
    rhs = each(lambda k_, b_, e_: bf(jnp.concatenate([bd(k_ * e_), bd(b_ * e_)], axis=0)), k, beta, einv)
    m = each(lambda p, q: _mm1(p, q, NT), lhs, rhs)
    g = each(lambda p, h: _mm1(p, bf(h), NT), lhs, ht)
    v_b = each(lambda t: bf(bd(t)), v)
    x = each(lambda g_, m_, v_, bm: g_[:n] + _mm1(bf(jnp.where(bm, m_[:n, :n], 0.0)), v_), g, m, v_b, before)
    lb = each(lambda m_, bm: jnp.where(bm, m_[:n, n:], 0.0), m, before)
    ld = each(lambda t: jnp.where(diag_blk, t, 0.0), lb)
    lo = each(lambda p, q: p - q, lb, ld)
    xd = each(lambda t: eye + t, ld)
    lp = each(bf, ld)
    p = 2
    while p < S:
        lp = each(lambda t: bf(_mm1(t, t)), lp)
        xd = each(lambda x_, l_: x_ + _mm1(l_, bf(x_)), xd, lp)
        p *= 2
    xd_b = each(bf, xd)
    n_b = each(lambda x_, l_: bf(_mm1(x_, bf(l_))), xd_b, lo)
    y = each(lambda x_, t: _mm1(x_, bf(t)), xd_b, x)
    y = each(lambda y_, n_: y_ + _mm1(n_, bf(y_)), y, n_b)
    p = 2
    while p < nb:
        n_b = each(lambda t: bf(_mm1(t, t)), n_b)
        y = each(lambda y_, n_: y_ + _mm1(n_, bf(y_)), y, n_b)
        p *= 2
    u_b = each(bf, y)
    o = each(lambda g_, m_, v_, u_, im: (g_[n:] + _mm1(bf(jnp.where(im, m_[n:, :n], 0.0)), v_))
             + _mm1(bf(jnp.where(im, m_[n:, n:], 0.0)), u_), g, m, v_b, u_b, incl)
    efin = each(lambda ct, c_: jnp.exp(ct - c_), ctot, c)
    ht_new = each(lambda h, ct, v_, u_, k_, b_, e_: (h * jnp.exp(ct) + _mm1(v_, bf(bd(k_ * e_)), TN))
                  + _mm1(u_, bf(bd(b_ * e_)), TN), ht, ctot, v_b, u_b, k, beta, efin)
    for ch, (o_ref, ls) in enumerate(dest):
        o_ref[:, ls] = collapse(o[ch])
        ht_ref[ch] = ht_new[ch]


def rwkv_scan(r, v, kk, lw2, k2, a2, n_ctx_rows):
    N, W = r.shape
    T = SCAN_T
    n_all, n_ctx = N // T, n_ctx_rows // T
    shared = lambda d: pl.BlockSpec((T, W), lambda i: (_chunk_index(d, i, n_ctx, n_all), 0))
    per_dir = lambda d: pl.BlockSpec((1, T, W), lambda i: (d, _chunk_index(d, i, n_ctx, n_all), 0))
    ins = lambda d: [shared(d)] * 3 + [per_dir(d)] * 3
    return pl.pallas_call(
        functools.partial(_rwkv_scan_kernel, head=RW_HEAD),
        grid=(n_all,),
        in_specs=ins(0) + ins(1),
        out_specs=[shared(0), shared(1)],
        out_shape=[jax.ShapeDtypeStruct((N, W), jnp.float32)] * 2,
        scratch_shapes=[pltpu.VMEM((2 * (W // GROUP_LANES), GROUP_LANES, GROUP_LANES), jnp.float32)],
        compiler_params=_params(1),
    )(*([r, v, kk, lw2, k2, a2] * 2))


def rope_tables(n_ctx, L):
    d_axis = RT_HEAD // 2
    half = d_axis // 2
    inv = ROPE_BASE ** (-jnp.arange(0, d_axis, 2, dtype=jnp.float32) / d_axis)
    t = jnp.arange(L)
    pos = jnp.stack([(t // GRID_W).astype(jnp.float32), (t % GRID_W).astype(jnp.float32)], axis=1)
    j = jnp.arange(RT_HEAD)
    ang = pos[:, j // d_axis] * inv[j % half][None, :]
    sgn = jnp.where((j % d_axis) < half, -1.0, 1.0)[None, :]
    cos = jnp.concatenate([jnp.cos(ang), jnp.ones((n_ctx, RT_HEAD), jnp.float32)], axis=0)
    sin = jnp.concatenate([jnp.sin(ang) * sgn, jnp.zeros((n_ctx, RT_HEAD), jnp.float32)], axis=0)
    return jnp.tile(cos, (1, RT_HEADS)), jnp.tile(sin, (1, RT_HEADS))


def _rotate(x, cos, sin):
    G = x.shape[1]
    half = RT_HEAD // 4
    lane = lax.broadcasted_iota(jnp.int32, x.shape, 1)
    partner = jnp.where((lane % (2 * half)) < half, pltpu.roll(x, G - half, axis=1), pltpu.roll(x, half, axis=1))
    return x * cos + partner * sin


def _retention_scan_kernel(q_ref, k_ref, v_ref, cos_ref, sin_ref, lg_ref, o_ref, st_ref, *, head):
    T, G = SCAN_T, GROUP_LANES
    nh = G // head
    d = pl.program_id(0)
    sign = 1 - 2 * d

    @pl.when(pl.program_id(1) == 0)
    def _():
        st_ref[...] = jnp.zeros_like(st_ref)

    t = lax.broadcasted_iota(jnp.int32, (T, 1), 0)
    pos = (t + d * (T - 1 - 2 * t) + 1).astype(jnp.float32)
    bd, collapse, before, rt, cs = _stacking(T, T, nh, head, sign)
    incl = before | (rt == cs)
    bf = lambda x: x.astype(jnp.bfloat16)

    groups = range(q_ref.shape[1] // G)
    lanes = [slice(grp * G, (grp + 1) * G) for grp in groups]
    each = lambda f, *seqs: [f(*xs) for xs in zip(*seqs)]
    st = [st_ref[grp] for grp in groups]
    cos, sin, lg = ([ref[:, ls] for ls in lanes] for ref in (cos_ref, sin_ref, lg_ref))
    q = each(_rotate, [q_ref[:, ls] for ls in lanes], cos, sin)
    k = each(lambda x, c_, s_: _rotate(x, c_, s_) * (head ** -0.5), [k_ref[:, ls] for ls in lanes], cos, sin)
    c = each(lambda lg_: pos * lg_, lg)
    q_b = each(lambda q_, c_: bf(bd(q_ * jnp.exp(c_))), q, c)
    kt_b = each(lambda k_, c_: bf(bd(k_ * jnp.exp(-c_))), k, c)
    v_b = [bf(bd(v_ref[:, ls])) for ls in lanes]
    scores = each(lambda q_, k_: bf(jnp.where(incl, _mm1(q_, k_, NT), 0.0)), q_b, kt_b)
    inter = each(lambda q_, s_: _mm1(q_, bf(s_), NT), q_b, st)
    o = each(lambda g_, s_, v_: g_ + _mm1(s_, v_), inter, scores, v_b)
    st_new = each(lambda s_, lg_, v_, k_, c_: s_ * jnp.exp(float(T) * lg_)
                  + _mm1(v_, bf(bd(k_ * jnp.exp(float(T) * lg_ - c_))), TN), st, lg, v_b, k, c)
    for grp in groups:
        o_ref[0, :, lanes[grp]] = collapse(o[grp])
        st_ref[grp] = st_new[grp]


def retention_scan(z, cos, sin, lg, n_ctx_rows):
    N = z.shape[0]
    T = SCAN_T
    n_all, n_ctx = N // T, n_ctx_rows // T
    zc = lambda col: pl.BlockSpec((T, CB), lambda d, i: (_chunk_index(d, i, n_ctx, n_all), COL_RT + col))
    tab = pl.BlockSpec((T, GROUP_W), lambda d, i: (_chunk_index(d, i, n_ctx, n_all), 0))
    return pl.pallas_call(
        functools.partial(_retention_scan_kernel, head=RT_HEAD),
        grid=(2, n_all),
        in_specs=[zc(0), zc(1), zc(2), tab, tab, pl.BlockSpec((1, GROUP_W), lambda d, i: (0, 0))],
        out_specs=pl.BlockSpec((1, T, GROUP_W), lambda d, i: (d, _chunk_index(d, i, n_ctx, n_all), 0)),
        out_shape=jax.ShapeDtypeStruct((2, N, GROUP_W), jnp.float32),
        scratch_shapes=[pltpu.VMEM((GROUP_W // GROUP_LANES, GROUP_LANES, GROUP_LANES), jnp.float32)],
        compiler_params=_params(2),
    )(z, z, z, cos, sin, lg)


def _gla_scan_kernel(q_ref, f_ref, i_ref, lb_ref, o_ref, st_ref, *, head):
    @pl.when(pl.program_id(1) == 0)
    def _():
        st_ref[...] = jnp.zeros_like(st_ref)

    for direction, sign in ((0, 1), (1, -1)):
        pl.when(pl.program_id(0) == direction)(
            functools.partial(_gla_chunk, q_ref, f_ref, i_ref, lb_ref, o_ref, st_ref, head=head, sign=sign))


def _gla_chunk(q_ref, f_ref, i_ref, lb_ref, o_ref, st_ref, *, head, sign):
    T, S, H = SCAN_T, HG_SUB, HG_SUB // 2
    nh = q_ref.shape[-1] // head
    ti = lax.broadcasted_iota(jnp.int32, (S, S), 0)
    si = lax.broadcasted_iota(jnp.int32, (S, S), 1)
    tri_incl = jnp.where((ti - si) * sign >= 0, 1.0, 0.0)
    row = lax.broadcasted_iota(jnp.int32, (H, 1), 0)
    lb = lb_ref[0]

    states = [st_ref[h] for h in range(nh)]
    for j in range(T // S):
        jb = j if sign > 0 else T // S - 1 - j
        rows = pl.ds(jb * S, S)
        q = q_ref[rows, :]
        q = q * _sigmoid(q)
        v = i_ref[rows, :]
        gate = lb + (1.0 - lb) * _sigmoid(f_ref[rows, :])
        lf = jnp.log(jnp.maximum(gate, HG_MIN_GATE))
        k = 1.0 - gate
        b = _dot(tri_incl, lf)
        btot = jnp.sum(lf, axis=0, keepdims=True)
        qe = q * jnp.exp(b)
        ke = k * jnp.exp(btot - b)
        outs = []
        for h in range(nh):
            ls = slice(h * head, (h + 1) * head)
            qh, kh, vh = q[:, ls], k[:, ls], v[:, ls]
            bh = b[:, ls] * LOG2_E
            st = states[h]
            o = _mm1(qe[:, ls].astype(jnp.bfloat16), st.astype(jnp.bfloat16), NT)
            halves = [o[:H], o[H:]]
            for s in range(S):
                for hv in range(2):
                    order = (hv - s // H) * sign
                    if order < 0:
                        continue
                    rs = slice(hv * H, (hv + 1) * H)
                    e = jnp.exp2(bh[rs] - bh[s:s + 1, :])
                    a_s = jnp.sum(qh[rs] * kh[s:s + 1, :] * e, axis=-1, keepdims=True)
                    if order == 0:
                        a_s = jnp.where((row + (hv * H - s)) * sign >= 0, a_s, 0.0)
                    halves[hv] = halves[hv] + a_s * vh[s:s + 1, :]
            outs.append(jnp.concatenate(halves, axis=0))
            states[h] = st * jnp.exp(btot[:, ls]) + _mm1(vh.astype(jnp.bfloat16), ke[:, ls].astype(jnp.bfloat16), TN)
        o_ref[0, rows, :] = jnp.concatenate(outs, axis=1)
    for h in range(nh):
        st_ref[h] = states[h]


def gla_scan(z, lb2, n_ctx_rows):
    N = z.shape[0]
    W = GROUP_W
    T = SCAN_T
    n_all, n_ctx = N // T, n_ctx_rows // T
    zc = lambda col: pl.BlockSpec((T, W), lambda d, i: (_chunk_index(d, i, n_ctx, n_all), COL_HG + col))
    zf = pl.BlockSpec((T, W), lambda d, i: (_chunk_index(d, i, n_ctx, n_all), COL_HG + 1 + d))
    return pl.pallas_call(
        functools.partial(_gla_scan_kernel, head=HG_HEAD),
        grid=(2, n_all),
        in_specs=[zc(0), zf, zc(3), pl.BlockSpec((1, 1, W), lambda d, i: (d, 0, 0))],
        out_specs=pl.BlockSpec((1, T, W), lambda d, i: (d, _chunk_index(d, i, n_ctx, n_all), 0)),
        out_shape=jax.ShapeDtypeStruct((2, N, W), jnp.float32),
        scratch_shapes=[pltpu.VMEM((W // HG_HEAD, HG_HEAD, HG_HEAD), jnp.float32)],
        compiler_params=_params(2),
    )(z, z, z, lb2)


def _outproj_kernel(hy_ref, rwof_ref, rwob_ref, r_ref, k_ref, v_ref, rwg_ref, rto_ref, rtg_ref, hgo_ref, hgg_ref,
                    rk_ref, gng_ref, gnb_ref, hgn_ref, avg64_ref, avg128_ref,
                    w_ref, x_ref, gate_ref, g_ref, b_ref, o_ref, *, n_lat):
    avg64, avg128 = avg64_ref[...], avg128_ref[...]

    def head_norm(o, avg, eps, centre):
        if centre:
            o = o - _mm2(o, avg)
        return o * lax.rsqrt(_mm2(o * o, avg) + eps)

    silu = lambda t: t * _sigmoid(t)
    y_rw = head_norm(rwof_ref[...] + rwob_ref[...], avg64, RW_GN_EPS, True) * gng_ref[...] + gnb_ref[...]
    bonus = (float(RW_HEAD) * _mm2(r_ref[...] * k_ref[...] * rk_ref[...], avg64)) * v_ref[...]
    y_rw = (y_rw + bonus) * _sigmoid(rwg_ref[...])
    y_rt = head_norm(rto_ref[0] + rto_ref[1], avg64, HEAD_NORM_EPS, True) * silu(rtg_ref[...])
    y_hg = head_norm(hgo_ref[0] + hgo_ref[1], avg128, HEAD_NORM_EPS, False) * hgn_ref[...] * silu(hgg_ref[...])
    y = None
    for m, ym in enumerate((hy_ref[...], y_rw, y_rt, y_hg)):
        part = jnp.dot(ym.astype(jnp.bfloat16), w_ref[m * GROUP_W:(m + 1) * GROUP_W, :],
                       preferred_element_type=jnp.float32)
        y = part if y is None else y + part
    is_ctx = _ctx_rows(x_ref.shape[0], n_lat)
    r = ALPHA * x_ref[...] + _mod_row(gate_ref, is_ctx) * y
    o_ref[...] = _ln_rows(r) * g_ref[...] + b_ref[...]


def outproj_deepnorm(y_hy, rw_o2, rw_r, rw_k, rw_v, rw_g, rt_o2, hg_o2, z, r_k, gn_g, gn_b, hg_norm_g,
                     w_bf16, x, mod, g, b, n_lat, m):
    tm = OUT_TM
    one = pl.BlockSpec((tm, GROUP_W), lambda i: (i, 0))
    two = pl.BlockSpec((2, tm, GROUP_W), lambda i: (0, i, 0))
    zc = lambda col: pl.BlockSpec((tm, CB), lambda i: (i, col))
    row = pl.BlockSpec((tm, D_MODEL), lambda i: (i, 0))
    small = (r_k, gn_g, gn_b, hg_norm_g, _block_indicator(GROUP_W, RW_HEAD, 1.0 / RW_HEAD),
             _block_indicator(GROUP_W, HG_HEAD, 1.0 / HG_HEAD))
    return pl.pallas_call(
        functools.partial(_outproj_kernel, n_lat=n_lat),
        grid=(m // tm,),
        in_specs=[one, one, one, one, one, one, one, two, zc(COL_RT + 3), two, zc(COL_HG + 4)]
        + [_full(a) for a in small]
        + [_full(w_bf16), row, pl.BlockSpec((8, D_MODEL), lambda i: (0, 2)), _full(g), _full(b)],
        out_specs=row,
        out_shape=jax.ShapeDtypeStruct((m, D_MODEL), jnp.float32),
        compiler_params=_params(1),
    )(y_hy, rw_o2[0], rw_o2[1], rw_r, rw_k, rw_v, rw_g, rt_o2, z, hg_o2, z, *small, w_bf16, x, mod, g, b)


def _ffn_kernel(x_ref, sh_ref, sc_ref, w1_ref, w3_ref, w2_ref, gate_ref, g_ref, b_ref, o_ref, h_ref, acc_ref, *, n_lat):
    j = pl.program_id(1)

    @pl.when(j == 0)
    def _():
        is_ctx = _ctx_rows(x_ref.shape[0], n_lat)
        h = _ln_rows(x_ref[...]) * (1.0 + _mod_row(sc_ref, is_ctx)) + _mod_row(sh_ref, is_ctx)
        h_ref[...] = h.astype(jnp.bfloat16)
        acc_ref[...] = jnp.zeros_like(acc_ref)

    h = h_ref[...]
    a = jnp.dot(h, w1_ref[...], preferred_element_type=jnp.float32)
    u = jnp.dot(h, w3_ref[...], preferred_element_type=jnp.float32)
    s = (a * _sigmoid(a) * u).astype(jnp.bfloat16)
    acc_ref[...] += jnp.dot(s, w2_ref[...], preferred_element_type=jnp.float32)

    @pl.when(j == pl.num_programs(1) - 1)
    def _():
        is_ctx = _ctx_rows(x_ref.shape[0], n_lat)
        r = ALPHA * x_ref[...] + _mod_row(gate_ref, is_ctx) * acc_ref[...]
        o_ref[...] = _ln_rows(r) * g_ref[...] + b_ref[...]


def ffn_deepnorm(x, mod, w1, w3, w2, g, b, n_lat, tm):
    m = x.shape[0]
    row = pl.BlockSpec((tm, D_MODEL), lambda i, j: (i, 0))
    modc = lambda c: pl.BlockSpec((8, D_MODEL), lambda i, j: (0, c))
    vec = pl.BlockSpec((1, D_MODEL), lambda i, j: (0, 0))
    return pl.pallas_call(
        functools.partial(_ffn_kernel, n_lat=n_lat),
        grid=(m // tm, FFN_HIDDEN // FFN_TF),
        in_specs=[row, modc(3), modc(4),
                  pl.BlockSpec((D_MODEL, FFN_TF), lambda i, j: (0, j)),
                  pl.BlockSpec((D_MODEL, FFN_TF), lambda i, j: (0, j)),
                  pl.BlockSpec((FFN_TF, D_MODEL), lambda i, j: (j, 0)),
                  modc(5), vec, vec],
        out_specs=row,
        out_shape=jax.ShapeDtypeStruct((m, D_MODEL), jnp.float32),
        scratch_shapes=[pltpu.VMEM((tm, D_MODEL), jnp.bfloat16),
                        pltpu.VMEM((tm, D_MODEL), jnp.float32)],
        compiler_params=_params(2),
    )(x, mod, mod, w1, w3, w2, mod, g, b)


def kernel(x, c, ctx, c_ctx, ada_w, ada_b, w_in, w_out, ln_g, ln_b, hy_conv_w, hy_conv_b, hy_w1, hy_b1, hy_w2, hy_b2, hy_w3, hy_b3, hy_freq, hy_bias, rw_mu, rw_w0, rw_w2, rw_a0, rw_a2, rw_k_k, rw_k_a, rw_r_k, rw_gn_g, rw_gn_b, hg_lb_raw, hg_norm_g, ffn_w1, ffn_w3, ffn_w2):
    L, n_ctx = x.shape[1], ctx.shape[1]
    sm = jax.nn.softmax(hg_lb_raw.astype(jnp.float32), axis=1)
    lower_bounds = jnp.cumsum(sm, axis=1) - sm[:, :1]
    log_gamma = jnp.log1p(-jnp.exp2(-5.0 - jnp.arange(RT_HEADS, dtype=jnp.float32)))
    lg = jnp.repeat(log_gamma, RT_HEAD)[None, :]
    cos, sin = rope_tables(n_ctx, L)
    fft_tabs = fft_tables(FFT_N1, 2 * L // FFT_N1)
    ctx_tabs = dense_dft_tables(n_ctx)

    c8 = jnp.zeros((8, D_MODEL), jnp.float32).at[0].set(c[0]).at[1].set(c_ctx)
    xs = jnp.concatenate([x[0], ctx[0]], axis=0)
    n_rows = L + n_ctx
    for l in range(DEPTH):
        with_ctx = l < DEPTH - 1
        mod = ada_modulation(c8, ada_w, l, ada_b[l][None, :])
        w_out_b = w_out[l].astype(jnp.bfloat16)
        w1_b, w3_b, w2_b = (w[l].astype(jnp.bfloat16) for w in (ffn_w1, ffn_w3, ffn_w2))

        z = modulated_projection(xs, mod, w_in, l, L)

        u = hyena_conv3(z, hy_conv_w[l], hy_conv_b[l][None, :], L)
        hy_w = (hy_w1[l], hy_b1[l], hy_w2[l], hy_b2[l], hy_w3[l], hy_b3[l], hy_freq[l])
        filt, colsum = hyena_filter_bank(L, *hy_w)
        y_lat = hyena_long_conv_chain(u, L, hy_bias[l], filt, colsum, fft_tabs)
        if with_ctx:
            filt_c, colsum_c = hyena_filter_bank(n_ctx, *hy_w)
            y_ctx = hyena_ctx(u[L:], filt_c, colsum_c, hy_bias[l], ctx_tabs)
        else:
            y_ctx = jnp.zeros((n_ctx, GROUP_W), jnp.float32)
        y_hy = jnp.concatenate([y_lat, y_ctx], axis=0)

        rw_r, rw_k, rw_v, rw_g, rw_kk, rw_lw, rw_kd, rw_a = rwkv_prep(
            z, rw_mu[l], rw_k_k[l], rw_k_a[l], rw_w0[l], rw_a0[l], rw_w2[l], rw_a2[l], L)
        rw_o2 = rwkv_scan(rw_r, rw_v, rw_kk, rw_lw, rw_kd, rw_a, n_ctx)
        rt_o2 = retention_scan(z, cos, sin, lg, n_ctx)
        hg_o2 = gla_scan(z, lower_bounds[:, l][:, None, :], n_ctx)

        xs = outproj_deepnorm(y_hy, rw_o2, rw_r, rw_k, rw_v, rw_g, rt_o2, hg_o2, z,
                              rw_r_k[l].reshape(1, GROUP_W), rw_gn_g[l][None, :], rw_gn_b[l][None, :],
                              hg_norm_g[l][None, :], w_out_b, xs, mod, ln_g[l, 0][None, :], ln_b[l, 0][None, :],
                              L, n_rows if with_ctx else L)
        xs = ffn_deepnorm(xs, mod, w1_b, w3_b, w2_b, ln_g[l, 1][None, :], ln_b[l, 1][None, :], L,
                          FFN_TM if with_ctx else FFN_LAST_TM)
    return xs[None]
```

```python
import functools
import math

import jax
import jax.numpy as jnp
from jax import lax
from jax.experimental import pallas as pl
from jax.experimental.pallas import tpu as pltpu

D_MODEL = 2048
DEPTH = 2
GRID_W = 64
N_MIXERS = 4
GROUP_W = D_MODEL // N_MIXERS
HY_ORDER = 2
HY_EMB = 33
HY_FAST_DECAY = 0.3
HY_SLOW_DECAY = 1.5
HY_TARGET = 1e-2
RW_HEAD = 64
RW_LORA = 96
RW_GN_EPS = 64e-5
RT_HEAD = 64
RT_HEADS = GROUP_W // RT_HEAD
ROPE_BASE = 10000.0
HG_HEAD = 128
HG_MIN_GATE = 1e-30
FFN_HIDDEN = 5632
ALPHA = (2 * DEPTH) ** 0.25
LN_EPS = 1e-6
HEAD_NORM_EPS = 1e-6

LANES = 128
CB = 512
COL_HY = 0
COL_RW = 3
COL_RT = 8
COL_HG = 12
P_IN_PAD = 17 * CB
RW_REAL = 4 * GROUP_W + 4 * RW_LORA
PROJ_TM = 1408
FFN_TM = 768
FFN_LAST_TM = 512
OUT_TM = 256
FFN_TF = 512
PREP_ROWS = 256
HALO = 8
VMEM_LIMIT = 56 * 1024 * 1024

HI = lax.Precision.HIGHEST
NN = ((1,), (0,))
NT = ((1,), (1,))
TN = ((0,), (0,))


def _params(n_axes):
    return pltpu.CompilerParams(dimension_semantics=("arbitrary",) * n_axes, vmem_limit_bytes=VMEM_LIMIT)


def _full(a):
    return pl.BlockSpec(a.shape, lambda *_: (0,) * a.ndim)


def _dot(a, b):
    return jnp.dot(a, b, precision=HI, preferred_element_type=jnp.float32)


def _split(x):
    hi = x.astype(jnp.bfloat16)
    lo = (x - hi.astype(jnp.float32)).astype(jnp.bfloat16)
    return hi, lo


def _mm3(a, b, dims=NN):
    d = lambda p, q: lax.dot_general(p, q, (dims, ((), ())), preferred_element_type=jnp.float32)
    return d(a[0], b[0]) + (d(a[0], b[1]) + d(a[1], b[0]))


def _mm1(a, b, dims=NN):
    return lax.dot_general(a, b, (dims, ((), ())), preferred_element_type=jnp.float32)


def _mm2(x, b):
    hi, lo = _split(x)
    d = functools.partial(jnp.dot, preferred_element_type=jnp.float32)
    return d(hi, b) + d(lo, b)


def _ln_rows(x):
    mu = jnp.mean(x, axis=-1, keepdims=True)
    xc = x - mu
    var = jnp.mean(xc * xc, axis=-1, keepdims=True)
    return xc * lax.rsqrt(var + LN_EPS)


def _sigmoid(x):
    return 1.0 / (1.0 + jnp.exp(-x))


def _ctx_rows(tm, n_lat):
    return (pl.program_id(0) * tm + lax.broadcasted_iota(jnp.int32, (tm, 1), 0)) >= n_lat


def _mod_row(ref, is_ctx):
    return jnp.where(is_ctx, ref[1:2, :], ref[0:1, :])


def _block_indicator(width, head, value):
    i = jnp.arange(width) // head
    return jnp.where(i[:, None] == i[None, :], value, 0.0).astype(jnp.bfloat16)


def _ada_kernel(c_ref, w_ref, b_ref, o_ref):
    c = c_ref[...]
    h = c * _sigmoid(c)
    o_ref[...] = jnp.dot(h.astype(jnp.bfloat16), w_ref[...].astype(jnp.bfloat16),
                         preferred_element_type=jnp.float32) + b_ref[...]


def ada_modulation(c8, w, layer, b):
    n = w.shape[2]
    tn = 1024
    return pl.pallas_call(
        _ada_kernel,
        grid=(n // tn,),
        in_specs=[pl.BlockSpec((8, D_MODEL), lambda j: (0, 0)),
                  pl.BlockSpec((None, D_MODEL, tn), lambda j: (layer, 0, j)),
                  pl.BlockSpec((1, tn), lambda j: (0, j))],
        out_specs=pl.BlockSpec((8, tn), lambda j: (0, j)),
        out_shape=jax.ShapeDtypeStruct((8, n), jnp.float32),
        compiler_params=_params(1),
    )(c8, w, b)


W_SUB = CB // LANES
W_PAD_PIECE = (COL_RW * CB + RW_REAL) // LANES


def _proj_kernel(x_ref, sh_ref, sc_ref, *refs, n_lat):
    w_refs, (o_ref, h_ref) = refs[:W_SUB], refs[W_SUB:]
    j = pl.program_id(1)

    @pl.when(j == 0)
    def _():
        is_ctx = _ctx_rows(x_ref.shape[0], n_lat)
        h = _ln_rows(x_ref[...]) * (1.0 + _mod_row(sc_ref, is_ctx)) + _mod_row(sh_ref, is_ctx)
        h_ref[...] = h.astype(jnp.bfloat16)

    pieces = [w_refs[k][...] for k in range(W_SUB)]
    for k in range(W_SUB):
        if (W_PAD_PIECE - k) % W_SUB == 0:
            pieces[k] = jnp.where(j * W_SUB + k == W_PAD_PIECE, 0.0, pieces[k])
    w = jnp.concatenate(pieces, axis=1).astype(jnp.bfloat16)
    o_ref[...] = jnp.dot(h_ref[...], w, preferred_element_type=jnp.float32)


def modulated_projection(x, mod, w, layer, n_lat):
    m = x.shape[0]
    tm = PROJ_TM

    def piece(k):
        def index(i, j):
            p = j * W_SUB + k
            return layer, 0, jnp.where(p < W_PAD_PIECE, p, p - 1)
        return pl.BlockSpec((None, D_MODEL, LANES), index)

    return pl.pallas_call(
        functools.partial(_proj_kernel, n_lat=n_lat),
        grid=(m // tm, P_IN_PAD // CB),
        in_specs=[pl.BlockSpec((tm, D_MODEL), lambda i, j: (i, 0)),
                  pl.BlockSpec((8, D_MODEL), lambda i, j: (0, 0)),
                  pl.BlockSpec((8, D_MODEL), lambda i, j: (0, 1))] + [piece(k) for k in range(W_SUB)],
        out_specs=pl.BlockSpec((tm, CB), lambda i, j: (i, j)),
        out_shape=jax.ShapeDtypeStruct((m, P_IN_PAD), jnp.float32),
        scratch_shapes=[pltpu.VMEM((tm, D_MODEL), jnp.bfloat16)],
        compiler_params=_params(2),
    )(x, mod, mod, *([w] * W_SUB))


def _halo_specs(tp, n_rows, col):
    per = tp // HALO
    last = n_rows // HALO - 1
    main = pl.BlockSpec((tp, CB), lambda i, *_: (i, col(*_) if callable(col) else col))
    prev = pl.BlockSpec((HALO, CB), lambda i, *_: (jnp.maximum(i * per - 1, 0), col(*_) if callable(col) else col))
    nxt = pl.BlockSpec((HALO, CB), lambda i, *_: (jnp.minimum((i + 1) * per, last), col(*_) if callable(col) else col))
    return [main, prev, nxt]


def _neighbours(x, prev, nxt, n_lat, n_rows):
    tp = x.shape[0]
    loc = lax.broadcasted_iota(jnp.int32, (tp, 1), 0)
    row = pl.program_id(0) * tp + loc
    xp = jnp.where(loc == 0, prev[HALO - 1:HALO, :], pltpu.roll(x, 1, axis=0))
    xp = jnp.where((row == 0) | (row == n_lat), 0.0, xp)
    xn = jnp.where(loc == tp - 1, nxt[0:1, :], pltpu.roll(x, tp - 1, axis=0))
    xn = jnp.where((row == n_lat - 1) | (row == n_rows - 1), 0.0, xn)
    return xp, xn


def _conv3_kernel(z_ref, zp_ref, zn_ref, w_ref, b_ref, o_ref, *, n_lat, n_rows):
    x = z_ref[...]
    xp, xn = _neighbours(x, zp_ref[...], zn_ref[...], n_lat, n_rows)
    w = w_ref[...]
    o_ref[...] = xp * w[0:1, :] + x * w[1:2, :] + xn * w[2:3, :] + b_ref[...]


def hyena_conv3(z, w, b, n_lat):
    n_rows = z.shape[0]
    tp = PREP_ROWS
    ncol = w.shape[1] // CB
    return pl.pallas_call(
        functools.partial(_conv3_kernel, n_lat=n_lat, n_rows=n_rows),
        grid=(n_rows // tp, ncol),
        in_specs=_halo_specs(tp, n_rows, lambda c: COL_HY + c)
        + [pl.BlockSpec((3, CB), lambda i, c: (0, c)), pl.BlockSpec((1, CB), lambda i, c: (0, c))],
        out_specs=pl.BlockSpec((tp, CB), lambda i, c: (i, c)),
        out_shape=jax.ShapeDtypeStruct((n_rows, w.shape[1]), jnp.float32),
        compiler_params=_params(2),
    )(z, z, z, w, b)


HY_EMB_PAD = 40
HY_FILTER_ROWS = 512


def hyena_features(L):
    t = jnp.linspace(0.0, 1.0, L, dtype=jnp.float32)[:, None]
    n_bands = (HY_EMB - 1) // 2
    f = jnp.linspace(1e-4, n_bands - 1, n_bands, dtype=jnp.float32)[None, :]
    ang = (2.0 * math.pi / L) * jnp.arange(L, dtype=jnp.float32)[:, None] * f
    z = jnp.concatenate([t, jnp.cos(ang), -jnp.sin(ang)], -1)
    return jnp.pad(z, ((0, 0), (0, HY_EMB_PAD - HY_EMB)))


def _filter_kernel(z_ref, w1_ref, b1_ref, w2_ref, b2_ref, w3_ref, b3_ref, fr_ref, dl_ref, h_ref, s_ref):
    i = pl.program_id(0)
    z = z_ref[...]
    fr = fr_ref[...]
    h = jnp.sin(fr * (_dot(z, w1_ref[...]) + b1_ref[...]))
    h = jnp.sin(fr * (_dot(h, w2_ref[...]) + b2_ref[...]))
    h = _mm1(h.astype(jnp.bfloat16), w3_ref[...].astype(jnp.bfloat16)) + b3_ref[...]
    win = jnp.exp(-z[:, 0:1] * dl_ref[...])
    h = h * jnp.concatenate([win] * (h.shape[1] // win.shape[1]), axis=1)

    @pl.when(i == 0)
    def _():
        s_ref[...] = jnp.zeros_like(s_ref)

    s_ref[...] += jnp.sum(jnp.abs(h), axis=0, keepdims=True)
    row = lax.broadcasted_iota(jnp.int32, h.shape, 0) + i * h.shape[0]
    col = lax.broadcasted_iota(jnp.int32, h.shape, 1)
    neg = (col // GROUP_W) % 2 == 1
    h_ref[...] = jnp.where(neg & (row == 0), 0.0, h)


def hyena_filter_bank(L, w1, b1, w2, b2, w3, b3, freq):
    z = hyena_features(L)
    w1p = jnp.pad(w1, ((0, HY_EMB_PAD - HY_EMB), (0, 0)))
    max_decay = math.log(HY_TARGET) / HY_FAST_DECAY
    min_decay = math.log(HY_TARGET) / HY_SLOW_DECAY
    deltas = jnp.abs(jnp.linspace(min_decay, max_decay, GROUP_W, dtype=jnp.float32))[None, :]
    n = w3.shape[1]
    tr = min(L, HY_FILTER_ROWS)
    args = (z, w1p, b1[None, :], w2, b2[None, :], w3, b3[None, :], freq[None, :], deltas)
    return pl.pallas_call(
        _filter_kernel,
        grid=(L // tr,),
        in_specs=[pl.BlockSpec((tr, HY_EMB_PAD), lambda i: (i, 0))] + [_full(a) for a in args[1:]],
        out_specs=[pl.BlockSpec((tr, n), lambda i: (i, 0)), pl.BlockSpec((1, n), lambda i: (0, 0))],
        out_shape=[jax.ShapeDtypeStruct((L, n), jnp.float32), jax.ShapeDtypeStruct((1, n), jnp.float32)],
        compiler_params=_params(1),
    )(*args)


FFT_NB = 8
FFT_CB = 256
FFT_N1 = 128


def fft_tables(n1, n2):
    N = n1 * n2
    nk = -(-(n2 // 2 + 1) // FFT_NB) * FFT_NB
    a = jnp.arange(n1, dtype=jnp.float32)[:, None, None]
    k2 = jnp.arange(nk, dtype=jnp.float32)[None, :, None]
    b = jnp.arange(n2 // 2, dtype=jnp.float32)[None, None, :]
    ph = (jnp.mod(a * k2, float(N)) / N + jnp.mod(b * k2, float(n2)) / n2) * (-2.0 * math.pi)
    g = jnp.concatenate([jnp.cos(ph), jnp.sin(ph)], axis=1)
    kk = jnp.arange(nk)
    weight = jnp.where((kk == 0) | (kk == n2 // 2), 1.0, jnp.where(kk < n2 // 2, 2.0, 0.0))
    ginv = jnp.transpose(g, (0, 2, 1)) * (jnp.tile(weight, 2) / N)
    k1 = jnp.arange(n1, dtype=jnp.float32)[:, None]
    aa = jnp.arange(n1, dtype=jnp.float32)[None, :]
    f = jnp.mod(k1 * aa, float(n1)) * (-2.0 * math.pi / n1)
    fr, fi = jnp.cos(f), jnp.sin(f)
    ff = jnp.concatenate([jnp.concatenate([fr, -fi], axis=1), jnp.concatenate([fi, fr], axis=1)], axis=0)
    bf = lambda t: t.astype(jnp.bfloat16)
    return {"g": bf(g), "ginv": bf(ginv), "ff": bf(ff), "fft": bf(ff.T)}


def _stage_a_kernel(u_ref, g_ref, o_ref):
    for j in range(FFT_NB):
        o_ref[j] = _mm1(g_ref[j], u_ref[:, j, :].astype(jnp.bfloat16))


def fft_stage_a(u3, nb, col, ncol, g):
    n1 = u3.shape[1]
    rows = g.shape[1]
    gspec = pl.BlockSpec((FFT_NB, rows, nb), lambda c, i: (i, 0, 0))
    return pl.pallas_call(
        _stage_a_kernel,
        grid=(ncol, n1 // FFT_NB),
        in_specs=[pl.BlockSpec((nb, FFT_NB, CB), lambda c, i: (0, i, col + c)), gspec],
        out_specs=pl.BlockSpec((FFT_NB, rows, CB), lambda c, i: (i, 0, c)),
        out_shape=jax.ShapeDtypeStruct((n1, rows, ncol * CB), jnp.float32),
        compiler_params=_params(2),
    )(u3, g)


def _stage_b_filter_kernel(re_ref, im_ref, f_ref, o_ref):
    ff = f_ref[...]
    for j in range(FFT_NB):
        o_ref[j] = _mm1(ff, jnp.concatenate([re_ref[:, j, :], im_ref[:, j, :]], axis=0).astype(jnp.bfloat16))


def fft_stage_b_filter(t1, ff):
    n1, n2x2, C = t1.shape
    n2 = n2x2 // 2
    blk = lambda off: pl.BlockSpec((n1, FFT_NB, FFT_CB), lambda c, i: (0, i + off, c))
    mat = pl.BlockSpec((2 * n1, 2 * n1), lambda c, i: (0, 0))
    return pl.pallas_call(
        _stage_b_filter_kernel,
        grid=(C // FFT_CB, n2 // FFT_NB),
        in_specs=[blk(0), blk(n2 // FFT_NB), mat],
        out_specs=pl.BlockSpec((FFT_NB, 2 * n1, FFT_CB), lambda c, i: (i, 0, c)),
        out_shape=jax.ShapeDtypeStruct((n2, 2 * n1, C), jnp.float32),
        compiler_params=_params(2),
    )(t1, t1, ff)


def _filter_spectrum(p, q, s, n1):
    return (p[:n1] + q[:n1]) * s, (p[n1:] - q[n1:]) * s


def _stage_b_conv_kernel(re_ref, im_ref, p_ref, q_ref, s_ref, f_ref, ft_ref, ore_ref, oim_ref):
    n1 = re_ref.shape[0]
    s = s_ref[...]
    ff, fft_ = f_ref[...], ft_ref[...]
    bf = lambda t: t.astype(jnp.bfloat16)
    for j in range(FFT_NB):
        x = _mm1(ff, bf(jnp.concatenate([re_ref[:, j, :], im_ref[:, j, :]], axis=0)))
        hr, hi = _filter_spectrum(p_ref[j], q_ref[j], s, n1)
        xr, xi = x[:n1], x[n1:]
        z = _mm1(fft_, bf(jnp.concatenate([xr * hr - xi * hi, xr * hi + xi * hr], axis=0)))
        ore_ref[:, j, :] = z[:n1]
        oim_ref[:, j, :] = z[n1:]


def fft_stage_b_conv(t1, spec, col_p, col_q, inv_norm, ff, fft_):
    n1, n2x2, C = t1.shape
    n2 = n2x2 // 2
    ncb = C // FFT_CB
    blk = lambda off: pl.BlockSpec((n1, FFT_NB, FFT_CB), lambda c, i: (0, i + off, c))
    mat = pl.BlockSpec((2 * n1, 2 * n1), lambda c, i: (0, 0))
    sp = lambda col: pl.BlockSpec((FFT_NB, 2 * n1, FFT_CB), lambda c, i: (i, 0, col * ncb + c))
    return pl.pallas_call(
        _stage_b_conv_kernel,
        grid=(ncb, n2 // FFT_NB),
        in_specs=[blk(0), blk(n2 // FFT_NB), sp(col_p), sp(col_q), pl.BlockSpec((1, FFT_CB), lambda c, i: (0, c)),
                  mat, mat],
        out_specs=[blk(0), blk(0)],
        out_shape=[jax.ShapeDtypeStruct((n1, n2, C), jnp.float32)] * 2,
        compiler_params=_params(2),
    )(t1, t1, spec, spec, inv_norm, ff, fft_)


def _stage_a_inv_kernel(re_ref, im_ref, g_ref, u_ref, gate_ref, bias_ref, o_ref):
    for j in range(FFT_NB):
        y = _mm1(g_ref[j], jnp.concatenate([re_ref[j], im_ref[j]], axis=0).astype(jnp.bfloat16))
        o_ref[:, j, :] = gate_ref[:, j, :] * (y + u_ref[:, j, :] * bias_ref[...])


def fft_stage_a_inv(t2re, t2im, ginv, nb, u3, u_col, gate3, gate_col, bias):
    n1, nk, C = t2re.shape
    tb = pl.BlockSpec((FFT_NB, nk, CB), lambda c, i: (i, 0, c))
    gb = pl.BlockSpec((FFT_NB, nb, 2 * nk), lambda c, i: (i, 0, 0))
    ub = lambda col: pl.BlockSpec((nb, FFT_NB, CB), lambda c, i: (0, i, col + c))
    return pl.pallas_call(
        _stage_a_inv_kernel,
        grid=(C // CB, n1 // FFT_NB),
        in_specs=[tb, tb, gb, ub(u_col), ub(gate_col), pl.BlockSpec((1, CB), lambda c, i: (0, c))],
        out_specs=ub(0),
        out_shape=jax.ShapeDtypeStruct((nb, n1, C), jnp.float32),
        compiler_params=_params(2),
    )(t2re, t2im, ginv, u3, gate3, bias)


def hyena_long_conv_chain(u, L, biases, filt, colsum, tabs):
    C = GROUP_W
    n1 = FFT_N1
    n2 = 2 * L // n1
    nb = n2 // 2
    spec = fft_stage_b_filter(fft_stage_a(filt.reshape(nb, n1, filt.shape[1]), nb, 0, filt.shape[1] // CB, tabs["g"]),
                              tabs["ff"])
    s4 = colsum.reshape(HY_ORDER, 2, C)
    inv_norm = 1.0 / (s4[:, 0] + s4[:, 1])
    u3 = u.reshape(u.shape[0] // n1, n1, u.shape[1])
    y3, y_col = u3, 0
    for n in range(HY_ORDER):
        t1 = fft_stage_a(y3, nb, y_col, 1, tabs["g"])
        t2re, t2im = fft_stage_b_conv(t1, spec, 2 * n, 2 * n + 1, inv_norm[n][None, :], tabs["ff"], tabs["fft"])
        y3 = fft_stage_a_inv(t2re, t2im, tabs["ginv"], nb, y3, y_col, u3, n + 1, biases[n][None, :])
        y_col = 0
    return y3.reshape(L, C)


def dense_dft_tables(n):
    N = 2 * n
    k = jnp.arange(N, dtype=jnp.float32)[:, None]
    t = jnp.arange(n, dtype=jnp.float32)[None, :]
    ph = jnp.mod(k * t, float(N)) * (2.0 * math.pi / N)
    fd = jnp.concatenate([jnp.cos(ph), -jnp.sin(ph)], axis=0)
    return fd.astype(jnp.bfloat16), (fd.T / N).astype(jnp.bfloat16)


def _hyena_ctx_kernel(u_ref, h_ref, s_ref, bias_ref, fd_ref, ft_ref, o_ref):
    C = GROUP_W
    fd, ft = fd_ref[...], ft_ref[...]
    K = fd.shape[0] // 2
    bf = lambda t: t.astype(jnp.bfloat16)
    hs = _mm1(fd, bf(h_ref[...]))
    s = s_ref[...]
    y = u_ref[:, 0:C]
    for n in range(HY_ORDER):
        cp, cq = 2 * n * C, (2 * n + 1) * C
        inv = 1.0 / (s[:, cp:cp + C] + s[:, cq:cq + C])
        hr, hi = _filter_spectrum(hs[:, cp:cp + C], hs[:, cq:cq + C], inv, K)
        x = _mm1(fd, bf(y))
        xr, xi = x[:K], x[K:]
        conv = _mm1(ft, bf(jnp.concatenate([xr * hr - xi * hi, xr * hi + xi * hr], axis=0)))
        y = u_ref[:, (n + 1) * C:(n + 2) * C] * (conv + y * bias_ref[n:n + 1, :])
    o_ref[...] = y


def hyena_ctx(u, filt, colsum, biases, tabs):
    args = (u, filt, colsum, biases, tabs[0], tabs[1])
    return pl.pallas_call(
        _hyena_ctx_kernel,
        in_specs=[_full(a) for a in args],
        out_specs=pl.BlockSpec((u.shape[0], GROUP_W), lambda: (0, 0)),
        out_shape=jax.ShapeDtypeStruct((u.shape[0], GROUP_W), jnp.float32),
        compiler_params=pltpu.CompilerParams(vmem_limit_bytes=VMEM_LIMIT),
    )(*args)


SCAN_T = 64
GROUP_LANES = 256
RW_SUB = 16
HG_SUB = 16
LOG2_E = 1.4426950408889634


def _chunk_index(d, i, n_ctx, n_all):
    fwd = jnp.where(i < n_ctx, n_all - n_ctx + i, i - n_ctx)
    return jnp.where(d == 0, fwd, n_all - 1 - i)


def _stacking(T, S, nh, head, sign):
    G = nh * head
    nb = T // S
    n = nh * T
    rr = lax.broadcasted_iota(jnp.int32, (n, G), 0)
    same = ((rr // S) % nh) == (lax.broadcasted_iota(jnp.int32, (n, G), 1) // head)

    def bd(x):
        pieces = []
        for i in range(nb):
            pieces += [x[i * S:(i + 1) * S]] * nh
        return jnp.where(same, jnp.concatenate(pieces, axis=0), 0.0)

    def collapse(o):
        outs = []
        for i in range(nb):
            acc = o[i * nh * S:i * nh * S + S]
            for h in range(1, nh):
                acc = acc + o[i * nh * S + h * S:i * nh * S + (h + 1) * S]
            outs.append(acc)
        return jnp.concatenate(outs, axis=0)

    rt = lax.broadcasted_iota(jnp.int32, (n, n), 0)
    cs = lax.broadcasted_iota(jnp.int32, (n, n), 1)
    t_r = (rt // (nh * S)) * S + rt % S
    t_c = (cs // (nh * S)) * S + cs % S
    same_h = ((rt // S) % nh) == ((cs // S) % nh)
    before = same_h & ((t_r - t_c) * sign > 0)
    return bd, collapse, before, rt, cs


def _softplus(x):
    return jnp.maximum(x, 0.0) + jnp.log(1.0 + jnp.exp(-jnp.abs(x)))


def _rwkv_prep_kernel(*refs, n_lat, n_rows):
    zrefs, rest = refs[:15], refs[15:]
    (mu_ref, kk_w_ref, ka_ref, w0_ref, a0_ref, w2h_ref, w2l_ref, a2h_ref, a2l_ref, ones_ref,
     r_ref, k_ref, v_ref, g_ref, kk_ref, lw_ref, kd_ref, a_ref) = rest
    slabs = []
    for c in range(5):
        x = zrefs[3 * c][...]
        xp, xn = _neighbours(x, zrefs[3 * c + 1][...], zrefs[3 * c + 2][...], n_lat, n_rows)
        slabs.append(x + (0.5 * (xp + xn) - x) * mu_ref[:, c * CB:(c + 1) * CB])
    r, k, v, g, lora = slabs
    r_ref[...], k_ref[...], v_ref[...], g_ref[...] = r, k, v, g
    kk = k * kk_w_ref[...]
    ss = _mm2(kk * kk, ones_ref[...])
    kk_ref[...] = kk * lax.rsqrt(jnp.maximum(ss, 1e-24))
    lora_t = _split(jnp.tanh(lora))
    lora_s = _split(lora)
    for d in range(2):
        w_log = -_softplus(-(w0_ref[d:d + 1, :] + _mm3(lora_t, (w2h_ref[d], w2l_ref[d])))) - 0.5
        lw_ref[d] = -jnp.exp(w_log)
        a = _sigmoid(a0_ref[d:d + 1, :] + _mm3(lora_s, (a2h_ref[d], a2l_ref[d])))
        a_ref[d] = a
        kd_ref[d] = k * (1.0 + (a - 1.0) * ka_ref[...])


def rwkv_prep(z, mu, k_k, k_a, w0, a0, w2, a2, n_lat):
    n_rows = z.shape[0]
    tp = PREP_ROWS
    mu_p = jnp.pad(mu, (0, 5 * CB - RW_REAL))[None, :]
    w2p = jnp.zeros((2, CB, GROUP_W), jnp.float32)
    a2p = jnp.zeros((2, CB, GROUP_W), jnp.float32)
    for d in range(2):
        w2p = w2p.at[d, d * RW_LORA:(d + 1) * RW_LORA].set(w2[d])
        a2p = a2p.at[d, (2 + d) * RW_LORA:(3 + d) * RW_LORA].set(a2[d])
    w2s, a2s = _split(w2p), _split(a2p)
    small = (mu_p, k_k[None, :], k_a[None, :], w0, a0, w2s[0], w2s[1], a2s[0], a2s[1],
             _block_indicator(GROUP_W, RW_HEAD, 1.0))
    zspecs = []
    for c in range(5):
        zspecs += _halo_specs(tp, n_rows, COL_RW + c)
    one = pl.BlockSpec((tp, CB), lambda i: (i, 0))
    two = pl.BlockSpec((2, tp, CB), lambda i: (0, i, 0))
    s1 = jax.ShapeDtypeStruct((n_rows, GROUP_W), jnp.float32)
    s2 = jax.ShapeDtypeStruct((2, n_rows, GROUP_W), jnp.float32)
    return pl.pallas_call(
        functools.partial(_rwkv_prep_kernel, n_lat=n_lat, n_rows=n_rows),
        grid=(n_rows // tp,),
        in_specs=zspecs + [_full(a) for a in small],
        out_specs=[one] * 5 + [two] * 3,
        out_shape=[s1] * 5 + [s2] * 3,
        compiler_params=_params(1),
    )(*([z] * 15), *small)


def _rwkv_scan_kernel(*refs, head):
    in_f, in_b, (of_ref, ob_ref, ht_ref) = refs[:6], refs[6:12], refs[12:]
    T, G, S = SCAN_T, GROUP_LANES, RW_SUB
    nh = G // head
    nb = T // S
    n = nh * T

    @pl.when(pl.program_id(0) == 0)
    def _():
        ht_ref[...] = jnp.zeros_like(ht_ref)

    ti = lax.broadcasted_iota(jnp.int32, (T, T), 0)
    si = lax.broadcasted_iota(jnp.int32, (T, T), 1)
    bf = lambda t: t.astype(jnp.bfloat16)
    each = lambda f, *seqs: [f(*xs) for xs in zip(*seqs)]
    n_grp = in_f[0].shape[1] // G
    lanes = [slice(grp * G, (grp + 1) * G) for grp in range(n_grp)]

    r, v, kk, lw, k, a, tri, before, incl, dest = ([] for _ in range(10))
    for (r_ref, v_ref, kk_ref, lw_ref, k_ref, a_ref), o_ref, sign in ((in_f, of_ref, 1), (in_b, ob_ref, -1)):
        bd, collapse, before_d, rt, cs = _stacking(T, S, nh, head, sign)
        for ls in lanes:
            r.append(r_ref[:, ls]), v.append(v_ref[:, ls]), kk.append(kk_ref[:, ls])
            lw.append(lw_ref[0, :, ls]), k.append(k_ref[0, :, ls]), a.append(a_ref[0, :, ls])
            tri.append(jnp.where((ti - si) * sign >= 0, 1.0, 0.0))
            before.append(before_d), incl.append(before_d | (rt == cs)), dest.append((o_ref, ls))
    diag_blk = (rt // (nh * S)) == (cs // (nh * S))
    eye = jnp.where(rt == cs, 1.0, 0.0)
    ht = [ht_ref[ch] for ch in range(len(r))]

    c = each(lambda tr, t: _dot(tr, t), tri, lw)
    ctot = each(lambda t: jnp.sum(t, axis=0, keepdims=True), lw)
    beta = each(lambda p, q: p * q, kk, a)
    einv = each(lambda t: jnp.exp(-t), c)
    lhs = each(lambda kk_, r_, c_, lw_: bf(jnp.concatenate([bd(-kk_ * jnp.exp(c_ - lw_)), bd(r_ * jnp.exp(c_))], axis=0)),
               kk, r, c, lw)
    rhs = each(lambda k_, b_, e_: bf(jnp.concatenate([bd(k_ * e_), bd(b_ * e_)], axis=0)), k, beta, einv)
    m = each(lambda p, q: _mm1(p, q, NT), lhs, rhs)
    g = each(lambda p, h: _mm1(p, bf(h), NT), lhs, ht)
    v_b = each(lambda t: bf(bd(t)), v)
    x = each(lambda g_, m_, v_, bm: g_[:n] + _mm1(bf(jnp.where(bm, m_[:n, :n], 0.0)), v_), g, m, v_b, before)
    lb = each(lambda m_, bm: jnp.where(bm, m_[:n, n:], 0.0), m, before)
    ld = each(lambda t: jnp.where(diag_blk, t, 0.0), lb)
    lo = each(lambda p, q: p - q, lb, ld)
    xd = each(lambda t: eye + t, ld)
    lp = each(bf, ld)
    p = 2
    while p < S:
        lp = each(lambda t: bf(_mm1(t, t)), lp)
        xd = each(lambda x_, l_: x_ + _mm1(l_, bf(x_)), xd, lp)
        p *= 2
    xd_b = each(bf, xd)
    n_b = each(lambda x_, l_: bf(_mm1(x_, bf(l_))), xd_b, lo)
    y = each(lambda x_, t: _mm1(x_, bf(t)), xd_b, x)
    y = each(lambda y_, n_: y_ + _mm1(n_, bf(y_)), y, n_b)
    p = 2
    while p < nb:
        n_b = each(lambda t: bf(_mm1(t, t)), n_b)
        y = each(lambda y_, n_: y_ + _mm1(n_, bf(y_)), y, n_b)
        p *= 2
    u_b = each(bf, y)
    o = each(lambda g_, m_, v_, u_, im: (g_[n:] + _mm1(bf(jnp.where(im, m_[n:, :n], 0.0)), v_))
             + _mm1(bf(jnp.where(im, m_[n:, n:], 0.0)), u_), g, m, v_b, u_b, incl)
    efin = each(lambda ct, c_: jnp.exp(ct - c_), ctot, c)
    ht_new = each(lambda h, ct, v_, u_, k_, b_, e_: (h * jnp.exp(ct) + _mm1(v_, bf(bd(k_ * e_)), TN))
                  + _mm1(u_, bf(bd(b_ * e_)), TN), ht, ctot, v_b, u_b, k, beta, efin)
    for ch, (o_ref, ls) in enumerate(dest):
        o_ref[:, ls] = collapse(o[ch])
        ht_ref[ch] = ht_new[ch]


def rwkv_scan(r, v, kk, lw2, k2, a2, n_ctx_rows):
    N, W = r.shape
    T = SCAN_T
    n_all, n_ctx = N // T, n_ctx_rows // T
    shared = lambda d: pl.BlockSpec((T, W), lambda i: (_chunk_index(d, i, n_ctx, n_all), 0))
    per_dir = lambda d: pl.BlockSpec((1, T, W), lambda i: (d, _chunk_index(d, i, n_ctx, n_all), 0))
    ins = lambda d: [shared(d)] * 3 + [per_dir(d)] * 3
    return pl.pallas_call(
        functools.partial(_rwkv_scan_kernel, head=RW_HEAD),
        grid=(n_all,),
        in_specs=ins(0) + ins(1),
        out_specs=[shared(0), shared(1)],
        out_shape=[jax.ShapeDtypeStruct((N, W), jnp.float32)] * 2,
        scratch_shapes=[pltpu.VMEM((2 * (W // GROUP_LANES), GROUP_LANES, GROUP_LANES), jnp.float32)],
        compiler_params=_params(1),
    )(*([r, v, kk, lw2, k2, a2] * 2))


def rope_tables(n_ctx, L):
    d_axis = RT_HEAD // 2
    half = d_axis // 2
    inv = ROPE_BASE ** (-jnp.arange(0, d_axis, 2, dtype=jnp.float32) / d_axis)
    t = jnp.arange(L)
    pos = jnp.stack([(t // GRID_W).astype(jnp.float32), (t % GRID_W).astype(jnp.float32)], axis=1)
    j = jnp.arange(RT_HEAD)
    ang = pos[:, j // d_axis] * inv[j % half][None, :]
    sgn = jnp.where((j % d_axis) < half, -1.0, 1.0)[None, :]
    cos = jnp.concatenate([jnp.cos(ang), jnp.ones((n_ctx, RT_HEAD), jnp.float32)], axis=0)
    sin = jnp.concatenate([jnp.sin(ang) * sgn, jnp.zeros((n_ctx, RT_HEAD), jnp.float32)], axis=0)
    return jnp.tile(cos, (1, RT_HEADS)), jnp.tile(sin, (1, RT_HEADS))


def _rotate(x, cos, sin):
    G = x.shape[1]
    half = RT_HEAD // 4
    lane = lax.broadcasted_iota(jnp.int32, x.shape, 1)
    partner = jnp.where((lane % (2 * half)) < half, pltpu.roll(x, G - half, axis=1), pltpu.roll(x, half, axis=1))
    return x * cos + partner * sin


def _retention_stages(q_ref, k_ref, v_ref, cos_ref, sin_ref, lg_ref, o_ref, st_ref, *, head):
    T, G = SCAN_T, GROUP_LANES
    nh = G // head
    d = pl.program_id(0)
    sign = 1 - 2 * d

    @pl.when(pl.program_id(1) == 0)
    def _():
        st_ref[...] = jnp.zeros_like(st_ref)

    t = lax.broadcasted_iota(jnp.int32, (T, 1), 0)
    pos = (t + d * (T - 1 - 2 * t) + 1).astype(jnp.float32)
    bd, collapse, before, rt, cs = _stacking(T, T, nh, head, sign)
    incl = before | (rt == cs)
    bf = lambda x: x.astype(jnp.bfloat16)

    groups = range(q_ref.shape[1] // G)
    lanes = [slice(grp * G, (grp + 1) * G) for grp in groups]
    each = lambda f, *seqs: [f(*xs) for xs in zip(*seqs)]
    st = [st_ref[grp] for grp in groups]
    cos, sin, lg = ([ref[:, ls] for ls in lanes] for ref in (cos_ref, sin_ref, lg_ref))
    q = each(_rotate, [q_ref[:, ls] for ls in lanes], cos, sin)
    yield
    k = each(lambda x, c_, s_: _rotate(x, c_, s_) * (head ** -0.5), [k_ref[:, ls] for ls in lanes], cos, sin)
    c = each(lambda lg_: pos * lg_, lg)
    yield
    q_b = each(lambda q_, c_: bf(bd(q_ * jnp.exp(c_))), q, c)
    yield
    kt_b = each(lambda k_, c_: bf(bd(k_ * jnp.exp(-c_))), k, c)
    yield
    v_b = [bf(bd(v_ref[:, ls])) for ls in lanes]
    yield
    scores = each(lambda q_, k_: bf(jnp.where(incl, _mm1(q_, k_, NT), 0.0)), q_b, kt_b)
    yield
    inter = each(lambda q_, s_: _mm1(q_, bf(s_), NT), q_b, st)
    yield
    o = each(lambda g_, s_, v_: g_ + _mm1(s_, v_), inter, scores, v_b)
    yield
    st_new = each(lambda s_, lg_, v_, k_, c_: s_ * jnp.exp(float(T) * lg_)
                  + _mm1(v_, bf(bd(k_ * jnp.exp(float(T) * lg_ - c_))), TN), st, lg, v_b, k, c)
    yield
    for grp in groups:
        o_ref[0, :, lanes[grp]] = collapse(o[grp])
        st_ref[grp] = st_new[grp]


def _gla_scan_kernel(q_ref, f_ref, i_ref, lb_ref, o_ref, st_ref, *, head, weave):
    T, S = SCAN_T, HG_SUB
    W = q_ref.shape[-1]
    nh = W // head
    d = pl.program_id(0)
    sign = 1 - 2 * d

    @pl.when(pl.program_id(1) == 0)
    def _():
        st_ref[...] = jnp.zeros_like(st_ref)

    ti = lax.broadcasted_iota(jnp.int32, (S, S), 0)
    si = lax.broadcasted_iota(jnp.int32, (S, S), 1)
    tri_incl = jnp.where((ti - si) * sign >= 0, 1.0, 0.0)
    row = lax.broadcasted_iota(jnp.int32, (S, 1), 0)
    lb = lb_ref[0]

    states = [st_ref[h] for h in range(nh)]
    for j in range(T // S):
        jb = j + d * (T // S - 1 - 2 * j)
        rows = pl.ds(pl.multiple_of(jb * S, S), S)
        q = q_ref[rows, :]
        q = q * _sigmoid(q)
        v = i_ref[rows, :]
        gate = lb + (1.0 - lb) * _sigmoid(f_ref[rows, :])
        lf = jnp.log(jnp.maximum(gate, HG_MIN_GATE))
        k = 1.0 - gate
        b = _dot(tri_incl, lf)
        btot = jnp.sum(lf, axis=0, keepdims=True)
        qe = q * jnp.exp(b)
        ke = k * jnp.exp(btot - b)
        outs = []
        for h in range(nh):
            ls = slice(h * head, (h + 1) * head)
            qh, kh, vh = q[:, ls], k[:, ls], v[:, ls]
            bh = b[:, ls] * LOG2_E
            st = states[h]
            o = _mm1(qe[:, ls].astype(jnp.bfloat16), st.astype(jnp.bfloat16), NT)
            for s in range(S):
                e = jnp.exp2(bh - bh[s:s + 1, :])
                a_s = jnp.sum(qh * kh[s:s + 1, :] * e, axis=-1, keepdims=True)
                a_s = jnp.where((row - s) * sign >= 0, a_s, 0.0)
                o = o + a_s * vh[s:s + 1, :]
            outs.append(o)
            states[h] = st * jnp.exp(btot[:, ls]) + _mm1(vh.astype(jnp.bfloat16), ke[:, ls].astype(jnp.bfloat16), TN)
            next(weave, None)
        o_ref[0, rows, :] = jnp.concatenate(outs, axis=1)
    for h in range(nh):
        st_ref[h] = states[h]


def _retention_gla_kernel(*refs):
    rt_in, hg_in, (rt_o, hg_o, rt_st, hg_st) = refs[:6], refs[6:10], refs[10:]
    stages = _retention_stages(*rt_in, rt_o, rt_st, head=RT_HEAD)
    _gla_scan_kernel(*hg_in, hg_o, hg_st, head=HG_HEAD, weave=stages)
    for _ in stages:
        pass


def retention_gla_scan(z, cos, sin, lg, lb2, n_ctx_rows):
    N = z.shape[0]
    W = GROUP_W
    T = SCAN_T
    n_all, n_ctx = N // T, n_ctx_rows // T
    chunk = lambda d, i: _chunk_index(d, i, n_ctx, n_all)
    zc = lambda col: pl.BlockSpec((T, CB), lambda d, i: (chunk(d, i), col))
    tab = pl.BlockSpec((T, W), lambda d, i: (chunk(d, i), 0))
    out = pl.BlockSpec((1, T, W), lambda d, i: (d, chunk(d, i), 0))
    shape = jax.ShapeDtypeStruct((2, N, W), jnp.float32)
    return pl.pallas_call(
        _retention_gla_kernel,
        grid=(2, n_all),
        in_specs=[zc(COL_RT), zc(COL_RT + 1), zc(COL_RT + 2), tab, tab, pl.BlockSpec((1, W), lambda d, i: (0, 0)),
                  zc(COL_HG), pl.BlockSpec((T, CB), lambda d, i: (chunk(d, i), COL_HG + 1 + d)), zc(COL_HG + 3),
                  pl.BlockSpec((1, 1, W), lambda d, i: (d, 0, 0))],
        out_specs=[out, out],
        out_shape=[shape, shape],
        scratch_shapes=[pltpu.VMEM((W // GROUP_LANES, GROUP_LANES, GROUP_LANES), jnp.float32),
                        pltpu.VMEM((W // HG_HEAD, HG_HEAD, HG_HEAD), jnp.float32)],
        compiler_params=_params(2),
    )(z, z, z, cos, sin, lg, z, z, z, lb2)


def _outproj_kernel(hy_ref, rwof_ref, rwob_ref, r_ref, k_ref, v_ref, rwg_ref, rto_ref, rtg_ref, hgo_ref, hgg_ref,
                    rk_ref, gng_ref, gnb_ref, hgn_ref, avg64_ref, avg128_ref,
                    w_ref, x_ref, gate_ref, g_ref, b_ref, o_ref, *, n_lat):
    avg64, avg128 = avg64_ref[...], avg128_ref[...]

    def head_norm(o, avg, eps, centre):
        if centre:
            o = o - _mm2(o, avg)
        return o * lax.rsqrt(_mm2(o * o, avg) + eps)

    silu = lambda t: t * _sigmoid(t)
    y_rw = head_norm(rwof_ref[...] + rwob_ref[...], avg64, RW_GN_EPS, True) * gng_ref[...] + gnb_ref[...]
    bonus = (float(RW_HEAD) * _mm2(r_ref[...] * k_ref[...] * rk_ref[...], avg64)) * v_ref[...]
    y_rw = (y_rw + bonus) * _sigmoid(rwg_ref[...])
    y_rt = head_norm(rto_ref[0] + rto_ref[1], avg64, HEAD_NORM_EPS, True) * silu(rtg_ref[...])
    y_hg = head_norm(hgo_ref[0] + hgo_ref[1], avg128, HEAD_NORM_EPS, False) * hgn_ref[...] * silu(hgg_ref[...])
    y = None
    for m, ym in enumerate((hy_ref[...], y_rw, y_rt, y_hg)):
        part = jnp.dot(ym.astype(jnp.bfloat16), w_ref[m * GROUP_W:(m + 1) * GROUP_W, :],
                       preferred_element_type=jnp.float32)
        y = part if y is None else y + part
    is_ctx = _ctx_rows(x_ref.shape[0], n_lat)
    r = ALPHA * x_ref[...] + _mod_row(gate_ref, is_ctx) * y
    o_ref[...] = _ln_rows(r) * g_ref[...] + b_ref[...]


def outproj_deepnorm(y_hy, rw_o2, rw_r, rw_k, rw_v, rw_g, rt_o2, hg_o2, z, r_k, gn_g, gn_b, hg_norm_g,
                     w_bf16, x, mod, g, b, n_lat, m):
    tm = OUT_TM
    one = pl.BlockSpec((tm, GROUP_W), lambda i: (i, 0))
    two = pl.BlockSpec((2, tm, GROUP_W), lambda i: (0, i, 0))
    zc = lambda col: pl.BlockSpec((tm, CB), lambda i: (i, col))
    row = pl.BlockSpec((tm, D_MODEL), lambda i: (i, 0))
    small = (r_k, gn_g, gn_b, hg_norm_g, _block_indicator(GROUP_W, RW_HEAD, 1.0 / RW_HEAD),
             _block_indicator(GROUP_W, HG_HEAD, 1.0 / HG_HEAD))
    return pl.pallas_call(
        functools.partial(_outproj_kernel, n_lat=n_lat),
        grid=(m // tm,),
        in_specs=[one, one, one, one, one, one, one, two, zc(COL_RT + 3), two, zc(COL_HG + 4)]
        + [_full(a) for a in small]
        + [_full(w_bf16), row, pl.BlockSpec((8, D_MODEL), lambda i: (0, 2)), _full(g), _full(b)],
        out_specs=row,
        out_shape=jax.ShapeDtypeStruct((m, D_MODEL), jnp.float32),
        compiler_params=_params(1),
    )(y_hy, rw_o2[0], rw_o2[1], rw_r, rw_k, rw_v, rw_g, rt_o2, z, hg_o2, z, *small, w_bf16, x, mod, g, b)


def _ffn_kernel(x_ref, sh_ref, sc_ref, w1_ref, w3_ref, w2_ref, gate_ref, g_ref, b_ref, o_ref, h_ref, acc_ref, *, n_lat):
    j = pl.program_id(1)

    @pl.when(j == 0)
    def _():
        is_ctx = _ctx_rows(x_ref.shape[0], n_lat)
        h = _ln_rows(x_ref[...]) * (1.0 + _mod_row(sc_ref, is_ctx)) + _mod_row(sh_ref, is_ctx)
        h_ref[...] = h.astype(jnp.bfloat16)
        acc_ref[...] = jnp.zeros_like(acc_ref)

    h = h_ref[...]
    a = jnp.dot(h, w1_ref[...], preferred_element_type=jnp.float32)
    u = jnp.dot(h, w3_ref[...], preferred_element_type=jnp.float32)
    s = (a * _sigmoid(a) * u).astype(jnp.bfloat16)
    acc_ref[...] += jnp.dot(s, w2_ref[...], preferred_element_type=jnp.float32)

    @pl.when(j == pl.num_programs(1) - 1)
    def _():
        is_ctx = _ctx_rows(x_ref.shape[0], n_lat)
        r = ALPHA * x_ref[...] + _mod_row(gate_ref, is_ctx) * acc_ref[...]
        o_ref[...] = _ln_rows(r) * g_ref[...] + b_ref[...]


def ffn_deepnorm(x, mod, w1, w3, w2, g, b, n_lat, tm):
    m = x.shape[0]
    row = pl.BlockSpec((tm, D_MODEL), lambda i, j: (i, 0))
    modc = lambda c: pl.BlockSpec((8, D_MODEL), lambda i, j: (0, c))
    vec = pl.BlockSpec((1, D_MODEL), lambda i, j: (0, 0))
    return pl.pallas_call(
        functools.partial(_ffn_kernel, n_lat=n_lat),
        grid=(m // tm, FFN_HIDDEN // FFN_TF),
        in_specs=[row, modc(3), modc(4),
                  pl.BlockSpec((D_MODEL, FFN_TF), lambda i, j: (0, j)),
                  pl.BlockSpec((D_MODEL, FFN_TF), lambda i, j: (0, j)),
                  pl.BlockSpec((FFN_TF, D_MODEL), lambda i, j: (j, 0)),
                  modc(5), vec, vec],
        out_specs=row,
        out_shape=jax.ShapeDtypeStruct((m, D_MODEL), jnp.float32),
        scratch_shapes=[pltpu.VMEM((tm, D_MODEL), jnp.bfloat16),
                        pltpu.VMEM((tm, D_MODEL), jnp.float32)],
        compiler_params=_params(2),
    )(x, mod, mod, w1, w3, w2, mod, g, b)


def kernel(x, c, ctx, c_ctx, ada_w, ada_b, w_in, w_out, ln_g, ln_b, hy_conv_w, hy_conv_b, hy_w1, hy_b1, hy_w2, hy_b2, hy_w3, hy_b3, hy_freq, hy_bias, rw_mu, rw_w0, rw_w2, rw_a0, rw_a2, rw_k_k, rw_k_a, rw_r_k, rw_gn_g, rw_gn_b, hg_lb_raw, hg_norm_g, ffn_w1, ffn_w3, ffn_w2):
    L, n_ctx = x.shape[1], ctx.shape[1]
    sm = jax.nn.softmax(hg_lb_raw.astype(jnp.float32), axis=1)
    lower_bounds = jnp.cumsum(sm, axis=1) - sm[:, :1]
    log_gamma = jnp.log1p(-jnp.exp2(-5.0 - jnp.arange(RT_HEADS, dtype=jnp.float32)))
    lg = jnp.repeat(log_gamma, RT_HEAD)[None, :]
    cos, sin = rope_tables(n_ctx, L)
    fft_tabs = fft_tables(FFT_N1, 2 * L // FFT_N1)
    ctx_tabs = dense_dft_tables(n_ctx)

    c8 = jnp.zeros((8, D_MODEL), jnp.float32).at[0].set(c[0]).at[1].set(c_ctx)
    xs = jnp.concatenate([x[0], ctx[0]], axis=0)
    n_rows = L + n_ctx
    for l in range(DEPTH):
        with_ctx = l < DEPTH - 1
        mod = ada_modulation(c8, ada_w, l, ada_b[l][None, :])
        w_out_b = w_out[l].astype(jnp.bfloat16)
        w1_b, w3_b, w2_b = (w[l].astype(jnp.bfloat16) for w in (ffn_w1, ffn_w3, ffn_w2))

        z = modulated_projection(xs, mod, w_in, l, L)

        u = hyena_conv3(z, hy_conv_w[l], hy_conv_b[l][None, :], L)
        hy_w = (hy_w1[l], hy_b1[l], hy_w2[l], hy_b2[l], hy_w3[l], hy_b3[l], hy_freq[l])
        filt, colsum = hyena_filter_bank(L, *hy_w)
        y_lat = hyena_long_conv_chain(u, L, hy_bias[l], filt, colsum, fft_tabs)
        if with_ctx:
            filt_c, colsum_c = hyena_filter_bank(n_ctx, *hy_w)
            y_ctx = hyena_ctx(u[L:], filt_c, colsum_c, hy_bias[l], ctx_tabs)
        else:
            y_ctx = jnp.zeros((n_ctx, GROUP_W), jnp.float32)
        y_hy = jnp.concatenate([y_lat, y_ctx], axis=0)

        rw_r, rw_k, rw_v, rw_g, rw_kk, rw_lw, rw_kd, rw_a = rwkv_prep(
            z, rw_mu[l], rw_k_k[l], rw_k_a[l], rw_w0[l], rw_a0[l], rw_w2[l], rw_a2[l], L)
        rw_o2 = rwkv_scan(rw_r, rw_v, rw_kk, rw_lw, rw_kd, rw_a, n_ctx)
        rt_o2, hg_o2 = retention_gla_scan(z, cos, sin, lg, lower_bounds[:, l][:, None, :], n_ctx)

        xs = outproj_deepnorm(y_hy, rw_o2, rw_r, rw_k, rw_v, rw_g, rt_o2, hg_o2, z,
                              rw_r_k[l].reshape(1, GROUP_W), rw_gn_g[l][None, :], rw_gn_b[l][None, :],
                              hg_norm_g[l][None, :], w_out_b, xs, mod, ln_g[l, 0][None, :], ln_b[l, 0][None, :],
                              L, n_rows if with_ctx else L)
        xs = ffn_deepnorm(xs, mod, w1_b, w3_b, w2_b, ln_g[l, 1][None, :], ln_b[l, 1][None, :], L,
                          FFN_TM if with_ctx else FFN_LAST_TM)
    return xs[None]
```

```python
import functools
import math

import jax
import jax.numpy as jnp
from jax import lax
from jax.experimental import pallas as pl
from jax.experimental.pallas import tpu as pltpu

D_MODEL = 2048
DEPTH = 2
GRID_W = 64
N_MIXERS = 4
GROUP_W = D_MODEL // N_MIXERS
HY_ORDER = 2
HY_EMB = 33
HY_FAST_DECAY = 0.3
HY_SLOW_DECAY = 1.5
HY_TARGET = 1e-2
RW_HEAD = 64
RW_LORA = 96
RW_GN_EPS = 64e-5
RT_HEAD = 64
RT_HEADS = GROUP_W // RT_HEAD
ROPE_BASE = 10000.0
HG_HEAD = 128
HG_MIN_GATE = 1e-30
FFN_HIDDEN = 5632
ALPHA = (2 * DEPTH) ** 0.25
LN_EPS = 1e-6
HEAD_NORM_EPS = 1e-6

LANES = 128
CB = 512
COL_HY = 0
COL_RW = 3
COL_RT = 8
COL_HG = 12
P_IN_PAD = 17 * CB
RW_REAL = 4 * GROUP_W + 4 * RW_LORA
PROJ_TM = 1408
FFN_TM = 768
FFN_LAST_TM = 512
OUT_TM = 256
FFN_TF = 512
PREP_ROWS = 256
HALO = 8
VMEM_LIMIT = 56 * 1024 * 1024

HI = lax.Precision.HIGHEST
NN = ((1,), (0,))
NT = ((1,), (1,))
TN = ((0,), (0,))


def _params(n_axes):
    return pltpu.CompilerParams(dimension_semantics=("arbitrary",) * n_axes, vmem_limit_bytes=VMEM_LIMIT)


def _full(a):
    return pl.BlockSpec(a.shape, lambda *_: (0,) * a.ndim)


def _dot(a, b):
    return jnp.dot(a, b, precision=HI, preferred_element_type=jnp.float32)


def _split(x):
    hi = x.astype(jnp.bfloat16)
    lo = (x - hi.astype(jnp.float32)).astype(jnp.bfloat16)
    return hi, lo


def _mm3(a, b, dims=NN):
    d = lambda p, q: lax.dot_general(p, q, (dims, ((), ())), preferred_element_type=jnp.float32)
    return d(a[0], b[0]) + (d(a[0], b[1]) + d(a[1], b[0]))


def _mm1(a, b, dims=NN):
    return lax.dot_general(a, b, (dims, ((), ())), preferred_element_type=jnp.float32)


def _mm2(x, b):
    hi, lo = _split(x)
    d = functools.partial(jnp.dot, preferred_element_type=jnp.float32)
    return d(hi, b) + d(lo, b)


def _ln_rows(x):
    mu = jnp.mean(x, axis=-1, keepdims=True)
    xc = x - mu
    var = jnp.mean(xc * xc, axis=-1, keepdims=True)
    return xc * lax.rsqrt(var + LN_EPS)


def _sigmoid(x):
    return 1.0 / (1.0 + jnp.exp(-x))


def _ctx_rows(tm, n_lat):
    return (pl.program_id(0) * tm + lax.broadcasted_iota(jnp.int32, (tm, 1), 0)) >= n_lat


def _mod_row(ref, is_ctx):
    return jnp.where(is_ctx, ref[1:2, :], ref[0:1, :])


def _block_indicator(width, head, value):
    i = jnp.arange(width) // head
    return jnp.where(i[:, None] == i[None, :], value, 0.0).astype(jnp.bfloat16)


def _ada_kernel(c_ref, w_ref, b_ref, o_ref):
    c = c_ref[...]
    h = c * _sigmoid(c)
    o_ref[...] = jnp.dot(h.astype(jnp.bfloat16), w_ref[...].astype(jnp.bfloat16),
                         preferred_element_type=jnp.float32) + b_ref[...]


def ada_modulation(c8, w, layer, b):
    n = w.shape[2]
    tn = 1024
    return pl.pallas_call(
        _ada_kernel,
        grid=(n // tn,),
        in_specs=[pl.BlockSpec((8, D_MODEL), lambda j: (0, 0)),
                  pl.BlockSpec((None, D_MODEL, tn), lambda j: (layer, 0, j)),
                  pl.BlockSpec((1, tn), lambda j: (0, j))],
        out_specs=pl.BlockSpec((8, tn), lambda j: (0, j)),
        out_shape=jax.ShapeDtypeStruct((8, n), jnp.float32),
        compiler_params=_params(1),
    )(c8, w, b)


W_SUB = CB // LANES
W_PAD_PIECE = (COL_RW * CB + RW_REAL) // LANES


def _proj_kernel(x_ref, sh_ref, sc_ref, *refs, n_lat):
    w_refs, (o_ref, h_ref) = refs[:W_SUB], refs[W_SUB:]
    j = pl.program_id(1)

    @pl.when(j == 0)
    def _():
        is_ctx = _ctx_rows(x_ref.shape[0], n_lat)
        h = _ln_rows(x_ref[...]) * (1.0 + _mod_row(sc_ref, is_ctx)) + _mod_row(sh_ref, is_ctx)
        h_ref[...] = h.astype(jnp.bfloat16)

    pieces = [w_refs[k][...] for k in range(W_SUB)]
    for k in range(W_SUB):
        if (W_PAD_PIECE - k) % W_SUB == 0:
            pieces[k] = jnp.where(j * W_SUB + k == W_PAD_PIECE, 0.0, pieces[k])
    w = jnp.concatenate(pieces, axis=1).astype(jnp.bfloat16)
    o_ref[...] = jnp.dot(h_ref[...], w, preferred_element_type=jnp.float32)


def modulated_projection(x, mod, w, layer, n_lat):
    m = x.shape[0]
    tm = PROJ_TM

    def piece(k):
        def index(i, j):
            p = j * W_SUB + k
            return layer, 0, jnp.where(p < W_PAD_PIECE, p, p - 1)
        return pl.BlockSpec((None, D_MODEL, LANES), index)

    return pl.pallas_call(
        functools.partial(_proj_kernel, n_lat=n_lat),
        grid=(m // tm, P_IN_PAD // CB),
        in_specs=[pl.BlockSpec((tm, D_MODEL), lambda i, j: (i, 0)),
                  pl.BlockSpec((8, D_MODEL), lambda i, j: (0, 0)),
                  pl.BlockSpec((8, D_MODEL), lambda i, j: (0, 1))] + [piece(k) for k in range(W_SUB)],
        out_specs=pl.BlockSpec((tm, CB), lambda i, j: (i, j)),
        out_shape=jax.ShapeDtypeStruct((m, P_IN_PAD), jnp.float32),
        scratch_shapes=[pltpu.VMEM((tm, D_MODEL), jnp.bfloat16)],
        compiler_params=_params(2),
    )(x, mod, mod, *([w] * W_SUB))


def _halo_specs(tp, n_rows, col):
    per = tp // HALO
    last = n_rows // HALO - 1
    main = pl.BlockSpec((tp, CB), lambda i, *_: (i, col(*_) if callable(col) else col))
    prev = pl.BlockSpec((HALO, CB), lambda i, *_: (jnp.maximum(i * per - 1, 0), col(*_) if callable(col) else col))
    nxt = pl.BlockSpec((HALO, CB), lambda i, *_: (jnp.minimum((i + 1) * per, last), col(*_) if callable(col) else col))
    return [main, prev, nxt]


def _neighbours(x, prev, nxt, n_lat, n_rows):
    tp = x.shape[0]
    loc = lax.broadcasted_iota(jnp.int32, (tp, 1), 0)
    row = pl.program_id(0) * tp + loc
    xp = jnp.where(loc == 0, prev[HALO - 1:HALO, :], pltpu.roll(x, 1, axis=0))
    xp = jnp.where((row == 0) | (row == n_lat), 0.0, xp)
    xn = jnp.where(loc == tp - 1, nxt[0:1, :], pltpu.roll(x, tp - 1, axis=0))
    xn = jnp.where((row == n_lat - 1) | (row == n_rows - 1), 0.0, xn)
    return xp, xn


def _conv3_kernel(z_ref, zp_ref, zn_ref, w_ref, b_ref, o_ref, *, n_lat, n_rows):
    x = z_ref[...]
    xp, xn = _neighbours(x, zp_ref[...], zn_ref[...], n_lat, n_rows)
    w = w_ref[...]
    o_ref[...] = xp * w[0:1, :] + x * w[1:2, :] + xn * w[2:3, :] + b_ref[...]


def hyena_conv3(z, w, b, n_lat):
    n_rows = z.shape[0]
    tp = PREP_ROWS
    ncol = w.shape[1] // CB
    return pl.pallas_call(
        functools.partial(_conv3_kernel, n_lat=n_lat, n_rows=n_rows),
        grid=(n_rows // tp, ncol),
        in_specs=_halo_specs(tp, n_rows, lambda c: COL_HY + c)
        + [pl.BlockSpec((3, CB), lambda i, c: (0, c)), pl.BlockSpec((1, CB), lambda i, c: (0, c))],
        out_specs=pl.BlockSpec((tp, CB), lambda i, c: (i, c)),
        out_shape=jax.ShapeDtypeStruct((n_rows, w.shape[1]), jnp.float32),
        compiler_params=_params(2),
    )(z, z, z, w, b)


HY_EMB_PAD = 40
HY_FILTER_ROWS = 512


def hyena_features(L):
    t = jnp.linspace(0.0, 1.0, L, dtype=jnp.float32)[:, None]
    n_bands = (HY_EMB - 1) // 2
    f = jnp.linspace(1e-4, n_bands - 1, n_bands, dtype=jnp.float32)[None, :]
    ang = (2.0 * math.pi / L) * jnp.arange(L, dtype=jnp.float32)[:, None] * f
    z = jnp.concatenate([t, jnp.cos(ang), -jnp.sin(ang)], -1)
    return jnp.pad(z, ((0, 0), (0, HY_EMB_PAD - HY_EMB)))


def _filter_kernel(z_ref, w1_ref, b1_ref, w2_ref, b2_ref, w3_ref, b3_ref, fr_ref, dl_ref, h_ref, s_ref):
    i = pl.program_id(0)
    z = z_ref[...]
    fr = fr_ref[...]
    h = jnp.sin(fr * (_dot(z, w1_ref[...]) + b1_ref[...]))
    h = jnp.sin(fr * (_dot(h, w2_ref[...]) + b2_ref[...]))
    h = _mm1(h.astype(jnp.bfloat16), w3_ref[...].astype(jnp.bfloat16)) + b3_ref[...]
    win = jnp.exp(-z[:, 0:1] * dl_ref[...])
    h = h * jnp.concatenate([win] * (h.shape[1] // win.shape[1]), axis=1)

    @pl.when(i == 0)
    def _():
        s_ref[...] = jnp.zeros_like(s_ref)

    s_ref[...] += jnp.sum(jnp.abs(h), axis=0, keepdims=True)
    row = lax.broadcasted_iota(jnp.int32, h.shape, 0) + i * h.shape[0]
    col = lax.broadcasted_iota(jnp.int32, h.shape, 1)
    neg = (col // GROUP_W) % 2 == 1
    h_ref[...] = jnp.where(neg & (row == 0), 0.0, h)


def hyena_filter_bank(L, w1, b1, w2, b2, w3, b3, freq):
    z = hyena_features(L)
    w1p = jnp.pad(w1, ((0, HY_EMB_PAD - HY_EMB), (0, 0)))
    max_decay = math.log(HY_TARGET) / HY_FAST_DECAY
    min_decay = math.log(HY_TARGET) / HY_SLOW_DECAY
    deltas = jnp.abs(jnp.linspace(min_decay, max_decay, GROUP_W, dtype=jnp.float32))[None, :]
    n = w3.shape[1]
    tr = min(L, HY_FILTER_ROWS)
    args = (z, w1p, b1[None, :], w2, b2[None, :], w3, b3[None, :], freq[None, :], deltas)
    return pl.pallas_call(
        _filter_kernel,
        grid=(L // tr,),
        in_specs=[pl.BlockSpec((tr, HY_EMB_PAD), lambda i: (i, 0))] + [_full(a) for a in args[1:]],
        out_specs=[pl.BlockSpec((tr, n), lambda i: (i, 0)), pl.BlockSpec((1, n), lambda i: (0, 0))],
        out_shape=[jax.ShapeDtypeStruct((L, n), jnp.float32), jax.ShapeDtypeStruct((1, n), jnp.float32)],
        compiler_params=_params(1),
    )(*args)


FFT_NB = 8
FFT_CB = 256
FFT_N1 = 128


def fft_tables(n1, n2):
    N = n1 * n2
    nk = -(-(n2 // 2 + 1) // FFT_NB) * FFT_NB
    a = jnp.arange(n1, dtype=jnp.float32)[:, None, None]
    k2 = jnp.arange(nk, dtype=jnp.float32)[None, :, None]
    b = jnp.arange(n2 // 2, dtype=jnp.float32)[None, None, :]
    ph = (jnp.mod(a * k2, float(N)) / N + jnp.mod(b * k2, float(n2)) / n2) * (-2.0 * math.pi)
    g = jnp.concatenate([jnp.cos(ph), jnp.sin(ph)], axis=1)
    kk = jnp.arange(nk)
    weight = jnp.where((kk == 0) | (kk == n2 // 2), 1.0, jnp.where(kk < n2 // 2, 2.0, 0.0))
    ginv = jnp.transpose(g, (0, 2, 1)) * (jnp.tile(weight, 2) / N)
    k1 = jnp.arange(n1, dtype=jnp.float32)[:, None]
    aa = jnp.arange(n1, dtype=jnp.float32)[None, :]
    f = jnp.mod(k1 * aa, float(n1)) * (-2.0 * math.pi / n1)
    fr, fi = jnp.cos(f), jnp.sin(f)
    ff = jnp.concatenate([jnp.concatenate([fr, -fi], axis=1), jnp.concatenate([fi, fr], axis=1)], axis=0)
    bf = lambda t: t.astype(jnp.bfloat16)
    return {"g": bf(g), "ginv": bf(ginv), "ff": bf(ff), "fft": bf(ff.T)}


def _stage_a_kernel(u_ref, g_ref, o_ref):
    for j in range(FFT_NB):
        o_ref[j] = _mm1(g_ref[j], u_ref[:, j, :].astype(jnp.bfloat16))


def fft_stage_a(u3, nb, col, ncol, g):
    n1 = u3.shape[1]
    rows = g.shape[1]
    gspec = pl.BlockSpec((FFT_NB, rows, nb), lambda c, i: (i, 0, 0))
    return pl.pallas_call(
        _stage_a_kernel,
        grid=(ncol, n1 // FFT_NB),
        in_specs=[pl.BlockSpec((nb, FFT_NB, CB), lambda c, i: (0, i, col + c)), gspec],
        out_specs=pl.BlockSpec((FFT_NB, rows, CB), lambda c, i: (i, 0, c)),
        out_shape=jax.ShapeDtypeStruct((n1, rows, ncol * CB), jnp.float32),
        compiler_params=_params(2),
    )(u3, g)


def _stage_b_filter_kernel(re_ref, im_ref, f_ref, o_ref):
    ff = f_ref[...]
    for j in range(FFT_NB):
        o_ref[j] = _mm1(ff, jnp.concatenate([re_ref[:, j, :], im_ref[:, j, :]], axis=0).astype(jnp.bfloat16))


def fft_stage_b_filter(t1, ff):
    n1, n2x2, C = t1.shape
    n2 = n2x2 // 2
    blk = lambda off: pl.BlockSpec((n1, FFT_NB, FFT_CB), lambda c, i: (0, i + off, c))
    mat = pl.BlockSpec((2 * n1, 2 * n1), lambda c, i: (0, 0))
    return pl.pallas_call(
        _stage_b_filter_kernel,
        grid=(C // FFT_CB, n2 // FFT_NB),
        in_specs=[blk(0), blk(n2 // FFT_NB), mat],
        out_specs=pl.BlockSpec((FFT_NB, 2 * n1, FFT_CB), lambda c, i: (i, 0, c)),
        out_shape=jax.ShapeDtypeStruct((n2, 2 * n1, C), jnp.float32),
        compiler_params=_params(2),
    )(t1, t1, ff)


def _filter_spectrum(p, q, s, n1):
    return (p[:n1] + q[:n1]) * s, (p[n1:] - q[n1:]) * s


def _stage_b_conv_kernel(re_ref, im_ref, p_ref, q_ref, s_ref, f_ref, ft_ref, ore_ref, oim_ref):
    n1 = re_ref.shape[0]
    s = s_ref[...]
    ff, fft_ = f_ref[...], ft_ref[...]
    bf = lambda t: t.astype(jnp.bfloat16)
    for j in range(FFT_NB):
        x = _mm1(ff, bf(jnp.concatenate([re_ref[:, j, :], im_ref[:, j, :]], axis=0)))
        hr, hi = _filter_spectrum(p_ref[j], q_ref[j], s, n1)
        xr, xi = x[:n1], x[n1:]
        z = _mm1(fft_, bf(jnp.concatenate([xr * hr - xi * hi, xr * hi + xi * hr], axis=0)))
        ore_ref[:, j, :] = z[:n1]
        oim_ref[:, j, :] = z[n1:]


def fft_stage_b_conv(t1, spec, col_p, col_q, inv_norm, ff, fft_):
    n1, n2x2, C = t1.shape
    n2 = n2x2 // 2
    ncb = C // FFT_CB
    blk = lambda off: pl.BlockSpec((n1, FFT_NB, FFT_CB), lambda c, i: (0, i + off, c))
    mat = pl.BlockSpec((2 * n1, 2 * n1), lambda c, i: (0, 0))
    sp = lambda col: pl.BlockSpec((FFT_NB, 2 * n1, FFT_CB), lambda c, i: (i, 0, col * ncb + c))
    return pl.pallas_call(
        _stage_b_conv_kernel,
        grid=(ncb, n2 // FFT_NB),
        in_specs=[blk(0), blk(n2 // FFT_NB), sp(col_p), sp(col_q), pl.BlockSpec((1, FFT_CB), lambda c, i: (0, c)),
                  mat, mat],
        out_specs=[blk(0), blk(0)],
        out_shape=[jax.ShapeDtypeStruct((n1, n2, C), jnp.float32)] * 2,
        compiler_params=_params(2),
    )(t1, t1, spec, spec, inv_norm, ff, fft_)


def _stage_a_inv_kernel(re_ref, im_ref, g_ref, u_ref, gate_ref, bias_ref, o_ref):
    for j in range(FFT_NB):
        y = _mm1(g_ref[j], jnp.concatenate([re_ref[j], im_ref[j]], axis=0).astype(jnp.bfloat16))
        o_ref[:, j, :] = gate_ref[:, j, :] * (y + u_ref[:, j, :] * bias_ref[...])


def fft_stage_a_inv(t2re, t2im, ginv, nb, u3, u_col, gate3, gate_col, bias):
    n1, nk, C = t2re.shape
    tb = pl.BlockSpec((FFT_NB, nk, CB), lambda c, i: (i, 0, c))
    gb = pl.BlockSpec((FFT_NB, nb, 2 * nk), lambda c, i: (i, 0, 0))
    ub = lambda col: pl.BlockSpec((nb, FFT_NB, CB), lambda c, i: (0, i, col + c))
    return pl.pallas_call(
        _stage_a_inv_kernel,
        grid=(C // CB, n1 // FFT_NB),
        in_specs=[tb, tb, gb, ub(u_col), ub(gate_col), pl.BlockSpec((1, CB), lambda c, i: (0, c))],
        out_specs=ub(0),
        out_shape=jax.ShapeDtypeStruct((nb, n1, C), jnp.float32),
        compiler_params=_params(2),
    )(t2re, t2im, ginv, u3, gate3, bias)


def hyena_long_conv_chain(u, L, biases, filt, colsum, tabs):
    C = GROUP_W
    n1 = FFT_N1
    n2 = 2 * L // n1
    nb = n2 // 2
    spec = fft_stage_b_filter(fft_stage_a(filt.reshape(nb, n1, filt.shape[1]), nb, 0, filt.shape[1] // CB, tabs["g"]),
                              tabs["ff"])
    s4 = colsum.reshape(HY_ORDER, 2, C)
    inv_norm = 1.0 / (s4[:, 0] + s4[:, 1])
    u3 = u.reshape(u.shape[0] // n1, n1, u.shape[1])
    y3, y_col = u3, 0
    for n in range(HY_ORDER):
        t1 = fft_stage_a(y3, nb, y_col, 1, tabs["g"])
        t2re, t2im = fft_stage_b_conv(t1, spec, 2 * n, 2 * n + 1, inv_norm[n][None, :], tabs["ff"], tabs["fft"])
        y3 = fft_stage_a_inv(t2re, t2im, tabs["ginv"], nb, y3, y_col, u3, n + 1, biases[n][None, :])
        y_col = 0
    return y3.reshape(L, C)


def dense_dft_tables(n):
    N = 2 * n
    k = jnp.arange(N, dtype=jnp.float32)[:, None]
    t = jnp.arange(n, dtype=jnp.float32)[None, :]
    ph = jnp.mod(k * t, float(N)) * (2.0 * math.pi / N)
    fd = jnp.concatenate([jnp.cos(ph), -jnp.sin(ph)], axis=0)
    return fd.astype(jnp.bfloat16), (fd.T / N).astype(jnp.bfloat16)


def _hyena_ctx_kernel(u_ref, h_ref, s_ref, bias_ref, fd_ref, ft_ref, o_ref):
    C = GROUP_W
    fd, ft = fd_ref[...], ft_ref[...]
    K = fd.shape[0] // 2
    bf = lambda t: t.astype(jnp.bfloat16)
    hs = _mm1(fd, bf(h_ref[...]))
    s = s_ref[...]
    y = u_ref[:, 0:C]
    for n in range(HY_ORDER):
        cp, cq = 2 * n * C, (2 * n + 1) * C
        inv = 1.0 / (s[:, cp:cp + C] + s[:, cq:cq + C])
        hr, hi = _filter_spectrum(hs[:, cp:cp + C], hs[:, cq:cq + C], inv, K)
        x = _mm1(fd, bf(y))
        xr, xi = x[:K], x[K:]
        conv = _mm1(ft, bf(jnp.concatenate([xr * hr - xi * hi, xr * hi + xi * hr], axis=0)))
        y = u_ref[:, (n + 1) * C:(n + 2) * C] * (conv + y * bias_ref[n:n + 1, :])
    o_ref[...] = y


def hyena_ctx(u, filt, colsum, biases, tabs):
    args = (u, filt, colsum, biases, tabs[0], tabs[1])
    return pl.pallas_call(
        _hyena_ctx_kernel,
        in_specs=[_full(a) for a in args],
        out_specs=pl.BlockSpec((u.shape[0], GROUP_W), lambda: (0, 0)),
        out_shape=jax.ShapeDtypeStruct((u.shape[0], GROUP_W), jnp.float32),
        compiler_params=pltpu.CompilerParams(vmem_limit_bytes=VMEM_LIMIT),
    )(*args)


SCAN_T = 64
GROUP_LANES = 256
RW_SUB = 16
HG_SUB = 16
LOG2_E = 1.4426950408889634


def _chunk_index(d, i, n_ctx, n_all):
    fwd = jnp.where(i < n_ctx, n_all - n_ctx + i, i - n_ctx)
    return jnp.where(d == 0, fwd, n_all - 1 - i)


def _stacking(T, S, nh, head, sign):
    G = nh * head
    nb = T // S
    n = nh * T
    rr = lax.broadcasted_iota(jnp.int32, (n, G), 0)
    same = ((rr // S) % nh) == (lax.broadcasted_iota(jnp.int32, (n, G), 1) // head)

    def bd(x):
        pieces = []
        for i in range(nb):
            pieces += [x[i * S:(i + 1) * S]] * nh
        return jnp.where(same, jnp.concatenate(pieces, axis=0), 0.0)

    def collapse(o):
        outs = []
        for i in range(nb):
            acc = o[i * nh * S:i * nh * S + S]
            for h in range(1, nh):
                acc = acc + o[i * nh * S + h * S:i * nh * S + (h + 1) * S]
            outs.append(acc)
        return jnp.concatenate(outs, axis=0)

    rt = lax.broadcasted_iota(jnp.int32, (n, n), 0)
    cs = lax.broadcasted_iota(jnp.int32, (n, n), 1)
    t_r = (rt // (nh * S)) * S + rt % S
    t_c = (cs // (nh * S)) * S + cs % S
    same_h = ((rt // S) % nh) == ((cs // S) % nh)
    before = same_h & ((t_r - t_c) * sign > 0)
    return bd, collapse, before, rt, cs


def _softplus(x):
    return jnp.maximum(x, 0.0) + jnp.log(1.0 + jnp.exp(-jnp.abs(x)))


def _rwkv_prep_kernel(*refs, n_lat, n_rows):
    zrefs, rest = refs[:15], refs[15:]
    (mu_ref, kk_w_ref, ka_ref, w0_ref, a0_ref, w2h_ref, w2l_ref, a2h_ref, a2l_ref, ones_ref,
     r_ref, k_ref, v_ref, g_ref, kk_ref, lw_ref, kd_ref, a_ref) = rest
    slabs = []
    for c in range(5):
        x = zrefs[3 * c][...]
        xp, xn = _neighbours(x, zrefs[3 * c + 1][...], zrefs[3 * c + 2][...], n_lat, n_rows)
        slabs.append(x + (0.5 * (xp + xn) - x) * mu_ref[:, c * CB:(c + 1) * CB])
    r, k, v, g, lora = slabs
    r_ref[...], k_ref[...], v_ref[...], g_ref[...] = r, k, v, g
    kk = k * kk_w_ref[...]
    ss = _mm2(kk * kk, ones_ref[...])
    kk_ref[...] = kk * lax.rsqrt(jnp.maximum(ss, 1e-24))
    lora_t = _split(jnp.tanh(lora))
    lora_s = _split(lora)
    for d in range(2):
        w_log = -_softplus(-(w0_ref[d:d + 1, :] + _mm3(lora_t, (w2h_ref[d], w2l_ref[d])))) - 0.5
        lw_ref[d] = -jnp.exp(w_log)
        a = _sigmoid(a0_ref[d:d + 1, :] + _mm3(lora_s, (a2h_ref[d], a2l_ref[d])))
        a_ref[d] = a
        kd_ref[d] = k * (1.0 + (a - 1.0) * ka_ref[...])


def rwkv_prep(z, mu, k_k, k_a, w0, a0, w2, a2, n_lat):
    n_rows = z.shape[0]
    tp = PREP_ROWS
    mu_p = jnp.pad(mu, (0, 5 * CB - RW_REAL))[None, :]
    w2p = jnp.zeros((2, CB, GROUP_W), jnp.float32)
    a2p = jnp.zeros((2, CB, GROUP_W), jnp.float32)
    for d in range(2):
        w2p = w2p.at[d, d * RW_LORA:(d + 1) * RW_LORA].set(w2[d])
        a2p = a2p.at[d, (2 + d) * RW_LORA:(3 + d) * RW_LORA].set(a2[d])
    w2s, a2s = _split(w2p), _split(a2p)
    small = (mu_p, k_k[None, :], k_a[None, :], w0, a0, w2s[0], w2s[1], a2s[0], a2s[1],
             _block_indicator(GROUP_W, RW_HEAD, 1.0))
    zspecs = []
    for c in range(5):
        zspecs += _halo_specs(tp, n_rows, COL_RW + c)
    one = pl.BlockSpec((tp, CB), lambda i: (i, 0))
    two = pl.BlockSpec((2, tp, CB), lambda i: (0, i, 0))
    s1 = jax.ShapeDtypeStruct((n_rows, GROUP_W), jnp.float32)
    s2 = jax.ShapeDtypeStruct((2, n_rows, GROUP_W), jnp.float32)
    return pl.pallas_call(
        functools.partial(_rwkv_prep_kernel, n_lat=n_lat, n_rows=n_rows),
        grid=(n_rows // tp,),
        in_specs=zspecs + [_full(a) for a in small],
        out_specs=[one] * 5 + [two] * 3,
        out_shape=[s1] * 5 + [s2] * 3,
        compiler_params=_params(1),
    )(*([z] * 15), *small)


def _rwkv_scan_body(refs, head, tick):
    in_f, in_b, (of_ref, ob_ref, ht_ref) = refs[:6], refs[6:12], refs[12:]
    T, G, S = SCAN_T, GROUP_LANES, RW_SUB
    nh = G // head
    nb = T // S
    n = nh * T

    @pl.when(pl.program_id(0) == 0)
    def _():
        ht_ref[...] = jnp.zeros_like(ht_ref)

    ti = lax.broadcasted_iota(jnp.int32, (T, T), 0)
    si = lax.broadcasted_iota(jnp.int32, (T, T), 1)
    bf = lambda t: t.astype(jnp.bfloat16)

    def each(f, *seqs):
        tick()
        return [f(*xs) for xs in zip(*seqs)]

    n_grp = in_f[0].shape[1] // G
    lanes = [slice(grp * G, (grp + 1) * G) for grp in range(n_grp)]

    r, v, kk, lw, k, a, tri, before, incl, dest = ([] for _ in range(10))
    for (r_ref, v_ref, kk_ref, lw_ref, k_ref, a_ref), o_ref, sign in ((in_f, of_ref, 1), (in_b, ob_ref, -1)):
        bd, collapse, before_d, rt, cs = _stacking(T, S, nh, head, sign)
        for ls in lanes:
            r.append(r_ref[:, ls]), v.append(v_ref[:, ls]), kk.append(kk_ref[:, ls])
            lw.append(lw_ref[0, :, ls]), k.append(k_ref[0, :, ls]), a.append(a_ref[0, :, ls])
            tri.append(jnp.where((ti - si) * sign >= 0, 1.0, 0.0))
            before.append(before_d), incl.append(before_d | (rt == cs)), dest.append((o_ref, ls))
    diag_blk = (rt // (nh * S)) == (cs // (nh * S))
    eye = jnp.where(rt == cs, 1.0, 0.0)
    ht = [ht_ref[ch] for ch in range(len(r))]

    c = each(lambda tr, t: _dot(tr, t), tri, lw)
    ctot = each(lambda t: jnp.sum(t, axis=0, keepdims=True), lw)
    beta = each(lambda p, q: p * q, kk, a)
    einv = each(lambda t: jnp.exp(-t), c)
    lhs = each(lambda kk_, r_, c_, lw_: bf(jnp.concatenate([bd(-kk_ * jnp.exp(c_ - lw_)), bd(r_ * jnp.exp(c_))], axis=0)),
               kk, r, c, lw)
    rhs = each(lambda k_, b_, e_: bf(jnp.concatenate([bd(k_ * e_), bd(b_ * e_)], axis=0)), k, beta, einv)
    m = each(lambda p, q: _mm1(p, q, NT), lhs, rhs)
    g = each(lambda p, h: _mm1(p, bf(h), NT), lhs, ht)
    v_b = each(lambda t: bf(bd(t)), v)
    x = each(lambda g_, m_, v_, bm: g_[:n] + _mm1(bf(jnp.where(bm, m_[:n, :n], 0.0)), v_), g, m, v_b, before)
    lb = each(lambda m_, bm: jnp.where(bm, m_[:n, n:], 0.0), m, before)
    ld = each(lambda t: jnp.where(diag_blk, t, 0.0), lb)
    lo = each(lambda p, q: p - q, lb, ld)
    xd = each(lambda t: eye + t, ld)
    lp = each(bf, ld)
    p = 2
    while p < S:
        lp = each(lambda t: bf(_mm1(t, t)), lp)
        xd = each(lambda x_, l_: x_ + _mm1(l_, bf(x_)), xd, lp)
        p *= 2
    xd_b = each(bf, xd)
    n_b = each(lambda x_, l_: bf(_mm1(x_, bf(l_))), xd_b, lo)
    y = each(lambda x_, t: _mm1(x_, bf(t)), xd_b, x)
    y = each(lambda y_, n_: y_ + _mm1(n_, bf(y_)), y, n_b)
    p = 2
    while p < nb:
        n_b = each(lambda t: bf(_mm1(t, t)), n_b)
        y = each(lambda y_, n_: y_ + _mm1(n_, bf(y_)), y, n_b)
        p *= 2
    u_b = each(bf, y)
    o = each(lambda g_, m_, v_, u_, im: (g_[n:] + _mm1(bf(jnp.where(im, m_[n:, :n], 0.0)), v_))
             + _mm1(bf(jnp.where(im, m_[n:, n:], 0.0)), u_), g, m, v_b, u_b, incl)
    efin = each(lambda ct, c_: jnp.exp(ct - c_), ctot, c)
    ht_new = each(lambda h, ct, v_, u_, k_, b_, e_: (h * jnp.exp(ct) + _mm1(v_, bf(bd(k_ * e_)), TN))
                  + _mm1(u_, bf(bd(b_ * e_)), TN), ht, ctot, v_b, u_b, k, beta, efin)
    for ch, (o_ref, ls) in enumerate(dest):
        o_ref[:, ls] = collapse(o[ch])
        ht_ref[ch] = ht_new[ch]


def rope_tables(n_ctx, L):
    d_axis = RT_HEAD // 2
    half = d_axis // 2
    inv = ROPE_BASE ** (-jnp.arange(0, d_axis, 2, dtype=jnp.float32) / d_axis)
    t = jnp.arange(L)
    pos = jnp.stack([(t // GRID_W).astype(jnp.float32), (t % GRID_W).astype(jnp.float32)], axis=1)
    j = jnp.arange(RT_HEAD)
    ang = pos[:, j // d_axis] * inv[j % half][None, :]
    sgn = jnp.where((j % d_axis) < half, -1.0, 1.0)[None, :]
    cos = jnp.concatenate([jnp.cos(ang), jnp.ones((n_ctx, RT_HEAD), jnp.float32)], axis=0)
    sin = jnp.concatenate([jnp.sin(ang) * sgn, jnp.zeros((n_ctx, RT_HEAD), jnp.float32)], axis=0)
    return jnp.tile(cos, (1, RT_HEADS)), jnp.tile(sin, (1, RT_HEADS))


def _rotate(x, cos, sin):
    G = x.shape[1]
    half = RT_HEAD // 4
    lane = lax.broadcasted_iota(jnp.int32, x.shape, 1)
    partner = jnp.where((lane % (2 * half)) < half, pltpu.roll(x, G - half, axis=1), pltpu.roll(x, half, axis=1))
    return x * cos + partner * sin


def _retention_stages(q_ref, k_ref, v_ref, cos_ref, sin_ref, lg_ref, o_ref, st_ref, *, head, d):
    T, G = SCAN_T, GROUP_LANES
    nh = G // head
    sign = 1 - 2 * d

    t = lax.broadcasted_iota(jnp.int32, (T, 1), 0)
    pos = (t + d * (T - 1 - 2 * t) + 1).astype(jnp.float32)
    bd, collapse, before, rt, cs = _stacking(T, T, nh, head, sign)
    incl = before | (rt == cs)
    bf = lambda x: x.astype(jnp.bfloat16)

    groups = range(q_ref.shape[1] // G)
    lanes = [slice(grp * G, (grp + 1) * G) for grp in groups]
    each = lambda f, *seqs: [f(*xs) for xs in zip(*seqs)]
    st = [st_ref[grp] for grp in groups]
    cos, sin, lg = ([ref[:, ls] for ls in lanes] for ref in (cos_ref, sin_ref, lg_ref))
    q = each(_rotate, [q_ref[:, ls] for ls in lanes], cos, sin)
    yield
    k = each(lambda x, c_, s_: _rotate(x, c_, s_) * (head ** -0.5), [k_ref[:, ls] for ls in lanes], cos, sin)
    c = each(lambda lg_: pos * lg_, lg)
    yield
    q_b = each(lambda q_, c_: bf(bd(q_ * jnp.exp(c_))), q, c)
    yield
    kt_b = each(lambda k_, c_: bf(bd(k_ * jnp.exp(-c_))), k, c)
    yield
    v_b = [bf(bd(v_ref[:, ls])) for ls in lanes]
    yield
    scores = each(lambda q_, k_: bf(jnp.where(incl, _mm1(q_, k_, NT), 0.0)), q_b, kt_b)
    yield
    inter = each(lambda q_, s_: _mm1(q_, bf(s_), NT), q_b, st)
    yield
    o = each(lambda g_, s_, v_: g_ + _mm1(s_, v_), inter, scores, v_b)
    yield
    st_new = each(lambda s_, lg_, v_, k_, c_: s_ * jnp.exp(float(T) * lg_)
                  + _mm1(v_, bf(bd(k_ * jnp.exp(float(T) * lg_ - c_))), TN), st, lg, v_b, k, c)
    yield
    for grp in groups:
        o_ref[:, lanes[grp]] = collapse(o[grp])
        st_ref[grp] = st_new[grp]


def _gla_stages(q_ref, f_ref, i_ref, lb_ref, o_ref, st_ref, *, head, d):
    T, S = SCAN_T, HG_SUB
    W = q_ref.shape[-1]
    nh = W // head
    sign = 1 - 2 * d

    ti = lax.broadcasted_iota(jnp.int32, (S, S), 0)
    si = lax.broadcasted_iota(jnp.int32, (S, S), 1)
    tri_incl = jnp.where((ti - si) * sign >= 0, 1.0, 0.0)
    row = lax.broadcasted_iota(jnp.int32, (S, 1), 0)
    lb = lb_ref[0]

    states = [st_ref[h] for h in range(nh)]
    for j in range(T // S):
        jb = j + d * (T // S - 1 - 2 * j)
        rows = pl.ds(jb * S, S)
        q = q_ref[rows, :]
        q = q * _sigmoid(q)
        v = i_ref[rows, :]
        gate = lb + (1.0 - lb) * _sigmoid(f_ref[rows, :])
        lf = jnp.log(jnp.maximum(gate, HG_MIN_GATE))
        k = 1.0 - gate
        b = _dot(tri_incl, lf)
        btot = jnp.sum(lf, axis=0, keepdims=True)
        qe = q * jnp.exp(b)
        ke = k * jnp.exp(btot - b)
        outs = []
        for h in range(nh):
            ls = slice(h * head, (h + 1) * head)
            qh, kh, vh = q[:, ls], k[:, ls], v[:, ls]
            bh = b[:, ls] * LOG2_E
            st = states[h]
            o = _mm1(qe[:, ls].astype(jnp.bfloat16), st.astype(jnp.bfloat16), NT)
            for s in range(S):
                e = jnp.exp2(bh - bh[s:s + 1, :])
                a_s = jnp.sum(qh * kh[s:s + 1, :] * e, axis=-1, keepdims=True)
                a_s = jnp.where((row - s) * sign >= 0, a_s, 0.0)
                o = o + a_s * vh[s:s + 1, :]
            outs.append(o)
            states[h] = st * jnp.exp(btot[:, ls]) + _mm1(vh.astype(jnp.bfloat16), ke[:, ls].astype(jnp.bfloat16), TN)
            yield
        o_ref[rows, :] = jnp.concatenate(outs, axis=1)
    for h in range(nh):
        st_ref[h] = states[h]


N_RW, N_RT, N_HG = 6, 6, 4


def _scans_kernel(*refs):
    n_in = 2 * (N_RW + N_RT + N_HG)
    ins, (rw_of, rw_ob, rt_of, rt_ob, hg_of, hg_ob, rw_st, rt_st, hg_st) = refs[:n_in], refs[n_in:]
    rw_in, rt_in, hg_in = ins[:2 * N_RW], ins[2 * N_RW:2 * (N_RW + N_RT)], ins[2 * (N_RW + N_RT):]

    @pl.when(pl.program_id(0) == 0)
    def _():
        rt_st[...] = jnp.zeros_like(rt_st)
        hg_st[...] = jnp.zeros_like(hg_st)

    n_rt, n_hg = rt_st.shape[0] // 2, hg_st.shape[0] // 2
    others = []
    for d, (rt_o, hg_o) in enumerate(((rt_of, hg_of), (rt_ob, hg_ob))):
        others.append(_gla_stages(*hg_in[d * N_HG:(d + 1) * N_HG], hg_o, hg_st.at[d * n_hg:(d + 1) * n_hg],
                                  head=HG_HEAD, d=d))
        others.append(_retention_stages(*rt_in[d * N_RT:(d + 1) * N_RT], rt_o, rt_st.at[d * n_rt:(d + 1) * n_rt],
                                        head=RT_HEAD, d=d))
    turn = [0]

    def tick():
        for _ in range(2):
            for _ in range(len(others)):
                turn[0] = (turn[0] + 1) % len(others)
                if next(others[turn[0]], "done") != "done":
                    break

    _rwkv_scan_body(list(rw_in) + [rw_of, rw_ob, rw_st], RW_HEAD, tick)
    for gen in others:
        for _ in gen:
            pass


def recurrent_scans(z, rw, cos, sin, lg, lb2, n_ctx_rows):
    N = z.shape[0]
    W = GROUP_W
    T = SCAN_T
    n_all, n_ctx = N // T, n_ctx_rows // T
    chunk = lambda d: (lambda i: (_chunk_index(d, i, n_ctx, n_all), 0))
    row = lambda d: pl.BlockSpec((T, W), chunk(d))
    per_dir = lambda d: pl.BlockSpec((1, T, W), lambda i: (d, _chunk_index(d, i, n_ctx, n_all), 0))
    zc = lambda d, col: pl.BlockSpec((T, CB), lambda i: (_chunk_index(d, i, n_ctx, n_all), col))
    rw_specs = lambda d: [row(d)] * 3 + [per_dir(d)] * 3
    rt_specs = lambda d: [zc(d, COL_RT), zc(d, COL_RT + 1), zc(d, COL_RT + 2), row(d), row(d),
                          pl.BlockSpec((1, W), lambda i: (0, 0))]
    hg_specs = lambda d: [zc(d, COL_HG), zc(d, COL_HG + 1 + d), zc(d, COL_HG + 3),
                          pl.BlockSpec((1, 1, W), lambda i: (d, 0, 0))]
    shape = jax.ShapeDtypeStruct((N, W), jnp.float32)
    return pl.pallas_call(
        _scans_kernel,
        grid=(n_all,),
        in_specs=rw_specs(0) + rw_specs(1) + rt_specs(0) + rt_specs(1) + hg_specs(0) + hg_specs(1),
        out_specs=[row(0), row(1)] * 3,
        out_shape=[shape] * 6,
        scratch_shapes=[pltpu.VMEM((2 * (W // GROUP_LANES), GROUP_LANES, GROUP_LANES), jnp.float32),
                        pltpu.VMEM((2 * (W // GROUP_LANES), GROUP_LANES, GROUP_LANES), jnp.float32),
                        pltpu.VMEM((2 * (W // HG_HEAD), HG_HEAD, HG_HEAD), jnp.float32)],
        compiler_params=_params(1),
    )(*(list(rw) * 2), *([z, z, z, cos, sin, lg] * 2), *([z, z, z, lb2] * 2))


def _outproj_kernel(hy_ref, rwof_ref, rwob_ref, r_ref, k_ref, v_ref, rwg_ref, rtof_ref, rtob_ref, rtg_ref,
                    hgof_ref, hgob_ref, hgg_ref,
                    rk_ref, gng_ref, gnb_ref, hgn_ref, avg64_ref, avg128_ref,
                    w_ref, x_ref, gate_ref, g_ref, b_ref, o_ref, *, n_lat):
    avg64, avg128 = avg64_ref[...], avg128_ref[...]

    def head_norm(o, avg, eps, centre):
        if centre:
            o = o - _mm2(o, avg)
        return o * lax.rsqrt(_mm2(o * o, avg) + eps)

    silu = lambda t: t * _sigmoid(t)
    y_rw = head_norm(rwof_ref[...] + rwob_ref[...], avg64, RW_GN_EPS, True) * gng_ref[...] + gnb_ref[...]
    bonus = (float(RW_HEAD) * _mm2(r_ref[...] * k_ref[...] * rk_ref[...], avg64)) * v_ref[...]
    y_rw = (y_rw + bonus) * _sigmoid(rwg_ref[...])
    y_rt = head_norm(rtof_ref[...] + rtob_ref[...], avg64, HEAD_NORM_EPS, True) * silu(rtg_ref[...])
    y_hg = head_norm(hgof_ref[...] + hgob_ref[...], avg128, HEAD_NORM_EPS, False) * hgn_ref[...] * silu(hgg_ref[...])
    y = None
    for m, ym in enumerate((hy_ref[...], y_rw, y_rt, y_hg)):
        part = jnp.dot(ym.astype(jnp.bfloat16), w_ref[m * GROUP_W:(m + 1) * GROUP_W, :],
                       preferred_element_type=jnp.float32)
        y = part if y is None else y + part
    is_ctx = _ctx_rows(x_ref.shape[0], n_lat)
    r = ALPHA * x_ref[...] + _mod_row(gate_ref, is_ctx) * y
    o_ref[...] = _ln_rows(r) * g_ref[...] + b_ref[...]


def outproj_deepnorm(y_hy, scans, rw_r, rw_k, rw_v, rw_g, z, r_k, gn_g, gn_b, hg_norm_g,
                     w_bf16, x, mod, g, b, n_lat, m):
    tm = OUT_TM
    one = pl.BlockSpec((tm, GROUP_W), lambda i: (i, 0))
    zc = lambda col: pl.BlockSpec((tm, CB), lambda i: (i, col))
    row = pl.BlockSpec((tm, D_MODEL), lambda i: (i, 0))
    small = (r_k, gn_g, gn_b, hg_norm_g, _block_indicator(GROUP_W, RW_HEAD, 1.0 / RW_HEAD),
             _block_indicator(GROUP_W, HG_HEAD, 1.0 / HG_HEAD))
    return pl.pallas_call(
        functools.partial(_outproj_kernel, n_lat=n_lat),
        grid=(m // tm,),
        in_specs=[one] * 9 + [zc(COL_RT + 3), one, one, zc(COL_HG + 4)]
        + [_full(a) for a in small]
        + [_full(w_bf16), row, pl.BlockSpec((8, D_MODEL), lambda i: (0, 2)), _full(g), _full(b)],
        out_specs=row,
        out_shape=jax.ShapeDtypeStruct((m, D_MODEL), jnp.float32),
        compiler_params=_params(1),
    )(y_hy, scans[0], scans[1], rw_r, rw_k, rw_v, rw_g, scans[2], scans[3], z, scans[4], scans[5], z,
      *small, w_bf16, x, mod, g, b)


def _ffn_kernel(x_ref, sh_ref, sc_ref, w1_ref, w3_ref, w2_ref, gate_ref, g_ref, b_ref, o_ref, h_ref, acc_ref, *, n_lat):
    j = pl.program_id(1)

    @pl.when(j == 0)
    def _():
        is_ctx = _ctx_rows(x_ref.shape[0], n_lat)
        h = _ln_rows(x_ref[...]) * (1.0 + _mod_row(sc_ref, is_ctx)) + _mod_row(sh_ref, is_ctx)
        h_ref[...] = h.astype(jnp.bfloat16)
        acc_ref[...] = jnp.zeros_like(acc_ref)

    h = h_ref[...]
    a = jnp.dot(h, w1_ref[...], preferred_element_type=jnp.float32)
    u = jnp.dot(h, w3_ref[...], preferred_element_type=jnp.float32)
    s = (a * _sigmoid(a) * u).astype(jnp.bfloat16)
    acc_ref[...] += jnp.dot(s, w2_ref[...], preferred_element_type=jnp.float32)

    @pl.when(j == pl.num_programs(1) - 1)
    def _():
        is_ctx = _ctx_rows(x_ref.shape[0], n_lat)
        r = ALPHA * x_ref[...] + _mod_row(gate_ref, is_ctx) * acc_ref[...]
        o_ref[...] = _ln_rows(r) * g_ref[...] + b_ref[...]


def ffn_deepnorm(x, mod, w1, w3, w2, g, b, n_lat, tm):
    m = x.shape[0]
    row = pl.BlockSpec((tm, D_MODEL), lambda i, j: (i, 0))
    modc = lambda c: pl.BlockSpec((8, D_MODEL), lambda i, j: (0, c))
    vec = pl.BlockSpec((1, D_MODEL), lambda i, j: (0, 0))
    return pl.pallas_call(
        functools.partial(_ffn_kernel, n_lat=n_lat),
        grid=(m // tm, FFN_HIDDEN // FFN_TF),
        in_specs=[row, modc(3), modc(4),
                  pl.BlockSpec((D_MODEL, FFN_TF), lambda i, j: (0, j)),
                  pl.BlockSpec((D_MODEL, FFN_TF), lambda i, j: (0, j)),
                  pl.BlockSpec((FFN_TF, D_MODEL), lambda i, j: (j, 0)),
                  modc(5), vec, vec],
        out_specs=row,
        out_shape=jax.ShapeDtypeStruct((m, D_MODEL), jnp.float32),
        scratch_shapes=[pltpu.VMEM((tm, D_MODEL), jnp.bfloat16),
                        pltpu.VMEM((tm, D_MODEL), jnp.float32)],
        compiler_params=_params(2),
    )(x, mod, mod, w1, w3, w2, mod, g, b)


def kernel(x, c, ctx, c_ctx, ada_w, ada_b, w_in, w_out, ln_g, ln_b, hy_conv_w, hy_conv_b, hy_w1, hy_b1, hy_w2, hy_b2, hy_w3, hy_b3, hy_freq, hy_bias, rw_mu, rw_w0, rw_w2, rw_a0, rw_a2, rw_k_k, rw_k_a, rw_r_k, rw_gn_g, rw_gn_b, hg_lb_raw, hg_norm_g, ffn_w1, ffn_w3, ffn_w2):
    L, n_ctx = x.shape[1], ctx.shape[1]
    sm = jax.nn.softmax(hg_lb_raw.astype(jnp.float32), axis=1)
    lower_bounds = jnp.cumsum(sm, axis=1) - sm[:, :1]
    log_gamma = jnp.log1p(-jnp.exp2(-5.0 - jnp.arange(RT_HEADS, dtype=jnp.float32)))
    lg = jnp.repeat(log_gamma, RT_HEAD)[None, :]
    cos, sin = rope_tables(n_ctx, L)
    fft_tabs = fft_tables(FFT_N1, 2 * L // FFT_N1)
    ctx_tabs = dense_dft_tables(n_ctx)

    c8 = jnp.zeros((8, D_MODEL), jnp.float32).at[0].set(c[0]).at[1].set(c_ctx)
    xs = jnp.concatenate([x[0], ctx[0]], axis=0)
    n_rows = L + n_ctx
    for l in range(DEPTH):
        with_ctx = l < DEPTH - 1
        mod = ada_modulation(c8, ada_w, l, ada_b[l][None, :])
        w_out_b = w_out[l].astype(jnp.bfloat16)
        w1_b, w3_b, w2_b = (w[l].astype(jnp.bfloat16) for w in (ffn_w1, ffn_w3, ffn_w2))

        z = modulated_projection(xs, mod, w_in, l, L)

        u = hyena_conv3(z, hy_conv_w[l], hy_conv_b[l][None, :], L)
        hy_w = (hy_w1[l], hy_b1[l], hy_w2[l], hy_b2[l], hy_w3[l], hy_b3[l], hy_freq[l])
        filt, colsum = hyena_filter_bank(L, *hy_w)
        y_lat = hyena_long_conv_chain(u, L, hy_bias[l], filt, colsum, fft_tabs)
        if with_ctx:
            filt_c, colsum_c = hyena_filter_bank(n_ctx, *hy_w)
            y_ctx = hyena_ctx(u[L:], filt_c, colsum_c, hy_bias[l], ctx_tabs)
        else:
            y_ctx = jnp.zeros((n_ctx, GROUP_W), jnp.float32)
        y_hy = jnp.concatenate([y_lat, y_ctx], axis=0)

        rw_r, rw_k, rw_v, rw_g, rw_kk, rw_lw, rw_kd, rw_a = rwkv_prep(
            z, rw_mu[l], rw_k_k[l], rw_k_a[l], rw_w0[l], rw_a0[l], rw_w2[l], rw_a2[l], L)
        scans = recurrent_scans(z, (rw_r, rw_v, rw_kk, rw_lw, rw_kd, rw_a), cos, sin, lg,
                                lower_bounds[:, l][:, None, :], n_ctx)

        xs = outproj_deepnorm(y_hy, scans, rw_r, rw_k, rw_v, rw_g, z,
                              rw_r_k[l].reshape(1, GROUP_W), rw_gn_g[l][None, :], rw_gn_b[l][None, :],
                              hg_norm_g[l][None, :], w_out_b, xs, mod, ln_g[l, 0][None, :], ln_b[l, 0][None, :],
                              L, n_rows if with_ctx else L)
        xs = ffn_deepnorm(xs, mod, w1_b, w3_b, w2_b, ln_g[l, 1][None, :], ln_b[l, 1][None, :], L,
                          FFN_TM if with_ctx else FFN_LAST_TM)
    return xs[None]
```

```python
import functools
import math

import jax
import jax.numpy as jnp
from jax import lax
from jax.experimental import pallas as pl
from jax.experimental.pallas import tpu as pltpu

D_MODEL = 2048
DEPTH = 2
GRID_W = 64
N_MIXERS = 4
GROUP_W = D_MODEL // N_MIXERS
HY_ORDER = 2
HY_EMB = 33
HY_FAST_DECAY = 0.3
HY_SLOW_DECAY = 1.5
HY_TARGET = 1e-2
RW_HEAD = 64
RW_LORA = 96
RW_GN_EPS = 64e-5
RT_HEAD = 64
RT_HEADS = GROUP_W // RT_HEAD
ROPE_BASE = 10000.0
HG_HEAD = 128
HG_MIN_GATE = 1e-30
FFN_HIDDEN = 5632
ALPHA = (2 * DEPTH) ** 0.25
LN_EPS = 1e-6
HEAD_NORM_EPS = 1e-6

LANES = 128
CB = 512
COL_HY = 0
COL_RW = 3
COL_RT = 8
COL_HG = 12
P_IN_PAD = 17 * CB
RW_REAL = 4 * GROUP_W + 4 * RW_LORA
PROJ_TM = 1408
FFN_TM = 768
FFN_LAST_TM = 512
OUT_TM = 256
FFN_TF = 512
PREP_ROWS = 256
HALO = 8
VMEM_LIMIT = 56 * 1024 * 1024

HI = lax.Precision.HIGHEST
NN = ((1,), (0,))
NT = ((1,), (1,))
TN = ((0,), (0,))


def _params(n_axes):
    return pltpu.CompilerParams(dimension_semantics=("arbitrary",) * n_axes, vmem_limit_bytes=VMEM_LIMIT)


def _full(a):
    return pl.BlockSpec(a.shape, lambda *_: (0,) * a.ndim)


def _dot(a, b):
    return jnp.dot(a, b, precision=HI, preferred_element_type=jnp.float32)


def _split(x):
    hi = x.astype(jnp.bfloat16)
    lo = (x - hi.astype(jnp.float32)).astype(jnp.bfloat16)
    return hi, lo


def _mm3(a, b, dims=NN):
    d = lambda p, q: lax.dot_general(p, q, (dims, ((), ())), preferred_element_type=jnp.float32)
    return d(a[0], b[0]) + (d(a[0], b[1]) + d(a[1], b[0]))


def _mm1(a, b, dims=NN):
    return lax.dot_general(a, b, (dims, ((), ())), preferred_element_type=jnp.float32)


def _mm2(x, b):
    hi, lo = _split(x)
    d = functools.partial(jnp.dot, preferred_element_type=jnp.float32)
    return d(hi, b) + d(lo, b)


def _ln_rows(x):
    mu = jnp.mean(x, axis=-1, keepdims=True)
    xc = x - mu
    var = jnp.mean(xc * xc, axis=-1, keepdims=True)
    return xc * lax.rsqrt(var + LN_EPS)


def _sigmoid(x):
    return 1.0 / (1.0 + jnp.exp(-x))


def _ctx_rows(tm, n_lat):
    return (pl.program_id(0) * tm + lax.broadcasted_iota(jnp.int32, (tm, 1), 0)) >= n_lat


def _mod_row(ref, is_ctx):
    return jnp.where(is_ctx, ref[1:2, :], ref[0:1, :])


def _block_indicator(width, head, value):
    i = jnp.arange(width) // head
    return jnp.where(i[:, None] == i[None, :], value, 0.0).astype(jnp.bfloat16)


def _ada_kernel(c_ref, w_ref, b_ref, o_ref):
    c = c_ref[...]
    h = c * _sigmoid(c)
    o_ref[...] = jnp.dot(h.astype(jnp.bfloat16), w_ref[...].astype(jnp.bfloat16),
                         preferred_element_type=jnp.float32) + b_ref[...]


def ada_modulation(c8, w, layer, b):
    n = w.shape[2]
    tn = 1024
    return pl.pallas_call(
        _ada_kernel,
        grid=(n // tn,),
        in_specs=[pl.BlockSpec((8, D_MODEL), lambda j: (0, 0)),
                  pl.BlockSpec((None, D_MODEL, tn), lambda j: (layer, 0, j)),
                  pl.BlockSpec((1, tn), lambda j: (0, j))],
        out_specs=pl.BlockSpec((8, tn), lambda j: (0, j)),
        out_shape=jax.ShapeDtypeStruct((8, n), jnp.float32),
        compiler_params=_params(1),
    )(c8, w, b)


W_SUB = CB // LANES
W_PAD_PIECE = (COL_RW * CB + RW_REAL) // LANES


def _proj_kernel(x_ref, sh_ref, sc_ref, *refs, n_lat):
    w_refs, (o_ref, h_ref) = refs[:W_SUB], refs[W_SUB:]
    j = pl.program_id(1)

    @pl.when(j == 0)
    def _():
        is_ctx = _ctx_rows(x_ref.shape[0], n_lat)
        h = _ln_rows(x_ref[...]) * (1.0 + _mod_row(sc_ref, is_ctx)) + _mod_row(sh_ref, is_ctx)
        h_ref[...] = h.astype(jnp.bfloat16)

    pieces = [w_refs[k][...] for k in range(W_SUB)]
    for k in range(W_SUB):
        if (W_PAD_PIECE - k) % W_SUB == 0:
            pieces[k] = jnp.where(j * W_SUB + k == W_PAD_PIECE, 0.0, pieces[k])
    w = jnp.concatenate(pieces, axis=1).astype(jnp.bfloat16)
    o_ref[...] = jnp.dot(h_ref[...], w, preferred_element_type=jnp.float32)


def modulated_projection(x, mod, w, layer, n_lat):
    m = x.shape[0]
    tm = PROJ_TM

    def piece(k):
        def index(i, j):
            p = j * W_SUB + k
            return layer, 0, jnp.where(p < W_PAD_PIECE, p, p - 1)
        return pl.BlockSpec((None, D_MODEL, LANES), index)

    return pl.pallas_call(
        functools.partial(_proj_kernel, n_lat=n_lat),
        grid=(m // tm, P_IN_PAD // CB),
        in_specs=[pl.BlockSpec((tm, D_MODEL), lambda i, j: (i, 0)),
                  pl.BlockSpec((8, D_MODEL), lambda i, j: (0, 0)),
                  pl.BlockSpec((8, D_MODEL), lambda i, j: (0, 1))] + [piece(k) for k in range(W_SUB)],
        out_specs=pl.BlockSpec((tm, CB), lambda i, j: (i, j)),
        out_shape=jax.ShapeDtypeStruct((m, P_IN_PAD), jnp.float32),
        scratch_shapes=[pltpu.VMEM((tm, D_MODEL), jnp.bfloat16)],
        compiler_params=_params(2),
    )(x, mod, mod, *([w] * W_SUB))


def _halo_specs(tp, n_rows, col):
    per = tp // HALO
    last = n_rows // HALO - 1
    main = pl.BlockSpec((tp, CB), lambda i, *_: (i, col(*_) if callable(col) else col))
    prev = pl.BlockSpec((HALO, CB), lambda i, *_: (jnp.maximum(i * per - 1, 0), col(*_) if callable(col) else col))
    nxt = pl.BlockSpec((HALO, CB), lambda i, *_: (jnp.minimum((i + 1) * per, last), col(*_) if callable(col) else col))
    return [main, prev, nxt]


def _neighbours(x, prev, nxt, n_lat, n_rows):
    tp = x.shape[0]
    loc = lax.broadcasted_iota(jnp.int32, (tp, 1), 0)
    row = pl.program_id(0) * tp + loc
    xp = jnp.where(loc == 0, prev[HALO - 1:HALO, :], pltpu.roll(x, 1, axis=0))
    xp = jnp.where((row == 0) | (row == n_lat), 0.0, xp)
    xn = jnp.where(loc == tp - 1, nxt[0:1, :], pltpu.roll(x, tp - 1, axis=0))
    xn = jnp.where((row == n_lat - 1) | (row == n_rows - 1), 0.0, xn)
    return xp, xn


def _conv3_kernel(z_ref, zp_ref, zn_ref, w_ref, b_ref, o_ref, *, n_lat, n_rows):
    x = z_ref[...]
    xp, xn = _neighbours(x, zp_ref[...], zn_ref[...], n_lat, n_rows)
    w = w_ref[...]
    o_ref[...] = xp * w[0:1, :] + x * w[1:2, :] + xn * w[2:3, :] + b_ref[...]


def hyena_conv3(z, w, b, n_lat):
    n_rows = z.shape[0]
    tp = PREP_ROWS
    ncol = w.shape[1] // CB
    return pl.pallas_call(
        functools.partial(_conv3_kernel, n_lat=n_lat, n_rows=n_rows),
        grid=(n_rows // tp, ncol),
        in_specs=_halo_specs(tp, n_rows, lambda c: COL_HY + c)
        + [pl.BlockSpec((3, CB), lambda i, c: (0, c)), pl.BlockSpec((1, CB), lambda i, c: (0, c))],
        out_specs=pl.BlockSpec((tp, CB), lambda i, c: (i, c)),
        out_shape=jax.ShapeDtypeStruct((n_rows, w.shape[1]), jnp.float32),
        compiler_params=_params(2),
    )(z, z, z, w, b)


HY_EMB_PAD = 40
HY_FILTER_ROWS = 512


def hyena_features(L):
    t = jnp.linspace(0.0, 1.0, L, dtype=jnp.float32)[:, None]
    n_bands = (HY_EMB - 1) // 2
    f = jnp.linspace(1e-4, n_bands - 1, n_bands, dtype=jnp.float32)[None, :]
    ang = (2.0 * math.pi / L) * jnp.arange(L, dtype=jnp.float32)[:, None] * f
    z = jnp.concatenate([t, jnp.cos(ang), -jnp.sin(ang)], -1)
    return jnp.pad(z, ((0, 0), (0, HY_EMB_PAD - HY_EMB)))


def _filter_kernel(z_ref, w1_ref, b1_ref, w2_ref, b2_ref, w3_ref, b3_ref, fr_ref, dl_ref, h_ref, s_ref):
    i = pl.program_id(0)
    z = z_ref[...]
    fr = fr_ref[...]
    h = jnp.sin(fr * (_dot(z, w1_ref[...]) + b1_ref[...]))
    h = jnp.sin(fr * (_dot(h, w2_ref[...]) + b2_ref[...]))
    h = _mm1(h.astype(jnp.bfloat16), w3_ref[...].astype(jnp.bfloat16)) + b3_ref[...]
    win = jnp.exp(-z[:, 0:1] * dl_ref[...])
    h = h * jnp.concatenate([win] * (h.shape[1] // win.shape[1]), axis=1)

    @pl.when(i == 0)
    def _():
        s_ref[...] = jnp.zeros_like(s_ref)

    s_ref[...] += jnp.sum(jnp.abs(h), axis=0, keepdims=True)
    row = lax.broadcasted_iota(jnp.int32, h.shape, 0) + i * h.shape[0]
    col = lax.broadcasted_iota(jnp.int32, h.shape, 1)
    neg = (col // GROUP_W) % 2 == 1
    h_ref[...] = jnp.where(neg & (row == 0), 0.0, h)


def hyena_filter_bank(L, w1, b1, w2, b2, w3, b3, freq):
    z = hyena_features(L)
    w1p = jnp.pad(w1, ((0, HY_EMB_PAD - HY_EMB), (0, 0)))
    max_decay = math.log(HY_TARGET) / HY_FAST_DECAY
    min_decay = math.log(HY_TARGET) / HY_SLOW_DECAY
    deltas = jnp.abs(jnp.linspace(min_decay, max_decay, GROUP_W, dtype=jnp.float32))[None, :]
    n = w3.shape[1]
    tr = min(L, HY_FILTER_ROWS)
    args = (z, w1p, b1[None, :], w2, b2[None, :], w3, b3[None, :], freq[None, :], deltas)
    return pl.pallas_call(
        _filter_kernel,
        grid=(L // tr,),
        in_specs=[pl.BlockSpec((tr, HY_EMB_PAD), lambda i: (i, 0))] + [_full(a) for a in args[1:]],
        out_specs=[pl.BlockSpec((tr, n), lambda i: (i, 0)), pl.BlockSpec((1, n), lambda i: (0, 0))],
        out_shape=[jax.ShapeDtypeStruct((L, n), jnp.float32), jax.ShapeDtypeStruct((1, n), jnp.float32)],
        compiler_params=_params(1),
    )(*args)


FFT_NB = 8
FFT_CB = 256
FFT_N1 = 128


def fft_tables(n1, n2):
    N = n1 * n2
    nk = -(-(n2 // 2 + 1) // FFT_NB) * FFT_NB
    a = jnp.arange(n1, dtype=jnp.float32)[:, None, None]
    k2 = jnp.arange(nk, dtype=jnp.float32)[None, :, None]
    b = jnp.arange(n2 // 2, dtype=jnp.float32)[None, None, :]
    ph = (jnp.mod(a * k2, float(N)) / N + jnp.mod(b * k2, float(n2)) / n2) * (-2.0 * math.pi)
    g = jnp.concatenate([jnp.cos(ph), jnp.sin(ph)], axis=1)
    kk = jnp.arange(nk)
    weight = jnp.where((kk == 0) | (kk == n2 // 2), 1.0, jnp.where(kk < n2 // 2, 2.0, 0.0))
    ginv = jnp.transpose(g, (0, 2, 1)) * (jnp.tile(weight, 2) / N)
    k1 = jnp.arange(n1, dtype=jnp.float32)[:, None]
    aa = jnp.arange(n1, dtype=jnp.float32)[None, :]
    f = jnp.mod(k1 * aa, float(n1)) * (-2.0 * math.pi / n1)
    fr, fi = jnp.cos(f), jnp.sin(f)
    ff = jnp.concatenate([jnp.concatenate([fr, -fi], axis=1), jnp.concatenate([fi, fr], axis=1)], axis=0)
    bf = lambda t: t.astype(jnp.bfloat16)
    return {"g": bf(g), "ginv": bf(ginv), "ff": bf(ff), "fft": bf(ff.T)}


def _stage_a_kernel(u_ref, g_ref, o_ref):
    for j in range(FFT_NB):
        o_ref[j] = _mm1(g_ref[j], u_ref[:, j, :].astype(jnp.bfloat16))


def fft_stage_a(u3, nb, col, ncol, g):
    n1 = u3.shape[1]
    rows = g.shape[1]
    gspec = pl.BlockSpec((FFT_NB, rows, nb), lambda c, i: (i, 0, 0))
    return pl.pallas_call(
        _stage_a_kernel,
        grid=(ncol, n1 // FFT_NB),
        in_specs=[pl.BlockSpec((nb, FFT_NB, CB), lambda c, i: (0, i, col + c)), gspec],
        out_specs=pl.BlockSpec((FFT_NB, rows, CB), lambda c, i: (i, 0, c)),
        out_shape=jax.ShapeDtypeStruct((n1, rows, ncol * CB), jnp.float32),
        compiler_params=_params(2),
    )(u3, g)


def _stage_b_filter_kernel(re_ref, im_ref, f_ref, o_ref):
    ff = f_ref[...]
    for j in range(FFT_NB):
        o_ref[j] = _mm1(ff, jnp.concatenate([re_ref[:, j, :], im_ref[:, j, :]], axis=0).astype(jnp.bfloat16))


def fft_stage_b_filter(t1, ff):
    n1, n2x2, C = t1.shape
    n2 = n2x2 // 2
    blk = lambda off: pl.BlockSpec((n1, FFT_NB, FFT_CB), lambda c, i: (0, i + off, c))
    mat = pl.BlockSpec((2 * n1, 2 * n1), lambda c, i: (0, 0))
    return pl.pallas_call(
        _stage_b_filter_kernel,
        grid=(C // FFT_CB, n2 // FFT_NB),
        in_specs=[blk(0), blk(n2 // FFT_NB), mat],
        out_specs=pl.BlockSpec((FFT_NB, 2 * n1, FFT_CB), lambda c, i: (i, 0, c)),
        out_shape=jax.ShapeDtypeStruct((n2, 2 * n1, C), jnp.float32),
        compiler_params=_params(2),
    )(t1, t1, ff)


def _filter_spectrum(p, q, s, n1):
    return (p[:n1] + q[:n1]) * s, (p[n1:] - q[n1:]) * s


def _stage_b_conv_kernel(re_ref, im_ref, p_ref, q_ref, s_ref, f_ref, ft_ref, ore_ref, oim_ref):
    n1 = re_ref.shape[0]
    s = s_ref[...]
    ff, fft_ = f_ref[...], ft_ref[...]
    bf = lambda t: t.astype(jnp.bfloat16)
    for j in range(FFT_NB):
        x = _mm1(ff, bf(jnp.concatenate([re_ref[:, j, :], im_ref[:, j, :]], axis=0)))
        hr, hi = _filter_spectrum(p_ref[j], q_ref[j], s, n1)
        xr, xi = x[:n1], x[n1:]
        z = _mm1(fft_, bf(jnp.concatenate([xr * hr - xi * hi, xr * hi + xi * hr], axis=0)))
        ore_ref[:, j, :] = z[:n1]
        oim_ref[:, j, :] = z[n1:]


def fft_stage_b_conv(t1, spec, col_p, col_q, inv_norm, ff, fft_):
    n1, n2x2, C = t1.shape
    n2 = n2x2 // 2
    ncb = C // FFT_CB
    blk = lambda off: pl.BlockSpec((n1, FFT_NB, FFT_CB), lambda c, i: (0, i + off, c))
    mat = pl.BlockSpec((2 * n1, 2 * n1), lambda c, i: (0, 0))
    sp = lambda col: pl.BlockSpec((FFT_NB, 2 * n1, FFT_CB), lambda c, i: (i, 0, col * ncb + c))
    return pl.pallas_call(
        _stage_b_conv_kernel,
        grid=(ncb, n2 // FFT_NB),
        in_specs=[blk(0), blk(n2 // FFT_NB), sp(col_p), sp(col_q), pl.BlockSpec((1, FFT_CB), lambda c, i: (0, c)),
                  mat, mat],
        out_specs=[blk(0), blk(0)],
        out_shape=[jax.ShapeDtypeStruct((n1, n2, C), jnp.float32)] * 2,
        compiler_params=_params(2),
    )(t1, t1, spec, spec, inv_norm, ff, fft_)


def _stage_a_inv_kernel(re_ref, im_ref, g_ref, u_ref, gate_ref, bias_ref, o_ref):
    for j in range(FFT_NB):
        y = _mm1(g_ref[j], jnp.concatenate([re_ref[j], im_ref[j]], axis=0).astype(jnp.bfloat16))
        o_ref[:, j, :] = gate_ref[:, j, :] * (y + u_ref[:, j, :] * bias_ref[...])


def fft_stage_a_inv(t2re, t2im, ginv, nb, u3, u_col, gate3, gate_col, bias):
    n1, nk, C = t2re.shape
    tb = pl.BlockSpec((FFT_NB, nk, CB), lambda c, i: (i, 0, c))
    gb = pl.BlockSpec((FFT_NB, nb, 2 * nk), lambda c, i: (i, 0, 0))
    ub = lambda col: pl.BlockSpec((nb, FFT_NB, CB), lambda c, i: (0, i, col + c))
    return pl.pallas_call(
        _stage_a_inv_kernel,
        grid=(C // CB, n1 // FFT_NB),
        in_specs=[tb, tb, gb, ub(u_col), ub(gate_col), pl.BlockSpec((1, CB), lambda c, i: (0, c))],
        out_specs=ub(0),
        out_shape=jax.ShapeDtypeStruct((nb, n1, C), jnp.float32),
        compiler_params=_params(2),
    )(t2re, t2im, ginv, u3, gate3, bias)


def hyena_long_conv_chain(u, L, biases, filt, colsum, tabs):
    C = GROUP_W
    n1 = FFT_N1
    n2 = 2 * L // n1
    nb = n2 // 2
    spec = fft_stage_b_filter(fft_stage_a(filt.reshape(nb, n1, filt.shape[1]), nb, 0, filt.shape[1] // CB, tabs["g"]),
                              tabs["ff"])
    s4 = colsum.reshape(HY_ORDER, 2, C)
    inv_norm = 1.0 / (s4[:, 0] + s4[:, 1])
    u3 = u.reshape(u.shape[0] // n1, n1, u.shape[1])
    y3, y_col = u3, 0
    for n in range(HY_ORDER):
        t1 = fft_stage_a(y3, nb, y_col, 1, tabs["g"])
        t2re, t2im = fft_stage_b_conv(t1, spec, 2 * n, 2 * n + 1, inv_norm[n][None, :], tabs["ff"], tabs["fft"])
        y3 = fft_stage_a_inv(t2re, t2im, tabs["ginv"], nb, y3, y_col, u3, n + 1, biases[n][None, :])
        y_col = 0
    return y3.reshape(L, C)


def dense_dft_tables(n):
    N = 2 * n
    k = jnp.arange(N, dtype=jnp.float32)[:, None]
    t = jnp.arange(n, dtype=jnp.float32)[None, :]
    ph = jnp.mod(k * t, float(N)) * (2.0 * math.pi / N)
    fd = jnp.concatenate([jnp.cos(ph), -jnp.sin(ph)], axis=0)
    return fd.astype(jnp.bfloat16), (fd.T / N).astype(jnp.bfloat16)


def _hyena_ctx_kernel(u_ref, h_ref, s_ref, bias_ref, fd_ref, ft_ref, o_ref):
    C = GROUP_W
    fd, ft = fd_ref[...], ft_ref[...]
    K = fd.shape[0] // 2
    bf = lambda t: t.astype(jnp.bfloat16)
    hs = _mm1(fd, bf(h_ref[...]))
    s = s_ref[...]
    y = u_ref[:, 0:C]
    for n in range(HY_ORDER):
        cp, cq = 2 * n * C, (2 * n + 1) * C
        inv = 1.0 / (s[:, cp:cp + C] + s[:, cq:cq + C])
        hr, hi = _filter_spectrum(hs[:, cp:cp + C], hs[:, cq:cq + C], inv, K)
        x = _mm1(fd, bf(y))
        xr, xi = x[:K], x[K:]
        conv = _mm1(ft, bf(jnp.concatenate([xr * hr - xi * hi, xr * hi + xi * hr], axis=0)))
        y = u_ref[:, (n + 1) * C:(n + 2) * C] * (conv + y * bias_ref[n:n + 1, :])
    o_ref[...] = y


def hyena_ctx(u, filt, colsum, biases, tabs):
    args = (u, filt, colsum, biases, tabs[0], tabs[1])
    return pl.pallas_call(
        _hyena_ctx_kernel,
        in_specs=[_full(a) for a in args],
        out_specs=pl.BlockSpec((u.shape[0], GROUP_W), lambda: (0, 0)),
        out_shape=jax.ShapeDtypeStruct((u.shape[0], GROUP_W), jnp.float32),
        compiler_params=pltpu.CompilerParams(vmem_limit_bytes=VMEM_LIMIT),
    )(*args)


SCAN_T = 64
GROUP_LANES = 256
RW_SUB = 16
HG_SUB = 16
LOG2_E = 1.4426950408889634


def _chunk_index(d, i, n_ctx, n_all):
    fwd = jnp.where(i < n_ctx, n_all - n_ctx + i, i - n_ctx)
    return jnp.where(d == 0, fwd, n_all - 1 - i)


def _stacking(T, S, nh, head, sign):
    G = nh * head
    nb = T // S
    n = nh * T
    rr = lax.broadcasted_iota(jnp.int32, (n, G), 0)
    same = ((rr // S) % nh) == (lax.broadcasted_iota(jnp.int32, (n, G), 1) // head)

    def bd(x):
        pieces = []
        for i in range(nb):
            pieces += [x[i * S:(i + 1) * S]] * nh
        return jnp.where(same, jnp.concatenate(pieces, axis=0), 0.0)

    def collapse(o):
        outs = []
        for i in range(nb):
            acc = o[i * nh * S:i * nh * S + S]
            for h in range(1, nh):
                acc = acc + o[i * nh * S + h * S:i * nh * S + (h + 1) * S]
            outs.append(acc)
        return jnp.concatenate(outs, axis=0)

    rt = lax.broadcasted_iota(jnp.int32, (n, n), 0)
    cs = lax.broadcasted_iota(jnp.int32, (n, n), 1)
    t_r = (rt // (nh * S)) * S + rt % S
    t_c = (cs // (nh * S)) * S + cs % S
    same_h = ((rt // S) % nh) == ((cs // S) % nh)
    before = same_h & ((t_r - t_c) * sign > 0)
    return bd, collapse, before, rt, cs


def _softplus(x):
    return jnp.maximum(x, 0.0) + jnp.log(1.0 + jnp.exp(-jnp.abs(x)))


def _rwkv_prep_kernel(*refs, n_lat, n_rows):
    zrefs, rest = refs[:15], refs[15:]
    (mu_ref, kk_w_ref, ka_ref, w0_ref, a0_ref, w2h_ref, w2l_ref, a2h_ref, a2l_ref, ones_ref,
     r_ref, k_ref, v_ref, g_ref, kk_ref, lw_ref, kd_ref, a_ref) = rest
    slabs = []
    for c in range(5):
        x = zrefs[3 * c][...]
        xp, xn = _neighbours(x, zrefs[3 * c + 1][...], zrefs[3 * c + 2][...], n_lat, n_rows)
        slabs.append(x + (0.5 * (xp + xn) - x) * mu_ref[:, c * CB:(c + 1) * CB])
    r, k, v, g, lora = slabs
    r_ref[...], k_ref[...], v_ref[...], g_ref[...] = r, k, v, g
    kk = k * kk_w_ref[...]
    ss = _mm2(kk * kk, ones_ref[...])
    kk_ref[...] = kk * lax.rsqrt(jnp.maximum(ss, 1e-24))
    lora_t = _split(jnp.tanh(lora))
    lora_s = _split(lora)
    for d in range(2):
        w_log = -_softplus(-(w0_ref[d:d + 1, :] + _mm3(lora_t, (w2h_ref[d], w2l_ref[d])))) - 0.5
        lw_ref[d] = -jnp.exp(w_log)
        a = _sigmoid(a0_ref[d:d + 1, :] + _mm3(lora_s, (a2h_ref[d], a2l_ref[d])))
        a_ref[d] = a
        kd_ref[d] = k * (1.0 + (a - 1.0) * ka_ref[...])


def rwkv_prep(z, mu, k_k, k_a, w0, a0, w2, a2, n_lat):
    n_rows = z.shape[0]
    tp = PREP_ROWS
    mu_p = jnp.pad(mu, (0, 5 * CB - RW_REAL))[None, :]
    w2p = jnp.zeros((2, CB, GROUP_W), jnp.float32)
    a2p = jnp.zeros((2, CB, GROUP_W), jnp.float32)
    for d in range(2):
        w2p = w2p.at[d, d * RW_LORA:(d + 1) * RW_LORA].set(w2[d])
        a2p = a2p.at[d, (2 + d) * RW_LORA:(3 + d) * RW_LORA].set(a2[d])
    w2s, a2s = _split(w2p), _split(a2p)
    small = (mu_p, k_k[None, :], k_a[None, :], w0, a0, w2s[0], w2s[1], a2s[0], a2s[1],
             _block_indicator(GROUP_W, RW_HEAD, 1.0))
    zspecs = []
    for c in range(5):
        zspecs += _halo_specs(tp, n_rows, COL_RW + c)
    one = pl.BlockSpec((tp, CB), lambda i: (i, 0))
    two = pl.BlockSpec((2, tp, CB), lambda i: (0, i, 0))
    s1 = jax.ShapeDtypeStruct((n_rows, GROUP_W), jnp.float32)
    s2 = jax.ShapeDtypeStruct((2, n_rows, GROUP_W), jnp.float32)
    return pl.pallas_call(
        functools.partial(_rwkv_prep_kernel, n_lat=n_lat, n_rows=n_rows),
        grid=(n_rows // tp,),
        in_specs=zspecs + [_full(a) for a in small],
        out_specs=[one] * 5 + [two] * 3,
        out_shape=[s1] * 5 + [s2] * 3,
        compiler_params=_params(1),
    )(*([z] * 15), *small)


def _rwkv_scan_body(refs, head):
    in_f, in_b, (of_ref, ob_ref, ht_ref) = refs[:6], refs[6:12], refs[12:]
    T, G, S = SCAN_T, GROUP_LANES, RW_SUB
    nh = G // head
    nb = T // S
    n = nh * T

    @pl.when(pl.program_id(0) == 0)
    def _():
        ht_ref[...] = jnp.zeros_like(ht_ref)

    ti = lax.broadcasted_iota(jnp.int32, (T, T), 0)
    si = lax.broadcasted_iota(jnp.int32, (T, T), 1)
    bf = lambda t: t.astype(jnp.bfloat16)
    each = lambda f, *seqs: [f(*xs) for xs in zip(*seqs)]
    n_grp = in_f[0].shape[1] // G
    lanes = [slice(grp * G, (grp + 1) * G) for grp in range(n_grp)]

    r, v, kk, lw, k, a, tri, before, incl, dest = ([] for _ in range(10))
    for (r_ref, v_ref, kk_ref, lw_ref, k_ref, a_ref), o_ref, sign in ((in_f, of_ref, 1), (in_b, ob_ref, -1)):
        bd, collapse, before_d, rt, cs = _stacking(T, S, nh, head, sign)
        for ls in lanes:
            r.append(r_ref[:, ls]), v.append(v_ref[:, ls]), kk.append(kk_ref[:, ls])
            lw.append(lw_ref[0, :, ls]), k.append(k_ref[0, :, ls]), a.append(a_ref[0, :, ls])
            tri.append(jnp.where((ti - si) * sign >= 0, 1.0, 0.0))
            before.append(before_d), incl.append(before_d | (rt == cs)), dest.append((o_ref, ls))
    diag_blk = (rt // (nh * S)) == (cs // (nh * S))
    eye = jnp.where(rt == cs, 1.0, 0.0)
    ht = [ht_ref[ch] for ch in range(len(r))]

    c = each(lambda tr, t: _dot(tr, t), tri, lw)
    ctot = each(lambda t: jnp.sum(t, axis=0, keepdims=True), lw)
    beta = each(lambda p, q: p * q, kk, a)
    einv = each(lambda t: jnp.exp(-t), c)
    lhs = each(lambda kk_, r_, c_, lw_: bf(jnp.concatenate([bd(-kk_ * jnp.exp(c_ - lw_)), bd(r_ * jnp.exp(c_))], axis=0)),
               kk, r, c, lw)
    rhs = each(lambda k_, b_, e_: bf(jnp.concatenate([bd(k_ * e_), bd(b_ * e_)], axis=0)), k, beta, einv)
    m = each(lambda p, q: _mm1(p, q, NT), lhs, rhs)
    g = each(lambda p, h: _mm1(p, bf(h), NT), lhs, ht)
    v_b = each(lambda t: bf(bd(t)), v)
    x = each(lambda g_, m_, v_, bm: g_[:n] + _mm1(bf(jnp.where(bm, m_[:n, :n], 0.0)), v_), g, m, v_b, before)
    lb = each(lambda m_, bm: jnp.where(bm, m_[:n, n:], 0.0), m, before)
    ld = each(lambda t: jnp.where(diag_blk, t, 0.0), lb)
    lo = each(lambda p, q: p - q, lb, ld)
    xd = each(lambda t: eye + t, ld)
    lp = each(bf, ld)
    p = 2
    while p < S:
        lp = each(lambda t: bf(_mm1(t, t)), lp)
        xd = each(lambda x_, l_: x_ + _mm1(l_, bf(x_)), xd, lp)
        p *= 2
    xd_b = each(bf, xd)
    n_b = each(lambda x_, l_: bf(_mm1(x_, bf(l_))), xd_b, lo)
    y = each(lambda x_, t: _mm1(x_, bf(t)), xd_b, x)
    y = each(lambda y_, n_: y_ + _mm1(n_, bf(y_)), y, n_b)
    p = 2
    while p < nb:
        n_b = each(lambda t: bf(_mm1(t, t)), n_b)
        y = each(lambda y_, n_: y_ + _mm1(n_, bf(y_)), y, n_b)
        p *= 2
    u_b = each(bf, y)
    o = each(lambda g_, m_, v_, u_, im: (g_[n:] + _mm1(bf(jnp.where(im, m_[n:, :n], 0.0)), v_))
             + _mm1(bf(jnp.where(im, m_[n:, n:], 0.0)), u_), g, m, v_b, u_b, incl)
    efin = each(lambda ct, c_: jnp.exp(ct - c_), ctot, c)
    ht_new = each(lambda h, ct, v_, u_, k_, b_, e_: (h * jnp.exp(ct) + _mm1(v_, bf(bd(k_ * e_)), TN))
                  + _mm1(u_, bf(bd(b_ * e_)), TN), ht, ctot, v_b, u_b, k, beta, efin)
    for ch, (o_ref, ls) in enumerate(dest):
        o_ref[:, ls] = collapse(o[ch])
        ht_ref[ch] = ht_new[ch]


def rope_tables(n_ctx, L):
    d_axis = RT_HEAD // 2
    half = d_axis // 2
    inv = ROPE_BASE ** (-jnp.arange(0, d_axis, 2, dtype=jnp.float32) / d_axis)
    t = jnp.arange(L)
    pos = jnp.stack([(t // GRID_W).astype(jnp.float32), (t % GRID_W).astype(jnp.float32)], axis=1)
    j = jnp.arange(RT_HEAD)
    ang = pos[:, j // d_axis] * inv[j % half][None, :]
    sgn = jnp.where((j % d_axis) < half, -1.0, 1.0)[None, :]
    cos = jnp.concatenate([jnp.cos(ang), jnp.ones((n_ctx, RT_HEAD), jnp.float32)], axis=0)
    sin = jnp.concatenate([jnp.sin(ang) * sgn, jnp.zeros((n_ctx, RT_HEAD), jnp.float32)], axis=0)
    return jnp.tile(cos, (1, RT_HEADS)), jnp.tile(sin, (1, RT_HEADS))


def _rotate(x, cos, sin):
    G = x.shape[1]
    half = RT_HEAD // 4
    lane = lax.broadcasted_iota(jnp.int32, x.shape, 1)
    partner = jnp.where((lane % (2 * half)) < half, pltpu.roll(x, G - half, axis=1), pltpu.roll(x, half, axis=1))
    return x * cos + partner * sin


def _retention_scan_body(q_ref, k_ref, v_ref, cos_ref, sin_ref, lg_ref, o_ref, st_ref, *, head, d):
    T, G = SCAN_T, GROUP_LANES
    nh = G // head
    sign = 1 - 2 * d

    t = lax.broadcasted_iota(jnp.int32, (T, 1), 0)
    pos = (t + d * (T - 1 - 2 * t) + 1).astype(jnp.float32)
    bd, collapse, before, rt, cs = _stacking(T, T, nh, head, sign)
    incl = before | (rt == cs)
    bf = lambda x: x.astype(jnp.bfloat16)

    groups = range(q_ref.shape[1] // G)
    lanes = [slice(grp * G, (grp + 1) * G) for grp in groups]
    each = lambda f, *seqs: [f(*xs) for xs in zip(*seqs)]
    st = [st_ref[grp] for grp in groups]
    cos, sin, lg = ([ref[:, ls] for ls in lanes] for ref in (cos_ref, sin_ref, lg_ref))
    q = each(_rotate, [q_ref[:, ls] for ls in lanes], cos, sin)
    k = each(lambda x, c_, s_: _rotate(x, c_, s_) * (head ** -0.5), [k_ref[:, ls] for ls in lanes], cos, sin)
    c = each(lambda lg_: pos * lg_, lg)
    q_b = each(lambda q_, c_: bf(bd(q_ * jnp.exp(c_))), q, c)
    kt_b = each(lambda k_, c_: bf(bd(k_ * jnp.exp(-c_))), k, c)
    v_b = [bf(bd(v_ref[:, ls])) for ls in lanes]
    scores = each(lambda q_, k_: bf(jnp.where(incl, _mm1(q_, k_, NT), 0.0)), q_b, kt_b)
    inter = each(lambda q_, s_: _mm1(q_, bf(s_), NT), q_b, st)
    o = each(lambda g_, s_, v_: g_ + _mm1(s_, v_), inter, scores, v_b)
    st_new = each(lambda s_, lg_, v_, k_, c_: s_ * jnp.exp(float(T) * lg_)
                  + _mm1(v_, bf(bd(k_ * jnp.exp(float(T) * lg_ - c_))), TN), st, lg, v_b, k, c)
    for grp in groups:
        o_ref[:, lanes[grp]] = collapse(o[grp])
        st_ref[grp] = st_new[grp]


def _gla_scan_body(q_ref, f_ref, i_ref, lb_ref, o_ref, st_ref, *, head, d):
    T, S = SCAN_T, HG_SUB
    W = q_ref.shape[-1]
    nh = W // head
    sign = 1 - 2 * d

    ti = lax.broadcasted_iota(jnp.int32, (S, S), 0)
    si = lax.broadcasted_iota(jnp.int32, (S, S), 1)
    tri_incl = jnp.where((ti - si) * sign >= 0, 1.0, 0.0)
    row = lax.broadcasted_iota(jnp.int32, (S, 1), 0)
    lb = lb_ref[0]

    states = [st_ref[h] for h in range(nh)]
    for j in range(T // S):
        jb = j + d * (T // S - 1 - 2 * j)
        rows = pl.ds(jb * S, S)
        q = q_ref[rows, :]
        q = q * _sigmoid(q)
        v = i_ref[rows, :]
        gate = lb + (1.0 - lb) * _sigmoid(f_ref[rows, :])
        lf = jnp.log(jnp.maximum(gate, HG_MIN_GATE))
        k = 1.0 - gate
        b = _dot(tri_incl, lf)
        btot = jnp.sum(lf, axis=0, keepdims=True)
        qe = q * jnp.exp(b)
        ke = k * jnp.exp(btot - b)
        outs = []
        for h in range(nh):
            ls = slice(h * head, (h + 1) * head)
            qh, kh, vh = q[:, ls], k[:, ls], v[:, ls]
            bh = b[:, ls] * LOG2_E
            st = states[h]
            o = _mm1(qe[:, ls].astype(jnp.bfloat16), st.astype(jnp.bfloat16), NT)
            for s in range(S):
                e = jnp.exp2(bh - bh[s:s + 1, :])
                a_s = jnp.sum(qh * kh[s:s + 1, :] * e, axis=-1, keepdims=True)
                a_s = jnp.where((row - s) * sign >= 0, a_s, 0.0)
                o = o + a_s * vh[s:s + 1, :]
            outs.append(o)
            states[h] = st * jnp.exp(btot[:, ls]) + _mm1(vh.astype(jnp.bfloat16), ke[:, ls].astype(jnp.bfloat16), TN)
        o_ref[rows, :] = jnp.concatenate(outs, axis=1)
    for h in range(nh):
        st_ref[h] = states[h]


N_RW, N_RT, N_HG = 6, 6, 4


def _scans_kernel(*refs):
    n_in = 2 * (N_RW + N_RT + N_HG)
    ins, (rw_of, rw_ob, rt_of, rt_ob, hg_of, hg_ob, rw_st, rt_st, hg_st) = refs[:n_in], refs[n_in:]
    rw_in, rt_in, hg_in = ins[:2 * N_RW], ins[2 * N_RW:2 * (N_RW + N_RT)], ins[2 * (N_RW + N_RT):]

    @pl.when(pl.program_id(0) == 0)
    def _():
        rt_st[...] = jnp.zeros_like(rt_st)
        hg_st[...] = jnp.zeros_like(hg_st)

    n_rt, n_hg = rt_st.shape[0] // 2, hg_st.shape[0] // 2
    _rwkv_scan_body(list(rw_in) + [rw_of, rw_ob, rw_st], RW_HEAD)
    for d, (rt_o, hg_o) in enumerate(((rt_of, hg_of), (rt_ob, hg_ob))):
        _gla_scan_body(*hg_in[d * N_HG:(d + 1) * N_HG], hg_o, hg_st.at[d * n_hg:(d + 1) * n_hg], head=HG_HEAD, d=d)
        _retention_scan_body(*rt_in[d * N_RT:(d + 1) * N_RT], rt_o, rt_st.at[d * n_rt:(d + 1) * n_rt],
                             head=RT_HEAD, d=d)


def recurrent_scans(z, rw, cos, sin, lg, lb2, n_ctx_rows):
    N = z.shape[0]
    W = GROUP_W
    T = SCAN_T
    n_all, n_ctx = N // T, n_ctx_rows // T
    chunk = lambda d: (lambda i: (_chunk_index(d, i, n_ctx, n_all), 0))
    row = lambda d: pl.BlockSpec((T, W), chunk(d))
    per_dir = lambda d: pl.BlockSpec((1, T, W), lambda i: (d, _chunk_index(d, i, n_ctx, n_all), 0))
    zc = lambda d, col: pl.BlockSpec((T, CB), lambda i: (_chunk_index(d, i, n_ctx, n_all), col))
    rw_specs = lambda d: [row(d)] * 3 + [per_dir(d)] * 3
    rt_specs = lambda d: [zc(d, COL_RT), zc(d, COL_RT + 1), zc(d, COL_RT + 2), row(d), row(d),
                          pl.BlockSpec((1, W), lambda i: (0, 0))]
    hg_specs = lambda d: [zc(d, COL_HG), zc(d, COL_HG + 1 + d), zc(d, COL_HG + 3),
                          pl.BlockSpec((1, 1, W), lambda i: (d, 0, 0))]
    shape = jax.ShapeDtypeStruct((N, W), jnp.float32)
    return pl.pallas_call(
        _scans_kernel,
        grid=(n_all,),
        in_specs=rw_specs(0) + rw_specs(1) + rt_specs(0) + rt_specs(1) + hg_specs(0) + hg_specs(1),
        out_specs=[row(0), row(1)] * 3,
        out_shape=[shape] * 6,
        scratch_shapes=[pltpu.VMEM((2 * (W // GROUP_LANES), GROUP_LANES, GROUP_LANES), jnp.float32),
                        pltpu.VMEM((2 * (W // GROUP_LANES), GROUP_LANES, GROUP_LANES), jnp.float32),
                        pltpu.VMEM((2 * (W // HG_HEAD), HG_HEAD, HG_HEAD), jnp.float32)],
        compiler_params=_params(1),
    )(*(list(rw) * 2), *([z, z, z, cos, sin, lg] * 2), *([z, z, z, lb2] * 2))


def _outproj_kernel(hy_ref, rwof_ref, rwob_ref, r_ref, k_ref, v_ref, rwg_ref, rtof_ref, rtob_ref, rtg_ref,
                    hgof_ref, hgob_ref, hgg_ref,
                    rk_ref, gng_ref, gnb_ref, hgn_ref, avg64_ref, avg128_ref,
                    w_ref, x_ref, gate_ref, g_ref, b_ref, o_ref, *, n_lat):
    avg64, avg128 = avg64_ref[...], avg128_ref[...]

    def head_norm(o, avg, eps, centre):
        if centre:
            o = o - _mm2(o, avg)
        return o * lax.rsqrt(_mm2(o * o, avg) + eps)

    silu = lambda t: t * _sigmoid(t)
    y_rw = head_norm(rwof_ref[...] + rwob_ref[...], avg64, RW_GN_EPS, True) * gng_ref[...] + gnb_ref[...]
    bonus = (float(RW_HEAD) * _mm2(r_ref[...] * k_ref[...] * rk_ref[...], avg64)) * v_ref[...]
    y_rw = (y_rw + bonus) * _sigmoid(rwg_ref[...])
    y_rt = head_norm(rtof_ref[...] + rtob_ref[...], avg64, HEAD_NORM_EPS, True) * silu(rtg_ref[...])
    y_hg = head_norm(hgof_ref[...] + hgob_ref[...], avg128, HEAD_NORM_EPS, False) * hgn_ref[...] * silu(hgg_ref[...])
    y = None
    for m, ym in enumerate((hy_ref[...], y_rw, y_rt, y_hg)):
        part = jnp.dot(ym.astype(jnp.bfloat16), w_ref[m * GROUP_W:(m + 1) * GROUP_W, :],
                       preferred_element_type=jnp.float32)
        y = part if y is None else y + part
    is_ctx = _ctx_rows(x_ref.shape[0], n_lat)
    r = ALPHA * x_ref[...] + _mod_row(gate_ref, is_ctx) * y
    o_ref[...] = _ln_rows(r) * g_ref[...] + b_ref[...]


def outproj_deepnorm(y_hy, scans, rw_r, rw_k, rw_v, rw_g, z, r_k, gn_g, gn_b, hg_norm_g,
                     w_bf16, x, mod, g, b, n_lat, m):
    tm = OUT_TM
    one = pl.BlockSpec((tm, GROUP_W), lambda i: (i, 0))
    zc = lambda col: pl.BlockSpec((tm, CB), lambda i: (i, col))
    row = pl.BlockSpec((tm, D_MODEL), lambda i: (i, 0))
    small = (r_k, gn_g, gn_b, hg_norm_g, _block_indicator(GROUP_W, RW_HEAD, 1.0 / RW_HEAD),
             _block_indicator(GROUP_W, HG_HEAD, 1.0 / HG_HEAD))
    return pl.pallas_call(
        functools.partial(_outproj_kernel, n_lat=n_lat),
        grid=(m // tm,),
        in_specs=[one] * 9 + [zc(COL_RT + 3), one, one, zc(COL_HG + 4)]
        + [_full(a) for a in small]
        + [_full(w_bf16), row, pl.BlockSpec((8, D_MODEL), lambda i: (0, 2)), _full(g), _full(b)],
        out_specs=row,
        out_shape=jax.ShapeDtypeStruct((m, D_MODEL), jnp.float32),
        compiler_params=_params(1),
    )(y_hy, scans[0], scans[1], rw_r, rw_k, rw_v, rw_g, scans[2], scans[3], z, scans[4], scans[5], z,
      *small, w_bf16, x, mod, g, b)


def _ffn_kernel(x_ref, sh_ref, sc_ref, w1_ref, w3_ref, w2_ref, gate_ref, g_ref, b_ref, o_ref, h_ref, acc_ref, *, n_lat):
    j = pl.program_id(1)

    @pl.when(j == 0)
    def _():
        is_ctx = _ctx_rows(x_ref.shape[0], n_lat)
        h = _ln_rows(x_ref[...]) * (1.0 + _mod_row(sc_ref, is_ctx)) + _mod_row(sh_ref, is_ctx)
        h_ref[...] = h.astype(jnp.bfloat16)
        acc_ref[...] = jnp.zeros_like(acc_ref)

    h = h_ref[...]
    a = jnp.dot(h, w1_ref[...], preferred_element_type=jnp.float32)
    u = jnp.dot(h, w3_ref[...], preferred_element_type=jnp.float32)
    s = (a * _sigmoid(a) * u).astype(jnp.bfloat16)
    acc_ref[...] += jnp.dot(s, w2_ref[...], preferred_element_type=jnp.float32)

    @pl.when(j == pl.num_programs(1) - 1)
    def _():
        is_ctx = _ctx_rows(x_ref.shape[0], n_lat)
        r = ALPHA * x_ref[...] + _mod_row(gate_ref, is_ctx) * acc_ref[...]
        o_ref[...] = _ln_rows(r) * g_ref[...] + b_ref[...]


def ffn_deepnorm(x, mod, w1, w3, w2, g, b, n_lat, tm):
    m = x.shape[0]
    row = pl.BlockSpec((tm, D_MODEL), lambda i, j: (i, 0))
    modc = lambda c: pl.BlockSpec((8, D_MODEL), lambda i, j: (0, c))
    vec = pl.BlockSpec((1, D_MODEL), lambda i, j: (0, 0))
    return pl.pallas_call(
        functools.partial(_ffn_kernel, n_lat=n_lat),
        grid=(m // tm, FFN_HIDDEN // FFN_TF),
        in_specs=[row, modc(3), modc(4),
                  pl.BlockSpec((D_MODEL, FFN_TF), lambda i, j: (0, j)),
                  pl.BlockSpec((D_MODEL, FFN_TF), lambda i, j: (0, j)),
                  pl.BlockSpec((FFN_TF, D_MODEL), lambda i, j: (j, 0)),
                  modc(5), vec, vec],
        out_specs=row,
        out_shape=jax.ShapeDtypeStruct((m, D_MODEL), jnp.float32),
        scratch_shapes=[pltpu.VMEM((tm, D_MODEL), jnp.bfloat16),
                        pltpu.VMEM((tm, D_MODEL), jnp.float32)],
        compiler_params=_params(2),
    )(x, mod, mod, w1, w3, w2, mod, g, b)


def kernel(x, c, ctx, c_ctx, ada_w, ada_b, w_in, w_out, ln_g, ln_b, hy_conv_w, hy_conv_b, hy_w1, hy_b1, hy_w2, hy_b2, hy_w3, hy_b3, hy_freq, hy_bias, rw_mu, rw_w0, rw_w2, rw_a0, rw_a2, rw_k_k, rw_k_a, rw_r_k, rw_gn_g, rw_gn_b, hg_lb_raw, hg_norm_g, ffn_w1, ffn_w3, ffn_w2):
    L, n_ctx = x.shape[1], ctx.shape[1]
    sm = jax.nn.softmax(hg_lb_raw.astype(jnp.float32), axis=1)
    lower_bounds = jnp.cumsum(sm, axis=1) - sm[:, :1]
    log_gamma = jnp.log1p(-jnp.exp2(-5.0 - jnp.arange(RT_HEADS, dtype=jnp.float32)))
    lg = jnp.repeat(log_gamma, RT_HEAD)[None, :]
    cos, sin = rope_tables(n_ctx, L)
    fft_tabs = fft_tables(FFT_N1, 2 * L // FFT_N1)
    ctx_tabs = dense_dft_tables(n_ctx)

    c8 = jnp.zeros((8, D_MODEL), jnp.float32).at[0].set(c[0]).at[1].set(c_ctx)
    xs = jnp.concatenate([x[0], ctx[0]], axis=0)
    n_rows = L + n_ctx
    for l in range(DEPTH):
        with_ctx = l < DEPTH - 1
        mod = ada_modulation(c8, ada_w, l, ada_b[l][None, :])
        w_out_b = w_out[l].astype(jnp.bfloat16)
        w1_b, w3_b, w2_b = (w[l].astype(jnp.bfloat16) for w in (ffn_w1, ffn_w3, ffn_w2))

        z = modulated_projection(xs, mod, w_in, l, L)

        u = hyena_conv3(z, hy_conv_w[l], hy_conv_b[l][None, :], L)
        hy_w = (hy_w1[l], hy_b1[l], hy_w2[l], hy_b2[l], hy_w3[l], hy_b3[l], hy_freq[l])
        filt, colsum = hyena_filter_bank(L, *hy_w)
        y_lat = hyena_long_conv_chain(u, L, hy_bias[l], filt, colsum, fft_tabs)
        if with_ctx:
            filt_c, colsum_c = hyena_filter_bank(n_ctx, *hy_w)
            y_ctx = hyena_ctx(u[L:], filt_c, colsum_c, hy_bias[l], ctx_tabs)
        else:
            y_ctx = jnp.zeros((n_ctx, GROUP_W), jnp.float32)
        y_hy = jnp.concatenate([y_lat, y_ctx], axis=0)

        rw_r, rw_k, rw_v, rw_g, rw_kk, rw_lw, rw_kd, rw_a = rwkv_prep(
            z, rw_mu[l], rw_k_k[l], rw_k_a[l], rw_w0[l], rw_a0[l], rw_w2[l], rw_a2[l], L)
        scans = recurrent_scans(z, (rw_r, rw_v, rw_kk, rw_lw, rw_kd, rw_a), cos, sin, lg,
                                lower_bounds[:, l][:, None, :], n_ctx)

        xs = outproj_deepnorm(y_hy, scans, rw_r, rw_k, rw_v, rw_g, z,
                              rw_r_k[l].reshape(1, GROUP_W), rw_gn_g[l][None, :], rw_gn_b[l][None, :],
                              hg_norm_g[l][None, :], w_out_b, xs, mod, ln_g[l, 0][None, :], ln_b[l, 0][None, :],
                              L, n_rows if with_ctx else L)
        xs = ffn_deepnorm(xs, mod, w1_b, w3_b, w2_b, ln_g[l, 1][None, :], ln_b[l, 1][None, :], L,
                          FFN_TM if with_ctx else FFN_LAST_TM)
    return xs[None]
```

```python
import functools
import math

import jax
import jax.numpy as jnp
from jax import lax
from jax.experimental import pallas as pl
from jax.experimental.pallas import tpu as pltpu

D_MODEL = 2048
DEPTH = 2
GRID_W = 64
N_MIXERS = 4
GROUP_W = D_MODEL // N_MIXERS
HY_ORDER = 2
HY_EMB = 33
HY_FAST_DECAY = 0.3
HY_SLOW_DECAY = 1.5
HY_TARGET = 1e-2
RW_HEAD = 64
RW_LORA = 96
RW_GN_EPS = 64e-5
RT_HEAD = 64
RT_HEADS = GROUP_W // RT_HEAD
ROPE_BASE = 10000.0
HG_HEAD = 128
HG_MIN_GATE = 1e-30
FFN_HIDDEN = 5632
ALPHA = (2 * DEPTH) ** 0.25
LN_EPS = 1e-6
HEAD_NORM_EPS = 1e-6

LANES = 128
CB = 512
COL_HY = 0
COL_RW = 3
COL_RT = 8
COL_HG = 12
P_IN_PAD = 17 * CB
RW_REAL = 4 * GROUP_W + 4 * RW_LORA
PROJ_TM = 1408
FFN_TM = 768
FFN_LAST_TM = 512
OUT_TM = 256
FFN_TF = 512
PREP_ROWS = 256
HALO = 8
VMEM_LIMIT = 56 * 1024 * 1024

HI = lax.Precision.HIGHEST
NN = ((1,), (0,))
NT = ((1,), (1,))
TN = ((0,), (0,))


def _params(n_axes):
    return pltpu.CompilerParams(dimension_semantics=("arbitrary",) * n_axes, vmem_limit_bytes=VMEM_LIMIT)


def _full(a):
    return pl.BlockSpec(a.shape, lambda *_: (0,) * a.ndim)


def _dot(a, b):
    return jnp.dot(a, b, precision=HI, preferred_element_type=jnp.float32)


def _split(x):
    hi = x.astype(jnp.bfloat16)
    lo = (x - hi.astype(jnp.float32)).astype(jnp.bfloat16)
    return hi, lo


def _mm3(a, b, dims=NN):
    d = lambda p, q: lax.dot_general(p, q, (dims, ((), ())), preferred_element_type=jnp.float32)
    return d(a[0], b[0]) + (d(a[0], b[1]) + d(a[1], b[0]))


def _mm1(a, b, dims=NN):
    return lax.dot_general(a, b, (dims, ((), ())), preferred_element_type=jnp.float32)


def _mm2(x, b):
    hi, lo = _split(x)
    d = functools.partial(jnp.dot, preferred_element_type=jnp.float32)
    return d(hi, b) + d(lo, b)


def _ln_rows(x):
    mu = jnp.mean(x, axis=-1, keepdims=True)
    xc = x - mu
    var = jnp.mean(xc * xc, axis=-1, keepdims=True)
    return xc * lax.rsqrt(var + LN_EPS)


def _sigmoid(x):
    return 1.0 / (1.0 + jnp.exp(-x))


def _ctx_rows(tm, n_lat):
    return (pl.program_id(0) * tm + lax.broadcasted_iota(jnp.int32, (tm, 1), 0)) >= n_lat


def _mod_row(ref, is_ctx):
    return jnp.where(is_ctx, ref[1:2, :], ref[0:1, :])


def _block_indicator(width, head, value):
    i = jnp.arange(width) // head
    return jnp.where(i[:, None] == i[None, :], value, 0.0).astype(jnp.bfloat16)


def _ada_kernel(c_ref, w_ref, b_ref, o_ref):
    c = c_ref[...]
    h = c * _sigmoid(c)
    o_ref[...] = jnp.dot(h.astype(jnp.bfloat16), w_ref[...].astype(jnp.bfloat16),
                         preferred_element_type=jnp.float32) + b_ref[...]


def ada_modulation(c8, w, layer, b):
    n = w.shape[2]
    tn = 1024
    return pl.pallas_call(
        _ada_kernel,
        grid=(n // tn,),
        in_specs=[pl.BlockSpec((8, D_MODEL), lambda j: (0, 0)),
                  pl.BlockSpec((None, D_MODEL, tn), lambda j: (layer, 0, j)),
                  pl.BlockSpec((1, tn), lambda j: (0, j))],
        out_specs=pl.BlockSpec((8, tn), lambda j: (0, j)),
        out_shape=jax.ShapeDtypeStruct((8, n), jnp.float32),
        compiler_params=_params(1),
    )(c8, w, b)


W_SUB = CB // LANES
W_PAD_PIECE = (COL_RW * CB + RW_REAL) // LANES


def _proj_kernel(x_ref, sh_ref, sc_ref, *refs, n_lat):
    w_refs, (o_ref, h_ref) = refs[:W_SUB], refs[W_SUB:]
    j = pl.program_id(1)

    @pl.when(j == 0)
    def _():
        is_ctx = _ctx_rows(x_ref.shape[0], n_lat)
        h = _ln_rows(x_ref[...]) * (1.0 + _mod_row(sc_ref, is_ctx)) + _mod_row(sh_ref, is_ctx)
        h_ref[...] = h.astype(jnp.bfloat16)

    pieces = [w_refs[k][...] for k in range(W_SUB)]
    for k in range(W_SUB):
        if (W_PAD_PIECE - k) % W_SUB == 0:
            pieces[k] = jnp.where(j * W_SUB + k == W_PAD_PIECE, 0.0, pieces[k])
    w = jnp.concatenate(pieces, axis=1).astype(jnp.bfloat16)
    o_ref[...] = jnp.dot(h_ref[...], w, preferred_element_type=jnp.float32)


def modulated_projection(x, mod, w, layer, n_lat):
    m = x.shape[0]
    tm = PROJ_TM

    def piece(k):
        def index(i, j):
            p = j * W_SUB + k
            return layer, 0, jnp.where(p < W_PAD_PIECE, p, p - 1)
        return pl.BlockSpec((None, D_MODEL, LANES), index)

    return pl.pallas_call(
        functools.partial(_proj_kernel, n_lat=n_lat),
        grid=(m // tm, P_IN_PAD // CB),
        in_specs=[pl.BlockSpec((tm, D_MODEL), lambda i, j: (i, 0)),
                  pl.BlockSpec((8, D_MODEL), lambda i, j: (0, 0)),
                  pl.BlockSpec((8, D_MODEL), lambda i, j: (0, 1))] + [piece(k) for k in range(W_SUB)],
        out_specs=pl.BlockSpec((tm, CB), lambda i, j: (i, j)),
        out_shape=jax.ShapeDtypeStruct((m, P_IN_PAD), jnp.float32),
        scratch_shapes=[pltpu.VMEM((tm, D_MODEL), jnp.bfloat16)],
        compiler_params=_params(2),
    )(x, mod, mod, *([w] * W_SUB))


def _halo_specs(tp, n_rows, col):
    per = tp // HALO
    last = n_rows // HALO - 1
    main = pl.BlockSpec((tp, CB), lambda i, *_: (i, col(*_) if callable(col) else col))
    prev = pl.BlockSpec((HALO, CB), lambda i, *_: (jnp.maximum(i * per - 1, 0), col(*_) if callable(col) else col))
    nxt = pl.BlockSpec((HALO, CB), lambda i, *_: (jnp.minimum((i + 1) * per, last), col(*_) if callable(col) else col))
    return [main, prev, nxt]


def _neighbours(x, prev, nxt, n_lat, n_rows):
    tp = x.shape[0]
    loc = lax.broadcasted_iota(jnp.int32, (tp, 1), 0)
    row = pl.program_id(0) * tp + loc
    xp = jnp.where(loc == 0, prev[HALO - 1:HALO, :], pltpu.roll(x, 1, axis=0))
    xp = jnp.where((row == 0) | (row == n_lat), 0.0, xp)
    xn = jnp.where(loc == tp - 1, nxt[0:1, :], pltpu.roll(x, tp - 1, axis=0))
    xn = jnp.where((row == n_lat - 1) | (row == n_rows - 1), 0.0, xn)
    return xp, xn


def _conv3_kernel(z_ref, zp_ref, zn_ref, w_ref, b_ref, o_ref, *, n_lat, n_rows):
    x = z_ref[...]
    xp, xn = _neighbours(x, zp_ref[...], zn_ref[...], n_lat, n_rows)
    w = w_ref[...]
    o_ref[...] = xp * w[0:1, :] + x * w[1:2, :] + xn * w[2:3, :] + b_ref[...]


def hyena_conv3(z, w, b, n_lat):
    n_rows = z.shape[0]
    tp = PREP_ROWS
    ncol = w.shape[1] // CB
    return pl.pallas_call(
        functools.partial(_conv3_kernel, n_lat=n_lat, n_rows=n_rows),
        grid=(n_rows // tp, ncol),
        in_specs=_halo_specs(tp, n_rows, lambda c: COL_HY + c)
        + [pl.BlockSpec((3, CB), lambda i, c: (0, c)), pl.BlockSpec((1, CB), lambda i, c: (0, c))],
        out_specs=pl.BlockSpec((tp, CB), lambda i, c: (i, c)),
        out_shape=jax.ShapeDtypeStruct((n_rows, w.shape[1]), jnp.float32),
        compiler_params=_params(2),
    )(z, z, z, w, b)


HY_EMB_PAD = 40
HY_FILTER_ROWS = 512


def hyena_features(L):
    t = jnp.linspace(0.0, 1.0, L, dtype=jnp.float32)[:, None]
    n_bands = (HY_EMB - 1) // 2
    f = jnp.linspace(1e-4, n_bands - 1, n_bands, dtype=jnp.float32)[None, :]
    ang = (2.0 * math.pi / L) * jnp.arange(L, dtype=jnp.float32)[:, None] * f
    z = jnp.concatenate([t, jnp.cos(ang), -jnp.sin(ang)], -1)
    return jnp.pad(z, ((0, 0), (0, HY_EMB_PAD - HY_EMB)))


def _filter_kernel(z_ref, w1_ref, b1_ref, w2_ref, b2_ref, w3_ref, b3_ref, fr_ref, dl_ref, h_ref, s_ref):
    i = pl.program_id(0)
    z = z_ref[...]
    fr = fr_ref[...]
    h = jnp.sin(fr * (_dot(z, w1_ref[...]) + b1_ref[...]))
    h = jnp.sin(fr * (_dot(h, w2_ref[...]) + b2_ref[...]))
    h = _mm1(h.astype(jnp.bfloat16), w3_ref[...].astype(jnp.bfloat16)) + b3_ref[...]
    win = jnp.exp(-z[:, 0:1] * dl_ref[...])
    h = h * jnp.concatenate([win] * (h.shape[1] // win.shape[1]), axis=1)

    @pl.when(i == 0)
    def _():
        s_ref[...] = jnp.zeros_like(s_ref)

    s_ref[...] += jnp.sum(jnp.abs(h), axis=0, keepdims=True)
    row = lax.broadcasted_iota(jnp.int32, h.shape, 0) + i * h.shape[0]
    col = lax.broadcasted_iota(jnp.int32, h.shape, 1)
    neg = (col // GROUP_W) % 2 == 1
    h_ref[...] = jnp.where(neg & (row == 0), 0.0, h)


def hyena_filter_bank(L, w1, b1, w2, b2, w3, b3, freq):
    z = hyena_features(L)
    w1p = jnp.pad(w1, ((0, HY_EMB_PAD - HY_EMB), (0, 0)))
    max_decay = math.log(HY_TARGET) / HY_FAST_DECAY
    min_decay = math.log(HY_TARGET) / HY_SLOW_DECAY
    deltas = jnp.abs(jnp.linspace(min_decay, max_decay, GROUP_W, dtype=jnp.float32))[None, :]
    n = w3.shape[1]
    tr = min(L, HY_FILTER_ROWS)
    args = (z, w1p, b1[None, :], w2, b2[None, :], w3, b3[None, :], freq[None, :], deltas)
    return pl.pallas_call(
        _filter_kernel,
        grid=(L // tr,),
        in_specs=[pl.BlockSpec((tr, HY_EMB_PAD), lambda i: (i, 0))] + [_full(a) for a in args[1:]],
        out_specs=[pl.BlockSpec((tr, n), lambda i: (i, 0)), pl.BlockSpec((1, n), lambda i: (0, 0))],
        out_shape=[jax.ShapeDtypeStruct((L, n), jnp.float32), jax.ShapeDtypeStruct((1, n), jnp.float32)],
        compiler_params=_params(1),
    )(*args)


FFT_NB = 8
FFT_CB = 512
FFT_N1 = 128


def fft_tables(n1, n2):
    N = n1 * n2
    nk = -(-(n2 // 2 + 1) // FFT_NB) * FFT_NB
    a = jnp.arange(n1, dtype=jnp.float32)[:, None, None]
    k2 = jnp.arange(nk, dtype=jnp.float32)[None, :, None]
    b = jnp.arange(n2 // 2, dtype=jnp.float32)[None, None, :]
    ph = (jnp.mod(a * k2, float(N)) / N + jnp.mod(b * k2, float(n2)) / n2) * (-2.0 * math.pi)
    g = jnp.concatenate([jnp.cos(ph), jnp.sin(ph)], axis=1)
    kk = jnp.arange(nk)
    weight = jnp.where((kk == 0) | (kk == n2 // 2), 1.0, jnp.where(kk < n2 // 2, 2.0, 0.0))
    ginv = jnp.transpose(g, (0, 2, 1)) * (jnp.tile(weight, 2) / N)
    k1 = jnp.arange(n1, dtype=jnp.float32)[:, None]
    aa = jnp.arange(n1, dtype=jnp.float32)[None, :]
    f = jnp.mod(k1 * aa, float(n1)) * (-2.0 * math.pi / n1)
    fr, fi = jnp.cos(f), jnp.sin(f)
    ff = jnp.concatenate([jnp.concatenate([fr, -fi], axis=1), jnp.concatenate([fi, fr], axis=1)], axis=0)
    bf = lambda t: t.astype(jnp.bfloat16)
    return {"g": bf(g), "ginv": bf(ginv), "ff": bf(ff), "fft": bf(ff.T)}


def _stage_a_kernel(u_ref, g_ref, o_ref):
    for j in range(FFT_NB):
        o_ref[j] = _mm1(g_ref[j], u_ref[:, j, :].astype(jnp.bfloat16))


def fft_stage_a(u3, nb, col, ncol, g):
    n1 = u3.shape[1]
    rows = g.shape[1]
    gspec = pl.BlockSpec((FFT_NB, rows, nb), lambda c, i: (i, 0, 0))
    return pl.pallas_call(
        _stage_a_kernel,
        grid=(ncol, n1 // FFT_NB),
        in_specs=[pl.BlockSpec((nb, FFT_NB, CB), lambda c, i: (0, i, col + c)), gspec],
        out_specs=pl.BlockSpec((FFT_NB, rows, CB), lambda c, i: (i, 0, c)),
        out_shape=jax.ShapeDtypeStruct((n1, rows, ncol * CB), jnp.float32),
        compiler_params=_params(2),
    )(u3, g)


def _stage_b_filter_kernel(re_ref, im_ref, f_ref, o_ref):
    ff = f_ref[...]
    for j in range(FFT_NB):
        o_ref[j] = _mm1(ff, jnp.concatenate([re_ref[:, j, :], im_ref[:, j, :]], axis=0).astype(jnp.bfloat16))


def fft_stage_b_filter(t1, ff):
    n1, n2x2, C = t1.shape
    n2 = n2x2 // 2
    blk = lambda off: pl.BlockSpec((n1, FFT_NB, FFT_CB), lambda c, i: (0, i + off, c))
    mat = pl.BlockSpec((2 * n1, 2 * n1), lambda c, i: (0, 0))
    return pl.pallas_call(
        _stage_b_filter_kernel,
        grid=(C // FFT_CB, n2 // FFT_NB),
        in_specs=[blk(0), blk(n2 // FFT_NB), mat],
        out_specs=pl.BlockSpec((FFT_NB, 2 * n1, FFT_CB), lambda c, i: (i, 0, c)),
        out_shape=jax.ShapeDtypeStruct((n2, 2 * n1, C), jnp.float32),
        compiler_params=_params(2),
    )(t1, t1, ff)


def _filter_spectrum(p, q, s, n1):
    return (p[:n1] + q[:n1]) * s, (p[n1:] - q[n1:]) * s


def _stage_b_conv_kernel(re_ref, im_ref, p_ref, q_ref, s_ref, f_ref, ft_ref, ore_ref, oim_ref):
    n1 = re_ref.shape[0]
    s = s_ref[...]
    ff, fft_ = f_ref[...], ft_ref[...]
    bf = lambda t: t.astype(jnp.bfloat16)
    for j in range(FFT_NB):
        x = _mm1(ff, bf(jnp.concatenate([re_ref[:, j, :], im_ref[:, j, :]], axis=0)))
        hr, hi = _filter_spectrum(p_ref[j], q_ref[j], s, n1)
        xr, xi = x[:n1], x[n1:]
        z = _mm1(fft_, bf(jnp.concatenate([xr * hr - xi * hi, xr * hi + xi * hr], axis=0)))
        ore_ref[:, j, :] = z[:n1]
        oim_ref[:, j, :] = z[n1:]


def fft_stage_b_conv(t1, spec, col_p, col_q, inv_norm, ff, fft_):
    n1, n2x2, C = t1.shape
    n2 = n2x2 // 2
    ncb = C // FFT_CB
    blk = lambda off: pl.BlockSpec((n1, FFT_NB, FFT_CB), lambda c, i: (0, i + off, c))
    mat = pl.BlockSpec((2 * n1, 2 * n1), lambda c, i: (0, 0))
    sp = lambda col: pl.BlockSpec((FFT_NB, 2 * n1, FFT_CB), lambda c, i: (i, 0, col * ncb + c))
    return pl.pallas_call(
        _stage_b_conv_kernel,
        grid=(ncb, n2 // FFT_NB),
        in_specs=[blk(0), blk(n2 // FFT_NB), sp(col_p), sp(col_q), pl.BlockSpec((1, FFT_CB), lambda c, i: (0, c)),
                  mat, mat],
        out_specs=[blk(0), blk(0)],
        out_shape=[jax.ShapeDtypeStruct((n1, n2, C), jnp.float32)] * 2,
        compiler_params=_params(2),
    )(t1, t1, spec, spec, inv_norm, ff, fft_)


def _stage_a_inv_kernel(re_ref, im_ref, g_ref, u_ref, gate_ref, bias_ref, o_ref):
    for j in range(FFT_NB):
        y = _mm1(g_ref[j], jnp.concatenate([re_ref[j], im_ref[j]], axis=0).astype(jnp.bfloat16))
        o_ref[:, j, :] = gate_ref[:, j, :] * (y + u_ref[:, j, :] * bias_ref[...])


def fft_stage_a_inv(t2re, t2im, ginv, nb, u3, u_col, gate3, gate_col, bias):
    n1, nk, C = t2re.shape
    tb = pl.BlockSpec((FFT_NB, nk, CB), lambda c, i: (i, 0, c))
    gb = pl.BlockSpec((FFT_NB, nb, 2 * nk), lambda c, i: (i, 0, 0))
    ub = lambda col: pl.BlockSpec((nb, FFT_NB, CB), lambda c, i: (0, i, col + c))
    return pl.pallas_call(
        _stage_a_inv_kernel,
        grid=(C // CB, n1 // FFT_NB),
        in_specs=[tb, tb, gb, ub(u_col), ub(gate_col), pl.BlockSpec((1, CB), lambda c, i: (0, c))],
        out_specs=ub(0),
        out_shape=jax.ShapeDtypeStruct((nb, n1, C), jnp.float32),
        compiler_params=_params(2),
    )(t2re, t2im, ginv, u3, gate3, bias)


def hyena_long_conv_chain(u, L, biases, filt, colsum, tabs):
    C = GROUP_W
    n1 = FFT_N1
    n2 = 2 * L // n1
    nb = n2 // 2
    spec = fft_stage_b_filter(fft_stage_a(filt.reshape(nb, n1, filt.shape[1]), nb, 0, filt.shape[1] // CB, tabs["g"]),
                              tabs["ff"])
    s4 = colsum.reshape(HY_ORDER, 2, C)
    inv_norm = 1.0 / (s4[:, 0] + s4[:, 1])
    u3 = u.reshape(u.shape[0] // n1, n1, u.shape[1])
    y3, y_col = u3, 0
    for n in range(HY_ORDER):
        t1 = fft_stage_a(y3, nb, y_col, 1, tabs["g"])
        t2re, t2im = fft_stage_b_conv(t1, spec, 2 * n, 2 * n + 1, inv_norm[n][None, :], tabs["ff"], tabs["fft"])
        y3 = fft_stage_a_inv(t2re, t2im, tabs["ginv"], nb, y3, y_col, u3, n + 1, biases[n][None, :])
        y_col = 0
    return y3.reshape(L, C)


def dense_dft_tables(n):
    N = 2 * n
    k = jnp.arange(N, dtype=jnp.float32)[:, None]
    t = jnp.arange(n, dtype=jnp.float32)[None, :]
    ph = jnp.mod(k * t, float(N)) * (2.0 * math.pi / N)
    fd = jnp.concatenate([jnp.cos(ph), -jnp.sin(ph)], axis=0)
    return fd.astype(jnp.bfloat16), (fd.T / N).astype(jnp.bfloat16)


def _hyena_ctx_kernel(u_ref, h_ref, s_ref, bias_ref, fd_ref, ft_ref, o_ref):
    C = GROUP_W
    fd, ft = fd_ref[...], ft_ref[...]
    K = fd.shape[0] // 2
    bf = lambda t: t.astype(jnp.bfloat16)
    hs = _mm1(fd, bf(h_ref[...]))
    s = s_ref[...]
    y = u_ref[:, 0:C]
    for n in range(HY_ORDER):
        cp, cq = 2 * n * C, (2 * n + 1) * C
        inv = 1.0 / (s[:, cp:cp + C] + s[:, cq:cq + C])
        hr, hi = _filter_spectrum(hs[:, cp:cp + C], hs[:, cq:cq + C], inv, K)
        x = _mm1(fd, bf(y))
        xr, xi = x[:K], x[K:]
        conv = _mm1(ft, bf(jnp.concatenate([xr * hr - xi * hi, xr * hi + xi * hr], axis=0)))
        y = u_ref[:, (n + 1) * C:(n + 2) * C] * (conv + y * bias_ref[n:n + 1, :])
    o_ref[...] = y


def hyena_ctx(u, filt, colsum, biases, tabs):
    args = (u, filt, colsum, biases, tabs[0], tabs[1])
    return pl.pallas_call(
        _hyena_ctx_kernel,
        in_specs=[_full(a) for a in args],
        out_specs=pl.BlockSpec((u.shape[0], GROUP_W), lambda: (0, 0)),
        out_shape=jax.ShapeDtypeStruct((u.shape[0], GROUP_W), jnp.float32),
        compiler_params=pltpu.CompilerParams(vmem_limit_bytes=VMEM_LIMIT),
    )(*args)


SCAN_T = 64
GROUP_LANES = 256
RW_SUB = 16
HG_SUB = 16
LOG2_E = 1.4426950408889634


def _chunk_index(d, i, n_ctx, n_all):
    fwd = jnp.where(i < n_ctx, n_all - n_ctx + i, i - n_ctx)
    return jnp.where(d == 0, fwd, n_all - 1 - i)


def _stacking(T, S, nh, head, sign):
    G = nh * head
    nb = T // S
    n = nh * T
    rr = lax.broadcasted_iota(jnp.int32, (n, G), 0)
    same = ((rr // S) % nh) == (lax.broadcasted_iota(jnp.int32, (n, G), 1) // head)

    def bd(x):
        pieces = []
        for i in range(nb):
            pieces += [x[i * S:(i + 1) * S]] * nh
        return jnp.where(same, jnp.concatenate(pieces, axis=0), 0.0)

    def collapse(o):
        outs = []
        for i in range(nb):
            acc = o[i * nh * S:i * nh * S + S]
            for h in range(1, nh):
                acc = acc + o[i * nh * S + h * S:i * nh * S + (h + 1) * S]
            outs.append(acc)
        return jnp.concatenate(outs, axis=0)

    rt = lax.broadcasted_iota(jnp.int32, (n, n), 0)
    cs = lax.broadcasted_iota(jnp.int32, (n, n), 1)
    t_r = (rt // (nh * S)) * S + rt % S
    t_c = (cs // (nh * S)) * S + cs % S
    same_h = ((rt // S) % nh) == ((cs // S) % nh)
    before = same_h & ((t_r - t_c) * sign > 0)
    return bd, collapse, before, rt, cs


def _softplus(x):
    return jnp.maximum(x, 0.0) + jnp.log(1.0 + jnp.exp(-jnp.abs(x)))


def _rwkv_prep_kernel(*refs, n_lat, n_rows):
    zrefs, rest = refs[:15], refs[15:]
    (mu_ref, kk_w_ref, ka_ref, w0_ref, a0_ref, w2h_ref, w2l_ref, a2h_ref, a2l_ref, ones_ref,
     r_ref, k_ref, v_ref, g_ref, kk_ref, lw_ref, kd_ref, a_ref) = rest
    slabs = []
    for c in range(5):
        x = zrefs[3 * c][...]
        xp, xn = _neighbours(x, zrefs[3 * c + 1][...], zrefs[3 * c + 2][...], n_lat, n_rows)
        slabs.append(x + (0.5 * (xp + xn) - x) * mu_ref[:, c * CB:(c + 1) * CB])
    r, k, v, g, lora = slabs
    r_ref[...], k_ref[...], v_ref[...], g_ref[...] = r, k, v, g
    kk = k * kk_w_ref[...]
    ss = _mm2(kk * kk, ones_ref[...])
    kk_ref[...] = kk * lax.rsqrt(jnp.maximum(ss, 1e-24))
    lora_t = _split(jnp.tanh(lora))
    lora_s = _split(lora)
    for d in range(2):
        w_log = -_softplus(-(w0_ref[d:d + 1, :] + _mm3(lora_t, (w2h_ref[d], w2l_ref[d])))) - 0.5
        lw_ref[d] = -jnp.exp(w_log)
        a = _sigmoid(a0_ref[d:d + 1, :] + _mm3(lora_s, (a2h_ref[d], a2l_ref[d])))
        a_ref[d] = a
        kd_ref[d] = k * (1.0 + (a - 1.0) * ka_ref[...])


def rwkv_prep(z, mu, k_k, k_a, w0, a0, w2, a2, n_lat):
    n_rows = z.shape[0]
    tp = PREP_ROWS
    mu_p = jnp.pad(mu, (0, 5 * CB - RW_REAL))[None, :]
    w2p = jnp.zeros((2, CB, GROUP_W), jnp.float32)
    a2p = jnp.zeros((2, CB, GROUP_W), jnp.float32)
    for d in range(2):
        w2p = w2p.at[d, d * RW_LORA:(d + 1) * RW_LORA].set(w2[d])
        a2p = a2p.at[d, (2 + d) * RW_LORA:(3 + d) * RW_LORA].set(a2[d])
    w2s, a2s = _split(w2p), _split(a2p)
    small = (mu_p, k_k[None, :], k_a[None, :], w0, a0, w2s[0], w2s[1], a2s[0], a2s[1],
             _block_indicator(GROUP_W, RW_HEAD, 1.0))
    zspecs = []
    for c in range(5):
        zspecs += _halo_specs(tp, n_rows, COL_RW + c)
    one = pl.BlockSpec((tp, CB), lambda i: (i, 0))
    two = pl.BlockSpec((2, tp, CB), lambda i: (0, i, 0))
    s1 = jax.ShapeDtypeStruct((n_rows, GROUP_W), jnp.float32)
    s2 = jax.ShapeDtypeStruct((2, n_rows, GROUP_W), jnp.float32)
    return pl.pallas_call(
        functools.partial(_rwkv_prep_kernel, n_lat=n_lat, n_rows=n_rows),
        grid=(n_rows // tp,),
        in_specs=zspecs + [_full(a) for a in small],
        out_specs=[one] * 5 + [two] * 3,
        out_shape=[s1] * 5 + [s2] * 3,
        compiler_params=_params(1),
    )(*([z] * 15), *small)


def _rwkv_scan_body(refs, head):
    in_f, in_b, (of_ref, ob_ref, ht_ref) = refs[:6], refs[6:12], refs[12:]
    T, G, S = SCAN_T, GROUP_LANES, RW_SUB
    nh = G // head
    nb = T // S
    n = nh * T

    @pl.when(pl.program_id(0) == 0)
    def _():
        ht_ref[...] = jnp.zeros_like(ht_ref)

    ti = lax.broadcasted_iota(jnp.int32, (T, T), 0)
    si = lax.broadcasted_iota(jnp.int32, (T, T), 1)
    bf = lambda t: t.astype(jnp.bfloat16)
    each = lambda f, *seqs: [f(*xs) for xs in zip(*seqs)]
    n_grp = in_f[0].shape[1] // G
    lanes = [slice(grp * G, (grp + 1) * G) for grp in range(n_grp)]

    r, v, kk, lw, k, a, tri, before, incl, dest = ([] for _ in range(10))
    for (r_ref, v_ref, kk_ref, lw_ref, k_ref, a_ref), o_ref, sign in ((in_f, of_ref, 1), (in_b, ob_ref, -1)):
        bd, collapse, before_d, rt, cs = _stacking(T, S, nh, head, sign)
        for ls in lanes:
            r.append(r_ref[:, ls]), v.append(v_ref[:, ls]), kk.append(kk_ref[:, ls])
            lw.append(lw_ref[0, :, ls]), k.append(k_ref[0, :, ls]), a.append(a_ref[0, :, ls])
            tri.append(jnp.where((ti - si) * sign >= 0, 1.0, 0.0))
            before.append(before_d), incl.append(before_d | (rt == cs)), dest.append((o_ref, ls))
    diag_blk = (rt // (nh * S)) == (cs // (nh * S))
    eye = jnp.where(rt == cs, 1.0, 0.0)
    ht = [ht_ref[ch] for ch in range(len(r))]

    c = each(lambda tr, t: _dot(tr, t), tri, lw)
    ctot = each(lambda t: jnp.sum(t, axis=0, keepdims=True), lw)
    beta = each(lambda p, q: p * q, kk, a)
    einv = each(lambda t: jnp.exp(-t), c)
    lhs = each(lambda kk_, r_, c_, lw_: bf(jnp.concatenate([bd(-kk_ * jnp.exp(c_ - lw_)), bd(r_ * jnp.exp(c_))], axis=0)),
               kk, r, c, lw)
    rhs = each(lambda k_, b_, e_: bf(jnp.concatenate([bd(k_ * e_), bd(b_ * e_)], axis=0)), k, beta, einv)
    m = each(lambda p, q: _mm1(p, q, NT), lhs, rhs)
    g = each(lambda p, h: _mm1(p, bf(h), NT), lhs, ht)
    v_b = each(lambda t: bf(bd(t)), v)
    x = each(lambda g_, m_, v_, bm: g_[:n] + _mm1(bf(jnp.where(bm, m_[:n, :n], 0.0)), v_), g, m, v_b, before)
    lb = each(lambda m_, bm: jnp.where(bm, m_[:n, n:], 0.0), m, before)
    ld = each(lambda t: jnp.where(diag_blk, t, 0.0), lb)
    lo = each(lambda p, q: p - q, lb, ld)
    xd = each(lambda t: eye + t, ld)
    lp = each(bf, ld)
    p = 2
    while p < S:
        lp = each(lambda t: bf(_mm1(t, t)), lp)
        xd = each(lambda x_, l_: x_ + _mm1(l_, bf(x_)), xd, lp)
        p *= 2
    xd_b = each(bf, xd)
    n_b = each(lambda x_, l_: bf(_mm1(x_, bf(l_))), xd_b, lo)
    y = each(lambda x_, t: _mm1(x_, bf(t)), xd_b, x)
    y = each(lambda y_, n_: y_ + _mm1(n_, bf(y_)), y, n_b)
    p = 2
    while p < nb:
        n_b = each(lambda t: bf(_mm1(t, t)), n_b)
        y = each(lambda y_, n_: y_ + _mm1(n_, bf(y_)), y, n_b)
        p *= 2
    u_b = each(bf, y)
    o = each(lambda g_, m_, v_, u_, im: (g_[n:] + _mm1(bf(jnp.where(im, m_[n:, :n], 0.0)), v_))
             + _mm1(bf(jnp.where(im, m_[n:, n:], 0.0)), u_), g, m, v_b, u_b, incl)
    efin = each(lambda ct, c_: jnp.exp(ct - c_), ctot, c)
    ht_new = each(lambda h, ct, v_, u_, k_, b_, e_: (h * jnp.exp(ct) + _mm1(v_, bf(bd(k_ * e_)), TN))
                  + _mm1(u_, bf(bd(b_ * e_)), TN), ht, ctot, v_b, u_b, k, beta, efin)
    for ch, (o_ref, ls) in enumerate(dest):
        o_ref[:, ls] = collapse(o[ch])
        ht_ref[ch] = ht_new[ch]


def rope_tables(n_ctx, L):
    d_axis = RT_HEAD // 2
    half = d_axis // 2
    inv = ROPE_BASE ** (-jnp.arange(0, d_axis, 2, dtype=jnp.float32) / d_axis)
    t = jnp.arange(L)
    pos = jnp.stack([(t // GRID_W).astype(jnp.float32), (t % GRID_W).astype(jnp.float32)], axis=1)
    j = jnp.arange(RT_HEAD)
    ang = pos[:, j // d_axis] * inv[j % half][None, :]
    sgn = jnp.where((j % d_axis) < half, -1.0, 1.0)[None, :]
    cos = jnp.concatenate([jnp.cos(ang), jnp.ones((n_ctx, RT_HEAD), jnp.float32)], axis=0)
    sin = jnp.concatenate([jnp.sin(ang) * sgn, jnp.zeros((n_ctx, RT_HEAD), jnp.float32)], axis=0)
    return jnp.tile(cos, (1, RT_HEADS)), jnp.tile(sin, (1, RT_HEADS))


def _rotate(x, cos, sin):
    G = x.shape[1]
    half = RT_HEAD // 4
    lane = lax.broadcasted_iota(jnp.int32, x.shape, 1)
    partner = jnp.where((lane % (2 * half)) < half, pltpu.roll(x, G - half, axis=1), pltpu.roll(x, half, axis=1))
    return x * cos + partner * sin


def _retention_scan_body(q_ref, k_ref, v_ref, cos_ref, sin_ref, lg_ref, o_ref, st_ref, *, head, d):
    T, G = SCAN_T, GROUP_LANES
    nh = G // head
    sign = 1 - 2 * d

    t = lax.broadcasted_iota(jnp.int32, (T, 1), 0)
    pos = (t + d * (T - 1 - 2 * t) + 1).astype(jnp.float32)
    bd, collapse, before, rt, cs = _stacking(T, T, nh, head, sign)
    incl = before | (rt == cs)
    bf = lambda x: x.astype(jnp.bfloat16)

    groups = range(q_ref.shape[1] // G)
    lanes = [slice(grp * G, (grp + 1) * G) for grp in groups]
    each = lambda f, *seqs: [f(*xs) for xs in zip(*seqs)]
    st = [st_ref[grp] for grp in groups]
    cos, sin, lg = ([ref[:, ls] for ls in lanes] for ref in (cos_ref, sin_ref, lg_ref))
    q = each(_rotate, [q_ref[:, ls] for ls in lanes], cos, sin)
    k = each(lambda x, c_, s_: _rotate(x, c_, s_) * (head ** -0.5), [k_ref[:, ls] for ls in lanes], cos, sin)
    c = each(lambda lg_: pos * lg_, lg)
    q_b = each(lambda q_, c_: bf(bd(q_ * jnp.exp(c_))), q, c)
    kt_b = each(lambda k_, c_: bf(bd(k_ * jnp.exp(-c_))), k, c)
    v_b = [bf(bd(v_ref[:, ls])) for ls in lanes]
    scores = each(lambda q_, k_: bf(jnp.where(incl, _mm1(q_, k_, NT), 0.0)), q_b, kt_b)
    inter = each(lambda q_, s_: _mm1(q_, bf(s_), NT), q_b, st)
    o = each(lambda g_, s_, v_: g_ + _mm1(s_, v_), inter, scores, v_b)
    st_new = each(lambda s_, lg_, v_, k_, c_: s_ * jnp.exp(float(T) * lg_)
                  + _mm1(v_, bf(bd(k_ * jnp.exp(float(T) * lg_ - c_))), TN), st, lg, v_b, k, c)
    for grp in groups:
        o_ref[:, lanes[grp]] = collapse(o[grp])
        st_ref[grp] = st_new[grp]


def _gla_scan_body(q_ref, f_ref, i_ref, lb_ref, o_ref, st_ref, *, head, d):
    T, S = SCAN_T, HG_SUB
    W = q_ref.shape[-1]
    nh = W // head
    sign = 1 - 2 * d

    ti = lax.broadcasted_iota(jnp.int32, (S, S), 0)
    si = lax.broadcasted_iota(jnp.int32, (S, S), 1)
    tri_incl = jnp.where((ti - si) * sign >= 0, 1.0, 0.0)
    row = lax.broadcasted_iota(jnp.int32, (S, 1), 0)
    lb = lb_ref[0]

    states = [st_ref[h] for h in range(nh)]
    for j in range(T // S):
        jb = j + d * (T // S - 1 - 2 * j)
        rows = pl.ds(jb * S, S)
        q = q_ref[rows, :]
        q = q * _sigmoid(q)
        v = i_ref[rows, :]
        gate = lb + (1.0 - lb) * _sigmoid(f_ref[rows, :])
        lf = jnp.log(jnp.maximum(gate, HG_MIN_GATE))
        k = 1.0 - gate
        b = _dot(tri_incl, lf)
        btot = jnp.sum(lf, axis=0, keepdims=True)
        qe = q * jnp.exp(b)
        ke = k * jnp.exp(btot - b)
        outs = []
        for h in range(nh):
            ls = slice(h * head, (h + 1) * head)
            qh, kh, vh = q[:, ls], k[:, ls], v[:, ls]
            bh = b[:, ls] * LOG2_E
            st = states[h]
            o = _mm1(qe[:, ls].astype(jnp.bfloat16), st.astype(jnp.bfloat16), NT)
            for s in range(S):
                e = jnp.exp2(bh - bh[s:s + 1, :])
                a_s = jnp.sum(qh * kh[s:s + 1, :] * e, axis=-1, keepdims=True)
                a_s = jnp.where((row - s) * sign >= 0, a_s, 0.0)
                o = o + a_s * vh[s:s + 1, :]
            outs.append(o)
            states[h] = st * jnp.exp(btot[:, ls]) + _mm1(vh.astype(jnp.bfloat16), ke[:, ls].astype(jnp.bfloat16), TN)
        o_ref[rows, :] = jnp.concatenate(outs, axis=1)
    for h in range(nh):
        st_ref[h] = states[h]


N_RW, N_RT, N_HG = 6, 6, 4


def _scans_kernel(*refs):
    n_in = 2 * (N_RW + N_RT + N_HG)
    ins, (rw_of, rw_ob, rt_of, rt_ob, hg_of, hg_ob, rw_st, rt_st, hg_st) = refs[:n_in], refs[n_in:]
    rw_in, rt_in, hg_in = ins[:2 * N_RW], ins[2 * N_RW:2 * (N_RW + N_RT)], ins[2 * (N_RW + N_RT):]

    @pl.when(pl.program_id(0) == 0)
    def _():
        rt_st[...] = jnp.zeros_like(rt_st)
        hg_st[...] = jnp.zeros_like(hg_st)

    n_rt, n_hg = rt_st.shape[0] // 2, hg_st.shape[0] // 2
    _rwkv_scan_body(list(rw_in) + [rw_of, rw_ob, rw_st], RW_HEAD)
    for d, (rt_o, hg_o) in enumerate(((rt_of, hg_of), (rt_ob, hg_ob))):
        _gla_scan_body(*hg_in[d * N_HG:(d + 1) * N_HG], hg_o, hg_st.at[d * n_hg:(d + 1) * n_hg], head=HG_HEAD, d=d)
        _retention_scan_body(*rt_in[d * N_RT:(d + 1) * N_RT], rt_o, rt_st.at[d * n_rt:(d + 1) * n_rt],
                             head=RT_HEAD, d=d)


def recurrent_scans(z, rw, cos, sin, lg, lb2, n_ctx_rows):
    N = z.shape[0]
    W = GROUP_W
    T = SCAN_T
    n_all, n_ctx = N // T, n_ctx_rows // T
    chunk = lambda d: (lambda i: (_chunk_index(d, i, n_ctx, n_all), 0))
    row = lambda d: pl.BlockSpec((T, W), chunk(d))
    per_dir = lambda d: pl.BlockSpec((1, T, W), lambda i: (d, _chunk_index(d, i, n_ctx, n_all), 0))
    zc = lambda d, col: pl.BlockSpec((T, CB), lambda i: (_chunk_index(d, i, n_ctx, n_all), col))
    rw_specs = lambda d: [row(d)] * 3 + [per_dir(d)] * 3
    rt_specs = lambda d: [zc(d, COL_RT), zc(d, COL_RT + 1), zc(d, COL_RT + 2), row(d), row(d),
                          pl.BlockSpec((1, W), lambda i: (0, 0))]
    hg_specs = lambda d: [zc(d, COL_HG), zc(d, COL_HG + 1 + d), zc(d, COL_HG + 3),
                          pl.BlockSpec((1, 1, W), lambda i: (d, 0, 0))]
    shape = jax.ShapeDtypeStruct((N, W), jnp.float32)
    return pl.pallas_call(
        _scans_kernel,
        grid=(n_all,),
        in_specs=rw_specs(0) + rw_specs(1) + rt_specs(0) + rt_specs(1) + hg_specs(0) + hg_specs(1),
        out_specs=[row(0), row(1)] * 3,
        out_shape=[shape] * 6,
        scratch_shapes=[pltpu.VMEM((2 * (W // GROUP_LANES), GROUP_LANES, GROUP_LANES), jnp.float32),
                        pltpu.VMEM((2 * (W // GROUP_LANES), GROUP_LANES, GROUP_LANES), jnp.float32),
                        pltpu.VMEM((2 * (W // HG_HEAD), HG_HEAD, HG_HEAD), jnp.float32)],
        compiler_params=_params(1),
    )(*(list(rw) * 2), *([z, z, z, cos, sin, lg] * 2), *([z, z, z, lb2] * 2))


def _outproj_kernel(hy_ref, rwof_ref, rwob_ref, r_ref, k_ref, v_ref, rwg_ref, rtof_ref, rtob_ref, rtg_ref,
                    hgof_ref, hgob_ref, hgg_ref,
                    rk_ref, gng_ref, gnb_ref, hgn_ref, avg64_ref, avg128_ref,
                    w_ref, x_ref, gate_ref, g_ref, b_ref, o_ref, *, n_lat):
    avg64, avg128 = avg64_ref[...], avg128_ref[...]

    def head_norm(o, avg, eps, centre):
        if centre:
            o = o - _mm2(o, avg)
        return o * lax.rsqrt(_mm2(o * o, avg) + eps)

    silu = lambda t: t * _sigmoid(t)
    y_rw = head_norm(rwof_ref[...] + rwob_ref[...], avg64, RW_GN_EPS, True) * gng_ref[...] + gnb_ref[...]
    bonus = (float(RW_HEAD) * _mm2(r_ref[...] * k_ref[...] * rk_ref[...], avg64)) * v_ref[...]
    y_rw = (y_rw + bonus) * _sigmoid(rwg_ref[...])
    y_rt = head_norm(rtof_ref[...] + rtob_ref[...], avg64, HEAD_NORM_EPS, True) * silu(rtg_ref[...])
    y_hg = head_norm(hgof_ref[...] + hgob_ref[...], avg128, HEAD_NORM_EPS, False) * hgn_ref[...] * silu(hgg_ref[...])
    y = None
    for m, ym in enumerate((hy_ref[...], y_rw, y_rt, y_hg)):
        part = jnp.dot(ym.astype(jnp.bfloat16), w_ref[m * GROUP_W:(m + 1) * GROUP_W, :],
                       preferred_element_type=jnp.float32)
        y = part if y is None else y + part
    is_ctx = _ctx_rows(x_ref.shape[0], n_lat)
    r = ALPHA * x_ref[...] + _mod_row(gate_ref, is_ctx) * y
    o_ref[...] = _ln_rows(r) * g_ref[...] + b_ref[...]


def outproj_deepnorm(y_hy, scans, rw_r, rw_k, rw_v, rw_g, z, r_k, gn_g, gn_b, hg_norm_g,
                     w_bf16, x, mod, g, b, n_lat, m):
    tm = OUT_TM
    one = pl.BlockSpec((tm, GROUP_W), lambda i: (i, 0))
    zc = lambda col: pl.BlockSpec((tm, CB), lambda i: (i, col))
    row = pl.BlockSpec((tm, D_MODEL), lambda i: (i, 0))
    small = (r_k, gn_g, gn_b, hg_norm_g, _block_indicator(GROUP_W, RW_HEAD, 1.0 / RW_HEAD),
             _block_indicator(GROUP_W, HG_HEAD, 1.0 / HG_HEAD))
    return pl.pallas_call(
        functools.partial(_outproj_kernel, n_lat=n_lat),
        grid=(m // tm,),
        in_specs=[one] * 9 + [zc(COL_RT + 3), one, one, zc(COL_HG + 4)]
        + [_full(a) for a in small]
        + [_full(w_bf16), row, pl.BlockSpec((8, D_MODEL), lambda i: (0, 2)), _full(g), _full(b)],
        out_specs=row,
        out_shape=jax.ShapeDtypeStruct((m, D_MODEL), jnp.float32),
        compiler_params=_params(1),
    )(y_hy, scans[0], scans[1], rw_r, rw_k, rw_v, rw_g, scans[2], scans[3], z, scans[4], scans[5], z,
      *small, w_bf16, x, mod, g, b)


def _ffn_kernel(x_ref, sh_ref, sc_ref, w1_ref, w3_ref, w2_ref, gate_ref, g_ref, b_ref, o_ref, h_ref, acc_ref, *, n_lat):
    j = pl.program_id(1)

    @pl.when(j == 0)
    def _():
        is_ctx = _ctx_rows(x_ref.shape[0], n_lat)
        h = _ln_rows(x_ref[...]) * (1.0 + _mod_row(sc_ref, is_ctx)) + _mod_row(sh_ref, is_ctx)
        h_ref[...] = h.astype(jnp.bfloat16)
        acc_ref[...] = jnp.zeros_like(acc_ref)

    h = h_ref[...]
    a = jnp.dot(h, w1_ref[...], preferred_element_type=jnp.float32)
    u = jnp.dot(h, w3_ref[...], preferred_element_type=jnp.float32)
    s = (a * _sigmoid(a) * u).astype(jnp.bfloat16)
    acc_ref[...] += jnp.dot(s, w2_ref[...], preferred_element_type=jnp.float32)

    @pl.when(j == pl.num_programs(1) - 1)
    def _():
        is_ctx = _ctx_rows(x_ref.shape[0], n_lat)
        r = ALPHA * x_ref[...] + _mod_row(gate_ref, is_ctx) * acc_ref[...]
        o_ref[...] = _ln_rows(r) * g_ref[...] + b_ref[...]


def ffn_deepnorm(x, mod, w1, w3, w2, g, b, n_lat, tm):
    m = x.shape[0]
    row = pl.BlockSpec((tm, D_MODEL), lambda i, j: (i, 0))
    modc = lambda c: pl.BlockSpec((8, D_MODEL), lambda i, j: (0, c))
    vec = pl.BlockSpec((1, D_MODEL), lambda i, j: (0, 0))
    return pl.pallas_call(
        functools.partial(_ffn_kernel, n_lat=n_lat),
        grid=(m // tm, FFN_HIDDEN // FFN_TF),
        in_specs=[row, modc(3), modc(4),
                  pl.BlockSpec((D_MODEL, FFN_TF), lambda i, j: (0, j)),
                  pl.BlockSpec((D_MODEL, FFN_TF), lambda i, j: (0, j)),
                  pl.BlockSpec((FFN_TF, D_MODEL), lambda i, j: (j, 0)),
                  modc(5), vec, vec],
        out_specs=row,
        out_shape=jax.ShapeDtypeStruct((m, D_MODEL), jnp.float32),
        scratch_shapes=[pltpu.VMEM((tm, D_MODEL), jnp.bfloat16),
                        pltpu.VMEM((tm, D_MODEL), jnp.float32)],
        compiler_params=_params(2),
    )(x, mod, mod, w1, w3, w2, mod, g, b)


def kernel(x, c, ctx, c_ctx, ada_w, ada_b, w_in, w_out, ln_g, ln_b, hy_conv_w, hy_conv_b, hy_w1, hy_b1, hy_w2, hy_b2, hy_w3, hy_b3, hy_freq, hy_bias, rw_mu, rw_w0, rw_w2, rw_a0, rw_a2, rw_k_k, rw_k_a, rw_r_k, rw_gn_g, rw_gn_b, hg_lb_raw, hg_norm_g, ffn_w1, ffn_w3, ffn_w2):
    L, n_ctx = x.shape[1], ctx.shape[1]
    sm = jax.nn.softmax(hg_lb_raw.astype(jnp.float32), axis=1)
    lower_bounds = jnp.cumsum(sm, axis=1) - sm[:, :1]
    log_gamma = jnp.log1p(-jnp.exp2(-5.0 - jnp.arange(RT_HEADS, dtype=jnp.float32)))
    lg = jnp.repeat(log_gamma, RT_HEAD)[None, :]
    cos, sin = rope_tables(n_ctx, L)
    fft_tabs = fft_tables(FFT_N1, 2 * L // FFT_N1)
    ctx_tabs = dense_dft_tables(n_ctx)

    c8 = jnp.zeros((8, D_MODEL), jnp.float32).at[0].set(c[0]).at[1].set(c_ctx)
    xs = jnp.concatenate([x[0], ctx[0]], axis=0)
    n_rows = L + n_ctx
    for l in range(DEPTH):
        with_ctx = l < DEPTH - 1
        mod = ada_modulation(c8, ada_w, l, ada_b[l][None, :])
        w_out_b = w_out[l].astype(jnp.bfloat16)
        w1_b, w3_b, w2_b = (w[l].astype(jnp.bfloat16) for w in (ffn_w1, ffn_w3, ffn_w2))

        z = modulated_projection(xs, mod, w_in, l, L)

        u = hyena_conv3(z, hy_conv_w[l], hy_conv_b[l][None, :], L)
        hy_w = (hy_w1[l], hy_b1[l], hy_w2[l], hy_b2[l], hy_w3[l], hy_b3[l], hy_freq[l])
        filt, colsum = hyena_filter_bank(L, *hy_w)
        y_lat = hyena_long_conv_chain(u, L, hy_bias[l], filt, colsum, fft_tabs)
        if with_ctx:
            filt_c, colsum_c = hyena_filter_bank(n_ctx, *hy_w)
            y_ctx = hyena_ctx(u[L:], filt_c, colsum_c, hy_bias[l], ctx_tabs)
        else:
            y_ctx = jnp.zeros((n_ctx, GROUP_W), jnp.float32)
        y_hy = jnp.concatenate([y_lat, y_ctx], axis=0)

        rw_r, rw_k, rw_v, rw_g, rw_kk, rw_lw, rw_kd, rw_a = rwkv_prep(
            z, rw_mu[l], rw_k_k[l], rw_k_a[l], rw_w0[l], rw_a0[l], rw_w2[l], rw_a2[l], L)
        scans = recurrent_scans(z, (rw_r, rw_v, rw_kk, rw_lw, rw_kd, rw_a), cos, sin, lg,
                                lower_bounds[:, l][:, None, :], n_ctx)

        xs = outproj_deepnorm(y_hy, scans, rw_r, rw_k, rw_v, rw_g, z,
                              rw_r_k[l].reshape(1, GROUP_W), rw_gn_g[l][None, :], rw_gn_b[l][None, :],
                              hg_norm_g[l][None, :], w_out_b, xs, mod, ln_g[l, 0][None, :], ln_b[l, 0][None, :],
                              L, n_rows if with_ctx else L)
        xs = ffn_deepnorm(xs, mod, w1_b, w3_b, w2_b, ln_g[l, 1][None, :], ln_b[l, 1][None, :], L,
                          FFN_TM if with_ctx else FFN_LAST_TM)
    return xs[None]
```

```python
import functools
import math

import jax
import jax.numpy as jnp
from jax import lax
from jax.experimental import pallas as pl
from jax.experimental.pallas import tpu as pltpu

D_MODEL = 2048
DEPTH = 2
GRID_W = 64
N_MIXERS = 4
GROUP_W = D_MODEL // N_MIXERS
HY_ORDER = 2
HY_EMB = 33
HY_FAST_DECAY = 0.3
HY_SLOW_DECAY = 1.5
HY_TARGET = 1e-2
RW_HEAD = 64
RW_LORA = 96
RW_GN_EPS = 64e-5
RT_HEAD = 64
RT_HEADS = GROUP_W // RT_HEAD
ROPE_BASE = 10000.0
HG_HEAD = 128
HG_MIN_GATE = 1e-30
FFN_HIDDEN = 5632
ALPHA = (2 * DEPTH) ** 0.25
LN_EPS = 1e-6
HEAD_NORM_EPS = 1e-6

LANES = 128
CB = 512
COL_HY = 0
COL_RW = 3
COL_RT = 8
COL_HG = 12
P_IN_PAD = 17 * CB
RW_REAL = 4 * GROUP_W + 4 * RW_LORA
PROJ_TM = 1408
FFN_TM = 768
FFN_LAST_TM = 512
OUT_TM = 256
FFN_TF = 512
PREP_ROWS = 256
HALO = 8
VMEM_LIMIT = 56 * 1024 * 1024

HI = lax.Precision.HIGHEST
NN = ((1,), (0,))
NT = ((1,), (1,))
TN = ((0,), (0,))


def _params(n_axes):
    return pltpu.CompilerParams(dimension_semantics=("arbitrary",) * n_axes, vmem_limit_bytes=VMEM_LIMIT)


def _full(a):
    return pl.BlockSpec(a.shape, lambda *_: (0,) * a.ndim)


def _dot(a, b):
    return jnp.dot(a, b, precision=HI, preferred_element_type=jnp.float32)


def _split(x):
    hi = x.astype(jnp.bfloat16)
    lo = (x - hi.astype(jnp.float32)).astype(jnp.bfloat16)
    return hi, lo


def _mm3(a, b, dims=NN):
    d = lambda p, q: lax.dot_general(p, q, (dims, ((), ())), preferred_element_type=jnp.float32)
    return d(a[0], b[0]) + (d(a[0], b[1]) + d(a[1], b[0]))


def _mm1(a, b, dims=NN):
    return lax.dot_general(a, b, (dims, ((), ())), preferred_element_type=jnp.float32)


def _mm2(x, b):
    hi, lo = _split(x)
    d = functools.partial(jnp.dot, preferred_element_type=jnp.float32)
    return d(hi, b) + d(lo, b)


def _ln_rows(x):
    mu = jnp.mean(x, axis=-1, keepdims=True)
    xc = x - mu
    var = jnp.mean(xc * xc, axis=-1, keepdims=True)
    return xc * lax.rsqrt(var + LN_EPS)


def _sigmoid(x):
    return 1.0 / (1.0 + jnp.exp(-x))


def _ctx_rows(tm, n_lat):
    return (pl.program_id(0) * tm + lax.broadcasted_iota(jnp.int32, (tm, 1), 0)) >= n_lat


def _mod_row(ref, is_ctx):
    return jnp.where(is_ctx, ref[1:2, :], ref[0:1, :])


def _block_indicator(width, head, value):
    i = jnp.arange(width) // head
    return jnp.where(i[:, None] == i[None, :], value, 0.0).astype(jnp.bfloat16)


def _ada_kernel(c_ref, w_ref, b_ref, o_ref):
    c = c_ref[...]
    h = c * _sigmoid(c)
    o_ref[...] = jnp.dot(h.astype(jnp.bfloat16), w_ref[...].astype(jnp.bfloat16),
                         preferred_element_type=jnp.float32) + b_ref[...]


def ada_modulation(c8, w, layer, b):
    n = w.shape[2]
    tn = 1024
    return pl.pallas_call(
        _ada_kernel,
        grid=(n // tn,),
        in_specs=[pl.BlockSpec((8, D_MODEL), lambda j: (0, 0)),
                  pl.BlockSpec((None, D_MODEL, tn), lambda j: (layer, 0, j)),
                  pl.BlockSpec((1, tn), lambda j: (0, j))],
        out_specs=pl.BlockSpec((8, tn), lambda j: (0, j)),
        out_shape=jax.ShapeDtypeStruct((8, n), jnp.float32),
        compiler_params=_params(1),
    )(c8, w, b)


W_SUB = CB // LANES
W_PAD_PIECE = (COL_RW * CB + RW_REAL) // LANES


def _proj_kernel(x_ref, sh_ref, sc_ref, *refs, n_lat):
    w_refs, (o_ref, h_ref) = refs[:W_SUB], refs[W_SUB:]
    j = pl.program_id(1)

    @pl.when(j == 0)
    def _():
        is_ctx = _ctx_rows(x_ref.shape[0], n_lat)
        h = _ln_rows(x_ref[...]) * (1.0 + _mod_row(sc_ref, is_ctx)) + _mod_row(sh_ref, is_ctx)
        h_ref[...] = h.astype(jnp.bfloat16)

    pieces = [w_refs[k][...] for k in range(W_SUB)]
    for k in range(W_SUB):
        if (W_PAD_PIECE - k) % W_SUB == 0:
            pieces[k] = jnp.where(j * W_SUB + k == W_PAD_PIECE, 0.0, pieces[k])
    w = jnp.concatenate(pieces, axis=1).astype(jnp.bfloat16)
    o_ref[...] = jnp.dot(h_ref[...], w, preferred_element_type=jnp.float32)


def modulated_projection(x, mod, w, layer, n_lat):
    m = x.shape[0]
    tm = PROJ_TM

    def piece(k):
        def index(i, j):
            p = j * W_SUB + k
            return layer, 0, jnp.where(p < W_PAD_PIECE, p, p - 1)
        return pl.BlockSpec((None, D_MODEL, LANES), index)

    return pl.pallas_call(
        functools.partial(_proj_kernel, n_lat=n_lat),
        grid=(m // tm, P_IN_PAD // CB),
        in_specs=[pl.BlockSpec((tm, D_MODEL), lambda i, j: (i, 0)),
                  pl.BlockSpec((8, D_MODEL), lambda i, j: (0, 0)),
                  pl.BlockSpec((8, D_MODEL), lambda i, j: (0, 1))] + [piece(k) for k in range(W_SUB)],
        out_specs=pl.BlockSpec((tm, CB), lambda i, j: (i, j)),
        out_shape=jax.ShapeDtypeStruct((m, P_IN_PAD), jnp.float32),
        scratch_shapes=[pltpu.VMEM((tm, D_MODEL), jnp.bfloat16)],
        compiler_params=_params(2),
    )(x, mod, mod, *([w] * W_SUB))


def _halo_specs(tp, n_rows, col, width=CB):
    per = tp // HALO
    last = n_rows // HALO - 1
    main = pl.BlockSpec((tp, width), lambda i: (i, col))
    prev = pl.BlockSpec((HALO, width), lambda i: (jnp.maximum(i * per - 1, 0), col))
    nxt = pl.BlockSpec((HALO, width), lambda i: (jnp.minimum((i + 1) * per, last), col))
    return [main, prev, nxt]


def _neighbours(x, prev, nxt, n_lat, n_rows):
    tp = x.shape[0]
    loc = lax.broadcasted_iota(jnp.int32, (tp, 1), 0)
    row = pl.program_id(0) * tp + loc
    xp = jnp.where(loc == 0, prev[HALO - 1:HALO, :], pltpu.roll(x, 1, axis=0))
    xp = jnp.where((row == 0) | (row == n_lat), 0.0, xp)
    xn = jnp.where(loc == tp - 1, nxt[0:1, :], pltpu.roll(x, tp - 1, axis=0))
    xn = jnp.where((row == n_lat - 1) | (row == n_rows - 1), 0.0, xn)
    return xp, xn


def _conv3_kernel(z_ref, zp_ref, zn_ref, w_ref, b_ref, o_ref, *, n_lat, n_rows):
    x = z_ref[...]
    xp, xn = _neighbours(x, zp_ref[...], zn_ref[...], n_lat, n_rows)
    w = w_ref[...]
    o_ref[...] = xp * w[0:1, :] + x * w[1:2, :] + xn * w[2:3, :] + b_ref[...]


def hyena_conv3(z, w, b, n_lat):
    n_rows = z.shape[0]
    tp = PREP_ROWS
    width = w.shape[1]
    return pl.pallas_call(
        functools.partial(_conv3_kernel, n_lat=n_lat, n_rows=n_rows),
        grid=(n_rows // tp,),
        in_specs=_halo_specs(tp, n_rows, COL_HY, width) + [_full(w), _full(b)],
        out_specs=pl.BlockSpec((tp, width), lambda i: (i, 0)),
        out_shape=jax.ShapeDtypeStruct((n_rows, width), jnp.float32),
        compiler_params=_params(1),
    )(z, z, z, w, b)


HY_EMB_PAD = 40
HY_FILTER_ROWS = 512


def hyena_features(L):
    t = jnp.linspace(0.0, 1.0, L, dtype=jnp.float32)[:, None]
    n_bands = (HY_EMB - 1) // 2
    f = jnp.linspace(1e-4, n_bands - 1, n_bands, dtype=jnp.float32)[None, :]
    ang = (2.0 * math.pi / L) * jnp.arange(L, dtype=jnp.float32)[:, None] * f
    z = jnp.concatenate([t, jnp.cos(ang), -jnp.sin(ang)], -1)
    return jnp.pad(z, ((0, 0), (0, HY_EMB_PAD - HY_EMB)))


def _filter_kernel(z_ref, w1_ref, b1_ref, w2_ref, b2_ref, w3_ref, b3_ref, fr_ref, dl_ref, h_ref, s_ref):
    i = pl.program_id(0)
    z = z_ref[...]
    fr = fr_ref[...]
    h = jnp.sin(fr * (_dot(z, w1_ref[...]) + b1_ref[...]))
    h = jnp.sin(fr * (_dot(h, w2_ref[...]) + b2_ref[...]))
    h = _mm1(h.astype(jnp.bfloat16), w3_ref[...].astype(jnp.bfloat16)) + b3_ref[...]
    win = jnp.exp(-z[:, 0:1] * dl_ref[...])
    h = h * jnp.concatenate([win] * (h.shape[1] // win.shape[1]), axis=1)

    @pl.when(i == 0)
    def _():
        s_ref[...] = jnp.zeros_like(s_ref)

    s_ref[...] += jnp.sum(jnp.abs(h), axis=0, keepdims=True)
    row = lax.broadcasted_iota(jnp.int32, h.shape, 0) + i * h.shape[0]
    col = lax.broadcasted_iota(jnp.int32, h.shape, 1)
    neg = (col // GROUP_W) % 2 == 1
    h_ref[...] = jnp.where(neg & (row == 0), 0.0, h)


def hyena_filter_bank(L, w1, b1, w2, b2, w3, b3, freq):
    z = hyena_features(L)
    w1p = jnp.pad(w1, ((0, HY_EMB_PAD - HY_EMB), (0, 0)))
    max_decay = math.log(HY_TARGET) / HY_FAST_DECAY
    min_decay = math.log(HY_TARGET) / HY_SLOW_DECAY
    deltas = jnp.abs(jnp.linspace(min_decay, max_decay, GROUP_W, dtype=jnp.float32))[None, :]
    n = w3.shape[1]
    tr = min(L, HY_FILTER_ROWS)
    args = (z, w1p, b1[None, :], w2, b2[None, :], w3, b3[None, :], freq[None, :], deltas)
    return pl.pallas_call(
        _filter_kernel,
        grid=(L // tr,),
        in_specs=[pl.BlockSpec((tr, HY_EMB_PAD), lambda i: (i, 0))] + [_full(a) for a in args[1:]],
        out_specs=[pl.BlockSpec((tr, n), lambda i: (i, 0)), pl.BlockSpec((1, n), lambda i: (0, 0))],
        out_shape=[jax.ShapeDtypeStruct((L, n), jnp.float32), jax.ShapeDtypeStruct((1, n), jnp.float32)],
        compiler_params=_params(1),
    )(*args)


FFT_NB = 8
FFT_NA = 16
FFT_CB = 512
FFT_N1 = 128


def fft_tables(n1, n2):
    N = n1 * n2
    nk = -(-(n2 // 2 + 1) // FFT_NB) * FFT_NB
    a = jnp.arange(n1, dtype=jnp.float32)[:, None, None]
    k2 = jnp.arange(nk, dtype=jnp.float32)[None, :, None]
    b = jnp.arange(n2 // 2, dtype=jnp.float32)[None, None, :]
    ph = (jnp.mod(a * k2, float(N)) / N + jnp.mod(b * k2, float(n2)) / n2) * (-2.0 * math.pi)
    g = jnp.concatenate([jnp.cos(ph), jnp.sin(ph)], axis=1)
    kk = jnp.arange(nk)
    weight = jnp.where((kk == 0) | (kk == n2 // 2), 1.0, jnp.where(kk < n2 // 2, 2.0, 0.0))
    ginv = jnp.transpose(g, (0, 2, 1)) * (jnp.tile(weight, 2) / N)
    k1 = jnp.arange(n1, dtype=jnp.float32)[:, None]
    aa = jnp.arange(n1, dtype=jnp.float32)[None, :]
    f = jnp.mod(k1 * aa, float(n1)) * (-2.0 * math.pi / n1)
    fr, fi = jnp.cos(f), jnp.sin(f)
    ff = jnp.concatenate([jnp.concatenate([fr, -fi], axis=1), jnp.concatenate([fi, fr], axis=1)], axis=0)
    bf = lambda t: t.astype(jnp.bfloat16)
    return {"g": bf(g), "ginv": bf(ginv), "ff": bf(ff), "fft": bf(ff.T)}


def _stage_a_kernel(u_ref, g_ref, o_ref):
    for j in range(FFT_NA):
        o_ref[j] = _mm1(g_ref[j], u_ref[:, j, :].astype(jnp.bfloat16))


def fft_stage_a(u3, nb, col, ncol, g):
    n1 = u3.shape[1]
    rows = g.shape[1]
    gspec = pl.BlockSpec((FFT_NA, rows, nb), lambda c, i: (i, 0, 0))
    return pl.pallas_call(
        _stage_a_kernel,
        grid=(ncol, n1 // FFT_NA),
        in_specs=[pl.BlockSpec((nb, FFT_NA, CB), lambda c, i: (0, i, col + c)), gspec],
        out_specs=pl.BlockSpec((FFT_NA, rows, CB), lambda c, i: (i, 0, c)),
        out_shape=jax.ShapeDtypeStruct((n1, rows, ncol * CB), jnp.float32),
        compiler_params=_params(2),
    )(u3, g)


def _stage_b_filter_kernel(re_ref, im_ref, f_ref, o_ref):
    ff = f_ref[...]
    for j in range(FFT_NB):
        o_ref[j] = _mm1(ff, jnp.concatenate([re_ref[:, j, :], im_ref[:, j, :]], axis=0).astype(jnp.bfloat16))


def fft_stage_b_filter(t1, ff):
    n1, n2x2, C = t1.shape
    n2 = n2x2 // 2
    blk = lambda off: pl.BlockSpec((n1, FFT_NB, FFT_CB), lambda c, i: (0, i + off, c))
    mat = pl.BlockSpec((2 * n1, 2 * n1), lambda c, i: (0, 0))
    return pl.pallas_call(
        _stage_b_filter_kernel,
        grid=(C // FFT_CB, n2 // FFT_NB),
        in_specs=[blk(0), blk(n2 // FFT_NB), mat],
        out_specs=pl.BlockSpec((FFT_NB, 2 * n1, FFT_CB), lambda c, i: (i, 0, c)),
        out_shape=jax.ShapeDtypeStruct((n2, 2 * n1, C), jnp.float32),
        compiler_params=_params(2),
    )(t1, t1, ff)


def _filter_spectrum(p, q, s, n1):
    return (p[:n1] + q[:n1]) * s, (p[n1:] - q[n1:]) * s


def _stage_b_conv_kernel(re_ref, im_ref, p_ref, q_ref, s_ref, f_ref, ft_ref, ore_ref, oim_ref):
    n1 = re_ref.shape[0]
    s = s_ref[...]
    ff, fft_ = f_ref[...], ft_ref[...]
    bf = lambda t: t.astype(jnp.bfloat16)
    for j in range(FFT_NB):
        x = _mm1(ff, bf(jnp.concatenate([re_ref[:, j, :], im_ref[:, j, :]], axis=0)))
        hr, hi = _filter_spectrum(p_ref[j], q_ref[j], s, n1)
        xr, xi = x[:n1], x[n1:]
        z = _mm1(fft_, bf(jnp.concatenate([xr * hr - xi * hi, xr * hi + xi * hr], axis=0)))
        ore_ref[:, j, :] = z[:n1]
        oim_ref[:, j, :] = z[n1:]


def fft_stage_b_conv(t1, spec, col_p, col_q, inv_norm, ff, fft_):
    n1, n2x2, C = t1.shape
    n2 = n2x2 // 2
    ncb = C // FFT_CB
    blk = lambda off: pl.BlockSpec((n1, FFT_NB, FFT_CB), lambda c, i: (0, i + off, c))
    mat = pl.BlockSpec((2 * n1, 2 * n1), lambda c, i: (0, 0))
    sp = lambda col: pl.BlockSpec((FFT_NB, 2 * n1, FFT_CB), lambda c, i: (i, 0, col * ncb + c))
    return pl.pallas_call(
        _stage_b_conv_kernel,
        grid=(ncb, n2 // FFT_NB),
        in_specs=[blk(0), blk(n2 // FFT_NB), sp(col_p), sp(col_q), pl.BlockSpec((1, FFT_CB), lambda c, i: (0, c)),
                  mat, mat],
        out_specs=[blk(0), blk(0)],
        out_shape=[jax.ShapeDtypeStruct((n1, n2, C), jnp.float32)] * 2,
        compiler_params=_params(2),
    )(t1, t1, spec, spec, inv_norm, ff, fft_)


def _stage_a_inv_kernel(re_ref, im_ref, g_ref, u_ref, gate_ref, bias_ref, o_ref):
    for j in range(FFT_NA):
        y = _mm1(g_ref[j], jnp.concatenate([re_ref[j], im_ref[j]], axis=0).astype(jnp.bfloat16))
        o_ref[:, j, :] = gate_ref[:, j, :] * (y + u_ref[:, j, :] * bias_ref[...])


def fft_stage_a_inv(t2re, t2im, ginv, nb, u3, u_col, gate3, gate_col, bias):
    n1, nk, C = t2re.shape
    tb = pl.BlockSpec((FFT_NA, nk, CB), lambda c, i: (i, 0, c))
    gb = pl.BlockSpec((FFT_NA, nb, 2 * nk), lambda c, i: (i, 0, 0))
    ub = lambda col: pl.BlockSpec((nb, FFT_NA, CB), lambda c, i: (0, i, col + c))
    return pl.pallas_call(
        _stage_a_inv_kernel,
        grid=(C // CB, n1 // FFT_NA),
        in_specs=[tb, tb, gb, ub(u_col), ub(gate_col), pl.BlockSpec((1, CB), lambda c, i: (0, c))],
        out_specs=ub(0),
        out_shape=jax.ShapeDtypeStruct((nb, n1, C), jnp.float32),
        compiler_params=_params(2),
    )(t2re, t2im, ginv, u3, gate3, bias)


def hyena_long_conv_chain(u, L, biases, filt, colsum, tabs):
    C = GROUP_W
    n1 = FFT_N1
    n2 = 2 * L // n1
    nb = n2 // 2
    spec = fft_stage_b_filter(fft_stage_a(filt.reshape(nb, n1, filt.shape[1]), nb, 0, filt.shape[1] // CB, tabs["g"]),
                              tabs["ff"])
    s4 = colsum.reshape(HY_ORDER, 2, C)
    inv_norm = 1.0 / (s4[:, 0] + s4[:, 1])
    u3 = u.reshape(u.shape[0] // n1, n1, u.shape[1])
    y3, y_col = u3, 0
    for n in range(HY_ORDER):
        t1 = fft_stage_a(y3, nb, y_col, 1, tabs["g"])
        t2re, t2im = fft_stage_b_conv(t1, spec, 2 * n, 2 * n + 1, inv_norm[n][None, :], tabs["ff"], tabs["fft"])
        y3 = fft_stage_a_inv(t2re, t2im, tabs["ginv"], nb, y3, y_col, u3, n + 1, biases[n][None, :])
        y_col = 0
    return y3.reshape(L, C)


def dense_dft_tables(n):
    N = 2 * n
    k = jnp.arange(N, dtype=jnp.float32)[:, None]
    t = jnp.arange(n, dtype=jnp.float32)[None, :]
    ph = jnp.mod(k * t, float(N)) * (2.0 * math.pi / N)
    fd = jnp.concatenate([jnp.cos(ph), -jnp.sin(ph)], axis=0)
    return fd.astype(jnp.bfloat16), (fd.T / N).astype(jnp.bfloat16)


def _hyena_ctx_kernel(u_ref, h_ref, s_ref, bias_ref, fd_ref, ft_ref, o_ref):
    C = GROUP_W
    fd, ft = fd_ref[...], ft_ref[...]
    K = fd.shape[0] // 2
    bf = lambda t: t.astype(jnp.bfloat16)
    hs = _mm1(fd, bf(h_ref[...]))
    s = s_ref[...]
    y = u_ref[:, 0:C]
    for n in range(HY_ORDER):
        cp, cq = 2 * n * C, (2 * n + 1) * C
        inv = 1.0 / (s[:, cp:cp + C] + s[:, cq:cq + C])
        hr, hi = _filter_spectrum(hs[:, cp:cp + C], hs[:, cq:cq + C], inv, K)
        x = _mm1(fd, bf(y))
        xr, xi = x[:K], x[K:]
        conv = _mm1(ft, bf(jnp.concatenate([xr * hr - xi * hi, xr * hi + xi * hr], axis=0)))
        y = u_ref[:, (n + 1) * C:(n + 2) * C] * (conv + y * bias_ref[n:n + 1, :])
    o_ref[...] = y


def hyena_ctx(u, filt, colsum, biases, tabs):
    args = (u, filt, colsum, biases, tabs[0], tabs[1])
    return pl.pallas_call(
        _hyena_ctx_kernel,
        in_specs=[_full(a) for a in args],
        out_specs=pl.BlockSpec((u.shape[0], GROUP_W), lambda: (0, 0)),
        out_shape=jax.ShapeDtypeStruct((u.shape[0], GROUP_W), jnp.float32),
        compiler_params=pltpu.CompilerParams(vmem_limit_bytes=VMEM_LIMIT),
    )(*args)


SCAN_T = 64
GROUP_LANES = 256
RW_SUB = 16
HG_SUB = 16
LOG2_E = 1.4426950408889634


def _chunk_index(d, i, n_ctx, n_all):
    fwd = jnp.where(i < n_ctx, n_all - n_ctx + i, i - n_ctx)
    return jnp.where(d == 0, fwd, n_all - 1 - i)


def _stacking(T, S, nh, head, sign):
    G = nh * head
    nb = T // S
    n = nh * T
    rr = lax.broadcasted_iota(jnp.int32, (n, G), 0)
    same = ((rr // S) % nh) == (lax.broadcasted_iota(jnp.int32, (n, G), 1) // head)

    def bd(x):
        pieces = []
        for i in range(nb):
            pieces += [x[i * S:(i + 1) * S]] * nh
        return jnp.where(same, jnp.concatenate(pieces, axis=0), 0.0)

    def collapse(o):
        outs = []
        for i in range(nb):
            acc = o[i * nh * S:i * nh * S + S]
            for h in range(1, nh):
                acc = acc + o[i * nh * S + h * S:i * nh * S + (h + 1) * S]
            outs.append(acc)
        return jnp.concatenate(outs, axis=0)

    rt = lax.broadcasted_iota(jnp.int32, (n, n), 0)
    cs = lax.broadcasted_iota(jnp.int32, (n, n), 1)
    t_r = (rt // (nh * S)) * S + rt % S
    t_c = (cs // (nh * S)) * S + cs % S
    same_h = ((rt // S) % nh) == ((cs // S) % nh)
    before = same_h & ((t_r - t_c) * sign > 0)
    return bd, collapse, before, rt, cs


def _softplus(x):
    return jnp.maximum(x, 0.0) + jnp.log(1.0 + jnp.exp(-jnp.abs(x)))


def _rwkv_prep_kernel(*refs, n_lat, n_rows):
    zrefs, rest = refs[:15], refs[15:]
    (mu_ref, kk_w_ref, ka_ref, w0_ref, a0_ref, w2h_ref, w2l_ref, a2h_ref, a2l_ref, ones_ref,
     r_ref, k_ref, v_ref, g_ref, kk_ref, lw_ref, kd_ref, a_ref) = rest
    slabs = []
    for c in range(5):
        x = zrefs[3 * c][...]
        xp, xn = _neighbours(x, zrefs[3 * c + 1][...], zrefs[3 * c + 2][...], n_lat, n_rows)
        slabs.append(x + (0.5 * (xp + xn) - x) * mu_ref[:, c * CB:(c + 1) * CB])
    r, k, v, g, lora = slabs
    r_ref[...], k_ref[...], v_ref[...], g_ref[...] = r, k, v, g
    kk = k * kk_w_ref[...]
    ss = _mm2(kk * kk, ones_ref[...])
    kk_ref[...] = kk * lax.rsqrt(jnp.maximum(ss, 1e-24))
    lora_t = _split(jnp.tanh(lora))
    lora_s = _split(lora)
    for d in range(2):
        w_log = -_softplus(-(w0_ref[d:d + 1, :] + _mm3(lora_t, (w2h_ref[d], w2l_ref[d])))) - 0.5
        lw_ref[d] = -jnp.exp(w_log)
        a = _sigmoid(a0_ref[d:d + 1, :] + _mm3(lora_s, (a2h_ref[d], a2l_ref[d])))
        a_ref[d] = a
        kd_ref[d] = k * (1.0 + (a - 1.0) * ka_ref[...])


def rwkv_prep(z, mu, k_k, k_a, w0, a0, w2, a2, n_lat):
    n_rows = z.shape[0]
    tp = PREP_ROWS
    mu_p = jnp.pad(mu, (0, 5 * CB - RW_REAL))[None, :]
    w2p = jnp.zeros((2, CB, GROUP_W), jnp.float32)
    a2p = jnp.zeros((2, CB, GROUP_W), jnp.float32)
    for d in range(2):
        w2p = w2p.at[d, d * RW_LORA:(d + 1) * RW_LORA].set(w2[d])
        a2p = a2p.at[d, (2 + d) * RW_LORA:(3 + d) * RW_LORA].set(a2[d])
    w2s, a2s = _split(w2p), _split(a2p)
    small = (mu_p, k_k[None, :], k_a[None, :], w0, a0, w2s[0], w2s[1], a2s[0], a2s[1],
             _block_indicator(GROUP_W, RW_HEAD, 1.0))
    zspecs = []
    for c in range(5):
        zspecs += _halo_specs(tp, n_rows, COL_RW + c)
    one = pl.BlockSpec((tp, CB), lambda i: (i, 0))
    two = pl.BlockSpec((2, tp, CB), lambda i: (0, i, 0))
    s1 = jax.ShapeDtypeStruct((n_rows, GROUP_W), jnp.float32)
    s2 = jax.ShapeDtypeStruct((2, n_rows, GROUP_W), jnp.float32)
    return pl.pallas_call(
        functools.partial(_rwkv_prep_kernel, n_lat=n_lat, n_rows=n_rows),
        grid=(n_rows // tp,),
        in_specs=zspecs + [_full(a) for a in small],
        out_specs=[one] * 5 + [two] * 3,
        out_shape=[s1] * 5 + [s2] * 3,
        compiler_params=_params(1),
    )(*([z] * 15), *small)


def _rwkv_scan_body(refs, head):
    in_f, in_b, (of_ref, ob_ref, ht_ref) = refs[:6], refs[6:12], refs[12:]
    T, G, S = SCAN_T, GROUP_LANES, RW_SUB
    nh = G // head
    nb = T // S
    n = nh * T

    @pl.when(pl.program_id(0) == 0)
    def _():
        ht_ref[...] = jnp.zeros_like(ht_ref)

    ti = lax.broadcasted_iota(jnp.int32, (T, T), 0)
    si = lax.broadcasted_iota(jnp.int32, (T, T), 1)
    bf = lambda t: t.astype(jnp.bfloat16)
    each = lambda f, *seqs: [f(*xs) for xs in zip(*seqs)]
    n_grp = in_f[0].shape[1] // G
    lanes = [slice(grp * G, (grp + 1) * G) for grp in range(n_grp)]

    r, v, kk, lw, k, a, tri, before, incl, dest = ([] for _ in range(10))
    for (r_ref, v_ref, kk_ref, lw_ref, k_ref, a_ref), o_ref, sign in ((in_f, of_ref, 1), (in_b, ob_ref, -1)):
        bd, collapse, before_d, rt, cs = _stacking(T, S, nh, head, sign)
        for ls in lanes:
            r.append(r_ref[:, ls]), v.append(v_ref[:, ls]), kk.append(kk_ref[:, ls])
            lw.append(lw_ref[0, :, ls]), k.append(k_ref[0, :, ls]), a.append(a_ref[0, :, ls])
            tri.append(jnp.where((ti - si) * sign >= 0, 1.0, 0.0))
            before.append(before_d), incl.append(before_d | (rt == cs)), dest.append((o_ref, ls))
    diag_blk = (rt // (nh * S)) == (cs // (nh * S))
    eye = jnp.where(rt == cs, 1.0, 0.0)
    ht = [ht_ref[ch] for ch in range(len(r))]

    c = each(lambda tr, t: _dot(tr, t), tri, lw)
    ctot = each(lambda t: jnp.sum(t, axis=0, keepdims=True), lw)
    beta = each(lambda p, q: p * q, kk, a)
    einv = each(lambda t: jnp.exp(-t), c)
    lhs = each(lambda kk_, r_, c_, lw_: bf(jnp.concatenate([bd(-kk_ * jnp.exp(c_ - lw_)), bd(r_ * jnp.exp(c_))], axis=0)),
               kk, r, c, lw)
    rhs = each(lambda k_, b_, e_: bf(jnp.concatenate([bd(k_ * e_), bd(b_ * e_)], axis=0)), k, beta, einv)
    m = each(lambda p, q: _mm1(p, q, NT), lhs, rhs)
    g = each(lambda p, h: _mm1(p, bf(h), NT), lhs, ht)
    v_b = each(lambda t: bf(bd(t)), v)
    x = each(lambda g_, m_, v_, bm: g_[:n] + _mm1(bf(jnp.where(bm, m_[:n, :n], 0.0)), v_), g, m, v_b, before)
    lb = each(lambda m_, bm: jnp.where(bm, m_[:n, n:], 0.0), m, before)
    ld = each(lambda t: jnp.where(diag_blk, t, 0.0), lb)
    lo = each(lambda p, q: p - q, lb, ld)
    xd = each(lambda t: eye + t, ld)
    lp = each(bf, ld)
    p = 2
    while p < S:
        lp = each(lambda t: bf(_mm1(t, t)), lp)
        xd = each(lambda x_, l_: x_ + _mm1(l_, bf(x_)), xd, lp)
        p *= 2
    xd_b = each(bf, xd)
    n_b = each(lambda x_, l_: bf(_mm1(x_, bf(l_))), xd_b, lo)
    y = each(lambda x_, t: _mm1(x_, bf(t)), xd_b, x)
    y = each(lambda y_, n_: y_ + _mm1(n_, bf(y_)), y, n_b)
    p = 2
    while p < nb:
        n_b = each(lambda t: bf(_mm1(t, t)), n_b)
        y = each(lambda y_, n_: y_ + _mm1(n_, bf(y_)), y, n_b)
        p *= 2
    u_b = each(bf, y)
    o = each(lambda g_, m_, v_, u_, im: (g_[n:] + _mm1(bf(jnp.where(im, m_[n:, :n], 0.0)), v_))
             + _mm1(bf(jnp.where(im, m_[n:, n:], 0.0)), u_), g, m, v_b, u_b, incl)
    efin = each(lambda ct, c_: jnp.exp(ct - c_), ctot, c)
    ht_new = each(lambda h, ct, v_, u_, k_, b_, e_: (h * jnp.exp(ct) + _mm1(v_, bf(bd(k_ * e_)), TN))
                  + _mm1(u_, bf(bd(b_ * e_)), TN), ht, ctot, v_b, u_b, k, beta, efin)
    for ch, (o_ref, ls) in enumerate(dest):
        o_ref[:, ls] = collapse(o[ch])
        ht_ref[ch] = ht_new[ch]


def rope_tables(n_ctx, L):
    d_axis = RT_HEAD // 2
    half = d_axis // 2
    inv = ROPE_BASE ** (-jnp.arange(0, d_axis, 2, dtype=jnp.float32) / d_axis)
    t = jnp.arange(L)
    pos = jnp.stack([(t // GRID_W).astype(jnp.float32), (t % GRID_W).astype(jnp.float32)], axis=1)
    j = jnp.arange(RT_HEAD)
    ang = pos[:, j // d_axis] * inv[j % half][None, :]
    sgn = jnp.where((j % d_axis) < half, -1.0, 1.0)[None, :]
    cos = jnp.concatenate([jnp.cos(ang), jnp.ones((n_ctx, RT_HEAD), jnp.float32)], axis=0)
    sin = jnp.concatenate([jnp.sin(ang) * sgn, jnp.zeros((n_ctx, RT_HEAD), jnp.float32)], axis=0)
    return jnp.tile(cos, (1, RT_HEADS)), jnp.tile(sin, (1, RT_HEADS))


def _rotate(x, cos, sin):
    G = x.shape[1]
    half = RT_HEAD // 4
    lane = lax.broadcasted_iota(jnp.int32, x.shape, 1)
    partner = jnp.where((lane % (2 * half)) < half, pltpu.roll(x, G - half, axis=1), pltpu.roll(x, half, axis=1))
    return x * cos + partner * sin


def _retention_scan_body(q_ref, k_ref, v_ref, cos_ref, sin_ref, lg_ref, o_ref, st_ref, *, head, d):
    T, G = SCAN_T, GROUP_LANES
    nh = G // head
    sign = 1 - 2 * d

    t = lax.broadcasted_iota(jnp.int32, (T, 1), 0)
    pos = (t + d * (T - 1 - 2 * t) + 1).astype(jnp.float32)
    bd, collapse, before, rt, cs = _stacking(T, T, nh, head, sign)
    incl = before | (rt == cs)
    bf = lambda x: x.astype(jnp.bfloat16)

    groups = range(q_ref.shape[1] // G)
    lanes = [slice(grp * G, (grp + 1) * G) for grp in groups]
    each = lambda f, *seqs: [f(*xs) for xs in zip(*seqs)]
    st = [st_ref[grp] for grp in groups]
    cos, sin, lg = ([ref[:, ls] for ls in lanes] for ref in (cos_ref, sin_ref, lg_ref))
    q = each(_rotate, [q_ref[:, ls] for ls in lanes], cos, sin)
    k = each(lambda x, c_, s_: _rotate(x, c_, s_) * (head ** -0.5), [k_ref[:, ls] for ls in lanes], cos, sin)
    c = each(lambda lg_: pos * lg_, lg)
    q_b = each(lambda q_, c_: bf(bd(q_ * jnp.exp(c_))), q, c)
    kt_b = each(lambda k_, c_: bf(bd(k_ * jnp.exp(-c_))), k, c)
    v_b = [bf(bd(v_ref[:, ls])) for ls in lanes]
    scores = each(lambda q_, k_: bf(jnp.where(incl, _mm1(q_, k_, NT), 0.0)), q_b, kt_b)
    inter = each(lambda q_, s_: _mm1(q_, bf(s_), NT), q_b, st)
    o = each(lambda g_, s_, v_: g_ + _mm1(s_, v_), inter, scores, v_b)
    st_new = each(lambda s_, lg_, v_, k_, c_: s_ * jnp.exp(float(T) * lg_)
                  + _mm1(v_, bf(bd(k_ * jnp.exp(float(T) * lg_ - c_))), TN), st, lg, v_b, k, c)
    for grp in groups:
        o_ref[:, lanes[grp]] = collapse(o[grp])
        st_ref[grp] = st_new[grp]


def _gla_scan_body(q_ref, f_ref, i_ref, lb_ref, o_ref, st_ref, *, head, d):
    T, S = SCAN_T, HG_SUB
    W = q_ref.shape[-1]
    nh = W // head
    sign = 1 - 2 * d

    ti = lax.broadcasted_iota(jnp.int32, (S, S), 0)
    si = lax.broadcasted_iota(jnp.int32, (S, S), 1)
    tri_incl = jnp.where((ti - si) * sign >= 0, 1.0, 0.0)
    row = lax.broadcasted_iota(jnp.int32, (S, 1), 0)
    lb = lb_ref[0]

    states = [st_ref[h] for h in range(nh)]
    for j in range(T // S):
        jb = j + d * (T // S - 1 - 2 * j)
        rows = pl.ds(jb * S, S)
        q = q_ref[rows, :]
        q = q * _sigmoid(q)
        v = i_ref[rows, :]
        gate = lb + (1.0 - lb) * _sigmoid(f_ref[rows, :])
        lf = jnp.log(jnp.maximum(gate, HG_MIN_GATE))
        k = 1.0 - gate
        b = _dot(tri_incl, lf)
        btot = jnp.sum(lf, axis=0, keepdims=True)
        qe = q * jnp.exp(b)
        ke = k * jnp.exp(btot - b)
        outs = []
        for h in range(nh):
            ls = slice(h * head, (h + 1) * head)
            qh, kh, vh = q[:, ls], k[:, ls], v[:, ls]
            bh = b[:, ls] * LOG2_E
            st = states[h]
            o = _mm1(qe[:, ls].astype(jnp.bfloat16), st.astype(jnp.bfloat16), NT)
            for s in range(S):
                e = jnp.exp2(bh - bh[s:s + 1, :])
                a_s = jnp.sum(qh * kh[s:s + 1, :] * e, axis=-1, keepdims=True)
                a_s = jnp.where((row - s) * sign >= 0, a_s, 0.0)
                o = o + a_s * vh[s:s + 1, :]
            outs.append(o)
            states[h] = st * jnp.exp(btot[:, ls]) + _mm1(vh.astype(jnp.bfloat16), ke[:, ls].astype(jnp.bfloat16), TN)
        o_ref[rows, :] = jnp.concatenate(outs, axis=1)
    for h in range(nh):
        st_ref[h] = states[h]


N_RW, N_RT, N_HG = 6, 6, 4


def _scans_kernel(*refs):
    n_in = 2 * (N_RW + N_RT + N_HG)
    ins, (rw_of, rw_ob, rt_of, rt_ob, hg_of, hg_ob, rw_st, rt_st, hg_st) = refs[:n_in], refs[n_in:]
    rw_in, rt_in, hg_in = ins[:2 * N_RW], ins[2 * N_RW:2 * (N_RW + N_RT)], ins[2 * (N_RW + N_RT):]

    @pl.when(pl.program_id(0) == 0)
    def _():
        rt_st[...] = jnp.zeros_like(rt_st)
        hg_st[...] = jnp.zeros_like(hg_st)

    n_rt, n_hg = rt_st.shape[0] // 2, hg_st.shape[0] // 2
    _rwkv_scan_body(list(rw_in) + [rw_of, rw_ob, rw_st], RW_HEAD)
    for d, (rt_o, hg_o) in enumerate(((rt_of, hg_of), (rt_ob, hg_ob))):
        _gla_scan_body(*hg_in[d * N_HG:(d + 1) * N_HG], hg_o, hg_st.at[d * n_hg:(d + 1) * n_hg], head=HG_HEAD, d=d)
        _retention_scan_body(*rt_in[d * N_RT:(d + 1) * N_RT], rt_o, rt_st.at[d * n_rt:(d + 1) * n_rt],
                             head=RT_HEAD, d=d)


def recurrent_scans(z, rw, cos, sin, lg, lb2, n_ctx_rows):
    N = z.shape[0]
    W = GROUP_W
    T = SCAN_T
    n_all, n_ctx = N // T, n_ctx_rows // T
    chunk = lambda d: (lambda i: (_chunk_index(d, i, n_ctx, n_all), 0))
    row = lambda d: pl.BlockSpec((T, W), chunk(d))
    per_dir = lambda d: pl.BlockSpec((1, T, W), lambda i: (d, _chunk_index(d, i, n_ctx, n_all), 0))
    zc = lambda d, col: pl.BlockSpec((T, CB), lambda i: (_chunk_index(d, i, n_ctx, n_all), col))
    rw_specs = lambda d: [row(d)] * 3 + [per_dir(d)] * 3
    rt_specs = lambda d: [zc(d, COL_RT), zc(d, COL_RT + 1), zc(d, COL_RT + 2), row(d), row(d),
                          pl.BlockSpec((1, W), lambda i: (0, 0))]
    hg_specs = lambda d: [zc(d, COL_HG), zc(d, COL_HG + 1 + d), zc(d, COL_HG + 3),
                          pl.BlockSpec((1, 1, W), lambda i: (d, 0, 0))]
    shape = jax.ShapeDtypeStruct((N, W), jnp.float32)
    return pl.pallas_call(
        _scans_kernel,
        grid=(n_all,),
        in_specs=rw_specs(0) + rw_specs(1) + rt_specs(0) + rt_specs(1) + hg_specs(0) + hg_specs(1),
        out_specs=[row(0), row(1)] * 3,
        out_shape=[shape] * 6,
        scratch_shapes=[pltpu.VMEM((2 * (W // GROUP_LANES), GROUP_LANES, GROUP_LANES), jnp.float32),
                        pltpu.VMEM((2 * (W // GROUP_LANES), GROUP_LANES, GROUP_LANES), jnp.float32),
                        pltpu.VMEM((2 * (W // HG_HEAD), HG_HEAD, HG_HEAD), jnp.float32)],
        compiler_params=_params(1),
    )(*(list(rw) * 2), *([z, z, z, cos, sin, lg] * 2), *([z, z, z, lb2] * 2))


def _outproj_kernel(hy_ref, rwof_ref, rwob_ref, r_ref, k_ref, v_ref, rwg_ref, rtof_ref, rtob_ref, rtg_ref,
                    hgof_ref, hgob_ref, hgg_ref,
                    rk_ref, gng_ref, gnb_ref, hgn_ref, avg64_ref, avg128_ref,
                    w_ref, x_ref, gate_ref, g_ref, b_ref, o_ref, *, n_lat):
    avg64, avg128 = avg64_ref[...], avg128_ref[...]

    def head_norm(o, avg, eps, centre):
        if centre:
            o = o - _mm2(o, avg)
        return o * lax.rsqrt(_mm2(o * o, avg) + eps)

    silu = lambda t: t * _sigmoid(t)
    y_rw = head_norm(rwof_ref[...] + rwob_ref[...], avg64, RW_GN_EPS, True) * gng_ref[...] + gnb_ref[...]
    bonus = (float(RW_HEAD) * _mm2(r_ref[...] * k_ref[...] * rk_ref[...], avg64)) * v_ref[...]
    y_rw = (y_rw + bonus) * _sigmoid(rwg_ref[...])
    y_rt = head_norm(rtof_ref[...] + rtob_ref[...], avg64, HEAD_NORM_EPS, True) * silu(rtg_ref[...])
    y_hg = head_norm(hgof_ref[...] + hgob_ref[...], avg128, HEAD_NORM_EPS, False) * hgn_ref[...] * silu(hgg_ref[...])
    y = None
    for m, ym in enumerate((hy_ref[...], y_rw, y_rt, y_hg)):
        part = jnp.dot(ym.astype(jnp.bfloat16), w_ref[m * GROUP_W:(m + 1) * GROUP_W, :],
                       preferred_element_type=jnp.float32)
        y = part if y is None else y + part
    is_ctx = _ctx_rows(x_ref.shape[0], n_lat)
    r = ALPHA * x_ref[...] + _mod_row(gate_ref, is_ctx) * y
    o_ref[...] = _ln_rows(r) * g_ref[...] + b_ref[...]


def outproj_deepnorm(y_hy, scans, rw_r, rw_k, rw_v, rw_g, z, r_k, gn_g, gn_b, hg_norm_g,
                     w_bf16, x, mod, g, b, n_lat, m):
    tm = OUT_TM
    one = pl.BlockSpec((tm, GROUP_W), lambda i: (i, 0))
    zc = lambda col: pl.BlockSpec((tm, CB), lambda i: (i, col))
    row = pl.BlockSpec((tm, D_MODEL), lambda i: (i, 0))
    small = (r_k, gn_g, gn_b, hg_norm_g, _block_indicator(GROUP_W, RW_HEAD, 1.0 / RW_HEAD),
             _block_indicator(GROUP_W, HG_HEAD, 1.0 / HG_HEAD))
    return pl.pallas_call(
        functools.partial(_outproj_kernel, n_lat=n_lat),
        grid=(m // tm,),
        in_specs=[one] * 9 + [zc(COL_RT + 3), one, one, zc(COL_HG + 4)]
        + [_full(a) for a in small]
        + [_full(w_bf16), row, pl.BlockSpec((8, D_MODEL), lambda i: (0, 2)), _full(g), _full(b)],
        out_specs=row,
        out_shape=jax.ShapeDtypeStruct((m, D_MODEL), jnp.float32),
        compiler_params=_params(1),
    )(y_hy, scans[0], scans[1], rw_r, rw_k, rw_v, rw_g, scans[2], scans[3], z, scans[4], scans[5], z,
      *small, w_bf16, x, mod, g, b)


def _ffn_kernel(x_ref, sh_ref, sc_ref, w1_ref, w3_ref, w2_ref, gate_ref, g_ref, b_ref, o_ref, h_ref, acc_ref, *, n_lat):
    j = pl.program_id(1)

    @pl.when(j == 0)
    def _():
        is_ctx = _ctx_rows(x_ref.shape[0], n_lat)
        h = _ln_rows(x_ref[...]) * (1.0 + _mod_row(sc_ref, is_ctx)) + _mod_row(sh_ref, is_ctx)
        h_ref[...] = h.astype(jnp.bfloat16)
        acc_ref[...] = jnp.zeros_like(acc_ref)

    h = h_ref[...]
    a = jnp.dot(h, w1_ref[...], preferred_element_type=jnp.float32)
    u = jnp.dot(h, w3_ref[...], preferred_element_type=jnp.float32)
    s = (a * _sigmoid(a) * u).astype(jnp.bfloat16)
    acc_ref[...] += jnp.dot(s, w2_ref[...], preferred_element_type=jnp.float32)

    @pl.when(j == pl.num_programs(1) - 1)
    def _():
        is_ctx = _ctx_rows(x_ref.shape[0], n_lat)
        r = ALPHA * x_ref[...] + _mod_row(gate_ref, is_ctx) * acc_ref[...]
        o_ref[...] = _ln_rows(r) * g_ref[...] + b_ref[...]


def ffn_deepnorm(x, mod, w1, w3, w2, g, b, n_lat, tm):
    m = x.shape[0]
    row = pl.BlockSpec((tm, D_MODEL), lambda i, j: (i, 0))
    modc = lambda c: pl.BlockSpec((8, D_MODEL), lambda i, j: (0, c))
    vec = pl.BlockSpec((1, D_MODEL), lambda i, j: (0, 0))
    return pl.pallas_call(
        functools.partial(_ffn_kernel, n_lat=n_lat),
        grid=(m // tm, FFN_HIDDEN // FFN_TF),
        in_specs=[row, modc(3), modc(4),
                  pl.BlockSpec((D_MODEL, FFN_TF), lambda i, j: (0, j)),
                  pl.BlockSpec((D_MODEL, FFN_TF), lambda i, j: (0, j)),
                  pl.BlockSpec((FFN_TF, D_MODEL), lambda i, j: (j, 0)),
                  modc(5), vec, vec],
        out_specs=row,
        out_shape=jax.ShapeDtypeStruct((m, D_MODEL), jnp.float32),
        scratch_shapes=[pltpu.VMEM((tm, D_MODEL), jnp.bfloat16),
                        pltpu.VMEM((tm, D_MODEL), jnp.float32)],
        compiler_params=_params(2),
    )(x, mod, mod, w1, w3, w2, mod, g, b)


def kernel(x, c, ctx, c_ctx, ada_w, ada_b, w_in, w_out, ln_g, ln_b, hy_conv_w, hy_conv_b, hy_w1, hy_b1, hy_w2, hy_b2, hy_w3, hy_b3, hy_freq, hy_bias, rw_mu, rw_w0, rw_w2, rw_a0, rw_a2, rw_k_k, rw_k_a, rw_r_k, rw_gn_g, rw_gn_b, hg_lb_raw, hg_norm_g, ffn_w1, ffn_w3, ffn_w2):
    L, n_ctx = x.shape[1], ctx.shape[1]
    sm = jax.nn.softmax(hg_lb_raw.astype(jnp.float32), axis=1)
    lower_bounds = jnp.cumsum(sm, axis=1) - sm[:, :1]
    log_gamma = jnp.log1p(-jnp.exp2(-5.0 - jnp.arange(RT_HEADS, dtype=jnp.float32)))
    lg = jnp.repeat(log_gamma, RT_HEAD)[None, :]
    cos, sin = rope_tables(n_ctx, L)
    fft_tabs = fft_tables(FFT_N1, 2 * L // FFT_N1)
    ctx_tabs = dense_dft_tables(n_ctx)

    c8 = jnp.zeros((8, D_MODEL), jnp.float32).at[0].set(c[0]).at[1].set(c_ctx)
    xs = jnp.concatenate([x[0], ctx[0]], axis=0)
    n_rows = L + n_ctx
    for l in range(DEPTH):
        with_ctx = l < DEPTH - 1
        mod = ada_modulation(c8, ada_w, l, ada_b[l][None, :])
        w_out_b = w_out[l].astype(jnp.bfloat16)
        w1_b, w3_b, w2_b = (w[l].astype(jnp.bfloat16) for w in (ffn_w1, ffn_w3, ffn_w2))

        z = modulated_projection(xs, mod, w_in, l, L)

        u = hyena_conv3(z, hy_conv_w[l], hy_conv_b[l][None, :], L)
        hy_w = (hy_w1[l], hy_b1[l], hy_w2[l], hy_b2[l], hy_w3[l], hy_b3[l], hy_freq[l])
        filt, colsum = hyena_filter_bank(L, *hy_w)
        y_lat = hyena_long_conv_chain(u, L, hy_bias[l], filt, colsum, fft_tabs)
        if with_ctx:
            filt_c, colsum_c = hyena_filter_bank(n_ctx, *hy_w)
            y_ctx = hyena_ctx(u[L:], filt_c, colsum_c, hy_bias[l], ctx_tabs)
        else:
            y_ctx = jnp.zeros((n_ctx, GROUP_W), jnp.float32)
        y_hy = jnp.concatenate([y_lat, y_ctx], axis=0)

        rw_r, rw_k, rw_v, rw_g, rw_kk, rw_lw, rw_kd, rw_a = rwkv_prep(
            z, rw_mu[l], rw_k_k[l], rw_k_a[l], rw_w0[l], rw_a0[l], rw_w2[l], rw_a2[l], L)
        scans = recurrent_scans(z, (rw_r, rw_v, rw_kk, rw_lw, rw_kd, rw_a), cos, sin, lg,
                                lower_bounds[:, l][:, None, :], n_ctx)

        xs = outproj_deepnorm(y_hy, scans, rw_r, rw_k, rw_v, rw_g, z,
                              rw_r_k[l].reshape(1, GROUP_W), rw_gn_g[l][None, :], rw_gn_b[l][None, :],
                              hg_norm_g[l][None, :], w_out_b, xs, mod, ln_g[l, 0][None, :], ln_b[l, 0][None, :],
                              L, n_rows if with_ctx else L)
        xs = ffn_deepnorm(xs, mod, w1_b, w3_b, w2_b, ln_g[l, 1][None, :], ln_b[l, 1][None, :], L,
                          FFN_TM if with_ctx else FFN_LAST_TM)
    return xs[None]
```

```python
import functools
import math

import jax
import jax.numpy as jnp
from jax import lax
from jax.experimental import pallas as pl
from jax.experimental.pallas import tpu as pltpu

D_MODEL = 2048
DEPTH = 2
GRID_W = 64
N_MIXERS = 4
GROUP_W = D_MODEL // N_MIXERS
HY_ORDER = 2
HY_EMB = 33
HY_FAST_DECAY = 0.3
HY_SLOW_DECAY = 1.5
HY_TARGET = 1e-2
RW_HEAD = 64
RW_LORA = 96
RW_GN_EPS = 64e-5
RT_HEAD = 64
RT_HEADS = GROUP_W // RT_HEAD
ROPE_BASE = 10000.0
HG_HEAD = 128
HG_MIN_GATE = 1e-30
FFN_HIDDEN = 5632
ALPHA = (2 * DEPTH) ** 0.25
LN_EPS = 1e-6
HEAD_NORM_EPS = 1e-6

LANES = 128
CB = 512
COL_HY = 0
COL_RW = 3
COL_RT = 8
COL_HG = 12
P_IN_PAD = 17 * CB
RW_REAL = 4 * GROUP_W + 4 * RW_LORA
PROJ_TM = 1408
FFN_TM = 768
FFN_LAST_TM = 512
OUT_TM = 256
FFN_TF = 512
PREP_ROWS = 256
HALO = 8
VMEM_LIMIT = 56 * 1024 * 1024

HI = lax.Precision.HIGHEST
NN = ((1,), (0,))
NT = ((1,), (1,))
TN = ((0,), (0,))


def _params(n_axes):
    return pltpu.CompilerParams(dimension_semantics=("arbitrary",) * n_axes, vmem_limit_bytes=VMEM_LIMIT)


def _full(a):
    return pl.BlockSpec(a.shape, lambda *_: (0,) * a.ndim)


def _dot(a, b):
    return jnp.dot(a, b, precision=HI, preferred_element_type=jnp.float32)


def _split(x):
    hi = x.astype(jnp.bfloat16)
    lo = (x - hi.astype(jnp.float32)).astype(jnp.bfloat16)
    return hi, lo


def _mm3(a, b, dims=NN):
    d = lambda p, q: lax.dot_general(p, q, (dims, ((), ())), preferred_element_type=jnp.float32)
    return d(a[0], b[0]) + (d(a[0], b[1]) + d(a[1], b[0]))


def _mm1(a, b, dims=NN):
    return lax.dot_general(a, b, (dims, ((), ())), preferred_element_type=jnp.float32)


def _mm2(x, b):
    hi, lo = _split(x)
    d = functools.partial(jnp.dot, preferred_element_type=jnp.float32)
    return d(hi, b) + d(lo, b)


def _ln_rows(x):
    mu = jnp.mean(x, axis=-1, keepdims=True)
    xc = x - mu
    var = jnp.mean(xc * xc, axis=-1, keepdims=True)
    return xc * lax.rsqrt(var + LN_EPS)


def _sigmoid(x):
    return 1.0 / (1.0 + jnp.exp(-x))


def _ctx_rows(tm, n_lat):
    return (pl.program_id(0) * tm + lax.broadcasted_iota(jnp.int32, (tm, 1), 0)) >= n_lat


def _mod_row(ref, is_ctx):
    return jnp.where(is_ctx, ref[1:2, :], ref[0:1, :])


def _block_indicator(width, head, value):
    i = jnp.arange(width) // head
    return jnp.where(i[:, None] == i[None, :], value, 0.0).astype(jnp.bfloat16)


def _ada_kernel(c_ref, w_ref, b_ref, o_ref):
    c = c_ref[...]
    h = c * _sigmoid(c)
    o_ref[...] = jnp.dot(h.astype(jnp.bfloat16), w_ref[...].astype(jnp.bfloat16),
                         preferred_element_type=jnp.float32) + b_ref[...]


def ada_modulation(c8, w, layer, b):
    n = w.shape[2]
    tn = 1024
    return pl.pallas_call(
        _ada_kernel,
        grid=(n // tn,),
        in_specs=[pl.BlockSpec((8, D_MODEL), lambda j: (0, 0)),
                  pl.BlockSpec((None, D_MODEL, tn), lambda j: (layer, 0, j)),
                  pl.BlockSpec((1, tn), lambda j: (0, j))],
        out_specs=pl.BlockSpec((8, tn), lambda j: (0, j)),
        out_shape=jax.ShapeDtypeStruct((8, n), jnp.float32),
        compiler_params=_params(1),
    )(c8, w, b)


W_SUB = CB // LANES
W_PAD_PIECE = (COL_RW * CB + RW_REAL) // LANES


def _proj_kernel(x_ref, sh_ref, sc_ref, *refs, n_lat):
    w_refs, (o_ref, h_ref) = refs[:W_SUB], refs[W_SUB:]
    j = pl.program_id(1)

    @pl.when(j == 0)
    def _():
        is_ctx = _ctx_rows(x_ref.shape[0], n_lat)
        h = _ln_rows(x_ref[...]) * (1.0 + _mod_row(sc_ref, is_ctx)) + _mod_row(sh_ref, is_ctx)
        h_ref[...] = h.astype(jnp.bfloat16)

    pieces = [w_refs[k][...] for k in range(W_SUB)]
    for k in range(W_SUB):
        if (W_PAD_PIECE - k) % W_SUB == 0:
            pieces[k] = jnp.where(j * W_SUB + k == W_PAD_PIECE, 0.0, pieces[k])
    w = jnp.concatenate(pieces, axis=1).astype(jnp.bfloat16)
    o_ref[...] = jnp.dot(h_ref[...], w, preferred_element_type=jnp.float32)


def modulated_projection(x, mod, w, layer, n_lat):
    m = x.shape[0]
    tm = PROJ_TM

    def piece(k):
        def index(i, j):
            p = j * W_SUB + k
            return layer, 0, jnp.where(p < W_PAD_PIECE, p, p - 1)
        return pl.BlockSpec((None, D_MODEL, LANES), index)

    return pl.pallas_call(
        functools.partial(_proj_kernel, n_lat=n_lat),
        grid=(m // tm, P_IN_PAD // CB),
        in_specs=[pl.BlockSpec((tm, D_MODEL), lambda i, j: (i, 0)),
                  pl.BlockSpec((8, D_MODEL), lambda i, j: (0, 0)),
                  pl.BlockSpec((8, D_MODEL), lambda i, j: (0, 1))] + [piece(k) for k in range(W_SUB)],
        out_specs=pl.BlockSpec((tm, CB), lambda i, j: (i, j)),
        out_shape=jax.ShapeDtypeStruct((m, P_IN_PAD), jnp.float32),
        scratch_shapes=[pltpu.VMEM((tm, D_MODEL), jnp.bfloat16)],
        compiler_params=_params(2),
    )(x, mod, mod, *([w] * W_SUB))


def _halo_specs(tp, n_rows, col, width=CB):
    per = tp // HALO
    last = n_rows // HALO - 1
    main = pl.BlockSpec((tp, width), lambda i: (i, col))
    prev = pl.BlockSpec((HALO, width), lambda i: (jnp.maximum(i * per - 1, 0), col))
    nxt = pl.BlockSpec((HALO, width), lambda i: (jnp.minimum((i + 1) * per, last), col))
    return [main, prev, nxt]


def _neighbours(x, prev, nxt, n_lat, n_rows):
    tp = x.shape[0]
    loc = lax.broadcasted_iota(jnp.int32, (tp, 1), 0)
    row = pl.program_id(0) * tp + loc
    xp = jnp.where(loc == 0, prev[HALO - 1:HALO, :], pltpu.roll(x, 1, axis=0))
    xp = jnp.where((row == 0) | (row == n_lat), 0.0, xp)
    xn = jnp.where(loc == tp - 1, nxt[0:1, :], pltpu.roll(x, tp - 1, axis=0))
    xn = jnp.where((row == n_lat - 1) | (row == n_rows - 1), 0.0, xn)
    return xp, xn


def _conv3_kernel(z_ref, zp_ref, zn_ref, w_ref, b_ref, o_ref, *, n_lat, n_rows):
    x = z_ref[...]
    xp, xn = _neighbours(x, zp_ref[...], zn_ref[...], n_lat, n_rows)
    w = w_ref[...]
    o_ref[...] = xp * w[0:1, :] + x * w[1:2, :] + xn * w[2:3, :] + b_ref[...]


def hyena_conv3(z, w, b, n_lat):
    n_rows = z.shape[0]
    tp = PREP_ROWS
    width = w.shape[1]
    return pl.pallas_call(
        functools.partial(_conv3_kernel, n_lat=n_lat, n_rows=n_rows),
        grid=(n_rows // tp,),
        in_specs=_halo_specs(tp, n_rows, COL_HY, width) + [_full(w), _full(b)],
        out_specs=pl.BlockSpec((tp, width), lambda i: (i, 0)),
        out_shape=jax.ShapeDtypeStruct((n_rows, width), jnp.float32),
        compiler_params=_params(1),
    )(z, z, z, w, b)


HY_EMB_PAD = 40
HY_FILTER_ROWS = 512


def hyena_features(L):
    t = jnp.linspace(0.0, 1.0, L, dtype=jnp.float32)[:, None]
    n_bands = (HY_EMB - 1) // 2
    f = jnp.linspace(1e-4, n_bands - 1, n_bands, dtype=jnp.float32)[None, :]
    ang = (2.0 * math.pi / L) * jnp.arange(L, dtype=jnp.float32)[:, None] * f
    z = jnp.concatenate([t, jnp.cos(ang), -jnp.sin(ang)], -1)
    return jnp.pad(z, ((0, 0), (0, HY_EMB_PAD - HY_EMB)))


def _filter_kernel(z_ref, w1_ref, b1_ref, w2_ref, b2_ref, w3_ref, b3_ref, fr_ref, dl_ref, h_ref, s_ref):
    i = pl.program_id(0)
    z = z_ref[...]
    fr = fr_ref[...]
    h = jnp.sin(fr * (_dot(z, w1_ref[...]) + b1_ref[...]))
    h = jnp.sin(fr * (_dot(h, w2_ref[...]) + b2_ref[...]))
    h = _mm1(h.astype(jnp.bfloat16), w3_ref[...].astype(jnp.bfloat16)) + b3_ref[...]
    win = jnp.exp(-z[:, 0:1] * dl_ref[...])
    h = h * jnp.concatenate([win] * (h.shape[1] // win.shape[1]), axis=1)

    @pl.when(i == 0)
    def _():
        s_ref[...] = jnp.zeros_like(s_ref)

    s_ref[...] += jnp.sum(jnp.abs(h), axis=0, keepdims=True)
    row = lax.broadcasted_iota(jnp.int32, h.shape, 0) + i * h.shape[0]
    col = lax.broadcasted_iota(jnp.int32, h.shape, 1)
    neg = (col // GROUP_W) % 2 == 1
    h_ref[...] = jnp.where(neg & (row == 0), 0.0, h)


def hyena_filter_bank(L, w1, b1, w2, b2, w3, b3, freq):
    z = hyena_features(L)
    w1p = jnp.pad(w1, ((0, HY_EMB_PAD - HY_EMB), (0, 0)))
    max_decay = math.log(HY_TARGET) / HY_FAST_DECAY
    min_decay = math.log(HY_TARGET) / HY_SLOW_DECAY
    deltas = jnp.abs(jnp.linspace(min_decay, max_decay, GROUP_W, dtype=jnp.float32))[None, :]
    n = w3.shape[1]
    tr = min(L, HY_FILTER_ROWS)
    args = (z, w1p, b1[None, :], w2, b2[None, :], w3, b3[None, :], freq[None, :], deltas)
    return pl.pallas_call(
        _filter_kernel,
        grid=(L // tr,),
        in_specs=[pl.BlockSpec((tr, HY_EMB_PAD), lambda i: (i, 0))] + [_full(a) for a in args[1:]],
        out_specs=[pl.BlockSpec((tr, n), lambda i: (i, 0)), pl.BlockSpec((1, n), lambda i: (0, 0))],
        out_shape=[jax.ShapeDtypeStruct((L, n), jnp.float32), jax.ShapeDtypeStruct((1, n), jnp.float32)],
        compiler_params=_params(1),
    )(*args)


FFT_NB = 8
FFT_NA = 16
FFT_CB = 512
FFT_N1 = 128


def fft_tables(n1, n2):
    N = n1 * n2
    nk = -(-(n2 // 2 + 1) // FFT_NB) * FFT_NB
    a = jnp.arange(n1, dtype=jnp.float32)[:, None, None]
    k2 = jnp.arange(nk, dtype=jnp.float32)[None, :, None]
    b = jnp.arange(n2 // 2, dtype=jnp.float32)[None, None, :]
    ph = (jnp.mod(a * k2, float(N)) / N + jnp.mod(b * k2, float(n2)) / n2) * (-2.0 * math.pi)
    g = jnp.concatenate([jnp.cos(ph), jnp.sin(ph)], axis=1)
    kk = jnp.arange(nk)
    weight = jnp.where((kk == 0) | (kk == n2 // 2), 1.0, jnp.where(kk < n2 // 2, 2.0, 0.0))
    ginv = jnp.transpose(g, (0, 2, 1)) * (jnp.tile(weight, 2) / N)
    k1 = jnp.arange(n1, dtype=jnp.float32)[:, None]
    aa = jnp.arange(n1, dtype=jnp.float32)[None, :]
    f = jnp.mod(k1 * aa, float(n1)) * (-2.0 * math.pi / n1)
    fr, fi = jnp.cos(f), jnp.sin(f)
    ff = jnp.concatenate([jnp.concatenate([fr, -fi], axis=1), jnp.concatenate([fi, fr], axis=1)], axis=0)
    bf = lambda t: t.astype(jnp.bfloat16)
    return {"g": bf(g), "ginv": bf(ginv), "ff": bf(ff), "fft": bf(ff.T)}


def _stage_a_kernel(u_ref, g_ref, o_ref):
    for j in range(FFT_NA):
        o_ref[j] = _mm1(g_ref[j], u_ref[:, j, :].astype(jnp.bfloat16))


def fft_stage_a(u3, nb, col, ncol, g):
    n1 = u3.shape[1]
    rows = g.shape[1]
    gspec = pl.BlockSpec((FFT_NA, rows, nb), lambda c, i: (i, 0, 0))
    return pl.pallas_call(
        _stage_a_kernel,
        grid=(ncol, n1 // FFT_NA),
        in_specs=[pl.BlockSpec((nb, FFT_NA, CB), lambda c, i: (0, i, col + c)), gspec],
        out_specs=pl.BlockSpec((FFT_NA, rows, CB), lambda c, i: (i, 0, c)),
        out_shape=jax.ShapeDtypeStruct((n1, rows, ncol * CB), jnp.float32),
        compiler_params=_params(2),
    )(u3, g)


def _stage_b_filter_kernel(re_ref, im_ref, f_ref, o_ref):
    ff = f_ref[...]
    for j in range(FFT_NB):
        o_ref[j] = _mm1(ff, jnp.concatenate([re_ref[:, j, :], im_ref[:, j, :]], axis=0).astype(jnp.bfloat16))


def fft_stage_b_filter(t1, ff):
    n1, n2x2, C = t1.shape
    n2 = n2x2 // 2
    blk = lambda off: pl.BlockSpec((n1, FFT_NB, FFT_CB), lambda c, i: (0, i + off, c))
    mat = pl.BlockSpec((2 * n1, 2 * n1), lambda c, i: (0, 0))
    return pl.pallas_call(
        _stage_b_filter_kernel,
        grid=(C // FFT_CB, n2 // FFT_NB),
        in_specs=[blk(0), blk(n2 // FFT_NB), mat],
        out_specs=pl.BlockSpec((FFT_NB, 2 * n1, FFT_CB), lambda c, i: (i, 0, c)),
        out_shape=jax.ShapeDtypeStruct((n2, 2 * n1, C), jnp.float32),
        compiler_params=_params(2),
    )(t1, t1, ff)


def _filter_spectrum(p, q, s, n1):
    return (p[:n1] + q[:n1]) * s, (p[n1:] - q[n1:]) * s


def _stage_b_conv_kernel(re_ref, im_ref, p_ref, q_ref, s_ref, f_ref, ft_ref, ore_ref, oim_ref):
    n1 = re_ref.shape[0]
    s = s_ref[...]
    ff, fft_ = f_ref[...], ft_ref[...]
    bf = lambda t: t.astype(jnp.bfloat16)
    for j in range(FFT_NB):
        x = _mm1(ff, bf(jnp.concatenate([re_ref[:, j, :], im_ref[:, j, :]], axis=0)))
        hr, hi = _filter_spectrum(p_ref[j], q_ref[j], s, n1)
        xr, xi = x[:n1], x[n1:]
        z = _mm1(fft_, bf(jnp.concatenate([xr * hr - xi * hi, xr * hi + xi * hr], axis=0)))
        ore_ref[:, j, :] = z[:n1]
        oim_ref[:, j, :] = z[n1:]


def fft_stage_b_conv(t1, spec, col_p, col_q, inv_norm, ff, fft_):
    n1, n2x2, C = t1.shape
    n2 = n2x2 // 2
    ncb = C // FFT_CB
    blk = lambda off: pl.BlockSpec((n1, FFT_NB, FFT_CB), lambda c, i: (0, i + off, c))
    mat = pl.BlockSpec((2 * n1, 2 * n1), lambda c, i: (0, 0))
    sp = lambda col: pl.BlockSpec((FFT_NB, 2 * n1, FFT_CB), lambda c, i: (i, 0, col * ncb + c))
    return pl.pallas_call(
        _stage_b_conv_kernel,
        grid=(ncb, n2 // FFT_NB),
        in_specs=[blk(0), blk(n2 // FFT_NB), sp(col_p), sp(col_q), pl.BlockSpec((1, FFT_CB), lambda c, i: (0, c)),
                  mat, mat],
        out_specs=[blk(0), blk(0)],
        out_shape=[jax.ShapeDtypeStruct((n1, n2, C), jnp.float32)] * 2,
        compiler_params=_params(2),
    )(t1, t1, spec, spec, inv_norm, ff, fft_)


def _stage_a_inv_kernel(re_ref, im_ref, g_ref, u_ref, gate_ref, bias_ref, o_ref):
    for j in range(FFT_NA):
        y = _mm1(g_ref[j], jnp.concatenate([re_ref[j], im_ref[j]], axis=0).astype(jnp.bfloat16))
        o_ref[:, j, :] = gate_ref[:, j, :] * (y + u_ref[:, j, :] * bias_ref[...])


def fft_stage_a_inv(t2re, t2im, ginv, nb, u3, u_col, gate3, gate_col, bias):
    n1, nk, C = t2re.shape
    tb = pl.BlockSpec((FFT_NA, nk, CB), lambda c, i: (i, 0, c))
    gb = pl.BlockSpec((FFT_NA, nb, 2 * nk), lambda c, i: (i, 0, 0))
    ub = lambda col: pl.BlockSpec((nb, FFT_NA, CB), lambda c, i: (0, i, col + c))
    return pl.pallas_call(
        _stage_a_inv_kernel,
        grid=(C // CB, n1 // FFT_NA),
        in_specs=[tb, tb, gb, ub(u_col), ub(gate_col), pl.BlockSpec((1, CB), lambda c, i: (0, c))],
        out_specs=ub(0),
        out_shape=jax.ShapeDtypeStruct((nb, n1, C), jnp.float32),
        compiler_params=_params(2),
    )(t2re, t2im, ginv, u3, gate3, bias)


def hyena_long_conv_chain(u, L, biases, filt, colsum, tabs):
    C = GROUP_W
    n1 = FFT_N1
    n2 = 2 * L // n1
    nb = n2 // 2
    spec = fft_stage_b_filter(fft_stage_a(filt.reshape(nb, n1, filt.shape[1]), nb, 0, filt.shape[1] // CB, tabs["g"]),
                              tabs["ff"])
    s4 = colsum.reshape(HY_ORDER, 2, C)
    inv_norm = 1.0 / (s4[:, 0] + s4[:, 1])
    u3 = u.reshape(u.shape[0] // n1, n1, u.shape[1])
    y3, y_col = u3, 0
    for n in range(HY_ORDER):
        t1 = fft_stage_a(y3, nb, y_col, 1, tabs["g"])
        t2re, t2im = fft_stage_b_conv(t1, spec, 2 * n, 2 * n + 1, inv_norm[n][None, :], tabs["ff"], tabs["fft"])
        y3 = fft_stage_a_inv(t2re, t2im, tabs["ginv"], nb, y3, y_col, u3, n + 1, biases[n][None, :])
        y_col = 0
    return y3.reshape(L, C)


def dense_dft_tables(n):
    N = 2 * n
    k = jnp.arange(N, dtype=jnp.float32)[:, None]
    t = jnp.arange(n, dtype=jnp.float32)[None, :]
    ph = jnp.mod(k * t, float(N)) * (2.0 * math.pi / N)
    fd = jnp.concatenate([jnp.cos(ph), -jnp.sin(ph)], axis=0)
    return fd.astype(jnp.bfloat16), (fd.T / N).astype(jnp.bfloat16)


def _hyena_ctx_kernel(u_ref, h_ref, s_ref, bias_ref, fd_ref, ft_ref, o_ref):
    C = GROUP_W
    fd, ft = fd_ref[...], ft_ref[...]
    K = fd.shape[0] // 2
    bf = lambda t: t.astype(jnp.bfloat16)
    hs = _mm1(fd, bf(h_ref[...]))
    s = s_ref[...]
    y = u_ref[:, 0:C]
    for n in range(HY_ORDER):
        cp, cq = 2 * n * C, (2 * n + 1) * C
        inv = 1.0 / (s[:, cp:cp + C] + s[:, cq:cq + C])
        hr, hi = _filter_spectrum(hs[:, cp:cp + C], hs[:, cq:cq + C], inv, K)
        x = _mm1(fd, bf(y))
        xr, xi = x[:K], x[K:]
        conv = _mm1(ft, bf(jnp.concatenate([xr * hr - xi * hi, xr * hi + xi * hr], axis=0)))
        y = u_ref[:, (n + 1) * C:(n + 2) * C] * (conv + y * bias_ref[n:n + 1, :])
    o_ref[...] = y


def hyena_ctx(u, filt, colsum, biases, tabs):
    args = (u, filt, colsum, biases, tabs[0], tabs[1])
    return pl.pallas_call(
        _hyena_ctx_kernel,
        in_specs=[_full(a) for a in args],
        out_specs=pl.BlockSpec((u.shape[0], GROUP_W), lambda: (0, 0)),
        out_shape=jax.ShapeDtypeStruct((u.shape[0], GROUP_W), jnp.float32),
        compiler_params=pltpu.CompilerParams(vmem_limit_bytes=VMEM_LIMIT),
    )(*args)


SCAN_T = 64
GROUP_LANES = 256
RW_SUB = 16
HG_SUB = 16
LOG2_E = 1.4426950408889634


def _chunk_index(d, i, n_ctx, n_all):
    fwd = jnp.where(i < n_ctx, n_all - n_ctx + i, i - n_ctx)
    return jnp.where(d == 0, fwd, n_all - 1 - i)


def _stacking(T, S, nh, head, sign):
    G = nh * head
    nb = T // S
    n = nh * T
    rr = lax.broadcasted_iota(jnp.int32, (n, G), 0)
    same = ((rr // S) % nh) == (lax.broadcasted_iota(jnp.int32, (n, G), 1) // head)

    def bd(x):
        pieces = []
        for i in range(nb):
            pieces += [x[i * S:(i + 1) * S]] * nh
        return jnp.where(same, jnp.concatenate(pieces, axis=0), 0.0)

    def collapse(o):
        outs = []
        for i in range(nb):
            acc = o[i * nh * S:i * nh * S + S]
            for h in range(1, nh):
                acc = acc + o[i * nh * S + h * S:i * nh * S + (h + 1) * S]
            outs.append(acc)
        return jnp.concatenate(outs, axis=0)

    rt = lax.broadcasted_iota(jnp.int32, (n, n), 0)
    cs = lax.broadcasted_iota(jnp.int32, (n, n), 1)
    t_r = (rt // (nh * S)) * S + rt % S
    t_c = (cs // (nh * S)) * S + cs % S
    same_h = ((rt // S) % nh) == ((cs // S) % nh)
    before = same_h & ((t_r - t_c) * sign > 0)
    return bd, collapse, before, rt, cs, (t_r, t_c, same_h)


def _softplus(x):
    return jnp.maximum(x, 0.0) + jnp.log(1.0 + jnp.exp(-jnp.abs(x)))


def _rwkv_prep_kernel(*refs, n_lat, n_rows):
    zrefs, rest = refs[:15], refs[15:]
    (mu_ref, kk_w_ref, ka_ref, w0_ref, a0_ref, w2h_ref, w2l_ref, a2h_ref, a2l_ref, ones_ref,
     r_ref, k_ref, v_ref, g_ref, kk_ref, lw_ref, kd_ref, a_ref) = rest
    slabs = []
    for c in range(5):
        x = zrefs[3 * c][...]
        xp, xn = _neighbours(x, zrefs[3 * c + 1][...], zrefs[3 * c + 2][...], n_lat, n_rows)
        slabs.append(x + (0.5 * (xp + xn) - x) * mu_ref[:, c * CB:(c + 1) * CB])
    r, k, v, g, lora = slabs
    r_ref[...], k_ref[...], v_ref[...], g_ref[...] = r, k, v, g
    kk = k * kk_w_ref[...]
    ss = _mm2(kk * kk, ones_ref[...])
    kk_ref[...] = kk * lax.rsqrt(jnp.maximum(ss, 1e-24))
    lora_t = _split(jnp.tanh(lora))
    lora_s = _split(lora)
    for d in range(2):
        w_log = -_softplus(-(w0_ref[d:d + 1, :] + _mm3(lora_t, (w2h_ref[d], w2l_ref[d])))) - 0.5
        lw_ref[d] = -jnp.exp(w_log)
        a = _sigmoid(a0_ref[d:d + 1, :] + _mm3(lora_s, (a2h_ref[d], a2l_ref[d])))
        a_ref[d] = a
        kd_ref[d] = k * (1.0 + (a - 1.0) * ka_ref[...])


def rwkv_prep(z, mu, k_k, k_a, w0, a0, w2, a2, n_lat):
    n_rows = z.shape[0]
    tp = PREP_ROWS
    mu_p = jnp.pad(mu, (0, 5 * CB - RW_REAL))[None, :]
    w2p = jnp.zeros((2, CB, GROUP_W), jnp.float32)
    a2p = jnp.zeros((2, CB, GROUP_W), jnp.float32)
    for d in range(2):
        w2p = w2p.at[d, d * RW_LORA:(d + 1) * RW_LORA].set(w2[d])
        a2p = a2p.at[d, (2 + d) * RW_LORA:(3 + d) * RW_LORA].set(a2[d])
    w2s, a2s = _split(w2p), _split(a2p)
    small = (mu_p, k_k[None, :], k_a[None, :], w0, a0, w2s[0], w2s[1], a2s[0], a2s[1],
             _block_indicator(GROUP_W, RW_HEAD, 1.0))
    zspecs = []
    for c in range(5):
        zspecs += _halo_specs(tp, n_rows, COL_RW + c)
    one = pl.BlockSpec((tp, CB), lambda i: (i, 0))
    two = pl.BlockSpec((2, tp, CB), lambda i: (0, i, 0))
    s1 = jax.ShapeDtypeStruct((n_rows, GROUP_W), jnp.float32)
    s2 = jax.ShapeDtypeStruct((2, n_rows, GROUP_W), jnp.float32)
    return pl.pallas_call(
        functools.partial(_rwkv_prep_kernel, n_lat=n_lat, n_rows=n_rows),
        grid=(n_rows // tp,),
        in_specs=zspecs + [_full(a) for a in small],
        out_specs=[one] * 5 + [two] * 3,
        out_shape=[s1] * 5 + [s2] * 3,
        compiler_params=_params(1),
    )(*([z] * 15), *small)


def _rwkv_scan_body(refs, head):
    in_f, in_b, (of_ref, ob_ref, ht_ref) = refs[:6], refs[6:12], refs[12:]
    T, G, S = SCAN_T, GROUP_LANES, RW_SUB
    nh = G // head
    nb = T // S
    n = nh * T

    @pl.when(pl.program_id(0) == 0)
    def _():
        ht_ref[...] = jnp.zeros_like(ht_ref)

    ti = lax.broadcasted_iota(jnp.int32, (T, T), 0)
    si = lax.broadcasted_iota(jnp.int32, (T, T), 1)
    bf = lambda t: t.astype(jnp.bfloat16)
    each = lambda f, *seqs: [f(*xs) for xs in zip(*seqs)]
    n_grp = in_f[0].shape[1] // G
    lanes = [slice(grp * G, (grp + 1) * G) for grp in range(n_grp)]

    r, v, kk, lw, k, a, tri, before, incl, dest = ([] for _ in range(10))
    for (r_ref, v_ref, kk_ref, lw_ref, k_ref, a_ref), o_ref, sign in ((in_f, of_ref, 1), (in_b, ob_ref, -1)):
        bd, collapse, before_d, rt, cs, (t_r, t_c, same_h) = _stacking(T, S, nh, head, sign)
        for ls in lanes:
            r.append(r_ref[:, ls]), v.append(v_ref[:, ls]), kk.append(kk_ref[:, ls])
            lw.append(lw_ref[0, :, ls]), k.append(k_ref[0, :, ls]), a.append(a_ref[0, :, ls])
            tri.append(jnp.where((ti - si) * sign >= 0, 1.0, 0.0))
            before.append(before_d), incl.append(before_d | (rt == cs)), dest.append((o_ref, ls))
    eye = jnp.where(rt == cs, 1.0, 0.0)
    ht = [ht_ref[ch] for ch in range(len(r))]

    c = each(lambda tr, t: _dot(tr, t), tri, lw)
    ctot = each(lambda t: jnp.sum(t, axis=0, keepdims=True), lw)
    beta = each(lambda p, q: p * q, kk, a)
    einv = each(lambda t: jnp.exp(-t), c)
    lhs = each(lambda kk_, r_, c_, lw_: bf(jnp.concatenate([bd(-kk_ * jnp.exp(c_ - lw_)), bd(r_ * jnp.exp(c_))], axis=0)),
               kk, r, c, lw)
    rhs = each(lambda k_, b_, e_: bf(jnp.concatenate([bd(k_ * e_), bd(b_ * e_)], axis=0)), k, beta, einv)
    m = each(lambda p, q: _mm1(p, q, NT), lhs, rhs)
    g = each(lambda p, h: _mm1(p, bf(h), NT), lhs, ht)
    v_b = each(lambda t: bf(bd(t)), v)
    x = each(lambda g_, m_, v_, bm: g_[:n] + _mm1(bf(jnp.where(bm, m_[:n, :n], 0.0)), v_), g, m, v_b, before)
    lb = each(lambda m_, bm: jnp.where(bm, m_[:n, n:], 0.0), m, before)
    half = lambda s: same_h & ((t_r // (2 * s)) == (t_c // (2 * s))) & ((t_r // s) != (t_c // s))
    dmat = each(lambda t: eye + jnp.where(half(1), t, 0.0), lb)
    s = 2
    while s < T:
        coupling = half(s)
        dmat = each(lambda d_, t: d_ + _mm1(bf(_mm1(bf(d_), bf(jnp.where(coupling, t, 0.0)))), bf(d_)), dmat, lb)
        s *= 2
    u_b = each(lambda d_, t: bf(_mm1(bf(d_), bf(t))), dmat, x)
    o = each(lambda g_, m_, v_, u_, im: (g_[n:] + _mm1(bf(jnp.where(im, m_[n:, :n], 0.0)), v_))
             + _mm1(bf(jnp.where(im, m_[n:, n:], 0.0)), u_), g, m, v_b, u_b, incl)
    efin = each(lambda ct, c_: jnp.exp(ct - c_), ctot, c)
    ht_new = each(lambda h, ct, v_, u_, k_, b_, e_: (h * jnp.exp(ct) + _mm1(v_, bf(bd(k_ * e_)), TN))
                  + _mm1(u_, bf(bd(b_ * e_)), TN), ht, ctot, v_b, u_b, k, beta, efin)
    for ch, (o_ref, ls) in enumerate(dest):
        o_ref[:, ls] = collapse(o[ch])
        ht_ref[ch] = ht_new[ch]


def rope_tables(n_ctx, L):
    d_axis = RT_HEAD // 2
    half = d_axis // 2
    inv = ROPE_BASE ** (-jnp.arange(0, d_axis, 2, dtype=jnp.float32) / d_axis)
    t = jnp.arange(L)
    pos = jnp.stack([(t // GRID_W).astype(jnp.float32), (t % GRID_W).astype(jnp.float32)], axis=1)
    j = jnp.arange(RT_HEAD)
    ang = pos[:, j // d_axis] * inv[j % half][None, :]
    sgn = jnp.where((j % d_axis) < half, -1.0, 1.0)[None, :]
    cos = jnp.concatenate([jnp.cos(ang), jnp.ones((n_ctx, RT_HEAD), jnp.float32)], axis=0)
    sin = jnp.concatenate([jnp.sin(ang) * sgn, jnp.zeros((n_ctx, RT_HEAD), jnp.float32)], axis=0)
    return jnp.tile(cos, (1, RT_HEADS)), jnp.tile(sin, (1, RT_HEADS))


def _rotate(x, cos, sin):
    G = x.shape[1]
    half = RT_HEAD // 4
    lane = lax.broadcasted_iota(jnp.int32, x.shape, 1)
    partner = jnp.where((lane % (2 * half)) < half, pltpu.roll(x, G - half, axis=1), pltpu.roll(x, half, axis=1))
    return x * cos + partner * sin


def _retention_scan_body(q_ref, k_ref, v_ref, cos_ref, sin_ref, lg_ref, o_ref, st_ref, *, head, d):
    T, G = SCAN_T, GROUP_LANES
    nh = G // head
    sign = 1 - 2 * d

    t = lax.broadcasted_iota(jnp.int32, (T, 1), 0)
    pos = (t + d * (T - 1 - 2 * t) + 1).astype(jnp.float32)
    bd, collapse, before, rt, cs, _ = _stacking(T, T, nh, head, sign)
    incl = before | (rt == cs)
    bf = lambda x: x.astype(jnp.bfloat16)

    groups = range(q_ref.shape[1] // G)
    lanes = [slice(grp * G, (grp + 1) * G) for grp in groups]
    each = lambda f, *seqs: [f(*xs) for xs in zip(*seqs)]
    st = [st_ref[grp] for grp in groups]
    cos, sin, lg = ([ref[:, ls] for ls in lanes] for ref in (cos_ref, sin_ref, lg_ref))
    q = each(_rotate, [q_ref[:, ls] for ls in lanes], cos, sin)
    k = each(lambda x, c_, s_: _rotate(x, c_, s_) * (head ** -0.5), [k_ref[:, ls] for ls in lanes], cos, sin)
    c = each(lambda lg_: pos * lg_, lg)
    q_b = each(lambda q_, c_: bf(bd(q_ * jnp.exp(c_))), q, c)
    kt_b = each(lambda k_, c_: bf(bd(k_ * jnp.exp(-c_))), k, c)
    v_b = [bf(bd(v_ref[:, ls])) for ls in lanes]
    scores = each(lambda q_, k_: bf(jnp.where(incl, _mm1(q_, k_, NT), 0.0)), q_b, kt_b)
    inter = each(lambda q_, s_: _mm1(q_, bf(s_), NT), q_b, st)
    o = each(lambda g_, s_, v_: g_ + _mm1(s_, v_), inter, scores, v_b)
    st_new = each(lambda s_, lg_, v_, k_, c_: s_ * jnp.exp(float(T) * lg_)
                  + _mm1(v_, bf(bd(k_ * jnp.exp(float(T) * lg_ - c_))), TN), st, lg, v_b, k, c)
    for grp in groups:
        o_ref[:, lanes[grp]] = collapse(o[grp])
        st_ref[grp] = st_new[grp]


def _gla_scan_body(q_ref, f_ref, i_ref, lb_ref, o_ref, st_ref, *, head, d):
    T, S = SCAN_T, HG_SUB
    W = q_ref.shape[-1]
    nh = W // head
    sign = 1 - 2 * d

    ti = lax.broadcasted_iota(jnp.int32, (S, S), 0)
    si = lax.broadcasted_iota(jnp.int32, (S, S), 1)
    tri_incl = jnp.where((ti - si) * sign >= 0, 1.0, 0.0)
    row = lax.broadcasted_iota(jnp.int32, (S, 1), 0)
    lb = lb_ref[0]

    states = [st_ref[h] for h in range(nh)]
    for j in range(T // S):
        jb = j + d * (T // S - 1 - 2 * j)
        rows = pl.ds(jb * S, S)
        q = q_ref[rows, :]
        q = q * _sigmoid(q)
        v = i_ref[rows, :]
        gate = lb + (1.0 - lb) * _sigmoid(f_ref[rows, :])
        lf = jnp.log(jnp.maximum(gate, HG_MIN_GATE))
        k = 1.0 - gate
        b = _dot(tri_incl, lf)
        btot = jnp.sum(lf, axis=0, keepdims=True)
        qe = q * jnp.exp(b)
        ke = k * jnp.exp(btot - b)
        outs = []
        for h in range(nh):
            ls = slice(h * head, (h + 1) * head)
            qh, kh, vh = q[:, ls], k[:, ls], v[:, ls]
            bh = b[:, ls] * LOG2_E
            st = states[h]
            o = _mm1(qe[:, ls].astype(jnp.bfloat16), st.astype(jnp.bfloat16), NT)
            for s in range(S):
                e = jnp.exp2(bh - bh[s:s + 1, :])
                a_s = jnp.sum(qh * kh[s:s + 1, :] * e, axis=-1, keepdims=True)
                a_s = jnp.where((row - s) * sign >= 0, a_s, 0.0)
                o = o + a_s * vh[s:s + 1, :]
            outs.append(o)
            states[h] = st * jnp.exp(btot[:, ls]) + _mm1(vh.astype(jnp.bfloat16), ke[:, ls].astype(jnp.bfloat16), TN)
        o_ref[rows, :] = jnp.concatenate(outs, axis=1)
    for h in range(nh):
        st_ref[h] = states[h]


N_RW, N_RT, N_HG = 6, 6, 4


def _scans_kernel(*refs):
    n_in = 2 * (N_RW + N_RT + N_HG)
    ins, (rw_of, rw_ob, rt_of, rt_ob, hg_of, hg_ob, rw_st, rt_st, hg_st) = refs[:n_in], refs[n_in:]
    rw_in, rt_in, hg_in = ins[:2 * N_RW], ins[2 * N_RW:2 * (N_RW + N_RT)], ins[2 * (N_RW + N_RT):]

    @pl.when(pl.program_id(0) == 0)
    def _():
        rt_st[...] = jnp.zeros_like(rt_st)
        hg_st[...] = jnp.zeros_like(hg_st)

    n_rt, n_hg = rt_st.shape[0] // 2, hg_st.shape[0] // 2
    _rwkv_scan_body(list(rw_in) + [rw_of, rw_ob, rw_st], RW_HEAD)
    for d, (rt_o, hg_o) in enumerate(((rt_of, hg_of), (rt_ob, hg_ob))):
        _gla_scan_body(*hg_in[d * N_HG:(d + 1) * N_HG], hg_o, hg_st.at[d * n_hg:(d + 1) * n_hg], head=HG_HEAD, d=d)
        _retention_scan_body(*rt_in[d * N_RT:(d + 1) * N_RT], rt_o, rt_st.at[d * n_rt:(d + 1) * n_rt],
                             head=RT_HEAD, d=d)


def recurrent_scans(z, rw, cos, sin, lg, lb2, n_ctx_rows):
    N = z.shape[0]
    W = GROUP_W
    T = SCAN_T
    n_all, n_ctx = N // T, n_ctx_rows // T
    chunk = lambda d: (lambda i: (_chunk_index(d, i, n_ctx, n_all), 0))
    row = lambda d: pl.BlockSpec((T, W), chunk(d))
    per_dir = lambda d: pl.BlockSpec((1, T, W), lambda i: (d, _chunk_index(d, i, n_ctx, n_all), 0))
    zc = lambda d, col: pl.BlockSpec((T, CB), lambda i: (_chunk_index(d, i, n_ctx, n_all), col))
    rw_specs = lambda d: [row(d)] * 3 + [per_dir(d)] * 3
    rt_specs = lambda d: [zc(d, COL_RT), zc(d, COL_RT + 1), zc(d, COL_RT + 2), row(d), row(d),
                          pl.BlockSpec((1, W), lambda i: (0, 0))]
    hg_specs = lambda d: [zc(d, COL_HG), zc(d, COL_HG + 1 + d), zc(d, COL_HG + 3),
                          pl.BlockSpec((1, 1, W), lambda i: (d, 0, 0))]
    shape = jax.ShapeDtypeStruct((N, W), jnp.float32)
    return pl.pallas_call(
        _scans_kernel,
        grid=(n_all,),
        in_specs=rw_specs(0) + rw_specs(1) + rt_specs(0) + rt_specs(1) + hg_specs(0) + hg_specs(1),
        out_specs=[row(0), row(1)] * 3,
        out_shape=[shape] * 6,
        scratch_shapes=[pltpu.VMEM((2 * (W // GROUP_LANES), GROUP_LANES, GROUP_LANES), jnp.float32),
                        pltpu.VMEM((2 * (W // GROUP_LANES), GROUP_LANES, GROUP_LANES), jnp.float32),
                        pltpu.VMEM((2 * (W // HG_HEAD), HG_HEAD, HG_HEAD), jnp.float32)],
        compiler_params=_params(1),
    )(*(list(rw) * 2), *([z, z, z, cos, sin, lg] * 2), *([z, z, z, lb2] * 2))


def _outproj_kernel(hy_ref, rwof_ref, rwob_ref, r_ref, k_ref, v_ref, rwg_ref, rtof_ref, rtob_ref, rtg_ref,
                    hgof_ref, hgob_ref, hgg_ref,
                    rk_ref, gng_ref, gnb_ref, hgn_ref, avg64_ref, avg128_ref,
                    w_ref, x_ref, gate_ref, g_ref, b_ref, o_ref, *, n_lat):
    avg64, avg128 = avg64_ref[...], avg128_ref[...]

    def head_norm(o, avg, eps, centre):
        if centre:
            o = o - _mm2(o, avg)
        return o * lax.rsqrt(_mm2(o * o, avg) + eps)

    silu = lambda t: t * _sigmoid(t)
    y_rw = head_norm(rwof_ref[...] + rwob_ref[...], avg64, RW_GN_EPS, True) * gng_ref[...] + gnb_ref[...]
    bonus = (float(RW_HEAD) * _mm2(r_ref[...] * k_ref[...] * rk_ref[...], avg64)) * v_ref[...]
    y_rw = (y_rw + bonus) * _sigmoid(rwg_ref[...])
    y_rt = head_norm(rtof_ref[...] + rtob_ref[...], avg64, HEAD_NORM_EPS, True) * silu(rtg_ref[...])
    y_hg = head_norm(hgof_ref[...] + hgob_ref[...], avg128, HEAD_NORM_EPS, False) * hgn_ref[...] * silu(hgg_ref[...])
    y = None
    for m, ym in enumerate((hy_ref[...], y_rw, y_rt, y_hg)):
        part = jnp.dot(ym.astype(jnp.bfloat16), w_ref[m * GROUP_W:(m + 1) * GROUP_W, :],
                       preferred_element_type=jnp.float32)
        y = part if y is None else y + part
    is_ctx = _ctx_rows(x_ref.shape[0], n_lat)
    r = ALPHA * x_ref[...] + _mod_row(gate_ref, is_ctx) * y
    o_ref[...] = _ln_rows(r) * g_ref[...] + b_ref[...]


def outproj_deepnorm(y_hy, scans, rw_r, rw_k, rw_v, rw_g, z, r_k, gn_g, gn_b, hg_norm_g,
                     w_bf16, x, mod, g, b, n_lat, m):
    tm = OUT_TM
    one = pl.BlockSpec((tm, GROUP_W), lambda i: (i, 0))
    zc = lambda col: pl.BlockSpec((tm, CB), lambda i: (i, col))
    row = pl.BlockSpec((tm, D_MODEL), lambda i: (i, 0))
    small = (r_k, gn_g, gn_b, hg_norm_g, _block_indicator(GROUP_W, RW_HEAD, 1.0 / RW_HEAD),
             _block_indicator(GROUP_W, HG_HEAD, 1.0 / HG_HEAD))
    return pl.pallas_call(
        functools.partial(_outproj_kernel, n_lat=n_lat),
        grid=(m // tm,),
        in_specs=[one] * 9 + [zc(COL_RT + 3), one, one, zc(COL_HG + 4)]
        + [_full(a) for a in small]
        + [_full(w_bf16), row, pl.BlockSpec((8, D_MODEL), lambda i: (0, 2)), _full(g), _full(b)],
        out_specs=row,
        out_shape=jax.ShapeDtypeStruct((m, D_MODEL), jnp.float32),
        compiler_params=_params(1),
    )(y_hy, scans[0], scans[1], rw_r, rw_k, rw_v, rw_g, scans[2], scans[3], z, scans[4], scans[5], z,
      *small, w_bf16, x, mod, g, b)


def _ffn_kernel(x_ref, sh_ref, sc_ref, w1_ref, w3_ref, w2_ref, gate_ref, g_ref, b_ref, o_ref, h_ref, acc_ref, *, n_lat):
    j = pl.program_id(1)

    @pl.when(j == 0)
    def _():
        is_ctx = _ctx_rows(x_ref.shape[0], n_lat)
        h = _ln_rows(x_ref[...]) * (1.0 + _mod_row(sc_ref, is_ctx)) + _mod_row(sh_ref, is_ctx)
        h_ref[...] = h.astype(jnp.bfloat16)
        acc_ref[...] = jnp.zeros_like(acc_ref)

    h = h_ref[...]
    a = jnp.dot(h, w1_ref[...], preferred_element_type=jnp.float32)
    u = jnp.dot(h, w3_ref[...], preferred_element_type=jnp.float32)
    s = (a * _sigmoid(a) * u).astype(jnp.bfloat16)
    acc_ref[...] += jnp.dot(s, w2_ref[...], preferred_element_type=jnp.float32)

    @pl.when(j == pl.num_programs(1) - 1)
    def _():
        is_ctx = _ctx_rows(x_ref.shape[0], n_lat)
        r = ALPHA * x_ref[...] + _mod_row(gate_ref, is_ctx) * acc_ref[...]
        o_ref[...] = _ln_rows(r) * g_ref[...] + b_ref[...]


def ffn_deepnorm(x, mod, w1, w3, w2, g, b, n_lat, tm):
    m = x.shape[0]
    row = pl.BlockSpec((tm, D_MODEL), lambda i, j: (i, 0))
    modc = lambda c: pl.BlockSpec((8, D_MODEL), lambda i, j: (0, c))
    vec = pl.BlockSpec((1, D_MODEL), lambda i, j: (0, 0))
    return pl.pallas_call(
        functools.partial(_ffn_kernel, n_lat=n_lat),
        grid=(m // tm, FFN_HIDDEN // FFN_TF),
        in_specs=[row, modc(3), modc(4),
                  pl.BlockSpec((D_MODEL, FFN_TF), lambda i, j: (0, j)),
                  pl.BlockSpec((D_MODEL, FFN_TF), lambda i, j: (0, j)),
                  pl.BlockSpec((FFN_TF, D_MODEL), lambda i, j: (j, 0)),
                  modc(5), vec, vec],
        out_specs=row,
        out_shape=jax.ShapeDtypeStruct((m, D_MODEL), jnp.float32),
        scratch_shapes=[pltpu.VMEM((tm, D_MODEL), jnp.bfloat16),
                        pltpu.VMEM((tm, D_MODEL), jnp.float32)],
        compiler_params=_params(2),
    )(x, mod, mod, w1, w3, w2, mod, g, b)


def kernel(x, c, ctx, c_ctx, ada_w, ada_b, w_in, w_out, ln_g, ln_b, hy_conv_w, hy_conv_b, hy_w1, hy_b1, hy_w2, hy_b2, hy_w3, hy_b3, hy_freq, hy_bias, rw_mu, rw_w0, rw_w2, rw_a0, rw_a2, rw_k_k, rw_k_a, rw_r_k, rw_gn_g, rw_gn_b, hg_lb_raw, hg_norm_g, ffn_w1, ffn_w3, ffn_w2):
    L, n_ctx = x.shape[1], ctx.shape[1]
    sm = jax.nn.softmax(hg_lb_raw.astype(jnp.float32), axis=1)
    lower_bounds = jnp.cumsum(sm, axis=1) - sm[:, :1]
    log_gamma = jnp.log1p(-jnp.exp2(-5.0 - jnp.arange(RT_HEADS, dtype=jnp.float32)))
    lg = jnp.repeat(log_gamma, RT_HEAD)[None, :]
    cos, sin = rope_tables(n_ctx, L)
    fft_tabs = fft_tables(FFT_N1, 2 * L // FFT_N1)
    ctx_tabs = dense_dft_tables(n_ctx)

    c8 = jnp.zeros((8, D_MODEL), jnp.float32).at[0].set(c[0]).at[1].set(c_ctx)
    xs = jnp.concatenate([x[0], ctx[0]], axis=0)
    n_rows = L + n_ctx
    for l in range(DEPTH):
        with_ctx = l < DEPTH - 1
        mod = ada_modulation(c8, ada_w, l, ada_b[l][None, :])
        w_out_b = w_out[l].astype(jnp.bfloat16)
        w1_b, w3_b, w2_b = (w[l].astype(jnp.bfloat16) for w in (ffn_w1, ffn_w3, ffn_w2))

        z = modulated_projection(xs, mod, w_in, l, L)

        u = hyena_conv3(z, hy_conv_w[l], hy_conv_b[l][None, :], L)
        hy_w = (hy_w1[l], hy_b1[l], hy_w2[l], hy_b2[l], hy_w3[l], hy_b3[l], hy_freq[l])
        filt, colsum = hyena_filter_bank(L, *hy_w)
        y_lat = hyena_long_conv_chain(u, L, hy_bias[l], filt, colsum, fft_tabs)
        if with_ctx:
            filt_c, colsum_c = hyena_filter_bank(n_ctx, *hy_w)
            y_ctx = hyena_ctx(u[L:], filt_c, colsum_c, hy_bias[l], ctx_tabs)
        else:
            y_ctx = jnp.zeros((n_ctx, GROUP_W), jnp.float32)
        y_hy = jnp.concatenate([y_lat, y_ctx], axis=0)

        rw_r, rw_k, rw_v, rw_g, rw_kk, rw_lw, rw_kd, rw_a = rwkv_prep(
            z, rw_mu[l], rw_k_k[l], rw_k_a[l], rw_w0[l], rw_a0[l], rw_w2[l], rw_a2[l], L)
        scans = recurrent_scans(z, (rw_r, rw_v, rw_kk, rw_lw, rw_kd, rw_a), cos, sin, lg,
                                lower_bounds[:, l][:, None, :], n_ctx)

        xs = outproj_deepnorm(y_hy, scans, rw_r, rw_k, rw_v, rw_g, z,
                              rw_r_k[l].reshape(1, GROUP_W), rw_gn_g[l][None, :], rw_gn_b[l][None, :],
                              hg_norm_g[l][None, :], w_out_b, xs, mod, ln_g[l, 0][None, :], ln_b[l, 0][None, :],
                              L, n_rows if with_ctx else L)
        xs = ffn_deepnorm(xs, mod, w1_b, w3_b, w2_b, ln_g[l, 1][None, :], ln_b[l, 1][None, :], L,
                          FFN_TM if with_ctx else FFN_LAST_TM)
    return xs[None]
```

```python
import functools
import math

import jax
import jax.numpy as jnp
from jax import lax
from jax.experimental import pallas as pl
from jax.experimental.pallas import tpu as pltpu

D_MODEL = 2048
DEPTH = 2
GRID_W = 64
N_MIXERS = 4
GROUP_W = D_MODEL // N_MIXERS
HY_ORDER = 2
HY_EMB = 33
HY_FAST_DECAY = 0.3
HY_SLOW_DECAY = 1.5
HY_TARGET = 1e-2
RW_HEAD = 64
RW_LORA = 96
RW_GN_EPS = 64e-5
RT_HEAD = 64
RT_HEADS = GROUP_W // RT_HEAD
ROPE_BASE = 10000.0
HG_HEAD = 128
HG_MIN_GATE = 1e-30
FFN_HIDDEN = 5632
ALPHA = (2 * DEPTH) ** 0.25
LN_EPS = 1e-6
HEAD_NORM_EPS = 1e-6

LANES = 128
CB = 512
COL_HY = 0
COL_RW = 3
COL_RT = 8
COL_HG = 12
P_IN_PAD = 17 * CB
RW_REAL = 4 * GROUP_W + 4 * RW_LORA
PROJ_TM = 1408
FFN_TM = 768
FFN_LAST_TM = 512
OUT_TM = 256
FFN_TF = 512
PREP_ROWS = 256
HALO = 8
VMEM_LIMIT = 56 * 1024 * 1024

HI = lax.Precision.HIGHEST
NN = ((1,), (0,))
NT = ((1,), (1,))
TN = ((0,), (0,))


def _params(n_axes):
    return pltpu.CompilerParams(dimension_semantics=("arbitrary",) * n_axes, vmem_limit_bytes=VMEM_LIMIT)


def _full(a):
    return pl.BlockSpec(a.shape, lambda *_: (0,) * a.ndim)


def _dot(a, b):
    return jnp.dot(a, b, precision=HI, preferred_element_type=jnp.float32)


def _split(x):
    hi = x.astype(jnp.bfloat16)
    lo = (x - hi.astype(jnp.float32)).astype(jnp.bfloat16)
    return hi, lo


def _mm3(a, b, dims=NN):
    d = lambda p, q: lax.dot_general(p, q, (dims, ((), ())), preferred_element_type=jnp.float32)
    return d(a[0], b[0]) + (d(a[0], b[1]) + d(a[1], b[0]))


def _mm1(a, b, dims=NN):
    return lax.dot_general(a, b, (dims, ((), ())), preferred_element_type=jnp.float32)


def _mm2(x, b):
    hi, lo = _split(x)
    d = functools.partial(jnp.dot, preferred_element_type=jnp.float32)
    return d(hi, b) + d(lo, b)


def _ln_rows(x):
    mu = jnp.mean(x, axis=-1, keepdims=True)
    xc = x - mu
    var = jnp.mean(xc * xc, axis=-1, keepdims=True)
    return xc * lax.rsqrt(var + LN_EPS)


def _sigmoid(x):
    return 1.0 / (1.0 + jnp.exp(-x))


def _ctx_rows(tm, n_lat):
    return (pl.program_id(0) * tm + lax.broadcasted_iota(jnp.int32, (tm, 1), 0)) >= n_lat


def _mod_row(ref, is_ctx):
    return jnp.where(is_ctx, ref[1:2, :], ref[0:1, :])


def _block_indicator(width, head, value):
    i = jnp.arange(width) // head
    return jnp.where(i[:, None] == i[None, :], value, 0.0).astype(jnp.bfloat16)


def _ada_kernel(c_ref, w_ref, b_ref, o_ref):
    c = c_ref[...]
    h = c * _sigmoid(c)
    o_ref[...] = jnp.dot(h.astype(jnp.bfloat16), w_ref[...].astype(jnp.bfloat16),
                         preferred_element_type=jnp.float32) + b_ref[...]


def ada_modulation(c8, w, layer, b):
    n = w.shape[2]
    tn = 1024
    return pl.pallas_call(
        _ada_kernel,
        grid=(n // tn,),
        in_specs=[pl.BlockSpec((8, D_MODEL), lambda j: (0, 0)),
                  pl.BlockSpec((None, D_MODEL, tn), lambda j: (layer, 0, j)),
                  pl.BlockSpec((1, tn), lambda j: (0, j))],
        out_specs=pl.BlockSpec((8, tn), lambda j: (0, j)),
        out_shape=jax.ShapeDtypeStruct((8, n), jnp.float32),
        compiler_params=_params(1),
    )(c8, w, b)


W_SUB = CB // LANES
W_PAD_PIECE = (COL_RW * CB + RW_REAL) // LANES


def _proj_kernel(x_ref, sh_ref, sc_ref, *refs, n_lat):
    w_refs, (o_ref, h_ref) = refs[:W_SUB], refs[W_SUB:]
    j = pl.program_id(1)

    @pl.when(j == 0)
    def _():
        is_ctx = _ctx_rows(x_ref.shape[0], n_lat)
        h = _ln_rows(x_ref[...]) * (1.0 + _mod_row(sc_ref, is_ctx)) + _mod_row(sh_ref, is_ctx)
        h_ref[...] = h.astype(jnp.bfloat16)

    pieces = [w_refs[k][...] for k in range(W_SUB)]
    for k in range(W_SUB):
        if (W_PAD_PIECE - k) % W_SUB == 0:
            pieces[k] = jnp.where(j * W_SUB + k == W_PAD_PIECE, 0.0, pieces[k])
    w = jnp.concatenate(pieces, axis=1).astype(jnp.bfloat16)
    o_ref[...] = jnp.dot(h_ref[...], w, preferred_element_type=jnp.float32)


def modulated_projection(x, mod, w, layer, n_lat):
    m = x.shape[0]
    tm = PROJ_TM

    def piece(k):
        def index(i, j):
            p = j * W_SUB + k
            return layer, 0, jnp.where(p < W_PAD_PIECE, p, p - 1)
        return pl.BlockSpec((None, D_MODEL, LANES), index)

    return pl.pallas_call(
        functools.partial(_proj_kernel, n_lat=n_lat),
        grid=(m // tm, P_IN_PAD // CB),
        in_specs=[pl.BlockSpec((tm, D_MODEL), lambda i, j: (i, 0)),
                  pl.BlockSpec((8, D_MODEL), lambda i, j: (0, 0)),
                  pl.BlockSpec((8, D_MODEL), lambda i, j: (0, 1))] + [piece(k) for k in range(W_SUB)],
        out_specs=pl.BlockSpec((tm, CB), lambda i, j: (i, j)),
        out_shape=jax.ShapeDtypeStruct((m, P_IN_PAD), jnp.float32),
        scratch_shapes=[pltpu.VMEM((tm, D_MODEL), jnp.bfloat16)],
        compiler_params=_params(2),
    )(x, mod, mod, *([w] * W_SUB))


def _halo_specs(tp, n_rows, col, width=CB):
    per = tp // HALO
    last = n_rows // HALO - 1
    main = pl.BlockSpec((tp, width), lambda i: (i, col))
    prev = pl.BlockSpec((HALO, width), lambda i: (jnp.maximum(i * per - 1, 0), col))
    nxt = pl.BlockSpec((HALO, width), lambda i: (jnp.minimum((i + 1) * per, last), col))
    return [main, prev, nxt]


def _neighbours(x, prev, nxt, n_lat, n_rows):
    tp = x.shape[0]
    loc = lax.broadcasted_iota(jnp.int32, (tp, 1), 0)
    row = pl.program_id(0) * tp + loc
    xp = jnp.where(loc == 0, prev[HALO - 1:HALO, :], pltpu.roll(x, 1, axis=0))
    xp = jnp.where((row == 0) | (row == n_lat), 0.0, xp)
    xn = jnp.where(loc == tp - 1, nxt[0:1, :], pltpu.roll(x, tp - 1, axis=0))
    xn = jnp.where((row == n_lat - 1) | (row == n_rows - 1), 0.0, xn)
    return xp, xn


def _conv3_kernel(z_ref, zp_ref, zn_ref, w_ref, b_ref, o_ref, *, n_lat, n_rows):
    x = z_ref[...]
    xp, xn = _neighbours(x, zp_ref[...], zn_ref[...], n_lat, n_rows)
    w = w_ref[...]
    o_ref[...] = xp * w[0:1, :] + x * w[1:2, :] + xn * w[2:3, :] + b_ref[...]


def hyena_conv3(z, w, b, n_lat):
    n_rows = z.shape[0]
    tp = PREP_ROWS
    width = w.shape[1]
    return pl.pallas_call(
        functools.partial(_conv3_kernel, n_lat=n_lat, n_rows=n_rows),
        grid=(n_rows // tp,),
        in_specs=_halo_specs(tp, n_rows, COL_HY, width) + [_full(w), _full(b)],
        out_specs=pl.BlockSpec((tp, width), lambda i: (i, 0)),
        out_shape=jax.ShapeDtypeStruct((n_rows, width), jnp.float32),
        compiler_params=_params(1),
    )(z, z, z, w, b)


HY_EMB_PAD = 40
HY_FILTER_ROWS = 512


def hyena_features(L):
    t = jnp.linspace(0.0, 1.0, L, dtype=jnp.float32)[:, None]
    n_bands = (HY_EMB - 1) // 2
    f = jnp.linspace(1e-4, n_bands - 1, n_bands, dtype=jnp.float32)[None, :]
    ang = (2.0 * math.pi / L) * jnp.arange(L, dtype=jnp.float32)[:, None] * f
    z = jnp.concatenate([t, jnp.cos(ang), -jnp.sin(ang)], -1)
    return jnp.pad(z, ((0, 0), (0, HY_EMB_PAD - HY_EMB)))


def _filter_kernel(z_ref, w1_ref, b1_ref, w2_ref, b2_ref, w3_ref, b3_ref, fr_ref, dl_ref, h_ref, s_ref):
    i = pl.program_id(0)
    z = z_ref[...]
    fr = fr_ref[...]
    h = jnp.sin(fr * (_dot(z, w1_ref[...]) + b1_ref[...]))
    h = jnp.sin(fr * (_dot(h, w2_ref[...]) + b2_ref[...]))
    h = _mm1(h.astype(jnp.bfloat16), w3_ref[...].astype(jnp.bfloat16)) + b3_ref[...]
    win = jnp.exp(-z[:, 0:1] * dl_ref[...])
    h = h * jnp.concatenate([win] * (h.shape[1] // win.shape[1]), axis=1)

    @pl.when(i == 0)
    def _():
        s_ref[...] = jnp.zeros_like(s_ref)

    s_ref[...] += jnp.sum(jnp.abs(h), axis=0, keepdims=True)
    row = lax.broadcasted_iota(jnp.int32, h.shape, 0) + i * h.shape[0]
    col = lax.broadcasted_iota(jnp.int32, h.shape, 1)
    neg = (col // GROUP_W) % 2 == 1
    h_ref[...] = jnp.where(neg & (row == 0), 0.0, h)


def hyena_filter_bank(L, w1, b1, w2, b2, w3, b3, freq):
    z = hyena_features(L)
    w1p = jnp.pad(w1, ((0, HY_EMB_PAD - HY_EMB), (0, 0)))
    max_decay = math.log(HY_TARGET) / HY_FAST_DECAY
    min_decay = math.log(HY_TARGET) / HY_SLOW_DECAY
    deltas = jnp.abs(jnp.linspace(min_decay, max_decay, GROUP_W, dtype=jnp.float32))[None, :]
    n = w3.shape[1]
    tr = min(L, HY_FILTER_ROWS)
    args = (z, w1p, b1[None, :], w2, b2[None, :], w3, b3[None, :], freq[None, :], deltas)
    return pl.pallas_call(
        _filter_kernel,
        grid=(L // tr,),
        in_specs=[pl.BlockSpec((tr, HY_EMB_PAD), lambda i: (i, 0))] + [_full(a) for a in args[1:]],
        out_specs=[pl.BlockSpec((tr, n), lambda i: (i, 0)), pl.BlockSpec((1, n), lambda i: (0, 0))],
        out_shape=[jax.ShapeDtypeStruct((L, n), jnp.float32), jax.ShapeDtypeStruct((1, n), jnp.float32)],
        compiler_params=_params(1),
    )(*args)


FFT_NB = 8
FFT_NA = 16
FFT_CB = 512
FFT_N1 = 128


def fft_tables(n1, n2):
    N = n1 * n2
    nk = -(-(n2 // 2 + 1) // FFT_NB) * FFT_NB
    a = jnp.arange(n1, dtype=jnp.float32)[:, None, None]
    k2 = jnp.arange(nk, dtype=jnp.float32)[None, :, None]
    b = jnp.arange(n2 // 2, dtype=jnp.float32)[None, None, :]
    ph = (jnp.mod(a * k2, float(N)) / N + jnp.mod(b * k2, float(n2)) / n2) * (-2.0 * math.pi)
    g = jnp.concatenate([jnp.cos(ph), jnp.sin(ph)], axis=1)
    kk = jnp.arange(nk)
    weight = jnp.where((kk == 0) | (kk == n2 // 2), 1.0, jnp.where(kk < n2 // 2, 2.0, 0.0))
    ginv = jnp.transpose(g, (0, 2, 1)) * (jnp.tile(weight, 2) / N)
    k1 = jnp.arange(n1, dtype=jnp.float32)[:, None]
    aa = jnp.arange(n1, dtype=jnp.float32)[None, :]
    f = jnp.mod(k1 * aa, float(n1)) * (-2.0 * math.pi / n1)
    fr, fi = jnp.cos(f), jnp.sin(f)
    ff = jnp.concatenate([jnp.concatenate([fr, -fi], axis=1), jnp.concatenate([fi, fr], axis=1)], axis=0)
    bf = lambda t: t.astype(jnp.bfloat16)
    return {"g": bf(g), "ginv": bf(ginv), "ff": bf(ff), "fft": bf(ff.T)}


def _stage_a_kernel(u_ref, g_ref, o_ref):
    for j in range(FFT_NA):
        o_ref[j] = _mm1(g_ref[j], u_ref[:, j, :].astype(jnp.bfloat16))


def fft_stage_a(u3, nb, col, ncol, g):
    n1 = u3.shape[1]
    rows = g.shape[1]
    gspec = pl.BlockSpec((FFT_NA, rows, nb), lambda c, i: (i, 0, 0))
    return pl.pallas_call(
        _stage_a_kernel,
        grid=(ncol, n1 // FFT_NA),
        in_specs=[pl.BlockSpec((nb, FFT_NA, CB), lambda c, i: (0, i, col + c)), gspec],
        out_specs=pl.BlockSpec((FFT_NA, rows, CB), lambda c, i: (i, 0, c)),
        out_shape=jax.ShapeDtypeStruct((n1, rows, ncol * CB), jnp.float32),
        compiler_params=_params(2),
    )(u3, g)


def _stage_b_filter_kernel(re_ref, im_ref, f_ref, o_ref):
    ff = f_ref[...]
    for j in range(FFT_NB):
        o_ref[j] = _mm1(ff, jnp.concatenate([re_ref[:, j, :], im_ref[:, j, :]], axis=0).astype(jnp.bfloat16))


def fft_stage_b_filter(t1, ff):
    n1, n2x2, C = t1.shape
    n2 = n2x2 // 2
    blk = lambda off: pl.BlockSpec((n1, FFT_NB, FFT_CB), lambda c, i: (0, i + off, c))
    mat = pl.BlockSpec((2 * n1, 2 * n1), lambda c, i: (0, 0))
    return pl.pallas_call(
        _stage_b_filter_kernel,
        grid=(C // FFT_CB, n2 // FFT_NB),
        in_specs=[blk(0), blk(n2 // FFT_NB), mat],
        out_specs=pl.BlockSpec((FFT_NB, 2 * n1, FFT_CB), lambda c, i: (i, 0, c)),
        out_shape=jax.ShapeDtypeStruct((n2, 2 * n1, C), jnp.float32),
        compiler_params=_params(2),
    )(t1, t1, ff)


def _filter_spectrum(p, q, s, n1):
    return (p[:n1] + q[:n1]) * s, (p[n1:] - q[n1:]) * s


def _stage_b_conv_kernel(re_ref, im_ref, p_ref, q_ref, s_ref, f_ref, ft_ref, ore_ref, oim_ref):
    n1 = re_ref.shape[0]
    s = s_ref[...]
    ff, fft_ = f_ref[...], ft_ref[...]
    bf = lambda t: t.astype(jnp.bfloat16)
    for j in range(FFT_NB):
        x = _mm1(ff, bf(jnp.concatenate([re_ref[:, j, :], im_ref[:, j, :]], axis=0)))
        hr, hi = _filter_spectrum(p_ref[j], q_ref[j], s, n1)
        xr, xi = x[:n1], x[n1:]
        z = _mm1(fft_, bf(jnp.concatenate([xr * hr - xi * hi, xr * hi + xi * hr], axis=0)))
        ore_ref[:, j, :] = z[:n1]
        oim_ref[:, j, :] = z[n1:]


def fft_stage_b_conv(t1, spec, col_p, col_q, inv_norm, ff, fft_):
    n1, n2x2, C = t1.shape
    n2 = n2x2 // 2
    ncb = C // FFT_CB
    blk = lambda off: pl.BlockSpec((n1, FFT_NB, FFT_CB), lambda c, i: (0, i + off, c))
    mat = pl.BlockSpec((2 * n1, 2 * n1), lambda c, i: (0, 0))
    sp = lambda col: pl.BlockSpec((FFT_NB, 2 * n1, FFT_CB), lambda c, i: (i, 0, col * ncb + c))
    return pl.pallas_call(
        _stage_b_conv_kernel,
        grid=(ncb, n2 // FFT_NB),
        in_specs=[blk(0), blk(n2 // FFT_NB), sp(col_p), sp(col_q), pl.BlockSpec((1, FFT_CB), lambda c, i: (0, c)),
                  mat, mat],
        out_specs=[blk(0), blk(0)],
        out_shape=[jax.ShapeDtypeStruct((n1, n2, C), jnp.float32)] * 2,
        compiler_params=_params(2),
    )(t1, t1, spec, spec, inv_norm, ff, fft_)


def _stage_a_inv_kernel(re_ref, im_ref, g_ref, u_ref, gate_ref, bias_ref, o_ref):
    for j in range(FFT_NA):
        y = _mm1(g_ref[j], jnp.concatenate([re_ref[j], im_ref[j]], axis=0).astype(jnp.bfloat16))
        o_ref[:, j, :] = gate_ref[:, j, :] * (y + u_ref[:, j, :] * bias_ref[...])


def fft_stage_a_inv(t2re, t2im, ginv, nb, u3, u_col, gate3, gate_col, bias):
    n1, nk, C = t2re.shape
    tb = pl.BlockSpec((FFT_NA, nk, CB), lambda c, i: (i, 0, c))
    gb = pl.BlockSpec((FFT_NA, nb, 2 * nk), lambda c, i: (i, 0, 0))
    ub = lambda col: pl.BlockSpec((nb, FFT_NA, CB), lambda c, i: (0, i, col + c))
    return pl.pallas_call(
        _stage_a_inv_kernel,
        grid=(C // CB, n1 // FFT_NA),
        in_specs=[tb, tb, gb, ub(u_col), ub(gate_col), pl.BlockSpec((1, CB), lambda c, i: (0, c))],
        out_specs=ub(0),
        out_shape=jax.ShapeDtypeStruct((nb, n1, C), jnp.float32),
        compiler_params=_params(2),
    )(t2re, t2im, ginv, u3, gate3, bias)


def hyena_long_conv_chain(u, L, biases, filt, colsum, tabs):
    C = GROUP_W
    n1 = FFT_N1
    n2 = 2 * L // n1
    nb = n2 // 2
    spec = fft_stage_b_filter(fft_stage_a(filt.reshape(nb, n1, filt.shape[1]), nb, 0, filt.shape[1] // CB, tabs["g"]),
                              tabs["ff"])
    s4 = colsum.reshape(HY_ORDER, 2, C)
    inv_norm = 1.0 / (s4[:, 0] + s4[:, 1])
    u3 = u.reshape(u.shape[0] // n1, n1, u.shape[1])
    y3, y_col = u3, 0
    for n in range(HY_ORDER):
        t1 = fft_stage_a(y3, nb, y_col, 1, tabs["g"])
        t2re, t2im = fft_stage_b_conv(t1, spec, 2 * n, 2 * n + 1, inv_norm[n][None, :], tabs["ff"], tabs["fft"])
        y3 = fft_stage_a_inv(t2re, t2im, tabs["ginv"], nb, y3, y_col, u3, n + 1, biases[n][None, :])
        y_col = 0
    return y3.reshape(L, C)


def dense_dft_tables(n):
    N = 2 * n
    k = jnp.arange(N, dtype=jnp.float32)[:, None]
    t = jnp.arange(n, dtype=jnp.float32)[None, :]
    ph = jnp.mod(k * t, float(N)) * (2.0 * math.pi / N)
    fd = jnp.concatenate([jnp.cos(ph), -jnp.sin(ph)], axis=0)
    return fd.astype(jnp.bfloat16), (fd.T / N).astype(jnp.bfloat16)


def _hyena_ctx_kernel(u_ref, h_ref, s_ref, bias_ref, fd_ref, ft_ref, o_ref):
    C = GROUP_W
    fd, ft = fd_ref[...], ft_ref[...]
    K = fd.shape[0] // 2
    bf = lambda t: t.astype(jnp.bfloat16)
    hs = _mm1(fd, bf(h_ref[...]))
    s = s_ref[...]
    y = u_ref[:, 0:C]
    for n in range(HY_ORDER):
        cp, cq = 2 * n * C, (2 * n + 1) * C
        inv = 1.0 / (s[:, cp:cp + C] + s[:, cq:cq + C])
        hr, hi = _filter_spectrum(hs[:, cp:cp + C], hs[:, cq:cq + C], inv, K)
        x = _mm1(fd, bf(y))
        xr, xi = x[:K], x[K:]
        conv = _mm1(ft, bf(jnp.concatenate([xr * hr - xi * hi, xr * hi + xi * hr], axis=0)))
        y = u_ref[:, (n + 1) * C:(n + 2) * C] * (conv + y * bias_ref[n:n + 1, :])
    o_ref[...] = y


def hyena_ctx(u, filt, colsum, biases, tabs):
    args = (u, filt, colsum, biases, tabs[0], tabs[1])
    return pl.pallas_call(
        _hyena_ctx_kernel,
        in_specs=[_full(a) for a in args],
        out_specs=pl.BlockSpec((u.shape[0], GROUP_W), lambda: (0, 0)),
        out_shape=jax.ShapeDtypeStruct((u.shape[0], GROUP_W), jnp.float32),
        compiler_params=pltpu.CompilerParams(vmem_limit_bytes=VMEM_LIMIT),
    )(*args)


SCAN_T = 64
GROUP_LANES = 256
RW_SUB = 16
HG_SUB = 16
LOG2_E = 1.4426950408889634


def _chunk_index(d, i, n_ctx, n_all):
    fwd = jnp.where(i < n_ctx, n_all - n_ctx + i, i - n_ctx)
    return jnp.where(d == 0, fwd, n_all - 1 - i)


def _stacking(T, S, nh, head, sign):
    G = nh * head
    nb = T // S
    n = nh * T
    rr = lax.broadcasted_iota(jnp.int32, (n, G), 0)
    same = ((rr // S) % nh) == (lax.broadcasted_iota(jnp.int32, (n, G), 1) // head)

    def bd(x):
        pieces = []
        for i in range(nb):
            pieces += [x[i * S:(i + 1) * S]] * nh
        return jnp.where(same, jnp.concatenate(pieces, axis=0), 0.0)

    def collapse(o):
        outs = []
        for i in range(nb):
            acc = o[i * nh * S:i * nh * S + S]
            for h in range(1, nh):
                acc = acc + o[i * nh * S + h * S:i * nh * S + (h + 1) * S]
            outs.append(acc)
        return jnp.concatenate(outs, axis=0)

    rt = lax.broadcasted_iota(jnp.int32, (n, n), 0)
    cs = lax.broadcasted_iota(jnp.int32, (n, n), 1)
    t_r = (rt // (nh * S)) * S + rt % S
    t_c = (cs // (nh * S)) * S + cs % S
    same_h = ((rt // S) % nh) == ((cs // S) % nh)
    before = same_h & ((t_r - t_c) * sign > 0)
    return bd, collapse, before, rt, cs, (t_r, t_c, same_h)


def _softplus(x):
    return jnp.maximum(x, 0.0) + jnp.log(1.0 + jnp.exp(-jnp.abs(x)))


def _rwkv_prep_kernel(*refs, n_lat, n_rows):
    zrefs, rest = refs[:15], refs[15:]
    (mu_ref, kk_w_ref, ka_ref, w0_ref, a0_ref, w2h_ref, w2l_ref, a2h_ref, a2l_ref, ones_ref,
     r_ref, k_ref, v_ref, g_ref, kk_ref, lw_ref, kd_ref, a_ref) = rest
    slabs = []
    for c in range(5):
        x = zrefs[3 * c][...]
        xp, xn = _neighbours(x, zrefs[3 * c + 1][...], zrefs[3 * c + 2][...], n_lat, n_rows)
        slabs.append(x + (0.5 * (xp + xn) - x) * mu_ref[:, c * CB:(c + 1) * CB])
    r, k, v, g, lora = slabs
    r_ref[...], k_ref[...], v_ref[...], g_ref[...] = r, k, v, g
    kk = k * kk_w_ref[...]
    ss = _mm2(kk * kk, ones_ref[...])
    kk_ref[...] = kk * lax.rsqrt(jnp.maximum(ss, 1e-24))
    lora_t = _split(jnp.tanh(lora))
    lora_s = _split(lora)
    for d in range(2):
        w_log = -_softplus(-(w0_ref[d:d + 1, :] + _mm3(lora_t, (w2h_ref[d], w2l_ref[d])))) - 0.5
        lw_ref[d] = -jnp.exp(w_log)
        a = _sigmoid(a0_ref[d:d + 1, :] + _mm3(lora_s, (a2h_ref[d], a2l_ref[d])))
        a_ref[d] = a
        kd_ref[d] = k * (1.0 + (a - 1.0) * ka_ref[...])


def rwkv_prep(z, mu, k_k, k_a, w0, a0, w2, a2, n_lat):
    n_rows = z.shape[0]
    tp = PREP_ROWS
    mu_p = jnp.pad(mu, (0, 5 * CB - RW_REAL))[None, :]
    w2p = jnp.zeros((2, CB, GROUP_W), jnp.float32)
    a2p = jnp.zeros((2, CB, GROUP_W), jnp.float32)
    for d in range(2):
        w2p = w2p.at[d, d * RW_LORA:(d + 1) * RW_LORA].set(w2[d])
        a2p = a2p.at[d, (2 + d) * RW_LORA:(3 + d) * RW_LORA].set(a2[d])
    w2s, a2s = _split(w2p), _split(a2p)
    small = (mu_p, k_k[None, :], k_a[None, :], w0, a0, w2s[0], w2s[1], a2s[0], a2s[1],
             _block_indicator(GROUP_W, RW_HEAD, 1.0))
    zspecs = []
    for c in range(5):
        zspecs += _halo_specs(tp, n_rows, COL_RW + c)
    one = pl.BlockSpec((tp, CB), lambda i: (i, 0))
    two = pl.BlockSpec((2, tp, CB), lambda i: (0, i, 0))
    s1 = jax.ShapeDtypeStruct((n_rows, GROUP_W), jnp.float32)
    s2 = jax.ShapeDtypeStruct((2, n_rows, GROUP_W), jnp.float32)
    return pl.pallas_call(
        functools.partial(_rwkv_prep_kernel, n_lat=n_lat, n_rows=n_rows),
        grid=(n_rows // tp,),
        in_specs=zspecs + [_full(a) for a in small],
        out_specs=[one] * 5 + [two] * 3,
        out_shape=[s1] * 5 + [s2] * 3,
        compiler_params=_params(1),
    )(*([z] * 15), *small)


def _rwkv_scan_body(refs, head):
    in_f, in_b, (of_ref, ob_ref, ht_ref) = refs[:6], refs[6:12], refs[12:]
    T, G, S = SCAN_T, GROUP_LANES, RW_SUB
    nh = G // head
    n = nh * T

    @pl.when(pl.program_id(0) == 0)
    def _():
        ht_ref[...] = jnp.zeros_like(ht_ref)

    ti = lax.broadcasted_iota(jnp.int32, (T, T), 0)
    si = lax.broadcasted_iota(jnp.int32, (T, T), 1)
    bf = lambda t: t.astype(jnp.bfloat16)
    each = lambda f, *seqs: [f(*xs) for xs in zip(*seqs)]
    n_grp = in_f[0].shape[1] // G
    lanes = [slice(grp * G, (grp + 1) * G) for grp in range(n_grp)]

    r, v, kk, lw, k, a, tri, before, incl, dest = ([] for _ in range(10))
    for (r_ref, v_ref, kk_ref, lw_ref, k_ref, a_ref), o_ref, sign in ((in_f, of_ref, 1), (in_b, ob_ref, -1)):
        bd, collapse, before_d, rt, cs, (t_r, t_c, same_h) = _stacking(T, S, nh, head, sign)
        for ls in lanes:
            r.append(r_ref[:, ls]), v.append(v_ref[:, ls]), kk.append(kk_ref[:, ls])
            lw.append(lw_ref[0, :, ls]), k.append(k_ref[0, :, ls]), a.append(a_ref[0, :, ls])
            tri.append(jnp.where((ti - si) * sign >= 0, 1.0, 0.0))
            before.append(before_d), incl.append(before_d | (rt == cs)), dest.append((o_ref, ls))
    eye = jnp.where(rt == cs, 1.0, 0.0)
    ht = [ht_ref[ch] for ch in range(len(r))]

    c = each(lambda tr, t: _dot(tr, t), tri, lw)
    ctot = each(lambda t: jnp.sum(t, axis=0, keepdims=True), lw)
    beta = each(lambda p, q: p * q, kk, a)
    einv = each(lambda t: jnp.exp(-t), c)
    lhs = each(lambda kk_, r_, c_, lw_: bf(jnp.concatenate([bd(-kk_ * jnp.exp(c_ - lw_)), bd(r_ * jnp.exp(c_))], axis=0)),
               kk, r, c, lw)
    rhs = each(lambda k_, b_, e_: bf(jnp.concatenate([bd(k_ * e_), bd(b_ * e_)], axis=0)), k, beta, einv)
    m = each(lambda p, q: _mm1(p, q, NT), lhs, rhs)
    g = each(lambda p, h: _mm1(p, bf(h), NT), lhs, ht)
    v_b = each(lambda t: bf(bd(t)), v)
    x = each(lambda g_, m_, v_, bm: g_[:n] + _mm1(bf(jnp.where(bm, m_[:n, :n], 0.0)), v_), g, m, v_b, before)
    lb = each(lambda m_, bm: jnp.where(bm, m_[:n, n:], 0.0), m, before)
    differ = jnp.where(same_h, t_r ^ t_c, 0)
    half = lambda s: (differ >= s) & (differ < 2 * s)
    dmat = each(lambda t: eye + jnp.where(half(1), t, 0.0), lb)
    s = 2
    while s < T:
        coupling = half(s)
        dmat = each(lambda d_, t: d_ + _mm1(bf(_mm1(bf(d_), bf(jnp.where(coupling, t, 0.0)))), bf(d_)), dmat, lb)
        s *= 2
    u_b = each(lambda d_, t: bf(_mm1(bf(d_), bf(t))), dmat, x)
    o = each(lambda g_, m_, v_, u_, im: (g_[n:] + _mm1(bf(jnp.where(im, m_[n:, :n], 0.0)), v_))
             + _mm1(bf(jnp.where(im, m_[n:, n:], 0.0)), u_), g, m, v_b, u_b, incl)
    efin = each(lambda ct, c_: jnp.exp(ct - c_), ctot, c)
    ht_new = each(lambda h, ct, v_, u_, k_, b_, e_: (h * jnp.exp(ct) + _mm1(v_, bf(bd(k_ * e_)), TN))
                  + _mm1(u_, bf(bd(b_ * e_)), TN), ht, ctot, v_b, u_b, k, beta, efin)
    for ch, (o_ref, ls) in enumerate(dest):
        o_ref[:, ls] = collapse(o[ch])
        ht_ref[ch] = ht_new[ch]


def rope_tables(n_ctx, L):
    d_axis = RT_HEAD // 2
    half = d_axis // 2
    inv = ROPE_BASE ** (-jnp.arange(0, d_axis, 2, dtype=jnp.float32) / d_axis)
    t = jnp.arange(L)
    pos = jnp.stack([(t // GRID_W).astype(jnp.float32), (t % GRID_W).astype(jnp.float32)], axis=1)
    j = jnp.arange(RT_HEAD)
    ang = pos[:, j // d_axis] * inv[j % half][None, :]
    sgn = jnp.where((j % d_axis) < half, -1.0, 1.0)[None, :]
    cos = jnp.concatenate([jnp.cos(ang), jnp.ones((n_ctx, RT_HEAD), jnp.float32)], axis=0)
    sin = jnp.concatenate([jnp.sin(ang) * sgn, jnp.zeros((n_ctx, RT_HEAD), jnp.float32)], axis=0)
    return jnp.tile(cos, (1, RT_HEADS)), jnp.tile(sin, (1, RT_HEADS))


def _rotate(x, cos, sin):
    G = x.shape[1]
    half = RT_HEAD // 4
    lane = lax.broadcasted_iota(jnp.int32, x.shape, 1)
    partner = jnp.where((lane % (2 * half)) < half, pltpu.roll(x, G - half, axis=1), pltpu.roll(x, half, axis=1))
    return x * cos + partner * sin


def _retention_scan_body(q_ref, k_ref, v_ref, cos_ref, sin_ref, lg_ref, o_ref, st_ref, *, head, d):
    T, G = SCAN_T, GROUP_LANES
    nh = G // head
    sign = 1 - 2 * d

    t = lax.broadcasted_iota(jnp.int32, (T, 1), 0)
    pos = (t + d * (T - 1 - 2 * t) + 1).astype(jnp.float32)
    bd, collapse, before, rt, cs, _ = _stacking(T, T, nh, head, sign)
    incl = before | (rt == cs)
    bf = lambda x: x.astype(jnp.bfloat16)

    groups = range(q_ref.shape[1] // G)
    lanes = [slice(grp * G, (grp + 1) * G) for grp in groups]
    each = lambda f, *seqs: [f(*xs) for xs in zip(*seqs)]
    st = [st_ref[grp] for grp in groups]
    cos, sin, lg = ([ref[:, ls] for ls in lanes] for ref in (cos_ref, sin_ref, lg_ref))
    q = each(_rotate, [q_ref[:, ls] for ls in lanes], cos, sin)
    k = each(lambda x, c_, s_: _rotate(x, c_, s_) * (head ** -0.5), [k_ref[:, ls] for ls in lanes], cos, sin)
    c = each(lambda lg_: pos * lg_, lg)
    q_b = each(lambda q_, c_: bf(bd(q_ * jnp.exp(c_))), q, c)
    kt_b = each(lambda k_, c_: bf(bd(k_ * jnp.exp(-c_))), k, c)
    v_b = [bf(bd(v_ref[:, ls])) for ls in lanes]
    scores = each(lambda q_, k_: bf(jnp.where(incl, _mm1(q_, k_, NT), 0.0)), q_b, kt_b)
    inter = each(lambda q_, s_: _mm1(q_, bf(s_), NT), q_b, st)
    o = each(lambda g_, s_, v_: g_ + _mm1(s_, v_), inter, scores, v_b)
    st_new = each(lambda s_, lg_, v_, k_, c_: s_ * jnp.exp(float(T) * lg_)
                  + _mm1(v_, bf(bd(k_ * jnp.exp(float(T) * lg_ - c_))), TN), st, lg, v_b, k, c)
    for grp in groups:
        o_ref[:, lanes[grp]] = collapse(o[grp])
        st_ref[grp] = st_new[grp]


def _gla_scan_body(q_ref, f_ref, i_ref, lb_ref, o_ref, st_ref, *, head, d):
    T, S = SCAN_T, HG_SUB
    W = q_ref.shape[-1]
    nh = W // head
    sign = 1 - 2 * d

    ti = lax.broadcasted_iota(jnp.int32, (S, S), 0)
    si = lax.broadcasted_iota(jnp.int32, (S, S), 1)
    tri_incl = jnp.where((ti - si) * sign >= 0, 1.0, 0.0)
    row = lax.broadcasted_iota(jnp.int32, (S, 1), 0)
    lb = lb_ref[0]

    states = [st_ref[h] for h in range(nh)]
    for j in range(T // S):
        jb = j + d * (T // S - 1 - 2 * j)
        rows = pl.ds(jb * S, S)
        q = q_ref[rows, :]
        q = q * _sigmoid(q)
        v = i_ref[rows, :]
        gate = lb + (1.0 - lb) * _sigmoid(f_ref[rows, :])
        lf = jnp.log(jnp.maximum(gate, HG_MIN_GATE))
        k = 1.0 - gate
        b = _dot(tri_incl, lf)
        btot = jnp.sum(lf, axis=0, keepdims=True)
        qe = q * jnp.exp(b)
        ke = k * jnp.exp(btot - b)
        outs = []
        for h in range(nh):
            ls = slice(h * head, (h + 1) * head)
            qh, kh, vh = q[:, ls], k[:, ls], v[:, ls]
            bh = b[:, ls] * LOG2_E
            st = states[h]
            o = _mm1(qe[:, ls].astype(jnp.bfloat16), st.astype(jnp.bfloat16), NT)
            for s in range(S):
                e = jnp.exp2(bh - bh[s:s + 1, :])
                a_s = jnp.sum(qh * kh[s:s + 1, :] * e, axis=-1, keepdims=True)
                a_s = jnp.where((row - s) * sign >= 0, a_s, 0.0)
                o = o + a_s * vh[s:s + 1, :]
            outs.append(o)
            states[h] = st * jnp.exp(btot[:, ls]) + _mm1(vh.astype(jnp.bfloat16), ke[:, ls].astype(jnp.bfloat16), TN)
        o_ref[rows, :] = jnp.concatenate(outs, axis=1)
    for h in range(nh):
        st_ref[h] = states[h]


N_RW, N_RT, N_HG = 6, 6, 4


def _scans_kernel(*refs):
    n_in = 2 * (N_RW + N_RT + N_HG)
    ins, (rw_of, rw_ob, rt_of, rt_ob, hg_of, hg_ob, rw_st, rt_st, hg_st) = refs[:n_in], refs[n_in:]
    rw_in, rt_in, hg_in = ins[:2 * N_RW], ins[2 * N_RW:2 * (N_RW + N_RT)], ins[2 * (N_RW + N_RT):]

    @pl.when(pl.program_id(0) == 0)
    def _():
        rt_st[...] = jnp.zeros_like(rt_st)
        hg_st[...] = jnp.zeros_like(hg_st)

    n_rt, n_hg = rt_st.shape[0] // 2, hg_st.shape[0] // 2
    _rwkv_scan_body(list(rw_in) + [rw_of, rw_ob, rw_st], RW_HEAD)
    for d, (rt_o, hg_o) in enumerate(((rt_of, hg_of), (rt_ob, hg_ob))):
        _gla_scan_body(*hg_in[d * N_HG:(d + 1) * N_HG], hg_o, hg_st.at[d * n_hg:(d + 1) * n_hg], head=HG_HEAD, d=d)
        _retention_scan_body(*rt_in[d * N_RT:(d + 1) * N_RT], rt_o, rt_st.at[d * n_rt:(d + 1) * n_rt],
                             head=RT_HEAD, d=d)


def recurrent_scans(z, rw, cos, sin, lg, lb2, n_ctx_rows):
    N = z.shape[0]
    W = GROUP_W
    T = SCAN_T
    n_all, n_ctx = N // T, n_ctx_rows // T
    chunk = lambda d: (lambda i: (_chunk_index(d, i, n_ctx, n_all), 0))
    row = lambda d: pl.BlockSpec((T, W), chunk(d))
    per_dir = lambda d: pl.BlockSpec((1, T, W), lambda i: (d, _chunk_index(d, i, n_ctx, n_all), 0))
    zc = lambda d, col: pl.BlockSpec((T, CB), lambda i: (_chunk_index(d, i, n_ctx, n_all), col))
    rw_specs = lambda d: [row(d)] * 3 + [per_dir(d)] * 3
    rt_specs = lambda d: [zc(d, COL_RT), zc(d, COL_RT + 1), zc(d, COL_RT + 2), row(d), row(d),
                          pl.BlockSpec((1, W), lambda i: (0, 0))]
    hg_specs = lambda d: [zc(d, COL_HG), zc(d, COL_HG + 1 + d), zc(d, COL_HG + 3),
                          pl.BlockSpec((1, 1, W), lambda i: (d, 0, 0))]
    shape = jax.ShapeDtypeStruct((N, W), jnp.float32)
    return pl.pallas_call(
        _scans_kernel,
        grid=(n_all,),
        in_specs=rw_specs(0) + rw_specs(1) + rt_specs(0) + rt_specs(1) + hg_specs(0) + hg_specs(1),
        out_specs=[row(0), row(1)] * 3,
        out_shape=[shape] * 6,
        scratch_shapes=[pltpu.VMEM((2 * (W // GROUP_LANES), GROUP_LANES, GROUP_LANES), jnp.float32),
                        pltpu.VMEM((2 * (W // GROUP_LANES), GROUP_LANES, GROUP_LANES), jnp.float32),
                        pltpu.VMEM((2 * (W // HG_HEAD), HG_HEAD, HG_HEAD), jnp.float32)],
        compiler_params=_params(1),
    )(*(list(rw) * 2), *([z, z, z, cos, sin, lg] * 2), *([z, z, z, lb2] * 2))


def _outproj_kernel(hy_ref, rwof_ref, rwob_ref, r_ref, k_ref, v_ref, rwg_ref, rtof_ref, rtob_ref, rtg_ref,
                    hgof_ref, hgob_ref, hgg_ref,
                    rk_ref, gng_ref, gnb_ref, hgn_ref, avg64_ref, avg128_ref,
                    w_ref, x_ref, gate_ref, g_ref, b_ref, o_ref, *, n_lat):
    avg64, avg128 = avg64_ref[...], avg128_ref[...]

    def head_norm(o, avg, eps, centre):
        if centre:
            o = o - _mm2(o, avg)
        return o * lax.rsqrt(_mm2(o * o, avg) + eps)

    silu = lambda t: t * _sigmoid(t)
    y_rw = head_norm(rwof_ref[...] + rwob_ref[...], avg64, RW_GN_EPS, True) * gng_ref[...] + gnb_ref[...]
    bonus = (float(RW_HEAD) * _mm2(r_ref[...] * k_ref[...] * rk_ref[...], avg64)) * v_ref[...]
    y_rw = (y_rw + bonus) * _sigmoid(rwg_ref[...])
    y_rt = head_norm(rtof_ref[...] + rtob_ref[...], avg64, HEAD_NORM_EPS, True) * silu(rtg_ref[...])
    y_hg = head_norm(hgof_ref[...] + hgob_ref[...], avg128, HEAD_NORM_EPS, False) * hgn_ref[...] * silu(hgg_ref[...])
    y = None
    for m, ym in enumerate((hy_ref[...], y_rw, y_rt, y_hg)):
        part = jnp.dot(ym.astype(jnp.bfloat16), w_ref[m * GROUP_W:(m + 1) * GROUP_W, :],
                       preferred_element_type=jnp.float32)
        y = part if y is None else y + part
    is_ctx = _ctx_rows(x_ref.shape[0], n_lat)
    r = ALPHA * x_ref[...] + _mod_row(gate_ref, is_ctx) * y
    o_ref[...] = _ln_rows(r) * g_ref[...] + b_ref[...]


def outproj_deepnorm(y_hy, scans, rw_r, rw_k, rw_v, rw_g, z, r_k, gn_g, gn_b, hg_norm_g,
                     w_bf16, x, mod, g, b, n_lat, m):
    tm = OUT_TM
    one = pl.BlockSpec((tm, GROUP_W), lambda i: (i, 0))
    zc = lambda col: pl.BlockSpec((tm, CB), lambda i: (i, col))
    row = pl.BlockSpec((tm, D_MODEL), lambda i: (i, 0))
    small = (r_k, gn_g, gn_b, hg_norm_g, _block_indicator(GROUP_W, RW_HEAD, 1.0 / RW_HEAD),
             _block_indicator(GROUP_W, HG_HEAD, 1.0 / HG_HEAD))
    return pl.pallas_call(
        functools.partial(_outproj_kernel, n_lat=n_lat),
        grid=(m // tm,),
        in_specs=[one] * 9 + [zc(COL_RT + 3), one, one, zc(COL_HG + 4)]
        + [_full(a) for a in small]
        + [_full(w_bf16), row, pl.BlockSpec((8, D_MODEL), lambda i: (0, 2)), _full(g), _full(b)],
        out_specs=row,
        out_shape=jax.ShapeDtypeStruct((m, D_MODEL), jnp.float32),
        compiler_params=_params(1),
    )(y_hy, scans[0], scans[1], rw_r, rw_k, rw_v, rw_g, scans[2], scans[3], z, scans[4], scans[5], z,
      *small, w_bf16, x, mod, g, b)


def _ffn_kernel(x_ref, sh_ref, sc_ref, w1_ref, w3_ref, w2_ref, gate_ref, g_ref, b_ref, o_ref, h_ref, acc_ref, *, n_lat):
    j = pl.program_id(1)

    @pl.when(j == 0)
    def _():
        is_ctx = _ctx_rows(x_ref.shape[0], n_lat)
        h = _ln_rows(x_ref[...]) * (1.0 + _mod_row(sc_ref, is_ctx)) + _mod_row(sh_ref, is_ctx)
        h_ref[...] = h.astype(jnp.bfloat16)
        acc_ref[...] = jnp.zeros_like(acc_ref)

    h = h_ref[...]
    a = jnp.dot(h, w1_ref[...], preferred_element_type=jnp.float32)
    u = jnp.dot(h, w3_ref[...], preferred_element_type=jnp.float32)
    s = (a * _sigmoid(a) * u).astype(jnp.bfloat16)
    acc_ref[...] += jnp.dot(s, w2_ref[...], preferred_element_type=jnp.float32)

    @pl.when(j == pl.num_programs(1) - 1)
    def _():
        is_ctx = _ctx_rows(x_ref.shape[0], n_lat)
        r = ALPHA * x_ref[...] + _mod_row(gate_ref, is_ctx) * acc_ref[...]
        o_ref[...] = _ln_rows(r) * g_ref[...] + b_ref[...]


def ffn_deepnorm(x, mod, w1, w3, w2, g, b, n_lat, tm):
    m = x.shape[0]
    row = pl.BlockSpec((tm, D_MODEL), lambda i, j: (i, 0))
    modc = lambda c: pl.BlockSpec((8, D_MODEL), lambda i, j: (0, c))
    vec = pl.BlockSpec((1, D_MODEL), lambda i, j: (0, 0))
    return pl.pallas_call(
        functools.partial(_ffn_kernel, n_lat=n_lat),
        grid=(m // tm, FFN_HIDDEN // FFN_TF),
        in_specs=[row, modc(3), modc(4),
                  pl.BlockSpec((D_MODEL, FFN_TF), lambda i, j: (0, j)),
                  pl.BlockSpec((D_MODEL, FFN_TF), lambda i, j: (0, j)),
                  pl.BlockSpec((FFN_TF, D_MODEL), lambda i, j: (j, 0)),
                  modc(5), vec, vec],
        out_specs=row,
        out_shape=jax.ShapeDtypeStruct((m, D_MODEL), jnp.float32),
        scratch_shapes=[pltpu.VMEM((tm, D_MODEL), jnp.bfloat16),
                        pltpu.VMEM((tm, D_MODEL), jnp.float32)],
        compiler_params=_params(2),
    )(x, mod, mod, w1, w3, w2, mod, g, b)


def kernel(x, c, ctx, c_ctx, ada_w, ada_b, w_in, w_out, ln_g, ln_b, hy_conv_w, hy_conv_b, hy_w1, hy_b1, hy_w2, hy_b2, hy_w3, hy_b3, hy_freq, hy_bias, rw_mu, rw_w0, rw_w2, rw_a0, rw_a2, rw_k_k, rw_k_a, rw_r_k, rw_gn_g, rw_gn_b, hg_lb_raw, hg_norm_g, ffn_w1, ffn_w3, ffn_w2):
    L, n_ctx = x.shape[1], ctx.shape[1]
    sm = jax.nn.softmax(hg_lb_raw.astype(jnp.float32), axis=1)
    lower_bounds = jnp.cumsum(sm, axis=1) - sm[:, :1]
    log_gamma = jnp.log1p(-jnp.exp2(-5.0 - jnp.arange(RT_HEADS, dtype=jnp.float32)))
    lg = jnp.repeat(log_gamma, RT_HEAD)[None, :]
    cos, sin = rope_tables(n_ctx, L)
    fft_tabs = fft_tables(FFT_N1, 2 * L // FFT_N1)
    ctx_tabs = dense_dft_tables(n_ctx)

    c8 = jnp.zeros((8, D_MODEL), jnp.float32).at[0].set(c[0]).at[1].set(c_ctx)
    xs = jnp.concatenate([x[0], ctx[0]], axis=0)
    n_rows = L + n_ctx
    for l in range(DEPTH):
        with_ctx = l < DEPTH - 1
        mod = ada_modulation(c8, ada_w, l, ada_b[l][None, :])
        w_out_b = w_out[l].astype(jnp.bfloat16)
        w1_b, w3_b, w2_b = (w[l].astype(jnp.bfloat16) for w in (ffn_w1, ffn_w3, ffn_w2))

        z = modulated_projection(xs, mod, w_in, l, L)

        u = hyena_conv3(z, hy_conv_w[l], hy_conv_b[l][None, :], L)
        hy_w = (hy_w1[l], hy_b1[l], hy_w2[l], hy_b2[l], hy_w3[l], hy_b3[l], hy_freq[l])
        filt, colsum = hyena_filter_bank(L, *hy_w)
        y_lat = hyena_long_conv_chain(u, L, hy_bias[l], filt, colsum, fft_tabs)
        if with_ctx:
            filt_c, colsum_c = hyena_filter_bank(n_ctx, *hy_w)
            y_ctx = hyena_ctx(u[L:], filt_c, colsum_c, hy_bias[l], ctx_tabs)
        else:
            y_ctx = jnp.zeros((n_ctx, GROUP_W), jnp.float32)
        y_hy = jnp.concatenate([y_lat, y_ctx], axis=0)

        rw_r, rw_k, rw_v, rw_g, rw_kk, rw_lw, rw_kd, rw_a = rwkv_prep(
            z, rw_mu[l], rw_k_k[l], rw_k_a[l], rw_w0[l], rw_a0[l], rw_w2[l], rw_a2[l], L)
        scans = recurrent_scans(z, (rw_r, rw_v, rw_kk, rw_lw, rw_kd, rw_a), cos, sin, lg,
                                lower_bounds[:, l][:, None, :], n_ctx)

        xs = outproj_deepnorm(y_hy, scans, rw_r, rw_k, rw_v, rw_g, z,
                              rw_r_k[l].reshape(1, GROUP_W), rw_gn_g[l][None, :], rw_gn_b[l][None, :],
                              hg_norm_g[l][None, :], w_out_b, xs, mod, ln_g[l, 0][None, :], ln_b[l, 0][None, :],
                              L, n_rows if with_ctx else L)
        xs = ffn_deepnorm(xs, mod, w1_b, w3_b, w2_b, ln_g[l, 1][None, :], ln_b[l, 1][None, :], L,
                          FFN_TM if with_ctx else FFN_LAST_TM)
    return xs[None]
```

```python
import functools
import math

import jax
import jax.numpy as jnp
from jax import lax
from jax.experimental import pallas as pl
from jax.experimental.pallas import tpu as pltpu

D_MODEL = 2048
DEPTH = 2
GRID_W = 64
N_MIXERS = 4
GROUP_W = D_MODEL // N_MIXERS
HY_ORDER = 2
HY_EMB = 33
HY_FAST_DECAY = 0.3
HY_SLOW_DECAY = 1.5
HY_TARGET = 1e-2
RW_HEAD = 64
RW_LORA = 96
RW_GN_EPS = 64e-5
RT_HEAD = 64
RT_HEADS = GROUP_W // RT_HEAD
ROPE_BASE = 10000.0
HG_HEAD = 128
HG_MIN_GATE = 1e-30
FFN_HIDDEN = 5632
ALPHA = (2 * DEPTH) ** 0.25
LN_EPS = 1e-6
HEAD_NORM_EPS = 1e-6

LANES = 128
CB = 512
COL_HY = 0
COL_RW = 3
COL_RT = 8
COL_HG = 12
P_IN_PAD = 17 * CB
RW_REAL = 4 * GROUP_W + 4 * RW_LORA
PROJ_TM = 1408
FFN_TM = 768
FFN_LAST_TM = 512
OUT_TM = 256
FFN_TF = 512
PREP_ROWS = 256
HALO = 8
VMEM_LIMIT = 56 * 1024 * 1024

HI = lax.Precision.HIGHEST
NN = ((1,), (0,))
NT = ((1,), (1,))
TN = ((0,), (0,))


def _params(n_axes):
    return pltpu.CompilerParams(dimension_semantics=("arbitrary",) * n_axes, vmem_limit_bytes=VMEM_LIMIT)


def _full(a):
    return pl.BlockSpec(a.shape, lambda *_: (0,) * a.ndim)


def _dot(a, b):
    return jnp.dot(a, b, precision=HI, preferred_element_type=jnp.float32)


def _split(x):
    hi = x.astype(jnp.bfloat16)
    lo = (x - hi.astype(jnp.float32)).astype(jnp.bfloat16)
    return hi, lo


def _mm3(a, b, dims=NN):
    d = lambda p, q: lax.dot_general(p, q, (dims, ((), ())), preferred_element_type=jnp.float32)
    return d(a[0], b[0]) + (d(a[0], b[1]) + d(a[1], b[0]))


def _mm1(a, b, dims=NN):
    return lax.dot_general(a, b, (dims, ((), ())), preferred_element_type=jnp.float32)


def _mm2(x, b):
    hi, lo = _split(x)
    d = functools.partial(jnp.dot, preferred_element_type=jnp.float32)
    return d(hi, b) + d(lo, b)


def _ln_rows(x):
    mu = jnp.mean(x, axis=-1, keepdims=True)
    xc = x - mu
    var = jnp.mean(xc * xc, axis=-1, keepdims=True)
    return xc * lax.rsqrt(var + LN_EPS)


def _sigmoid(x):
    return 1.0 / (1.0 + jnp.exp(-x))


def _ctx_rows(tm, n_lat):
    return (pl.program_id(0) * tm + lax.broadcasted_iota(jnp.int32, (tm, 1), 0)) >= n_lat


def _mod_row(ref, is_ctx):
    return jnp.where(is_ctx, ref[1:2, :], ref[0:1, :])


def _block_indicator(width, head, value):
    i = jnp.arange(width) // head
    return jnp.where(i[:, None] == i[None, :], value, 0.0).astype(jnp.bfloat16)


def _ada_kernel(c_ref, w_ref, b_ref, o_ref):
    c = c_ref[...]
    h = c * _sigmoid(c)
    o_ref[...] = jnp.dot(h.astype(jnp.bfloat16), w_ref[...].astype(jnp.bfloat16),
                         preferred_element_type=jnp.float32) + b_ref[...]


def ada_modulation(c8, w, layer, b):
    n = w.shape[2]
    tn = 1024
    return pl.pallas_call(
        _ada_kernel,
        grid=(n // tn,),
        in_specs=[pl.BlockSpec((8, D_MODEL), lambda j: (0, 0)),
                  pl.BlockSpec((None, D_MODEL, tn), lambda j: (layer, 0, j)),
                  pl.BlockSpec((1, tn), lambda j: (0, j))],
        out_specs=pl.BlockSpec((8, tn), lambda j: (0, j)),
        out_shape=jax.ShapeDtypeStruct((8, n), jnp.float32),
        compiler_params=_params(1),
    )(c8, w, b)


W_SUB = CB // LANES
W_PAD_PIECE = (COL_RW * CB + RW_REAL) // LANES


def _proj_kernel(x_ref, sh_ref, sc_ref, *refs, n_lat):
    w_refs, (o_ref, h_ref) = refs[:W_SUB], refs[W_SUB:]
    j = pl.program_id(1)

    @pl.when(j == 0)
    def _():
        is_ctx = _ctx_rows(x_ref.shape[0], n_lat)
        h = _ln_rows(x_ref[...]) * (1.0 + _mod_row(sc_ref, is_ctx)) + _mod_row(sh_ref, is_ctx)
        h_ref[...] = h.astype(jnp.bfloat16)

    pieces = [w_refs[k][...] for k in range(W_SUB)]
    for k in range(W_SUB):
        if (W_PAD_PIECE - k) % W_SUB == 0:
            pieces[k] = jnp.where(j * W_SUB + k == W_PAD_PIECE, 0.0, pieces[k])
    w = jnp.concatenate(pieces, axis=1).astype(jnp.bfloat16)
    o_ref[...] = jnp.dot(h_ref[...], w, preferred_element_type=jnp.float32)


def modulated_projection(x, mod, w, layer, n_lat):
    m = x.shape[0]
    tm = PROJ_TM

    def piece(k):
        def index(i, j):
            p = j * W_SUB + k
            return layer, 0, jnp.where(p < W_PAD_PIECE, p, p - 1)
        return pl.BlockSpec((None, D_MODEL, LANES), index)

    return pl.pallas_call(
        functools.partial(_proj_kernel, n_lat=n_lat),
        grid=(m // tm, P_IN_PAD // CB),
        in_specs=[pl.BlockSpec((tm, D_MODEL), lambda i, j: (i, 0)),
                  pl.BlockSpec((8, D_MODEL), lambda i, j: (0, 0)),
                  pl.BlockSpec((8, D_MODEL), lambda i, j: (0, 1))] + [piece(k) for k in range(W_SUB)],
        out_specs=pl.BlockSpec((tm, CB), lambda i, j: (i, j)),
        out_shape=jax.ShapeDtypeStruct((m, P_IN_PAD), jnp.float32),
        scratch_shapes=[pltpu.VMEM((tm, D_MODEL), jnp.bfloat16)],
        compiler_params=_params(2),
    )(x, mod, mod, *([w] * W_SUB))


def _halo_specs(tp, n_rows, col, width=CB):
    per = tp // HALO
    last = n_rows // HALO - 1
    main = pl.BlockSpec((tp, width), lambda i: (i, col))
    prev = pl.BlockSpec((HALO, width), lambda i: (jnp.maximum(i * per - 1, 0), col))
    nxt = pl.BlockSpec((HALO, width), lambda i: (jnp.minimum((i + 1) * per, last), col))
    return [main, prev, nxt]


def _neighbours(x, prev, nxt, n_lat, n_rows):
    tp = x.shape[0]
    loc = lax.broadcasted_iota(jnp.int32, (tp, 1), 0)
    row = pl.program_id(0) * tp + loc
    xp = jnp.where(loc == 0, prev[HALO - 1:HALO, :], pltpu.roll(x, 1, axis=0))
    xp = jnp.where((row == 0) | (row == n_lat), 0.0, xp)
    xn = jnp.where(loc == tp - 1, nxt[0:1, :], pltpu.roll(x, tp - 1, axis=0))
    xn = jnp.where((row == n_lat - 1) | (row == n_rows - 1), 0.0, xn)
    return xp, xn


def _conv3_kernel(z_ref, zp_ref, zn_ref, w_ref, b_ref, o_ref, *, n_lat, n_rows):
    x = z_ref[...]
    xp, xn = _neighbours(x, zp_ref[...], zn_ref[...], n_lat, n_rows)
    w = w_ref[...]
    o_ref[...] = xp * w[0:1, :] + x * w[1:2, :] + xn * w[2:3, :] + b_ref[...]


def hyena_conv3(z, w, b, n_lat):
    n_rows = z.shape[0]
    tp = PREP_ROWS
    width = w.shape[1]
    return pl.pallas_call(
        functools.partial(_conv3_kernel, n_lat=n_lat, n_rows=n_rows),
        grid=(n_rows // tp,),
        in_specs=_halo_specs(tp, n_rows, COL_HY, width) + [_full(w), _full(b)],
        out_specs=pl.BlockSpec((tp, width), lambda i: (i, 0)),
        out_shape=jax.ShapeDtypeStruct((n_rows, width), jnp.float32),
        compiler_params=_params(1),
    )(z, z, z, w, b)


HY_EMB_PAD = 40
HY_FILTER_ROWS = 512


def hyena_features(L):
    t = jnp.linspace(0.0, 1.0, L, dtype=jnp.float32)[:, None]
    n_bands = (HY_EMB - 1) // 2
    f = jnp.linspace(1e-4, n_bands - 1, n_bands, dtype=jnp.float32)[None, :]
    ang = (2.0 * math.pi / L) * jnp.arange(L, dtype=jnp.float32)[:, None] * f
    z = jnp.concatenate([t, jnp.cos(ang), -jnp.sin(ang)], -1)
    return jnp.pad(z, ((0, 0), (0, HY_EMB_PAD - HY_EMB)))


def _filter_kernel(z_ref, w1_ref, b1_ref, w2_ref, b2_ref, w3_ref, b3_ref, fr_ref, dl_ref, h_ref, s_ref):
    i = pl.program_id(0)
    z = z_ref[...]
    fr = fr_ref[...]
    h = jnp.sin(fr * (_dot(z, w1_ref[...]) + b1_ref[...]))
    h = jnp.sin(fr * (_dot(h, w2_ref[...]) + b2_ref[...]))
    h = _mm1(h.astype(jnp.bfloat16), w3_ref[...].astype(jnp.bfloat16)) + b3_ref[...]
    win = jnp.exp(-z[:, 0:1] * dl_ref[...])
    h = h * jnp.concatenate([win] * (h.shape[1] // win.shape[1]), axis=1)

    @pl.when(i == 0)
    def _():
        s_ref[...] = jnp.zeros_like(s_ref)

    s_ref[...] += jnp.sum(jnp.abs(h), axis=0, keepdims=True)
    row = lax.broadcasted_iota(jnp.int32, h.shape, 0) + i * h.shape[0]
    col = lax.broadcasted_iota(jnp.int32, h.shape, 1)
    neg = (col // GROUP_W) % 2 == 1
    h_ref[...] = jnp.where(neg & (row == 0), 0.0, h)


def hyena_filter_bank(L, w1, b1, w2, b2, w3, b3, freq):
    z = hyena_features(L)
    w1p = jnp.pad(w1, ((0, HY_EMB_PAD - HY_EMB), (0, 0)))
    max_decay = math.log(HY_TARGET) / HY_FAST_DECAY
    min_decay = math.log(HY_TARGET) / HY_SLOW_DECAY
    deltas = jnp.abs(jnp.linspace(min_decay, max_decay, GROUP_W, dtype=jnp.float32))[None, :]
    n = w3.shape[1]
    tr = min(L, HY_FILTER_ROWS)
    args = (z, w1p, b1[None, :], w2, b2[None, :], w3, b3[None, :], freq[None, :], deltas)
    return pl.pallas_call(
        _filter_kernel,
        grid=(L // tr,),
        in_specs=[pl.BlockSpec((tr, HY_EMB_PAD), lambda i: (i, 0))] + [_full(a) for a in args[1:]],
        out_specs=[pl.BlockSpec((tr, n), lambda i: (i, 0)), pl.BlockSpec((1, n), lambda i: (0, 0))],
        out_shape=[jax.ShapeDtypeStruct((L, n), jnp.float32), jax.ShapeDtypeStruct((1, n), jnp.float32)],
        compiler_params=_params(1),
    )(*args)


FFT_NB = 8
FFT_NA = 16
FFT_CB = 512
FFT_N1 = 128


def fft_tables(n1, n2):
    N = n1 * n2
    nk = -(-(n2 // 2 + 1) // FFT_NB) * FFT_NB
    a = jnp.arange(n1, dtype=jnp.float32)[:, None, None]
    k2 = jnp.arange(nk, dtype=jnp.float32)[None, :, None]
    b = jnp.arange(n2 // 2, dtype=jnp.float32)[None, None, :]
    ph = (jnp.mod(a * k2, float(N)) / N + jnp.mod(b * k2, float(n2)) / n2) * (-2.0 * math.pi)
    g = jnp.concatenate([jnp.cos(ph), jnp.sin(ph)], axis=1)
    kk = jnp.arange(nk)
    weight = jnp.where((kk == 0) | (kk == n2 // 2), 1.0, jnp.where(kk < n2 // 2, 2.0, 0.0))
    ginv = jnp.transpose(g, (0, 2, 1)) * (jnp.tile(weight, 2) / N)
    k1 = jnp.arange(n1, dtype=jnp.float32)[:, None]
    aa = jnp.arange(n1, dtype=jnp.float32)[None, :]
    f = jnp.mod(k1 * aa, float(n1)) * (-2.0 * math.pi / n1)
    fr, fi = jnp.cos(f), jnp.sin(f)
    ff = jnp.concatenate([jnp.concatenate([fr, -fi], axis=1), jnp.concatenate([fi, fr], axis=1)], axis=0)
    bf = lambda t: t.astype(jnp.bfloat16)
    return {"g": bf(g), "ginv": bf(ginv), "ff": bf(ff), "fft": bf(ff.T)}


def _stage_a_kernel(u_ref, g_ref, o_ref):
    for j in range(FFT_NA):
        o_ref[j] = _mm1(g_ref[j], u_ref[:, j, :].astype(jnp.bfloat16))


def fft_stage_a(u3, nb, col, ncol, g):
    n1 = u3.shape[1]
    rows = g.shape[1]
    gspec = pl.BlockSpec((FFT_NA, rows, nb), lambda c, i: (i, 0, 0))
    return pl.pallas_call(
        _stage_a_kernel,
        grid=(ncol, n1 // FFT_NA),
        in_specs=[pl.BlockSpec((nb, FFT_NA, CB), lambda c, i: (0, i, col + c)), gspec],
        out_specs=pl.BlockSpec((FFT_NA, rows, CB), lambda c, i: (i, 0, c)),
        out_shape=jax.ShapeDtypeStruct((n1, rows, ncol * CB), jnp.float32),
        compiler_params=_params(2),
    )(u3, g)


def _stage_b_filter_kernel(re_ref, im_ref, f_ref, o_ref):
    ff = f_ref[...]
    for j in range(FFT_NB):
        o_ref[j] = _mm1(ff, jnp.concatenate([re_ref[:, j, :], im_ref[:, j, :]], axis=0).astype(jnp.bfloat16))


def fft_stage_b_filter(t1, ff):
    n1, n2x2, C = t1.shape
    n2 = n2x2 // 2
    blk = lambda off: pl.BlockSpec((n1, FFT_NB, FFT_CB), lambda c, i: (0, i + off, c))
    mat = pl.BlockSpec((2 * n1, 2 * n1), lambda c, i: (0, 0))
    return pl.pallas_call(
        _stage_b_filter_kernel,
        grid=(C // FFT_CB, n2 // FFT_NB),
        in_specs=[blk(0), blk(n2 // FFT_NB), mat],
        out_specs=pl.BlockSpec((FFT_NB, 2 * n1, FFT_CB), lambda c, i: (i, 0, c)),
        out_shape=jax.ShapeDtypeStruct((n2, 2 * n1, C), jnp.float32),
        compiler_params=_params(2),
    )(t1, t1, ff)


def _filter_spectrum(p, q, s, n1):
    return (p[:n1] + q[:n1]) * s, (p[n1:] - q[n1:]) * s


def _stage_b_conv_kernel(re_ref, im_ref, p_ref, q_ref, s_ref, f_ref, ft_ref, ore_ref, oim_ref):
    n1 = re_ref.shape[0]
    s = s_ref[...]
    ff, fft_ = f_ref[...], ft_ref[...]
    bf = lambda t: t.astype(jnp.bfloat16)
    for j in range(FFT_NB):
        x = _mm1(ff, bf(jnp.concatenate([re_ref[:, j, :], im_ref[:, j, :]], axis=0)))
        hr, hi = _filter_spectrum(p_ref[j], q_ref[j], s, n1)
        xr, xi = x[:n1], x[n1:]
        z = _mm1(fft_, bf(jnp.concatenate([xr * hr - xi * hi, xr * hi + xi * hr], axis=0)))
        ore_ref[:, j, :] = z[:n1]
        oim_ref[:, j, :] = z[n1:]


def fft_stage_b_conv(t1, spec, col_p, col_q, inv_norm, ff, fft_):
    n1, n2x2, C = t1.shape
    n2 = n2x2 // 2
    ncb = C // FFT_CB
    blk = lambda off: pl.BlockSpec((n1, FFT_NB, FFT_CB), lambda c, i: (0, i + off, c))
    mat = pl.BlockSpec((2 * n1, 2 * n1), lambda c, i: (0, 0))
    sp = lambda col: pl.BlockSpec((FFT_NB, 2 * n1, FFT_CB), lambda c, i: (i, 0, col * ncb + c))
    return pl.pallas_call(
        _stage_b_conv_kernel,
        grid=(ncb, n2 // FFT_NB),
        in_specs=[blk(0), blk(n2 // FFT_NB), sp(col_p), sp(col_q), pl.BlockSpec((1, FFT_CB), lambda c, i: (0, c)),
                  mat, mat],
        out_specs=[blk(0), blk(0)],
        out_shape=[jax.ShapeDtypeStruct((n1, n2, C), jnp.float32)] * 2,
        compiler_params=_params(2),
    )(t1, t1, spec, spec, inv_norm, ff, fft_)


def _stage_a_inv_kernel(re_ref, im_ref, g_ref, u_ref, gate_ref, bias_ref, o_ref):
    for j in range(FFT_NA):
        y = _mm1(g_ref[j], jnp.concatenate([re_ref[j], im_ref[j]], axis=0).astype(jnp.bfloat16))
        o_ref[:, j, :] = gate_ref[:, j, :] * (y + u_ref[:, j, :] * bias_ref[...])


def fft_stage_a_inv(t2re, t2im, ginv, nb, u3, u_col, gate3, gate_col, bias):
    n1, nk, C = t2re.shape
    tb = pl.BlockSpec((FFT_NA, nk, CB), lambda c, i: (i, 0, c))
    gb = pl.BlockSpec((FFT_NA, nb, 2 * nk), lambda c, i: (i, 0, 0))
    ub = lambda col: pl.BlockSpec((nb, FFT_NA, CB), lambda c, i: (0, i, col + c))
    return pl.pallas_call(
        _stage_a_inv_kernel,
        grid=(C // CB, n1 // FFT_NA),
        in_specs=[tb, tb, gb, ub(u_col), ub(gate_col), pl.BlockSpec((1, CB), lambda c, i: (0, c))],
        out_specs=ub(0),
        out_shape=jax.ShapeDtypeStruct((nb, n1, C), jnp.float32),
        compiler_params=_params(2),
    )(t2re, t2im, ginv, u3, gate3, bias)


def hyena_long_conv_chain(u, L, biases, filt, colsum, tabs):
    C = GROUP_W
    n1 = FFT_N1
    n2 = 2 * L // n1
    nb = n2 // 2
    spec = fft_stage_b_filter(fft_stage_a(filt.reshape(nb, n1, filt.shape[1]), nb, 0, filt.shape[1] // CB, tabs["g"]),
                              tabs["ff"])
    s4 = colsum.reshape(HY_ORDER, 2, C)
    inv_norm = 1.0 / (s4[:, 0] + s4[:, 1])
    u3 = u.reshape(u.shape[0] // n1, n1, u.shape[1])
    y3, y_col = u3, 0
    for n in range(HY_ORDER):
        t1 = fft_stage_a(y3, nb, y_col, 1, tabs["g"])
        t2re, t2im = fft_stage_b_conv(t1, spec, 2 * n, 2 * n + 1, inv_norm[n][None, :], tabs["ff"], tabs["fft"])
        y3 = fft_stage_a_inv(t2re, t2im, tabs["ginv"], nb, y3, y_col, u3, n + 1, biases[n][None, :])
        y_col = 0
    return y3.reshape(L, C)


def dense_dft_tables(n):
    N = 2 * n
    k = jnp.arange(N, dtype=jnp.float32)[:, None]
    t = jnp.arange(n, dtype=jnp.float32)[None, :]
    ph = jnp.mod(k * t, float(N)) * (2.0 * math.pi / N)
    fd = jnp.concatenate([jnp.cos(ph), -jnp.sin(ph)], axis=0)
    return fd.astype(jnp.bfloat16), (fd.T / N).astype(jnp.bfloat16)


def _hyena_ctx_kernel(u_ref, h_ref, s_ref, bias_ref, fd_ref, ft_ref, o_ref):
    C = GROUP_W
    fd, ft = fd_ref[...], ft_ref[...]
    K = fd.shape[0] // 2
    bf = lambda t: t.astype(jnp.bfloat16)
    hs = _mm1(fd, bf(h_ref[...]))
    s = s_ref[...]
    y = u_ref[:, 0:C]
    for n in range(HY_ORDER):
        cp, cq = 2 * n * C, (2 * n + 1) * C
        inv = 1.0 / (s[:, cp:cp + C] + s[:, cq:cq + C])
        hr, hi = _filter_spectrum(hs[:, cp:cp + C], hs[:, cq:cq + C], inv, K)
        x = _mm1(fd, bf(y))
        xr, xi = x[:K], x[K:]
        conv = _mm1(ft, bf(jnp.concatenate([xr * hr - xi * hi, xr * hi + xi * hr], axis=0)))
        y = u_ref[:, (n + 1) * C:(n + 2) * C] * (conv + y * bias_ref[n:n + 1, :])
    o_ref[...] = y


def hyena_ctx(u, filt, colsum, biases, tabs):
    args = (u, filt, colsum, biases, tabs[0], tabs[1])
    return pl.pallas_call(
        _hyena_ctx_kernel,
        in_specs=[_full(a) for a in args],
        out_specs=pl.BlockSpec((u.shape[0], GROUP_W), lambda: (0, 0)),
        out_shape=jax.ShapeDtypeStruct((u.shape[0], GROUP_W), jnp.float32),
        compiler_params=pltpu.CompilerParams(vmem_limit_bytes=VMEM_LIMIT),
    )(*args)


SCAN_T = 64
GROUP_LANES = 256
RW_SUB = 64
HG_SUB = 16
LOG2_E = 1.4426950408889634


def _chunk_index(d, i, n_ctx, n_all):
    fwd = jnp.where(i < n_ctx, n_all - n_ctx + i, i - n_ctx)
    return jnp.where(d == 0, fwd, n_all - 1 - i)


def _stacking(T, S, nh, head, sign):
    G = nh * head
    nb = T // S
    n = nh * T
    rr = lax.broadcasted_iota(jnp.int32, (n, G), 0)
    same = ((rr // S) % nh) == (lax.broadcasted_iota(jnp.int32, (n, G), 1) // head)

    def bd(x):
        pieces = []
        for i in range(nb):
            pieces += [x[i * S:(i + 1) * S]] * nh
        return jnp.where(same, jnp.concatenate(pieces, axis=0), 0.0)

    def collapse(o):
        outs = []
        for i in range(nb):
            acc = o[i * nh * S:i * nh * S + S]
            for h in range(1, nh):
                acc = acc + o[i * nh * S + h * S:i * nh * S + (h + 1) * S]
            outs.append(acc)
        return jnp.concatenate(outs, axis=0)

    rt = lax.broadcasted_iota(jnp.int32, (n, n), 0)
    cs = lax.broadcasted_iota(jnp.int32, (n, n), 1)
    t_r = (rt // (nh * S)) * S + rt % S
    t_c = (cs // (nh * S)) * S + cs % S
    same_h = ((rt // S) % nh) == ((cs // S) % nh)
    before = same_h & ((t_r - t_c) * sign > 0)
    return bd, collapse, before, rt, cs, (t_r, t_c, same_h)


def _softplus(x):
    return jnp.maximum(x, 0.0) + jnp.log(1.0 + jnp.exp(-jnp.abs(x)))


def _rwkv_prep_kernel(*refs, n_lat, n_rows):
    zrefs, rest = refs[:15], refs[15:]
    (mu_ref, kk_w_ref, ka_ref, w0_ref, a0_ref, w2h_ref, w2l_ref, a2h_ref, a2l_ref, ones_ref,
     r_ref, k_ref, v_ref, g_ref, kk_ref, lw_ref, kd_ref, a_ref) = rest
    slabs = []
    for c in range(5):
        x = zrefs[3 * c][...]
        xp, xn = _neighbours(x, zrefs[3 * c + 1][...], zrefs[3 * c + 2][...], n_lat, n_rows)
        slabs.append(x + (0.5 * (xp + xn) - x) * mu_ref[:, c * CB:(c + 1) * CB])
    r, k, v, g, lora = slabs
    r_ref[...], k_ref[...], v_ref[...], g_ref[...] = r, k, v, g
    kk = k * kk_w_ref[...]
    ss = _mm2(kk * kk, ones_ref[...])
    kk_ref[...] = kk * lax.rsqrt(jnp.maximum(ss, 1e-24))
    lora_t = _split(jnp.tanh(lora))
    lora_s = _split(lora)
    for d in range(2):
        w_log = -_softplus(-(w0_ref[d:d + 1, :] + _mm3(lora_t, (w2h_ref[d], w2l_ref[d])))) - 0.5
        lw_ref[d] = -jnp.exp(w_log)
        a = _sigmoid(a0_ref[d:d + 1, :] + _mm3(lora_s, (a2h_ref[d], a2l_ref[d])))
        a_ref[d] = a
        kd_ref[d] = k * (1.0 + (a - 1.0) * ka_ref[...])


def rwkv_prep(z, mu, k_k, k_a, w0, a0, w2, a2, n_lat):
    n_rows = z.shape[0]
    tp = PREP_ROWS
    mu_p = jnp.pad(mu, (0, 5 * CB - RW_REAL))[None, :]
    w2p = jnp.zeros((2, CB, GROUP_W), jnp.float32)
    a2p = jnp.zeros((2, CB, GROUP_W), jnp.float32)
    for d in range(2):
        w2p = w2p.at[d, d * RW_LORA:(d + 1) * RW_LORA].set(w2[d])
        a2p = a2p.at[d, (2 + d) * RW_LORA:(3 + d) * RW_LORA].set(a2[d])
    w2s, a2s = _split(w2p), _split(a2p)
    small = (mu_p, k_k[None, :], k_a[None, :], w0, a0, w2s[0], w2s[1], a2s[0], a2s[1],
             _block_indicator(GROUP_W, RW_HEAD, 1.0))
    zspecs = []
    for c in range(5):
        zspecs += _halo_specs(tp, n_rows, COL_RW + c)
    one = pl.BlockSpec((tp, CB), lambda i: (i, 0))
    two = pl.BlockSpec((2, tp, CB), lambda i: (0, i, 0))
    s1 = jax.ShapeDtypeStruct((n_rows, GROUP_W), jnp.float32)
    s2 = jax.ShapeDtypeStruct((2, n_rows, GROUP_W), jnp.float32)
    return pl.pallas_call(
        functools.partial(_rwkv_prep_kernel, n_lat=n_lat, n_rows=n_rows),
        grid=(n_rows // tp,),
        in_specs=zspecs + [_full(a) for a in small],
        out_specs=[one] * 5 + [two] * 3,
        out_shape=[s1] * 5 + [s2] * 3,
        compiler_params=_params(1),
    )(*([z] * 15), *small)


def _rwkv_scan_body(refs, head):
    in_f, in_b, (of_ref, ob_ref, ht_ref) = refs[:6], refs[6:12], refs[12:]
    T, G, S = SCAN_T, GROUP_LANES, RW_SUB
    nh = G // head
    n = nh * T

    @pl.when(pl.program_id(0) == 0)
    def _():
        ht_ref[...] = jnp.zeros_like(ht_ref)

    ti = lax.broadcasted_iota(jnp.int32, (T, T), 0)
    si = lax.broadcasted_iota(jnp.int32, (T, T), 1)
    bf = lambda t: t.astype(jnp.bfloat16)
    each = lambda f, *seqs: [f(*xs) for xs in zip(*seqs)]
    n_grp = in_f[0].shape[1] // G
    lanes = [slice(grp * G, (grp + 1) * G) for grp in range(n_grp)]

    r, v, kk, lw, k, a, tri, before, incl, dest = ([] for _ in range(10))
    for (r_ref, v_ref, kk_ref, lw_ref, k_ref, a_ref), o_ref, sign in ((in_f, of_ref, 1), (in_b, ob_ref, -1)):
        bd, collapse, before_d, rt, cs, (t_r, t_c, same_h) = _stacking(T, S, nh, head, sign)
        for ls in lanes:
            r.append(r_ref[:, ls]), v.append(v_ref[:, ls]), kk.append(kk_ref[:, ls])
            lw.append(lw_ref[0, :, ls]), k.append(k_ref[0, :, ls]), a.append(a_ref[0, :, ls])
            tri.append(jnp.where((ti - si) * sign >= 0, 1.0, 0.0))
            before.append(before_d), incl.append(before_d | (rt == cs)), dest.append((o_ref, ls))
    eye = jnp.where(rt == cs, 1.0, 0.0)
    ht = [ht_ref[ch] for ch in range(len(r))]

    c = each(lambda tr, t: _dot(tr, t), tri, lw)
    ctot = each(lambda t: jnp.sum(t, axis=0, keepdims=True), lw)
    beta = each(lambda p, q: p * q, kk, a)
    einv = each(lambda t: jnp.exp(-t), c)
    lhs = each(lambda kk_, r_, c_, lw_: bf(jnp.concatenate([bd(-kk_ * jnp.exp(c_ - lw_)), bd(r_ * jnp.exp(c_))], axis=0)),
               kk, r, c, lw)
    rhs = each(lambda k_, b_, e_: bf(jnp.concatenate([bd(k_ * e_), bd(b_ * e_)], axis=0)), k, beta, einv)
    m = each(lambda p, q: _mm1(p, q, NT), lhs, rhs)
    g = each(lambda p, h: _mm1(p, bf(h), NT), lhs, ht)
    v_b = each(lambda t: bf(bd(t)), v)
    x = each(lambda g_, m_, v_, bm: g_[:n] + _mm1(bf(jnp.where(bm, m_[:n, :n], 0.0)), v_), g, m, v_b, before)
    lb = each(lambda m_, bm: jnp.where(bm, m_[:n, n:], 0.0), m, before)
    differ = jnp.where(same_h, t_r ^ t_c, 0)
    half = lambda s: (differ >= s) & (differ < 2 * s)
    dmat = each(lambda t: eye + jnp.where(half(1), t, 0.0), lb)
    s = 2
    while s < T:
        coupling = half(s)
        dmat = each(lambda d_, t: d_ + _mm1(bf(_mm1(bf(d_), bf(jnp.where(coupling, t, 0.0)))), bf(d_)), dmat, lb)
        s *= 2
    u_b = each(lambda d_, t: bf(_mm1(bf(d_), bf(t))), dmat, x)
    o = each(lambda g_, m_, v_, u_, im: (g_[n:] + _mm1(bf(jnp.where(im, m_[n:, :n], 0.0)), v_))
             + _mm1(bf(jnp.where(im, m_[n:, n:], 0.0)), u_), g, m, v_b, u_b, incl)
    efin = each(lambda ct, c_: jnp.exp(ct - c_), ctot, c)
    ht_new = each(lambda h, ct, v_, u_, k_, b_, e_: (h * jnp.exp(ct) + _mm1(v_, bf(bd(k_ * e_)), TN))
                  + _mm1(u_, bf(bd(b_ * e_)), TN), ht, ctot, v_b, u_b, k, beta, efin)
    for ch, (o_ref, ls) in enumerate(dest):
        o_ref[:, ls] = collapse(o[ch])
        ht_ref[ch] = ht_new[ch]


def rope_tables(n_ctx, L):
    d_axis = RT_HEAD // 2
    half = d_axis // 2
    inv = ROPE_BASE ** (-jnp.arange(0, d_axis, 2, dtype=jnp.float32) / d_axis)
    t = jnp.arange(L)
    pos = jnp.stack([(t // GRID_W).astype(jnp.float32), (t % GRID_W).astype(jnp.float32)], axis=1)
    j = jnp.arange(RT_HEAD)
    ang = pos[:, j // d_axis] * inv[j % half][None, :]
    sgn = jnp.where((j % d_axis) < half, -1.0, 1.0)[None, :]
    cos = jnp.concatenate([jnp.cos(ang), jnp.ones((n_ctx, RT_HEAD), jnp.float32)], axis=0)
    sin = jnp.concatenate([jnp.sin(ang) * sgn, jnp.zeros((n_ctx, RT_HEAD), jnp.float32)], axis=0)
    return jnp.tile(cos, (1, RT_HEADS)), jnp.tile(sin, (1, RT_HEADS))


def _rotate(x, cos, sin):
    G = x.shape[1]
    half = RT_HEAD // 4
    lane = lax.broadcasted_iota(jnp.int32, x.shape, 1)
    partner = jnp.where((lane % (2 * half)) < half, pltpu.roll(x, G - half, axis=1), pltpu.roll(x, half, axis=1))
    return x * cos + partner * sin


def _retention_scan_body(q_ref, k_ref, v_ref, cos_ref, sin_ref, lg_ref, o_ref, st_ref, *, head, d):
    T, G = SCAN_T, GROUP_LANES
    nh = G // head
    sign = 1 - 2 * d

    t = lax.broadcasted_iota(jnp.int32, (T, 1), 0)
    pos = (t + d * (T - 1 - 2 * t) + 1).astype(jnp.float32)
    bd, collapse, before, rt, cs, _ = _stacking(T, T, nh, head, sign)
    incl = before | (rt == cs)
    bf = lambda x: x.astype(jnp.bfloat16)

    groups = range(q_ref.shape[1] // G)
    lanes = [slice(grp * G, (grp + 1) * G) for grp in groups]
    each = lambda f, *seqs: [f(*xs) for xs in zip(*seqs)]
    st = [st_ref[grp] for grp in groups]
    cos, sin, lg = ([ref[:, ls] for ls in lanes] for ref in (cos_ref, sin_ref, lg_ref))
    q = each(_rotate, [q_ref[:, ls] for ls in lanes], cos, sin)
    k = each(lambda x, c_, s_: _rotate(x, c_, s_) * (head ** -0.5), [k_ref[:, ls] for ls in lanes], cos, sin)
    c = each(lambda lg_: pos * lg_, lg)
    q_b = each(lambda q_, c_: bf(bd(q_ * jnp.exp(c_))), q, c)
    kt_b = each(lambda k_, c_: bf(bd(k_ * jnp.exp(-c_))), k, c)
    v_b = [bf(bd(v_ref[:, ls])) for ls in lanes]
    scores = each(lambda q_, k_: bf(jnp.where(incl, _mm1(q_, k_, NT), 0.0)), q_b, kt_b)
    inter = each(lambda q_, s_: _mm1(q_, bf(s_), NT), q_b, st)
    o = each(lambda g_, s_, v_: g_ + _mm1(s_, v_), inter, scores, v_b)
    st_new = each(lambda s_, lg_, v_, k_, c_: s_ * jnp.exp(float(T) * lg_)
                  + _mm1(v_, bf(bd(k_ * jnp.exp(float(T) * lg_ - c_))), TN), st, lg, v_b, k, c)
    for grp in groups:
        o_ref[:, lanes[grp]] = collapse(o[grp])
        st_ref[grp] = st_new[grp]


def _gla_scan_body(q_ref, f_ref, i_ref, lb_ref, o_ref, st_ref, *, head, d):
    T, S = SCAN_T, HG_SUB
    W = q_ref.shape[-1]
    nh = W // head
    sign = 1 - 2 * d

    ti = lax.broadcasted_iota(jnp.int32, (S, S), 0)
    si = lax.broadcasted_iota(jnp.int32, (S, S), 1)
    tri_incl = jnp.where((ti - si) * sign >= 0, 1.0, 0.0)
    row = lax.broadcasted_iota(jnp.int32, (S, 1), 0)
    lb = lb_ref[0]

    states = [st_ref[h] for h in range(nh)]
    for j in range(T // S):
        jb = j + d * (T // S - 1 - 2 * j)
        rows = pl.ds(jb * S, S)
        q = q_ref[rows, :]
        q = q * _sigmoid(q)
        v = i_ref[rows, :]
        gate = lb + (1.0 - lb) * _sigmoid(f_ref[rows, :])
        lf = jnp.log(jnp.maximum(gate, HG_MIN_GATE))
        k = 1.0 - gate
        b = _dot(tri_incl, lf)
        btot = jnp.sum(lf, axis=0, keepdims=True)
        qe = q * jnp.exp(b)
        ke = k * jnp.exp(btot - b)
        outs = []
        for h in range(nh):
            ls = slice(h * head, (h + 1) * head)
            qh, kh, vh = q[:, ls], k[:, ls], v[:, ls]
            bh = b[:, ls] * LOG2_E
            st = states[h]
            o = _mm1(qe[:, ls].astype(jnp.bfloat16), st.astype(jnp.bfloat16), NT)
            for s in range(S):
                e = jnp.exp2(bh - bh[s:s + 1, :])
                a_s = jnp.sum(qh * kh[s:s + 1, :] * e, axis=-1, keepdims=True)
                a_s = jnp.where((row - s) * sign >= 0, a_s, 0.0)
                o = o + a_s * vh[s:s + 1, :]
            outs.append(o)
            states[h] = st * jnp.exp(btot[:, ls]) + _mm1(vh.astype(jnp.bfloat16), ke[:, ls].astype(jnp.bfloat16), TN)
        o_ref[rows, :] = jnp.concatenate(outs, axis=1)
    for h in range(nh):
        st_ref[h] = states[h]


N_RW, N_RT, N_HG = 6, 6, 4


def _scans_kernel(*refs):
    n_in = 2 * (N_RW + N_RT + N_HG)
    ins, (rw_of, rw_ob, rt_of, rt_ob, hg_of, hg_ob, rw_st, rt_st, hg_st) = refs[:n_in], refs[n_in:]
    rw_in, rt_in, hg_in = ins[:2 * N_RW], ins[2 * N_RW:2 * (N_RW + N_RT)], ins[2 * (N_RW + N_RT):]

    @pl.when(pl.program_id(0) == 0)
    def _():
        rt_st[...] = jnp.zeros_like(rt_st)
        hg_st[...] = jnp.zeros_like(hg_st)

    n_rt, n_hg = rt_st.shape[0] // 2, hg_st.shape[0] // 2
    _rwkv_scan_body(list(rw_in) + [rw_of, rw_ob, rw_st], RW_HEAD)
    for d, (rt_o, hg_o) in enumerate(((rt_of, hg_of), (rt_ob, hg_ob))):
        _gla_scan_body(*hg_in[d * N_HG:(d + 1) * N_HG], hg_o, hg_st.at[d * n_hg:(d + 1) * n_hg], head=HG_HEAD, d=d)
        _retention_scan_body(*rt_in[d * N_RT:(d + 1) * N_RT], rt_o, rt_st.at[d * n_rt:(d + 1) * n_rt],
                             head=RT_HEAD, d=d)


def recurrent_scans(z, rw, cos, sin, lg, lb2, n_ctx_rows):
    N = z.shape[0]
    W = GROUP_W
    T = SCAN_T
    n_all, n_ctx = N // T, n_ctx_rows // T
    chunk = lambda d: (lambda i: (_chunk_index(d, i, n_ctx, n_all), 0))
    row = lambda d: pl.BlockSpec((T, W), chunk(d))
    per_dir = lambda d: pl.BlockSpec((1, T, W), lambda i: (d, _chunk_index(d, i, n_ctx, n_all), 0))
    zc = lambda d, col: pl.BlockSpec((T, CB), lambda i: (_chunk_index(d, i, n_ctx, n_all), col))
    rw_specs = lambda d: [row(d)] * 3 + [per_dir(d)] * 3
    rt_specs = lambda d: [zc(d, COL_RT), zc(d, COL_RT + 1), zc(d, COL_RT + 2), row(d), row(d),
                          pl.BlockSpec((1, W), lambda i: (0, 0))]
    hg_specs = lambda d: [zc(d, COL_HG), zc(d, COL_HG + 1 + d), zc(d, COL_HG + 3),
                          pl.BlockSpec((1, 1, W), lambda i: (d, 0, 0))]
    shape = jax.ShapeDtypeStruct((N, W), jnp.float32)
    return pl.pallas_call(
        _scans_kernel,
        grid=(n_all,),
        in_specs=rw_specs(0) + rw_specs(1) + rt_specs(0) + rt_specs(1) + hg_specs(0) + hg_specs(1),
        out_specs=[row(0), row(1)] * 3,
        out_shape=[shape] * 6,
        scratch_shapes=[pltpu.VMEM((2 * (W // GROUP_LANES), GROUP_LANES, GROUP_LANES), jnp.float32),
                        pltpu.VMEM((2 * (W // GROUP_LANES), GROUP_LANES, GROUP_LANES), jnp.float32),
                        pltpu.VMEM((2 * (W // HG_HEAD), HG_HEAD, HG_HEAD), jnp.float32)],
        compiler_params=_params(1),
    )(*(list(rw) * 2), *([z, z, z, cos, sin, lg] * 2), *([z, z, z, lb2] * 2))


def _outproj_kernel(hy_ref, rwof_ref, rwob_ref, r_ref, k_ref, v_ref, rwg_ref, rtof_ref, rtob_ref, rtg_ref,
                    hgof_ref, hgob_ref, hgg_ref,
                    rk_ref, gng_ref, gnb_ref, hgn_ref, avg64_ref, avg128_ref,
                    w_ref, x_ref, gate_ref, g_ref, b_ref, o_ref, *, n_lat):
    avg64, avg128 = avg64_ref[...], avg128_ref[...]

    def head_norm(o, avg, eps, centre):
        if centre:
            o = o - _mm2(o, avg)
        return o * lax.rsqrt(_mm2(o * o, avg) + eps)

    silu = lambda t: t * _sigmoid(t)
    y_rw = head_norm(rwof_ref[...] + rwob_ref[...], avg64, RW_GN_EPS, True) * gng_ref[...] + gnb_ref[...]
    bonus = (float(RW_HEAD) * _mm2(r_ref[...] * k_ref[...] * rk_ref[...], avg64)) * v_ref[...]
    y_rw = (y_rw + bonus) * _sigmoid(rwg_ref[...])
    y_rt = head_norm(rtof_ref[...] + rtob_ref[...], avg64, HEAD_NORM_EPS, True) * silu(rtg_ref[...])
    y_hg = head_norm(hgof_ref[...] + hgob_ref[...], avg128, HEAD_NORM_EPS, False) * hgn_ref[...] * silu(hgg_ref[...])
    y = None
    for m, ym in enumerate((hy_ref[...], y_rw, y_rt, y_hg)):
        part = jnp.dot(ym.astype(jnp.bfloat16), w_ref[m * GROUP_W:(m + 1) * GROUP_W, :],
                       preferred_element_type=jnp.float32)
        y = part if y is None else y + part
    is_ctx = _ctx_rows(x_ref.shape[0], n_lat)
    r = ALPHA * x_ref[...] + _mod_row(gate_ref, is_ctx) * y
    o_ref[...] = _ln_rows(r) * g_ref[...] + b_ref[...]


def outproj_deepnorm(y_hy, scans, rw_r, rw_k, rw_v, rw_g, z, r_k, gn_g, gn_b, hg_norm_g,
                     w_bf16, x, mod, g, b, n_lat, m):
    tm = OUT_TM
    one = pl.BlockSpec((tm, GROUP_W), lambda i: (i, 0))
    zc = lambda col: pl.BlockSpec((tm, CB), lambda i: (i, col))
    row = pl.BlockSpec((tm, D_MODEL), lambda i: (i, 0))
    small = (r_k, gn_g, gn_b, hg_norm_g, _block_indicator(GROUP_W, RW_HEAD, 1.0 / RW_HEAD),
             _block_indicator(GROUP_W, HG_HEAD, 1.0 / HG_HEAD))
    return pl.pallas_call(
        functools.partial(_outproj_kernel, n_lat=n_lat),
        grid=(m // tm,),
        in_specs=[one] * 9 + [zc(COL_RT + 3), one, one, zc(COL_HG + 4)]
        + [_full(a) for a in small]
        + [_full(w_bf16), row, pl.BlockSpec((8, D_MODEL), lambda i: (0, 2)), _full(g), _full(b)],
        out_specs=row,
        out_shape=jax.ShapeDtypeStruct((m, D_MODEL), jnp.float32),
        compiler_params=_params(1),
    )(y_hy, scans[0], scans[1], rw_r, rw_k, rw_v, rw_g, scans[2], scans[3], z, scans[4], scans[5], z,
      *small, w_bf16, x, mod, g, b)


def _ffn_kernel(x_ref, sh_ref, sc_ref, w1_ref, w3_ref, w2_ref, gate_ref, g_ref, b_ref, o_ref, h_ref, acc_ref, *, n_lat):
    j = pl.program_id(1)

    @pl.when(j == 0)
    def _():
        is_ctx = _ctx_rows(x_ref.shape[0], n_lat)
        h = _ln_rows(x_ref[...]) * (1.0 + _mod_row(sc_ref, is_ctx)) + _mod_row(sh_ref, is_ctx)
        h_ref[...] = h.astype(jnp.bfloat16)
        acc_ref[...] = jnp.zeros_like(acc_ref)

    h = h_ref[...]
    a = jnp.dot(h, w1_ref[...], preferred_element_type=jnp.float32)
    u = jnp.dot(h, w3_ref[...], preferred_element_type=jnp.float32)
    s = (a * _sigmoid(a) * u).astype(jnp.bfloat16)
    acc_ref[...] += jnp.dot(s, w2_ref[...], preferred_element_type=jnp.float32)

    @pl.when(j == pl.num_programs(1) - 1)
    def _():
        is_ctx = _ctx_rows(x_ref.shape[0], n_lat)
        r = ALPHA * x_ref[...] + _mod_row(gate_ref, is_ctx) * acc_ref[...]
        o_ref[...] = _ln_rows(r) * g_ref[...] + b_ref[...]


def ffn_deepnorm(x, mod, w1, w3, w2, g, b, n_lat, tm):
    m = x.shape[0]
    row = pl.BlockSpec((tm, D_MODEL), lambda i, j: (i, 0))
    modc = lambda c: pl.BlockSpec((8, D_MODEL), lambda i, j: (0, c))
    vec = pl.BlockSpec((1, D_MODEL), lambda i, j: (0, 0))
    return pl.pallas_call(
        functools.partial(_ffn_kernel, n_lat=n_lat),
        grid=(m // tm, FFN_HIDDEN // FFN_TF),
        in_specs=[row, modc(3), modc(4),
                  pl.BlockSpec((D_MODEL, FFN_TF), lambda i, j: (0, j)),
                  pl.BlockSpec((D_MODEL, FFN_TF), lambda i, j: (0, j)),
                  pl.BlockSpec((FFN_TF, D_MODEL), lambda i, j: (j, 0)),
                  modc(5), vec, vec],
        out_specs=row,
        out_shape=jax.ShapeDtypeStruct((m, D_MODEL), jnp.float32),
        scratch_shapes=[pltpu.VMEM((tm, D_MODEL), jnp.bfloat16),
                        pltpu.VMEM((tm, D_MODEL), jnp.float32)],
        compiler_params=_params(2),
    )(x, mod, mod, w1, w3, w2, mod, g, b)


def kernel(x, c, ctx, c_ctx, ada_w, ada_b, w_in, w_out, ln_g, ln_b, hy_conv_w, hy_conv_b, hy_w1, hy_b1, hy_w2, hy_b2, hy_w3, hy_b3, hy_freq, hy_bias, rw_mu, rw_w0, rw_w2, rw_a0, rw_a2, rw_k_k, rw_k_a, rw_r_k, rw_gn_g, rw_gn_b, hg_lb_raw, hg_norm_g, ffn_w1, ffn_w3, ffn_w2):
    L, n_ctx = x.shape[1], ctx.shape[1]
    sm = jax.nn.softmax(hg_lb_raw.astype(jnp.float32), axis=1)
    lower_bounds = jnp.cumsum(sm, axis=1) - sm[:, :1]
    log_gamma = jnp.log1p(-jnp.exp2(-5.0 - jnp.arange(RT_HEADS, dtype=jnp.float32)))
    lg = jnp.repeat(log_gamma, RT_HEAD)[None, :]
    cos, sin = rope_tables(n_ctx, L)
    fft_tabs = fft_tables(FFT_N1, 2 * L // FFT_N1)
    ctx_tabs = dense_dft_tables(n_ctx)

    c8 = jnp.zeros((8, D_MODEL), jnp.float32).at[0].set(c[0]).at[1].set(c_ctx)
    xs = jnp.concatenate([x[0], ctx[0]], axis=0)
    n_rows = L + n_ctx
    for l in range(DEPTH):
        with_ctx = l < DEPTH - 1
        mod = ada_modulation(c8, ada_w, l, ada_b[l][None, :])
        w_out_b = w_out[l].astype(jnp.bfloat16)
        w1_b, w3_b, w2_b = (w[l].astype(jnp.bfloat16) for w in (ffn_w1, ffn_w3, ffn_w2))

        z = modulated_projection(xs, mod, w_in, l, L)

        u = hyena_conv3(z, hy_conv_w[l], hy_conv_b[l][None, :], L)
        hy_w = (hy_w1[l], hy_b1[l], hy_w2[l], hy_b2[l], hy_w3[l], hy_b3[l], hy_freq[l])
        filt, colsum = hyena_filter_bank(L, *hy_w)
        y_lat = hyena_long_conv_chain(u, L, hy_bias[l], filt, colsum, fft_tabs)
        if with_ctx:
            filt_c, colsum_c = hyena_filter_bank(n_ctx, *hy_w)
            y_ctx = hyena_ctx(u[L:], filt_c, colsum_c, hy_bias[l], ctx_tabs)
        else:
            y_ctx = jnp.zeros((n_ctx, GROUP_W), jnp.float32)
        y_hy = jnp.concatenate([y_lat, y_ctx], axis=0)

        rw_r, rw_k, rw_v, rw_g, rw_kk, rw_lw, rw_kd, rw_a = rwkv_prep(
            z, rw_mu[l], rw_k_k[l], rw_k_a[l], rw_w0[l], rw_a0[l], rw_w2[l], rw_a2[l], L)
        scans = recurrent_scans(z, (rw_r, rw_v, rw_kk, rw_lw, rw_kd, rw_a), cos, sin, lg,
                                lower_bounds[:, l][:, None, :], n_ctx)

        xs = outproj_deepnorm(y_hy, scans, rw_r, rw_k, rw_v, rw_g, z,
                              rw_r_k[l].reshape(1, GROUP_W), rw_gn_g[l][None, :], rw_gn_b[l][None, :],
                              hg_norm_g[l][None, :], w_out_b, xs, mod, ln_g[l, 0][None, :], ln_b[l, 0][None, :],
                              L, n_rows if with_ctx else L)
        xs = ffn_deepnorm(xs, mod, w1_b, w3_b, w2_b, ln_g[l, 1][None, :], ln_b[l, 1][None, :], L,
                          FFN_TM if with_ctx else FFN_LAST_TM)
    return xs[None]
```

```python
import functools
import math

import jax
import jax.numpy as jnp
from jax import lax
from jax.experimental import pallas as pl
from jax.experimental.pallas import tpu as pltpu

D_MODEL = 2048
DEPTH = 2
GRID_W = 64
N_MIXERS = 4
GROUP_W = D_MODEL // N_MIXERS
HY_ORDER = 2
HY_EMB = 33
HY_FAST_DECAY = 0.3
HY_SLOW_DECAY = 1.5
HY_TARGET = 1e-2
RW_HEAD = 64
RW_LORA = 96
RW_GN_EPS = 64e-5
RT_HEAD = 64
RT_HEADS = GROUP_W // RT_HEAD
ROPE_BASE = 10000.0
HG_HEAD = 128
HG_MIN_GATE = 1e-30
FFN_HIDDEN = 5632
ALPHA = (2 * DEPTH) ** 0.25
LN_EPS = 1e-6
HEAD_NORM_EPS = 1e-6

LANES = 128
CB = 512
COL_HY = 0
COL_RW = 3
COL_RT = 8
COL_HG = 12
P_IN_PAD = 17 * CB
RW_REAL = 4 * GROUP_W + 4 * RW_LORA
PROJ_TM = 1408
FFN_TM = 768
FFN_LAST_TM = 512
OUT_TM = 256
FFN_TF = 512
PREP_ROWS = 384
HALO = 8
VMEM_LIMIT = 56 * 1024 * 1024

HI = lax.Precision.HIGHEST
NN = ((1,), (0,))
NT = ((1,), (1,))
TN = ((0,), (0,))


def _params(n_axes):
    return pltpu.CompilerParams(dimension_semantics=("arbitrary",) * n_axes, vmem_limit_bytes=VMEM_LIMIT)


def _full(a):
    return pl.BlockSpec(a.shape, lambda *_: (0,) * a.ndim)


def _dot(a, b):
    return jnp.dot(a, b, precision=HI, preferred_element_type=jnp.float32)


def _split(x):
    hi = x.astype(jnp.bfloat16)
    lo = (x - hi.astype(jnp.float32)).astype(jnp.bfloat16)
    return hi, lo


def _mm3(a, b, dims=NN):
    d = lambda p, q: lax.dot_general(p, q, (dims, ((), ())), preferred_element_type=jnp.float32)
    return d(a[0], b[0]) + (d(a[0], b[1]) + d(a[1], b[0]))


def _mm1(a, b, dims=NN):
    return lax.dot_general(a, b, (dims, ((), ())), preferred_element_type=jnp.float32)


def _mm2(x, b):
    hi, lo = _split(x)
    d = functools.partial(jnp.dot, preferred_element_type=jnp.float32)
    return d(hi, b) + d(lo, b)


def _ln_rows(x):
    mu = jnp.mean(x, axis=-1, keepdims=True)
    xc = x - mu
    var = jnp.mean(xc * xc, axis=-1, keepdims=True)
    return xc * lax.rsqrt(var + LN_EPS)


def _sigmoid(x):
    return 1.0 / (1.0 + jnp.exp(-x))


def _ctx_rows(tm, n_lat):
    return (pl.program_id(0) * tm + lax.broadcasted_iota(jnp.int32, (tm, 1), 0)) >= n_lat


def _mod_row(ref, is_ctx):
    return jnp.where(is_ctx, ref[1:2, :], ref[0:1, :])


def _block_indicator(width, head, value):
    i = jnp.arange(width) // head
    return jnp.where(i[:, None] == i[None, :], value, 0.0).astype(jnp.bfloat16)


def _ada_kernel(c_ref, w_ref, b_ref, o_ref):
    c = c_ref[...]
    h = c * _sigmoid(c)
    o_ref[...] = jnp.dot(h.astype(jnp.bfloat16), w_ref[...].astype(jnp.bfloat16),
                         preferred_element_type=jnp.float32) + b_ref[...]


def ada_modulation(c8, w, layer, b):
    n = w.shape[2]
    tn = 1024
    return pl.pallas_call(
        _ada_kernel,
        grid=(n // tn,),
        in_specs=[pl.BlockSpec((8, D_MODEL), lambda j: (0, 0)),
                  pl.BlockSpec((None, D_MODEL, tn), lambda j: (layer, 0, j)),
                  pl.BlockSpec((1, tn), lambda j: (0, j))],
        out_specs=pl.BlockSpec((8, tn), lambda j: (0, j)),
        out_shape=jax.ShapeDtypeStruct((8, n), jnp.float32),
        compiler_params=_params(1),
    )(c8, w, b)


W_SUB = CB // LANES
W_PAD_PIECE = (COL_RW * CB + RW_REAL) // LANES


def _proj_kernel(x_ref, sh_ref, sc_ref, *refs, n_lat):
    w_refs, (o_ref, h_ref) = refs[:W_SUB], refs[W_SUB:]
    j = pl.program_id(1)

    @pl.when(j == 0)
    def _():
        is_ctx = _ctx_rows(x_ref.shape[0], n_lat)
        h = _ln_rows(x_ref[...]) * (1.0 + _mod_row(sc_ref, is_ctx)) + _mod_row(sh_ref, is_ctx)
        h_ref[...] = h.astype(jnp.bfloat16)

    pieces = [w_refs[k][...] for k in range(W_SUB)]
    for k in range(W_SUB):
        if (W_PAD_PIECE - k) % W_SUB == 0:
            pieces[k] = jnp.where(j * W_SUB + k == W_PAD_PIECE, 0.0, pieces[k])
    w = jnp.concatenate(pieces, axis=1).astype(jnp.bfloat16)
    o_ref[...] = jnp.dot(h_ref[...], w, preferred_element_type=jnp.float32)


def modulated_projection(x, mod, w, layer, n_lat):
    m = x.shape[0]
    tm = PROJ_TM

    def piece(k):
        def index(i, j):
            p = j * W_SUB + k
            return layer, 0, jnp.where(p < W_PAD_PIECE, p, p - 1)
        return pl.BlockSpec((None, D_MODEL, LANES), index)

    return pl.pallas_call(
        functools.partial(_proj_kernel, n_lat=n_lat),
        grid=(m // tm, P_IN_PAD // CB),
        in_specs=[pl.BlockSpec((tm, D_MODEL), lambda i, j: (i, 0)),
                  pl.BlockSpec((8, D_MODEL), lambda i, j: (0, 0)),
                  pl.BlockSpec((8, D_MODEL), lambda i, j: (0, 1))] + [piece(k) for k in range(W_SUB)],
        out_specs=pl.BlockSpec((tm, CB), lambda i, j: (i, j)),
        out_shape=jax.ShapeDtypeStruct((m, P_IN_PAD), jnp.float32),
        scratch_shapes=[pltpu.VMEM((tm, D_MODEL), jnp.bfloat16)],
        compiler_params=_params(2),
    )(x, mod, mod, *([w] * W_SUB))


def _halo_specs(tp, n_rows, col, width=CB):
    per = tp // HALO
    last = n_rows // HALO - 1
    main = pl.BlockSpec((tp, width), lambda i: (i, col))
    prev = pl.BlockSpec((HALO, width), lambda i: (jnp.maximum(i * per - 1, 0), col))
    nxt = pl.BlockSpec((HALO, width), lambda i: (jnp.minimum((i + 1) * per, last), col))
    return [main, prev, nxt]


def _neighbours(x, prev, nxt, n_lat, n_rows):
    tp = x.shape[0]
    loc = lax.broadcasted_iota(jnp.int32, (tp, 1), 0)
    row = pl.program_id(0) * tp + loc
    xp = jnp.where(loc == 0, prev[HALO - 1:HALO, :], pltpu.roll(x, 1, axis=0))
    xp = jnp.where((row == 0) | (row == n_lat), 0.0, xp)
    xn = jnp.where(loc == tp - 1, nxt[0:1, :], pltpu.roll(x, tp - 1, axis=0))
    xn = jnp.where((row == n_lat - 1) | (row == n_rows - 1), 0.0, xn)
    return xp, xn


def _conv3_kernel(z_ref, zp_ref, zn_ref, w_ref, b_ref, o_ref, *, n_lat, n_rows):
    x = z_ref[...]
    xp, xn = _neighbours(x, zp_ref[...], zn_ref[...], n_lat, n_rows)
    w = w_ref[...]
    o_ref[...] = xp * w[0:1, :] + x * w[1:2, :] + xn * w[2:3, :] + b_ref[...]


def hyena_conv3(z, w, b, n_lat):
    n_rows = z.shape[0]
    tp = PREP_ROWS
    width = w.shape[1]
    return pl.pallas_call(
        functools.partial(_conv3_kernel, n_lat=n_lat, n_rows=n_rows),
        grid=(n_rows // tp,),
        in_specs=_halo_specs(tp, n_rows, COL_HY, width) + [_full(w), _full(b)],
        out_specs=pl.BlockSpec((tp, width), lambda i: (i, 0)),
        out_shape=jax.ShapeDtypeStruct((n_rows, width), jnp.float32),
        compiler_params=_params(1),
    )(z, z, z, w, b)


HY_EMB_PAD = 40
HY_FILTER_ROWS = 512


def hyena_features(L):
    t = jnp.linspace(0.0, 1.0, L, dtype=jnp.float32)[:, None]
    n_bands = (HY_EMB - 1) // 2
    f = jnp.linspace(1e-4, n_bands - 1, n_bands, dtype=jnp.float32)[None, :]
    ang = (2.0 * math.pi / L) * jnp.arange(L, dtype=jnp.float32)[:, None] * f
    z = jnp.concatenate([t, jnp.cos(ang), -jnp.sin(ang)], -1)
    return jnp.pad(z, ((0, 0), (0, HY_EMB_PAD - HY_EMB)))


def _filter_kernel(z_ref, w1_ref, b1_ref, w2_ref, b2_ref, w3_ref, b3_ref, fr_ref, dl_ref, h_ref, s_ref):
    i = pl.program_id(0)
    z = z_ref[...]
    fr = fr_ref[...]
    h = jnp.sin(fr * (_dot(z, w1_ref[...]) + b1_ref[...]))
    h = jnp.sin(fr * (_dot(h, w2_ref[...]) + b2_ref[...]))
    h = _mm1(h.astype(jnp.bfloat16), w3_ref[...].astype(jnp.bfloat16)) + b3_ref[...]
    win = jnp.exp(-z[:, 0:1] * dl_ref[...])
    h = h * jnp.concatenate([win] * (h.shape[1] // win.shape[1]), axis=1)

    @pl.when(i == 0)
    def _():
        s_ref[...] = jnp.zeros_like(s_ref)

    s_ref[...] += jnp.sum(jnp.abs(h), axis=0, keepdims=True)
    row = lax.broadcasted_iota(jnp.int32, h.shape, 0) + i * h.shape[0]
    col = lax.broadcasted_iota(jnp.int32, h.shape, 1)
    neg = (col // GROUP_W) % 2 == 1
    h_ref[...] = jnp.where(neg & (row == 0), 0.0, h)


def hyena_filter_bank(L, w1, b1, w2, b2, w3, b3, freq):
    z = hyena_features(L)
    w1p = jnp.pad(w1, ((0, HY_EMB_PAD - HY_EMB), (0, 0)))
    max_decay = math.log(HY_TARGET) / HY_FAST_DECAY
    min_decay = math.log(HY_TARGET) / HY_SLOW_DECAY
    deltas = jnp.abs(jnp.linspace(min_decay, max_decay, GROUP_W, dtype=jnp.float32))[None, :]
    n = w3.shape[1]
    tr = min(L, HY_FILTER_ROWS)
    args = (z, w1p, b1[None, :], w2, b2[None, :], w3, b3[None, :], freq[None, :], deltas)
    return pl.pallas_call(
        _filter_kernel,
        grid=(L // tr,),
        in_specs=[pl.BlockSpec((tr, HY_EMB_PAD), lambda i: (i, 0))] + [_full(a) for a in args[1:]],
        out_specs=[pl.BlockSpec((tr, n), lambda i: (i, 0)), pl.BlockSpec((1, n), lambda i: (0, 0))],
        out_shape=[jax.ShapeDtypeStruct((L, n), jnp.float32), jax.ShapeDtypeStruct((1, n), jnp.float32)],
        compiler_params=_params(1),
    )(*args)


FFT_NB = 8
FFT_NA = 32
FFT_CB = 512
FFT_N1 = 128


def fft_tables(n1, n2):
    N = n1 * n2
    nk = -(-(n2 // 2 + 1) // FFT_NB) * FFT_NB
    a = jnp.arange(n1, dtype=jnp.float32)[:, None, None]
    k2 = jnp.arange(nk, dtype=jnp.float32)[None, :, None]
    b = jnp.arange(n2 // 2, dtype=jnp.float32)[None, None, :]
    ph = (jnp.mod(a * k2, float(N)) / N + jnp.mod(b * k2, float(n2)) / n2) * (-2.0 * math.pi)
    g = jnp.concatenate([jnp.cos(ph), jnp.sin(ph)], axis=1)
    kk = jnp.arange(nk)
    weight = jnp.where((kk == 0) | (kk == n2 // 2), 1.0, jnp.where(kk < n2 // 2, 2.0, 0.0))
    ginv = jnp.transpose(g, (0, 2, 1)) * (jnp.tile(weight, 2) / N)
    k1 = jnp.arange(n1, dtype=jnp.float32)[:, None]
    aa = jnp.arange(n1, dtype=jnp.float32)[None, :]
    f = jnp.mod(k1 * aa, float(n1)) * (-2.0 * math.pi / n1)
    fr, fi = jnp.cos(f), jnp.sin(f)
    ff = jnp.concatenate([jnp.concatenate([fr, -fi], axis=1), jnp.concatenate([fi, fr], axis=1)], axis=0)
    bf = lambda t: t.astype(jnp.bfloat16)
    return {"g": bf(g), "ginv": bf(ginv), "ff": bf(ff), "fft": bf(ff.T)}


def _stage_a_kernel(u_ref, g_ref, o_ref):
    for j in range(FFT_NA):
        o_ref[j] = _mm1(g_ref[j], u_ref[:, j, :].astype(jnp.bfloat16))


def fft_stage_a(u3, nb, col, ncol, g):
    n1 = u3.shape[1]
    rows = g.shape[1]
    gspec = pl.BlockSpec((FFT_NA, rows, nb), lambda c, i: (i, 0, 0))
    return pl.pallas_call(
        _stage_a_kernel,
        grid=(ncol, n1 // FFT_NA),
        in_specs=[pl.BlockSpec((nb, FFT_NA, CB), lambda c, i: (0, i, col + c)), gspec],
        out_specs=pl.BlockSpec((FFT_NA, rows, CB), lambda c, i: (i, 0, c)),
        out_shape=jax.ShapeDtypeStruct((n1, rows, ncol * CB), jnp.float32),
        compiler_params=_params(2),
    )(u3, g)


def _stage_b_filter_kernel(re_ref, im_ref, f_ref, o_ref):
    ff = f_ref[...]
    for j in range(FFT_NB):
        o_ref[j] = _mm1(ff, jnp.concatenate([re_ref[:, j, :], im_ref[:, j, :]], axis=0).astype(jnp.bfloat16))


def fft_stage_b_filter(t1, ff):
    n1, n2x2, C = t1.shape
    n2 = n2x2 // 2
    blk = lambda off: pl.BlockSpec((n1, FFT_NB, FFT_CB), lambda c, i: (0, i + off, c))
    mat = pl.BlockSpec((2 * n1, 2 * n1), lambda c, i: (0, 0))
    return pl.pallas_call(
        _stage_b_filter_kernel,
        grid=(C // FFT_CB, n2 // FFT_NB),
        in_specs=[blk(0), blk(n2 // FFT_NB), mat],
        out_specs=pl.BlockSpec((FFT_NB, 2 * n1, FFT_CB), lambda c, i: (i, 0, c)),
        out_shape=jax.ShapeDtypeStruct((n2, 2 * n1, C), jnp.float32),
        compiler_params=_params(2),
    )(t1, t1, ff)


def _filter_spectrum(p, q, s, n1):
    return (p[:n1] + q[:n1]) * s, (p[n1:] - q[n1:]) * s


def _stage_b_conv_kernel(re_ref, im_ref, p_ref, q_ref, s_ref, f_ref, ft_ref, ore_ref, oim_ref):
    n1 = re_ref.shape[0]
    s = s_ref[...]
    ff, fft_ = f_ref[...], ft_ref[...]
    bf = lambda t: t.astype(jnp.bfloat16)
    for j in range(FFT_NB):
        x = _mm1(ff, bf(jnp.concatenate([re_ref[:, j, :], im_ref[:, j, :]], axis=0)))
        hr, hi = _filter_spectrum(p_ref[j], q_ref[j], s, n1)
        xr, xi = x[:n1], x[n1:]
        z = _mm1(fft_, bf(jnp.concatenate([xr * hr - xi * hi, xr * hi + xi * hr], axis=0)))
        ore_ref[:, j, :] = z[:n1]
        oim_ref[:, j, :] = z[n1:]


def fft_stage_b_conv(t1, spec, col_p, col_q, inv_norm, ff, fft_):
    n1, n2x2, C = t1.shape
    n2 = n2x2 // 2
    ncb = C // FFT_CB
    blk = lambda off: pl.BlockSpec((n1, FFT_NB, FFT_CB), lambda c, i: (0, i + off, c))
    mat = pl.BlockSpec((2 * n1, 2 * n1), lambda c, i: (0, 0))
    sp = lambda col: pl.BlockSpec((FFT_NB, 2 * n1, FFT_CB), lambda c, i: (i, 0, col * ncb + c))
    return pl.pallas_call(
        _stage_b_conv_kernel,
        grid=(ncb, n2 // FFT_NB),
        in_specs=[blk(0), blk(n2 // FFT_NB), sp(col_p), sp(col_q), pl.BlockSpec((1, FFT_CB), lambda c, i: (0, c)),
                  mat, mat],
        out_specs=[blk(0), blk(0)],
        out_shape=[jax.ShapeDtypeStruct((n1, n2, C), jnp.float32)] * 2,
        compiler_params=_params(2),
    )(t1, t1, spec, spec, inv_norm, ff, fft_)


def _stage_a_inv_kernel(re_ref, im_ref, g_ref, u_ref, gate_ref, bias_ref, o_ref):
    for j in range(FFT_NA):
        y = _mm1(g_ref[j], jnp.concatenate([re_ref[j], im_ref[j]], axis=0).astype(jnp.bfloat16))
        o_ref[:, j, :] = gate_ref[:, j, :] * (y + u_ref[:, j, :] * bias_ref[...])


def fft_stage_a_inv(t2re, t2im, ginv, nb, u3, u_col, gate3, gate_col, bias):
    n1, nk, C = t2re.shape
    tb = pl.BlockSpec((FFT_NA, nk, CB), lambda c, i: (i, 0, c))
    gb = pl.BlockSpec((FFT_NA, nb, 2 * nk), lambda c, i: (i, 0, 0))
    ub = lambda col: pl.BlockSpec((nb, FFT_NA, CB), lambda c, i: (0, i, col + c))
    return pl.pallas_call(
        _stage_a_inv_kernel,
        grid=(C // CB, n1 // FFT_NA),
        in_specs=[tb, tb, gb, ub(u_col), ub(gate_col), pl.BlockSpec((1, CB), lambda c, i: (0, c))],
        out_specs=ub(0),
        out_shape=jax.ShapeDtypeStruct((nb, n1, C), jnp.float32),
        compiler_params=_params(2),
    )(t2re, t2im, ginv, u3, gate3, bias)


def hyena_long_conv_chain(u, L, biases, filt, colsum, tabs):
    C = GROUP_W
    n1 = FFT_N1
    n2 = 2 * L // n1
    nb = n2 // 2
    spec = fft_stage_b_filter(fft_stage_a(filt.reshape(nb, n1, filt.shape[1]), nb, 0, filt.shape[1] // CB, tabs["g"]),
                              tabs["ff"])
    s4 = colsum.reshape(HY_ORDER, 2, C)
    inv_norm = 1.0 / (s4[:, 0] + s4[:, 1])
    u3 = u.reshape(u.shape[0] // n1, n1, u.shape[1])
    y3, y_col = u3, 0
    for n in range(HY_ORDER):
        t1 = fft_stage_a(y3, nb, y_col, 1, tabs["g"])
        t2re, t2im = fft_stage_b_conv(t1, spec, 2 * n, 2 * n + 1, inv_norm[n][None, :], tabs["ff"], tabs["fft"])
        y3 = fft_stage_a_inv(t2re, t2im, tabs["ginv"], nb, y3, y_col, u3, n + 1, biases[n][None, :])
        y_col = 0
    return y3.reshape(L, C)


def dense_dft_tables(n):
    N = 2 * n
    k = jnp.arange(N, dtype=jnp.float32)[:, None]
    t = jnp.arange(n, dtype=jnp.float32)[None, :]
    ph = jnp.mod(k * t, float(N)) * (2.0 * math.pi / N)
    fd = jnp.concatenate([jnp.cos(ph), -jnp.sin(ph)], axis=0)
    return fd.astype(jnp.bfloat16), (fd.T / N).astype(jnp.bfloat16)


def _hyena_ctx_kernel(u_ref, h_ref, s_ref, bias_ref, fd_ref, ft_ref, o_ref):
    C = GROUP_W
    fd, ft = fd_ref[...], ft_ref[...]
    K = fd.shape[0] // 2
    bf = lambda t: t.astype(jnp.bfloat16)
    hs = _mm1(fd, bf(h_ref[...]))
    s = s_ref[...]
    y = u_ref[:, 0:C]
    for n in range(HY_ORDER):
        cp, cq = 2 * n * C, (2 * n + 1) * C
        inv = 1.0 / (s[:, cp:cp + C] + s[:, cq:cq + C])
        hr, hi = _filter_spectrum(hs[:, cp:cp + C], hs[:, cq:cq + C], inv, K)
        x = _mm1(fd, bf(y))
        xr, xi = x[:K], x[K:]
        conv = _mm1(ft, bf(jnp.concatenate([xr * hr - xi * hi, xr * hi + xi * hr], axis=0)))
        y = u_ref[:, (n + 1) * C:(n + 2) * C] * (conv + y * bias_ref[n:n + 1, :])
    o_ref[...] = y


def hyena_ctx(u, filt, colsum, biases, tabs):
    args = (u, filt, colsum, biases, tabs[0], tabs[1])
    return pl.pallas_call(
        _hyena_ctx_kernel,
        in_specs=[_full(a) for a in args],
        out_specs=pl.BlockSpec((u.shape[0], GROUP_W), lambda: (0, 0)),
        out_shape=jax.ShapeDtypeStruct((u.shape[0], GROUP_W), jnp.float32),
        compiler_params=pltpu.CompilerParams(vmem_limit_bytes=VMEM_LIMIT),
    )(*args)


SCAN_T = 64
GROUP_LANES = 256
RW_SUB = 64
HG_SUB = 16
LOG2_E = 1.4426950408889634


def _chunk_index(d, i, n_ctx, n_all):
    fwd = jnp.where(i < n_ctx, n_all - n_ctx + i, i - n_ctx)
    return jnp.where(d == 0, fwd, n_all - 1 - i)


def _stacking(T, S, nh, head, sign):
    G = nh * head
    nb = T // S
    n = nh * T
    rr = lax.broadcasted_iota(jnp.int32, (n, G), 0)
    same = ((rr // S) % nh) == (lax.broadcasted_iota(jnp.int32, (n, G), 1) // head)

    def bd(x):
        pieces = []
        for i in range(nb):
            pieces += [x[i * S:(i + 1) * S]] * nh
        return jnp.where(same, jnp.concatenate(pieces, axis=0), 0.0)

    def collapse(o):
        outs = []
        for i in range(nb):
            acc = o[i * nh * S:i * nh * S + S]
            for h in range(1, nh):
                acc = acc + o[i * nh * S + h * S:i * nh * S + (h + 1) * S]
            outs.append(acc)
        return jnp.concatenate(outs, axis=0)

    rt = lax.broadcasted_iota(jnp.int32, (n, n), 0)
    cs = lax.broadcasted_iota(jnp.int32, (n, n), 1)
    t_r = (rt // (nh * S)) * S + rt % S
    t_c = (cs // (nh * S)) * S + cs % S
    same_h = ((rt // S) % nh) == ((cs // S) % nh)
    before = same_h & ((t_r - t_c) * sign > 0)
    return bd, collapse, before, rt, cs, (t_r, t_c, same_h)


def _softplus(x):
    return jnp.maximum(x, 0.0) + jnp.log(1.0 + jnp.exp(-jnp.abs(x)))


def _rwkv_prep_kernel(*refs, n_lat, n_rows):
    zrefs, rest = refs[:15], refs[15:]
    (mu_ref, kk_w_ref, ka_ref, w0_ref, a0_ref, w2h_ref, w2l_ref, a2h_ref, a2l_ref, ones_ref,
     r_ref, k_ref, v_ref, g_ref, kk_ref, lw_ref, kd_ref, a_ref) = rest
    slabs = []
    for c in range(5):
        x = zrefs[3 * c][...]
        xp, xn = _neighbours(x, zrefs[3 * c + 1][...], zrefs[3 * c + 2][...], n_lat, n_rows)
        slabs.append(x + (0.5 * (xp + xn) - x) * mu_ref[:, c * CB:(c + 1) * CB])
    r, k, v, g, lora = slabs
    r_ref[...], k_ref[...], v_ref[...], g_ref[...] = r, k, v, g
    kk = k * kk_w_ref[...]
    ss = _mm2(kk * kk, ones_ref[...])
    kk_ref[...] = kk * lax.rsqrt(jnp.maximum(ss, 1e-24))
    lora_t = _split(jnp.tanh(lora))
    lora_s = _split(lora)
    for d in range(2):
        w_log = -_softplus(-(w0_ref[d:d + 1, :] + _mm3(lora_t, (w2h_ref[d], w2l_ref[d])))) - 0.5
        lw_ref[d] = -jnp.exp(w_log)
        a = _sigmoid(a0_ref[d:d + 1, :] + _mm3(lora_s, (a2h_ref[d], a2l_ref[d])))
        a_ref[d] = a
        kd_ref[d] = k * (1.0 + (a - 1.0) * ka_ref[...])


def rwkv_prep(z, mu, k_k, k_a, w0, a0, w2, a2, n_lat):
    n_rows = z.shape[0]
    tp = PREP_ROWS
    mu_p = jnp.pad(mu, (0, 5 * CB - RW_REAL))[None, :]
    w2p = jnp.zeros((2, CB, GROUP_W), jnp.float32)
    a2p = jnp.zeros((2, CB, GROUP_W), jnp.float32)
    for d in range(2):
        w2p = w2p.at[d, d * RW_LORA:(d + 1) * RW_LORA].set(w2[d])
        a2p = a2p.at[d, (2 + d) * RW_LORA:(3 + d) * RW_LORA].set(a2[d])
    w2s, a2s = _split(w2p), _split(a2p)
    small = (mu_p, k_k[None, :], k_a[None, :], w0, a0, w2s[0], w2s[1], a2s[0], a2s[1],
             _block_indicator(GROUP_W, RW_HEAD, 1.0))
    zspecs = []
    for c in range(5):
        zspecs += _halo_specs(tp, n_rows, COL_RW + c)
    one = pl.BlockSpec((tp, CB), lambda i: (i, 0))
    two = pl.BlockSpec((2, tp, CB), lambda i: (0, i, 0))
    s1 = jax.ShapeDtypeStruct((n_rows, GROUP_W), jnp.float32)
    s2 = jax.ShapeDtypeStruct((2, n_rows, GROUP_W), jnp.float32)
    return pl.pallas_call(
        functools.partial(_rwkv_prep_kernel, n_lat=n_lat, n_rows=n_rows),
        grid=(n_rows // tp,),
        in_specs=zspecs + [_full(a) for a in small],
        out_specs=[one] * 5 + [two] * 3,
        out_shape=[s1] * 5 + [s2] * 3,
        compiler_params=_params(1),
    )(*([z] * 15), *small)


def _rwkv_scan_body(refs, head):
    in_f, in_b, (of_ref, ob_ref, ht_ref) = refs[:6], refs[6:12], refs[12:]
    T, G, S = SCAN_T, GROUP_LANES, RW_SUB
    nh = G // head
    n = nh * T

    @pl.when(pl.program_id(0) == 0)
    def _():
        ht_ref[...] = jnp.zeros_like(ht_ref)

    ti = lax.broadcasted_iota(jnp.int32, (T, T), 0)
    si = lax.broadcasted_iota(jnp.int32, (T, T), 1)
    bf = lambda t: t.astype(jnp.bfloat16)
    each = lambda f, *seqs: [f(*xs) for xs in zip(*seqs)]
    n_grp = in_f[0].shape[1] // G
    lanes = [slice(grp * G, (grp + 1) * G) for grp in range(n_grp)]

    r, v, kk, lw, k, a, tri, before, incl, dest = ([] for _ in range(10))
    for (r_ref, v_ref, kk_ref, lw_ref, k_ref, a_ref), o_ref, sign in ((in_f, of_ref, 1), (in_b, ob_ref, -1)):
        bd, collapse, before_d, rt, cs, (t_r, t_c, same_h) = _stacking(T, S, nh, head, sign)
        for ls in lanes:
            r.append(r_ref[:, ls]), v.append(v_ref[:, ls]), kk.append(kk_ref[:, ls])
            lw.append(lw_ref[0, :, ls]), k.append(k_ref[0, :, ls]), a.append(a_ref[0, :, ls])
            tri.append(jnp.where((ti - si) * sign >= 0, 1.0, 0.0))
            before.append(before_d), incl.append(before_d | (rt == cs)), dest.append((o_ref, ls))
    eye = jnp.where(rt == cs, 1.0, 0.0)
    ht = [ht_ref[ch] for ch in range(len(r))]

    c = each(lambda tr, t: _dot(tr, t), tri, lw)
    ctot = each(lambda t: jnp.sum(t, axis=0, keepdims=True), lw)
    beta = each(lambda p, q: p * q, kk, a)
    einv = each(lambda t: jnp.exp(-t), c)
    lhs = each(lambda kk_, r_, c_, lw_: bf(jnp.concatenate([bd(-kk_ * jnp.exp(c_ - lw_)), bd(r_ * jnp.exp(c_))], axis=0)),
               kk, r, c, lw)
    rhs = each(lambda k_, b_, e_: bf(jnp.concatenate([bd(k_ * e_), bd(b_ * e_)], axis=0)), k, beta, einv)
    m = each(lambda p, q: _mm1(p, q, NT), lhs, rhs)
    g = each(lambda p, h: _mm1(p, bf(h), NT), lhs, ht)
    v_b = each(lambda t: bf(bd(t)), v)
    x = each(lambda g_, m_, v_, bm: g_[:n] + _mm1(bf(jnp.where(bm, m_[:n, :n], 0.0)), v_), g, m, v_b, before)
    lb = each(lambda m_, bm: jnp.where(bm, m_[:n, n:], 0.0), m, before)
    differ = jnp.where(same_h, t_r ^ t_c, 0)
    half = lambda s: (differ >= s) & (differ < 2 * s)
    dmat = each(lambda t: eye + jnp.where(half(1), t, 0.0), lb)
    s = 2
    while s < T:
        coupling = half(s)
        dmat = each(lambda d_, t: d_ + _mm1(bf(_mm1(bf(d_), bf(jnp.where(coupling, t, 0.0)))), bf(d_)), dmat, lb)
        s *= 2
    u_b = each(lambda d_, t: bf(_mm1(bf(d_), bf(t))), dmat, x)
    o = each(lambda g_, m_, v_, u_, im: (g_[n:] + _mm1(bf(jnp.where(im, m_[n:, :n], 0.0)), v_))
             + _mm1(bf(jnp.where(im, m_[n:, n:], 0.0)), u_), g, m, v_b, u_b, incl)
    efin = each(lambda ct, c_: jnp.exp(ct - c_), ctot, c)
    ht_new = each(lambda h, ct, v_, u_, k_, b_, e_: (h * jnp.exp(ct) + _mm1(v_, bf(bd(k_ * e_)), TN))
                  + _mm1(u_, bf(bd(b_ * e_)), TN), ht, ctot, v_b, u_b, k, beta, efin)
    for ch, (o_ref, ls) in enumerate(dest):
        o_ref[:, ls] = collapse(o[ch])
        ht_ref[ch] = ht_new[ch]


def rope_tables(n_ctx, L):
    d_axis = RT_HEAD // 2
    half = d_axis // 2
    inv = ROPE_BASE ** (-jnp.arange(0, d_axis, 2, dtype=jnp.float32) / d_axis)
    t = jnp.arange(L)
    pos = jnp.stack([(t // GRID_W).astype(jnp.float32), (t % GRID_W).astype(jnp.float32)], axis=1)
    j = jnp.arange(RT_HEAD)
    ang = pos[:, j // d_axis] * inv[j % half][None, :]
    sgn = jnp.where((j % d_axis) < half, -1.0, 1.0)[None, :]
    cos = jnp.concatenate([jnp.cos(ang), jnp.ones((n_ctx, RT_HEAD), jnp.float32)], axis=0)
    sin = jnp.concatenate([jnp.sin(ang) * sgn, jnp.zeros((n_ctx, RT_HEAD), jnp.float32)], axis=0)
    return jnp.tile(cos, (1, RT_HEADS)), jnp.tile(sin, (1, RT_HEADS))


def _rotate(x, cos, sin):
    G = x.shape[1]
    half = RT_HEAD // 4
    lane = lax.broadcasted_iota(jnp.int32, x.shape, 1)
    partner = jnp.where((lane % (2 * half)) < half, pltpu.roll(x, G - half, axis=1), pltpu.roll(x, half, axis=1))
    return x * cos + partner * sin


def _retention_scan_body(q_ref, k_ref, v_ref, cos_ref, sin_ref, lg_ref, o_ref, st_ref, *, head, d):
    T, G = SCAN_T, GROUP_LANES
    nh = G // head
    sign = 1 - 2 * d

    t = lax.broadcasted_iota(jnp.int32, (T, 1), 0)
    pos = (t + d * (T - 1 - 2 * t) + 1).astype(jnp.float32)
    bd, collapse, before, rt, cs, _ = _stacking(T, T, nh, head, sign)
    incl = before | (rt == cs)
    bf = lambda x: x.astype(jnp.bfloat16)

    groups = range(q_ref.shape[1] // G)
    lanes = [slice(grp * G, (grp + 1) * G) for grp in groups]
    each = lambda f, *seqs: [f(*xs) for xs in zip(*seqs)]
    st = [st_ref[grp] for grp in groups]
    cos, sin, lg = ([ref[:, ls] for ls in lanes] for ref in (cos_ref, sin_ref, lg_ref))
    q = each(_rotate, [q_ref[:, ls] for ls in lanes], cos, sin)
    k = each(lambda x, c_, s_: _rotate(x, c_, s_) * (head ** -0.5), [k_ref[:, ls] for ls in lanes], cos, sin)
    c = each(lambda lg_: pos * lg_, lg)
    q_b = each(lambda q_, c_: bf(bd(q_ * jnp.exp(c_))), q, c)
    kt_b = each(lambda k_, c_: bf(bd(k_ * jnp.exp(-c_))), k, c)
    v_b = [bf(bd(v_ref[:, ls])) for ls in lanes]
    scores = each(lambda q_, k_: bf(jnp.where(incl, _mm1(q_, k_, NT), 0.0)), q_b, kt_b)
    inter = each(lambda q_, s_: _mm1(q_, bf(s_), NT), q_b, st)
    o = each(lambda g_, s_, v_: g_ + _mm1(s_, v_), inter, scores, v_b)
    st_new = each(lambda s_, lg_, v_, k_, c_: s_ * jnp.exp(float(T) * lg_)
                  + _mm1(v_, bf(bd(k_ * jnp.exp(float(T) * lg_ - c_))), TN), st, lg, v_b, k, c)
    for grp in groups:
        o_ref[:, lanes[grp]] = collapse(o[grp])
        st_ref[grp] = st_new[grp]


def _gla_scan_body(q_ref, f_ref, i_ref, lb_ref, o_ref, st_ref, *, head, d):
    T, S = SCAN_T, HG_SUB
    W = q_ref.shape[-1]
    nh = W // head
    sign = 1 - 2 * d

    ti = lax.broadcasted_iota(jnp.int32, (S, S), 0)
    si = lax.broadcasted_iota(jnp.int32, (S, S), 1)
    tri_incl = jnp.where((ti - si) * sign >= 0, 1.0, 0.0)
    row = lax.broadcasted_iota(jnp.int32, (S, 1), 0)
    lb = lb_ref[0]

    states = [st_ref[h] for h in range(nh)]
    for j in range(T // S):
        jb = j + d * (T // S - 1 - 2 * j)
        rows = pl.ds(jb * S, S)
        q = q_ref[rows, :]
        q = q * _sigmoid(q)
        v = i_ref[rows, :]
        gate = lb + (1.0 - lb) * _sigmoid(f_ref[rows, :])
        lf = jnp.log(jnp.maximum(gate, HG_MIN_GATE))
        k = 1.0 - gate
        b = _dot(tri_incl, lf)
        btot = jnp.sum(lf, axis=0, keepdims=True)
        qe = q * jnp.exp(b)
        ke = k * jnp.exp(btot - b)
        outs = []
        for h in range(nh):
            ls = slice(h * head, (h + 1) * head)
            qh, kh, vh = q[:, ls], k[:, ls], v[:, ls]
            bh = b[:, ls] * LOG2_E
            st = states[h]
            o = _mm1(qe[:, ls].astype(jnp.bfloat16), st.astype(jnp.bfloat16), NT)
            for s in range(S):
                e = jnp.exp2(bh - bh[s:s + 1, :])
                a_s = jnp.sum(qh * kh[s:s + 1, :] * e, axis=-1, keepdims=True)
                a_s = jnp.where((row - s) * sign >= 0, a_s, 0.0)
                o = o + a_s * vh[s:s + 1, :]
            outs.append(o)
            states[h] = st * jnp.exp(btot[:, ls]) + _mm1(vh.astype(jnp.bfloat16), ke[:, ls].astype(jnp.bfloat16), TN)
        o_ref[rows, :] = jnp.concatenate(outs, axis=1)
    for h in range(nh):
        st_ref[h] = states[h]


N_RW, N_RT, N_HG = 6, 6, 4


def _scans_kernel(*refs):
    n_in = 2 * (N_RW + N_RT + N_HG)
    ins, (rw_of, rw_ob, rt_of, rt_ob, hg_of, hg_ob, rw_st, rt_st, hg_st) = refs[:n_in], refs[n_in:]
    rw_in, rt_in, hg_in = ins[:2 * N_RW], ins[2 * N_RW:2 * (N_RW + N_RT)], ins[2 * (N_RW + N_RT):]

    @pl.when(pl.program_id(0) == 0)
    def _():
        rt_st[...] = jnp.zeros_like(rt_st)
        hg_st[...] = jnp.zeros_like(hg_st)

    n_rt, n_hg = rt_st.shape[0] // 2, hg_st.shape[0] // 2
    _rwkv_scan_body(list(rw_in) + [rw_of, rw_ob, rw_st], RW_HEAD)
    for d, (rt_o, hg_o) in enumerate(((rt_of, hg_of), (rt_ob, hg_ob))):
        _gla_scan_body(*hg_in[d * N_HG:(d + 1) * N_HG], hg_o, hg_st.at[d * n_hg:(d + 1) * n_hg], head=HG_HEAD, d=d)
        _retention_scan_body(*rt_in[d * N_RT:(d + 1) * N_RT], rt_o, rt_st.at[d * n_rt:(d + 1) * n_rt],
                             head=RT_HEAD, d=d)


def recurrent_scans(z, rw, cos, sin, lg, lb2, n_ctx_rows):
    N = z.shape[0]
    W = GROUP_W
    T = SCAN_T
    n_all, n_ctx = N // T, n_ctx_rows // T
    chunk = lambda d: (lambda i: (_chunk_index(d, i, n_ctx, n_all), 0))
    row = lambda d: pl.BlockSpec((T, W), chunk(d))
    per_dir = lambda d: pl.BlockSpec((1, T, W), lambda i: (d, _chunk_index(d, i, n_ctx, n_all), 0))
    zc = lambda d, col: pl.BlockSpec((T, CB), lambda i: (_chunk_index(d, i, n_ctx, n_all), col))
    rw_specs = lambda d: [row(d)] * 3 + [per_dir(d)] * 3
    rt_specs = lambda d: [zc(d, COL_RT), zc(d, COL_RT + 1), zc(d, COL_RT + 2), row(d), row(d),
                          pl.BlockSpec((1, W), lambda i: (0, 0))]
    hg_specs = lambda d: [zc(d, COL_HG), zc(d, COL_HG + 1 + d), zc(d, COL_HG + 3),
                          pl.BlockSpec((1, 1, W), lambda i: (d, 0, 0))]
    shape = jax.ShapeDtypeStruct((N, W), jnp.float32)
    return pl.pallas_call(
        _scans_kernel,
        grid=(n_all,),
        in_specs=rw_specs(0) + rw_specs(1) + rt_specs(0) + rt_specs(1) + hg_specs(0) + hg_specs(1),
        out_specs=[row(0), row(1)] * 3,
        out_shape=[shape] * 6,
        scratch_shapes=[pltpu.VMEM((2 * (W // GROUP_LANES), GROUP_LANES, GROUP_LANES), jnp.float32),
                        pltpu.VMEM((2 * (W // GROUP_LANES), GROUP_LANES, GROUP_LANES), jnp.float32),
                        pltpu.VMEM((2 * (W // HG_HEAD), HG_HEAD, HG_HEAD), jnp.float32)],
        compiler_params=_params(1),
    )(*(list(rw) * 2), *([z, z, z, cos, sin, lg] * 2), *([z, z, z, lb2] * 2))


def _outproj_kernel(hy_ref, rwof_ref, rwob_ref, r_ref, k_ref, v_ref, rwg_ref, rtof_ref, rtob_ref, rtg_ref,
                    hgof_ref, hgob_ref, hgg_ref,
                    rk_ref, gng_ref, gnb_ref, hgn_ref, avg64_ref, avg128_ref,
                    w_ref, x_ref, gate_ref, g_ref, b_ref, o_ref, *, n_lat):
    avg64, avg128 = avg64_ref[...], avg128_ref[...]

    def head_norm(o, avg, eps, centre):
        if centre:
            o = o - _mm2(o, avg)
        return o * lax.rsqrt(_mm2(o * o, avg) + eps)

    silu = lambda t: t * _sigmoid(t)
    y_rw = head_norm(rwof_ref[...] + rwob_ref[...], avg64, RW_GN_EPS, True) * gng_ref[...] + gnb_ref[...]
    bonus = (float(RW_HEAD) * _mm2(r_ref[...] * k_ref[...] * rk_ref[...], avg64)) * v_ref[...]
    y_rw = (y_rw + bonus) * _sigmoid(rwg_ref[...])
    y_rt = head_norm(rtof_ref[...] + rtob_ref[...], avg64, HEAD_NORM_EPS, True) * silu(rtg_ref[...])
    y_hg = head_norm(hgof_ref[...] + hgob_ref[...], avg128, HEAD_NORM_EPS, False) * hgn_ref[...] * silu(hgg_ref[...])
    y = None
    for m, ym in enumerate((hy_ref[...], y_rw, y_rt, y_hg)):
        part = jnp.dot(ym.astype(jnp.bfloat16), w_ref[m * GROUP_W:(m + 1) * GROUP_W, :],
                       preferred_element_type=jnp.float32)
        y = part if y is None else y + part
    is_ctx = _ctx_rows(x_ref.shape[0], n_lat)
    r = ALPHA * x_ref[...] + _mod_row(gate_ref, is_ctx) * y
    o_ref[...] = _ln_rows(r) * g_ref[...] + b_ref[...]


def outproj_deepnorm(y_hy, scans, rw_r, rw_k, rw_v, rw_g, z, r_k, gn_g, gn_b, hg_norm_g,
                     w_bf16, x, mod, g, b, n_lat, m):
    tm = OUT_TM
    one = pl.BlockSpec((tm, GROUP_W), lambda i: (i, 0))
    zc = lambda col: pl.BlockSpec((tm, CB), lambda i: (i, col))
    row = pl.BlockSpec((tm, D_MODEL), lambda i: (i, 0))
    small = (r_k, gn_g, gn_b, hg_norm_g, _block_indicator(GROUP_W, RW_HEAD, 1.0 / RW_HEAD),
             _block_indicator(GROUP_W, HG_HEAD, 1.0 / HG_HEAD))
    return pl.pallas_call(
        functools.partial(_outproj_kernel, n_lat=n_lat),
        grid=(m // tm,),
        in_specs=[one] * 9 + [zc(COL_RT + 3), one, one, zc(COL_HG + 4)]
        + [_full(a) for a in small]
        + [_full(w_bf16), row, pl.BlockSpec((8, D_MODEL), lambda i: (0, 2)), _full(g), _full(b)],
        out_specs=row,
        out_shape=jax.ShapeDtypeStruct((m, D_MODEL), jnp.float32),
        compiler_params=_params(1),
    )(y_hy, scans[0], scans[1], rw_r, rw_k, rw_v, rw_g, scans[2], scans[3], z, scans[4], scans[5], z,
      *small, w_bf16, x, mod, g, b)


def _ffn_kernel(x_ref, sh_ref, sc_ref, w1_ref, w3_ref, w2_ref, gate_ref, g_ref, b_ref, o_ref, h_ref, acc_ref, *, n_lat):
    j = pl.program_id(1)

    @pl.when(j == 0)
    def _():
        is_ctx = _ctx_rows(x_ref.shape[0], n_lat)
        h = _ln_rows(x_ref[...]) * (1.0 + _mod_row(sc_ref, is_ctx)) + _mod_row(sh_ref, is_ctx)
        h_ref[...] = h.astype(jnp.bfloat16)
        acc_ref[...] = jnp.zeros_like(acc_ref)

    h = h_ref[...]
    a = jnp.dot(h, w1_ref[...], preferred_element_type=jnp.float32)
    u = jnp.dot(h, w3_ref[...], preferred_element_type=jnp.float32)
    s = (a * _sigmoid(a) * u).astype(jnp.bfloat16)
    acc_ref[...] += jnp.dot(s, w2_ref[...], preferred_element_type=jnp.float32)

    @pl.when(j == pl.num_programs(1) - 1)
    def _():
        is_ctx = _ctx_rows(x_ref.shape[0], n_lat)
        r = ALPHA * x_ref[...] + _mod_row(gate_ref, is_ctx) * acc_ref[...]
        o_ref[...] = _ln_rows(r) * g_ref[...] + b_ref[...]


def ffn_deepnorm(x, mod, w1, w3, w2, g, b, n_lat, tm):
    m = x.shape[0]
    row = pl.BlockSpec((tm, D_MODEL), lambda i, j: (i, 0))
    modc = lambda c: pl.BlockSpec((8, D_MODEL), lambda i, j: (0, c))
    vec = pl.BlockSpec((1, D_MODEL), lambda i, j: (0, 0))
    return pl.pallas_call(
        functools.partial(_ffn_kernel, n_lat=n_lat),
        grid=(m // tm, FFN_HIDDEN // FFN_TF),
        in_specs=[row, modc(3), modc(4),
                  pl.BlockSpec((D_MODEL, FFN_TF), lambda i, j: (0, j)),
                  pl.BlockSpec((D_MODEL, FFN_TF), lambda i, j: (0, j)),
                  pl.BlockSpec((FFN_TF, D_MODEL), lambda i, j: (j, 0)),
                  modc(5), vec, vec],
        out_specs=row,
        out_shape=jax.ShapeDtypeStruct((m, D_MODEL), jnp.float32),
        scratch_shapes=[pltpu.VMEM((tm, D_MODEL), jnp.bfloat16),
                        pltpu.VMEM((tm, D_MODEL), jnp.float32)],
        compiler_params=_params(2),
    )(x, mod, mod, w1, w3, w2, mod, g, b)


def kernel(x, c, ctx, c_ctx, ada_w, ada_b, w_in, w_out, ln_g, ln_b, hy_conv_w, hy_conv_b, hy_w1, hy_b1, hy_w2, hy_b2, hy_w3, hy_b3, hy_freq, hy_bias, rw_mu, rw_w0, rw_w2, rw_a0, rw_a2, rw_k_k, rw_k_a, rw_r_k, rw_gn_g, rw_gn_b, hg_lb_raw, hg_norm_g, ffn_w1, ffn_w3, ffn_w2):
    L, n_ctx = x.shape[1], ctx.shape[1]
    sm = jax.nn.softmax(hg_lb_raw.astype(jnp.float32), axis=1)
    lower_bounds = jnp.cumsum(sm, axis=1) - sm[:, :1]
    log_gamma = jnp.log1p(-jnp.exp2(-5.0 - jnp.arange(RT_HEADS, dtype=jnp.float32)))
    lg = jnp.repeat(log_gamma, RT_HEAD)[None, :]
    cos, sin = rope_tables(n_ctx, L)
    fft_tabs = fft_tables(FFT_N1, 2 * L // FFT_N1)
    ctx_tabs = dense_dft_tables(n_ctx)

    c8 = jnp.zeros((8, D_MODEL), jnp.float32).at[0].set(c[0]).at[1].set(c_ctx)
    xs = jnp.concatenate([x[0], ctx[0]], axis=0)
    n_rows = L + n_ctx
    for l in range(DEPTH):
        with_ctx = l < DEPTH - 1
        mod = ada_modulation(c8, ada_w, l, ada_b[l][None, :])
        w_out_b = w_out[l].astype(jnp.bfloat16)
        w1_b, w3_b, w2_b = (w[l].astype(jnp.bfloat16) for w in (ffn_w1, ffn_w3, ffn_w2))

        z = modulated_projection(xs, mod, w_in, l, L)

        u = hyena_conv3(z, hy_conv_w[l], hy_conv_b[l][None, :], L)
        hy_w = (hy_w1[l], hy_b1[l], hy_w2[l], hy_b2[l], hy_w3[l], hy_b3[l], hy_freq[l])
        filt, colsum = hyena_filter_bank(L, *hy_w)
        y_lat = hyena_long_conv_chain(u, L, hy_bias[l], filt, colsum, fft_tabs)
        if with_ctx:
            filt_c, colsum_c = hyena_filter_bank(n_ctx, *hy_w)
            y_ctx = hyena_ctx(u[L:], filt_c, colsum_c, hy_bias[l], ctx_tabs)
        else:
            y_ctx = jnp.zeros((n_ctx, GROUP_W), jnp.float32)
        y_hy = jnp.concatenate([y_lat, y_ctx], axis=0)

        rw_r, rw_k, rw_v, rw_g, rw_kk, rw_lw, rw_kd, rw_a = rwkv_prep(
            z, rw_mu[l], rw_k_k[l], rw_k_a[l], rw_w0[l], rw_a0[l], rw_w2[l], rw_a2[l], L)
        scans = recurrent_scans(z, (rw_r, rw_v, rw_kk, rw_lw, rw_kd, rw_a), cos, sin, lg,
                                lower_bounds[:, l][:, None, :], n_ctx)

        xs = outproj_deepnorm(y_hy, scans, rw_r, rw_k, rw_v, rw_g, z,
                              rw_r_k[l].reshape(1, GROUP_W), rw_gn_g[l][None, :], rw_gn_b[l][None, :],
                              hg_norm_g[l][None, :], w_out_b, xs, mod, ln_g[l, 0][None, :], ln_b[l, 0][None, :],
                              L, n_rows if with_ctx else L)
        xs = ffn_deepnorm(xs, mod, w1_b, w3_b, w2_b, ln_g[l, 1][None, :], ln_b[l, 1][None, :], L,
                          FFN_TM if with_ctx else FFN_LAST_TM)
    return xs[None]
```

```python
import functools
import math

import jax
import jax.numpy as jnp
from jax import lax
from jax.experimental import pallas as pl
from jax.experimental.pallas import tpu as pltpu

D_MODEL = 2048
DEPTH = 2
GRID_W = 64
N_MIXERS = 4
GROUP_W = D_MODEL // N_MIXERS
HY_ORDER = 2
HY_EMB = 33
HY_FAST_DECAY = 0.3
HY_SLOW_DECAY = 1.5
HY_TARGET = 1e-2
RW_HEAD = 64
RW_LORA = 96
RW_GN_EPS = 64e-5
RT_HEAD = 64
RT_HEADS = GROUP_W // RT_HEAD
ROPE_BASE = 10000.0
HG_HEAD = 128
HG_MIN_GATE = 1e-30
FFN_HIDDEN = 5632
ALPHA = (2 * DEPTH) ** 0.25
LN_EPS = 1e-6
HEAD_NORM_EPS = 1e-6

LANES = 128
CB = 512
COL_HY = 0
COL_RW = 3
COL_RT = 8
COL_HG = 12
P_IN_PAD = 17 * CB
RW_REAL = 4 * GROUP_W + 4 * RW_LORA
PROJ_TM = 1408
FFN_TM = 768
FFN_LAST_TM = 512
OUT_TM = 256
FFN_TF = 512
PREP_ROWS = 384
HALO = 8
VMEM_LIMIT = 56 * 1024 * 1024

HI = lax.Precision.HIGHEST
NN = ((1,), (0,))
NT = ((1,), (1,))
TN = ((0,), (0,))


def _params(n_axes):
    return pltpu.CompilerParams(dimension_semantics=("arbitrary",) * n_axes, vmem_limit_bytes=VMEM_LIMIT)


def _full(a):
    return pl.BlockSpec(a.shape, lambda *_: (0,) * a.ndim)


def _dot(a, b):
    return jnp.dot(a, b, precision=HI, preferred_element_type=jnp.float32)


def _split(x):
    hi = x.astype(jnp.bfloat16)
    lo = (x - hi.astype(jnp.float32)).astype(jnp.bfloat16)
    return hi, lo


def _mm3(a, b, dims=NN):
    d = lambda p, q: lax.dot_general(p, q, (dims, ((), ())), preferred_element_type=jnp.float32)
    return d(a[0], b[0]) + (d(a[0], b[1]) + d(a[1], b[0]))


def _mm1(a, b, dims=NN):
    return lax.dot_general(a, b, (dims, ((), ())), preferred_element_type=jnp.float32)


def _mm2(x, b):
    hi, lo = _split(x)
    d = functools.partial(jnp.dot, preferred_element_type=jnp.float32)
    return d(hi, b) + d(lo, b)


def _ln_rows(x):
    mu = jnp.mean(x, axis=-1, keepdims=True)
    xc = x - mu
    var = jnp.mean(xc * xc, axis=-1, keepdims=True)
    return xc * lax.rsqrt(var + LN_EPS)


def _sigmoid(x):
    return 1.0 / (1.0 + jnp.exp(-x))


def _ctx_rows(tm, n_lat):
    return (pl.program_id(0) * tm + lax.broadcasted_iota(jnp.int32, (tm, 1), 0)) >= n_lat


def _mod_row(ref, is_ctx):
    return jnp.where(is_ctx, ref[1:2, :], ref[0:1, :])


def _block_indicator(width, head, value):
    i = jnp.arange(width) // head
    return jnp.where(i[:, None] == i[None, :], value, 0.0).astype(jnp.bfloat16)


def _ada_kernel(c_ref, w_ref, b_ref, o_ref):
    c = c_ref[...]
    h = c * _sigmoid(c)
    o_ref[...] = jnp.dot(h.astype(jnp.bfloat16), w_ref[...].astype(jnp.bfloat16),
                         preferred_element_type=jnp.float32) + b_ref[...]


def ada_modulation(c8, w, layer, b):
    n = w.shape[2]
    tn = 1024
    return pl.pallas_call(
        _ada_kernel,
        grid=(n // tn,),
        in_specs=[pl.BlockSpec((8, D_MODEL), lambda j: (0, 0)),
                  pl.BlockSpec((None, D_MODEL, tn), lambda j: (layer, 0, j)),
                  pl.BlockSpec((1, tn), lambda j: (0, j))],
        out_specs=pl.BlockSpec((8, tn), lambda j: (0, j)),
        out_shape=jax.ShapeDtypeStruct((8, n), jnp.float32),
        compiler_params=_params(1),
    )(c8, w, b)


W_SUB = CB // LANES
W_PAD_PIECE = (COL_RW * CB + RW_REAL) // LANES


def _proj_kernel(x_ref, sh_ref, sc_ref, *refs, n_lat):
    w_refs, (o_ref, h_ref) = refs[:W_SUB], refs[W_SUB:]
    j = pl.program_id(1)

    @pl.when(j == 0)
    def _():
        is_ctx = _ctx_rows(x_ref.shape[0], n_lat)
        h = _ln_rows(x_ref[...]) * (1.0 + _mod_row(sc_ref, is_ctx)) + _mod_row(sh_ref, is_ctx)
        h_ref[...] = h.astype(jnp.bfloat16)

    pieces = [w_refs[k][...] for k in range(W_SUB)]
    for k in range(W_SUB):
        if (W_PAD_PIECE - k) % W_SUB == 0:
            pieces[k] = jnp.where(j * W_SUB + k == W_PAD_PIECE, 0.0, pieces[k])
    w = jnp.concatenate(pieces, axis=1).astype(jnp.bfloat16)
    o_ref[...] = jnp.dot(h_ref[...], w, preferred_element_type=jnp.float32)


def modulated_projection(x, mod, w, layer, n_lat):
    m = x.shape[0]
    tm = PROJ_TM

    def piece(k):
        def index(i, j):
            p = j * W_SUB + k
            return layer, 0, jnp.where(p < W_PAD_PIECE, p, p - 1)
        return pl.BlockSpec((None, D_MODEL, LANES), index)

    return pl.pallas_call(
        functools.partial(_proj_kernel, n_lat=n_lat),
        grid=(m // tm, P_IN_PAD // CB),
        in_specs=[pl.BlockSpec((tm, D_MODEL), lambda i, j: (i, 0)),
                  pl.BlockSpec((8, D_MODEL), lambda i, j: (0, 0)),
                  pl.BlockSpec((8, D_MODEL), lambda i, j: (0, 1))] + [piece(k) for k in range(W_SUB)],
        out_specs=pl.BlockSpec((tm, CB), lambda i, j: (i, j)),
        out_shape=jax.ShapeDtypeStruct((m, P_IN_PAD), jnp.float32),
        scratch_shapes=[pltpu.VMEM((tm, D_MODEL), jnp.bfloat16)],
        compiler_params=_params(2),
    )(x, mod, mod, *([w] * W_SUB))


def _halo_specs(tp, n_rows, col, width=CB):
    per = tp // HALO
    last = n_rows // HALO - 1
    main = pl.BlockSpec((tp, width), lambda i: (i, col))
    prev = pl.BlockSpec((HALO, width), lambda i: (jnp.maximum(i * per - 1, 0), col))
    nxt = pl.BlockSpec((HALO, width), lambda i: (jnp.minimum((i + 1) * per, last), col))
    return [main, prev, nxt]


def _neighbours(x, prev, nxt, n_lat, n_rows):
    tp = x.shape[0]
    loc = lax.broadcasted_iota(jnp.int32, (tp, 1), 0)
    row = pl.program_id(0) * tp + loc
    xp = jnp.where(loc == 0, prev[HALO - 1:HALO, :], pltpu.roll(x, 1, axis=0))
    xp = jnp.where((row == 0) | (row == n_lat), 0.0, xp)
    xn = jnp.where(loc == tp - 1, nxt[0:1, :], pltpu.roll(x, tp - 1, axis=0))
    xn = jnp.where((row == n_lat - 1) | (row == n_rows - 1), 0.0, xn)
    return xp, xn


def _conv3_kernel(z_ref, zp_ref, zn_ref, w_ref, b_ref, o_ref, *, n_lat, n_rows):
    x = z_ref[...]
    xp, xn = _neighbours(x, zp_ref[...], zn_ref[...], n_lat, n_rows)
    w = w_ref[...]
    o_ref[...] = xp * w[0:1, :] + x * w[1:2, :] + xn * w[2:3, :] + b_ref[...]


def hyena_conv3(z, w, b, n_lat):
    n_rows = z.shape[0]
    tp = PREP_ROWS
    width = w.shape[1]
    return pl.pallas_call(
        functools.partial(_conv3_kernel, n_lat=n_lat, n_rows=n_rows),
        grid=(n_rows // tp,),
        in_specs=_halo_specs(tp, n_rows, COL_HY, width) + [_full(w), _full(b)],
        out_specs=pl.BlockSpec((tp, width), lambda i: (i, 0)),
        out_shape=jax.ShapeDtypeStruct((n_rows, width), jnp.float32),
        compiler_params=_params(1),
    )(z, z, z, w, b)


HY_EMB_PAD = 40
HY_FILTER_ROWS = 512


def hyena_features(L):
    t = jnp.linspace(0.0, 1.0, L, dtype=jnp.float32)[:, None]
    n_bands = (HY_EMB - 1) // 2
    f = jnp.linspace(1e-4, n_bands - 1, n_bands, dtype=jnp.float32)[None, :]
    ang = (2.0 * math.pi / L) * jnp.arange(L, dtype=jnp.float32)[:, None] * f
    z = jnp.concatenate([t, jnp.cos(ang), -jnp.sin(ang)], -1)
    return jnp.pad(z, ((0, 0), (0, HY_EMB_PAD - HY_EMB)))


def _filter_kernel(z_ref, w1_ref, b1_ref, w2_ref, b2_ref, w3_ref, b3_ref, fr_ref, dl_ref, h_ref, s_ref):
    i = pl.program_id(0)
    z = z_ref[...]
    fr = fr_ref[...]
    h = jnp.sin(fr * (_dot(z, w1_ref[...]) + b1_ref[...]))
    h = jnp.sin(fr * (_dot(h, w2_ref[...]) + b2_ref[...]))
    h = _mm1(h.astype(jnp.bfloat16), w3_ref[...].astype(jnp.bfloat16)) + b3_ref[...]
    win = jnp.exp(-z[:, 0:1] * dl_ref[...])
    h = h * jnp.concatenate([win] * (h.shape[1] // win.shape[1]), axis=1)

    @pl.when(i == 0)
    def _():
        s_ref[...] = jnp.zeros_like(s_ref)

    s_ref[...] += jnp.sum(jnp.abs(h), axis=0, keepdims=True)
    row = lax.broadcasted_iota(jnp.int32, h.shape, 0) + i * h.shape[0]
    col = lax.broadcasted_iota(jnp.int32, h.shape, 1)
    neg = (col // GROUP_W) % 2 == 1
    h_ref[...] = jnp.where(neg & (row == 0), 0.0, h)


def hyena_filter_bank(L, w1, b1, w2, b2, w3, b3, freq):
    z = hyena_features(L)
    w1p = jnp.pad(w1, ((0, HY_EMB_PAD - HY_EMB), (0, 0)))
    max_decay = math.log(HY_TARGET) / HY_FAST_DECAY
    min_decay = math.log(HY_TARGET) / HY_SLOW_DECAY
    deltas = jnp.abs(jnp.linspace(min_decay, max_decay, GROUP_W, dtype=jnp.float32))[None, :]
    n = w3.shape[1]
    tr = min(L, HY_FILTER_ROWS)
    args = (z, w1p, b1[None, :], w2, b2[None, :], w3, b3[None, :], freq[None, :], deltas)
    return pl.pallas_call(
        _filter_kernel,
        grid=(L // tr,),
        in_specs=[pl.BlockSpec((tr, HY_EMB_PAD), lambda i: (i, 0))] + [_full(a) for a in args[1:]],
        out_specs=[pl.BlockSpec((tr, n), lambda i: (i, 0)), pl.BlockSpec((1, n), lambda i: (0, 0))],
        out_shape=[jax.ShapeDtypeStruct((L, n), jnp.float32), jax.ShapeDtypeStruct((1, n), jnp.float32)],
        compiler_params=_params(1),
    )(*args)


FFT_NB = 8
FFT_NA = 32
FFT_CB = 512
FFT_N1 = 128


def fft_tables(n1, n2):
    N = n1 * n2
    nk = -(-(n2 // 2 + 1) // FFT_NB) * FFT_NB
    a = jnp.arange(n1, dtype=jnp.float32)[:, None, None]
    k2 = jnp.arange(nk, dtype=jnp.float32)[None, :, None]
    b = jnp.arange(n2 // 2, dtype=jnp.float32)[None, None, :]
    ph = (jnp.mod(a * k2, float(N)) / N + jnp.mod(b * k2, float(n2)) / n2) * (-2.0 * math.pi)
    g = jnp.concatenate([jnp.cos(ph), jnp.sin(ph)], axis=1)
    kk = jnp.arange(nk)
    weight = jnp.where((kk == 0) | (kk == n2 // 2), 1.0, jnp.where(kk < n2 // 2, 2.0, 0.0))
    ginv = jnp.transpose(g, (0, 2, 1)) * (jnp.tile(weight, 2) / N)
    k1 = jnp.arange(n1, dtype=jnp.float32)[:, None]
    aa = jnp.arange(n1, dtype=jnp.float32)[None, :]
    f = jnp.mod(k1 * aa, float(n1)) * (-2.0 * math.pi / n1)
    fr, fi = jnp.cos(f), jnp.sin(f)
    ff = jnp.concatenate([jnp.concatenate([fr, -fi], axis=1), jnp.concatenate([fi, fr], axis=1)], axis=0)
    bf = lambda t: t.astype(jnp.bfloat16)
    return {"g": bf(g), "ginv": bf(ginv), "ff": bf(ff), "fft": bf(ff.T)}


def _stage_a_kernel(u_ref, g_ref, o_ref):
    for j in range(FFT_NA):
        o_ref[j] = _mm1(g_ref[j], u_ref[:, j, :].astype(jnp.bfloat16))


def fft_stage_a(u3, nb, col, ncol, g):
    n1 = u3.shape[1]
    rows = g.shape[1]
    gspec = pl.BlockSpec((FFT_NA, rows, nb), lambda c, i: (i, 0, 0))
    return pl.pallas_call(
        _stage_a_kernel,
        grid=(ncol, n1 // FFT_NA),
        in_specs=[pl.BlockSpec((nb, FFT_NA, CB), lambda c, i: (0, i, col + c)), gspec],
        out_specs=pl.BlockSpec((FFT_NA, rows, CB), lambda c, i: (i, 0, c)),
        out_shape=jax.ShapeDtypeStruct((n1, rows, ncol * CB), jnp.float32),
        compiler_params=_params(2),
    )(u3, g)


def _stage_b_filter_kernel(re_ref, im_ref, f_ref, o_ref):
    ff = f_ref[...]
    for j in range(FFT_NB):
        o_ref[j] = _mm1(ff, jnp.concatenate([re_ref[:, j, :], im_ref[:, j, :]], axis=0).astype(jnp.bfloat16))


def fft_stage_b_filter(t1, ff):
    n1, n2x2, C = t1.shape
    n2 = n2x2 // 2
    blk = lambda off: pl.BlockSpec((n1, FFT_NB, FFT_CB), lambda c, i: (0, i + off, c))
    mat = pl.BlockSpec((2 * n1, 2 * n1), lambda c, i: (0, 0))
    return pl.pallas_call(
        _stage_b_filter_kernel,
        grid=(C // FFT_CB, n2 // FFT_NB),
        in_specs=[blk(0), blk(n2 // FFT_NB), mat],
        out_specs=pl.BlockSpec((FFT_NB, 2 * n1, FFT_CB), lambda c, i: (i, 0, c)),
        out_shape=jax.ShapeDtypeStruct((n2, 2 * n1, C), jnp.float32),
        compiler_params=_params(2),
    )(t1, t1, ff)


def _filter_spectrum(p, q, s, n1):
    return (p[:n1] + q[:n1]) * s, (p[n1:] - q[n1:]) * s


def _stage_b_conv_kernel(re_ref, im_ref, p_ref, q_ref, s_ref, f_ref, ft_ref, ore_ref, oim_ref):
    n1 = re_ref.shape[0]
    s = s_ref[...]
    ff, fft_ = f_ref[...], ft_ref[...]
    bf = lambda t: t.astype(jnp.bfloat16)
    for j in range(FFT_NB):
        x = _mm1(ff, bf(jnp.concatenate([re_ref[:, j, :], im_ref[:, j, :]], axis=0)))
        hr, hi = _filter_spectrum(p_ref[j], q_ref[j], s, n1)
        xr, xi = x[:n1], x[n1:]
        z = _mm1(fft_, bf(jnp.concatenate([xr * hr - xi * hi, xr * hi + xi * hr], axis=0)))
        ore_ref[:, j, :] = z[:n1]
        oim_ref[:, j, :] = z[n1:]


def fft_stage_b_conv(t1, spec, col_p, col_q, inv_norm, ff, fft_):
    n1, n2x2, C = t1.shape
    n2 = n2x2 // 2
    ncb = C // FFT_CB
    blk = lambda off: pl.BlockSpec((n1, FFT_NB, FFT_CB), lambda c, i: (0, i + off, c))
    mat = pl.BlockSpec((2 * n1, 2 * n1), lambda c, i: (0, 0))
    sp = lambda col: pl.BlockSpec((FFT_NB, 2 * n1, FFT_CB), lambda c, i: (i, 0, col * ncb + c))
    return pl.pallas_call(
        _stage_b_conv_kernel,
        grid=(ncb, n2 // FFT_NB),
        in_specs=[blk(0), blk(n2 // FFT_NB), sp(col_p), sp(col_q), pl.BlockSpec((1, FFT_CB), lambda c, i: (0, c)),
                  mat, mat],
        out_specs=[blk(0), blk(0)],
        out_shape=[jax.ShapeDtypeStruct((n1, n2, C), jnp.float32)] * 2,
        compiler_params=_params(2),
    )(t1, t1, spec, spec, inv_norm, ff, fft_)


def _stage_a_inv_kernel(re_ref, im_ref, g_ref, u_ref, gate_ref, bias_ref, o_ref):
    for j in range(FFT_NA):
        y = _mm1(g_ref[j], jnp.concatenate([re_ref[j], im_ref[j]], axis=0).astype(jnp.bfloat16))
        o_ref[:, j, :] = gate_ref[:, j, :] * (y + u_ref[:, j, :] * bias_ref[...])


def fft_stage_a_inv(t2re, t2im, ginv, nb, u3, u_col, gate3, gate_col, bias):
    n1, nk, C = t2re.shape
    tb = pl.BlockSpec((FFT_NA, nk, CB), lambda c, i: (i, 0, c))
    gb = pl.BlockSpec((FFT_NA, nb, 2 * nk), lambda c, i: (i, 0, 0))
    ub = lambda col: pl.BlockSpec((nb, FFT_NA, CB), lambda c, i: (0, i, col + c))
    return pl.pallas_call(
        _stage_a_inv_kernel,
        grid=(C // CB, n1 // FFT_NA),
        in_specs=[tb, tb, gb, ub(u_col), ub(gate_col), pl.BlockSpec((1, CB), lambda c, i: (0, c))],
        out_specs=ub(0),
        out_shape=jax.ShapeDtypeStruct((nb, n1, C), jnp.float32),
        compiler_params=_params(2),
    )(t2re, t2im, ginv, u3, gate3, bias)


def hyena_long_conv_chain(u, L, biases, filt, colsum, tabs):
    C = GROUP_W
    n1 = FFT_N1
    n2 = 2 * L // n1
    nb = n2 // 2
    spec = fft_stage_b_filter(fft_stage_a(filt.reshape(nb, n1, filt.shape[1]), nb, 0, filt.shape[1] // CB, tabs["g"]),
                              tabs["ff"])
    s4 = colsum.reshape(HY_ORDER, 2, C)
    inv_norm = 1.0 / (s4[:, 0] + s4[:, 1])
    u3 = u.reshape(u.shape[0] // n1, n1, u.shape[1])
    y3, y_col = u3, 0
    for n in range(HY_ORDER):
        t1 = fft_stage_a(y3, nb, y_col, 1, tabs["g"])
        t2re, t2im = fft_stage_b_conv(t1, spec, 2 * n, 2 * n + 1, inv_norm[n][None, :], tabs["ff"], tabs["fft"])
        y3 = fft_stage_a_inv(t2re, t2im, tabs["ginv"], nb, y3, y_col, u3, n + 1, biases[n][None, :])
        y_col = 0
    return y3.reshape(L, C)


def dense_dft_tables(n):
    N = 2 * n
    k = jnp.arange(N, dtype=jnp.float32)[:, None]
    t = jnp.arange(n, dtype=jnp.float32)[None, :]
    ph = jnp.mod(k * t, float(N)) * (2.0 * math.pi / N)
    fd = jnp.concatenate([jnp.cos(ph), -jnp.sin(ph)], axis=0)
    return fd.astype(jnp.bfloat16), (fd.T / N).astype(jnp.bfloat16)


def _hyena_ctx_kernel(u_ref, h_ref, s_ref, bias_ref, fd_ref, ft_ref, o_ref):
    C = GROUP_W
    fd, ft = fd_ref[...], ft_ref[...]
    K = fd.shape[0] // 2
    bf = lambda t: t.astype(jnp.bfloat16)
    hs = _mm1(fd, bf(h_ref[...]))
    s = s_ref[...]
    y = u_ref[:, 0:C]
    for n in range(HY_ORDER):
        cp, cq = 2 * n * C, (2 * n + 1) * C
        inv = 1.0 / (s[:, cp:cp + C] + s[:, cq:cq + C])
        hr, hi = _filter_spectrum(hs[:, cp:cp + C], hs[:, cq:cq + C], inv, K)
        x = _mm1(fd, bf(y))
        xr, xi = x[:K], x[K:]
        conv = _mm1(ft, bf(jnp.concatenate([xr * hr - xi * hi, xr * hi + xi * hr], axis=0)))
        y = u_ref[:, (n + 1) * C:(n + 2) * C] * (conv + y * bias_ref[n:n + 1, :])
    o_ref[...] = y


def hyena_ctx(u, filt, colsum, biases, tabs):
    args = (u, filt, colsum, biases, tabs[0], tabs[1])
    return pl.pallas_call(
        _hyena_ctx_kernel,
        in_specs=[_full(a) for a in args],
        out_specs=pl.BlockSpec((u.shape[0], GROUP_W), lambda: (0, 0)),
        out_shape=jax.ShapeDtypeStruct((u.shape[0], GROUP_W), jnp.float32),
        compiler_params=pltpu.CompilerParams(vmem_limit_bytes=VMEM_LIMIT),
    )(*args)


SCAN_T = 64
GROUP_LANES = 256
RW_SUB = 64
HG_SUB = 16
LOG2_E = 1.4426950408889634


def _chunk_index(d, i, n_ctx, n_all):
    fwd = jnp.where(i < n_ctx, n_all - n_ctx + i, i - n_ctx)
    return jnp.where(d == 0, fwd, n_all - 1 - i)


def _stacking(T, S, nh, head, sign):
    G = nh * head
    nb = T // S
    n = nh * T
    rr = lax.broadcasted_iota(jnp.int32, (n, G), 0)
    same = ((rr // S) % nh) == (lax.broadcasted_iota(jnp.int32, (n, G), 1) // head)

    def bd(x):
        pieces = []
        for i in range(nb):
            pieces += [x[i * S:(i + 1) * S]] * nh
        return jnp.where(same, jnp.concatenate(pieces, axis=0), 0.0)

    def collapse(o):
        outs = []
        for i in range(nb):
            acc = o[i * nh * S:i * nh * S + S]
            for h in range(1, nh):
                acc = acc + o[i * nh * S + h * S:i * nh * S + (h + 1) * S]
            outs.append(acc)
        return jnp.concatenate(outs, axis=0)

    rt = lax.broadcasted_iota(jnp.int32, (n, n), 0)
    cs = lax.broadcasted_iota(jnp.int32, (n, n), 1)
    t_r = (rt // (nh * S)) * S + rt % S
    t_c = (cs // (nh * S)) * S + cs % S
    same_h = ((rt // S) % nh) == ((cs // S) % nh)
    before = same_h & ((t_r - t_c) * sign > 0)
    return bd, collapse, before, rt, cs, (t_r, t_c, same_h)


def _softplus(x):
    return jnp.maximum(x, 0.0) + jnp.log(1.0 + jnp.exp(-jnp.abs(x)))


def _rwkv_prep_kernel(*refs, n_lat, n_rows):
    zrefs, rest = refs[:15], refs[15:]
    (mu_ref, kk_w_ref, ka_ref, w0_ref, a0_ref, w2h_ref, w2l_ref, a2h_ref, a2l_ref, ones_ref,
     r_ref, k_ref, v_ref, g_ref, kk_ref, lw_ref, kd_ref, a_ref) = rest
    slabs = []
    for c in range(5):
        x = zrefs[3 * c][...]
        xp, xn = _neighbours(x, zrefs[3 * c + 1][...], zrefs[3 * c + 2][...], n_lat, n_rows)
        slabs.append(x + (0.5 * (xp + xn) - x) * mu_ref[:, c * CB:(c + 1) * CB])
    r, k, v, g, lora = slabs
    r_ref[...], k_ref[...], v_ref[...], g_ref[...] = r, k, v, g
    kk = k * kk_w_ref[...]
    ss = _mm2(kk * kk, ones_ref[...])
    kk_ref[...] = kk * lax.rsqrt(jnp.maximum(ss, 1e-24))
    lora_t = _split(jnp.tanh(lora))
    lora_s = _split(lora)
    for d in range(2):
        w_log = -_softplus(-(w0_ref[d:d + 1, :] + _mm3(lora_t, (w2h_ref[d], w2l_ref[d])))) - 0.5
        lw_ref[d] = -jnp.exp(w_log)
        a = _sigmoid(a0_ref[d:d + 1, :] + _mm3(lora_s, (a2h_ref[d], a2l_ref[d])))
        a_ref[d] = a
        kd_ref[d] = k * (1.0 + (a - 1.0) * ka_ref[...])


def rwkv_prep(z, mu, k_k, k_a, w0, a0, w2, a2, n_lat):
    n_rows = z.shape[0]
    tp = PREP_ROWS
    mu_p = jnp.pad(mu, (0, 5 * CB - RW_REAL))[None, :]
    w2p = jnp.zeros((2, CB, GROUP_W), jnp.float32)
    a2p = jnp.zeros((2, CB, GROUP_W), jnp.float32)
    for d in range(2):
        w2p = w2p.at[d, d * RW_LORA:(d + 1) * RW_LORA].set(w2[d])
        a2p = a2p.at[d, (2 + d) * RW_LORA:(3 + d) * RW_LORA].set(a2[d])
    w2s, a2s = _split(w2p), _split(a2p)
    small = (mu_p, k_k[None, :], k_a[None, :], w0, a0, w2s[0], w2s[1], a2s[0], a2s[1],
             _block_indicator(GROUP_W, RW_HEAD, 1.0))
    zspecs = []
    for c in range(5):
        zspecs += _halo_specs(tp, n_rows, COL_RW + c)
    one = pl.BlockSpec((tp, CB), lambda i: (i, 0))
    two = pl.BlockSpec((2, tp, CB), lambda i: (0, i, 0))
    s1 = jax.ShapeDtypeStruct((n_rows, GROUP_W), jnp.float32)
    s2 = jax.ShapeDtypeStruct((2, n_rows, GROUP_W), jnp.float32)
    return pl.pallas_call(
        functools.partial(_rwkv_prep_kernel, n_lat=n_lat, n_rows=n_rows),
        grid=(n_rows // tp,),
        in_specs=zspecs + [_full(a) for a in small],
        out_specs=[one] * 5 + [two] * 3,
        out_shape=[s1] * 5 + [s2] * 3,
        compiler_params=_params(1),
    )(*([z] * 15), *small)


def _rwkv_scan_body(refs, head):
    in_f, in_b, (of_ref, ob_ref, ht_ref) = refs[:6], refs[6:12], refs[12:]
    T, G, S = SCAN_T, GROUP_LANES, RW_SUB
    nh = G // head
    n = nh * T

    @pl.when(pl.program_id(0) == 0)
    def _():
        ht_ref[...] = jnp.zeros_like(ht_ref)

    ti = lax.broadcasted_iota(jnp.int32, (T, T), 0)
    si = lax.broadcasted_iota(jnp.int32, (T, T), 1)
    bf = lambda t: t.astype(jnp.bfloat16)
    each = lambda f, *seqs: [f(*xs) for xs in zip(*seqs)]
    n_grp = in_f[0].shape[1] // G
    lanes = [slice(grp * G, (grp + 1) * G) for grp in range(n_grp)]

    r, v, kk, lw, k, a, tri, before, incl, dest = ([] for _ in range(10))
    for (r_ref, v_ref, kk_ref, lw_ref, k_ref, a_ref), o_ref, sign in ((in_f, of_ref, 1), (in_b, ob_ref, -1)):
        bd, collapse, before_d, rt, cs, (t_r, t_c, same_h) = _stacking(T, S, nh, head, sign)
        for ls in lanes:
            r.append(r_ref[:, ls]), v.append(v_ref[:, ls]), kk.append(kk_ref[:, ls])
            lw.append(lw_ref[0, :, ls]), k.append(k_ref[0, :, ls]), a.append(a_ref[0, :, ls])
            tri.append(jnp.where((ti - si) * sign >= 0, 1.0, 0.0))
            before.append(before_d), incl.append(before_d | (rt == cs)), dest.append((o_ref, ls))
    eye = jnp.where(rt == cs, 1.0, 0.0)
    ht = [ht_ref[ch] for ch in range(len(r))]

    c = each(lambda tr, t: _dot(tr, t), tri, lw)
    ctot = each(lambda t: jnp.sum(t, axis=0, keepdims=True), lw)
    beta = each(lambda p, q: p * q, kk, a)
    einv = each(lambda t: jnp.exp(-t), c)
    lhs = each(lambda kk_, r_, c_, lw_: bf(jnp.concatenate([bd(-kk_ * jnp.exp(c_ - lw_)), bd(r_ * jnp.exp(c_))], axis=0)),
               kk, r, c, lw)
    rhs = each(lambda k_, b_, e_: bf(jnp.concatenate([bd(k_ * e_), bd(b_ * e_)], axis=0)), k, beta, einv)
    m = each(lambda p, q: _mm1(p, q, NT), lhs, rhs)
    g = each(lambda p, h: _mm1(p, bf(h), NT), lhs, ht)
    v_b = each(lambda t: bf(bd(t)), v)
    x = each(lambda g_, m_, v_, bm: g_[:n] + _mm1(bf(jnp.where(bm, m_[:n, :n], 0.0)), v_), g, m, v_b, before)
    lb = each(lambda m_, bm: jnp.where(bm, m_[:n, n:], 0.0), m, before)
    differ = jnp.where(same_h, t_r ^ t_c, 0)
    half = lambda s: (differ >= s) & (differ < 2 * s)
    dmat = each(lambda t: eye + jnp.where(half(1), t, 0.0), lb)
    s = 2
    while s < T:
        coupling = half(s)
        dmat = each(lambda d_, t: d_ + _mm1(bf(_mm1(bf(d_), bf(jnp.where(coupling, t, 0.0)))), bf(d_)), dmat, lb)
        s *= 2
    u_b = each(lambda d_, t: bf(_mm1(bf(d_), bf(t))), dmat, x)
    o = each(lambda g_, m_, v_, u_, im: (g_[n:] + _mm1(bf(jnp.where(im, m_[n:, :n], 0.0)), v_))
             + _mm1(bf(jnp.where(im, m_[n:, n:], 0.0)), u_), g, m, v_b, u_b, incl)
    efin = each(lambda ct, c_: jnp.exp(ct - c_), ctot, c)
    ht_new = each(lambda h, ct, v_, u_, k_, b_, e_: (h * jnp.exp(ct) + _mm1(v_, bf(bd(k_ * e_)), TN))
                  + _mm1(u_, bf(bd(b_ * e_)), TN), ht, ctot, v_b, u_b, k, beta, efin)
    for ch, (o_ref, ls) in enumerate(dest):
        o_ref[:, ls] = collapse(o[ch])
        ht_ref[ch] = ht_new[ch]


def rope_tables(n_ctx, L):
    d_axis = RT_HEAD // 2
    half = d_axis // 2
    inv = ROPE_BASE ** (-jnp.arange(0, d_axis, 2, dtype=jnp.float32) / d_axis)
    t = jnp.arange(L)
    pos = jnp.stack([(t // GRID_W).astype(jnp.float32), (t % GRID_W).astype(jnp.float32)], axis=1)
    j = jnp.arange(RT_HEAD)
    ang = pos[:, j // d_axis] * inv[j % half][None, :]
    sgn = jnp.where((j % d_axis) < half, -1.0, 1.0)[None, :]
    cos = jnp.concatenate([jnp.cos(ang), jnp.ones((n_ctx, RT_HEAD), jnp.float32)], axis=0)
    sin = jnp.concatenate([jnp.sin(ang) * sgn, jnp.zeros((n_ctx, RT_HEAD), jnp.float32)], axis=0)
    return jnp.tile(cos, (1, RT_HEADS)), jnp.tile(sin, (1, RT_HEADS))


def _rotate(x, cos, sin):
    G = x.shape[1]
    half = RT_HEAD // 4
    lane = lax.broadcasted_iota(jnp.int32, x.shape, 1)
    partner = jnp.where((lane % (2 * half)) < half, pltpu.roll(x, G - half, axis=1), pltpu.roll(x, half, axis=1))
    return x * cos + partner * sin


def _retention_scan_body(q_ref, k_ref, v_ref, cos_ref, sin_ref, lg_ref, o_ref, st_ref, *, head, d):
    T, G = SCAN_T, GROUP_LANES
    nh = G // head
    sign = 1 - 2 * d

    t = lax.broadcasted_iota(jnp.int32, (T, 1), 0)
    pos = (t + d * (T - 1 - 2 * t) + 1).astype(jnp.float32)
    bd, collapse, before, rt, cs, _ = _stacking(T, T, nh, head, sign)
    incl = before | (rt == cs)
    bf = lambda x: x.astype(jnp.bfloat16)

    groups = range(q_ref.shape[1] // G)
    lanes = [slice(grp * G, (grp + 1) * G) for grp in groups]
    each = lambda f, *seqs: [f(*xs) for xs in zip(*seqs)]
    st = [st_ref[grp] for grp in groups]
    cos, sin, lg = ([ref[:, ls] for ls in lanes] for ref in (cos_ref, sin_ref, lg_ref))
    q = each(_rotate, [q_ref[:, ls] for ls in lanes], cos, sin)
    k = each(lambda x, c_, s_: _rotate(x, c_, s_) * (head ** -0.5), [k_ref[:, ls] for ls in lanes], cos, sin)
    c = each(lambda lg_: pos * lg_, lg)
    q_b = each(lambda q_, c_: bf(bd(q_ * jnp.exp(c_))), q, c)
    kt_b = each(lambda k_, c_: bf(bd(k_ * jnp.exp(-c_))), k, c)
    v_b = [bf(bd(v_ref[:, ls])) for ls in lanes]
    scores = each(lambda q_, k_: bf(jnp.where(incl, _mm1(q_, k_, NT), 0.0)), q_b, kt_b)
    inter = each(lambda q_, s_: _mm1(q_, bf(s_), NT), q_b, st)
    o = each(lambda g_, s_, v_: g_ + _mm1(s_, v_), inter, scores, v_b)
    st_new = each(lambda s_, lg_, v_, k_, c_: s_ * jnp.exp(float(T) * lg_)
                  + _mm1(v_, bf(bd(k_ * jnp.exp(float(T) * lg_ - c_))), TN), st, lg, v_b, k, c)
    for grp in groups:
        o_ref[:, lanes[grp]] = collapse(o[grp])
        st_ref[grp] = st_new[grp]


def _gla_scan_body(q_ref, f_ref, i_ref, lb_ref, o_ref, st_ref, *, head, d):
    T, S = SCAN_T, HG_SUB
    W = q_ref.shape[-1]
    nh = W // head
    sign = 1 - 2 * d

    ti = lax.broadcasted_iota(jnp.int32, (S, S), 0)
    si = lax.broadcasted_iota(jnp.int32, (S, S), 1)
    tri_incl = jnp.where((ti - si) * sign >= 0, 1.0, 0.0)
    row = lax.broadcasted_iota(jnp.int32, (S, 1), 0)
    lb = lb_ref[0]

    states = [st_ref[h] for h in range(nh)]
    for j in range(T // S):
        jb = j + d * (T // S - 1 - 2 * j)
        rows = pl.ds(jb * S, S)
        q = q_ref[rows, :]
        q = q * _sigmoid(q)
        v = i_ref[rows, :]
        gate = lb + (1.0 - lb) * _sigmoid(f_ref[rows, :])
        lf = jnp.log(jnp.maximum(gate, HG_MIN_GATE))
        k = 1.0 - gate
        b = _dot(tri_incl, lf)
        btot = jnp.sum(lf, axis=0, keepdims=True)
        qe = q * jnp.exp(b)
        ke = k * jnp.exp(btot - b)
        outs = []
        for h in range(nh):
            ls = slice(h * head, (h + 1) * head)
            qh, kh, vh = q[:, ls], k[:, ls], v[:, ls]
            bh = b[:, ls] * LOG2_E
            st = states[h]
            o = _mm1(qe[:, ls].astype(jnp.bfloat16), st.astype(jnp.bfloat16), NT)
            for s in range(S):
                e = jnp.exp2(bh - bh[s:s + 1, :])
                a_s = jnp.sum(qh * kh[s:s + 1, :] * e, axis=-1, keepdims=True)
                a_s = jnp.where((row - s) * sign >= 0, a_s, 0.0)
                o = o + a_s * vh[s:s + 1, :]
            outs.append(o)
            states[h] = st * jnp.exp(btot[:, ls]) + _mm1(vh.astype(jnp.bfloat16), ke[:, ls].astype(jnp.bfloat16), TN)
        o_ref[rows, :] = jnp.concatenate(outs, axis=1)
    for h in range(nh):
        st_ref[h] = states[h]


N_RW, N_RT, N_HG = 6, 6, 4


def _scans_kernel(*refs):
    n_in = 2 * (N_RW + N_RT + N_HG)
    ins, (rw_of, rw_ob, rt_of, rt_ob, hg_of, hg_ob, rw_st, rt_st, hg_st) = refs[:n_in], refs[n_in:]
    rw_in, rt_in, hg_in = ins[:2 * N_RW], ins[2 * N_RW:2 * (N_RW + N_RT)], ins[2 * (N_RW + N_RT):]

    @pl.when(pl.program_id(0) == 0)
    def _():
        rt_st[...] = jnp.zeros_like(rt_st)
        hg_st[...] = jnp.zeros_like(hg_st)

    n_rt, n_hg = rt_st.shape[0] // 2, hg_st.shape[0] // 2
    for d, (rt_o, hg_o) in enumerate(((rt_of, hg_of), (rt_ob, hg_ob))):
        _gla_scan_body(*hg_in[d * N_HG:(d + 1) * N_HG], hg_o, hg_st.at[d * n_hg:(d + 1) * n_hg], head=HG_HEAD, d=d)
        _retention_scan_body(*rt_in[d * N_RT:(d + 1) * N_RT], rt_o, rt_st.at[d * n_rt:(d + 1) * n_rt],
                             head=RT_HEAD, d=d)
    _rwkv_scan_body(list(rw_in) + [rw_of, rw_ob, rw_st], RW_HEAD)


def recurrent_scans(z, rw, cos, sin, lg, lb2, n_ctx_rows):
    N = z.shape[0]
    W = GROUP_W
    T = SCAN_T
    n_all, n_ctx = N // T, n_ctx_rows // T
    chunk = lambda d: (lambda i: (_chunk_index(d, i, n_ctx, n_all), 0))
    row = lambda d: pl.BlockSpec((T, W), chunk(d))
    per_dir = lambda d: pl.BlockSpec((1, T, W), lambda i: (d, _chunk_index(d, i, n_ctx, n_all), 0))
    zc = lambda d, col: pl.BlockSpec((T, CB), lambda i: (_chunk_index(d, i, n_ctx, n_all), col))
    rw_specs = lambda d: [row(d)] * 3 + [per_dir(d)] * 3
    rt_specs = lambda d: [zc(d, COL_RT), zc(d, COL_RT + 1), zc(d, COL_RT + 2), row(d), row(d),
                          pl.BlockSpec((1, W), lambda i: (0, 0))]
    hg_specs = lambda d: [zc(d, COL_HG), zc(d, COL_HG + 1 + d), zc(d, COL_HG + 3),
                          pl.BlockSpec((1, 1, W), lambda i: (d, 0, 0))]
    shape = jax.ShapeDtypeStruct((N, W), jnp.float32)
    return pl.pallas_call(
        _scans_kernel,
        grid=(n_all,),
        in_specs=rw_specs(0) + rw_specs(1) + rt_specs(0) + rt_specs(1) + hg_specs(0) + hg_specs(1),
        out_specs=[row(0), row(1)] * 3,
        out_shape=[shape] * 6,
        scratch_shapes=[pltpu.VMEM((2 * (W // GROUP_LANES), GROUP_LANES, GROUP_LANES), jnp.float32),
                        pltpu.VMEM((2 * (W // GROUP_LANES), GROUP_LANES, GROUP_LANES), jnp.float32),
                        pltpu.VMEM((2 * (W // HG_HEAD), HG_HEAD, HG_HEAD), jnp.float32)],
        compiler_params=_params(1),
    )(*(list(rw) * 2), *([z, z, z, cos, sin, lg] * 2), *([z, z, z, lb2] * 2))


def _outproj_kernel(hy_ref, rwof_ref, rwob_ref, r_ref, k_ref, v_ref, rwg_ref, rtof_ref, rtob_ref, rtg_ref,
                    hgof_ref, hgob_ref, hgg_ref,
                    rk_ref, gng_ref, gnb_ref, hgn_ref, avg64_ref, avg128_ref,
                    w_ref, x_ref, gate_ref, g_ref, b_ref, o_ref, *, n_lat):
    avg64, avg128 = avg64_ref[...], avg128_ref[...]

    def head_norm(o, avg, eps, centre):
        if centre:
            o = o - _mm2(o, avg)
        return o * lax.rsqrt(_mm2(o * o, avg) + eps)

    silu = lambda t: t * _sigmoid(t)
    y_rw = head_norm(rwof_ref[...] + rwob_ref[...], avg64, RW_GN_EPS, True) * gng_ref[...] + gnb_ref[...]
    bonus = (float(RW_HEAD) * _mm2(r_ref[...] * k_ref[...] * rk_ref[...], avg64)) * v_ref[...]
    y_rw = (y_rw + bonus) * _sigmoid(rwg_ref[...])
    y_rt = head_norm(rtof_ref[...] + rtob_ref[...], avg64, HEAD_NORM_EPS, True) * silu(rtg_ref[...])
    y_hg = head_norm(hgof_ref[...] + hgob_ref[...], avg128, HEAD_NORM_EPS, False) * hgn_ref[...] * silu(hgg_ref[...])
    y = None
    for m, ym in enumerate((hy_ref[...], y_rw, y_rt, y_hg)):
        part = jnp.dot(ym.astype(jnp.bfloat16), w_ref[m * GROUP_W:(m + 1) * GROUP_W, :],
                       preferred_element_type=jnp.float32)
        y = part if y is None else y + part
    is_ctx = _ctx_rows(x_ref.shape[0], n_lat)
    r = ALPHA * x_ref[...] + _mod_row(gate_ref, is_ctx) * y
    o_ref[...] = _ln_rows(r) * g_ref[...] + b_ref[...]


def outproj_deepnorm(y_hy, scans, rw_r, rw_k, rw_v, rw_g, z, r_k, gn_g, gn_b, hg_norm_g,
                     w_bf16, x, mod, g, b, n_lat, m):
    tm = OUT_TM
    one = pl.BlockSpec((tm, GROUP_W), lambda i: (i, 0))
    zc = lambda col: pl.BlockSpec((tm, CB), lambda i: (i, col))
    row = pl.BlockSpec((tm, D_MODEL), lambda i: (i, 0))
    small = (r_k, gn_g, gn_b, hg_norm_g, _block_indicator(GROUP_W, RW_HEAD, 1.0 / RW_HEAD),
             _block_indicator(GROUP_W, HG_HEAD, 1.0 / HG_HEAD))
    return pl.pallas_call(
        functools.partial(_outproj_kernel, n_lat=n_lat),
        grid=(m // tm,),
        in_specs=[one] * 9 + [zc(COL_RT + 3), one, one, zc(COL_HG + 4)]
        + [_full(a) for a in small]
        + [_full(w_bf16), row, pl.BlockSpec((8, D_MODEL), lambda i: (0, 2)), _full(g), _full(b)],
        out_specs=row,
        out_shape=jax.ShapeDtypeStruct((m, D_MODEL), jnp.float32),
        compiler_params=_params(1),
    )(y_hy, scans[0], scans[1], rw_r, rw_k, rw_v, rw_g, scans[2], scans[3], z, scans[4], scans[5], z,
      *small, w_bf16, x, mod, g, b)


def _ffn_kernel(x_ref, sh_ref, sc_ref, w1_ref, w3_ref, w2_ref, gate_ref, g_ref, b_ref, o_ref, h_ref, acc_ref, *, n_lat):
    j = pl.program_id(1)

    @pl.when(j == 0)
    def _():
        is_ctx = _ctx_rows(x_ref.shape[0], n_lat)
        h = _ln_rows(x_ref[...]) * (1.0 + _mod_row(sc_ref, is_ctx)) + _mod_row(sh_ref, is_ctx)
        h_ref[...] = h.astype(jnp.bfloat16)
        acc_ref[...] = jnp.zeros_like(acc_ref)

    h = h_ref[...]
    a = jnp.dot(h, w1_ref[...], preferred_element_type=jnp.float32)
    u = jnp.dot(h, w3_ref[...], preferred_element_type=jnp.float32)
    s = (a * _sigmoid(a) * u).astype(jnp.bfloat16)
    acc_ref[...] += jnp.dot(s, w2_ref[...], preferred_element_type=jnp.float32)

    @pl.when(j == pl.num_programs(1) - 1)
    def _():
        is_ctx = _ctx_rows(x_ref.shape[0], n_lat)
        r = ALPHA * x_ref[...] + _mod_row(gate_ref, is_ctx) * acc_ref[...]
        o_ref[...] = _ln_rows(r) * g_ref[...] + b_ref[...]


def ffn_deepnorm(x, mod, w1, w3, w2, g, b, n_lat, tm):
    m = x.shape[0]
    row = pl.BlockSpec((tm, D_MODEL), lambda i, j: (i, 0))
    modc = lambda c: pl.BlockSpec((8, D_MODEL), lambda i, j: (0, c))
    vec = pl.BlockSpec((1, D_MODEL), lambda i, j: (0, 0))
    return pl.pallas_call(
        functools.partial(_ffn_kernel, n_lat=n_lat),
        grid=(m // tm, FFN_HIDDEN // FFN_TF),
        in_specs=[row, modc(3), modc(4),
                  pl.BlockSpec((D_MODEL, FFN_TF), lambda i, j: (0, j)),
                  pl.BlockSpec((D_MODEL, FFN_TF), lambda i, j: (0, j)),
                  pl.BlockSpec((FFN_TF, D_MODEL), lambda i, j: (j, 0)),
                  modc(5), vec, vec],
        out_specs=row,
        out_shape=jax.ShapeDtypeStruct((m, D_MODEL), jnp.float32),
        scratch_shapes=[pltpu.VMEM((tm, D_MODEL), jnp.bfloat16),
                        pltpu.VMEM((tm, D_MODEL), jnp.float32)],
        compiler_params=_params(2),
    )(x, mod, mod, w1, w3, w2, mod, g, b)


def kernel(x, c, ctx, c_ctx, ada_w, ada_b, w_in, w_out, ln_g, ln_b, hy_conv_w, hy_conv_b, hy_w1, hy_b1, hy_w2, hy_b2, hy_w3, hy_b3, hy_freq, hy_bias, rw_mu, rw_w0, rw_w2, rw_a0, rw_a2, rw_k_k, rw_k_a, rw_r_k, rw_gn_g, rw_gn_b, hg_lb_raw, hg_norm_g, ffn_w1, ffn_w3, ffn_w2):
    L, n_ctx = x.shape[1], ctx.shape[1]
    sm = jax.nn.softmax(hg_lb_raw.astype(jnp.float32), axis=1)
    lower_bounds = jnp.cumsum(sm, axis=1) - sm[:, :1]
    log_gamma = jnp.log1p(-jnp.exp2(-5.0 - jnp.arange(RT_HEADS, dtype=jnp.float32)))
    lg = jnp.repeat(log_gamma, RT_HEAD)[None, :]
    cos, sin = rope_tables(n_ctx, L)
    fft_tabs = fft_tables(FFT_N1, 2 * L // FFT_N1)
    ctx_tabs = dense_dft_tables(n_ctx)

    c8 = jnp.zeros((8, D_MODEL), jnp.float32).at[0].set(c[0]).at[1].set(c_ctx)
    xs = jnp.concatenate([x[0], ctx[0]], axis=0)
    n_rows = L + n_ctx
    for l in range(DEPTH):
        with_ctx = l < DEPTH - 1
        mod = ada_modulation(c8, ada_w, l, ada_b[l][None, :])
        w_out_b = w_out[l].astype(jnp.bfloat16)
        w1_b, w3_b, w2_b = (w[l].astype(jnp.bfloat16) for w in (ffn_w1, ffn_w3, ffn_w2))

        z = modulated_projection(xs, mod, w_in, l, L)

        u = hyena_conv3(z, hy_conv_w[l], hy_conv_b[l][None, :], L)
        hy_w = (hy_w1[l], hy_b1[l], hy_w2[l], hy_b2[l], hy_w3[l], hy_b3[l], hy_freq[l])
        filt, colsum = hyena_filter_bank(L, *hy_w)
        y_lat = hyena_long_conv_chain(u, L, hy_bias[l], filt, colsum, fft_tabs)
        if with_ctx:
            filt_c, colsum_c = hyena_filter_bank(n_ctx, *hy_w)
            y_ctx = hyena_ctx(u[L:], filt_c, colsum_c, hy_bias[l], ctx_tabs)
        else:
            y_ctx = jnp.zeros((n_ctx, GROUP_W), jnp.float32)
        y_hy = jnp.concatenate([y_lat, y_ctx], axis=0)

        rw_r, rw_k, rw_v, rw_g, rw_kk, rw_lw, rw_kd, rw_a = rwkv_prep(
            z, rw_mu[l], rw_k_k[l], rw_k_a[l], rw_w0[l], rw_a0[l], rw_w2[l], rw_a2[l], L)
        scans = recurrent_scans(z, (rw_r, rw_v, rw_kk, rw_lw, rw_kd, rw_a), cos, sin, lg,
                                lower_bounds[:, l][:, None, :], n_ctx)

        xs = outproj_deepnorm(y_hy, scans, rw_r, rw_k, rw_v, rw_g, z,
                              rw_r_k[l].reshape(1, GROUP_W), rw_gn_g[l][None, :], rw_gn_b[l][None, :],
                              hg_norm_g[l][None, :], w_out_b, xs, mod, ln_g[l, 0][None, :], ln_b[l, 0][None, :],
                              L, n_rows if with_ctx else L)
        xs = ffn_deepnorm(xs, mod, w1_b, w3_b, w2_b, ln_g[l, 1][None, :], ln_b[l, 1][None, :], L,
                          FFN_TM if with_ctx else FFN_LAST_TM)
    return xs[None]
```
